```python
import math
import jax, jax.numpy as jnp
from jax import lax
import numpy as np

D_MODEL = 1024
BATCH = 8
SEQ = 8192
DEPTH = 1

D_FF = 2816
SSM_WIDTH = 512
SSM_GROUP = 16
SSM_GROUPS = SSM_WIDTH // SSM_GROUP
SSM_STATE = 64
DT_MIN = 1e-3
DT_MAX = 1e-1
ATT_HEADS = 8
ATT_HEAD_DIM = 64
ATT_WIDTH = ATT_HEADS * ATT_HEAD_DIM
GRID_W = 64
WIN_H = 8
WIN_W = 16
IN_COLS = SSM_WIDTH + 3 * ATT_WIDTH + 2 * D_MODEL
SPLITS = (SSM_WIDTH,
          SSM_WIDTH + ATT_WIDTH,
          SSM_WIDTH + 2 * ATT_WIDTH,
          SSM_WIDTH + 3 * ATT_WIDTH,
          SSM_WIDTH + 3 * ATT_WIDTH + D_MODEL)
EPS = 1e-6
NEG_INF = -1e30

kernel_name = "hybrid_s5_natten_macaron_encoder"


def rms_norm(x, gain):
    xf = x.astype(jnp.float32)
    inv = lax.rsqrt(jnp.mean(xf * xf, axis=-1, keepdims=True) + EPS)
    return (xf * inv * gain.astype(jnp.float32)).astype(x.dtype)


def swiglu(x, w_gate, w_up, w_down):
    return (jax.nn.silu(x @ w_gate) * (x @ w_up)) @ w_down


def _complex_scan_combine(left, right):
    a1r, a1i, b1r, b1i = left
    a2r, a2i, b2r, b2i = right
    return (a1r * a2r - a1i * a2i,
            a1r * a2i + a1i * a2r,
            a2r * b1r - a2i * b1i + b2r,
            a2r * b1i + a2i * b1r + b2i)


def s5_direction(u, a_re, a_im, log_dt, b_re, b_im, c_re, c_im, reverse):
    f32 = jnp.float32
    dt = jnp.exp(log_dt.astype(f32))[:, None]
    lam_re = a_re.astype(f32)
    lam_im = a_im.astype(f32)
    zr, zi = lam_re * dt, lam_im * dt
    mag = jnp.exp(zr)
    lb_re, lb_im = mag * jnp.cos(zi), mag * jnp.sin(zi)
    den = lam_re * lam_re + lam_im * lam_im
    nr, ni = lb_re - 1.0, lb_im
    f_re = (nr * lam_re + ni * lam_im) / den
    f_im = (ni * lam_re - nr * lam_im) / den
    br, bi = b_re.astype(f32), b_im.astype(f32)
    bb_re = f_re[..., None] * br - f_im[..., None] * bi
    bb_im = f_re[..., None] * bi + f_im[..., None] * br
    uf = u.astype(f32)
    bu_re = jnp.einsum('blgc,gpc->blgp', uf, bb_re)
    bu_im = jnp.einsum('blgc,gpc->blgp', uf, bb_im)
    seq_len = u.shape[1]
    a_shape = (1, seq_len) + lb_re.shape
    a_seq_re = jnp.broadcast_to(lb_re, a_shape)
    a_seq_im = jnp.broadcast_to(lb_im, a_shape)
    _, _, s_re, s_im = lax.associative_scan(
        _complex_scan_combine, (a_seq_re, a_seq_im, bu_re, bu_im),
        reverse=reverse, axis=1)
    return (jnp.einsum('blgp,gcp->blgc', s_re, c_re.astype(f32))
            - jnp.einsum('blgp,gcp->blgc', s_im, c_im.astype(f32)))


def neighbourhood_attention_2d(q, k, v, rpb):
    bsz, seq_len, _ = q.shape
    rows = seq_len // GRID_W
    kh = min(WIN_H, rows)
    grid = (bsz, rows, GRID_W, ATT_HEADS, ATT_HEAD_DIM)
    qg = q.reshape(grid) * (ATT_HEAD_DIM ** -0.5)
    kg = k.reshape(grid)
    vg = v.reshape(grid)
    r = jnp.arange(rows)
    row_start = jnp.clip(r - kh // 2, 0, rows - kh)
    row_idx = row_start[:, None] + jnp.arange(kh)[None, :]
    k_band = kg[:, row_idx]
    v_band = vg[:, row_idx]
    col = jnp.arange(GRID_W)
    col_start = jnp.clip(col - WIN_W // 2, 0, GRID_W - WIN_W)
    col_mask = ((col[None, :] >= col_start[:, None])
                & (col[None, :] < col_start[:, None] + WIN_W))
    scores = jnp.einsum('brqhd,brkchd->bhrqkc', qg, k_band).astype(jnp.float32)
    dr = row_idx - r[:, None] + (WIN_H - 1)
    dc = jnp.clip(col[None, :] - col[:, None], -(WIN_W - 1), WIN_W - 1) + (WIN_W - 1)
    bias = rpb.astype(jnp.float32)[:, dr[:, None, :, None], dc[None, :, None, :]]
    scores = jnp.where(col_mask[:, None, :], scores + bias[None], NEG_INF)
    probs = jax.nn.softmax(scores, axis=(-2, -1)).astype(v.dtype)
    out = jnp.einsum('bhrqkc,brkchd->brqhd', probs, v_band)
    return out.reshape(bsz, seq_len, ATT_WIDTH)


def _fwd_setup_inputs(seed: int = 0) -> dict:
    key = jax.random.key(seed)
    ks = iter(jax.random.split(key, 40))
    f32 = jnp.float32

    def nrm(shape, scale):
        return jax.random.normal(next(ks), shape, f32) * scale

    def gain(shape):
        return 1.0 + nrm(shape, 0.01)

    L_ = DEPTH
    G, P, C = SSM_GROUPS, SSM_STATE, SSM_GROUP
    a_im_init = jnp.broadcast_to(jnp.pi * jnp.arange(P, dtype=f32), (L_, G, P))

    def log_dt():
        return jax.random.uniform(next(ks), (L_, G), f32,
                                  minval=math.log(DT_MIN), maxval=math.log(DT_MAX))

    inp = {}
    inp["x"] = nrm((BATCH, SEQ, D_MODEL), 1.0)
    inp["ffn1_norm"] = gain((L_, D_MODEL))
    inp["ffn1_w_gate"] = nrm((L_, D_MODEL, D_FF), D_MODEL ** -0.5)
    inp["ffn1_w_up"] = nrm((L_, D_MODEL, D_FF), D_MODEL ** -0.5)
    inp["ffn1_w_down"] = nrm((L_, D_FF, D_MODEL), D_FF ** -0.5)
    inp["mix_norm"] = gain((L_, D_MODEL))
    inp["w_in"] = nrm((L_, D_MODEL, IN_COLS), D_MODEL ** -0.5)
    inp["ssm_a_re_fwd"] = -0.5 + nrm((L_, G, P), 0.01)
    inp["ssm_a_im_fwd"] = a_im_init + nrm((L_, G, P), 0.01)
    inp["ssm_log_dt_fwd"] = log_dt()
    inp["ssm_b_re_fwd"] = nrm((L_, G, P, C), (2.0 * C) ** -0.5)
    inp["ssm_b_im_fwd"] = nrm((L_, G, P, C), (2.0 * C) ** -0.5)
    inp["ssm_c_re_fwd"] = nrm((L_, G, C, P), P ** -0.5)
    inp["ssm_c_im_fwd"] = nrm((L_, G, C, P), P ** -0.5)
    inp["ssm_a_re_bwd"] = -0.5 + nrm((L_, G, P), 0.01)
    inp["ssm_a_im_bwd"] = a_im_init + nrm((L_, G, P), 0.01)
    inp["ssm_log_dt_bwd"] = log_dt()
    inp["ssm_b_re_bwd"] = nrm((L_, G, P, C), (2.0 * C) ** -0.5)
    inp["ssm_b_im_bwd"] = nrm((L_, G, P, C), (2.0 * C) ** -0.5)
    inp["ssm_c_re_bwd"] = nrm((L_, G, C, P), P ** -0.5)
    inp["ssm_c_im_bwd"] = nrm((L_, G, C, P), P ** -0.5)
    inp["ssm_d"] = nrm((L_, SSM_WIDTH), 1.0)
    inp["ssm_w_glu"] = nrm((L_, SSM_WIDTH, SSM_WIDTH), SSM_WIDTH ** -0.5)
    inp["ssm_b_glu"] = nrm((L_, SSM_WIDTH), 0.02)
    inp["att_rpb"] = nrm((L_, ATT_HEADS, 2 * WIN_H - 1, 2 * WIN_W - 1), 0.02)
    inp["w_branch_ssm"] = nrm((L_, SSM_WIDTH, D_MODEL), SSM_WIDTH ** -0.5)
    inp["w_branch_att"] = nrm((L_, ATT_WIDTH, D_MODEL), ATT_WIDTH ** -0.5)
    inp["w_out"] = nrm((L_, D_MODEL, D_MODEL), D_MODEL ** -0.5)
    inp["ffn2_norm"] = gain((L_, D_MODEL))
    inp["ffn2_w_gate"] = nrm((L_, D_MODEL, D_FF), D_MODEL ** -0.5)
    inp["ffn2_w_up"] = nrm((L_, D_MODEL, D_FF), D_MODEL ** -0.5)
    inp["ffn2_w_down"] = nrm((L_, D_FF, D_MODEL), D_FF ** -0.5)
    inp["final_norm"] = gain((D_MODEL,))
    return inp


def _fwd_reference(x, ffn1_norm, ffn1_w_gate, ffn1_w_up, ffn1_w_down, mix_norm, w_in,
              ssm_a_re_fwd, ssm_a_im_fwd, ssm_log_dt_fwd, ssm_b_re_fwd, ssm_b_im_fwd,
              ssm_c_re_fwd, ssm_c_im_fwd,
              ssm_a_re_bwd, ssm_a_im_bwd, ssm_log_dt_bwd, ssm_b_re_bwd, ssm_b_im_bwd,
              ssm_c_re_bwd, ssm_c_im_bwd,
              ssm_d, ssm_w_glu, ssm_b_glu, att_rpb, w_branch_ssm, w_branch_att, w_out,
              ffn2_norm, ffn2_w_gate, ffn2_w_up, ffn2_w_down, final_norm):
    bsz, seq_len, _ = x.shape
    h = x
    for layer in range(DEPTH):
        h = h + 0.5 * swiglu(rms_norm(h, ffn1_norm[layer]), ffn1_w_gate[layer],
                             ffn1_w_up[layer], ffn1_w_down[layer])
        u = rms_norm(h, mix_norm[layer])
        z = u @ w_in[layer]
        z_ssm, z_q, z_k, z_v, g_ssm, g_att = jnp.split(z, SPLITS, axis=-1)

        us = z_ssm.reshape(bsz, seq_len, SSM_GROUPS, SSM_GROUP)
        y_fwd = s5_direction(us, ssm_a_re_fwd[layer], ssm_a_im_fwd[layer], ssm_log_dt_fwd[layer],
                             ssm_b_re_fwd[layer], ssm_b_im_fwd[layer],
                             ssm_c_re_fwd[layer], ssm_c_im_fwd[layer], False)
        y_bwd = s5_direction(us, ssm_a_re_bwd[layer], ssm_a_im_bwd[layer], ssm_log_dt_bwd[layer],
                             ssm_b_re_bwd[layer], ssm_b_im_bwd[layer],
                             ssm_c_re_bwd[layer], ssm_c_im_bwd[layer], True)
        y_s = (y_fwd + y_bwd).reshape(bsz, seq_len, SSM_WIDTH) \
            + ssm_d[layer].astype(jnp.float32) * z_ssm.astype(jnp.float32)
        y_s = jax.nn.gelu(y_s.astype(x.dtype))
        y_s = y_s * jax.nn.sigmoid(y_s @ ssm_w_glu[layer] + ssm_b_glu[layer])
        branch_ssm = y_s @ w_branch_ssm[layer]

        y_a = neighbourhood_attention_2d(z_q, z_k, z_v, att_rpb[layer])
        branch_att = y_a @ w_branch_att[layer]

        merged = jax.nn.sigmoid(g_ssm) * branch_ssm + jax.nn.sigmoid(g_att) * branch_att
        h = h + merged @ w_out[layer]

        h = h + 0.5 * swiglu(rms_norm(h, ffn2_norm[layer]), ffn2_w_gate[layer],
                             ffn2_w_up[layer], ffn2_w_down[layer])
    return rms_norm(h, final_norm)


import jax as _jax
import jax.numpy as _jnp

TWIN_FORMAT = 'train_step'
FWD_PARAMS = ['x', 'ffn1_norm', 'ffn1_w_gate', 'ffn1_w_up', 'ffn1_w_down', 'mix_norm', 'w_in', 'ssm_a_re_fwd', 'ssm_a_im_fwd', 'ssm_log_dt_fwd', 'ssm_b_re_fwd', 'ssm_b_im_fwd', 'ssm_c_re_fwd', 'ssm_c_im_fwd', 'ssm_a_re_bwd', 'ssm_a_im_bwd', 'ssm_log_dt_bwd', 'ssm_b_re_bwd', 'ssm_b_im_bwd', 'ssm_c_re_bwd', 'ssm_c_im_bwd', 'ssm_d', 'ssm_w_glu', 'ssm_b_glu', 'att_rpb', 'w_branch_ssm', 'w_branch_att', 'w_out', 'ffn2_norm', 'ffn2_w_gate', 'ffn2_w_up', 'ffn2_w_down', 'final_norm']
TWIN_WEIGHTS = ['ffn1_norm', 'ffn1_w_gate', 'ffn1_w_up', 'ffn1_w_down', 'mix_norm', 'w_in', 'ssm_a_re_fwd', 'ssm_a_im_fwd', 'ssm_log_dt_fwd', 'ssm_b_re_fwd', 'ssm_b_im_fwd', 'ssm_c_re_fwd', 'ssm_c_im_fwd', 'ssm_a_re_bwd', 'ssm_a_im_bwd', 'ssm_log_dt_bwd', 'ssm_b_re_bwd', 'ssm_b_im_bwd', 'ssm_c_re_bwd', 'ssm_c_im_bwd', 'ssm_d', 'ssm_w_glu', 'ssm_b_glu', 'att_rpb', 'w_branch_ssm', 'w_branch_att', 'w_out', 'ffn2_norm', 'ffn2_w_gate', 'ffn2_w_up', 'ffn2_w_down', 'final_norm']
TWIN_DIFF_INPUT = 'x'
TWIN_INPUTS = ['x', 'ffn1_norm', 'ffn1_w_gate', 'ffn1_w_up', 'ffn1_w_down', 'mix_norm', 'w_in', 'ssm_a_re_fwd', 'ssm_a_im_fwd', 'ssm_log_dt_fwd', 'ssm_b_re_fwd', 'ssm_b_im_fwd', 'ssm_c_re_fwd', 'ssm_c_im_fwd', 'ssm_a_re_bwd', 'ssm_a_im_bwd', 'ssm_log_dt_bwd', 'ssm_b_re_bwd', 'ssm_b_im_bwd', 'ssm_c_re_bwd', 'ssm_c_im_bwd', 'ssm_d', 'ssm_w_glu', 'ssm_b_glu', 'att_rpb', 'w_branch_ssm', 'w_branch_att', 'w_out', 'ffn2_norm', 'ffn2_w_gate', 'ffn2_w_up', 'ffn2_w_down', 'final_norm', 'loss_target', 'm_ffn1_norm', 'm_ffn1_w_gate', 'm_ffn1_w_up', 'm_ffn1_w_down', 'm_mix_norm', 'm_w_in', 'm_ssm_a_re_fwd', 'm_ssm_a_im_fwd', 'm_ssm_log_dt_fwd', 'm_ssm_b_re_fwd', 'm_ssm_b_im_fwd', 'm_ssm_c_re_fwd', 'm_ssm_c_im_fwd', 'm_ssm_a_re_bwd', 'm_ssm_a_im_bwd', 'm_ssm_log_dt_bwd', 'm_ssm_b_re_bwd', 'm_ssm_b_im_bwd', 'm_ssm_c_re_bwd', 'm_ssm_c_im_bwd', 'm_ssm_d', 'm_ssm_w_glu', 'm_ssm_b_glu', 'm_att_rpb', 'm_w_branch_ssm', 'm_w_branch_att', 'm_w_out', 'm_ffn2_norm', 'm_ffn2_w_gate', 'm_ffn2_w_up', 'm_ffn2_w_down', 'm_final_norm', 'v_ffn1_norm', 'v_ffn1_w_gate', 'v_ffn1_w_up', 'v_ffn1_w_down', 'v_mix_norm', 'v_w_in', 'v_ssm_a_re_fwd', 'v_ssm_a_im_fwd', 'v_ssm_log_dt_fwd', 'v_ssm_b_re_fwd', 'v_ssm_b_im_fwd', 'v_ssm_c_re_fwd', 'v_ssm_c_im_fwd', 'v_ssm_a_re_bwd', 'v_ssm_a_im_bwd', 'v_ssm_log_dt_bwd', 'v_ssm_b_re_bwd', 'v_ssm_b_im_bwd', 'v_ssm_c_re_bwd', 'v_ssm_c_im_bwd', 'v_ssm_d', 'v_ssm_w_glu', 'v_ssm_b_glu', 'v_att_rpb', 'v_w_branch_ssm', 'v_w_branch_att', 'v_w_out', 'v_ffn2_norm', 'v_ffn2_w_gate', 'v_ffn2_w_up', 'v_ffn2_w_down', 'v_final_norm']
TWIN_OUTPUTS = ['loss', 'grad_x', 'grad_ffn1_norm', 'grad_ffn1_w_gate', 'grad_ffn1_w_up', 'grad_ffn1_w_down', 'grad_mix_norm', 'grad_w_in', 'grad_ssm_a_re_fwd', 'grad_ssm_a_im_fwd', 'grad_ssm_log_dt_fwd', 'grad_ssm_b_re_fwd', 'grad_ssm_b_im_fwd', 'grad_ssm_c_re_fwd', 'grad_ssm_c_im_fwd', 'grad_ssm_a_re_bwd', 'grad_ssm_a_im_bwd', 'grad_ssm_log_dt_bwd', 'grad_ssm_b_re_bwd', 'grad_ssm_b_im_bwd', 'grad_ssm_c_re_bwd', 'grad_ssm_c_im_bwd', 'grad_ssm_d', 'grad_ssm_w_glu', 'grad_ssm_b_glu', 'grad_att_rpb', 'grad_w_branch_ssm', 'grad_w_branch_att', 'grad_w_out', 'grad_ffn2_norm', 'grad_ffn2_w_gate', 'grad_ffn2_w_up', 'grad_ffn2_w_down', 'grad_final_norm', 'delta_ffn1_norm', 'delta_ffn1_w_gate', 'delta_ffn1_w_up', 'delta_ffn1_w_down', 'delta_mix_norm', 'delta_w_in', 'delta_ssm_a_re_fwd', 'delta_ssm_a_im_fwd', 'delta_ssm_log_dt_fwd', 'delta_ssm_b_re_fwd', 'delta_ssm_b_im_fwd', 'delta_ssm_c_re_fwd', 'delta_ssm_c_im_fwd', 'delta_ssm_a_re_bwd', 'delta_ssm_a_im_bwd', 'delta_ssm_log_dt_bwd', 'delta_ssm_b_re_bwd', 'delta_ssm_b_im_bwd', 'delta_ssm_c_re_bwd', 'delta_ssm_c_im_bwd', 'delta_ssm_d', 'delta_ssm_w_glu', 'delta_ssm_b_glu', 'delta_att_rpb', 'delta_w_branch_ssm', 'delta_w_branch_att', 'delta_w_out', 'delta_ffn2_norm', 'delta_ffn2_w_gate', 'delta_ffn2_w_up', 'delta_ffn2_w_down', 'delta_final_norm', 'new_m_ffn1_norm', 'new_m_ffn1_w_gate', 'new_m_ffn1_w_up', 'new_m_ffn1_w_down', 'new_m_mix_norm', 'new_m_w_in', 'new_m_ssm_a_re_fwd', 'new_m_ssm_a_im_fwd', 'new_m_ssm_log_dt_fwd', 'new_m_ssm_b_re_fwd', 'new_m_ssm_b_im_fwd', 'new_m_ssm_c_re_fwd', 'new_m_ssm_c_im_fwd', 'new_m_ssm_a_re_bwd', 'new_m_ssm_a_im_bwd', 'new_m_ssm_log_dt_bwd', 'new_m_ssm_b_re_bwd', 'new_m_ssm_b_im_bwd', 'new_m_ssm_c_re_bwd', 'new_m_ssm_c_im_bwd', 'new_m_ssm_d', 'new_m_ssm_w_glu', 'new_m_ssm_b_glu', 'new_m_att_rpb', 'new_m_w_branch_ssm', 'new_m_w_branch_att', 'new_m_w_out', 'new_m_ffn2_norm', 'new_m_ffn2_w_gate', 'new_m_ffn2_w_up', 'new_m_ffn2_w_down', 'new_m_final_norm', 'new_v_ffn1_norm', 'new_v_ffn1_w_gate', 'new_v_ffn1_w_up', 'new_v_ffn1_w_down', 'new_v_mix_norm', 'new_v_w_in', 'new_v_ssm_a_re_fwd', 'new_v_ssm_a_im_fwd', 'new_v_ssm_log_dt_fwd', 'new_v_ssm_b_re_fwd', 'new_v_ssm_b_im_fwd', 'new_v_ssm_c_re_fwd', 'new_v_ssm_c_im_fwd', 'new_v_ssm_a_re_bwd', 'new_v_ssm_a_im_bwd', 'new_v_ssm_log_dt_bwd', 'new_v_ssm_b_re_bwd', 'new_v_ssm_b_im_bwd', 'new_v_ssm_c_re_bwd', 'new_v_ssm_c_im_bwd', 'new_v_ssm_d', 'new_v_ssm_w_glu', 'new_v_ssm_b_glu', 'new_v_att_rpb', 'new_v_w_branch_ssm', 'new_v_w_branch_att', 'new_v_w_out', 'new_v_ffn2_norm', 'new_v_ffn2_w_gate', 'new_v_ffn2_w_up', 'new_v_ffn2_w_down', 'new_v_final_norm']
TWIN_LEAF_KINDS = {'loss': 'loss', 'grad_x': 'grad_x', 'grad_ffn1_norm': 'grad_w', 'grad_ffn1_w_gate': 'grad_w', 'grad_ffn1_w_up': 'grad_w', 'grad_ffn1_w_down': 'grad_w', 'grad_mix_norm': 'grad_w', 'grad_w_in': 'grad_w', 'grad_ssm_a_re_fwd': 'grad_w', 'grad_ssm_a_im_fwd': 'grad_w', 'grad_ssm_log_dt_fwd': 'grad_w', 'grad_ssm_b_re_fwd': 'grad_w', 'grad_ssm_b_im_fwd': 'grad_w', 'grad_ssm_c_re_fwd': 'grad_w', 'grad_ssm_c_im_fwd': 'grad_w', 'grad_ssm_a_re_bwd': 'grad_w', 'grad_ssm_a_im_bwd': 'grad_w', 'grad_ssm_log_dt_bwd': 'grad_w', 'grad_ssm_b_re_bwd': 'grad_w', 'grad_ssm_b_im_bwd': 'grad_w', 'grad_ssm_c_re_bwd': 'grad_w', 'grad_ssm_c_im_bwd': 'grad_w', 'grad_ssm_d': 'grad_w', 'grad_ssm_w_glu': 'grad_w', 'grad_ssm_b_glu': 'grad_w', 'grad_att_rpb': 'grad_w', 'grad_w_branch_ssm': 'grad_w', 'grad_w_branch_att': 'grad_w', 'grad_w_out': 'grad_w', 'grad_ffn2_norm': 'grad_w', 'grad_ffn2_w_gate': 'grad_w', 'grad_ffn2_w_up': 'grad_w', 'grad_ffn2_w_down': 'grad_w', 'grad_final_norm': 'grad_w', 'delta_ffn1_norm': 'delta_w', 'delta_ffn1_w_gate': 'delta_w', 'delta_ffn1_w_up': 'delta_w', 'delta_ffn1_w_down': 'delta_w', 'delta_mix_norm': 'delta_w', 'delta_w_in': 'delta_w', 'delta_ssm_a_re_fwd': 'delta_w', 'delta_ssm_a_im_fwd': 'delta_w', 'delta_ssm_log_dt_fwd': 'delta_w', 'delta_ssm_b_re_fwd': 'delta_w', 'delta_ssm_b_im_fwd': 'delta_w', 'delta_ssm_c_re_fwd': 'delta_w', 'delta_ssm_c_im_fwd': 'delta_w', 'delta_ssm_a_re_bwd': 'delta_w', 'delta_ssm_a_im_bwd': 'delta_w', 'delta_ssm_log_dt_bwd': 'delta_w', 'delta_ssm_b_re_bwd': 'delta_w', 'delta_ssm_b_im_bwd': 'delta_w', 'delta_ssm_c_re_bwd': 'delta_w', 'delta_ssm_c_im_bwd': 'delta_w', 'delta_ssm_d': 'delta_w', 'delta_ssm_w_glu': 'delta_w', 'delta_ssm_b_glu': 'delta_w', 'delta_att_rpb': 'delta_w', 'delta_w_branch_ssm': 'delta_w', 'delta_w_branch_att': 'delta_w', 'delta_w_out': 'delta_w', 'delta_ffn2_norm': 'delta_w', 'delta_ffn2_w_gate': 'delta_w', 'delta_ffn2_w_up': 'delta_w', 'delta_ffn2_w_down': 'delta_w', 'delta_final_norm': 'delta_w', 'new_m_ffn1_norm': 'new_m', 'new_m_ffn1_w_gate': 'new_m', 'new_m_ffn1_w_up': 'new_m', 'new_m_ffn1_w_down': 'new_m', 'new_m_mix_norm': 'new_m', 'new_m_w_in': 'new_m', 'new_m_ssm_a_re_fwd': 'new_m', 'new_m_ssm_a_im_fwd': 'new_m', 'new_m_ssm_log_dt_fwd': 'new_m', 'new_m_ssm_b_re_fwd': 'new_m', 'new_m_ssm_b_im_fwd': 'new_m', 'new_m_ssm_c_re_fwd': 'new_m', 'new_m_ssm_c_im_fwd': 'new_m', 'new_m_ssm_a_re_bwd': 'new_m', 'new_m_ssm_a_im_bwd': 'new_m', 'new_m_ssm_log_dt_bwd': 'new_m', 'new_m_ssm_b_re_bwd': 'new_m', 'new_m_ssm_b_im_bwd': 'new_m', 'new_m_ssm_c_re_bwd': 'new_m', 'new_m_ssm_c_im_bwd': 'new_m', 'new_m_ssm_d': 'new_m', 'new_m_ssm_w_glu': 'new_m', 'new_m_ssm_b_glu': 'new_m', 'new_m_att_rpb': 'new_m', 'new_m_w_branch_ssm': 'new_m', 'new_m_w_branch_att': 'new_m', 'new_m_w_out': 'new_m', 'new_m_ffn2_norm': 'new_m', 'new_m_ffn2_w_gate': 'new_m', 'new_m_ffn2_w_up': 'new_m', 'new_m_ffn2_w_down': 'new_m', 'new_m_final_norm': 'new_m', 'new_v_ffn1_norm': 'new_v', 'new_v_ffn1_w_gate': 'new_v', 'new_v_ffn1_w_up': 'new_v', 'new_v_ffn1_w_down': 'new_v', 'new_v_mix_norm': 'new_v', 'new_v_w_in': 'new_v', 'new_v_ssm_a_re_fwd': 'new_v', 'new_v_ssm_a_im_fwd': 'new_v', 'new_v_ssm_log_dt_fwd': 'new_v', 'new_v_ssm_b_re_fwd': 'new_v', 'new_v_ssm_b_im_fwd': 'new_v', 'new_v_ssm_c_re_fwd': 'new_v', 'new_v_ssm_c_im_fwd': 'new_v', 'new_v_ssm_a_re_bwd': 'new_v', 'new_v_ssm_a_im_bwd': 'new_v', 'new_v_ssm_log_dt_bwd': 'new_v', 'new_v_ssm_b_re_bwd': 'new_v', 'new_v_ssm_b_im_bwd': 'new_v', 'new_v_ssm_c_re_bwd': 'new_v', 'new_v_ssm_c_im_bwd': 'new_v', 'new_v_ssm_d': 'new_v', 'new_v_ssm_w_glu': 'new_v', 'new_v_ssm_b_glu': 'new_v', 'new_v_att_rpb': 'new_v', 'new_v_w_branch_ssm': 'new_v', 'new_v_w_branch_att': 'new_v', 'new_v_w_out': 'new_v', 'new_v_ffn2_norm': 'new_v', 'new_v_ffn2_w_gate': 'new_v', 'new_v_ffn2_w_up': 'new_v', 'new_v_ffn2_w_down': 'new_v', 'new_v_final_norm': 'new_v'}


def _forward(args):
    return _fwd_reference(*[args[k] for k in FWD_PARAMS])


def _output_shape():
    def fwd():
        inp = _fwd_setup_inputs(0)
        return _fwd_reference(*[inp[k] for k in FWD_PARAMS])
    out = _jax.eval_shape(fwd)
    return out.shape, out.dtype

N_MICROBATCH = 1
ADAM_LR = 0.001
ADAM_B1 = 0.9
ADAM_B2 = 0.999
ADAM_EPS = 1e-08
ADAM_WD = 0.01
ADAM_STEP = 10
PER_EXAMPLE_BATCH_AXIS = {'x': 0, 'loss_target': 0}
SHARED_INPUTS = []
_WEIGHT_DTYPES = {'ffn1_norm': _jnp.float32, 'ffn1_w_gate': _jnp.float32, 'ffn1_w_up': _jnp.float32, 'ffn1_w_down': _jnp.float32, 'mix_norm': _jnp.float32, 'w_in': _jnp.float32, 'ssm_a_re_fwd': _jnp.float32, 'ssm_a_im_fwd': _jnp.float32, 'ssm_log_dt_fwd': _jnp.float32, 'ssm_b_re_fwd': _jnp.float32, 'ssm_b_im_fwd': _jnp.float32, 'ssm_c_re_fwd': _jnp.float32, 'ssm_c_im_fwd': _jnp.float32, 'ssm_a_re_bwd': _jnp.float32, 'ssm_a_im_bwd': _jnp.float32, 'ssm_log_dt_bwd': _jnp.float32, 'ssm_b_re_bwd': _jnp.float32, 'ssm_b_im_bwd': _jnp.float32, 'ssm_c_re_bwd': _jnp.float32, 'ssm_c_im_bwd': _jnp.float32, 'ssm_d': _jnp.float32, 'ssm_w_glu': _jnp.float32, 'ssm_b_glu': _jnp.float32, 'att_rpb': _jnp.float32, 'w_branch_ssm': _jnp.float32, 'w_branch_att': _jnp.float32, 'w_out': _jnp.float32, 'ffn2_norm': _jnp.float32, 'ffn2_w_gate': _jnp.float32, 'ffn2_w_up': _jnp.float32, 'ffn2_w_down': _jnp.float32, 'final_norm': _jnp.float32}
MOMENT_SCALE = {'ffn1_norm': 1.146675e-01, 'ffn1_w_gate': 4.682860e-02, 'ffn1_w_up': 4.531036e-02, 'ffn1_w_down': 7.515099e-02, 'mix_norm': 6.863886e-02, 'w_in': 3.416343e-02, 'ssm_a_re_fwd': 5.289124e-03, 'ssm_a_im_fwd': 3.950218e-03, 'ssm_log_dt_fwd': 5.667021e+00, 'ssm_b_re_fwd': 3.117346e-03, 'ssm_b_im_fwd': 3.151203e-03, 'ssm_c_re_fwd': 4.444282e-03, 'ssm_c_im_fwd': 4.540493e-03, 'ssm_a_re_bwd': 4.893449e-03, 'ssm_a_im_bwd': 5.030394e-03, 'ssm_log_dt_bwd': 5.634610e+00, 'ssm_b_re_bwd': 3.347801e-03, 'ssm_b_im_bwd': 3.386145e-03, 'ssm_c_re_bwd': 4.799454e-03, 'ssm_c_im_bwd': 4.546333e-03, 'ssm_d': 8.356067e-02, 'ssm_w_glu': 1.903999e-02, 'ssm_b_glu': 2.890608e-02, 'att_rpb': 1.267251e-02, 'w_branch_ssm': 4.707418e-02, 'w_branch_att': 2.651064e-02, 'w_out': 5.206960e-02, 'ffn2_norm': 9.898873e-02, 'ffn2_w_gate': 4.226700e-02, 'ffn2_w_up': 4.092205e-02, 'ffn2_w_down': 6.767521e-02, 'final_norm': 6.394862e+01}


def _to_microbatches(a, axis):
    t = _jnp.moveaxis(a, axis, 0)
    t = t.reshape((N_MICROBATCH, t.shape[0] // N_MICROBATCH) + t.shape[1:])
    return _jnp.moveaxis(t, 1, axis + 1)


def setup_inputs(seed: int = 0) -> dict:
    inp = _fwd_setup_inputs(seed)
    key = _jax.random.fold_in(_jax.random.key(seed), 7919)
    shape, _ = _output_shape()
    out = dict(inp)
    out["loss_target"] = _jax.random.normal(_jax.random.fold_in(key, 0), shape, _jnp.float32)
    for i, name in enumerate(TWIN_WEIGHTS):
        w = inp[name].astype(_jnp.float32)
        if MOMENT_SCALE is None:
            s = _jnp.sqrt(_jnp.mean(_jnp.square(w)) + 1e-30)
        else:
            s = MOMENT_SCALE[name]
        km, kv = _jax.random.split(_jax.random.fold_in(key, i + 1))
        out[name] = w
        out["m_" + name] = s * _jax.random.normal(km, w.shape, _jnp.float32)
        out["v_" + name] = (s * s) * _jax.random.uniform(kv, w.shape, _jnp.float32, 0.5, 1.5)
    if N_MICROBATCH > 1:
        for name, axis in PER_EXAMPLE_BATCH_AXIS.items():
            out[name] = _to_microbatches(out[name], axis)
    return {'x': out['x'], 'ffn1_norm': out['ffn1_norm'], 'ffn1_w_gate': out['ffn1_w_gate'], 'ffn1_w_up': out['ffn1_w_up'], 'ffn1_w_down': out['ffn1_w_down'], 'mix_norm': out['mix_norm'], 'w_in': out['w_in'], 'ssm_a_re_fwd': out['ssm_a_re_fwd'], 'ssm_a_im_fwd': out['ssm_a_im_fwd'], 'ssm_log_dt_fwd': out['ssm_log_dt_fwd'], 'ssm_b_re_fwd': out['ssm_b_re_fwd'], 'ssm_b_im_fwd': out['ssm_b_im_fwd'], 'ssm_c_re_fwd': out['ssm_c_re_fwd'], 'ssm_c_im_fwd': out['ssm_c_im_fwd'], 'ssm_a_re_bwd': out['ssm_a_re_bwd'], 'ssm_a_im_bwd': out['ssm_a_im_bwd'], 'ssm_log_dt_bwd': out['ssm_log_dt_bwd'], 'ssm_b_re_bwd': out['ssm_b_re_bwd'], 'ssm_b_im_bwd': out['ssm_b_im_bwd'], 'ssm_c_re_bwd': out['ssm_c_re_bwd'], 'ssm_c_im_bwd': out['ssm_c_im_bwd'], 'ssm_d': out['ssm_d'], 'ssm_w_glu': out['ssm_w_glu'], 'ssm_b_glu': out['ssm_b_glu'], 'att_rpb': out['att_rpb'], 'w_branch_ssm': out['w_branch_ssm'], 'w_branch_att': out['w_branch_att'], 'w_out': out['w_out'], 'ffn2_norm': out['ffn2_norm'], 'ffn2_w_gate': out['ffn2_w_gate'], 'ffn2_w_up': out['ffn2_w_up'], 'ffn2_w_down': out['ffn2_w_down'], 'final_norm': out['final_norm'], 'loss_target': out['loss_target'], 'm_ffn1_norm': out['m_ffn1_norm'], 'm_ffn1_w_gate': out['m_ffn1_w_gate'], 'm_ffn1_w_up': out['m_ffn1_w_up'], 'm_ffn1_w_down': out['m_ffn1_w_down'], 'm_mix_norm': out['m_mix_norm'], 'm_w_in': out['m_w_in'], 'm_ssm_a_re_fwd': out['m_ssm_a_re_fwd'], 'm_ssm_a_im_fwd': out['m_ssm_a_im_fwd'], 'm_ssm_log_dt_fwd': out['m_ssm_log_dt_fwd'], 'm_ssm_b_re_fwd': out['m_ssm_b_re_fwd'], 'm_ssm_b_im_fwd': out['m_ssm_b_im_fwd'], 'm_ssm_c_re_fwd': out['m_ssm_c_re_fwd'], 'm_ssm_c_im_fwd': out['m_ssm_c_im_fwd'], 'm_ssm_a_re_bwd': out['m_ssm_a_re_bwd'], 'm_ssm_a_im_bwd': out['m_ssm_a_im_bwd'], 'm_ssm_log_dt_bwd': out['m_ssm_log_dt_bwd'], 'm_ssm_b_re_bwd': out['m_ssm_b_re_bwd'], 'm_ssm_b_im_bwd': out['m_ssm_b_im_bwd'], 'm_ssm_c_re_bwd': out['m_ssm_c_re_bwd'], 'm_ssm_c_im_bwd': out['m_ssm_c_im_bwd'], 'm_ssm_d': out['m_ssm_d'], 'm_ssm_w_glu': out['m_ssm_w_glu'], 'm_ssm_b_glu': out['m_ssm_b_glu'], 'm_att_rpb': out['m_att_rpb'], 'm_w_branch_ssm': out['m_w_branch_ssm'], 'm_w_branch_att': out['m_w_branch_att'], 'm_w_out': out['m_w_out'], 'm_ffn2_norm': out['m_ffn2_norm'], 'm_ffn2_w_gate': out['m_ffn2_w_gate'], 'm_ffn2_w_up': out['m_ffn2_w_up'], 'm_ffn2_w_down': out['m_ffn2_w_down'], 'm_final_norm': out['m_final_norm'], 'v_ffn1_norm': out['v_ffn1_norm'], 'v_ffn1_w_gate': out['v_ffn1_w_gate'], 'v_ffn1_w_up': out['v_ffn1_w_up'], 'v_ffn1_w_down': out['v_ffn1_w_down'], 'v_mix_norm': out['v_mix_norm'], 'v_w_in': out['v_w_in'], 'v_ssm_a_re_fwd': out['v_ssm_a_re_fwd'], 'v_ssm_a_im_fwd': out['v_ssm_a_im_fwd'], 'v_ssm_log_dt_fwd': out['v_ssm_log_dt_fwd'], 'v_ssm_b_re_fwd': out['v_ssm_b_re_fwd'], 'v_ssm_b_im_fwd': out['v_ssm_b_im_fwd'], 'v_ssm_c_re_fwd': out['v_ssm_c_re_fwd'], 'v_ssm_c_im_fwd': out['v_ssm_c_im_fwd'], 'v_ssm_a_re_bwd': out['v_ssm_a_re_bwd'], 'v_ssm_a_im_bwd': out['v_ssm_a_im_bwd'], 'v_ssm_log_dt_bwd': out['v_ssm_log_dt_bwd'], 'v_ssm_b_re_bwd': out['v_ssm_b_re_bwd'], 'v_ssm_b_im_bwd': out['v_ssm_b_im_bwd'], 'v_ssm_c_re_bwd': out['v_ssm_c_re_bwd'], 'v_ssm_c_im_bwd': out['v_ssm_c_im_bwd'], 'v_ssm_d': out['v_ssm_d'], 'v_ssm_w_glu': out['v_ssm_w_glu'], 'v_ssm_b_glu': out['v_ssm_b_glu'], 'v_att_rpb': out['v_att_rpb'], 'v_w_branch_ssm': out['v_w_branch_ssm'], 'v_w_branch_att': out['v_w_branch_att'], 'v_w_out': out['v_w_out'], 'v_ffn2_norm': out['v_ffn2_norm'], 'v_ffn2_w_gate': out['v_ffn2_w_gate'], 'v_ffn2_w_up': out['v_ffn2_w_up'], 'v_ffn2_w_down': out['v_ffn2_w_down'], 'v_final_norm': out['v_final_norm']}


def _loss(weights, diff, rest, loss_target):
    with _jax.named_scope("forward"):
        args = {**rest, TWIN_DIFF_INPUT: diff, **{k: w.astype(_WEIGHT_DTYPES[k]) for k, w in weights.items()}}
        y = _forward(args)
    with _jax.named_scope("loss_head"):
        err = _jnp.square(y.astype(_jnp.float32) - loss_target)
        return 0.5 * _jnp.sum(_jnp.mean(err, axis=-1)) if err.ndim else 0.5 * err


def _adamw(w, g, m, v):
    m = ADAM_B1 * m + (1.0 - ADAM_B1) * g
    v = ADAM_B2 * v + (1.0 - ADAM_B2) * _jnp.square(g)
    m_hat = m / (1.0 - ADAM_B1 ** ADAM_STEP)
    v_hat = v / (1.0 - ADAM_B2 ** ADAM_STEP)
    delta = -ADAM_LR * (m_hat / (_jnp.sqrt(v_hat) + ADAM_EPS) + ADAM_WD * w)
    return delta, m, v


def reference(x, ffn1_norm, ffn1_w_gate, ffn1_w_up, ffn1_w_down, mix_norm, w_in, ssm_a_re_fwd, ssm_a_im_fwd, ssm_log_dt_fwd, ssm_b_re_fwd, ssm_b_im_fwd, ssm_c_re_fwd, ssm_c_im_fwd, ssm_a_re_bwd, ssm_a_im_bwd, ssm_log_dt_bwd, ssm_b_re_bwd, ssm_b_im_bwd, ssm_c_re_bwd, ssm_c_im_bwd, ssm_d, ssm_w_glu, ssm_b_glu, att_rpb, w_branch_ssm, w_branch_att, w_out, ffn2_norm, ffn2_w_gate, ffn2_w_up, ffn2_w_down, final_norm, loss_target, m_ffn1_norm, m_ffn1_w_gate, m_ffn1_w_up, m_ffn1_w_down, m_mix_norm, m_w_in, m_ssm_a_re_fwd, m_ssm_a_im_fwd, m_ssm_log_dt_fwd, m_ssm_b_re_fwd, m_ssm_b_im_fwd, m_ssm_c_re_fwd, m_ssm_c_im_fwd, m_ssm_a_re_bwd, m_ssm_a_im_bwd, m_ssm_log_dt_bwd, m_ssm_b_re_bwd, m_ssm_b_im_bwd, m_ssm_c_re_bwd, m_ssm_c_im_bwd, m_ssm_d, m_ssm_w_glu, m_ssm_b_glu, m_att_rpb, m_w_branch_ssm, m_w_branch_att, m_w_out, m_ffn2_norm, m_ffn2_w_gate, m_ffn2_w_up, m_ffn2_w_down, m_final_norm, v_ffn1_norm, v_ffn1_w_gate, v_ffn1_w_up, v_ffn1_w_down, v_mix_norm, v_w_in, v_ssm_a_re_fwd, v_ssm_a_im_fwd, v_ssm_log_dt_fwd, v_ssm_b_re_fwd, v_ssm_b_im_fwd, v_ssm_c_re_fwd, v_ssm_c_im_fwd, v_ssm_a_re_bwd, v_ssm_a_im_bwd, v_ssm_log_dt_bwd, v_ssm_b_re_bwd, v_ssm_b_im_bwd, v_ssm_c_re_bwd, v_ssm_c_im_bwd, v_ssm_d, v_ssm_w_glu, v_ssm_b_glu, v_att_rpb, v_w_branch_ssm, v_w_branch_att, v_w_out, v_ffn2_norm, v_ffn2_w_gate, v_ffn2_w_up, v_ffn2_w_down, v_final_norm):
    given = dict(x=x, ffn1_norm=ffn1_norm, ffn1_w_gate=ffn1_w_gate, ffn1_w_up=ffn1_w_up, ffn1_w_down=ffn1_w_down, mix_norm=mix_norm, w_in=w_in, ssm_a_re_fwd=ssm_a_re_fwd, ssm_a_im_fwd=ssm_a_im_fwd, ssm_log_dt_fwd=ssm_log_dt_fwd, ssm_b_re_fwd=ssm_b_re_fwd, ssm_b_im_fwd=ssm_b_im_fwd, ssm_c_re_fwd=ssm_c_re_fwd, ssm_c_im_fwd=ssm_c_im_fwd, ssm_a_re_bwd=ssm_a_re_bwd, ssm_a_im_bwd=ssm_a_im_bwd, ssm_log_dt_bwd=ssm_log_dt_bwd, ssm_b_re_bwd=ssm_b_re_bwd, ssm_b_im_bwd=ssm_b_im_bwd, ssm_c_re_bwd=ssm_c_re_bwd, ssm_c_im_bwd=ssm_c_im_bwd, ssm_d=ssm_d, ssm_w_glu=ssm_w_glu, ssm_b_glu=ssm_b_glu, att_rpb=att_rpb, w_branch_ssm=w_branch_ssm, w_branch_att=w_branch_att, w_out=w_out, ffn2_norm=ffn2_norm, ffn2_w_gate=ffn2_w_gate, ffn2_w_up=ffn2_w_up, ffn2_w_down=ffn2_w_down, final_norm=final_norm, loss_target=loss_target, m_ffn1_norm=m_ffn1_norm, m_ffn1_w_gate=m_ffn1_w_gate, m_ffn1_w_up=m_ffn1_w_up, m_ffn1_w_down=m_ffn1_w_down, m_mix_norm=m_mix_norm, m_w_in=m_w_in, m_ssm_a_re_fwd=m_ssm_a_re_fwd, m_ssm_a_im_fwd=m_ssm_a_im_fwd, m_ssm_log_dt_fwd=m_ssm_log_dt_fwd, m_ssm_b_re_fwd=m_ssm_b_re_fwd, m_ssm_b_im_fwd=m_ssm_b_im_fwd, m_ssm_c_re_fwd=m_ssm_c_re_fwd, m_ssm_c_im_fwd=m_ssm_c_im_fwd, m_ssm_a_re_bwd=m_ssm_a_re_bwd, m_ssm_a_im_bwd=m_ssm_a_im_bwd, m_ssm_log_dt_bwd=m_ssm_log_dt_bwd, m_ssm_b_re_bwd=m_ssm_b_re_bwd, m_ssm_b_im_bwd=m_ssm_b_im_bwd, m_ssm_c_re_bwd=m_ssm_c_re_bwd, m_ssm_c_im_bwd=m_ssm_c_im_bwd, m_ssm_d=m_ssm_d, m_ssm_w_glu=m_ssm_w_glu, m_ssm_b_glu=m_ssm_b_glu, m_att_rpb=m_att_rpb, m_w_branch_ssm=m_w_branch_ssm, m_w_branch_att=m_w_branch_att, m_w_out=m_w_out, m_ffn2_norm=m_ffn2_norm, m_ffn2_w_gate=m_ffn2_w_gate, m_ffn2_w_up=m_ffn2_w_up, m_ffn2_w_down=m_ffn2_w_down, m_final_norm=m_final_norm, v_ffn1_norm=v_ffn1_norm, v_ffn1_w_gate=v_ffn1_w_gate, v_ffn1_w_up=v_ffn1_w_up, v_ffn1_w_down=v_ffn1_w_down, v_mix_norm=v_mix_norm, v_w_in=v_w_in, v_ssm_a_re_fwd=v_ssm_a_re_fwd, v_ssm_a_im_fwd=v_ssm_a_im_fwd, v_ssm_log_dt_fwd=v_ssm_log_dt_fwd, v_ssm_b_re_fwd=v_ssm_b_re_fwd, v_ssm_b_im_fwd=v_ssm_b_im_fwd, v_ssm_c_re_fwd=v_ssm_c_re_fwd, v_ssm_c_im_fwd=v_ssm_c_im_fwd, v_ssm_a_re_bwd=v_ssm_a_re_bwd, v_ssm_a_im_bwd=v_ssm_a_im_bwd, v_ssm_log_dt_bwd=v_ssm_log_dt_bwd, v_ssm_b_re_bwd=v_ssm_b_re_bwd, v_ssm_b_im_bwd=v_ssm_b_im_bwd, v_ssm_c_re_bwd=v_ssm_c_re_bwd, v_ssm_c_im_bwd=v_ssm_c_im_bwd, v_ssm_d=v_ssm_d, v_ssm_w_glu=v_ssm_w_glu, v_ssm_b_glu=v_ssm_b_glu, v_att_rpb=v_att_rpb, v_w_branch_ssm=v_w_branch_ssm, v_w_branch_att=v_w_branch_att, v_w_out=v_w_out, v_ffn2_norm=v_ffn2_norm, v_ffn2_w_gate=v_ffn2_w_gate, v_ffn2_w_up=v_ffn2_w_up, v_ffn2_w_down=v_ffn2_w_down, v_final_norm=v_final_norm)
    weights = {n: given[n] for n in TWIN_WEIGHTS}
    shared = {n: given[n] for n in SHARED_INPUTS}
    per_example = {n: given[n] for n in ['x']}
    grad_fn = _jax.value_and_grad(_loss, argnums=(0, 1))

    def one_microbatch(ex, loss_target):
        ex = dict(ex)
        diff = ex.pop(TWIN_DIFF_INPUT)
        return grad_fn(weights, diff, {**shared, **ex}, loss_target)

    if N_MICROBATCH == 1:
        loss, (grad_w, grad_x) = one_microbatch(per_example, given["loss_target"])
    else:
        def body(carry, xs):
            loss_sum, grad_sum = carry
            l_k, (gw_k, gx_k) = one_microbatch(xs[0], xs[1])
            with _jax.named_scope("update"):
                return (loss_sum + l_k, _jax.tree.map(_jnp.add, grad_sum, gw_k)), gx_k

        init = (_jnp.zeros((), _jnp.float32), _jax.tree.map(_jnp.zeros_like, weights))
        (loss, grad_w), grad_x = _jax.lax.scan(body, init, (per_example, given["loss_target"]))
    with _jax.named_scope("update"):
        delta_w, new_m, new_v = {}, {}, {}
        for n in TWIN_WEIGHTS:
            delta_w[n], new_m[n], new_v[n] = _adamw(weights[n], grad_w[n], given["m_" + n], given["v_" + n])
    return (loss, grad_x, *[grad_w[n] for n in TWIN_WEIGHTS], *[delta_w[n] for n in TWIN_WEIGHTS],
            *[new_m[n] for n in TWIN_WEIGHTS], *[new_v[n] for n in TWIN_WEIGHTS])
```

```python
import math

import numpy as np
import jax
import jax.numpy as jnp
from jax import lax
from jax.experimental import pallas as pl
from jax.experimental.pallas import tpu as pltpu

F32 = jnp.float32
BF16 = jnp.bfloat16
MESH_ID = pl.DeviceIdType.MESH

SSM_GROUP = 16
SSM_GROUPS = 32
SSM_STATE = 64
SSM_WIDTH = 512
ATT_HEADS = 8
ATT_HEAD_DIM = 64
ATT_WIDTH = 512
GRID_W = 64
WIN_H = 8
WIN_W = 16
EPS = 1e-6
NEG_INF = -1e30
ADAM_LR = 0.001
ADAM_B1 = 0.9
ADAM_B2 = 0.999
ADAM_EPS = 1e-08
ADAM_WD = 0.01
ADAM_STEP = 10

N_DEV = 8
V7X_VMEM_BYTES = 64 * 1024 * 1024
VMEM_LIMIT = V7X_VMEM_BYTES - 8 * 1024 * 1024
SCAN_LANES = 8
ATT_ROWS = 4


def _cparams(sem, vmem=None):
    return pltpu.CompilerParams(dimension_semantics=sem, vmem_limit_bytes=vmem)


def _dot(a, b):
    return jnp.dot(a, b, preferred_element_type=F32)


def _dot_nt(a, b):
    return lax.dot_general(a, b, (((1,), (1,)), ((), ())), preferred_element_type=F32)


def _dot_tn(a, b):
    return lax.dot_general(a, b, (((0,), (0,)), ((), ())), preferred_element_type=F32)


def _rms(h):
    return lax.rsqrt(jnp.mean(h * h, axis=-1, keepdims=True) + EPS)


def _rms_bwd(h, r, v):
    return r * v - h * (r * r * r) * jnp.mean(h * v, axis=-1, keepdims=True)


def _col_sum(x):
    return jnp.sum(x, axis=0, keepdims=True)


def _ffn_tiles(T, F):
    tm = min(T, 1024)
    tf = 256 if F % 256 == 0 else F
    return tm, tf


def _ffn_fwd(h, gain, wg, wu, wd, name):
    T, D = h.shape
    F = wg.shape[1]
    tm, tf = _ffn_tiles(T, F)
    nj = F // tf

    def body(h_ref, gain_ref, wg_ref, wu_ref, wd_ref, ho_ref, xn_ref, g_ref, u_ref, acc_ref):
        j = pl.program_id(1)

        @pl.when(j == 0)
        def _():
            hh = h_ref[...]
            xn_ref[...] = (hh * _rms(hh) * gain_ref[...]).astype(BF16)
            acc_ref[...] = jnp.zeros_like(acc_ref)

        xn = xn_ref[...]
        g = _dot(xn, wg_ref[...])
        u = _dot(xn, wu_ref[...])
        g_ref[...] = g.astype(BF16)
        u_ref[...] = u.astype(BF16)
        a = (g * jax.nn.sigmoid(g) * u).astype(BF16)
        acc_ref[...] += _dot(a, wd_ref[...])

        @pl.when(j == nj - 1)
        def _():
            ho_ref[...] = h_ref[...] + 0.5 * acc_ref[...]

    return pl.pallas_call(
        body, name=name, grid=(T // tm, nj),
        in_specs=[pl.BlockSpec((tm, D), lambda i, j: (i, 0)),
                  pl.BlockSpec((1, D), lambda i, j: (0, 0)),
                  pl.BlockSpec((D, tf), lambda i, j: (0, j)),
                  pl.BlockSpec((D, tf), lambda i, j: (0, j)),
                  pl.BlockSpec((tf, D), lambda i, j: (j, 0))],
        out_specs=[pl.BlockSpec((tm, D), lambda i, j: (i, 0)),
                   pl.BlockSpec((tm, D), lambda i, j: (i, 0)),
                   pl.BlockSpec((tm, tf), lambda i, j: (i, j)),
                   pl.BlockSpec((tm, tf), lambda i, j: (i, j))],
        out_shape=[jax.ShapeDtypeStruct((T, D), F32), jax.ShapeDtypeStruct((T, D), BF16),
                   jax.ShapeDtypeStruct((T, F), BF16), jax.ShapeDtypeStruct((T, F), BF16)],
        scratch_shapes=[pltpu.VMEM((tm, D), F32)],
        compiler_params=_cparams(("parallel", "arbitrary"), VMEM_LIMIT),
    )(h, gain, wg, wu, wd)


def _ffn_bwd(dho, h, gain, g, u, wg, wu, wd, name):
    T, D = h.shape
    F = wg.shape[1]
    tm, tf = _ffn_tiles(T, F)
    nj = F // tf

    def body(dho_ref, h_ref, gain_ref, g_ref, u_ref, wg_ref, wu_ref, wd_ref,
             dh_ref, dgain_ref, do_ref, a_ref, dg_ref, du_ref, acc_ref):
        i = pl.program_id(0)
        j = pl.program_id(1)

        @pl.when(j == 0)
        def _():
            do_ref[...] = (0.5 * dho_ref[...]).astype(BF16)
            acc_ref[...] = jnp.zeros_like(acc_ref)

        @pl.when((i == 0) & (j == 0))
        def _():
            dgain_ref[...] = jnp.zeros_like(dgain_ref)

        da = _dot_nt(do_ref[...], wd_ref[...])
        gg = g_ref[...].astype(F32)
        uu = u_ref[...].astype(F32)
        s = jax.nn.sigmoid(gg)
        sl = gg * s
        a_ref[...] = (sl * uu).astype(BF16)
        dg = (da * uu * (s * (1.0 + gg * (1.0 - s)))).astype(BF16)
        du = (da * sl).astype(BF16)
        dg_ref[...] = dg
        du_ref[...] = du
        acc_ref[...] += _dot_nt(dg, wg_ref[...]) + _dot_nt(du, wu_ref[...])

        @pl.when(j == nj - 1)
        def _():
            hh = h_ref[...]
            r = _rms(hh)
            dxn = acc_ref[...]
            dgain_ref[...] += _col_sum(dxn * hh * r)
            dh_ref[...] = dho_ref[...] + _rms_bwd(hh, r, dxn * gain_ref[...])

    return pl.pallas_call(
        body, name=name, grid=(T // tm, nj),
        in_specs=[pl.BlockSpec((tm, D), lambda i, j: (i, 0)),
                  pl.BlockSpec((tm, D), lambda i, j: (i, 0)),
                  pl.BlockSpec((1, D), lambda i, j: (0, 0)),
                  pl.BlockSpec((tm, tf), lambda i, j: (i, j)),
                  pl.BlockSpec((tm, tf), lambda i, j: (i, j)),
                  pl.BlockSpec((D, tf), lambda i, j: (0, j)),
                  pl.BlockSpec((D, tf), lambda i, j: (0, j)),
                  pl.BlockSpec((tf, D), lambda i, j: (j, 0))],
        out_specs=[pl.BlockSpec((tm, D), lambda i, j: (i, 0)),
                   pl.BlockSpec((1, D), lambda i, j: (0, 0)),
                   pl.BlockSpec((tm, D), lambda i, j: (i, 0)),
                   pl.BlockSpec((tm, tf), lambda i, j: (i, j)),
                   pl.BlockSpec((tm, tf), lambda i, j: (i, j)),
                   pl.BlockSpec((tm, tf), lambda i, j: (i, j))],
        out_shape=[jax.ShapeDtypeStruct((T, D), F32), jax.ShapeDtypeStruct((1, D), F32),
                   jax.ShapeDtypeStruct((T, D), BF16), jax.ShapeDtypeStruct((T, F), BF16),
                   jax.ShapeDtypeStruct((T, F), BF16), jax.ShapeDtypeStruct((T, F), BF16)],
        scratch_shapes=[pltpu.VMEM((tm, D), F32)],
        compiler_params=_cparams(("arbitrary", "arbitrary"), VMEM_LIMIT),
    )(dho, h, gain, g, u, wg, wu, wd)


def _xty(x, y, name):
    T, K = x.shape
    N = y.shape[1]
    tt = min(T, 1024)
    tk = K if K <= 1024 else (1408 if K % 1408 == 0 else K)
    tn = N if N <= 1024 else (1408 if N % 1408 == 0 else (1024 if N % 1024 == 0 else N))
    nt = T // tt

    def body(x_ref, y_ref, o_ref):
        t = pl.program_id(2)

        @pl.when(t == 0)
        def _():
            o_ref[...] = jnp.zeros_like(o_ref)

        o_ref[...] += _dot_tn(x_ref[...], y_ref[...])

    return pl.pallas_call(
        body, name=name, grid=(K // tk, N // tn, nt),
        in_specs=[pl.BlockSpec((tt, tk), lambda k, n, t: (t, k)),
                  pl.BlockSpec((tt, tn), lambda k, n, t: (t, n))],
        out_specs=pl.BlockSpec((tk, tn), lambda k, n, t: (k, n)),
        out_shape=jax.ShapeDtypeStruct((K, N), F32),
        compiler_params=_cparams(("parallel", "parallel", "arbitrary"), VMEM_LIMIT),
    )(x, y)


def _mixin_fwd(h, gain, w_in):
    T, D = h.shape
    N = w_in.shape[1]
    tm = min(T, 1024)
    tn = min(N, 512)
    nn = N // tn

    def body(h_ref, gain_ref, w_ref, z_ref, un_ref):
        @pl.when(pl.program_id(1) == 0)
        def _():
            hh = h_ref[...]
            un_ref[...] = (hh * _rms(hh) * gain_ref[...]).astype(BF16)

        z_ref[...] = _dot(un_ref[...], w_ref[...])

    return pl.pallas_call(
        body, name="mixin_fwd", grid=(T // tm, nn),
        in_specs=[pl.BlockSpec((tm, D), lambda i, n: (i, 0)),
                  pl.BlockSpec((1, D), lambda i, n: (0, 0)),
                  pl.BlockSpec((D, tn), lambda i, n: (0, n))],
        out_specs=[pl.BlockSpec((tm, tn), lambda i, n: (i, n)),
                   pl.BlockSpec((tm, D), lambda i, n: (i, 0))],
        out_shape=[jax.ShapeDtypeStruct((T, N), F32), jax.ShapeDtypeStruct((T, D), BF16)],
        compiler_params=_cparams(("parallel", "arbitrary"), VMEM_LIMIT),
    )(h, gain, w_in)


def _mixin_bwd(dz, dh_res, h, gain, w_in):
    T, D = h.shape
    N = w_in.shape[1]
    tm = min(T, 1024)
    tn = min(N, 512)
    nn = N // tn

    def body(dz_ref, dres_ref, h_ref, gain_ref, w_ref, dh_ref, dgain_ref, acc_ref):
        i = pl.program_id(0)
        n = pl.program_id(1)

        @pl.when(n == 0)
        def _():
            acc_ref[...] = jnp.zeros_like(acc_ref)

        @pl.when((i == 0) & (n == 0))
        def _():
            dgain_ref[...] = jnp.zeros_like(dgain_ref)

        acc_ref[...] += _dot_nt(dz_ref[...], w_ref[...])

        @pl.when(n == nn - 1)
        def _():
            hh = h_ref[...]
            r = _rms(hh)
            dun = acc_ref[...]
            dgain_ref[...] += _col_sum(dun * hh * r)
            dh_ref[...] = dres_ref[...] + _rms_bwd(hh, r, dun * gain_ref[...])

    return pl.pallas_call(
        body, name="mixin_bwd", grid=(T // tm, nn),
        in_specs=[pl.BlockSpec((tm, tn), lambda i, n: (i, n)),
                  pl.BlockSpec((tm, D), lambda i, n: (i, 0)),
                  pl.BlockSpec((tm, D), lambda i, n: (i, 0)),
                  pl.BlockSpec((1, D), lambda i, n: (0, 0)),
                  pl.BlockSpec((D, tn), lambda i, n: (0, n))],
        out_specs=[pl.BlockSpec((tm, D), lambda i, n: (i, 0)),
                   pl.BlockSpec((1, D), lambda i, n: (0, 0))],
        out_shape=[jax.ShapeDtypeStruct((T, D), F32), jax.ShapeDtypeStruct((1, D), F32)],
        scratch_shapes=[pltpu.VMEM((tm, D), F32)],
        compiler_params=_cparams(("arbitrary", "arbitrary"), VMEM_LIMIT),
    )(dz, dh_res, h, gain, w_in)


def _loss_head(h, gain, target):
    T, D = h.shape
    tm = min(T, 1024)

    def body(h_ref, gain_ref, t_ref, loss_ref, dh_ref, dgain_ref):
        @pl.when(pl.program_id(0) == 0)
        def _():
            loss_ref[...] = jnp.zeros_like(loss_ref)
            dgain_ref[...] = jnp.zeros_like(dgain_ref)

        hh = h_ref[...]
        r = _rms(hh)
        e = hh * r * gain_ref[...] - t_ref[...]
        loss_ref[...] += (0.5 / D) * jnp.sum(e * e)
        dy = e * (1.0 / D)
        dgain_ref[...] += _col_sum(dy * hh * r)
        dh_ref[...] = _rms_bwd(hh, r, dy * gain_ref[...])

    return pl.pallas_call(
        body, name="loss_head", grid=(T // tm,),
        in_specs=[pl.BlockSpec((tm, D), lambda i: (i, 0)),
                  pl.BlockSpec((1, D), lambda i: (0, 0)),
                  pl.BlockSpec((tm, D), lambda i: (i, 0))],
        out_specs=[pl.BlockSpec((1, 128), lambda i: (0, 0)),
                   pl.BlockSpec((tm, D), lambda i: (i, 0)),
                   pl.BlockSpec((1, D), lambda i: (0, 0))],
        out_shape=[jax.ShapeDtypeStruct((1, 128), F32), jax.ShapeDtypeStruct((T, D), F32),
                   jax.ShapeDtypeStruct((1, D), F32)],
        compiler_params=_cparams(("arbitrary",), VMEM_LIMIT),
    )(h, gain, target)


def _adamw(parts, w, m, v, name):
    R, C = w.shape
    mult = 16 if parts.dtype == BF16 else 8
    tr = max(t for t in range(mult, min(R, 512) + 1, mult) if R % t == 0)
    c1 = 1.0 - ADAM_B1 ** ADAM_STEP
    c2 = 1.0 - ADAM_B2 ** ADAM_STEP

    def body(p_ref, w_ref, m_ref, v_ref, g_ref, d_ref, nm_ref, nv_ref):
        g = p_ref[0].astype(F32)
        for k in range(1, N_DEV):
            g = g + p_ref[k].astype(F32)
        mm = ADAM_B1 * m_ref[...] + (1.0 - ADAM_B1) * g
        vv = ADAM_B2 * v_ref[...] + (1.0 - ADAM_B2) * (g * g)
        g_ref[...] = g
        nm_ref[...] = mm
        nv_ref[...] = vv
        d_ref[...] = -ADAM_LR * ((mm / c1) / (jnp.sqrt(vv / c2) + ADAM_EPS) + ADAM_WD * w_ref[...])

    spec = pl.BlockSpec((tr, C), lambda i: (i, 0))
    return pl.pallas_call(
        body, name=name, grid=(R // tr,),
        in_specs=[pl.BlockSpec((N_DEV, tr, C), lambda i: (0, i, 0)), spec, spec, spec],
        out_specs=[spec, spec, spec, spec],
        out_shape=[jax.ShapeDtypeStruct((R, C), F32)] * 4,
        compiler_params=_cparams(("parallel",), VMEM_LIMIT),
    )(parts, w, m, v)


S5_NS = 256
S5_NH = 2
S5_NCB = 4
S5_RC = 512


def _disc_math(a_re, a_im, log_dt, bt_re, bt_im):
    dt = jnp.exp(log_dt)
    zr, zi = a_re * dt, a_im * dt
    mag = jnp.exp(zr)
    lb_re, lb_im = mag * jnp.cos(zi), mag * jnp.sin(zi)
    den = a_re * a_re + a_im * a_im
    nr, ni = lb_re - 1.0, lb_im
    f_re = (nr * a_re + ni * a_im) / den
    f_im = (ni * a_re - nr * a_im) / den
    bb_re = f_re[:, None, :] * bt_re - f_im[:, None, :] * bt_im
    bb_im = f_re[:, None, :] * bt_im + f_im[:, None, :] * bt_re
    return lb_re, lb_im, bb_re, bb_im


def _disc_fwd(a_re, a_im, log_dt, bt_re, bt_im, seg_len, name):
    G, P = a_re.shape
    C = bt_re.shape[1]
    n_sq = int(round(math.log2(seg_len)))
    assert 2 ** n_sq == seg_len

    def body(a_re_ref, a_im_ref, ldt_ref, br_ref, bi_ref, lr_ref, li_ref, sr_ref, si_ref, bbr_ref, bbi_ref):
        lr, li, bbr, bbi = _disc_math(a_re_ref[...], a_im_ref[...], ldt_ref[...], br_ref[...], bi_ref[...])
        lr_ref[...] = lr
        li_ref[...] = li
        bbr_ref[...] = bbr
        bbi_ref[...] = bbi
        pr, pi = lr, li
        for _ in range(n_sq):
            pr, pi = pr * pr - pi * pi, 2.0 * pr * pi
        sr_ref[...] = pr
        si_ref[...] = pi

    s2 = jax.ShapeDtypeStruct((G, P), F32)
    s3 = jax.ShapeDtypeStruct((G, C, P), F32)
    return pl.pallas_call(body, name=name, out_shape=[s2, s2, s2, s2, s3, s3])(a_re, a_im, log_dt, bt_re, bt_im)


def _disc_bwd(a_re, a_im, log_dt, bt_re, bt_im, d_lr, d_li, d_bbr, d_bbi, name):
    G, P = a_re.shape
    C = bt_re.shape[1]

    def body(a_re_ref, a_im_ref, ldt_ref, br_ref, bi_ref, c1, c2, c3, c4, o1, o2, o3, o4, o5):
        _, vjp = jax.vjp(_disc_math, a_re_ref[...], a_im_ref[...], ldt_ref[...], br_ref[...], bi_ref[...])
        o1[...], o2[...], o3[...], o4[...], o5[...] = vjp((c1[...], c2[...], c3[...], c4[...]))

    s2 = jax.ShapeDtypeStruct((G, P), F32)
    s3 = jax.ShapeDtypeStruct((G, C, P), F32)
    return pl.pallas_call(body, name=name, out_shape=[s2, s2, jax.ShapeDtypeStruct((G, 1), F32), s3, s3])(
        a_re, a_im, log_dt, bt_re, bt_im, d_lr, d_li, d_bbr, d_bbi)


def _row_block(ib):
    return pl.ds(pl.multiple_of(ib * SCAN_LANES, SCAN_LANES), SCAN_LANES)


def _scan(xr_ref, xi_ref, lr, li, init_r, init_i, ascending, n_blocks, store):
    def step(i, carry):
        sr, si = carry
        rows = _row_block(jnp.where(ascending, i, n_blocks - 1 - i))
        nr = lr * sr - li * si + xr_ref[rows, :]
        ni = lr * si + li * sr + xi_ref[rows, :]
        if store:
            xr_ref[rows, :] = nr
            xi_ref[rows, :] = ni
        return nr, ni

    return lax.fori_loop(0, n_blocks, step, (init_r, init_i), unroll=8)


def _segment_starts(wr, wi, lsr, lsi, ascending):
    row = lax.broadcasted_iota(jnp.int32, wr.shape, 0)
    keep = row != jnp.where(ascending, 0, SCAN_LANES - 1)
    cr = jnp.zeros_like(wr)
    ci = jnp.zeros_like(wi)
    for _ in range(SCAN_LANES - 1):
        tr = lsr * cr - lsi * ci + wr
        ti = lsr * ci + lsi * cr + wi
        tr = jnp.where(ascending, pltpu.roll(tr, 1, 0), pltpu.roll(tr, SCAN_LANES - 1, 0))
        ti = jnp.where(ascending, pltpu.roll(ti, 1, 0), pltpu.roll(ti, SCAN_LANES - 1, 0))
        cr = jnp.where(keep, tr, 0.0)
        ci = jnp.where(keep, ti, 0.0)
    return cr, ci


def _full_scan(xr_ref, xi_ref, lam_ref, ascending, n_blocks, conj):
    shape = (SCAN_LANES, xr_ref.shape[1])
    sign = -1.0 if conj else 1.0
    lr = jnp.broadcast_to(lam_ref[0:1, :], shape)
    li = sign * jnp.broadcast_to(lam_ref[1:2, :], shape)
    lsr = jnp.broadcast_to(lam_ref[2:3, :], shape)
    lsi = sign * jnp.broadcast_to(lam_ref[3:4, :], shape)
    zero = jnp.zeros(shape, F32)
    wr, wi = _scan(xr_ref, xi_ref, lr, li, zero, zero, ascending, n_blocks, store=False)
    cr, ci = _segment_starts(wr, wi, lsr, lsi, ascending)
    return cr, ci, lr, li


def _s5_specs(T):
    NS = S5_NS
    tok = pl.BlockSpec((T, 128), lambda c, d, h: (0, c))
    b_spec = pl.BlockSpec((None, None, None, 128, NS), lambda c, d, h: (d, c, h, 0, 0))
    c_spec = pl.BlockSpec((None, None, None, NS, 128), lambda c, d, h: (d, c, h, 0, 0))
    lam_spec = pl.BlockSpec((None, None, None, 4, NS), lambda c, d, h: (d, c, h, 0, 0))
    return tok, b_spec, c_spec, lam_spec


def _s5_fwd(zp, bre, bim, lam, cre, cimn):
    T = zp.shape[0]
    NS = S5_NS
    nb = T // SCAN_LANES
    rc = min(S5_RC, T)
    tok, b_spec, c_spec, lam_spec = _s5_specs(T)

    def body(zp_ref, bre_ref, bim_ref, lam_ref, cre_ref, cim_ref, y_ref, xr_ref, xi_ref):
        d = pl.program_id(1)
        ascending = d == 0

        @pl.when((d == 0) & (pl.program_id(2) == 0))
        def _():
            y_ref[...] = jnp.zeros_like(y_ref)

        def proj(c, _):
            rows = pl.ds(pl.multiple_of(c * rc, rc), rc)
            zz = zp_ref[rows, :]
            xr_ref[rows, :] = _dot(zz, bre_ref[...])
            xi_ref[rows, :] = _dot(zz, bim_ref[...])
            return 0

        lax.fori_loop(0, T // rc, proj, 0)
        cr, ci, lr, li = _full_scan(xr_ref, xi_ref, lam_ref, ascending, nb, conj=False)
        _scan(xr_ref, xi_ref, lr, li, cr, ci, ascending, nb, store=True)

        def outp(c, _):
            rows = pl.ds(pl.multiple_of(c * rc, rc), rc)
            y_ref[rows, :] += (_dot(xr_ref[rows, :].astype(BF16), cre_ref[...])
                               + _dot(xi_ref[rows, :].astype(BF16), cim_ref[...]))
            return 0

        lax.fori_loop(0, T // rc, outp, 0)

    return pl.pallas_call(
        body, name="s5_fwd", grid=(S5_NCB, 2, S5_NH),
        in_specs=[tok, b_spec, b_spec, lam_spec, c_spec, c_spec],
        out_specs=tok,
        out_shape=jax.ShapeDtypeStruct((T, SSM_WIDTH), F32),
        scratch_shapes=[pltpu.VMEM((T, NS), F32), pltpu.VMEM((T, NS), F32)],
        compiler_params=_cparams(("parallel", "arbitrary", "arbitrary"), VMEM_LIMIT),
    )(zp, bre, bim, lam, cre, cimn)


def _s5_bwd(zp, dyp, bre, bim, lam, cre, cimn):
    T = zp.shape[0]
    NS, NH = S5_NS, S5_NH
    nb = T // SCAN_LANES
    rc = min(S5_RC, T)
    tok, b_spec, c_spec, lam_spec = _s5_specs(T)
    dlam_spec = pl.BlockSpec((None, None, None, 2, NS), lambda c, d, h: (d, c, h, 0, 0))

    def body(zp_ref, dyp_ref, bre_ref, bim_ref, lam_ref, cre_ref, cim_ref,
             dzp_ref, dbre_ref, dbim_ref, dlam_ref, dcre_ref, dcim_ref,
             sr_ref, si_ref, gr_ref, gi_ref):
        d = pl.program_id(1)
        ascending = d == 0
        g_ascending = d != 0

        @pl.when((d == 0) & (pl.program_id(2) == 0))
        def _():
            dzp_ref[...] = jnp.zeros_like(dzp_ref)

        dcre_ref[...] = jnp.zeros_like(dcre_ref)
        dcim_ref[...] = jnp.zeros_like(dcim_ref)
        dbre_ref[...] = jnp.zeros_like(dbre_ref)
        dbim_ref[...] = jnp.zeros_like(dbim_ref)

        def proj(c, _):
            rows = pl.ds(pl.multiple_of(c * rc, rc), rc)
            zz = zp_ref[rows, :]
            sr_ref[rows, :] = _dot(zz, bre_ref[...])
            si_ref[rows, :] = _dot(zz, bim_ref[...])
            dy = dyp_ref[rows, :]
            gr_ref[rows, :] = _dot_nt(dy, cre_ref[...])
            gi_ref[rows, :] = _dot_nt(dy, cim_ref[...])
            return 0

        lax.fori_loop(0, T // rc, proj, 0)
        cr, ci, lr, li = _full_scan(sr_ref, si_ref, lam_ref, ascending, nb, conj=False)
        _scan(sr_ref, si_ref, lr, li, cr, ci, ascending, nb, store=True)
        hr, hi, lr, lic = _full_scan(gr_ref, gi_ref, lam_ref, g_ascending, nb, conj=True)

        def dlam_terms(g_r, g_i, s_r, s_i, acc):
            return acc[0] + g_r * s_r + g_i * s_i, acc[1] + g_i * s_r - g_r * s_i

        def gstep(i, carry):
            g_r, g_i, ar, ai = carry
            ib = jnp.where(g_ascending, i, nb - 1 - i)
            rows = _row_block(ib)
            n_r = lr * g_r - lic * g_i + gr_ref[rows, :]
            n_i = lr * g_i + lic * g_r + gi_ref[rows, :]
            gr_ref[rows, :] = n_r
            gi_ref[rows, :] = n_i
            prev = _row_block(jnp.where(ascending, ib - 1, ib + 1))
            ar, ai = dlam_terms(n_r, n_i, sr_ref[prev, :], si_ref[prev, :], (ar, ai))
            return n_r, n_i, ar, ai

        zero = jnp.zeros((SCAN_LANES, NS), F32)
        g_r, g_i, ar, ai = lax.fori_loop(0, nb - 1, gstep, (hr, hi, zero, zero), unroll=4)
        rows = _row_block(jnp.where(g_ascending, nb - 1, 0))
        n_r = lr * g_r - lic * g_i + gr_ref[rows, :]
        n_i = lr * g_i + lic * g_r + gi_ref[rows, :]
        gr_ref[rows, :] = n_r
        gi_ref[rows, :] = n_i
        ar, ai = dlam_terms(n_r, n_i, cr, ci, (ar, ai))
        dlam_ref[0:1, :] = _col_sum(ar)
        dlam_ref[1:2, :] = _col_sum(ai)

        def grads(c, _):
            rows = pl.ds(pl.multiple_of(c * rc, rc), rc)
            zz = zp_ref[rows, :]
            dy = dyp_ref[rows, :]
            g_rb = gr_ref[rows, :].astype(BF16)
            g_ib = gi_ref[rows, :].astype(BF16)
            dcre_ref[...] += _dot_tn(sr_ref[rows, :].astype(BF16), dy)
            dcim_ref[...] += _dot_tn(si_ref[rows, :].astype(BF16), dy)
            dbre_ref[...] += _dot_tn(zz, g_rb)
            dbim_ref[...] += _dot_tn(zz, g_ib)
            dzp_ref[rows, :] += _dot_nt(g_rb, bre_ref[...]) + _dot_nt(g_ib, bim_ref[...])
            return 0

        lax.fori_loop(0, T // rc, grads, 0)

    f32 = lambda *s: jax.ShapeDtypeStruct(s, F32)
    return pl.pallas_call(
        body, name="s5_bwd", grid=(S5_NCB, 2, S5_NH),
        in_specs=[tok, tok, b_spec, b_spec, lam_spec, c_spec, c_spec],
        out_specs=[tok, b_spec, b_spec, dlam_spec, c_spec, c_spec],
        out_shape=[f32(T, SSM_WIDTH), f32(2, S5_NCB, NH, 128, NS), f32(2, S5_NCB, NH, 128, NS),
                   f32(2, S5_NCB, NH, 2, NS), f32(2, S5_NCB, NH, NS, 128), f32(2, S5_NCB, NH, NS, 128)],
        scratch_shapes=[pltpu.VMEM((T, NS), F32)] * 4,
        compiler_params=_cparams(("parallel", "arbitrary", "arbitrary"), VMEM_LIMIT),
    )(zp, dyp, bre, bim, lam, cre, cimn)


def _s5_delta():
    d = np.zeros((S5_NH, 8, 8 // S5_NH), np.float32)
    for h in range(S5_NH):
        for go in range(8 // S5_NH):
            d[h, h * (8 // S5_NH) + go, go] = 1.0
    return d


def _s5_pack_b(bbt):
    gh = 8 // S5_NH
    b5 = bbt.reshape(S5_NCB, S5_NH, gh, SSM_GROUP, SSM_STATE).transpose(0, 1, 3, 2, 4)
    m = b5[:, :, None] * _s5_delta()[None, :, :, None, :, None]
    return m.reshape(S5_NCB, S5_NH, 128, S5_NS)


def _s5_unpack_b(dm):
    gh = 8 // S5_NH
    d6 = dm.reshape(S5_NCB, S5_NH, 8, SSM_GROUP, gh, SSM_STATE)
    b5 = jnp.sum(d6 * _s5_delta()[None, :, :, None, :, None], axis=2)
    return b5.transpose(0, 1, 3, 2, 4).reshape(SSM_GROUPS, SSM_GROUP, SSM_STATE)


def _s5_pack_c(c):
    gh = 8 // S5_NH
    c5 = c.reshape(S5_NCB, S5_NH, gh, SSM_GROUP, SSM_STATE).transpose(0, 1, 2, 4, 3)
    m = c5[:, :, :, :, None, :] * _s5_delta().transpose(0, 2, 1)[None, :, :, None, :, None]
    return m.reshape(S5_NCB, S5_NH, S5_NS, 128)


def _s5_unpack_c(dm):
    gh = 8 // S5_NH
    d6 = dm.reshape(S5_NCB, S5_NH, gh, SSM_STATE, 8, SSM_GROUP)
    c5 = jnp.sum(d6 * _s5_delta().transpose(0, 2, 1)[None, :, :, None, :, None], axis=4)
    return c5.transpose(0, 1, 2, 4, 3).reshape(SSM_GROUPS, SSM_GROUP, SSM_STATE)


def _s5_pack_lam(x):
    return x.reshape(S5_NCB, S5_NH, S5_NS)


def _permute_rows(x):
    T = x.shape[0]
    return x.reshape(SCAN_LANES, T // SCAN_LANES, -1).transpose(1, 0, 2).reshape(T, -1)


def _unpermute_rows(x):
    T = x.shape[0]
    return x.reshape(T // SCAN_LANES, SCAN_LANES, -1).transpose(1, 0, 2).reshape(T, -1)


ATT_TB = ATT_ROWS * GRID_W
ATT_KB = 3 * ATT_TB


def _att_valid(i, n_rows):
    qi = lax.broadcasted_iota(jnp.int32, (ATT_TB, ATT_KB), 0)
    kj = lax.broadcasted_iota(jnp.int32, (ATT_TB, ATT_KB), 1)
    r = i * ATT_ROWS + qi // GRID_W
    c = qi % GRID_W
    rk = (i - 1) * ATT_ROWS + kj // GRID_W
    x = kj % GRID_W
    rs = jnp.clip(r - WIN_H // 2, 0, n_rows - WIN_H)
    cs = jnp.clip(c - WIN_W // 2, 0, GRID_W - WIN_W)
    return (rk >= rs) & (rk < rs + WIN_H) & (x >= cs) & (x < cs + WIN_W)


def _att_probs(qh, kh, bias, valid):
    s = jnp.where(valid, _dot_nt(qh, kh) + bias, NEG_INF)
    p = jnp.exp(s - jnp.max(s, axis=1, keepdims=True))
    return p / jnp.sum(p, axis=1, keepdims=True)


def _att_specs(n, width):
    last = n - 1
    cur = lambda i: (jnp.minimum(i, last), 0)
    prv = lambda i: (jnp.maximum(jnp.minimum(i, last) - 1, 0), 0)
    nxt = lambda i: (jnp.minimum(i + 1, last), 0)
    blk = lambda f: pl.BlockSpec((ATT_TB, width), f)
    return blk(cur), blk(prv), blk(nxt)


def _att_fwd(q, k, v, biasv):
    T, W = q.shape
    n = T // ATT_TB
    n_rows = T // GRID_W
    cur, prv, nxt = _att_specs(n, W)

    def body(q_ref, kp_ref, kc_ref, kn_ref, vp_ref, vc_ref, vn_ref, b_ref, y_ref):
        valid = _att_valid(pl.program_id(0), n_rows)
        qs = q_ref[...] * 0.125
        kb = jnp.concatenate([kp_ref[...], kc_ref[...], kn_ref[...]], axis=0)
        vb = jnp.concatenate([vp_ref[...], vc_ref[...], vn_ref[...]], axis=0)
        outs = []
        for h in range(ATT_HEADS):
            hs = slice(h * ATT_HEAD_DIM, (h + 1) * ATT_HEAD_DIM)
            p = _att_probs(qs[:, hs], kb[:, hs], b_ref[h], valid)
            outs.append(_dot(p.astype(BF16), vb[:, hs]))
        y_ref[...] = jnp.concatenate(outs, axis=1).astype(BF16)

    return pl.pallas_call(
        body, name="att_fwd", grid=(n,),
        in_specs=[cur, prv, cur, nxt, prv, cur, nxt,
                  pl.BlockSpec((ATT_HEADS, ATT_TB, ATT_KB), lambda i: (0, 0, 0))],
        out_specs=cur,
        out_shape=jax.ShapeDtypeStruct((T, W), BF16),
        compiler_params=_cparams(("parallel",), VMEM_LIMIT),
    )(q, k, k, k, v, v, v, biasv)


def _att_bwd(q, k, v, do, biasv):
    T, W = q.shape
    n = T // ATT_TB
    n_rows = T // GRID_W
    cur, prv, nxt = _att_specs(n, W)
    done = pl.BlockSpec((ATT_TB, W), lambda i: (jnp.maximum(i - 1, 0), 0))
    bias_spec = pl.BlockSpec((ATT_HEADS, ATT_TB, ATT_KB), lambda i: (0, 0, 0))

    def body(q_ref, do_ref, kp_ref, kc_ref, kn_ref, vp_ref, vc_ref, vn_ref, b_ref,
             dq_ref, dk_ref, dv_ref, db_ref, acck_ref, accv_ref):
        i = pl.program_id(0)

        @pl.when(i == 0)
        def _():
            db_ref[...] = jnp.zeros_like(db_ref)
            acck_ref[...] = jnp.zeros_like(acck_ref)
            accv_ref[...] = jnp.zeros_like(accv_ref)

        @pl.when((i > 0) & (i < n))
        def _():
            slot = lax.rem(i + 1, 3)
            acck_ref[slot] = jnp.zeros((ATT_TB, W), F32)
            accv_ref[slot] = jnp.zeros((ATT_TB, W), F32)

        @pl.when(i < n)
        def _():
            valid = _att_valid(i, n_rows)
            qs = q_ref[...] * 0.125
            dob = do_ref[...]
            kb = jnp.concatenate([kp_ref[...], kc_ref[...], kn_ref[...]], axis=0)
            vb = jnp.concatenate([vp_ref[...], vc_ref[...], vn_ref[...]], axis=0)
            dqs, dks, dvs = [], [], []
            for h in range(ATT_HEADS):
                hs = slice(h * ATT_HEAD_DIM, (h + 1) * ATT_HEAD_DIM)
                qh, kh, vh, doh = qs[:, hs], kb[:, hs], vb[:, hs], dob[:, hs]
                p = _att_probs(qh, kh, b_ref[h], valid)
                dp = _dot_nt(doh, vh)
                ds = p * (dp - jnp.sum(p * dp, axis=1, keepdims=True))
                db_ref[h] += ds
                dsb = ds.astype(BF16)
                dqs.append(_dot(dsb, kh) * 0.125)
                dks.append(_dot_tn(dsb, qh))
                dvs.append(_dot_tn(p.astype(BF16), doh))
            dq_ref[...] = jnp.concatenate(dqs, axis=1).astype(BF16)
            dk_all = jnp.concatenate(dks, axis=1)
            dv_all = jnp.concatenate(dvs, axis=1)
            for b in range(3):
                slot = lax.rem(i + 2 + b, 3)
                rows = slice(b * ATT_TB, (b + 1) * ATT_TB)
                acck_ref[slot] += dk_all[rows]
                accv_ref[slot] += dv_all[rows]

        slot = lax.rem(i + 2, 3)
        dk_ref[...] = acck_ref[slot].astype(BF16)
        dv_ref[...] = accv_ref[slot].astype(BF16)

    return pl.pallas_call(
        body, name="att_bwd", grid=(n + 1,),
        in_specs=[cur, cur, prv, cur, nxt, prv, cur, nxt, bias_spec],
        out_specs=[cur, done, done, bias_spec],
        out_shape=[jax.ShapeDtypeStruct((T, W), BF16)] * 3
        + [jax.ShapeDtypeStruct((ATT_HEADS, ATT_TB, ATT_KB), F32)],
        scratch_shapes=[pltpu.VMEM((3, ATT_TB, W), F32), pltpu.VMEM((3, ATT_TB, W), F32)],
        compiler_params=_cparams(("arbitrary",), VMEM_LIMIT),
    )(q, do, k, k, k, v, v, v, biasv)


def _att_selectors():
    rsel = np.zeros((ATT_ROWS, 3 * ATT_ROWS, 2 * WIN_H - 1), np.float32)
    for a in range(ATT_ROWS):
        for b in range(3 * ATT_ROWS):
            rsel[a, b, b - a - ATT_ROWS + WIN_H - 1] = 1.0
    csel = np.zeros((GRID_W, GRID_W, 2 * WIN_W - 1), np.float32)
    for c in range(GRID_W):
        for x in range(GRID_W):
            csel[c, x, min(max(x - c, -(WIN_W - 1)), WIN_W - 1) + WIN_W - 1] = 1.0
    return rsel, csel


def _att_bias_table(rpb):
    rsel, csel = _att_selectors()
    hi = lax.Precision.HIGHEST
    t = jnp.einsum('hrd,abr->habd', rpb, rsel, precision=hi)
    t = jnp.einsum('habd,cxd->hacbx', t, csel, precision=hi)
    return t.reshape(ATT_HEADS, ATT_TB, ATT_KB)


def _att_bias_table_t(dtable):
    rsel, csel = _att_selectors()
    hi = lax.Precision.HIGHEST
    t = dtable.reshape(ATT_HEADS, ATT_ROWS, GRID_W, 3 * ATT_ROWS, GRID_W)
    t = jnp.einsum('hacbx,cxd->habd', t, csel, precision=hi)
    return jnp.einsum('habd,abr->hrd', t, rsel, precision=hi)


GELU_K = math.sqrt(2.0 / math.pi)
GELU_C = 0.044715
MERGE_TM = 256


def _gelu(x):
    return 0.5 * x * (1.0 + jnp.tanh(GELU_K * (x + GELU_C * x * x * x)))


def _gelu_grad(x):
    t = jnp.tanh(GELU_K * (x + GELU_C * x * x * x))
    return 0.5 * (1.0 + t) + 0.5 * x * (1.0 - t * t) * GELU_K * (1.0 + 3.0 * GELU_C * x * x)


def _merge_forward(ypre, zs, gs, ga, ya, ssm_d, w_glu, b_glu, w_bs, w_ba):
    ys = ypre + ssm_d * zs
    yg = _gelu(ys)
    sg = jax.nn.sigmoid(_dot(yg.astype(BF16), w_glu) + b_glu)
    y2 = yg * sg
    bs = _dot(y2.astype(BF16), w_bs)
    ba = _dot(ya, w_ba)
    s1 = jax.nn.sigmoid(gs)
    s2 = jax.nn.sigmoid(ga)
    merged = s1 * bs + s2 * ba
    return ys, yg, sg, y2, bs, ba, s1, s2, merged


def _merge_in_specs(D, W, tm):
    tok = lambda w, c: pl.BlockSpec((tm, w), lambda i: (i, c))
    full = lambda r, c: pl.BlockSpec((r, c), lambda i: (0, 0))
    z_specs = [tok(W, 0), tok(D, 4 * W // D), tok(D, 4 * W // D + 1)]
    w_specs = [full(1, W), full(W, W), full(1, W), full(W, D), full(W, D), full(D, D)]
    return tok, z_specs, w_specs


def _merge_fwd(ypre, z, ya, h1, ssm_d, w_glu, b_glu, w_bs, w_ba, w_out):
    T, D = h1.shape
    W = ypre.shape[1]
    tm = min(T, MERGE_TM)
    tok, z_specs, w_specs = _merge_in_specs(D, W, tm)

    def body(ypre_ref, zs_ref, gs_ref, ga_ref, ya_ref, h1_ref, d_ref, wglu_ref, bglu_ref, wbs_ref, wba_ref, wout_ref,
             h2_ref):
        merged = _merge_forward(ypre_ref[...], zs_ref[...], gs_ref[...], ga_ref[...], ya_ref[...], d_ref[...],
                                wglu_ref[...], bglu_ref[...], wbs_ref[...], wba_ref[...])[-1]
        h2_ref[...] = h1_ref[...] + _dot(merged.astype(BF16), wout_ref[...])

    return pl.pallas_call(
        body, name="merge_fwd", grid=(T // tm,),
        in_specs=[tok(W, 0)] + z_specs + [tok(W, 0), tok(D, 0)] + w_specs,
        out_specs=tok(D, 0),
        out_shape=jax.ShapeDtypeStruct((T, D), F32),
        compiler_params=_cparams(("parallel",), VMEM_LIMIT),
    )(ypre, z, z, z, ya, h1, ssm_d, w_glu, b_glu, w_bs, w_ba, w_out)


def _merge_bwd(dh2, ypre, z, ya, ssm_d, w_glu, b_glu, w_bs, w_ba, w_out):
    T, D = dh2.shape
    W = ypre.shape[1]
    tm = min(T, MERGE_TM)
    tok, z_specs, w_specs = _merge_in_specs(D, W, tm)

    def body(dh2_ref, ypre_ref, zs_ref, gs_ref, ga_ref, ya_ref, d_ref, wglu_ref, bglu_ref, wbs_ref, wba_ref, wout_ref,
             dypre_ref, dzs_ref, dgs_ref, dga_ref, dya_ref, dd_ref, dwglu_ref, dbglu_ref, dwbs_ref, dwba_ref, dwout_ref):
        @pl.when(pl.program_id(0) == 0)
        def _():
            for r in (dd_ref, dwglu_ref, dbglu_ref, dwbs_ref, dwba_ref, dwout_ref):
                r[...] = jnp.zeros_like(r)

        zs = zs_ref[...]
        ya = ya_ref[...]
        ys, yg, sg, y2, bs, ba, s1, s2, merged = _merge_forward(
            ypre_ref[...], zs, gs_ref[...], ga_ref[...], ya, d_ref[...],
            wglu_ref[...], bglu_ref[...], wbs_ref[...], wba_ref[...])
        dh2b = dh2_ref[...].astype(BF16)
        dmerged = _dot_nt(dh2b, wout_ref[...])
        dwout_ref[...] += _dot_tn(merged.astype(BF16), dh2b)
        dbs = (dmerged * s1).astype(BF16)
        dba = (dmerged * s2).astype(BF16)
        dgs_ref[...] = (dmerged * bs * s1 * (1.0 - s1)).astype(BF16)
        dga_ref[...] = (dmerged * ba * s2 * (1.0 - s2)).astype(BF16)
        dwbs_ref[...] += _dot_tn(y2.astype(BF16), dbs)
        dwba_ref[...] += _dot_tn(ya, dba)
        dya_ref[...] = _dot_nt(dba, wba_ref[...]).astype(BF16)
        dy2 = _dot_nt(dbs, wbs_ref[...])
        dvv = dy2 * yg * sg * (1.0 - sg)
        dvvb = dvv.astype(BF16)
        dyg = dy2 * sg + _dot_nt(dvvb, wglu_ref[...])
        dwglu_ref[...] += _dot_tn(yg.astype(BF16), dvvb)
        dbglu_ref[...] += _col_sum(dvv)
        dys = dyg * _gelu_grad(ys)
        dd_ref[...] += _col_sum(dys * zs)
        dzs_ref[...] = dys * d_ref[...]
        dypre_ref[...] = dys

    f32 = lambda *s: jax.ShapeDtypeStruct(s, F32)
    b16 = lambda *s: jax.ShapeDtypeStruct(s, BF16)
    return pl.pallas_call(
        body, name="merge_bwd", grid=(T // tm,),
        in_specs=[tok(D, 0), tok(W, 0)] + z_specs + [tok(W, 0)] + w_specs,
        out_specs=[tok(W, 0), tok(W, 0), tok(D, 0), tok(D, 0), tok(W, 0)] + w_specs,
        out_shape=[f32(T, W), f32(T, W), b16(T, D), b16(T, D), b16(T, W),
                   f32(1, W), f32(W, W), f32(1, W), f32(W, D), f32(W, D), f32(D, D)],
        compiler_params=_cparams(("arbitrary",), VMEM_LIMIT),
    )(dh2, ypre, z, z, z, ya, ssm_d, w_glu, b_glu, w_bs, w_ba, w_out)


def _my_place():
    return lax.axis_index("x"), lax.axis_index("y"), lax.axis_index("c")


def _flat(px, py, pc):
    return 4 * px + 2 * py + pc


def _all_gather(shard):
    R, C = shard.shape

    def body(x_ref, out_ref, send_sems, recv_sems, local_sem):
        x, y, c = _my_place()
        me, sibling = (x, y, c), (x, y, 1 - c)
        chips = [(1 - x, y), (x, 1 - y), (1 - x, 1 - y)]

        def slot(place):
            return out_ref.at[_flat(*place)]

        def copy(k, block, to, src=None):
            return pltpu.make_async_remote_copy(
                src_ref=slot(block) if src is None else src, dst_ref=slot(block),
                send_sem=send_sems.at[k], recv_sem=recv_sems.at[k], device_id=to, device_id_type=MESH_ID)

        mine = pltpu.make_async_copy(x_ref, slot(me), local_sem)
        mine.start()
        first = [copy(0, me, sibling, src=x_ref)]
        first += [copy(1 + j, me, (*chip, c), src=x_ref) for j, chip in enumerate(chips)]
        for cp in first:
            cp.start()
        passed = [copy(4 + j, (*chip, c), sibling) for j, chip in enumerate(chips)]
        for j, chip in enumerate(chips):
            copy(1 + j, (*chip, c), me).wait_recv()
            passed[j].start()
        copy(0, sibling, me).wait_recv()
        for j, chip in enumerate(chips):
            copy(4 + j, (*chip, 1 - c), me).wait_recv()
        for cp in first + passed:
            cp.wait_send()
        mine.wait()

    hbm = pl.BlockSpec(memory_space=pltpu.HBM)
    return pl.pallas_call(
        body, name="all_gather_weights",
        in_specs=[hbm], out_specs=hbm,
        out_shape=jax.ShapeDtypeStruct((N_DEV, R, C), shard.dtype),
        scratch_shapes=[pltpu.SemaphoreType.DMA((7,)), pltpu.SemaphoreType.DMA((7,)), pltpu.SemaphoreType.DMA],
    )(shard)


def _exchange(parts):
    _, R, C = parts.shape

    def body(in_ref, out_ref, send_sems, recv_sems, local_sem):
        x, y, c = _my_place()
        mine = _flat(x, y, c)
        local = pltpu.make_async_copy(in_ref.at[mine], out_ref.at[mine], local_sem)
        local.start()
        copies = []
        for k in range(1, N_DEV):
            px = 1 - x if k & 4 else x
            py = 1 - y if k & 2 else y
            pc = 1 - c if k & 1 else c
            peer = _flat(px, py, pc)
            copies.append(pltpu.make_async_remote_copy(
                src_ref=in_ref.at[peer], dst_ref=out_ref.at[mine],
                send_sem=send_sems.at[k - 1], recv_sem=recv_sems.at[k - 1],
                device_id=(px, py, pc), device_id_type=MESH_ID))
        for cp in copies:
            cp.start()
        for cp in copies:
            cp.wait_recv()
        for cp in copies:
            cp.wait_send()
        local.wait()

    hbm = pl.BlockSpec(memory_space=pltpu.HBM)
    return pl.pallas_call(
        body, name="exchange_grads",
        in_specs=[hbm], out_specs=hbm,
        out_shape=jax.ShapeDtypeStruct(parts.shape, parts.dtype),
        scratch_shapes=[pltpu.SemaphoreType.DMA((7,)), pltpu.SemaphoreType.DMA((7,)), pltpu.SemaphoreType.DMA],
    )(parts)


PACK_COLS = 1024
BIG = (("ffn1_w_gate", 1), ("ffn1_w_up", 1), ("ffn1_w_down", 0), ("w_in", 1), ("ssm_w_glu", 0),
       ("w_branch_ssm", 1), ("w_branch_att", 1), ("w_out", 0),
       ("ffn2_w_gate", 1), ("ffn2_w_up", 1), ("ffn2_w_down", 0))
SSM_DIR = ("ssm_a_re", "ssm_a_im", "ssm_log_dt", "ssm_b_re", "ssm_b_im", "ssm_c_re", "ssm_c_im")
SMALL = (("ffn1_norm", "mix_norm")
         + tuple(n + "_fwd" for n in SSM_DIR) + tuple(n + "_bwd" for n in SSM_DIR)
         + ("ssm_d", "ssm_b_glu", "att_rpb", "ffn2_norm", "final_norm"))
WEIGHTS = ("ffn1_norm", "ffn1_w_gate", "ffn1_w_up", "ffn1_w_down", "mix_norm", "w_in") \
    + tuple(n + "_fwd" for n in SSM_DIR) + tuple(n + "_bwd" for n in SSM_DIR) \
    + ("ssm_d", "ssm_w_glu", "ssm_b_glu", "att_rpb", "w_branch_ssm", "w_branch_att", "w_out",
       "ffn2_norm", "ffn2_w_gate", "ffn2_w_up", "ffn2_w_down", "final_norm")


def _pad_rows(a, mult):
    pad = (-a.shape[-2]) % mult
    if pad:
        a = jnp.concatenate([a, jnp.zeros(a.shape[:-2] + (pad, a.shape[-1]), a.dtype)], axis=-2)
    return a


def _pack(arrays, row_mult):
    flat = jnp.concatenate([a.reshape(-1) for a in arrays])
    pad = (-flat.shape[0]) % PACK_COLS
    if pad:
        flat = jnp.concatenate([flat, jnp.zeros((pad,), flat.dtype)])
    return _pad_rows(flat.reshape(-1, PACK_COLS), row_mult)


def _unpack(slab, shapes):
    flat = slab.reshape(-1)
    out, at = [], 0
    for s in shapes:
        n = int(np.prod(s))
        out.append(flat[at:at + n].reshape(s))
        at += n
    return out


def _split_for_devices(g, axis):
    r, c = g.shape
    if axis == 1:
        return g.reshape(r, N_DEV, c // N_DEV).transpose(1, 0, 2).reshape(N_DEV, -1)
    return g.reshape(N_DEV, -1)


def _join_shards(gathered, shard_shape, axis):
    r, c = shard_shape
    g = gathered.reshape(N_DEV, r, c)
    if axis == 1:
        return g.transpose(1, 0, 2).reshape(r, N_DEV * c)
    return g.reshape(N_DEV * r, c)


def _s5_direction_inputs(p, sfx, seg_len):
    bt_re = p["ssm_b_re" + sfx][0].transpose(0, 2, 1)
    bt_im = p["ssm_b_im" + sfx][0].transpose(0, 2, 1)
    raw = (p["ssm_a_re" + sfx][0], p["ssm_a_im" + sfx][0], p["ssm_log_dt" + sfx][0][:, None], bt_re, bt_im)
    lr, li, sr, si, bbr, bbi = _disc_fwd(*raw, seg_len, "s5_disc" + sfx)
    lam = jnp.stack([_s5_pack_lam(t) for t in (lr, li, sr, si)], axis=2)
    mats = (_s5_pack_b(bbr), _s5_pack_b(bbi), lam,
            _s5_pack_c(p["ssm_c_re" + sfx][0]), _s5_pack_c(-p["ssm_c_im" + sfx][0]))
    return raw, mats


def kernel(x, ffn1_norm, ffn1_w_gate, ffn1_w_up, ffn1_w_down, mix_norm, w_in, ssm_a_re_fwd, ssm_a_im_fwd, ssm_log_dt_fwd, ssm_b_re_fwd, ssm_b_im_fwd, ssm_c_re_fwd, ssm_c_im_fwd, ssm_a_re_bwd, ssm_a_im_bwd, ssm_log_dt_bwd, ssm_b_re_bwd, ssm_b_im_bwd, ssm_c_re_bwd, ssm_c_im_bwd, ssm_d, ssm_w_glu, ssm_b_glu, att_rpb, w_branch_ssm, w_branch_att, w_out, ffn2_norm, ffn2_w_gate, ffn2_w_up, ffn2_w_down, final_norm, loss_target, m_ffn1_norm, m_ffn1_w_gate, m_ffn1_w_up, m_ffn1_w_down, m_mix_norm, m_w_in, m_ssm_a_re_fwd, m_ssm_a_im_fwd, m_ssm_log_dt_fwd, m_ssm_b_re_fwd, m_ssm_b_im_fwd, m_ssm_c_re_fwd, m_ssm_c_im_fwd, m_ssm_a_re_bwd, m_ssm_a_im_bwd, m_ssm_log_dt_bwd, m_ssm_b_re_bwd, m_ssm_b_im_bwd, m_ssm_c_re_bwd, m_ssm_c_im_bwd, m_ssm_d, m_ssm_w_glu, m_ssm_b_glu, m_att_rpb, m_w_branch_ssm, m_w_branch_att, m_w_out, m_ffn2_norm, m_ffn2_w_gate, m_ffn2_w_up, m_ffn2_w_down, m_final_norm, v_ffn1_norm, v_ffn1_w_gate, v_ffn1_w_up, v_ffn1_w_down, v_mix_norm, v_w_in, v_ssm_a_re_fwd, v_ssm_a_im_fwd, v_ssm_log_dt_fwd, v_ssm_b_re_fwd, v_ssm_b_im_fwd, v_ssm_c_re_fwd, v_ssm_c_im_fwd, v_ssm_a_re_bwd, v_ssm_a_im_bwd, v_ssm_log_dt_bwd, v_ssm_b_re_bwd, v_ssm_b_im_bwd, v_ssm_c_re_bwd, v_ssm_c_im_bwd, v_ssm_d, v_ssm_w_glu, v_ssm_b_glu, v_att_rpb, v_w_branch_ssm, v_w_branch_att, v_w_out, v_ffn2_norm, v_ffn2_w_gate, v_ffn2_w_up, v_ffn2_w_down, v_final_norm):
    p = dict(locals())
    x = p["x"][0]
    target = p["loss_target"][0]
    T, D = x.shape

    shard_shapes = {n: p[n].shape[1:] for n, _ in BIG}
    sizes = [int(np.prod(shard_shapes[n])) for n, _ in BIG]
    w_slab = _pack([p[n][0].astype(BF16) for n, _ in BIG], 16)
    gathered = _all_gather(w_slab).reshape(N_DEV, -1)
    full, at = {}, 0
    for (n, axis), size in zip(BIG, sizes):
        full[n] = _join_shards(gathered[:, at:at + size], shard_shapes[n], axis)
        at += size

    h0 = x
    h1, xn1, g1, u1 = _ffn_fwd(h0, p["ffn1_norm"], full["ffn1_w_gate"], full["ffn1_w_up"], full["ffn1_w_down"],
                               "ffn1_fwd")
    z, un = _mixin_fwd(h1, p["mix_norm"], full["w_in"])
    W = SSM_WIDTH
    zp = _permute_rows(z[:, :W]).astype(BF16)
    raw_f, mats_f = _s5_direction_inputs(p, "_fwd", T // SCAN_LANES)
    raw_b, mats_b = _s5_direction_inputs(p, "_bwd", T // SCAN_LANES)
    bre, bim, lam, cre, cimn = [jnp.stack([f, b]) for f, b in zip(mats_f, mats_b)]
    bre, bim, cre, cimn = [t.astype(BF16) for t in (bre, bim, cre, cimn)]
    ypre = _unpermute_rows(_s5_fwd(zp, bre, bim, lam, cre, cimn))
    q, k, v = [z[:, W * (1 + j):W * (2 + j)].astype(BF16) for j in range(3)]
    table = _att_bias_table(p["att_rpb"][0])
    ya = _att_fwd(q, k, v, table)
    tail_w = (p["ssm_d"], full["ssm_w_glu"], p["ssm_b_glu"], full["w_branch_ssm"], full["w_branch_att"], full["w_out"])
    h2 = _merge_fwd(ypre, z, ya, h1, *tail_w)
    h3, xn2, g2, u2 = _ffn_fwd(h2, p["ffn2_norm"], full["ffn2_w_gate"], full["ffn2_w_up"], full["ffn2_w_down"],
                               "ffn2_fwd")
    loss_part, dh3, d_final = _loss_head(h3, p["final_norm"][None], target)

    grads = {"final_norm": d_final[0]}
    dh2, grads["ffn2_norm"], do2, a2, dg2, du2 = _ffn_bwd(
        dh3, h2, p["ffn2_norm"], g2, u2, full["ffn2_w_gate"], full["ffn2_w_up"], full["ffn2_w_down"], "ffn2_bwd")
    grads["ffn2_w_gate"] = _xty(xn2, dg2, "ffn2_dw_gate")
    grads["ffn2_w_up"] = _xty(xn2, du2, "ffn2_dw_up")
    grads["ffn2_w_down"] = _xty(a2, do2, "ffn2_dw_down")
    (dypre, dzs_skip, dgs, dga, dya, grads["ssm_d"], grads["ssm_w_glu"], grads["ssm_b_glu"],
     grads["w_branch_ssm"], grads["w_branch_att"], grads["w_out"]) = _merge_bwd(dh2, ypre, z, ya, *tail_w)
    dq, dk, dv, dtable = _att_bwd(q, k, v, dya, table)
    grads["att_rpb"] = _att_bias_table_t(dtable)
    dyp = _permute_rows(dypre).astype(BF16)
    dzp, dbre, dbim, dlam, dcre, dcimn = _s5_bwd(zp, dyp, bre, bim, lam, cre, cimn)
    G, P = SSM_GROUPS, SSM_STATE
    for d, (sfx, raw) in enumerate((("_fwd", raw_f), ("_bwd", raw_b))):
        da_re, da_im, dldt, dbt_re, dbt_im = _disc_bwd(
            *raw, dlam[d, :, :, 0, :].reshape(G, P), dlam[d, :, :, 1, :].reshape(G, P),
            _s5_unpack_b(dbre[d]), _s5_unpack_b(dbim[d]), "s5_disc_grad" + sfx)
        grads["ssm_a_re" + sfx] = da_re
        grads["ssm_a_im" + sfx] = da_im
        grads["ssm_log_dt" + sfx] = dldt[:, 0]
        grads["ssm_b_re" + sfx] = dbt_re.transpose(0, 2, 1)
        grads["ssm_b_im" + sfx] = dbt_im.transpose(0, 2, 1)
        grads["ssm_c_re" + sfx] = _s5_unpack_c(dcre[d])
        grads["ssm_c_im" + sfx] = -_s5_unpack_c(dcimn[d])
    dzs = _unpermute_rows(dzp) + dzs_skip
    dz = jnp.concatenate([dzs.astype(BF16), dq, dk, dv, dgs, dga], axis=1)
    dh1, grads["mix_norm"] = _mixin_bwd(dz, dh2, h1, p["mix_norm"], full["w_in"])
    grads["w_in"] = _xty(un, dz, "dw_in")
    dh0, grads["ffn1_norm"], do1, a1, dg1, du1 = _ffn_bwd(
        dh1, h0, p["ffn1_norm"], g1, u1, full["ffn1_w_gate"], full["ffn1_w_up"], full["ffn1_w_down"], "ffn1_bwd")
    grads["ffn1_w_gate"] = _xty(xn1, dg1, "ffn1_dw_gate")
    grads["ffn1_w_up"] = _xty(xn1, du1, "ffn1_dw_up")
    grads["ffn1_w_down"] = _xty(a1, do1, "ffn1_dw_down")

    big_rows = w_slab.shape[0]
    big_part = jnp.concatenate([_split_for_devices(grads[n], axis) for n, axis in BIG], axis=1)
    big_part = _pad_rows(big_part.reshape(N_DEV, -1, PACK_COLS), 16).astype(BF16)
    small_shapes = [p[n].shape for n in SMALL]
    small_slab = _pack([grads[n].astype(F32) for n in SMALL], 8)
    small_rows = small_slab.shape[0]
    small_bits = lax.bitcast_convert_type(small_slab, BF16).reshape(2 * small_rows, PACK_COLS)
    parts = jnp.concatenate([big_part, jnp.broadcast_to(small_bits, (N_DEV,) + small_bits.shape)], axis=1)
    got = _exchange(parts)
    got_big = got[:, :big_rows]
    got_small = lax.bitcast_convert_type(got[:, big_rows:].reshape(N_DEV, small_rows, PACK_COLS, 2), F32)

    pack_big = lambda pre: _pack([p[pre + n][0] for n, _ in BIG], 16)
    big_out = _adamw(got_big, pack_big(""), pack_big("m_"), pack_big("v_"), "adamw_shards")
    pack_small = lambda pre: _pack([p[pre + n] for n in SMALL], 8)
    small_out = _adamw(got_small, pack_small(""), pack_small("m_"), pack_small("v_"), "adamw_small")
    results = []
    for b_slab, s_slab in zip(big_out, small_out):
        vals = dict(zip([n for n, _ in BIG], _unpack(b_slab, [p[n].shape for n, _ in BIG])))
        vals.update(zip(SMALL, _unpack(s_slab, small_shapes)))
        results.append(vals)

    loss = lax.psum(loss_part[0, 0], ("x", "y", "c"))
    out = [loss, dh0[None]]
    for vals in results:
        out += [vals[n] for n in WEIGHTS]
    return tuple(out)
```

```python
import math

import numpy as np
import jax
import jax.numpy as jnp
from jax import lax
from jax.experimental import pallas as pl
from jax.experimental.pallas import tpu as pltpu

F32 = jnp.float32
BF16 = jnp.bfloat16
MESH_ID = pl.DeviceIdType.MESH

SSM_GROUP = 16
SSM_GROUPS = 32
SSM_STATE = 64
SSM_WIDTH = 512
ATT_HEADS = 8
ATT_HEAD_DIM = 64
ATT_WIDTH = 512
GRID_W = 64
WIN_H = 8
WIN_W = 16
EPS = 1e-6
NEG_INF = -1e30
ADAM_LR = 0.001
ADAM_B1 = 0.9
ADAM_B2 = 0.999
ADAM_EPS = 1e-08
ADAM_WD = 0.01
ADAM_STEP = 10

N_DEV = 8
V7X_VMEM_BYTES = 64 * 1024 * 1024
VMEM_LIMIT = V7X_VMEM_BYTES - 8 * 1024 * 1024
SCAN_LANES = 8
ATT_ROWS = 4


def _cparams(sem, vmem=None):
    return pltpu.CompilerParams(dimension_semantics=sem, vmem_limit_bytes=vmem)


def _dot(a, b):
    return jnp.dot(a, b, preferred_element_type=F32)


def _dot_nt(a, b):
    return lax.dot_general(a, b, (((1,), (1,)), ((), ())), preferred_element_type=F32)


def _dot_tn(a, b):
    return lax.dot_general(a, b, (((0,), (0,)), ((), ())), preferred_element_type=F32)


def _rms(h):
    return lax.rsqrt(jnp.mean(h * h, axis=-1, keepdims=True) + EPS)


def _rms_bwd(h, r, v):
    return r * v - h * (r * r * r) * jnp.mean(h * v, axis=-1, keepdims=True)


def _col_sum(x):
    return jnp.sum(x, axis=0, keepdims=True)


def _ffn_tiles(T, F):
    tm = min(T, 1024)
    tf = 256 if F % 256 == 0 else F
    return tm, tf


def _ffn_fwd(h, gain, wg, wu, wd, name):
    T, D = h.shape
    F = wg.shape[1]
    tm, tf = _ffn_tiles(T, F)
    nj = F // tf

    def body(h_ref, gain_ref, wg_ref, wu_ref, wd_ref, ho_ref, xn_ref, g_ref, u_ref, acc_ref):
        j = pl.program_id(1)

        @pl.when(j == 0)
        def _():
            hh = h_ref[...]
            xn_ref[...] = (hh * _rms(hh) * gain_ref[...]).astype(BF16)
            acc_ref[...] = jnp.zeros_like(acc_ref)

        xn = xn_ref[...]
        g = _dot(xn, wg_ref[...])
        u = _dot(xn, wu_ref[...])
        g_ref[...] = g.astype(BF16)
        u_ref[...] = u.astype(BF16)
        a = (g * jax.nn.sigmoid(g) * u).astype(BF16)
        acc_ref[...] += _dot(a, wd_ref[...])

        @pl.when(j == nj - 1)
        def _():
            ho_ref[...] = h_ref[...] + 0.5 * acc_ref[...]

    return pl.pallas_call(
        body, name=name, grid=(T // tm, nj),
        in_specs=[pl.BlockSpec((tm, D), lambda i, j: (i, 0)),
                  pl.BlockSpec((1, D), lambda i, j: (0, 0)),
                  pl.BlockSpec((D, tf), lambda i, j: (0, j)),
                  pl.BlockSpec((D, tf), lambda i, j: (0, j)),
                  pl.BlockSpec((tf, D), lambda i, j: (j, 0))],
        out_specs=[pl.BlockSpec((tm, D), lambda i, j: (i, 0)),
                   pl.BlockSpec((tm, D), lambda i, j: (i, 0)),
                   pl.BlockSpec((tm, tf), lambda i, j: (i, j)),
                   pl.BlockSpec((tm, tf), lambda i, j: (i, j))],
        out_shape=[jax.ShapeDtypeStruct((T, D), F32), jax.ShapeDtypeStruct((T, D), BF16),
                   jax.ShapeDtypeStruct((T, F), BF16), jax.ShapeDtypeStruct((T, F), BF16)],
        scratch_shapes=[pltpu.VMEM((tm, D), F32)],
        compiler_params=_cparams(("parallel", "arbitrary"), VMEM_LIMIT),
    )(h, gain, wg, wu, wd)


def _ffn_bwd(dho, h, gain, g, u, wg, wu, wd, name):
    T, D = h.shape
    F = wg.shape[1]
    tm, tf = _ffn_tiles(T, F)
    nj = F // tf

    def body(dho_ref, h_ref, gain_ref, g_ref, u_ref, wg_ref, wu_ref, wd_ref,
             dh_ref, dgain_ref, do_ref, a_ref, dg_ref, du_ref, acc_ref):
        i = pl.program_id(0)
        j = pl.program_id(1)

        @pl.when(j == 0)
        def _():
            do_ref[...] = (0.5 * dho_ref[...]).astype(BF16)
            acc_ref[...] = jnp.zeros_like(acc_ref)

        @pl.when((i == 0) & (j == 0))
        def _():
            dgain_ref[...] = jnp.zeros_like(dgain_ref)

        da = _dot_nt(do_ref[...], wd_ref[...])
        gg = g_ref[...].astype(F32)
        uu = u_ref[...].astype(F32)
        s = jax.nn.sigmoid(gg)
        sl = gg * s
        a_ref[...] = (sl * uu).astype(BF16)
        dg = (da * uu * (s * (1.0 + gg * (1.0 - s)))).astype(BF16)
        du = (da * sl).astype(BF16)
        dg_ref[...] = dg
        du_ref[...] = du
        acc_ref[...] += _dot_nt(dg, wg_ref[...]) + _dot_nt(du, wu_ref[...])

        @pl.when(j == nj - 1)
        def _():
            hh = h_ref[...]
            r = _rms(hh)
            dxn = acc_ref[...]
            dgain_ref[...] += _col_sum(dxn * hh * r)
            dh_ref[...] = dho_ref[...] + _rms_bwd(hh, r, dxn * gain_ref[...])

    return pl.pallas_call(
        body, name=name, grid=(T // tm, nj),
        in_specs=[pl.BlockSpec((tm, D), lambda i, j: (i, 0)),
                  pl.BlockSpec((tm, D), lambda i, j: (i, 0)),
                  pl.BlockSpec((1, D), lambda i, j: (0, 0)),
                  pl.BlockSpec((tm, tf), lambda i, j: (i, j)),
                  pl.BlockSpec((tm, tf), lambda i, j: (i, j)),
                  pl.BlockSpec((D, tf), lambda i, j: (0, j)),
                  pl.BlockSpec((D, tf), lambda i, j: (0, j)),
                  pl.BlockSpec((tf, D), lambda i, j: (j, 0))],
        out_specs=[pl.BlockSpec((tm, D), lambda i, j: (i, 0)),
                   pl.BlockSpec((1, D), lambda i, j: (0, 0)),
                   pl.BlockSpec((tm, D), lambda i, j: (i, 0)),
                   pl.BlockSpec((tm, tf), lambda i, j: (i, j)),
                   pl.BlockSpec((tm, tf), lambda i, j: (i, j)),
                   pl.BlockSpec((tm, tf), lambda i, j: (i, j))],
        out_shape=[jax.ShapeDtypeStruct((T, D), F32), jax.ShapeDtypeStruct((1, D), F32),
                   jax.ShapeDtypeStruct((T, D), BF16), jax.ShapeDtypeStruct((T, F), BF16),
                   jax.ShapeDtypeStruct((T, F), BF16), jax.ShapeDtypeStruct((T, F), BF16)],
        scratch_shapes=[pltpu.VMEM((tm, D), F32)],
        compiler_params=_cparams(("arbitrary", "arbitrary"), VMEM_LIMIT),
    )(dho, h, gain, g, u, wg, wu, wd)


def _xty(x, y, name):
    T, K = x.shape
    N = y.shape[1]
    tt = min(T, 1024)
    tk = K if K <= 1024 else (1408 if K % 1408 == 0 else K)
    tn = N if N <= 1024 else (1408 if N % 1408 == 0 else (1024 if N % 1024 == 0 else N))
    nt = T // tt

    def body(x_ref, y_ref, o_ref):
        t = pl.program_id(2)

        @pl.when(t == 0)
        def _():
            o_ref[...] = jnp.zeros_like(o_ref)

        o_ref[...] += _dot_tn(x_ref[...], y_ref[...])

    return pl.pallas_call(
        body, name=name, grid=(K // tk, N // tn, nt),
        in_specs=[pl.BlockSpec((tt, tk), lambda k, n, t: (t, k)),
                  pl.BlockSpec((tt, tn), lambda k, n, t: (t, n))],
        out_specs=pl.BlockSpec((tk, tn), lambda k, n, t: (k, n)),
        out_shape=jax.ShapeDtypeStruct((K, N), F32),
        compiler_params=_cparams(("parallel", "parallel", "arbitrary"), VMEM_LIMIT),
    )(x, y)


def _mixin_fwd(h, gain, w_in):
    T, D = h.shape
    N = w_in.shape[1]
    tm = min(T, 1024)
    tn = min(N, 512)
    nn = N // tn

    def body(h_ref, gain_ref, w_ref, z_ref, zb_ref, un_ref):
        @pl.when(pl.program_id(1) == 0)
        def _():
            hh = h_ref[...]
            un_ref[...] = (hh * _rms(hh) * gain_ref[...]).astype(BF16)

        z = _dot(un_ref[...], w_ref[...])
        z_ref[...] = z
        zb_ref[...] = z.astype(BF16)

    return pl.pallas_call(
        body, name="mixin_fwd", grid=(T // tm, nn),
        in_specs=[pl.BlockSpec((tm, D), lambda i, n: (i, 0)),
                  pl.BlockSpec((1, D), lambda i, n: (0, 0)),
                  pl.BlockSpec((D, tn), lambda i, n: (0, n))],
        out_specs=[pl.BlockSpec((tm, tn), lambda i, n: (i, n)),
                   pl.BlockSpec((tm, tn), lambda i, n: (i, n)),
                   pl.BlockSpec((tm, D), lambda i, n: (i, 0))],
        out_shape=[jax.ShapeDtypeStruct((T, N), F32), jax.ShapeDtypeStruct((T, N), BF16),
                   jax.ShapeDtypeStruct((T, D), BF16)],
        compiler_params=_cparams(("parallel", "arbitrary"), VMEM_LIMIT),
    )(h, gain, w_in)


def _mixin_bwd(dz, dh_res, h, gain, w_in):
    T, D = h.shape
    N = w_in.shape[1]
    tm = min(T, 1024)
    tn = min(N, 512)
    nn = N // tn

    def body(dz_ref, dres_ref, h_ref, gain_ref, w_ref, dh_ref, dgain_ref, acc_ref):
        i = pl.program_id(0)
        n = pl.program_id(1)

        @pl.when(n == 0)
        def _():
            acc_ref[...] = jnp.zeros_like(acc_ref)

        @pl.when((i == 0) & (n == 0))
        def _():
            dgain_ref[...] = jnp.zeros_like(dgain_ref)

        acc_ref[...] += _dot_nt(dz_ref[...], w_ref[...])

        @pl.when(n == nn - 1)
        def _():
            hh = h_ref[...]
            r = _rms(hh)
            dun = acc_ref[...]
            dgain_ref[...] += _col_sum(dun * hh * r)
            dh_ref[...] = dres_ref[...] + _rms_bwd(hh, r, dun * gain_ref[...])

    return pl.pallas_call(
        body, name="mixin_bwd", grid=(T // tm, nn),
        in_specs=[pl.BlockSpec((tm, tn), lambda i, n: (i, n)),
                  pl.BlockSpec((tm, D), lambda i, n: (i, 0)),
                  pl.BlockSpec((tm, D), lambda i, n: (i, 0)),
                  pl.BlockSpec((1, D), lambda i, n: (0, 0)),
                  pl.BlockSpec((D, tn), lambda i, n: (0, n))],
        out_specs=[pl.BlockSpec((tm, D), lambda i, n: (i, 0)),
                   pl.BlockSpec((1, D), lambda i, n: (0, 0))],
        out_shape=[jax.ShapeDtypeStruct((T, D), F32), jax.ShapeDtypeStruct((1, D), F32)],
        scratch_shapes=[pltpu.VMEM((tm, D), F32)],
        compiler_params=_cparams(("arbitrary", "arbitrary"), VMEM_LIMIT),
    )(dz, dh_res, h, gain, w_in)


def _loss_head(h, gain, target):
    T, D = h.shape
    tm = min(T, 1024)

    def body(h_ref, gain_ref, t_ref, loss_ref, dh_ref, dgain_ref):
        @pl.when(pl.program_id(0) == 0)
        def _():
            loss_ref[...] = jnp.zeros_like(loss_ref)
            dgain_ref[...] = jnp.zeros_like(dgain_ref)

        hh = h_ref[...]
        r = _rms(hh)
        e = hh * r * gain_ref[...] - t_ref[...]
        loss_ref[...] += (0.5 / D) * jnp.sum(e * e)
        dy = e * (1.0 / D)
        dgain_ref[...] += _col_sum(dy * hh * r)
        dh_ref[...] = _rms_bwd(hh, r, dy * gain_ref[...])

    return pl.pallas_call(
        body, name="loss_head", grid=(T // tm,),
        in_specs=[pl.BlockSpec((tm, D), lambda i: (i, 0)),
                  pl.BlockSpec((1, D), lambda i: (0, 0)),
                  pl.BlockSpec((tm, D), lambda i: (i, 0))],
        out_specs=[pl.BlockSpec((1, 128), lambda i: (0, 0)),
                   pl.BlockSpec((tm, D), lambda i: (i, 0)),
                   pl.BlockSpec((1, D), lambda i: (0, 0))],
        out_shape=[jax.ShapeDtypeStruct((1, 128), F32), jax.ShapeDtypeStruct((T, D), F32),
                   jax.ShapeDtypeStruct((1, D), F32)],
        compiler_params=_cparams(("arbitrary",), VMEM_LIMIT),
    )(h, gain, target)


def _adamw(parts, w, m, v, name):
    R, C = w.shape
    mult = 16 if parts.dtype == BF16 else 8
    tr = max(t for t in range(mult, min(R, 512) + 1, mult) if R % t == 0)
    c1 = 1.0 - ADAM_B1 ** ADAM_STEP
    c2 = 1.0 - ADAM_B2 ** ADAM_STEP

    def body(p_ref, w_ref, m_ref, v_ref, g_ref, d_ref, nm_ref, nv_ref):
        g = p_ref[0].astype(F32)
        for k in range(1, N_DEV):
            g = g + p_ref[k].astype(F32)
        mm = ADAM_B1 * m_ref[...] + (1.0 - ADAM_B1) * g
        vv = ADAM_B2 * v_ref[...] + (1.0 - ADAM_B2) * (g * g)
        g_ref[...] = g
        nm_ref[...] = mm
        nv_ref[...] = vv
        d_ref[...] = -ADAM_LR * ((mm / c1) / (jnp.sqrt(vv / c2) + ADAM_EPS) + ADAM_WD * w_ref[...])

    spec = pl.BlockSpec((tr, C), lambda i: (i, 0))
    return pl.pallas_call(
        body, name=name, grid=(R // tr,),
        in_specs=[pl.BlockSpec((N_DEV, tr, C), lambda i: (0, i, 0)), spec, spec, spec],
        out_specs=[spec, spec, spec, spec],
        out_shape=[jax.ShapeDtypeStruct((R, C), F32)] * 4,
        compiler_params=_cparams(("parallel",), VMEM_LIMIT),
    )(parts, w, m, v)


S5_NS = 256
S5_NH = 2
S5_NCB = 4
S5_RC = 512
S5_NQ = 4


def _disc_math(a_re, a_im, log_dt, bt_re, bt_im):
    dt = jnp.exp(log_dt)
    zr, zi = a_re * dt, a_im * dt
    mag = jnp.exp(zr)
    lb_re, lb_im = mag * jnp.cos(zi), mag * jnp.sin(zi)
    den = a_re * a_re + a_im * a_im
    nr, ni = lb_re - 1.0, lb_im
    f_re = (nr * a_re + ni * a_im) / den
    f_im = (ni * a_re - nr * a_im) / den
    bb_re = f_re[:, None, :] * bt_re - f_im[:, None, :] * bt_im
    bb_im = f_re[:, None, :] * bt_im + f_im[:, None, :] * bt_re
    return lb_re, lb_im, bb_re, bb_im


def _disc_fwd(a_re, a_im, log_dt, bt_re, bt_im, chain_len, name):
    G, P = a_re.shape
    C = bt_re.shape[1]
    n_sq = int(round(math.log2(chain_len)))
    assert 2 ** n_sq == chain_len

    def body(a_re_ref, a_im_ref, ldt_ref, br_ref, bi_ref, lr_ref, li_ref, sr_ref, si_ref, bbr_ref, bbi_ref):
        lr, li, bbr, bbi = _disc_math(a_re_ref[...], a_im_ref[...], ldt_ref[...], br_ref[...], bi_ref[...])
        lr_ref[...] = lr
        li_ref[...] = li
        bbr_ref[...] = bbr
        bbi_ref[...] = bbi
        pr, pi = lr, li
        for _ in range(n_sq):
            pr, pi = pr * pr - pi * pi, 2.0 * pr * pi
        sr_ref[...] = pr
        si_ref[...] = pi

    s2 = jax.ShapeDtypeStruct((G, P), F32)
    s3 = jax.ShapeDtypeStruct((G, C, P), F32)
    return pl.pallas_call(body, name=name, out_shape=[s2, s2, s2, s2, s3, s3])(a_re, a_im, log_dt, bt_re, bt_im)


def _disc_bwd(a_re, a_im, log_dt, bt_re, bt_im, d_lr, d_li, d_bbr, d_bbi, name):
    G, P = a_re.shape
    C = bt_re.shape[1]

    def body(a_re_ref, a_im_ref, ldt_ref, br_ref, bi_ref, c1, c2, c3, c4, o1, o2, o3, o4, o5):
        _, vjp = jax.vjp(_disc_math, a_re_ref[...], a_im_ref[...], ldt_ref[...], br_ref[...], bi_ref[...])
        o1[...], o2[...], o3[...], o4[...], o5[...] = vjp((c1[...], c2[...], c3[...], c4[...]))

    s2 = jax.ShapeDtypeStruct((G, P), F32)
    s3 = jax.ShapeDtypeStruct((G, C, P), F32)
    return pl.pallas_call(body, name=name, out_shape=[s2, s2, jax.ShapeDtypeStruct((G, 1), F32), s3, s3])(
        a_re, a_im, log_dt, bt_re, bt_im, d_lr, d_li, d_bbr, d_bbi)


def _row_block(ib):
    return pl.ds(pl.multiple_of(ib * SCAN_LANES, SCAN_LANES), SCAN_LANES)


def _chain_block(j, i, ascending, n_blocks):
    at = j * (n_blocks // S5_NQ) + i
    return _row_block(jnp.where(ascending, at, n_blocks - 1 - at))


def _cmul_add(lr, li, sr, si, xr, xi):
    return lr * sr - li * si + xr, lr * si + li * sr + xi


def _scan(xr_ref, xi_ref, lr, li, init, ascending, n_blocks, store):
    def step(i, carry):
        out = []
        for j, (sr, si) in enumerate(carry):
            rows = _chain_block(j, i, ascending, n_blocks)
            nr, ni = _cmul_add(lr, li, sr, si, xr_ref[rows, :], xi_ref[rows, :])
            if store:
                xr_ref[rows, :] = nr
                xi_ref[rows, :] = ni
            out.append((nr, ni))
        return tuple(out)

    return lax.fori_loop(0, n_blocks // S5_NQ, step, init)


def _segment_starts(w, lsr, lsi, ascending):
    shape = w[0][0].shape
    row = lax.broadcasted_iota(jnp.int32, shape, 0)
    keep = row != jnp.where(ascending, 0, SCAN_LANES - 1)

    def shift(t):
        t = jnp.where(ascending, pltpu.roll(t, 1, 0), pltpu.roll(t, SCAN_LANES - 1, 0))
        return jnp.where(keep, t, 0.0)

    zero = jnp.zeros(shape, F32)
    c = [(zero, zero)] * S5_NQ
    for _ in range(SCAN_LANES):
        tr, ti = _cmul_add(lsr, lsi, *c[-1], *w[-1])
        c[0] = (shift(tr), shift(ti))
        for j in range(1, S5_NQ):
            c[j] = _cmul_add(lsr, lsi, *c[j - 1], *w[j - 1])
    return tuple(c)


def _first_pass(xr_ref, xi_ref, lam_ref, ascending, n_blocks, conj):
    shape = (SCAN_LANES, xr_ref.shape[1])
    sign = -1.0 if conj else 1.0
    lr = jnp.broadcast_to(lam_ref[0:1, :], shape)
    li = sign * jnp.broadcast_to(lam_ref[1:2, :], shape)
    lsr = jnp.broadcast_to(lam_ref[2:3, :], shape)
    lsi = sign * jnp.broadcast_to(lam_ref[3:4, :], shape)
    zero = jnp.zeros(shape, F32)
    w = _scan(xr_ref, xi_ref, lr, li, ((zero, zero),) * S5_NQ, ascending, n_blocks, store=False)
    return _segment_starts(w, lsr, lsi, ascending), lr, li


def _s5_specs(T):
    NS = S5_NS
    tok = pl.BlockSpec((T, 128), lambda c, d, h: (0, c))
    b_spec = pl.BlockSpec((None, None, None, 128, NS), lambda c, d, h: (d, c, h, 0, 0))
    c_spec = pl.BlockSpec((None, None, None, NS, 128), lambda c, d, h: (d, c, h, 0, 0))
    lam_spec = pl.BlockSpec((None, None, None, 4, NS), lambda c, d, h: (d, c, h, 0, 0))
    return tok, b_spec, c_spec, lam_spec


def _s5_fwd(zp, bre, bim, lam, cre, cimn):
    T = zp.shape[0]
    NS = S5_NS
    nb = T // SCAN_LANES
    rc = min(S5_RC, T)
    tok, b_spec, c_spec, lam_spec = _s5_specs(T)

    def body(zp_ref, bre_ref, bim_ref, lam_ref, cre_ref, cim_ref, y_ref, xr_ref, xi_ref):
        d = pl.program_id(1)
        ascending = d == 0

        @pl.when((d == 0) & (pl.program_id(2) == 0))
        def _():
            y_ref[...] = jnp.zeros_like(y_ref)

        def proj(c, _):
            rows = pl.ds(pl.multiple_of(c * rc, rc), rc)
            zz = zp_ref[rows, :]
            xr_ref[rows, :] = _dot(zz, bre_ref[...])
            xi_ref[rows, :] = _dot(zz, bim_ref[...])
            return 0

        lax.fori_loop(0, T // rc, proj, 0)
        starts, lr, li = _first_pass(xr_ref, xi_ref, lam_ref, ascending, nb, conj=False)
        _scan(xr_ref, xi_ref, lr, li, starts, ascending, nb, store=True)

        def outp(c, _):
            rows = pl.ds(pl.multiple_of(c * rc, rc), rc)
            y_ref[rows, :] += (_dot(xr_ref[rows, :].astype(BF16), cre_ref[...])
                               + _dot(xi_ref[rows, :].astype(BF16), cim_ref[...]))
            return 0

        lax.fori_loop(0, T // rc, outp, 0)

    return pl.pallas_call(
        body, name="s5_fwd", grid=(S5_NCB, 2, S5_NH),
        in_specs=[tok, b_spec, b_spec, lam_spec, c_spec, c_spec],
        out_specs=tok,
        out_shape=jax.ShapeDtypeStruct((T, SSM_WIDTH), F32),
        scratch_shapes=[pltpu.VMEM((T, NS), F32), pltpu.VMEM((T, NS), F32)],
        compiler_params=_cparams(("parallel", "arbitrary", "arbitrary"), VMEM_LIMIT),
    )(zp, bre, bim, lam, cre, cimn)


def _s5_bwd(zp, dyp, bre, bim, lam, cre, cimn):
    T = zp.shape[0]
    NS, NH = S5_NS, S5_NH
    nb = T // SCAN_LANES
    rc = min(S5_RC, T)
    tok, b_spec, c_spec, lam_spec = _s5_specs(T)
    dlam_spec = pl.BlockSpec((None, None, None, 2, NS), lambda c, d, h: (d, c, h, 0, 0))

    def body(zp_ref, dyp_ref, bre_ref, bim_ref, lam_ref, cre_ref, cim_ref,
             dzp_ref, dbre_ref, dbim_ref, dlam_ref, dcre_ref, dcim_ref,
             sr_ref, si_ref, gr_ref, gi_ref):
        d = pl.program_id(1)
        ascending = d == 0
        g_ascending = d != 0

        @pl.when((d == 0) & (pl.program_id(2) == 0))
        def _():
            dzp_ref[...] = jnp.zeros_like(dzp_ref)

        dcre_ref[...] = jnp.zeros_like(dcre_ref)
        dcim_ref[...] = jnp.zeros_like(dcim_ref)
        dbre_ref[...] = jnp.zeros_like(dbre_ref)
        dbim_ref[...] = jnp.zeros_like(dbim_ref)

        def proj(c, _):
            rows = pl.ds(pl.multiple_of(c * rc, rc), rc)
            zz = zp_ref[rows, :]
            sr_ref[rows, :] = _dot(zz, bre_ref[...])
            si_ref[rows, :] = _dot(zz, bim_ref[...])
            dy = dyp_ref[rows, :]
            gr_ref[rows, :] = _dot_nt(dy, cre_ref[...])
            gi_ref[rows, :] = _dot_nt(dy, cim_ref[...])
            return 0

        lax.fori_loop(0, T // rc, proj, 0)
        s_starts, lr, li = _first_pass(sr_ref, si_ref, lam_ref, ascending, nb, conj=False)
        _scan(sr_ref, si_ref, lr, li, s_starts, ascending, nb, store=True)
        g_starts, lr, lic = _first_pass(gr_ref, gi_ref, lam_ref, g_ascending, nb, conj=True)

        def gstep(i, carry, last):
            g, (ar, ai) = carry
            out = []
            for j, (g_r, g_i) in enumerate(g):
                rows = _chain_block(j, i, g_ascending, nb)
                n_r, n_i = _cmul_add(lr, lic, g_r, g_i, gr_ref[rows, :], gi_ref[rows, :])
                gr_ref[rows, :] = n_r
                gi_ref[rows, :] = n_i
                if last:
                    s_r, s_i = s_starts[S5_NQ - 1 - j]
                else:
                    prev = _chain_block(j, i + 1, g_ascending, nb)
                    s_r, s_i = sr_ref[prev, :], si_ref[prev, :]
                ar = ar + n_r * s_r + n_i * s_i
                ai = ai + n_i * s_r - n_r * s_i
                out.append((n_r, n_i))
            return tuple(out), (ar, ai)

        zero = jnp.zeros((SCAN_LANES, NS), F32)
        steps = nb // S5_NQ
        carry = lax.fori_loop(0, steps - 1, lambda i, c: gstep(i, c, False), (g_starts, (zero, zero)))
        _, (ar, ai) = gstep(steps - 1, carry, True)
        dlam_ref[0:1, :] = _col_sum(ar)
        dlam_ref[1:2, :] = _col_sum(ai)

        def grads(c, _):
            rows = pl.ds(pl.multiple_of(c * rc, rc), rc)
            zz = zp_ref[rows, :]
            dy = dyp_ref[rows, :]
            g_rb = gr_ref[rows, :].astype(BF16)
            g_ib = gi_ref[rows, :].astype(BF16)
            dcre_ref[...] += _dot_tn(sr_ref[rows, :].astype(BF16), dy)
            dcim_ref[...] += _dot_tn(si_ref[rows, :].astype(BF16), dy)
            dbre_ref[...] += _dot_tn(zz, g_rb)
            dbim_ref[...] += _dot_tn(zz, g_ib)
            dzp_ref[rows, :] += _dot_nt(g_rb, bre_ref[...]) + _dot_nt(g_ib, bim_ref[...])
            return 0

        lax.fori_loop(0, T // rc, grads, 0)

    f32 = lambda *s: jax.ShapeDtypeStruct(s, F32)
    return pl.pallas_call(
        body, name="s5_bwd", grid=(S5_NCB, 2, S5_NH),
        in_specs=[tok, tok, b_spec, b_spec, lam_spec, c_spec, c_spec],
        out_specs=[tok, b_spec, b_spec, dlam_spec, c_spec, c_spec],
        out_shape=[f32(T, SSM_WIDTH), f32(2, S5_NCB, NH, 128, NS), f32(2, S5_NCB, NH, 128, NS),
                   f32(2, S5_NCB, NH, 2, NS), f32(2, S5_NCB, NH, NS, 128), f32(2, S5_NCB, NH, NS, 128)],
        scratch_shapes=[pltpu.VMEM((T, NS), F32)] * 4,
        compiler_params=_cparams(("parallel", "arbitrary", "arbitrary"), VMEM_LIMIT),
    )(zp, dyp, bre, bim, lam, cre, cimn)


def _s5_delta():
    d = np.zeros((S5_NH, 8, 8 // S5_NH), np.float32)
    for h in range(S5_NH):
        for go in range(8 // S5_NH):
            d[h, h * (8 // S5_NH) + go, go] = 1.0
    return d


def _s5_pack_b(bbt):
    gh = 8 // S5_NH
    b5 = bbt.reshape(S5_NCB, S5_NH, gh, SSM_GROUP, SSM_STATE).transpose(0, 1, 3, 2, 4)
    m = b5[:, :, None] * _s5_delta()[None, :, :, None, :, None]
    return m.reshape(S5_NCB, S5_NH, 128, S5_NS)


def _s5_unpack_b(dm):
    gh = 8 // S5_NH
    d6 = dm.reshape(S5_NCB, S5_NH, 8, SSM_GROUP, gh, SSM_STATE)
    b5 = jnp.sum(d6 * _s5_delta()[None, :, :, None, :, None], axis=2)
    return b5.transpose(0, 1, 3, 2, 4).reshape(SSM_GROUPS, SSM_GROUP, SSM_STATE)


def _s5_pack_c(c):
    gh = 8 // S5_NH
    c5 = c.reshape(S5_NCB, S5_NH, gh, SSM_GROUP, SSM_STATE).transpose(0, 1, 2, 4, 3)
    m = c5[:, :, :, :, None, :] * _s5_delta().transpose(0, 2, 1)[None, :, :, None, :, None]
    return m.reshape(S5_NCB, S5_NH, S5_NS, 128)


def _s5_unpack_c(dm):
    gh = 8 // S5_NH
    d6 = dm.reshape(S5_NCB, S5_NH, gh, SSM_STATE, 8, SSM_GROUP)
    c5 = jnp.sum(d6 * _s5_delta().transpose(0, 2, 1)[None, :, :, None, :, None], axis=4)
    return c5.transpose(0, 1, 2, 4, 3).reshape(SSM_GROUPS, SSM_GROUP, SSM_STATE)


def _s5_pack_lam(x):
    return x.reshape(S5_NCB, S5_NH, S5_NS)


def _permute_rows(x):
    T = x.shape[0]
    return x.reshape(SCAN_LANES, T // SCAN_LANES, -1).transpose(1, 0, 2).reshape(T, -1)


def _unpermute_rows(x):
    T = x.shape[0]
    return x.reshape(T // SCAN_LANES, SCAN_LANES, -1).transpose(1, 0, 2).reshape(T, -1)


ATT_TB = ATT_ROWS * GRID_W
ATT_KB = 3 * ATT_TB


def _att_valid(i, n_rows):
    qi = lax.broadcasted_iota(jnp.int32, (ATT_TB, ATT_KB), 0)
    kj = lax.broadcasted_iota(jnp.int32, (ATT_TB, ATT_KB), 1)
    r = i * ATT_ROWS + qi // GRID_W
    c = qi % GRID_W
    rk = (i - 1) * ATT_ROWS + kj // GRID_W
    x = kj % GRID_W
    rs = jnp.clip(r - WIN_H // 2, 0, n_rows - WIN_H)
    cs = jnp.clip(c - WIN_W // 2, 0, GRID_W - WIN_W)
    return (rk >= rs) & (rk < rs + WIN_H) & (x >= cs) & (x < cs + WIN_W)


def _att_probs(qh, kh, bias, valid):
    s = jnp.where(valid, _dot_nt(qh, kh) + bias, NEG_INF)
    p = jnp.exp(s - jnp.max(s, axis=1, keepdims=True))
    return p / jnp.sum(p, axis=1, keepdims=True)


def _att_specs(n, col):
    last = n - 1
    cur = lambda i: (jnp.minimum(i, last), col)
    prv = lambda i: (jnp.maximum(jnp.minimum(i, last) - 1, 0), col)
    nxt = lambda i: (jnp.minimum(i + 1, last), col)
    blk = lambda f: pl.BlockSpec((ATT_TB, ATT_WIDTH), f)
    return blk(cur), blk(prv), blk(nxt)


def _att_fwd(zb, biasv):
    T = zb.shape[0]
    W = ATT_WIDTH
    n = T // ATT_TB
    n_rows = T // GRID_W
    cur = _att_specs(n, 0)[0]
    q_cur = _att_specs(n, 1)[0]
    k_cur, k_prv, k_nxt = _att_specs(n, 2)
    v_cur, v_prv, v_nxt = _att_specs(n, 3)

    def body(q_ref, kp_ref, kc_ref, kn_ref, vp_ref, vc_ref, vn_ref, b_ref, y_ref):
        valid = _att_valid(pl.program_id(0), n_rows)
        qs = q_ref[...] * 0.125
        kb = jnp.concatenate([kp_ref[...], kc_ref[...], kn_ref[...]], axis=0)
        vb = jnp.concatenate([vp_ref[...], vc_ref[...], vn_ref[...]], axis=0)
        outs = []
        for h in range(ATT_HEADS):
            hs = slice(h * ATT_HEAD_DIM, (h + 1) * ATT_HEAD_DIM)
            p = _att_probs(qs[:, hs], kb[:, hs], b_ref[h], valid)
            outs.append(_dot(p.astype(BF16), vb[:, hs]))
        y_ref[...] = jnp.concatenate(outs, axis=1).astype(BF16)

    return pl.pallas_call(
        body, name="att_fwd", grid=(n,),
        in_specs=[q_cur, k_prv, k_cur, k_nxt, v_prv, v_cur, v_nxt,
                  pl.BlockSpec((ATT_HEADS, ATT_TB, ATT_KB), lambda i: (0, 0, 0))],
        out_specs=cur,
        out_shape=jax.ShapeDtypeStruct((T, W), BF16),
        compiler_params=_cparams(("parallel",), VMEM_LIMIT),
    )(zb, zb, zb, zb, zb, zb, zb, biasv)


def _att_bwd(zb, do, biasv):
    T = zb.shape[0]
    W = ATT_WIDTH
    n = T // ATT_TB
    n_rows = T // GRID_W
    cur = _att_specs(n, 0)[0]
    q_cur = _att_specs(n, 1)[0]
    k_cur, k_prv, k_nxt = _att_specs(n, 2)
    v_cur, v_prv, v_nxt = _att_specs(n, 3)
    done = pl.BlockSpec((ATT_TB, W), lambda i: (jnp.maximum(i - 1, 0), 0))
    bias_spec = pl.BlockSpec((ATT_HEADS, ATT_TB, ATT_KB), lambda i: (0, 0, 0))

    def body(q_ref, do_ref, kp_ref, kc_ref, kn_ref, vp_ref, vc_ref, vn_ref, b_ref,
             dq_ref, dk_ref, dv_ref, db_ref, acck_ref, accv_ref):
        i = pl.program_id(0)

        @pl.when(i == 0)
        def _():
            db_ref[...] = jnp.zeros_like(db_ref)
            acck_ref[...] = jnp.zeros_like(acck_ref)
            accv_ref[...] = jnp.zeros_like(accv_ref)

        @pl.when((i > 0) & (i < n))
        def _():
            slot = lax.rem(i + 1, 3)
            acck_ref[slot] = jnp.zeros((ATT_TB, W), F32)
            accv_ref[slot] = jnp.zeros((ATT_TB, W), F32)

        @pl.when(i < n)
        def _():
            valid = _att_valid(i, n_rows)
            qs = q_ref[...] * 0.125
            dob = do_ref[...]
            kb = jnp.concatenate([kp_ref[...], kc_ref[...], kn_ref[...]], axis=0)
            vb = jnp.concatenate([vp_ref[...], vc_ref[...], vn_ref[...]], axis=0)
            dqs, dks, dvs = [], [], []
            for h in range(ATT_HEADS):
                hs = slice(h * ATT_HEAD_DIM, (h + 1) * ATT_HEAD_DIM)
                qh, kh, vh, doh = qs[:, hs], kb[:, hs], vb[:, hs], dob[:, hs]
                p = _att_probs(qh, kh, b_ref[h], valid)
                dp = _dot_nt(doh, vh)
                ds = p * (dp - jnp.sum(p * dp, axis=1, keepdims=True))
                db_ref[h] += ds
                dsb = ds.astype(BF16)
                dqs.append(_dot(dsb, kh) * 0.125)
                dks.append(_dot_tn(dsb, qh))
                dvs.append(_dot_tn(p.astype(BF16), doh))
            dq_ref[...] = jnp.concatenate(dqs, axis=1).astype(BF16)
            dk_all = jnp.concatenate(dks, axis=1)
            dv_all = jnp.concatenate(dvs, axis=1)
            for b in range(3):
                slot = lax.rem(i + 2 + b, 3)
                rows = slice(b * ATT_TB, (b + 1) * ATT_TB)
                acck_ref[slot] += dk_all[rows]
                accv_ref[slot] += dv_all[rows]

        slot = lax.rem(i + 2, 3)
        dk_ref[...] = acck_ref[slot].astype(BF16)
        dv_ref[...] = accv_ref[slot].astype(BF16)

    return pl.pallas_call(
        body, name="att_bwd", grid=(n + 1,),
        in_specs=[q_cur, cur, k_prv, k_cur, k_nxt, v_prv, v_cur, v_nxt, bias_spec],
        out_specs=[cur, done, done, bias_spec],
        out_shape=[jax.ShapeDtypeStruct((T, W), BF16)] * 3
        + [jax.ShapeDtypeStruct((ATT_HEADS, ATT_TB, ATT_KB), F32)],
        scratch_shapes=[pltpu.VMEM((3, ATT_TB, W), F32), pltpu.VMEM((3, ATT_TB, W), F32)],
        compiler_params=_cparams(("arbitrary",), VMEM_LIMIT),
    )(zb, do, zb, zb, zb, zb, zb, zb, biasv)


def _att_selectors():
    rsel = np.zeros((ATT_ROWS, 3 * ATT_ROWS, 2 * WIN_H - 1), np.float32)
    for a in range(ATT_ROWS):
        for b in range(3 * ATT_ROWS):
            rsel[a, b, b - a - ATT_ROWS + WIN_H - 1] = 1.0
    csel = np.zeros((GRID_W, GRID_W, 2 * WIN_W - 1), np.float32)
    for c in range(GRID_W):
        for x in range(GRID_W):
            csel[c, x, min(max(x - c, -(WIN_W - 1)), WIN_W - 1) + WIN_W - 1] = 1.0
    return rsel, csel


def _att_bias_table(rpb):
    rsel, csel = _att_selectors()
    hi = lax.Precision.HIGHEST
    t = jnp.einsum('hrd,abr->habd', rpb, rsel, precision=hi)
    t = jnp.einsum('habd,cxd->hacbx', t, csel, precision=hi)
    return t.reshape(ATT_HEADS, ATT_TB, ATT_KB)


def _att_bias_table_t(dtable):
    rsel, csel = _att_selectors()
    hi = lax.Precision.HIGHEST
    t = dtable.reshape(ATT_HEADS, ATT_ROWS, GRID_W, 3 * ATT_ROWS, GRID_W)
    t = jnp.einsum('hacbx,cxd->habd', t, csel, precision=hi)
    return jnp.einsum('habd,abr->hrd', t, rsel, precision=hi)


GELU_K = math.sqrt(2.0 / math.pi)
GELU_C = 0.044715
MERGE_TM = 256


def _gelu(x):
    return 0.5 * x * (1.0 + jnp.tanh(GELU_K * (x + GELU_C * x * x * x)))


def _gelu_grad(x):
    t = jnp.tanh(GELU_K * (x + GELU_C * x * x * x))
    return 0.5 * (1.0 + t) + 0.5 * x * (1.0 - t * t) * GELU_K * (1.0 + 3.0 * GELU_C * x * x)


def _merge_forward(ypre, zs, gs, ga, ya, ssm_d, w_glu, b_glu, w_bs, w_ba):
    ys = ypre + ssm_d * zs
    yg = _gelu(ys)
    sg = jax.nn.sigmoid(_dot(yg.astype(BF16), w_glu) + b_glu)
    y2 = yg * sg
    bs = _dot(y2.astype(BF16), w_bs)
    ba = _dot(ya, w_ba)
    s1 = jax.nn.sigmoid(gs)
    s2 = jax.nn.sigmoid(ga)
    merged = s1 * bs + s2 * ba
    return ys, yg, sg, y2, bs, ba, s1, s2, merged


def _merge_in_specs(D, W, tm):
    tok = lambda w, c: pl.BlockSpec((tm, w), lambda i: (i, c))
    full = lambda r, c: pl.BlockSpec((r, c), lambda i: (0, 0))
    z_specs = [tok(W, 0), tok(D, 4 * W // D), tok(D, 4 * W // D + 1)]
    w_specs = [full(1, W), full(W, W), full(1, W), full(W, D), full(W, D), full(D, D)]
    return tok, z_specs, w_specs


def _merge_fwd(ypre, z, ya, h1, ssm_d, w_glu, b_glu, w_bs, w_ba, w_out):
    T, D = h1.shape
    W = ypre.shape[1]
    tm = min(T, MERGE_TM)
    tok, z_specs, w_specs = _merge_in_specs(D, W, tm)

    def body(ypre_ref, zs_ref, gs_ref, ga_ref, ya_ref, h1_ref, d_ref, wglu_ref, bglu_ref, wbs_ref, wba_ref, wout_ref,
             h2_ref):
        merged = _merge_forward(ypre_ref[...], zs_ref[...], gs_ref[...], ga_ref[...], ya_ref[...], d_ref[...],
                                wglu_ref[...], bglu_ref[...], wbs_ref[...], wba_ref[...])[-1]
        h2_ref[...] = h1_ref[...] + _dot(merged.astype(BF16), wout_ref[...])

    return pl.pallas_call(
        body, name="merge_fwd", grid=(T // tm,),
        in_specs=[tok(W, 0)] + z_specs + [tok(W, 0), tok(D, 0)] + w_specs,
        out_specs=tok(D, 0),
        out_shape=jax.ShapeDtypeStruct((T, D), F32),
        compiler_params=_cparams(("parallel",), VMEM_LIMIT),
    )(ypre, z, z, z, ya, h1, ssm_d, w_glu, b_glu, w_bs, w_ba, w_out)


def _merge_bwd(dh2, ypre, z, ya, ssm_d, w_glu, b_glu, w_bs, w_ba, w_out):
    T, D = dh2.shape
    W = ypre.shape[1]
    tm = min(T, MERGE_TM)
    tok, z_specs, w_specs = _merge_in_specs(D, W, tm)

    def body(dh2_ref, ypre_ref, zs_ref, gs_ref, ga_ref, ya_ref, d_ref, wglu_ref, bglu_ref, wbs_ref, wba_ref, wout_ref,
             dypre_ref, dzs_ref, dgs_ref, dga_ref, dya_ref, dd_ref, dwglu_ref, dbglu_ref, dwbs_ref, dwba_ref, dwout_ref):
        @pl.when(pl.program_id(0) == 0)
        def _():
            for r in (dd_ref, dwglu_ref, dbglu_ref, dwbs_ref, dwba_ref, dwout_ref):
                r[...] = jnp.zeros_like(r)

        zs = zs_ref[...]
        ya = ya_ref[...]
        ys, yg, sg, y2, bs, ba, s1, s2, merged = _merge_forward(
            ypre_ref[...], zs, gs_ref[...], ga_ref[...], ya, d_ref[...],
            wglu_ref[...], bglu_ref[...], wbs_ref[...], wba_ref[...])
        dh2b = dh2_ref[...].astype(BF16)
        dmerged = _dot_nt(dh2b, wout_ref[...])
        dwout_ref[...] += _dot_tn(merged.astype(BF16), dh2b)
        dbs = (dmerged * s1).astype(BF16)
        dba = (dmerged * s2).astype(BF16)
        dgs_ref[...] = (dmerged * bs * s1 * (1.0 - s1)).astype(BF16)
        dga_ref[...] = (dmerged * ba * s2 * (1.0 - s2)).astype(BF16)
        dwbs_ref[...] += _dot_tn(y2.astype(BF16), dbs)
        dwba_ref[...] += _dot_tn(ya, dba)
        dya_ref[...] = _dot_nt(dba, wba_ref[...]).astype(BF16)
        dy2 = _dot_nt(dbs, wbs_ref[...])
        dvv = dy2 * yg * sg * (1.0 - sg)
        dvvb = dvv.astype(BF16)
        dyg = dy2 * sg + _dot_nt(dvvb, wglu_ref[...])
        dwglu_ref[...] += _dot_tn(yg.astype(BF16), dvvb)
        dbglu_ref[...] += _col_sum(dvv)
        dys = dyg * _gelu_grad(ys)
        dd_ref[...] += _col_sum(dys * zs)
        dzs_ref[...] = dys * d_ref[...]
        dypre_ref[...] = dys

    f32 = lambda *s: jax.ShapeDtypeStruct(s, F32)
    b16 = lambda *s: jax.ShapeDtypeStruct(s, BF16)
    return pl.pallas_call(
        body, name="merge_bwd", grid=(T // tm,),
        in_specs=[tok(D, 0), tok(W, 0)] + z_specs + [tok(W, 0)] + w_specs,
        out_specs=[tok(W, 0), tok(W, 0), tok(D, 0), tok(D, 0), tok(W, 0)] + w_specs,
        out_shape=[f32(T, W), f32(T, W), b16(T, D), b16(T, D), b16(T, W),
                   f32(1, W), f32(W, W), f32(1, W), f32(W, D), f32(W, D), f32(D, D)],
        compiler_params=_cparams(("arbitrary",), VMEM_LIMIT),
    )(dh2, ypre, z, z, z, ya, ssm_d, w_glu, b_glu, w_bs, w_ba, w_out)


def _my_place():
    return lax.axis_index("x"), lax.axis_index("y"), lax.axis_index("c")


def _flat(px, py, pc):
    return 4 * px + 2 * py + pc


def _all_gather(shard):
    R, C = shard.shape

    def body(x_ref, out_ref, send_sems, recv_sems, local_sem):
        x, y, c = _my_place()
        me, sibling = (x, y, c), (x, y, 1 - c)
        chips = [(1 - x, y), (x, 1 - y), (1 - x, 1 - y)]

        def slot(place):
            return out_ref.at[_flat(*place)]

        def copy(k, block, to, src=None):
            return pltpu.make_async_remote_copy(
                src_ref=slot(block) if src is None else src, dst_ref=slot(block),
                send_sem=send_sems.at[k], recv_sem=recv_sems.at[k], device_id=to, device_id_type=MESH_ID)

        mine = pltpu.make_async_copy(x_ref, slot(me), local_sem)
        mine.start()
        first = [copy(0, me, sibling, src=x_ref)]
        first += [copy(1 + j, me, (*chip, c), src=x_ref) for j, chip in enumerate(chips)]
        for cp in first:
            cp.start()
        passed = [copy(4 + j, (*chip, c), sibling) for j, chip in enumerate(chips)]
        for j, chip in enumerate(chips):
            copy(1 + j, (*chip, c), me).wait_recv()
            passed[j].start()
        copy(0, sibling, me).wait_recv()
        for j, chip in enumerate(chips):
            copy(4 + j, (*chip, 1 - c), me).wait_recv()
        for cp in first + passed:
            cp.wait_send()
        mine.wait()

    hbm = pl.BlockSpec(memory_space=pltpu.HBM)
    return pl.pallas_call(
        body, name="all_gather_weights",
        in_specs=[hbm], out_specs=hbm,
        out_shape=jax.ShapeDtypeStruct((N_DEV, R, C), shard.dtype),
        scratch_shapes=[pltpu.SemaphoreType.DMA((7,)), pltpu.SemaphoreType.DMA((7,)), pltpu.SemaphoreType.DMA],
    )(shard)


def _exchange(parts):
    _, R, C = parts.shape

    def body(in_ref, out_ref, send_sems, recv_sems, local_sem):
        x, y, c = _my_place()
        mine = _flat(x, y, c)
        local = pltpu.make_async_copy(in_ref.at[mine], out_ref.at[mine], local_sem)
        local.start()
        copies = []
        for k in range(1, N_DEV):
            px = 1 - x if k & 4 else x
            py = 1 - y if k & 2 else y
            pc = 1 - c if k & 1 else c
            peer = _flat(px, py, pc)
            copies.append(pltpu.make_async_remote_copy(
                src_ref=in_ref.at[peer], dst_ref=out_ref.at[mine],
                send_sem=send_sems.at[k - 1], recv_sem=recv_sems.at[k - 1],
                device_id=(px, py, pc), device_id_type=MESH_ID))
        for cp in copies:
            cp.start()
        for cp in copies:
            cp.wait_recv()
        for cp in copies:
            cp.wait_send()
        local.wait()

    hbm = pl.BlockSpec(memory_space=pltpu.HBM)
    return pl.pallas_call(
        body, name="exchange_grads",
        in_specs=[hbm], out_specs=hbm,
        out_shape=jax.ShapeDtypeStruct(parts.shape, parts.dtype),
        scratch_shapes=[pltpu.SemaphoreType.DMA((7,)), pltpu.SemaphoreType.DMA((7,)), pltpu.SemaphoreType.DMA],
    )(parts)


PACK_COLS = 1024
BIG = (("ffn1_w_gate", 1), ("ffn1_w_up", 1), ("ffn1_w_down", 0), ("w_in", 1), ("ssm_w_glu", 0),
       ("w_branch_ssm", 1), ("w_branch_att", 1), ("w_out", 0),
       ("ffn2_w_gate", 1), ("ffn2_w_up", 1), ("ffn2_w_down", 0))
SSM_DIR = ("ssm_a_re", "ssm_a_im", "ssm_log_dt", "ssm_b_re", "ssm_b_im", "ssm_c_re", "ssm_c_im")
SMALL = (("ffn1_norm", "mix_norm")
         + tuple(n + "_fwd" for n in SSM_DIR) + tuple(n + "_bwd" for n in SSM_DIR)
         + ("ssm_d", "ssm_b_glu", "att_rpb", "ffn2_norm", "final_norm"))
WEIGHTS = ("ffn1_norm", "ffn1_w_gate", "ffn1_w_up", "ffn1_w_down", "mix_norm", "w_in") \
    + tuple(n + "_fwd" for n in SSM_DIR) + tuple(n + "_bwd" for n in SSM_DIR) \
    + ("ssm_d", "ssm_w_glu", "ssm_b_glu", "att_rpb", "w_branch_ssm", "w_branch_att", "w_out",
       "ffn2_norm", "ffn2_w_gate", "ffn2_w_up", "ffn2_w_down", "final_norm")


def _pad_rows(a, mult):
    pad = (-a.shape[-2]) % mult
    if pad:
        a = jnp.concatenate([a, jnp.zeros(a.shape[:-2] + (pad, a.shape[-1]), a.dtype)], axis=-2)
    return a


def _pack(arrays, row_mult):
    flat = jnp.concatenate([a.reshape(-1) for a in arrays])
    pad = (-flat.shape[0]) % PACK_COLS
    if pad:
        flat = jnp.concatenate([flat, jnp.zeros((pad,), flat.dtype)])
    return _pad_rows(flat.reshape(-1, PACK_COLS), row_mult)


def _unpack(slab, shapes):
    flat = slab.reshape(-1)
    out, at = [], 0
    for s in shapes:
        n = int(np.prod(s))
        out.append(flat[at:at + n].reshape(s))
        at += n
    return out


def _split_for_devices(g, axis):
    r, c = g.shape
    if axis == 1:
        return g.reshape(r, N_DEV, c // N_DEV).transpose(1, 0, 2).reshape(N_DEV, -1)
    return g.reshape(N_DEV, -1)


def _join_shards(gathered, shard_shape, axis):
    r, c = shard_shape
    g = gathered.reshape(N_DEV, r, c)
    if axis == 1:
        return g.transpose(1, 0, 2).reshape(r, N_DEV * c)
    return g.reshape(N_DEV * r, c)


def _s5_direction_inputs(p, sfx, chain_len):
    bt_re = p["ssm_b_re" + sfx][0].transpose(0, 2, 1)
    bt_im = p["ssm_b_im" + sfx][0].transpose(0, 2, 1)
    raw = (p["ssm_a_re" + sfx][0], p["ssm_a_im" + sfx][0], p["ssm_log_dt" + sfx][0][:, None], bt_re, bt_im)
    lr, li, sr, si, bbr, bbi = _disc_fwd(*raw, chain_len,"s5_disc" + sfx)
    lam = jnp.stack([_s5_pack_lam(t) for t in (lr, li, sr, si)], axis=2)
    mats = (_s5_pack_b(bbr), _s5_pack_b(bbi), lam,
            _s5_pack_c(p["ssm_c_re" + sfx][0]), _s5_pack_c(-p["ssm_c_im" + sfx][0]))
    return raw, mats


def kernel(x, ffn1_norm, ffn1_w_gate, ffn1_w_up, ffn1_w_down, mix_norm, w_in, ssm_a_re_fwd, ssm_a_im_fwd, ssm_log_dt_fwd, ssm_b_re_fwd, ssm_b_im_fwd, ssm_c_re_fwd, ssm_c_im_fwd, ssm_a_re_bwd, ssm_a_im_bwd, ssm_log_dt_bwd, ssm_b_re_bwd, ssm_b_im_bwd, ssm_c_re_bwd, ssm_c_im_bwd, ssm_d, ssm_w_glu, ssm_b_glu, att_rpb, w_branch_ssm, w_branch_att, w_out, ffn2_norm, ffn2_w_gate, ffn2_w_up, ffn2_w_down, final_norm, loss_target, m_ffn1_norm, m_ffn1_w_gate, m_ffn1_w_up, m_ffn1_w_down, m_mix_norm, m_w_in, m_ssm_a_re_fwd, m_ssm_a_im_fwd, m_ssm_log_dt_fwd, m_ssm_b_re_fwd, m_ssm_b_im_fwd, m_ssm_c_re_fwd, m_ssm_c_im_fwd, m_ssm_a_re_bwd, m_ssm_a_im_bwd, m_ssm_log_dt_bwd, m_ssm_b_re_bwd, m_ssm_b_im_bwd, m_ssm_c_re_bwd, m_ssm_c_im_bwd, m_ssm_d, m_ssm_w_glu, m_ssm_b_glu, m_att_rpb, m_w_branch_ssm, m_w_branch_att, m_w_out, m_ffn2_norm, m_ffn2_w_gate, m_ffn2_w_up, m_ffn2_w_down, m_final_norm, v_ffn1_norm, v_ffn1_w_gate, v_ffn1_w_up, v_ffn1_w_down, v_mix_norm, v_w_in, v_ssm_a_re_fwd, v_ssm_a_im_fwd, v_ssm_log_dt_fwd, v_ssm_b_re_fwd, v_ssm_b_im_fwd, v_ssm_c_re_fwd, v_ssm_c_im_fwd, v_ssm_a_re_bwd, v_ssm_a_im_bwd, v_ssm_log_dt_bwd, v_ssm_b_re_bwd, v_ssm_b_im_bwd, v_ssm_c_re_bwd, v_ssm_c_im_bwd, v_ssm_d, v_ssm_w_glu, v_ssm_b_glu, v_att_rpb, v_w_branch_ssm, v_w_branch_att, v_w_out, v_ffn2_norm, v_ffn2_w_gate, v_ffn2_w_up, v_ffn2_w_down, v_final_norm):
    p = dict(locals())
    x = p["x"][0]
    target = p["loss_target"][0]
    T, D = x.shape

    shard_shapes = {n: p[n].shape[1:] for n, _ in BIG}
    sizes = [int(np.prod(shard_shapes[n])) for n, _ in BIG]
    w_slab = _pack([p[n][0].astype(BF16) for n, _ in BIG], 16)
    gathered = _all_gather(w_slab).reshape(N_DEV, -1)
    full, at = {}, 0
    for (n, axis), size in zip(BIG, sizes):
        full[n] = _join_shards(gathered[:, at:at + size], shard_shapes[n], axis)
        at += size

    h0 = x
    h1, xn1, g1, u1 = _ffn_fwd(h0, p["ffn1_norm"], full["ffn1_w_gate"], full["ffn1_w_up"], full["ffn1_w_down"],
                               "ffn1_fwd")
    z, zb, un = _mixin_fwd(h1, p["mix_norm"], full["w_in"])
    W = SSM_WIDTH
    zp = _permute_rows(zb[:, :W])
    chain_len = T // SCAN_LANES // S5_NQ
    raw_f, mats_f = _s5_direction_inputs(p, "_fwd", chain_len)
    raw_b, mats_b = _s5_direction_inputs(p, "_bwd", chain_len)
    bre, bim, lam, cre, cimn = [jnp.stack([f, b]) for f, b in zip(mats_f, mats_b)]
    bre, bim, cre, cimn = [t.astype(BF16) for t in (bre, bim, cre, cimn)]
    ypre = _unpermute_rows(_s5_fwd(zp, bre, bim, lam, cre, cimn))
    table = _att_bias_table(p["att_rpb"][0])
    ya = _att_fwd(zb, table)
    tail_w = (p["ssm_d"], full["ssm_w_glu"], p["ssm_b_glu"], full["w_branch_ssm"], full["w_branch_att"], full["w_out"])
    h2 = _merge_fwd(ypre, z, ya, h1, *tail_w)
    h3, xn2, g2, u2 = _ffn_fwd(h2, p["ffn2_norm"], full["ffn2_w_gate"], full["ffn2_w_up"], full["ffn2_w_down"],
                               "ffn2_fwd")
    loss_part, dh3, d_final = _loss_head(h3, p["final_norm"][None], target)

    grads = {"final_norm": d_final[0]}
    dh2, grads["ffn2_norm"], do2, a2, dg2, du2 = _ffn_bwd(
        dh3, h2, p["ffn2_norm"], g2, u2, full["ffn2_w_gate"], full["ffn2_w_up"], full["ffn2_w_down"], "ffn2_bwd")
    grads["ffn2_w_gate"] = _xty(xn2, dg2, "ffn2_dw_gate")
    grads["ffn2_w_up"] = _xty(xn2, du2, "ffn2_dw_up")
    grads["ffn2_w_down"] = _xty(a2, do2, "ffn2_dw_down")
    (dypre, dzs_skip, dgs, dga, dya, grads["ssm_d"], grads["ssm_w_glu"], grads["ssm_b_glu"],
     grads["w_branch_ssm"], grads["w_branch_att"], grads["w_out"]) = _merge_bwd(dh2, ypre, z, ya, *tail_w)
    dq, dk, dv, dtable = _att_bwd(zb, dya, table)
    grads["att_rpb"] = _att_bias_table_t(dtable)
    dyp = _permute_rows(dypre).astype(BF16)
    dzp, dbre, dbim, dlam, dcre, dcimn = _s5_bwd(zp, dyp, bre, bim, lam, cre, cimn)
    G, P = SSM_GROUPS, SSM_STATE
    for d, (sfx, raw) in enumerate((("_fwd", raw_f), ("_bwd", raw_b))):
        da_re, da_im, dldt, dbt_re, dbt_im = _disc_bwd(
            *raw, dlam[d, :, :, 0, :].reshape(G, P), dlam[d, :, :, 1, :].reshape(G, P),
            _s5_unpack_b(dbre[d]), _s5_unpack_b(dbim[d]), "s5_disc_grad" + sfx)
        grads["ssm_a_re" + sfx] = da_re
        grads["ssm_a_im" + sfx] = da_im
        grads["ssm_log_dt" + sfx] = dldt[:, 0]
        grads["ssm_b_re" + sfx] = dbt_re.transpose(0, 2, 1)
        grads["ssm_b_im" + sfx] = dbt_im.transpose(0, 2, 1)
        grads["ssm_c_re" + sfx] = _s5_unpack_c(dcre[d])
        grads["ssm_c_im" + sfx] = -_s5_unpack_c(dcimn[d])
    dzs = _unpermute_rows(dzp) + dzs_skip
    dz = jnp.concatenate([dzs.astype(BF16), dq, dk, dv, dgs, dga], axis=1)
    dh1, grads["mix_norm"] = _mixin_bwd(dz, dh2, h1, p["mix_norm"], full["w_in"])
    grads["w_in"] = _xty(un, dz, "dw_in")
    dh0, grads["ffn1_norm"], do1, a1, dg1, du1 = _ffn_bwd(
        dh1, h0, p["ffn1_norm"], g1, u1, full["ffn1_w_gate"], full["ffn1_w_up"], full["ffn1_w_down"], "ffn1_bwd")
    grads["ffn1_w_gate"] = _xty(xn1, dg1, "ffn1_dw_gate")
    grads["ffn1_w_up"] = _xty(xn1, du1, "ffn1_dw_up")
    grads["ffn1_w_down"] = _xty(a1, do1, "ffn1_dw_down")

    big_rows = w_slab.shape[0]
    big_part = jnp.concatenate([_split_for_devices(grads[n], axis) for n, axis in BIG], axis=1)
    big_part = _pad_rows(big_part.reshape(N_DEV, -1, PACK_COLS), 16).astype(BF16)
    small_shapes = [p[n].shape for n in SMALL]
    small_slab = _pack([grads[n].astype(F32) for n in SMALL], 8)
    small_rows = small_slab.shape[0]
    u16, u32 = jnp.uint16, jnp.uint32
    small_u32 = lax.bitcast_convert_type(small_slab, u32)
    small_u16 = jnp.concatenate([(small_u32 >> 16).astype(u16), (small_u32 & 0xFFFF).astype(u16)], axis=0)
    parts = jnp.concatenate([lax.bitcast_convert_type(big_part, u16),
                             jnp.broadcast_to(small_u16, (N_DEV,) + small_u16.shape)], axis=1)
    got = lax.bitcast_convert_type(_exchange(lax.bitcast_convert_type(parts, BF16)), u16)
    got_big = lax.bitcast_convert_type(got[:, :big_rows], BF16)
    hi = got[:, big_rows:big_rows + small_rows].astype(u32)
    lo = got[:, big_rows + small_rows:].astype(u32)
    got_small = lax.bitcast_convert_type((hi << 16) | lo, F32)

    pack_big = lambda pre: _pack([p[pre + n][0] for n, _ in BIG], 16)
    big_out = _adamw(got_big, pack_big(""), pack_big("m_"), pack_big("v_"), "adamw_shards")
    pack_small = lambda pre: _pack([p[pre + n] for n in SMALL], 8)
    small_out = _adamw(got_small, pack_small(""), pack_small("m_"), pack_small("v_"), "adamw_small")
    results = []
    for b_slab, s_slab in zip(big_out, small_out):
        vals = dict(zip([n for n, _ in BIG], _unpack(b_slab, [p[n].shape for n, _ in BIG])))
        vals.update(zip(SMALL, _unpack(s_slab, small_shapes)))
        results.append(vals)

    loss = lax.psum(loss_part[0, 0], ("x", "y", "c"))
    out = [loss, dh0[None]]
    for vals in results:
        out += [vals[n] for n in WEIGHTS]
    return tuple(out)
```

```python
import functools
import math

import numpy as np
import jax
import jax.numpy as jnp
from jax import lax
from jax.experimental import pallas as pl
from jax.experimental.pallas import tpu as pltpu

F32 = jnp.float32
BF16 = jnp.bfloat16
MESH_ID = pl.DeviceIdType.MESH

SSM_GROUP = 16
SSM_GROUPS = 32
SSM_STATE = 64
SSM_WIDTH = 512
ATT_HEADS = 8
ATT_HEAD_DIM = 64
ATT_WIDTH = 512
GRID_W = 64
WIN_H = 8
WIN_W = 16
EPS = 1e-6
NEG_INF = -1e30
ADAM_LR = 0.001
ADAM_B1 = 0.9
ADAM_B2 = 0.999
ADAM_EPS = 1e-08
ADAM_WD = 0.01
ADAM_STEP = 10

N_DEV = 8
V7X_VMEM_BYTES = 64 * 1024 * 1024
VMEM_LIMIT = V7X_VMEM_BYTES - 8 * 1024 * 1024
SCAN_LANES = 8
ATT_ROWS = 4


def _cparams(sem, vmem=None):
    return pltpu.CompilerParams(dimension_semantics=sem, vmem_limit_bytes=vmem)


def _dot(a, b):
    return jnp.dot(a, b, preferred_element_type=F32)


def _dot_nt(a, b):
    return lax.dot_general(a, b, (((1,), (1,)), ((), ())), preferred_element_type=F32)


def _dot_tn(a, b):
    return lax.dot_general(a, b, (((0,), (0,)), ((), ())), preferred_element_type=F32)


def _rms(h):
    return lax.rsqrt(jnp.mean(h * h, axis=-1, keepdims=True) + EPS)


def _rms_bwd(h, r, v):
    return r * v - h * (r * r * r) * jnp.mean(h * v, axis=-1, keepdims=True)


def _col_sum(x):
    return jnp.sum(x, axis=0, keepdims=True)


def _my_place():
    return lax.axis_index("x"), lax.axis_index("y"), lax.axis_index("c")


def _flat(px, py, pc):
    return 4 * px + 2 * py + pc


class _Comm:
    def __init__(self, kind, arrays):
        self.arrays = list(arrays)
        self.n = len(self.arrays)
        self.kinds = [kind] * self.n if isinstance(kind, str) else list(kind)

    def out_shapes(self):
        return [jax.ShapeDtypeStruct((N_DEV,) + a.shape if k == "gather" else a.shape, a.dtype)
                for k, a in zip(self.kinds, self.arrays)]

    def scratch(self):
        return [pltpu.SemaphoreType.DMA((7 * self.n,)), pltpu.SemaphoreType.DMA((7 * self.n,)),
                pltpu.SemaphoreType.DMA((self.n,))]

    def run(self, srcs, dsts, sems, start):
        send_sems, recv_sems, local_sems = sems
        x, y, c = _my_place()
        mine = _flat(x, y, c)
        for a, (src, dst) in enumerate(zip(srcs, dsts)):
            whole = self.kinds[a] == "gather"
            local = pltpu.make_async_copy(src if whole else src.at[mine], dst.at[mine], local_sems.at[a])
            local.start() if start else local.wait()
            for k in range(1, N_DEV):
                px = 1 - x if k & 4 else x
                py = 1 - y if k & 2 else y
                pc = 1 - c if k & 1 else c
                cp = pltpu.make_async_remote_copy(
                    src_ref=src if whole else src.at[_flat(px, py, pc)], dst_ref=dst.at[mine],
                    send_sem=send_sems.at[7 * a + k - 1], recv_sem=recv_sems.at[7 * a + k - 1],
                    device_id=(px, py, pc), device_id_type=MESH_ID)
                cp.start() if start else cp.wait()


_HBM = pl.BlockSpec(memory_space=pltpu.HBM)


def _comm_call(comm, name):
    def body(*refs):
        srcs, dsts, sems = refs[:comm.n], refs[comm.n:2 * comm.n], refs[2 * comm.n:]
        comm.run(srcs, dsts, sems, True)
        comm.run(srcs, dsts, sems, False)

    return pl.pallas_call(body, name=name, in_specs=[_HBM] * comm.n, out_specs=[_HBM] * comm.n,
                          out_shape=comm.out_shapes(), scratch_shapes=comm.scratch())(*comm.arrays)


def _pallas(core, name, grid, in_specs, out_specs, out_shape, scratch, sem, args, comm=None):
    if comm is None:
        out = pl.pallas_call(core, name=name, grid=grid, in_specs=in_specs, out_specs=out_specs,
                             out_shape=out_shape, scratch_shapes=scratch,
                             compiler_params=_cparams(sem, VMEM_LIMIT))(*args)
        return out, []
    n_in, n_out, n_scr, n = len(in_specs), len(out_specs), len(scratch), comm.n

    def body(*refs):
        ins, srcs = refs[:n_in], refs[n_in:n_in + n]
        outs, dsts = refs[n_in + n:n_in + n + n_out], refs[n_in + n + n_out:n_in + 2 * n + n_out]
        scr, sems = refs[n_in + 2 * n + n_out:n_in + 2 * n + n_out + n_scr], refs[n_in + 2 * n + n_out + n_scr:]
        ids = [pl.program_id(k) for k in range(len(grid))]
        first = functools.reduce(lambda a, b: a & b, [i == 0 for i in ids])
        last = functools.reduce(lambda a, b: a & b, [i == g - 1 for i, g in zip(ids, grid)])

        @pl.when(first)
        def _():
            comm.run(srcs, dsts, sems, True)

        core(*ins, *outs, *scr)

        @pl.when(last)
        def _():
            comm.run(srcs, dsts, sems, False)

    out = pl.pallas_call(
        body, name=name, grid=grid, in_specs=list(in_specs) + [_HBM] * n, out_specs=list(out_specs) + [_HBM] * n,
        out_shape=list(out_shape) + comm.out_shapes(), scratch_shapes=list(scratch) + comm.scratch(),
        compiler_params=_cparams(("arbitrary",) * len(grid), VMEM_LIMIT))(*args, *comm.arrays)
    return out[:n_out], out[n_out:]


def _ffn_tiles(T, F):
    tm = min(T, 1024)
    tf = 256 if F % 256 == 0 else F
    return tm, tf


def _ffn_fwd(h, gain, wg, wu, wd, name, comm=None):
    T, D = h.shape
    F = wg.shape[1]
    tm, tf = _ffn_tiles(T, F)
    nj = F // tf

    def body(h_ref, gain_ref, wg_ref, wu_ref, wd_ref, ho_ref, xn_ref, g_ref, u_ref, acc_ref):
        j = pl.program_id(1)

        @pl.when(j == 0)
        def _():
            hh = h_ref[...]
            xn_ref[...] = (hh * _rms(hh) * gain_ref[...]).astype(BF16)
            acc_ref[...] = jnp.zeros_like(acc_ref)

        xn = xn_ref[...]
        g = _dot(xn, wg_ref[...])
        u = _dot(xn, wu_ref[...])
        g_ref[...] = g.astype(BF16)
        u_ref[...] = u.astype(BF16)
        a = (g * jax.nn.sigmoid(g) * u).astype(BF16)
        acc_ref[...] += _dot(a, wd_ref[...])

        @pl.when(j == nj - 1)
        def _():
            ho_ref[...] = h_ref[...] + 0.5 * acc_ref[...]

    return _pallas(
        body, name, (T // tm, nj),
        [pl.BlockSpec((tm, D), lambda i, j: (i, 0)),
         pl.BlockSpec((1, D), lambda i, j: (0, 0)),
         pl.BlockSpec((D, tf), lambda i, j: (0, j)),
         pl.BlockSpec((D, tf), lambda i, j: (0, j)),
         pl.BlockSpec((tf, D), lambda i, j: (j, 0))],
        [pl.BlockSpec((tm, D), lambda i, j: (i, 0)),
         pl.BlockSpec((tm, D), lambda i, j: (i, 0)),
         pl.BlockSpec((tm, tf), lambda i, j: (i, j)),
         pl.BlockSpec((tm, tf), lambda i, j: (i, j))],
        [jax.ShapeDtypeStruct((T, D), F32), jax.ShapeDtypeStruct((T, D), BF16),
         jax.ShapeDtypeStruct((T, F), BF16), jax.ShapeDtypeStruct((T, F), BF16)],
        [pltpu.VMEM((tm, D), F32)], ("parallel", "arbitrary"), (h, gain, wg, wu, wd), comm)


def _ffn_bwd(dho, h, gain, g, u, wg, wu, wd, name, comm=None):
    T, D = h.shape
    F = wg.shape[1]
    tm, tf = _ffn_tiles(T, F)
    nj = F // tf

    def body(dho_ref, h_ref, gain_ref, g_ref, u_ref, wg_ref, wu_ref, wd_ref,
             dh_ref, dgain_ref, do_ref, a_ref, dg_ref, du_ref, acc_ref):
        i = pl.program_id(0)
        j = pl.program_id(1)

        @pl.when(j == 0)
        def _():
            do_ref[...] = (0.5 * dho_ref[...]).astype(BF16)
            acc_ref[...] = jnp.zeros_like(acc_ref)

        @pl.when((i == 0) & (j == 0))
        def _():
            dgain_ref[...] = jnp.zeros_like(dgain_ref)

        da = _dot_nt(do_ref[...], wd_ref[...])
        gg = g_ref[...].astype(F32)
        uu = u_ref[...].astype(F32)
        s = jax.nn.sigmoid(gg)
        sl = gg * s
        a_ref[...] = (sl * uu).astype(BF16)
        dg = (da * uu * (s * (1.0 + gg * (1.0 - s)))).astype(BF16)
        du = (da * sl).astype(BF16)
        dg_ref[...] = dg
        du_ref[...] = du
        acc_ref[...] += _dot_nt(dg, wg_ref[...]) + _dot_nt(du, wu_ref[...])

        @pl.when(j == nj - 1)
        def _():
            hh = h_ref[...]
            r = _rms(hh)
            dxn = acc_ref[...]
            dgain_ref[...] += _col_sum(dxn * hh * r)
            dh_ref[...] = dho_ref[...] + _rms_bwd(hh, r, dxn * gain_ref[...])

    return _pallas(
        body, name, (T // tm, nj),
        [pl.BlockSpec((tm, D), lambda i, j: (i, 0)),
         pl.BlockSpec((tm, D), lambda i, j: (i, 0)),
         pl.BlockSpec((1, D), lambda i, j: (0, 0)),
         pl.BlockSpec((tm, tf), lambda i, j: (i, j)),
         pl.BlockSpec((tm, tf), lambda i, j: (i, j)),
         pl.BlockSpec((D, tf), lambda i, j: (0, j)),
         pl.BlockSpec((D, tf), lambda i, j: (0, j)),
         pl.BlockSpec((tf, D), lambda i, j: (j, 0))],
        [pl.BlockSpec((tm, D), lambda i, j: (i, 0)),
         pl.BlockSpec((1, D), lambda i, j: (0, 0)),
         pl.BlockSpec((tm, D), lambda i, j: (i, 0)),
         pl.BlockSpec((tm, tf), lambda i, j: (i, j)),
         pl.BlockSpec((tm, tf), lambda i, j: (i, j)),
         pl.BlockSpec((tm, tf), lambda i, j: (i, j))],
        [jax.ShapeDtypeStruct((T, D), F32), jax.ShapeDtypeStruct((1, D), F32),
         jax.ShapeDtypeStruct((T, D), BF16), jax.ShapeDtypeStruct((T, F), BF16),
         jax.ShapeDtypeStruct((T, F), BF16), jax.ShapeDtypeStruct((T, F), BF16)],
        [pltpu.VMEM((tm, D), F32)], ("arbitrary", "arbitrary"), (dho, h, gain, g, u, wg, wu, wd), comm)


def _xty(x, y, name):
    T, K = x.shape
    N = y.shape[1]
    tt = min(T, 1024)
    tk = K if K <= 1024 else (1408 if K % 1408 == 0 else K)
    tn = N if N <= 1024 else (1408 if N % 1408 == 0 else (1024 if N % 1024 == 0 else N))
    nt = T // tt

    def body(x_ref, y_ref, o_ref):
        t = pl.program_id(2)

        @pl.when(t == 0)
        def _():
            o_ref[...] = jnp.zeros_like(o_ref)

        o_ref[...] += _dot_tn(x_ref[...], y_ref[...])

    return pl.pallas_call(
        body, name=name, grid=(K // tk, N // tn, nt),
        in_specs=[pl.BlockSpec((tt, tk), lambda k, n, t: (t, k)),
                  pl.BlockSpec((tt, tn), lambda k, n, t: (t, n))],
        out_specs=pl.BlockSpec((tk, tn), lambda k, n, t: (k, n)),
        out_shape=jax.ShapeDtypeStruct((K, N), F32),
        compiler_params=_cparams(("parallel", "parallel", "arbitrary"), VMEM_LIMIT),
    )(x, y)


def _mixin_fwd(h, gain, w_in):
    T, D = h.shape
    nn, _, tn = w_in.shape
    N = nn * tn
    tm = min(T, 1024)

    def body(h_ref, gain_ref, w_ref, z_ref, zb_ref, un_ref):
        @pl.when(pl.program_id(1) == 0)
        def _():
            hh = h_ref[...]
            un_ref[...] = (hh * _rms(hh) * gain_ref[...]).astype(BF16)

        z = _dot(un_ref[...], w_ref[...])
        z_ref[...] = z
        zb_ref[...] = z.astype(BF16)

    return pl.pallas_call(
        body, name="mixin_fwd", grid=(T // tm, nn),
        in_specs=[pl.BlockSpec((tm, D), lambda i, n: (i, 0)),
                  pl.BlockSpec((1, D), lambda i, n: (0, 0)),
                  pl.BlockSpec((None, D, tn), lambda i, n: (n, 0, 0))],
        out_specs=[pl.BlockSpec((tm, tn), lambda i, n: (i, n)),
                   pl.BlockSpec((tm, tn), lambda i, n: (i, n)),
                   pl.BlockSpec((tm, D), lambda i, n: (i, 0))],
        out_shape=[jax.ShapeDtypeStruct((T, N), F32), jax.ShapeDtypeStruct((T, N), BF16),
                   jax.ShapeDtypeStruct((T, D), BF16)],
        compiler_params=_cparams(("parallel", "arbitrary"), VMEM_LIMIT),
    )(h, gain, w_in)


def _mixin_bwd(dz, dh_res, h, gain, w_in):
    T, D = h.shape
    nn, _, tn = w_in.shape
    tm = min(T, 1024)

    def body(dz_ref, dres_ref, h_ref, gain_ref, w_ref, dh_ref, dgain_ref, acc_ref):
        i = pl.program_id(0)
        n = pl.program_id(1)

        @pl.when(n == 0)
        def _():
            acc_ref[...] = jnp.zeros_like(acc_ref)

        @pl.when((i == 0) & (n == 0))
        def _():
            dgain_ref[...] = jnp.zeros_like(dgain_ref)

        acc_ref[...] += _dot_nt(dz_ref[...], w_ref[...])

        @pl.when(n == nn - 1)
        def _():
            hh = h_ref[...]
            r = _rms(hh)
            dun = acc_ref[...]
            dgain_ref[...] += _col_sum(dun * hh * r)
            dh_ref[...] = dres_ref[...] + _rms_bwd(hh, r, dun * gain_ref[...])

    return pl.pallas_call(
        body, name="mixin_bwd", grid=(T // tm, nn),
        in_specs=[pl.BlockSpec((tm, tn), lambda i, n: (i, n)),
                  pl.BlockSpec((tm, D), lambda i, n: (i, 0)),
                  pl.BlockSpec((tm, D), lambda i, n: (i, 0)),
                  pl.BlockSpec((1, D), lambda i, n: (0, 0)),
                  pl.BlockSpec((None, D, tn), lambda i, n: (n, 0, 0))],
        out_specs=[pl.BlockSpec((tm, D), lambda i, n: (i, 0)),
                   pl.BlockSpec((1, D), lambda i, n: (0, 0))],
        out_shape=[jax.ShapeDtypeStruct((T, D), F32), jax.ShapeDtypeStruct((1, D), F32)],
        scratch_shapes=[pltpu.VMEM((tm, D), F32)],
        compiler_params=_cparams(("arbitrary", "arbitrary"), VMEM_LIMIT),
    )(dz, dh_res, h, gain, w_in)


def _loss_head(h, gain, target):
    T, D = h.shape
    tm = min(T, 1024)

    def body(h_ref, gain_ref, t_ref, loss_ref, dh_ref, dgain_ref):
        @pl.when(pl.program_id(0) == 0)
        def _():
            loss_ref[...] = jnp.zeros_like(loss_ref)
            dgain_ref[...] = jnp.zeros_like(dgain_ref)

        hh = h_ref[...]
        r = _rms(hh)
        e = hh * r * gain_ref[...] - t_ref[...]
        loss_ref[...] += (0.5 / D) * jnp.sum(e * e)
        dy = e * (1.0 / D)
        dgain_ref[...] += _col_sum(dy * hh * r)
        dh_ref[...] = _rms_bwd(hh, r, dy * gain_ref[...])

    return pl.pallas_call(
        body, name="loss_head", grid=(T // tm,),
        in_specs=[pl.BlockSpec((tm, D), lambda i: (i, 0)),
                  pl.BlockSpec((1, D), lambda i: (0, 0)),
                  pl.BlockSpec((tm, D), lambda i: (i, 0))],
        out_specs=[pl.BlockSpec((1, 128), lambda i: (0, 0)),
                   pl.BlockSpec((tm, D), lambda i: (i, 0)),
                   pl.BlockSpec((1, D), lambda i: (0, 0))],
        out_shape=[jax.ShapeDtypeStruct((1, 128), F32), jax.ShapeDtypeStruct((T, D), F32),
                   jax.ShapeDtypeStruct((1, D), F32)],
        compiler_params=_cparams(("arbitrary",), VMEM_LIMIT),
    )(h, gain, target)


def _adamw(parts, w, m, v, name):
    R, C = w.shape
    mult = 16 if parts.dtype == BF16 else 8
    tr = max(t for t in range(mult, min(R, 512) + 1, mult) if R % t == 0)
    c1 = 1.0 - ADAM_B1 ** ADAM_STEP
    c2 = 1.0 - ADAM_B2 ** ADAM_STEP

    def body(p_ref, w_ref, m_ref, v_ref, g_ref, d_ref, nm_ref, nv_ref):
        g = p_ref[0].astype(F32)
        for k in range(1, N_DEV):
            g = g + p_ref[k].astype(F32)
        mm = ADAM_B1 * m_ref[...] + (1.0 - ADAM_B1) * g
        vv = ADAM_B2 * v_ref[...] + (1.0 - ADAM_B2) * (g * g)
        g_ref[...] = g
        nm_ref[...] = mm
        nv_ref[...] = vv
        d_ref[...] = -ADAM_LR * ((mm / c1) / (jnp.sqrt(vv / c2) + ADAM_EPS) + ADAM_WD * w_ref[...])

    spec = pl.BlockSpec((tr, C), lambda i: (i, 0))
    return pl.pallas_call(
        body, name=name, grid=(R // tr,),
        in_specs=[pl.BlockSpec((N_DEV, tr, C), lambda i: (0, i, 0)), spec, spec, spec],
        out_specs=[spec, spec, spec, spec],
        out_shape=[jax.ShapeDtypeStruct((R, C), F32)] * 4,
        compiler_params=_cparams(("parallel",), VMEM_LIMIT),
    )(parts, w, m, v)


S5_NS = 256
S5_NH = 2
S5_NCB = 4
S5_RC = 512
S5_NQ = 4


def _disc_math(a_re, a_im, log_dt, bt_re, bt_im):
    dt = jnp.exp(log_dt)
    zr, zi = a_re * dt, a_im * dt
    mag = jnp.exp(zr)
    lb_re, lb_im = mag * jnp.cos(zi), mag * jnp.sin(zi)
    den = a_re * a_re + a_im * a_im
    nr, ni = lb_re - 1.0, lb_im
    f_re = (nr * a_re + ni * a_im) / den
    f_im = (ni * a_re - nr * a_im) / den
    bb_re = f_re[:, None, :] * bt_re - f_im[:, None, :] * bt_im
    bb_im = f_re[:, None, :] * bt_im + f_im[:, None, :] * bt_re
    return lb_re, lb_im, bb_re, bb_im


def _disc_fwd(a_re, a_im, log_dt, bt_re, bt_im, chain_len, name):
    G, P = a_re.shape
    C = bt_re.shape[1]
    n_sq = int(round(math.log2(chain_len)))
    assert 2 ** n_sq == chain_len

    def body(a_re_ref, a_im_ref, ldt_ref, br_ref, bi_ref, lr_ref, li_ref, sr_ref, si_ref, bbr_ref, bbi_ref):
        lr, li, bbr, bbi = _disc_math(a_re_ref[...], a_im_ref[...], ldt_ref[...], br_ref[...], bi_ref[...])
        lr_ref[...] = lr
        li_ref[...] = li
        bbr_ref[...] = bbr
        bbi_ref[...] = bbi
        pr, pi = lr, li
        for _ in range(n_sq):
            pr, pi = pr * pr - pi * pi, 2.0 * pr * pi
        sr_ref[...] = pr
        si_ref[...] = pi

    s2 = jax.ShapeDtypeStruct((G, P), F32)
    s3 = jax.ShapeDtypeStruct((G, C, P), F32)
    return pl.pallas_call(body, name=name, out_shape=[s2, s2, s2, s2, s3, s3])(a_re, a_im, log_dt, bt_re, bt_im)


def _disc_bwd(a_re, a_im, log_dt, bt_re, bt_im, d_lr, d_li, d_bbr, d_bbi, name):
    G, P = a_re.shape
    C = bt_re.shape[1]

    def body(a_re_ref, a_im_ref, ldt_ref, br_ref, bi_ref, c1, c2, c3, c4, o1, o2, o3, o4, o5):
        _, vjp = jax.vjp(_disc_math, a_re_ref[...], a_im_ref[...], ldt_ref[...], br_ref[...], bi_ref[...])
        o1[...], o2[...], o3[...], o4[...], o5[...] = vjp((c1[...], c2[...], c3[...], c4[...]))

    s2 = jax.ShapeDtypeStruct((G, P), F32)
    s3 = jax.ShapeDtypeStruct((G, C, P), F32)
    return pl.pallas_call(body, name=name, out_shape=[s2, s2, jax.ShapeDtypeStruct((G, 1), F32), s3, s3])(
        a_re, a_im, log_dt, bt_re, bt_im, d_lr, d_li, d_bbr, d_bbi)


def _row_block(ib):
    return pl.ds(pl.multiple_of(ib * SCAN_LANES, SCAN_LANES), SCAN_LANES)


def _chain_block(j, i, ascending, n_blocks):
    at = j * (n_blocks // S5_NQ) + i
    return _row_block(jnp.where(ascending, at, n_blocks - 1 - at))


def _cmul_add(lr, li, sr, si, xr, xi):
    return lr * sr - li * si + xr, lr * si + li * sr + xi


def _scan(xr_ref, xi_ref, lr, li, init, ascending, n_blocks, store):
    def step(i, carry):
        out = []
        for j, (sr, si) in enumerate(carry):
            rows = _chain_block(j, i, ascending, n_blocks)
            nr, ni = _cmul_add(lr, li, sr, si, xr_ref[rows, :], xi_ref[rows, :])
            if store:
                xr_ref[rows, :] = nr
                xi_ref[rows, :] = ni
            out.append((nr, ni))
        return tuple(out)

    return lax.fori_loop(0, n_blocks // S5_NQ, step, init)


def _segment_starts(w, lsr, lsi, ascending):
    shape = w[0][0].shape
    row = lax.broadcasted_iota(jnp.int32, shape, 0)
    keep = row != jnp.where(ascending, 0, SCAN_LANES - 1)

    def shift(t):
        t = jnp.where(ascending, pltpu.roll(t, 1, 0), pltpu.roll(t, SCAN_LANES - 1, 0))
        return jnp.where(keep, t, 0.0)

    zero = jnp.zeros(shape, F32)
    c = [(zero, zero)] * S5_NQ
    for _ in range(SCAN_LANES):
        tr, ti = _cmul_add(lsr, lsi, *c[-1], *w[-1])
        c[0] = (shift(tr), shift(ti))
        for j in range(1, S5_NQ):
            c[j] = _cmul_add(lsr, lsi, *c[j - 1], *w[j - 1])
    return tuple(c)


def _first_pass(xr_ref, xi_ref, lam_ref, ascending, n_blocks, conj):
    shape = (SCAN_LANES, xr_ref.shape[1])
    sign = -1.0 if conj else 1.0
    lr = jnp.broadcast_to(lam_ref[0:1, :], shape)
    li = sign * jnp.broadcast_to(lam_ref[1:2, :], shape)
    lsr = jnp.broadcast_to(lam_ref[2:3, :], shape)
    lsi = sign * jnp.broadcast_to(lam_ref[3:4, :], shape)
    zero = jnp.zeros(shape, F32)
    w = _scan(xr_ref, xi_ref, lr, li, ((zero, zero),) * S5_NQ, ascending, n_blocks, store=False)
    return _segment_starts(w, lsr, lsi, ascending), lr, li


def _s5_specs(T):
    NS = S5_NS
    tok = pl.BlockSpec((T, 128), lambda c, d, h: (0, c))
    b_spec = pl.BlockSpec((None, None, None, 128, NS), lambda c, d, h: (d, c, h, 0, 0))
    c_spec = pl.BlockSpec((None, None, None, NS, 128), lambda c, d, h: (d, c, h, 0, 0))
    lam_spec = pl.BlockSpec((None, None, None, 4, NS), lambda c, d, h: (d, c, h, 0, 0))
    return tok, b_spec, c_spec, lam_spec


def _s5_fwd(zp, bre, bim, lam, cre, cimn, comm=None):
    T = zp.shape[0]
    NS = S5_NS
    nb = T // SCAN_LANES
    rc = min(S5_RC, T)
    tok, b_spec, c_spec, lam_spec = _s5_specs(T)

    def body(zp_ref, bre_ref, bim_ref, lam_ref, cre_ref, cim_ref, y_ref, xr_ref, xi_ref):
        d = pl.program_id(1)
        ascending = d == 0

        @pl.when((d == 0) & (pl.program_id(2) == 0))
        def _():
            y_ref[...] = jnp.zeros_like(y_ref)

        def proj(c, _):
            rows = pl.ds(pl.multiple_of(c * rc, rc), rc)
            zz = zp_ref[rows, :]
            xr_ref[rows, :] = _dot(zz, bre_ref[...])
            xi_ref[rows, :] = _dot(zz, bim_ref[...])
            return 0

        lax.fori_loop(0, T // rc, proj, 0)
        starts, lr, li = _first_pass(xr_ref, xi_ref, lam_ref, ascending, nb, conj=False)
        _scan(xr_ref, xi_ref, lr, li, starts, ascending, nb, store=True)

        def outp(c, _):
            rows = pl.ds(pl.multiple_of(c * rc, rc), rc)
            y_ref[rows, :] += (_dot(xr_ref[rows, :].astype(BF16), cre_ref[...])
                               + _dot(xi_ref[rows, :].astype(BF16), cim_ref[...]))
            return 0

        lax.fori_loop(0, T // rc, outp, 0)

    return _pallas(
        body, "s5_fwd", (S5_NCB, 2, S5_NH),
        [tok, b_spec, b_spec, lam_spec, c_spec, c_spec], [tok],
        [jax.ShapeDtypeStruct((T, SSM_WIDTH), F32)],
        [pltpu.VMEM((T, NS), F32), pltpu.VMEM((T, NS), F32)],
        ("parallel", "arbitrary", "arbitrary"), (zp, bre, bim, lam, cre, cimn), comm)


def _s5_bwd(zp, dyp, bre, bim, lam, cre, cimn, comm=None):
    T = zp.shape[0]
    NS, NH = S5_NS, S5_NH
    nb = T // SCAN_LANES
    rc = min(S5_RC, T)
    tok, b_spec, c_spec, lam_spec = _s5_specs(T)
    dlam_spec = pl.BlockSpec((None, None, None, 2, NS), lambda c, d, h: (d, c, h, 0, 0))

    def body(zp_ref, dyp_ref, bre_ref, bim_ref, lam_ref, cre_ref, cim_ref,
             dzp_ref, dbre_ref, dbim_ref, dlam_ref, dcre_ref, dcim_ref,
             sr_ref, si_ref, gr_ref, gi_ref):
        d = pl.program_id(1)
        ascending = d == 0
        g_ascending = d != 0

        @pl.when((d == 0) & (pl.program_id(2) == 0))
        def _():
            dzp_ref[...] = jnp.zeros_like(dzp_ref)

        dcre_ref[...] = jnp.zeros_like(dcre_ref)
        dcim_ref[...] = jnp.zeros_like(dcim_ref)
        dbre_ref[...] = jnp.zeros_like(dbre_ref)
        dbim_ref[...] = jnp.zeros_like(dbim_ref)

        def proj(c, _):
            rows = pl.ds(pl.multiple_of(c * rc, rc), rc)
            zz = zp_ref[rows, :]
            sr_ref[rows, :] = _dot(zz, bre_ref[...])
            si_ref[rows, :] = _dot(zz, bim_ref[...])
            dy = dyp_ref[rows, :]
            gr_ref[rows, :] = _dot_nt(dy, cre_ref[...])
            gi_ref[rows, :] = _dot_nt(dy, cim_ref[...])
            return 0

        lax.fori_loop(0, T // rc, proj, 0)
        s_starts, lr, li = _first_pass(sr_ref, si_ref, lam_ref, ascending, nb, conj=False)
        _scan(sr_ref, si_ref, lr, li, s_starts, ascending, nb, store=True)
        g_starts, lr, lic = _first_pass(gr_ref, gi_ref, lam_ref, g_ascending, nb, conj=True)

        def gstep(i, carry, last):
            g, (ar, ai) = carry
            out = []
            for j, (g_r, g_i) in enumerate(g):
                rows = _chain_block(j, i, g_ascending, nb)
                n_r, n_i = _cmul_add(lr, lic, g_r, g_i, gr_ref[rows, :], gi_ref[rows, :])
                gr_ref[rows, :] = n_r
                gi_ref[rows, :] = n_i
                if last:
                    s_r, s_i = s_starts[S5_NQ - 1 - j]
                else:
                    prev = _chain_block(j, i + 1, g_ascending, nb)
                    s_r, s_i = sr_ref[prev, :], si_ref[prev, :]
                ar = ar + n_r * s_r + n_i * s_i
                ai = ai + n_i * s_r - n_r * s_i
                out.append((n_r, n_i))
            return tuple(out), (ar, ai)

        zero = jnp.zeros((SCAN_LANES, NS), F32)
        steps = nb // S5_NQ
        carry = lax.fori_loop(0, steps - 1, lambda i, c: gstep(i, c, False), (g_starts, (zero, zero)))
        _, (ar, ai) = gstep(steps - 1, carry, True)
        dlam_ref[0:1, :] = _col_sum(ar)
        dlam_ref[1:2, :] = _col_sum(ai)

        def grads(c, _):
            rows = pl.ds(pl.multiple_of(c * rc, rc), rc)
            zz = zp_ref[rows, :]
            dy = dyp_ref[rows, :]
            g_rb = gr_ref[rows, :].astype(BF16)
            g_ib = gi_ref[rows, :].astype(BF16)
            dcre_ref[...] += _dot_tn(sr_ref[rows, :].astype(BF16), dy)
            dcim_ref[...] += _dot_tn(si_ref[rows, :].astype(BF16), dy)
            dbre_ref[...] += _dot_tn(zz, g_rb)
            dbim_ref[...] += _dot_tn(zz, g_ib)
            dzp_ref[rows, :] += _dot_nt(g_rb, bre_ref[...]) + _dot_nt(g_ib, bim_ref[...])
            return 0

        lax.fori_loop(0, T // rc, grads, 0)

    f32 = lambda *s: jax.ShapeDtypeStruct(s, F32)
    return _pallas(
        body, "s5_bwd", (S5_NCB, 2, S5_NH),
        [tok, tok, b_spec, b_spec, lam_spec, c_spec, c_spec],
        [tok, b_spec, b_spec, dlam_spec, c_spec, c_spec],
        [f32(T, SSM_WIDTH), f32(2, S5_NCB, NH, 128, NS), f32(2, S5_NCB, NH, 128, NS),
         f32(2, S5_NCB, NH, 2, NS), f32(2, S5_NCB, NH, NS, 128), f32(2, S5_NCB, NH, NS, 128)],
        [pltpu.VMEM((T, NS), F32)] * 4,
        ("parallel", "arbitrary", "arbitrary"), (zp, dyp, bre, bim, lam, cre, cimn), comm)


def _s5_delta():
    d = np.zeros((S5_NH, 8, 8 // S5_NH), np.float32)
    for h in range(S5_NH):
        for go in range(8 // S5_NH):
            d[h, h * (8 // S5_NH) + go, go] = 1.0
    return d


def _s5_pack_b(bbt):
    gh = 8 // S5_NH
    b5 = bbt.reshape(S5_NCB, S5_NH, gh, SSM_GROUP, SSM_STATE).transpose(0, 1, 3, 2, 4)
    m = b5[:, :, None] * _s5_delta()[None, :, :, None, :, None]
    return m.reshape(S5_NCB, S5_NH, 128, S5_NS)


def _s5_unpack_b(dm):
    gh = 8 // S5_NH
    d6 = dm.reshape(S5_NCB, S5_NH, 8, SSM_GROUP, gh, SSM_STATE)
    b5 = jnp.sum(d6 * _s5_delta()[None, :, :, None, :, None], axis=2)
    return b5.transpose(0, 1, 3, 2, 4).reshape(SSM_GROUPS, SSM_GROUP, SSM_STATE)


def _s5_pack_c(c):
    gh = 8 // S5_NH
    c5 = c.reshape(S5_NCB, S5_NH, gh, SSM_GROUP, SSM_STATE).transpose(0, 1, 2, 4, 3)
    m = c5[:, :, :, :, None, :] * _s5_delta().transpose(0, 2, 1)[None, :, :, None, :, None]
    return m.reshape(S5_NCB, S5_NH, S5_NS, 128)


def _s5_unpack_c(dm):
    gh = 8 // S5_NH
    d6 = dm.reshape(S5_NCB, S5_NH, gh, SSM_STATE, 8, SSM_GROUP)
    c5 = jnp.sum(d6 * _s5_delta().transpose(0, 2, 1)[None, :, :, None, :, None], axis=4)
    return c5.transpose(0, 1, 2, 4, 3).reshape(SSM_GROUPS, SSM_GROUP, SSM_STATE)


def _s5_pack_lam(x):
    return x.reshape(S5_NCB, S5_NH, S5_NS)


def _permute_rows(x):
    T = x.shape[0]
    return x.reshape(SCAN_LANES, T // SCAN_LANES, -1).transpose(1, 0, 2).reshape(T, -1)


def _unpermute_rows(x):
    T = x.shape[0]
    return x.reshape(T // SCAN_LANES, SCAN_LANES, -1).transpose(1, 0, 2).reshape(T, -1)


ATT_TB = ATT_ROWS * GRID_W
ATT_KB = 3 * ATT_TB


def _att_valid(i, n_rows):
    qi = lax.broadcasted_iota(jnp.int32, (ATT_TB, ATT_KB), 0)
    kj = lax.broadcasted_iota(jnp.int32, (ATT_TB, ATT_KB), 1)
    r = i * ATT_ROWS + qi // GRID_W
    c = qi % GRID_W
    rk = (i - 1) * ATT_ROWS + kj // GRID_W
    x = kj % GRID_W
    rs = jnp.clip(r - WIN_H // 2, 0, n_rows - WIN_H)
    cs = jnp.clip(c - WIN_W // 2, 0, GRID_W - WIN_W)
    return (rk >= rs) & (rk < rs + WIN_H) & (x >= cs) & (x < cs + WIN_W)


def _att_probs(qh, kh, bias, valid):
    s = jnp.where(valid, _dot_nt(qh, kh) + bias, NEG_INF)
    p = jnp.exp(s - jnp.max(s, axis=1, keepdims=True))
    return p / jnp.sum(p, axis=1, keepdims=True)


def _att_specs(n, col):
    last = n - 1
    cur = lambda i: (jnp.minimum(i, last), col)
    prv = lambda i: (jnp.maximum(jnp.minimum(i, last) - 1, 0), col)
    nxt = lambda i: (jnp.minimum(i + 1, last), col)
    blk = lambda f: pl.BlockSpec((ATT_TB, ATT_WIDTH), f)
    return blk(cur), blk(prv), blk(nxt)


def _att_fwd(zb, biasv):
    T = zb.shape[0]
    W = ATT_WIDTH
    n = T // ATT_TB
    n_rows = T // GRID_W
    cur = _att_specs(n, 0)[0]
    q_cur = _att_specs(n, 1)[0]
    k_cur, k_prv, k_nxt = _att_specs(n, 2)
    v_cur, v_prv, v_nxt = _att_specs(n, 3)

    def body(q_ref, kp_ref, kc_ref, kn_ref, vp_ref, vc_ref, vn_ref, b_ref, y_ref):
        valid = _att_valid(pl.program_id(0), n_rows)
        qs = q_ref[...] * 0.125
        kb = jnp.concatenate([kp_ref[...], kc_ref[...], kn_ref[...]], axis=0)
        vb = jnp.concatenate([vp_ref[...], vc_ref[...], vn_ref[...]], axis=0)
        outs = []
        for h in range(ATT_HEADS):
            hs = slice(h * ATT_HEAD_DIM, (h + 1) * ATT_HEAD_DIM)
            p = _att_probs(qs[:, hs], kb[:, hs], b_ref[h], valid)
            outs.append(_dot(p.astype(BF16), vb[:, hs]))
        y_ref[...] = jnp.concatenate(outs, axis=1).astype(BF16)

    return pl.pallas_call(
        body, name="att_fwd", grid=(n,),
        in_specs=[q_cur, k_prv, k_cur, k_nxt, v_prv, v_cur, v_nxt,
                  pl.BlockSpec((ATT_HEADS, ATT_TB, ATT_KB), lambda i: (0, 0, 0))],
        out_specs=cur,
        out_shape=jax.ShapeDtypeStruct((T, W), BF16),
        compiler_params=_cparams(("parallel",), VMEM_LIMIT),
    )(zb, zb, zb, zb, zb, zb, zb, biasv)


def _att_bwd(zb, do, biasv, comm=None):
    T = zb.shape[0]
    W = ATT_WIDTH
    n = T // ATT_TB
    n_rows = T // GRID_W
    cur = _att_specs(n, 0)[0]
    q_cur = _att_specs(n, 1)[0]
    k_cur, k_prv, k_nxt = _att_specs(n, 2)
    v_cur, v_prv, v_nxt = _att_specs(n, 3)
    done = pl.BlockSpec((ATT_TB, W), lambda i: (jnp.maximum(i - 1, 0), 0))
    bias_spec = pl.BlockSpec((ATT_HEADS, ATT_TB, ATT_KB), lambda i: (0, 0, 0))

    def body(q_ref, do_ref, kp_ref, kc_ref, kn_ref, vp_ref, vc_ref, vn_ref, b_ref,
             dq_ref, dk_ref, dv_ref, db_ref, acck_ref, accv_ref):
        i = pl.program_id(0)

        @pl.when(i == 0)
        def _():
            db_ref[...] = jnp.zeros_like(db_ref)
            acck_ref[...] = jnp.zeros_like(acck_ref)
            accv_ref[...] = jnp.zeros_like(accv_ref)

        @pl.when((i > 0) & (i < n))
        def _():
            slot = lax.rem(i + 1, 3)
            acck_ref[slot] = jnp.zeros((ATT_TB, W), F32)
            accv_ref[slot] = jnp.zeros((ATT_TB, W), F32)

        @pl.when(i < n)
        def _():
            valid = _att_valid(i, n_rows)
            qs = q_ref[...] * 0.125
            dob = do_ref[...]
            kb = jnp.concatenate([kp_ref[...], kc_ref[...], kn_ref[...]], axis=0)
            vb = jnp.concatenate([vp_ref[...], vc_ref[...], vn_ref[...]], axis=0)
            dqs, dks, dvs = [], [], []
            for h in range(ATT_HEADS):
                hs = slice(h * ATT_HEAD_DIM, (h + 1) * ATT_HEAD_DIM)
                qh, kh, vh, doh = qs[:, hs], kb[:, hs], vb[:, hs], dob[:, hs]
                p = _att_probs(qh, kh, b_ref[h], valid)
                dp = _dot_nt(doh, vh)
                ds = p * (dp - jnp.sum(p * dp, axis=1, keepdims=True))
                db_ref[h] += ds
                dsb = ds.astype(BF16)
                dqs.append(_dot(dsb, kh) * 0.125)
                dks.append(_dot_tn(dsb, qh))
                dvs.append(_dot_tn(p.astype(BF16), doh))
            dq_ref[...] = jnp.concatenate(dqs, axis=1).astype(BF16)
            dk_all = jnp.concatenate(dks, axis=1)
            dv_all = jnp.concatenate(dvs, axis=1)
            for b in range(3):
                slot = lax.rem(i + 2 + b, 3)
                rows = slice(b * ATT_TB, (b + 1) * ATT_TB)
                acck_ref[slot] += dk_all[rows]
                accv_ref[slot] += dv_all[rows]

        slot = lax.rem(i + 2, 3)
        dk_ref[...] = acck_ref[slot].astype(BF16)
        dv_ref[...] = accv_ref[slot].astype(BF16)

    return _pallas(
        body, "att_bwd", (n + 1,),
        [q_cur, cur, k_prv, k_cur, k_nxt, v_prv, v_cur, v_nxt, bias_spec],
        [cur, done, done, bias_spec],
        [jax.ShapeDtypeStruct((T, W), BF16)] * 3 + [jax.ShapeDtypeStruct((ATT_HEADS, ATT_TB, ATT_KB), F32)],
        [pltpu.VMEM((3, ATT_TB, W), F32), pltpu.VMEM((3, ATT_TB, W), F32)],
        ("arbitrary",), (zb, do, zb, zb, zb, zb, zb, zb, biasv), comm)


def _att_selectors():
    rsel = np.zeros((ATT_ROWS, 3 * ATT_ROWS, 2 * WIN_H - 1), np.float32)
    for a in range(ATT_ROWS):
        for b in range(3 * ATT_ROWS):
            rsel[a, b, b - a - ATT_ROWS + WIN_H - 1] = 1.0
    csel = np.zeros((GRID_W, GRID_W, 2 * WIN_W - 1), np.float32)
    for c in range(GRID_W):
        for x in range(GRID_W):
            csel[c, x, min(max(x - c, -(WIN_W - 1)), WIN_W - 1) + WIN_W - 1] = 1.0
    return rsel, csel


def _att_bias_table(rpb):
    rsel, csel = _att_selectors()
    hi = lax.Precision.HIGHEST
    t = jnp.einsum('hrd,abr->habd', rpb, rsel, precision=hi)
    t = jnp.einsum('habd,cxd->hacbx', t, csel, precision=hi)
    return t.reshape(ATT_HEADS, ATT_TB, ATT_KB)


def _att_bias_table_t(dtable):
    rsel, csel = _att_selectors()
    hi = lax.Precision.HIGHEST
    t = dtable.reshape(ATT_HEADS, ATT_ROWS, GRID_W, 3 * ATT_ROWS, GRID_W)
    t = jnp.einsum('hacbx,cxd->habd', t, csel, precision=hi)
    return jnp.einsum('habd,abr->hrd', t, rsel, precision=hi)


GELU_K = math.sqrt(2.0 / math.pi)
GELU_C = 0.044715
MERGE_TM = 256


def _gelu(x):
    return 0.5 * x * (1.0 + jnp.tanh(GELU_K * (x + GELU_C * x * x * x)))


def _gelu_grad(x):
    t = jnp.tanh(GELU_K * (x + GELU_C * x * x * x))
    return 0.5 * (1.0 + t) + 0.5 * x * (1.0 - t * t) * GELU_K * (1.0 + 3.0 * GELU_C * x * x)


def _merge_forward(ypre, zs, gs, ga, ya, ssm_d, w_glu, b_glu, w_bs, w_ba):
    ys = ypre + ssm_d * zs
    yg = _gelu(ys)
    sg = jax.nn.sigmoid(_dot(yg.astype(BF16), w_glu) + b_glu)
    y2 = yg * sg
    bs = _dot(y2.astype(BF16), w_bs)
    ba = _dot(ya, w_ba)
    s1 = jax.nn.sigmoid(gs)
    s2 = jax.nn.sigmoid(ga)
    merged = s1 * bs + s2 * ba
    return ys, yg, sg, y2, bs, ba, s1, s2, merged


def _merge_in_specs(D, W, tm):
    tok = lambda w, c: pl.BlockSpec((tm, w), lambda i: (i, c))
    full = lambda r, c: pl.BlockSpec((r, c), lambda i: (0, 0))
    z_specs = [tok(W, 0), tok(D, 4 * W // D), tok(D, 4 * W // D + 1)]
    w_specs = [full(1, W), full(W, W), full(1, W), full(W, D), full(W, D), full(D, D)]
    return tok, z_specs, w_specs


def _merge_fwd(ypre, z, ya, h1, ssm_d, w_glu, b_glu, w_bs, w_ba, w_out):
    T, D = h1.shape
    W = ypre.shape[1]
    tm = min(T, MERGE_TM)
    tok, z_specs, w_specs = _merge_in_specs(D, W, tm)

    def body(ypre_ref, zs_ref, gs_ref, ga_ref, ya_ref, h1_ref, d_ref, wglu_ref, bglu_ref, wbs_ref, wba_ref, wout_ref,
             h2_ref):
        merged = _merge_forward(ypre_ref[...], zs_ref[...], gs_ref[...], ga_ref[...], ya_ref[...], d_ref[...],
                                wglu_ref[...], bglu_ref[...], wbs_ref[...], wba_ref[...])[-1]
        h2_ref[...] = h1_ref[...] + _dot(merged.astype(BF16), wout_ref[...])

    return pl.pallas_call(
        body, name="merge_fwd", grid=(T // tm,),
        in_specs=[tok(W, 0)] + z_specs + [tok(W, 0), tok(D, 0)] + w_specs,
        out_specs=tok(D, 0),
        out_shape=jax.ShapeDtypeStruct((T, D), F32),
        compiler_params=_cparams(("parallel",), VMEM_LIMIT),
    )(ypre, z, z, z, ya, h1, ssm_d, w_glu, b_glu, w_bs, w_ba, w_out)


def _merge_bwd(dh2, ypre, z, ya, ssm_d, w_glu, b_glu, w_bs, w_ba, w_out):
    T, D = dh2.shape
    W = ypre.shape[1]
    tm = min(T, MERGE_TM)
    tok, z_specs, w_specs = _merge_in_specs(D, W, tm)

    def body(dh2_ref, ypre_ref, zs_ref, gs_ref, ga_ref, ya_ref, d_ref, wglu_ref, bglu_ref, wbs_ref, wba_ref, wout_ref,
             dypre_ref, dzs_ref, dgs_ref, dga_ref, dya_ref, dd_ref, dwglu_ref, dbglu_ref, dwbs_ref, dwba_ref, dwout_ref):
        @pl.when(pl.program_id(0) == 0)
        def _():
            for r in (dd_ref, dwglu_ref, dbglu_ref, dwbs_ref, dwba_ref, dwout_ref):
                r[...] = jnp.zeros_like(r)

        zs = zs_ref[...]
        ya = ya_ref[...]
        ys, yg, sg, y2, bs, ba, s1, s2, merged = _merge_forward(
            ypre_ref[...], zs, gs_ref[...], ga_ref[...], ya, d_ref[...],
            wglu_ref[...], bglu_ref[...], wbs_ref[...], wba_ref[...])
        dh2b = dh2_ref[...].astype(BF16)
        dmerged = _dot_nt(dh2b, wout_ref[...])
        dwout_ref[...] += _dot_tn(merged.astype(BF16), dh2b)
        dbs = (dmerged * s1).astype(BF16)
        dba = (dmerged * s2).astype(BF16)
        dgs_ref[...] = (dmerged * bs * s1 * (1.0 - s1)).astype(BF16)
        dga_ref[...] = (dmerged * ba * s2 * (1.0 - s2)).astype(BF16)
        dwbs_ref[...] += _dot_tn(y2.astype(BF16), dbs)
        dwba_ref[...] += _dot_tn(ya, dba)
        dya_ref[...] = _dot_nt(dba, wba_ref[...]).astype(BF16)
        dy2 = _dot_nt(dbs, wbs_ref[...])
        dvv = dy2 * yg * sg * (1.0 - sg)
        dvvb = dvv.astype(BF16)
        dyg = dy2 * sg + _dot_nt(dvvb, wglu_ref[...])
        dwglu_ref[...] += _dot_tn(yg.astype(BF16), dvvb)
        dbglu_ref[...] += _col_sum(dvv)
        dys = dyg * _gelu_grad(ys)
        dd_ref[...] += _col_sum(dys * zs)
        dzs_ref[...] = dys * d_ref[...]
        dypre_ref[...] = dys

    f32 = lambda *s: jax.ShapeDtypeStruct(s, F32)
    b16 = lambda *s: jax.ShapeDtypeStruct(s, BF16)
    return pl.pallas_call(
        body, name="merge_bwd", grid=(T // tm,),
        in_specs=[tok(D, 0), tok(W, 0)] + z_specs + [tok(W, 0)] + w_specs,
        out_specs=[tok(W, 0), tok(W, 0), tok(D, 0), tok(D, 0), tok(W, 0)] + w_specs,
        out_shape=[f32(T, W), f32(T, W), b16(T, D), b16(T, D), b16(T, W),
                   f32(1, W), f32(W, W), f32(1, W), f32(W, D), f32(W, D), f32(D, D)],
        compiler_params=_cparams(("arbitrary",), VMEM_LIMIT),
    )(dh2, ypre, z, z, z, ya, ssm_d, w_glu, b_glu, w_bs, w_ba, w_out)


def _gather_two_level(shards, name):
    n = len(shards)

    def body(*refs):
        x_refs, out_refs = refs[:n], refs[n:2 * n]
        send_sems, recv_sems, local_sems = refs[2 * n:]
        x, y, c = _my_place()
        me, sibling = (x, y, c), (x, y, 1 - c)
        chips = [(1 - x, y), (x, 1 - y), (1 - x, 1 - y)]

        def copy(a, k, block, to, own=False):
            slot = out_refs[a].at[_flat(*block)]
            return pltpu.make_async_remote_copy(
                src_ref=x_refs[a] if own else slot, dst_ref=slot,
                send_sem=send_sems.at[7 * a + k], recv_sem=recv_sems.at[7 * a + k],
                device_id=to, device_id_type=MESH_ID)

        sent, local = [], []
        for a in range(n):
            local.append(pltpu.make_async_copy(x_refs[a], out_refs[a].at[_flat(*me)], local_sems.at[a]))
            local[-1].start()
            sent.append(copy(a, 0, me, sibling, own=True))
            sent += [copy(a, 1 + j, me, (*chip, c), own=True) for j, chip in enumerate(chips)]
        for cp in sent:
            cp.start()
        for a in range(n):
            for j, chip in enumerate(chips):
                copy(a, 1 + j, (*chip, c), me).wait_recv()
                sent.append(copy(a, 4 + j, (*chip, c), sibling))
                sent[-1].start()
        for a in range(n):
            copy(a, 0, sibling, me).wait_recv()
            for j, chip in enumerate(chips):
                copy(a, 4 + j, (*chip, 1 - c), me).wait_recv()
        for cp in sent:
            cp.wait_send()
        for cp in local:
            cp.wait()

    return pl.pallas_call(
        body, name=name, in_specs=[_HBM] * n, out_specs=[_HBM] * n,
        out_shape=[jax.ShapeDtypeStruct((N_DEV,) + s.shape, s.dtype) for s in shards],
        scratch_shapes=[pltpu.SemaphoreType.DMA((7 * n,)), pltpu.SemaphoreType.DMA((7 * n,)),
                        pltpu.SemaphoreType.DMA((n,))],
    )(*shards)


PACK_COLS = 1024
BIG = (("ffn1_w_gate", 1), ("ffn1_w_up", 1), ("ffn1_w_down", 0), ("w_in", 1), ("ssm_w_glu", 0),
       ("w_branch_ssm", 1), ("w_branch_att", 1), ("w_out", 0),
       ("ffn2_w_gate", 1), ("ffn2_w_up", 1), ("ffn2_w_down", 0))
BIG_AXIS = dict(BIG)
SSM_DIR = ("ssm_a_re", "ssm_a_im", "ssm_log_dt", "ssm_b_re", "ssm_b_im", "ssm_c_re", "ssm_c_im")
SMALL = (("ffn1_norm", "mix_norm")
         + tuple(n + "_fwd" for n in SSM_DIR) + tuple(n + "_bwd" for n in SSM_DIR)
         + ("ssm_d", "ssm_b_glu", "att_rpb", "ffn2_norm", "final_norm"))
WEIGHTS = ("ffn1_norm", "ffn1_w_gate", "ffn1_w_up", "ffn1_w_down", "mix_norm", "w_in") \
    + tuple(n + "_fwd" for n in SSM_DIR) + tuple(n + "_bwd" for n in SSM_DIR) \
    + ("ssm_d", "ssm_w_glu", "ssm_b_glu", "att_rpb", "w_branch_ssm", "w_branch_att", "w_out",
       "ffn2_norm", "ffn2_w_gate", "ffn2_w_up", "ffn2_w_down", "final_norm")


def _pad_rows(a, mult):
    pad = (-a.shape[-2]) % mult
    if pad:
        a = jnp.concatenate([a, jnp.zeros(a.shape[:-2] + (pad, a.shape[-1]), a.dtype)], axis=-2)
    return a


def _pack(arrays, row_mult):
    flat = jnp.concatenate([a.reshape(-1) for a in arrays])
    pad = (-flat.shape[0]) % PACK_COLS
    if pad:
        flat = jnp.concatenate([flat, jnp.zeros((pad,), flat.dtype)])
    return _pad_rows(flat.reshape(-1, PACK_COLS), row_mult)


def _unpack(slab, shapes):
    flat = slab.reshape(-1)
    out, at = [], 0
    for s in shapes:
        n = int(np.prod(s))
        out.append(flat[at:at + n].reshape(s))
        at += n
    return out


def _split_for_devices(g, axis):
    r, c = g.shape
    if axis == 1:
        return g.reshape(r, N_DEV, c // N_DEV).transpose(1, 0, 2).astype(BF16)
    return g.reshape(N_DEV, r // N_DEV, c).astype(BF16)


def _join_shards(gathered, axis):
    _, r, c = gathered.shape
    if axis == 1:
        return gathered.transpose(1, 0, 2).reshape(r, N_DEV * c)
    return gathered.reshape(N_DEV * r, c)


def _s5_direction_inputs(p, sfx, chain_len):
    bt_re = p["ssm_b_re" + sfx][0].transpose(0, 2, 1)
    bt_im = p["ssm_b_im" + sfx][0].transpose(0, 2, 1)
    raw = (p["ssm_a_re" + sfx][0], p["ssm_a_im" + sfx][0], p["ssm_log_dt" + sfx][0][:, None], bt_re, bt_im)
    lr, li, sr, si, bbr, bbi = _disc_fwd(*raw, chain_len,"s5_disc" + sfx)
    lam = jnp.stack([_s5_pack_lam(t) for t in (lr, li, sr, si)], axis=2)
    mats = (_s5_pack_b(bbr), _s5_pack_b(bbi), lam,
            _s5_pack_c(p["ssm_c_re" + sfx][0]), _s5_pack_c(-p["ssm_c_im" + sfx][0]))
    return raw, mats


def kernel(x, ffn1_norm, ffn1_w_gate, ffn1_w_up, ffn1_w_down, mix_norm, w_in, ssm_a_re_fwd, ssm_a_im_fwd, ssm_log_dt_fwd, ssm_b_re_fwd, ssm_b_im_fwd, ssm_c_re_fwd, ssm_c_im_fwd, ssm_a_re_bwd, ssm_a_im_bwd, ssm_log_dt_bwd, ssm_b_re_bwd, ssm_b_im_bwd, ssm_c_re_bwd, ssm_c_im_bwd, ssm_d, ssm_w_glu, ssm_b_glu, att_rpb, w_branch_ssm, w_branch_att, w_out, ffn2_norm, ffn2_w_gate, ffn2_w_up, ffn2_w_down, final_norm, loss_target, m_ffn1_norm, m_ffn1_w_gate, m_ffn1_w_up, m_ffn1_w_down, m_mix_norm, m_w_in, m_ssm_a_re_fwd, m_ssm_a_im_fwd, m_ssm_log_dt_fwd, m_ssm_b_re_fwd, m_ssm_b_im_fwd, m_ssm_c_re_fwd, m_ssm_c_im_fwd, m_ssm_a_re_bwd, m_ssm_a_im_bwd, m_ssm_log_dt_bwd, m_ssm_b_re_bwd, m_ssm_b_im_bwd, m_ssm_c_re_bwd, m_ssm_c_im_bwd, m_ssm_d, m_ssm_w_glu, m_ssm_b_glu, m_att_rpb, m_w_branch_ssm, m_w_branch_att, m_w_out, m_ffn2_norm, m_ffn2_w_gate, m_ffn2_w_up, m_ffn2_w_down, m_final_norm, v_ffn1_norm, v_ffn1_w_gate, v_ffn1_w_up, v_ffn1_w_down, v_mix_norm, v_w_in, v_ssm_a_re_fwd, v_ssm_a_im_fwd, v_ssm_log_dt_fwd, v_ssm_b_re_fwd, v_ssm_b_im_fwd, v_ssm_c_re_fwd, v_ssm_c_im_fwd, v_ssm_a_re_bwd, v_ssm_a_im_bwd, v_ssm_log_dt_bwd, v_ssm_b_re_bwd, v_ssm_b_im_bwd, v_ssm_c_re_bwd, v_ssm_c_im_bwd, v_ssm_d, v_ssm_w_glu, v_ssm_b_glu, v_att_rpb, v_w_branch_ssm, v_w_branch_att, v_w_out, v_ffn2_norm, v_ffn2_w_gate, v_ffn2_w_up, v_ffn2_w_down, v_final_norm):
    p = dict(locals())
    x = p["x"][0]
    target = p["loss_target"][0]
    T, D = x.shape

    shard = {n: p[n][0].astype(BF16) for n, _ in BIG}
    ffn1_w = ("ffn1_w_gate", "ffn1_w_up", "ffn1_w_down")
    mix_w = ("w_in", "ssm_w_glu", "w_branch_ssm", "w_branch_att", "w_out")
    ffn2_w = ("ffn2_w_gate", "ffn2_w_up", "ffn2_w_down")
    gathered = dict(zip(ffn1_w, _gather_two_level([shard[n] for n in ffn1_w], "gather_ffn1")))
    full = lambda n: _join_shards(gathered[n], BIG_AXIS[n])

    h0 = x
    wg1, wu1, wd1 = [full(n) for n in ffn1_w]
    (h1, xn1, g1, u1), got = _ffn_fwd(h0, p["ffn1_norm"], wg1, wu1, wd1, "ffn1_fwd",
                                      _Comm("gather", [shard[n] for n in mix_w]))
    gathered.update(zip(mix_w, got))
    z, zb, un = _mixin_fwd(h1, p["mix_norm"], gathered["w_in"])
    W = SSM_WIDTH
    zp = _permute_rows(zb[:, :W])
    chain_len = T // SCAN_LANES // S5_NQ
    raw_f, mats_f = _s5_direction_inputs(p, "_fwd", chain_len)
    raw_b, mats_b = _s5_direction_inputs(p, "_bwd", chain_len)
    bre, bim, lam, cre, cimn = [jnp.stack([f, b]) for f, b in zip(mats_f, mats_b)]
    bre, bim, cre, cimn = [t.astype(BF16) for t in (bre, bim, cre, cimn)]
    (yp,), got = _s5_fwd(zp, bre, bim, lam, cre, cimn, _Comm("gather", [shard[n] for n in ffn2_w]))
    gathered.update(zip(ffn2_w, got))
    ypre = _unpermute_rows(yp)
    table = _att_bias_table(p["att_rpb"][0])
    ya = _att_fwd(zb, table)
    tail_w = (p["ssm_d"], full("ssm_w_glu"), p["ssm_b_glu"], full("w_branch_ssm"), full("w_branch_att"), full("w_out"))
    h2 = _merge_fwd(ypre, z, ya, h1, *tail_w)
    wg2, wu2, wd2 = [full(n) for n in ffn2_w]
    (h3, xn2, g2, u2), _ = _ffn_fwd(h2, p["ffn2_norm"], wg2, wu2, wd2, "ffn2_fwd")
    loss_part, dh3, d_final = _loss_head(h3, p["final_norm"][None], target)

    grads = {"final_norm": d_final[0]}
    to_send = lambda names: _Comm("exchange", [_split_for_devices(grads[n], BIG_AXIS[n]) for n in names])
    parts = {}
    (dh2, grads["ffn2_norm"], do2, a2, dg2, du2), _ = _ffn_bwd(
        dh3, h2, p["ffn2_norm"], g2, u2, wg2, wu2, wd2, "ffn2_bwd")
    grads["ffn2_w_gate"] = _xty(xn2, dg2, "ffn2_dw_gate")
    grads["ffn2_w_up"] = _xty(xn2, du2, "ffn2_dw_up")
    grads["ffn2_w_down"] = _xty(a2, do2, "ffn2_dw_down")
    (dypre, dzs_skip, dgs, dga, dya, grads["ssm_d"], grads["ssm_w_glu"], grads["ssm_b_glu"],
     grads["w_branch_ssm"], grads["w_branch_att"], grads["w_out"]) = _merge_bwd(dh2, ypre, z, ya, *tail_w)
    (dq, dk, dv, dtable), got = _att_bwd(zb, dya, table, to_send(ffn2_w))
    parts.update(zip(ffn2_w, got))
    grads["att_rpb"] = _att_bias_table_t(dtable)
    dyp = _permute_rows(dypre).astype(BF16)
    tail_names = ("ssm_w_glu", "w_branch_ssm", "w_branch_att", "w_out")
    (dzp, dbre, dbim, dlam, dcre, dcimn), got = _s5_bwd(zp, dyp, bre, bim, lam, cre, cimn, to_send(tail_names))
    parts.update(zip(tail_names, got))
    G, P = SSM_GROUPS, SSM_STATE
    for d, (sfx, raw) in enumerate((("_fwd", raw_f), ("_bwd", raw_b))):
        da_re, da_im, dldt, dbt_re, dbt_im = _disc_bwd(
            *raw, dlam[d, :, :, 0, :].reshape(G, P), dlam[d, :, :, 1, :].reshape(G, P),
            _s5_unpack_b(dbre[d]), _s5_unpack_b(dbim[d]), "s5_disc_grad" + sfx)
        grads["ssm_a_re" + sfx] = da_re
        grads["ssm_a_im" + sfx] = da_im
        grads["ssm_log_dt" + sfx] = dldt[:, 0]
        grads["ssm_b_re" + sfx] = dbt_re.transpose(0, 2, 1)
        grads["ssm_b_im" + sfx] = dbt_im.transpose(0, 2, 1)
        grads["ssm_c_re" + sfx] = _s5_unpack_c(dcre[d])
        grads["ssm_c_im" + sfx] = -_s5_unpack_c(dcimn[d])
    dzs = _unpermute_rows(dzp) + dzs_skip
    dz = jnp.concatenate([dzs.astype(BF16), dq, dk, dv, dgs, dga], axis=1)
    dh1, grads["mix_norm"] = _mixin_bwd(dz, dh2, h1, p["mix_norm"], gathered["w_in"])
    grads["w_in"] = _xty(un, dz, "dw_in")
    (dh0, grads["ffn1_norm"], do1, a1, dg1, du1), got = _ffn_bwd(
        dh1, h0, p["ffn1_norm"], g1, u1, wg1, wu1, wd1, "ffn1_bwd", to_send(("w_in",)))
    parts["w_in"] = got[0]
    grads["ffn1_w_gate"] = _xty(xn1, dg1, "ffn1_dw_gate")
    grads["ffn1_w_up"] = _xty(xn1, du1, "ffn1_dw_up")
    grads["ffn1_w_down"] = _xty(a1, do1, "ffn1_dw_down")

    small_shapes = [p[n].shape for n in SMALL]
    small_slab = _pack([grads[n].astype(F32) for n in SMALL], 8)
    last = _Comm(["exchange"] * 3 + ["gather"],
                 [_split_for_devices(grads[n], BIG_AXIS[n]) for n in ffn1_w] + [small_slab])
    got = _comm_call(last, "exchange_last")
    parts.update(zip(ffn1_w, got[:3]))
    got_small = got[3]

    results = {}
    for n, _ in BIG:
        outs = _adamw(parts[n], p[n][0], p["m_" + n][0], p["v_" + n][0], "adamw_" + n)
        results[n] = [o[None] for o in outs]
    pack_small = lambda pre: _pack([p[pre + n] for n in SMALL], 8)
    small_out = _adamw(got_small, pack_small(""), pack_small("m_"), pack_small("v_"), "adamw_small")
    for n, vals in zip(SMALL, zip(*[_unpack(slab, small_shapes) for slab in small_out])):
        results[n] = list(vals)

    loss = lax.psum(loss_part[0, 0], ("x", "y", "c"))
    out = [loss, dh0[None]]
    for kind in range(4):
        out += [results[n][kind] for n in WEIGHTS]
    return tuple(out)
```

```python
import functools
import math

import numpy as np
import jax
import jax.numpy as jnp
from jax import lax
from jax.experimental import pallas as pl
from jax.experimental.pallas import tpu as pltpu

F32 = jnp.float32
BF16 = jnp.bfloat16
MESH_ID = pl.DeviceIdType.MESH

SSM_GROUP = 16
SSM_GROUPS = 32
SSM_STATE = 64
SSM_WIDTH = 512
ATT_HEADS = 8
ATT_HEAD_DIM = 64
ATT_WIDTH = 512
GRID_W = 64
WIN_H = 8
WIN_W = 16
EPS = 1e-6
NEG_INF = -1e30
ADAM_LR = 0.001
ADAM_B1 = 0.9
ADAM_B2 = 0.999
ADAM_EPS = 1e-08
ADAM_WD = 0.01
ADAM_STEP = 10

N_DEV = 8
V7X_VMEM_BYTES = 64 * 1024 * 1024
VMEM_LIMIT = V7X_VMEM_BYTES - 8 * 1024 * 1024
SCAN_LANES = 8
ATT_ROWS = 4


def _cparams(sem, vmem=None):
    return pltpu.CompilerParams(dimension_semantics=sem, vmem_limit_bytes=vmem)


def _dot(a, b):
    return jnp.dot(a, b, preferred_element_type=F32)


def _dot_nt(a, b):
    return lax.dot_general(a, b, (((1,), (1,)), ((), ())), preferred_element_type=F32)


def _dot_tn(a, b):
    return lax.dot_general(a, b, (((0,), (0,)), ((), ())), preferred_element_type=F32)


def _rms(h):
    return lax.rsqrt(jnp.mean(h * h, axis=-1, keepdims=True) + EPS)


def _rms_bwd(h, r, v):
    return r * v - h * (r * r * r) * jnp.mean(h * v, axis=-1, keepdims=True)


def _col_sum(x):
    return jnp.sum(x, axis=0, keepdims=True)


def _my_place():
    return lax.axis_index("x"), lax.axis_index("y"), lax.axis_index("c")


def _flat(px, py, pc):
    return 4 * px + 2 * py + pc


class _Comm:
    def __init__(self, kind, arrays):
        self.arrays = list(arrays)
        self.n = len(self.arrays)
        self.kinds = [kind] * self.n if isinstance(kind, str) else list(kind)

    def out_shapes(self):
        return [jax.ShapeDtypeStruct((N_DEV,) + a.shape if k == "gather" else a.shape, a.dtype)
                for k, a in zip(self.kinds, self.arrays)]

    def scratch(self):
        return [pltpu.SemaphoreType.DMA((7 * self.n,)), pltpu.SemaphoreType.DMA((7 * self.n,)),
                pltpu.SemaphoreType.DMA((self.n,))]

    def run(self, srcs, dsts, sems, start):
        send_sems, recv_sems, local_sems = sems
        x, y, c = _my_place()
        mine = _flat(x, y, c)
        for a, (src, dst) in enumerate(zip(srcs, dsts)):
            whole = self.kinds[a] == "gather"
            local = pltpu.make_async_copy(src if whole else src.at[mine], dst.at[mine], local_sems.at[a])
            local.start() if start else local.wait()
            for k in range(1, N_DEV):
                px = 1 - x if k & 4 else x
                py = 1 - y if k & 2 else y
                pc = 1 - c if k & 1 else c
                cp = pltpu.make_async_remote_copy(
                    src_ref=src if whole else src.at[_flat(px, py, pc)], dst_ref=dst.at[mine],
                    send_sem=send_sems.at[7 * a + k - 1], recv_sem=recv_sems.at[7 * a + k - 1],
                    device_id=(px, py, pc), device_id_type=MESH_ID)
                cp.start() if start else cp.wait()


_HBM = pl.BlockSpec(memory_space=pltpu.HBM)


def _comm_call(comm, name):
    def body(*refs):
        srcs, dsts, sems = refs[:comm.n], refs[comm.n:2 * comm.n], refs[2 * comm.n:]
        comm.run(srcs, dsts, sems, True)
        comm.run(srcs, dsts, sems, False)

    return pl.pallas_call(body, name=name, in_specs=[_HBM] * comm.n, out_specs=[_HBM] * comm.n,
                          out_shape=comm.out_shapes(), scratch_shapes=comm.scratch())(*comm.arrays)


def _pallas(core, name, grid, in_specs, out_specs, out_shape, scratch, sem, args, comm=None):
    if comm is None:
        out = pl.pallas_call(core, name=name, grid=grid, in_specs=in_specs, out_specs=out_specs,
                             out_shape=out_shape, scratch_shapes=scratch,
                             compiler_params=_cparams(sem, VMEM_LIMIT))(*args)
        return out, []
    n_in, n_out, n_scr, n = len(in_specs), len(out_specs), len(scratch), comm.n

    def body(*refs):
        ins, srcs = refs[:n_in], refs[n_in:n_in + n]
        outs, dsts = refs[n_in + n:n_in + n + n_out], refs[n_in + n + n_out:n_in + 2 * n + n_out]
        scr, sems = refs[n_in + 2 * n + n_out:n_in + 2 * n + n_out + n_scr], refs[n_in + 2 * n + n_out + n_scr:]
        ids = [pl.program_id(k) for k in range(len(grid))]
        first = functools.reduce(lambda a, b: a & b, [i == 0 for i in ids])
        last = functools.reduce(lambda a, b: a & b, [i == g - 1 for i, g in zip(ids, grid)])

        @pl.when(first)
        def _():
            comm.run(srcs, dsts, sems, True)

        core(*ins, *outs, *scr)

        @pl.when(last)
        def _():
            comm.run(srcs, dsts, sems, False)

    out = pl.pallas_call(
        body, name=name, grid=grid, in_specs=list(in_specs) + [_HBM] * n, out_specs=list(out_specs) + [_HBM] * n,
        out_shape=list(out_shape) + comm.out_shapes(), scratch_shapes=list(scratch) + comm.scratch(),
        compiler_params=_cparams(("arbitrary",) * len(grid), VMEM_LIMIT))(*args, *comm.arrays)
    return out[:n_out], out[n_out:]


def _ffn_tiles(T, F, wide):
    if wide and F % 1408 == 0:
        return min(T, 512), 1408
    return min(T, 1024), 256 if F % 256 == 0 else F


def _ffn_fwd(h, gain, wg, wu, wd, name, comm=None):
    T, D = h.shape
    F = wg.shape[1]
    tm, tf = _ffn_tiles(T, F, wide=True)
    nj = F // tf

    def body(h_ref, gain_ref, wg_ref, wu_ref, wd_ref, ho_ref, xn_ref, g_ref, u_ref, acc_ref):
        j = pl.program_id(1)

        @pl.when(j == 0)
        def _():
            hh = h_ref[...]
            xn_ref[...] = (hh * _rms(hh) * gain_ref[...]).astype(BF16)
            acc_ref[...] = jnp.zeros_like(acc_ref)

        xn = xn_ref[...]
        g = _dot(xn, wg_ref[...])
        u = _dot(xn, wu_ref[...])
        g_ref[...] = g.astype(BF16)
        u_ref[...] = u.astype(BF16)
        a = (g * jax.nn.sigmoid(g) * u).astype(BF16)
        acc_ref[...] += _dot(a, wd_ref[...])

        @pl.when(j == nj - 1)
        def _():
            ho_ref[...] = h_ref[...] + 0.5 * acc_ref[...]

    return _pallas(
        body, name, (T // tm, nj),
        [pl.BlockSpec((tm, D), lambda i, j: (i, 0)),
         pl.BlockSpec((1, D), lambda i, j: (0, 0)),
         pl.BlockSpec((D, tf), lambda i, j: (0, j)),
         pl.BlockSpec((D, tf), lambda i, j: (0, j)),
         pl.BlockSpec((tf, D), lambda i, j: (j, 0))],
        [pl.BlockSpec((tm, D), lambda i, j: (i, 0)),
         pl.BlockSpec((tm, D), lambda i, j: (i, 0)),
         pl.BlockSpec((tm, tf), lambda i, j: (i, j)),
         pl.BlockSpec((tm, tf), lambda i, j: (i, j))],
        [jax.ShapeDtypeStruct((T, D), F32), jax.ShapeDtypeStruct((T, D), BF16),
         jax.ShapeDtypeStruct((T, F), BF16), jax.ShapeDtypeStruct((T, F), BF16)],
        [pltpu.VMEM((tm, D), F32)], ("parallel", "arbitrary"), (h, gain, wg, wu, wd), comm)


def _ffn_bwd(dho, h, gain, g, u, wg, wu, wd, name, comm=None):
    T, D = h.shape
    F = wg.shape[1]
    tm, tf = _ffn_tiles(T, F, wide=False)
    nj = F // tf

    def body(dho_ref, h_ref, gain_ref, g_ref, u_ref, wg_ref, wu_ref, wd_ref,
             dh_ref, dgain_ref, do_ref, a_ref, dg_ref, du_ref, acc_ref):
        i = pl.program_id(0)
        j = pl.program_id(1)

        @pl.when(j == 0)
        def _():
            do_ref[...] = (0.5 * dho_ref[...]).astype(BF16)
            acc_ref[...] = jnp.zeros_like(acc_ref)

        @pl.when((i == 0) & (j == 0))
        def _():
            dgain_ref[...] = jnp.zeros_like(dgain_ref)

        da = _dot_nt(do_ref[...], wd_ref[...])
        gg = g_ref[...].astype(F32)
        uu = u_ref[...].astype(F32)
        s = jax.nn.sigmoid(gg)
        sl = gg * s
        a_ref[...] = (sl * uu).astype(BF16)
        dg = (da * uu * (s * (1.0 + gg * (1.0 - s)))).astype(BF16)
        du = (da * sl).astype(BF16)
        dg_ref[...] = dg
        du_ref[...] = du
        acc_ref[...] += _dot_nt(dg, wg_ref[...]) + _dot_nt(du, wu_ref[...])

        @pl.when(j == nj - 1)
        def _():
            hh = h_ref[...]
            r = _rms(hh)
            dxn = acc_ref[...]
            dgain_ref[...] += _col_sum(dxn * hh * r)
            dh_ref[...] = dho_ref[...] + _rms_bwd(hh, r, dxn * gain_ref[...])

    return _pallas(
        body, name, (T // tm, nj),
        [pl.BlockSpec((tm, D), lambda i, j: (i, 0)),
         pl.BlockSpec((tm, D), lambda i, j: (i, 0)),
         pl.BlockSpec((1, D), lambda i, j: (0, 0)),
         pl.BlockSpec((tm, tf), lambda i, j: (i, j)),
         pl.BlockSpec((tm, tf), lambda i, j: (i, j)),
         pl.BlockSpec((D, tf), lambda i, j: (0, j)),
         pl.BlockSpec((D, tf), lambda i, j: (0, j)),
         pl.BlockSpec((tf, D), lambda i, j: (j, 0))],
        [pl.BlockSpec((tm, D), lambda i, j: (i, 0)),
         pl.BlockSpec((1, D), lambda i, j: (0, 0)),
         pl.BlockSpec((tm, D), lambda i, j: (i, 0)),
         pl.BlockSpec((tm, tf), lambda i, j: (i, j)),
         pl.BlockSpec((tm, tf), lambda i, j: (i, j)),
         pl.BlockSpec((tm, tf), lambda i, j: (i, j))],
        [jax.ShapeDtypeStruct((T, D), F32), jax.ShapeDtypeStruct((1, D), F32),
         jax.ShapeDtypeStruct((T, D), BF16), jax.ShapeDtypeStruct((T, F), BF16),
         jax.ShapeDtypeStruct((T, F), BF16), jax.ShapeDtypeStruct((T, F), BF16)],
        [pltpu.VMEM((tm, D), F32)], ("arbitrary", "arbitrary"), (dho, h, gain, g, u, wg, wu, wd), comm)


def _xty(x, y, name, comm=None):
    T, K = x.shape
    N = y.shape[1]
    tt = min(T, 1024)
    tk = K if K <= 1024 else (1408 if K % 1408 == 0 else K)
    tn = N if N <= 1024 else (1408 if N % 1408 == 0 else (1024 if N % 1024 == 0 else N))
    nt = T // tt

    def body(x_ref, y_ref, o_ref):
        t = pl.program_id(2)

        @pl.when(t == 0)
        def _():
            o_ref[...] = jnp.zeros_like(o_ref)

        o_ref[...] += _dot_tn(x_ref[...], y_ref[...])

    (out,), got = _pallas(
        body, name, (K // tk, N // tn, nt),
        [pl.BlockSpec((tt, tk), lambda k, n, t: (t, k)), pl.BlockSpec((tt, tn), lambda k, n, t: (t, n))],
        [pl.BlockSpec((tk, tn), lambda k, n, t: (k, n))], [jax.ShapeDtypeStruct((K, N), F32)], [],
        ("parallel", "parallel", "arbitrary"), (x, y), comm)
    return out if comm is None else (out, got)


def _mixin_fwd(h, gain, w_in):
    T, D = h.shape
    nn, _, tn = w_in.shape
    N = nn * tn
    tm = min(T, 1024)

    def body(h_ref, gain_ref, w_ref, z_ref, zb_ref, un_ref):
        @pl.when(pl.program_id(1) == 0)
        def _():
            hh = h_ref[...]
            un_ref[...] = (hh * _rms(hh) * gain_ref[...]).astype(BF16)

        z = _dot(un_ref[...], w_ref[...])
        z_ref[...] = z
        zb_ref[...] = z.astype(BF16)

    return pl.pallas_call(
        body, name="mixin_fwd", grid=(T // tm, nn),
        in_specs=[pl.BlockSpec((tm, D), lambda i, n: (i, 0)),
                  pl.BlockSpec((1, D), lambda i, n: (0, 0)),
                  pl.BlockSpec((None, D, tn), lambda i, n: (n, 0, 0))],
        out_specs=[pl.BlockSpec((tm, tn), lambda i, n: (i, n)),
                   pl.BlockSpec((tm, tn), lambda i, n: (i, n)),
                   pl.BlockSpec((tm, D), lambda i, n: (i, 0))],
        out_shape=[jax.ShapeDtypeStruct((T, N), F32), jax.ShapeDtypeStruct((T, N), BF16),
                   jax.ShapeDtypeStruct((T, D), BF16)],
        compiler_params=_cparams(("parallel", "arbitrary"), VMEM_LIMIT),
    )(h, gain, w_in)


def _mixin_bwd(dz, dh_res, h, gain, w_in):
    T, D = h.shape
    nn, _, tn = w_in.shape
    tm = min(T, 1024)

    def body(dz_ref, dres_ref, h_ref, gain_ref, w_ref, dh_ref, dgain_ref, acc_ref):
        i = pl.program_id(0)
        n = pl.program_id(1)

        @pl.when(n == 0)
        def _():
            acc_ref[...] = jnp.zeros_like(acc_ref)

        @pl.when((i == 0) & (n == 0))
        def _():
            dgain_ref[...] = jnp.zeros_like(dgain_ref)

        acc_ref[...] += _dot_nt(dz_ref[...], w_ref[...])

        @pl.when(n == nn - 1)
        def _():
            hh = h_ref[...]
            r = _rms(hh)
            dun = acc_ref[...]
            dgain_ref[...] += _col_sum(dun * hh * r)
            dh_ref[...] = dres_ref[...] + _rms_bwd(hh, r, dun * gain_ref[...])

    return pl.pallas_call(
        body, name="mixin_bwd", grid=(T // tm, nn),
        in_specs=[pl.BlockSpec((tm, tn), lambda i, n: (i, n)),
                  pl.BlockSpec((tm, D), lambda i, n: (i, 0)),
                  pl.BlockSpec((tm, D), lambda i, n: (i, 0)),
                  pl.BlockSpec((1, D), lambda i, n: (0, 0)),
                  pl.BlockSpec((None, D, tn), lambda i, n: (n, 0, 0))],
        out_specs=[pl.BlockSpec((tm, D), lambda i, n: (i, 0)),
                   pl.BlockSpec((1, D), lambda i, n: (0, 0))],
        out_shape=[jax.ShapeDtypeStruct((T, D), F32), jax.ShapeDtypeStruct((1, D), F32)],
        scratch_shapes=[pltpu.VMEM((tm, D), F32)],
        compiler_params=_cparams(("arbitrary", "arbitrary"), VMEM_LIMIT),
    )(dz, dh_res, h, gain, w_in)


def _loss_head(h, gain, target):
    T, D = h.shape
    tm = min(T, 1024)

    def body(h_ref, gain_ref, t_ref, loss_ref, dh_ref, dgain_ref):
        @pl.when(pl.program_id(0) == 0)
        def _():
            loss_ref[...] = jnp.zeros_like(loss_ref)
            dgain_ref[...] = jnp.zeros_like(dgain_ref)

        hh = h_ref[...]
        r = _rms(hh)
        e = hh * r * gain_ref[...] - t_ref[...]
        loss_ref[...] += (0.5 / D) * jnp.sum(e * e)
        dy = e * (1.0 / D)
        dgain_ref[...] += _col_sum(dy * hh * r)
        dh_ref[...] = _rms_bwd(hh, r, dy * gain_ref[...])

    return pl.pallas_call(
        body, name="loss_head", grid=(T // tm,),
        in_specs=[pl.BlockSpec((tm, D), lambda i: (i, 0)),
                  pl.BlockSpec((1, D), lambda i: (0, 0)),
                  pl.BlockSpec((tm, D), lambda i: (i, 0))],
        out_specs=[pl.BlockSpec((1, 128), lambda i: (0, 0)),
                   pl.BlockSpec((tm, D), lambda i: (i, 0)),
                   pl.BlockSpec((1, D), lambda i: (0, 0))],
        out_shape=[jax.ShapeDtypeStruct((1, 128), F32), jax.ShapeDtypeStruct((T, D), F32),
                   jax.ShapeDtypeStruct((1, D), F32)],
        compiler_params=_cparams(("arbitrary",), VMEM_LIMIT),
    )(h, gain, target)


def _adamw(parts, w, m, v, name):
    R, C = w.shape
    mult = 16 if parts.dtype == BF16 else 8
    tr = max(t for t in range(mult, min(R, 512) + 1, mult) if R % t == 0)
    c1 = 1.0 - ADAM_B1 ** ADAM_STEP
    c2 = 1.0 - ADAM_B2 ** ADAM_STEP

    def body(p_ref, w_ref, m_ref, v_ref, g_ref, d_ref, nm_ref, nv_ref):
        g = p_ref[0].astype(F32)
        for k in range(1, N_DEV):
            g = g + p_ref[k].astype(F32)
        mm = ADAM_B1 * m_ref[...] + (1.0 - ADAM_B1) * g
        vv = ADAM_B2 * v_ref[...] + (1.0 - ADAM_B2) * (g * g)
        g_ref[...] = g
        nm_ref[...] = mm
        nv_ref[...] = vv
        d_ref[...] = -ADAM_LR * ((mm / c1) / (jnp.sqrt(vv / c2) + ADAM_EPS) + ADAM_WD * w_ref[...])

    spec = pl.BlockSpec((tr, C), lambda i: (i, 0))
    return pl.pallas_call(
        body, name=name, grid=(R // tr,),
        in_specs=[pl.BlockSpec((N_DEV, tr, C), lambda i: (0, i, 0)), spec, spec, spec],
        out_specs=[spec, spec, spec, spec],
        out_shape=[jax.ShapeDtypeStruct((R, C), F32)] * 4,
        compiler_params=_cparams(("parallel",), VMEM_LIMIT),
    )(parts, w, m, v)


S5_NS = 256
S5_NH = 2
S5_NCB = 4
S5_RC = 512
S5_NQ = 4


def _disc_math(a_re, a_im, log_dt, bt_re, bt_im):
    dt = jnp.exp(log_dt)
    zr, zi = a_re * dt, a_im * dt
    mag = jnp.exp(zr)
    lb_re, lb_im = mag * jnp.cos(zi), mag * jnp.sin(zi)
    den = a_re * a_re + a_im * a_im
    nr, ni = lb_re - 1.0, lb_im
    f_re = (nr * a_re + ni * a_im) / den
    f_im = (ni * a_re - nr * a_im) / den
    bb_re = f_re[:, None, :] * bt_re - f_im[:, None, :] * bt_im
    bb_im = f_re[:, None, :] * bt_im + f_im[:, None, :] * bt_re
    return lb_re, lb_im, bb_re, bb_im


def _disc_fwd(a_re, a_im, log_dt, bt_re, bt_im, chain_len, name):
    G, P = a_re.shape
    C = bt_re.shape[1]
    n_sq = int(round(math.log2(chain_len)))
    assert 2 ** n_sq == chain_len

    def body(a_re_ref, a_im_ref, ldt_ref, br_ref, bi_ref, lr_ref, li_ref, sr_ref, si_ref, bbr_ref, bbi_ref):
        lr, li, bbr, bbi = _disc_math(a_re_ref[...], a_im_ref[...], ldt_ref[...], br_ref[...], bi_ref[...])
        lr_ref[...] = lr
        li_ref[...] = li
        bbr_ref[...] = bbr
        bbi_ref[...] = bbi
        pr, pi = lr, li
        for _ in range(n_sq):
            pr, pi = pr * pr - pi * pi, 2.0 * pr * pi
        sr_ref[...] = pr
        si_ref[...] = pi

    s2 = jax.ShapeDtypeStruct((G, P), F32)
    s3 = jax.ShapeDtypeStruct((G, C, P), F32)
    return pl.pallas_call(body, name=name, out_shape=[s2, s2, s2, s2, s3, s3])(a_re, a_im, log_dt, bt_re, bt_im)


def _disc_bwd(a_re, a_im, log_dt, bt_re, bt_im, d_lr, d_li, d_bbr, d_bbi, name):
    G, P = a_re.shape
    C = bt_re.shape[1]

    def body(a_re_ref, a_im_ref, ldt_ref, br_ref, bi_ref, c1, c2, c3, c4, o1, o2, o3, o4, o5):
        _, vjp = jax.vjp(_disc_math, a_re_ref[...], a_im_ref[...], ldt_ref[...], br_ref[...], bi_ref[...])
        o1[...], o2[...], o3[...], o4[...], o5[...] = vjp((c1[...], c2[...], c3[...], c4[...]))

    s2 = jax.ShapeDtypeStruct((G, P), F32)
    s3 = jax.ShapeDtypeStruct((G, C, P), F32)
    return pl.pallas_call(body, name=name, out_shape=[s2, s2, jax.ShapeDtypeStruct((G, 1), F32), s3, s3])(
        a_re, a_im, log_dt, bt_re, bt_im, d_lr, d_li, d_bbr, d_bbi)


def _row_block(ib):
    return pl.ds(pl.multiple_of(ib * SCAN_LANES, SCAN_LANES), SCAN_LANES)


def _chain_block(j, i, ascending, n_blocks):
    at = j * (n_blocks // S5_NQ) + i
    return _row_block(jnp.where(ascending, at, n_blocks - 1 - at))


def _unrolled_loop(n, unroll, body, carry):
    trips = n // unroll
    carry = lax.fori_loop(
        0, trips, lambda t, c: functools.reduce(lambda cc, u: body(t * unroll + u, cc), range(unroll), c), carry)
    for i in range(trips * unroll, n):
        carry = body(i, carry)
    return carry


def _cmul_add(lr, li, sr, si, xr, xi):
    return lr * sr - li * si + xr, lr * si + li * sr + xi


def _scan(xr_ref, xi_ref, lr, li, init, ascending, n_blocks, store):
    def step(i, carry):
        out = []
        for j, (sr, si) in enumerate(carry):
            rows = _chain_block(j, i, ascending, n_blocks)
            nr, ni = _cmul_add(lr, li, sr, si, xr_ref[rows, :], xi_ref[rows, :])
            if store:
                xr_ref[rows, :] = nr
                xi_ref[rows, :] = ni
            out.append((nr, ni))
        return tuple(out)

    return _unrolled_loop(n_blocks // S5_NQ, 4, step, init)


def _segment_starts(w, lsr, lsi, ascending):
    shape = w[0][0].shape
    row = lax.broadcasted_iota(jnp.int32, shape, 0)
    keep = row != jnp.where(ascending, 0, SCAN_LANES - 1)

    def shift(t):
        t = jnp.where(ascending, pltpu.roll(t, 1, 0), pltpu.roll(t, SCAN_LANES - 1, 0))
        return jnp.where(keep, t, 0.0)

    zero = jnp.zeros(shape, F32)
    c = [(zero, zero)] * S5_NQ
    for _ in range(SCAN_LANES):
        tr, ti = _cmul_add(lsr, lsi, *c[-1], *w[-1])
        c[0] = (shift(tr), shift(ti))
        for j in range(1, S5_NQ):
            c[j] = _cmul_add(lsr, lsi, *c[j - 1], *w[j - 1])
    return tuple(c)


def _first_pass(xr_ref, xi_ref, lam_ref, ascending, n_blocks, conj):
    shape = (SCAN_LANES, xr_ref.shape[1])
    sign = -1.0 if conj else 1.0
    lr = jnp.broadcast_to(lam_ref[0:1, :], shape)
    li = sign * jnp.broadcast_to(lam_ref[1:2, :], shape)
    lsr = jnp.broadcast_to(lam_ref[2:3, :], shape)
    lsi = sign * jnp.broadcast_to(lam_ref[3:4, :], shape)
    zero = jnp.zeros(shape, F32)
    w = _scan(xr_ref, xi_ref, lr, li, ((zero, zero),) * S5_NQ, ascending, n_blocks, store=False)
    return _segment_starts(w, lsr, lsi, ascending), lr, li


def _s5_specs(T):
    NS = S5_NS
    tok = pl.BlockSpec((T, 128), lambda c, d, h: (0, c))
    b_spec = pl.BlockSpec((None, None, None, 128, NS), lambda c, d, h: (d, c, h, 0, 0))
    c_spec = pl.BlockSpec((None, None, None, NS, 128), lambda c, d, h: (d, c, h, 0, 0))
    lam_spec = pl.BlockSpec((None, None, None, 4, NS), lambda c, d, h: (d, c, h, 0, 0))
    return tok, b_spec, c_spec, lam_spec


def _s5_fwd(zp, bre, bim, lam, cre, cimn, comm=None):
    T = zp.shape[0]
    NS = S5_NS
    nb = T // SCAN_LANES
    rc = min(S5_RC, T)
    tok, b_spec, c_spec, lam_spec = _s5_specs(T)

    def body(zp_ref, bre_ref, bim_ref, lam_ref, cre_ref, cim_ref, y_ref, xr_ref, xi_ref):
        d = pl.program_id(1)
        ascending = d == 0

        @pl.when((d == 0) & (pl.program_id(2) == 0))
        def _():
            y_ref[...] = jnp.zeros_like(y_ref)

        def proj(c, _):
            rows = pl.ds(pl.multiple_of(c * rc, rc), rc)
            zz = zp_ref[rows, :]
            xr_ref[rows, :] = _dot(zz, bre_ref[...])
            xi_ref[rows, :] = _dot(zz, bim_ref[...])
            return 0

        lax.fori_loop(0, T // rc, proj, 0)
        starts, lr, li = _first_pass(xr_ref, xi_ref, lam_ref, ascending, nb, conj=False)
        _scan(xr_ref, xi_ref, lr, li, starts, ascending, nb, store=True)

        def outp(c, _):
            rows = pl.ds(pl.multiple_of(c * rc, rc), rc)
            y_ref[rows, :] += (_dot(xr_ref[rows, :].astype(BF16), cre_ref[...])
                               + _dot(xi_ref[rows, :].astype(BF16), cim_ref[...]))
            return 0

        lax.fori_loop(0, T // rc, outp, 0)

    return _pallas(
        body, "s5_fwd", (S5_NCB, 2, S5_NH),
        [tok, b_spec, b_spec, lam_spec, c_spec, c_spec], [tok],
        [jax.ShapeDtypeStruct((T, SSM_WIDTH), F32)],
        [pltpu.VMEM((T, NS), F32), pltpu.VMEM((T, NS), F32)],
        ("parallel", "arbitrary", "arbitrary"), (zp, bre, bim, lam, cre, cimn), comm)


def _s5_bwd(zp, dyp, bre, bim, lam, cre, cimn, comm=None):
    T = zp.shape[0]
    NS, NH = S5_NS, S5_NH
    nb = T // SCAN_LANES
    rc = min(S5_RC, T)
    tok, b_spec, c_spec, lam_spec = _s5_specs(T)
    dlam_spec = pl.BlockSpec((None, None, None, 2, NS), lambda c, d, h: (d, c, h, 0, 0))

    def body(zp_ref, dyp_ref, bre_ref, bim_ref, lam_ref, cre_ref, cim_ref,
             dzp_ref, dbre_ref, dbim_ref, dlam_ref, dcre_ref, dcim_ref,
             sr_ref, si_ref, gr_ref, gi_ref):
        d = pl.program_id(1)
        ascending = d == 0
        g_ascending = d != 0

        @pl.when((d == 0) & (pl.program_id(2) == 0))
        def _():
            dzp_ref[...] = jnp.zeros_like(dzp_ref)

        dcre_ref[...] = jnp.zeros_like(dcre_ref)
        dcim_ref[...] = jnp.zeros_like(dcim_ref)
        dbre_ref[...] = jnp.zeros_like(dbre_ref)
        dbim_ref[...] = jnp.zeros_like(dbim_ref)

        def proj(c, _):
            rows = pl.ds(pl.multiple_of(c * rc, rc), rc)
            zz = zp_ref[rows, :]
            sr_ref[rows, :] = _dot(zz, bre_ref[...])
            si_ref[rows, :] = _dot(zz, bim_ref[...])
            dy = dyp_ref[rows, :]
            gr_ref[rows, :] = _dot_nt(dy, cre_ref[...])
            gi_ref[rows, :] = _dot_nt(dy, cim_ref[...])
            return 0

        lax.fori_loop(0, T // rc, proj, 0)
        s_starts, lr, li = _first_pass(sr_ref, si_ref, lam_ref, ascending, nb, conj=False)
        _scan(sr_ref, si_ref, lr, li, s_starts, ascending, nb, store=True)
        g_starts, lr, lic = _first_pass(gr_ref, gi_ref, lam_ref, g_ascending, nb, conj=True)

        def gstep(i, carry, last):
            g, (ar, ai) = carry
            out = []
            for j, (g_r, g_i) in enumerate(g):
                rows = _chain_block(j, i, g_ascending, nb)
                n_r, n_i = _cmul_add(lr, lic, g_r, g_i, gr_ref[rows, :], gi_ref[rows, :])
                gr_ref[rows, :] = n_r
                gi_ref[rows, :] = n_i
                if last:
                    s_r, s_i = s_starts[S5_NQ - 1 - j]
                else:
                    prev = _chain_block(j, i + 1, g_ascending, nb)
                    s_r, s_i = sr_ref[prev, :], si_ref[prev, :]
                ar = ar + n_r * s_r + n_i * s_i
                ai = ai + n_i * s_r - n_r * s_i
                out.append((n_r, n_i))
            return tuple(out), (ar, ai)

        zero = jnp.zeros((SCAN_LANES, NS), F32)
        steps = nb // S5_NQ
        carry = _unrolled_loop(steps - 1, 2, lambda i, c: gstep(i, c, False), (g_starts, (zero, zero)))
        _, (ar, ai) = gstep(steps - 1, carry, True)
        dlam_ref[0:1, :] = _col_sum(ar)
        dlam_ref[1:2, :] = _col_sum(ai)

        def grads(c, _):
            rows = pl.ds(pl.multiple_of(c * rc, rc), rc)
            zz = zp_ref[rows, :]
            dy = dyp_ref[rows, :]
            g_rb = gr_ref[rows, :].astype(BF16)
            g_ib = gi_ref[rows, :].astype(BF16)
            dcre_ref[...] += _dot_tn(sr_ref[rows, :].astype(BF16), dy)
            dcim_ref[...] += _dot_tn(si_ref[rows, :].astype(BF16), dy)
            dbre_ref[...] += _dot_tn(zz, g_rb)
            dbim_ref[...] += _dot_tn(zz, g_ib)
            dzp_ref[rows, :] += _dot_nt(g_rb, bre_ref[...]) + _dot_nt(g_ib, bim_ref[...])
            return 0

        lax.fori_loop(0, T // rc, grads, 0)

    f32 = lambda *s: jax.ShapeDtypeStruct(s, F32)
    return _pallas(
        body, "s5_bwd", (S5_NCB, 2, S5_NH),
        [tok, tok, b_spec, b_spec, lam_spec, c_spec, c_spec],
        [tok, b_spec, b_spec, dlam_spec, c_spec, c_spec],
        [f32(T, SSM_WIDTH), f32(2, S5_NCB, NH, 128, NS), f32(2, S5_NCB, NH, 128, NS),
         f32(2, S5_NCB, NH, 2, NS), f32(2, S5_NCB, NH, NS, 128), f32(2, S5_NCB, NH, NS, 128)],
        [pltpu.VMEM((T, NS), F32)] * 4,
        ("parallel", "arbitrary", "arbitrary"), (zp, dyp, bre, bim, lam, cre, cimn), comm)


def _s5_delta():
    d = np.zeros((S5_NH, 8, 8 // S5_NH), np.float32)
    for h in range(S5_NH):
        for go in range(8 // S5_NH):
            d[h, h * (8 // S5_NH) + go, go] = 1.0
    return d


def _s5_pack_b(bbt):
    gh = 8 // S5_NH
    b5 = bbt.reshape(S5_NCB, S5_NH, gh, SSM_GROUP, SSM_STATE).transpose(0, 1, 3, 2, 4)
    m = b5[:, :, None] * _s5_delta()[None, :, :, None, :, None]
    return m.reshape(S5_NCB, S5_NH, 128, S5_NS)


def _s5_unpack_b(dm):
    gh = 8 // S5_NH
    d6 = dm.reshape(S5_NCB, S5_NH, 8, SSM_GROUP, gh, SSM_STATE)
    b5 = jnp.sum(d6 * _s5_delta()[None, :, :, None, :, None], axis=2)
    return b5.transpose(0, 1, 3, 2, 4).reshape(SSM_GROUPS, SSM_GROUP, SSM_STATE)


def _s5_pack_c(c):
    gh = 8 // S5_NH
    c5 = c.reshape(S5_NCB, S5_NH, gh, SSM_GROUP, SSM_STATE).transpose(0, 1, 2, 4, 3)
    m = c5[:, :, :, :, None, :] * _s5_delta().transpose(0, 2, 1)[None, :, :, None, :, None]
    return m.reshape(S5_NCB, S5_NH, S5_NS, 128)


def _s5_unpack_c(dm):
    gh = 8 // S5_NH
    d6 = dm.reshape(S5_NCB, S5_NH, gh, SSM_STATE, 8, SSM_GROUP)
    c5 = jnp.sum(d6 * _s5_delta().transpose(0, 2, 1)[None, :, :, None, :, None], axis=4)
    return c5.transpose(0, 1, 2, 4, 3).reshape(SSM_GROUPS, SSM_GROUP, SSM_STATE)


def _s5_pack_lam(x):
    return x.reshape(S5_NCB, S5_NH, S5_NS)


def _permute_rows(x):
    T = x.shape[0]
    return x.reshape(SCAN_LANES, T // SCAN_LANES, -1).transpose(1, 0, 2).reshape(T, -1)


def _unpermute_rows(x):
    T = x.shape[0]
    return x.reshape(T // SCAN_LANES, SCAN_LANES, -1).transpose(1, 0, 2).reshape(T, -1)


ATT_TB = ATT_ROWS * GRID_W
ATT_KB = 3 * ATT_TB


def _att_valid(i, n_rows):
    qi = lax.broadcasted_iota(jnp.int32, (ATT_TB, ATT_KB), 0)
    kj = lax.broadcasted_iota(jnp.int32, (ATT_TB, ATT_KB), 1)
    r = i * ATT_ROWS + qi // GRID_W
    c = qi % GRID_W
    rk = (i - 1) * ATT_ROWS + kj // GRID_W
    x = kj % GRID_W
    rs = jnp.clip(r - WIN_H // 2, 0, n_rows - WIN_H)
    cs = jnp.clip(c - WIN_W // 2, 0, GRID_W - WIN_W)
    return (rk >= rs) & (rk < rs + WIN_H) & (x >= cs) & (x < cs + WIN_W)


def _att_probs(qh, kh, bias, valid):
    s = jnp.where(valid, _dot_nt(qh, kh) + bias, NEG_INF)
    p = jnp.exp(s - jnp.max(s, axis=1, keepdims=True))
    return p * (1.0 / jnp.sum(p, axis=1, keepdims=True))


def _att_specs(n, col):
    last = n - 1
    cur = lambda i: (jnp.minimum(i, last), col)
    prv = lambda i: (jnp.maximum(jnp.minimum(i, last) - 1, 0), col)
    nxt = lambda i: (jnp.minimum(i + 1, last), col)
    blk = lambda f: pl.BlockSpec((ATT_TB, ATT_WIDTH), f)
    return blk(cur), blk(prv), blk(nxt)


def _att_fwd(zb, biasv):
    T = zb.shape[0]
    W = ATT_WIDTH
    n = T // ATT_TB
    n_rows = T // GRID_W
    cur = _att_specs(n, 0)[0]
    q_cur = _att_specs(n, 1)[0]
    k_cur, k_prv, k_nxt = _att_specs(n, 2)
    v_cur, v_prv, v_nxt = _att_specs(n, 3)

    def body(q_ref, kp_ref, kc_ref, kn_ref, vp_ref, vc_ref, vn_ref, b_ref, y_ref):
        valid = _att_valid(pl.program_id(0), n_rows)
        qs = q_ref[...] * 0.125
        kb = jnp.concatenate([kp_ref[...], kc_ref[...], kn_ref[...]], axis=0)
        vb = jnp.concatenate([vp_ref[...], vc_ref[...], vn_ref[...]], axis=0)
        outs = []
        for h in range(ATT_HEADS):
            hs = slice(h * ATT_HEAD_DIM, (h + 1) * ATT_HEAD_DIM)
            p = _att_probs(qs[:, hs], kb[:, hs], b_ref[h], valid)
            outs.append(_dot(p.astype(BF16), vb[:, hs]))
        y_ref[...] = jnp.concatenate(outs, axis=1).astype(BF16)

    return pl.pallas_call(
        body, name="att_fwd", grid=(n,),
        in_specs=[q_cur, k_prv, k_cur, k_nxt, v_prv, v_cur, v_nxt,
                  pl.BlockSpec((ATT_HEADS, ATT_TB, ATT_KB), lambda i: (0, 0, 0))],
        out_specs=cur,
        out_shape=jax.ShapeDtypeStruct((T, W), BF16),
        compiler_params=_cparams(("parallel",), VMEM_LIMIT),
    )(zb, zb, zb, zb, zb, zb, zb, biasv)


def _att_bwd(zb, do, biasv, comm=None):
    T = zb.shape[0]
    W = ATT_WIDTH
    n = T // ATT_TB
    n_rows = T // GRID_W
    cur = _att_specs(n, 0)[0]
    q_cur = _att_specs(n, 1)[0]
    k_cur, k_prv, k_nxt = _att_specs(n, 2)
    v_cur, v_prv, v_nxt = _att_specs(n, 3)
    done = pl.BlockSpec((ATT_TB, W), lambda i: (jnp.maximum(i - 1, 0), 0))
    bias_spec = pl.BlockSpec((ATT_HEADS, ATT_TB, ATT_KB), lambda i: (0, 0, 0))

    def body(q_ref, do_ref, kp_ref, kc_ref, kn_ref, vp_ref, vc_ref, vn_ref, b_ref,
             dq_ref, dk_ref, dv_ref, db_ref, acck_ref, accv_ref):
        i = pl.program_id(0)

        @pl.when(i == 0)
        def _():
            db_ref[...] = jnp.zeros_like(db_ref)
            acck_ref[...] = jnp.zeros_like(acck_ref)
            accv_ref[...] = jnp.zeros_like(accv_ref)

        @pl.when((i > 0) & (i < n))
        def _():
            slot = lax.rem(i + 1, 3)
            acck_ref[slot] = jnp.zeros((ATT_TB, W), F32)
            accv_ref[slot] = jnp.zeros((ATT_TB, W), F32)

        @pl.when(i < n)
        def _():
            valid = _att_valid(i, n_rows)
            qs = q_ref[...] * 0.125
            dob = do_ref[...]
            kb = jnp.concatenate([kp_ref[...], kc_ref[...], kn_ref[...]], axis=0)
            vb = jnp.concatenate([vp_ref[...], vc_ref[...], vn_ref[...]], axis=0)
            dqs, dks, dvs = [], [], []
            for h in range(ATT_HEADS):
                hs = slice(h * ATT_HEAD_DIM, (h + 1) * ATT_HEAD_DIM)
                qh, kh, vh, doh = qs[:, hs], kb[:, hs], vb[:, hs], dob[:, hs]
                p = _att_probs(qh, kh, b_ref[h], valid)
                dp = _dot_nt(doh, vh)
                ds = p * (dp - jnp.sum(p * dp, axis=1, keepdims=True))
                db_ref[h] += ds
                dsb = ds.astype(BF16)
                dqs.append(_dot(dsb, kh) * 0.125)
                dks.append(_dot_tn(dsb, qh))
                dvs.append(_dot_tn(p.astype(BF16), doh))
            dq_ref[...] = jnp.concatenate(dqs, axis=1).astype(BF16)
            dk_all = jnp.concatenate(dks, axis=1)
            dv_all = jnp.concatenate(dvs, axis=1)
            for b in range(3):
                slot = lax.rem(i + 2 + b, 3)
                rows = slice(b * ATT_TB, (b + 1) * ATT_TB)
                acck_ref[slot] += dk_all[rows]
                accv_ref[slot] += dv_all[rows]

        slot = lax.rem(i + 2, 3)
        dk_ref[...] = acck_ref[slot].astype(BF16)
        dv_ref[...] = accv_ref[slot].astype(BF16)

    return _pallas(
        body, "att_bwd", (n + 1,),
        [q_cur, cur, k_prv, k_cur, k_nxt, v_prv, v_cur, v_nxt, bias_spec],
        [cur, done, done, bias_spec],
        [jax.ShapeDtypeStruct((T, W), BF16)] * 3 + [jax.ShapeDtypeStruct((ATT_HEADS, ATT_TB, ATT_KB), F32)],
        [pltpu.VMEM((3, ATT_TB, W), F32), pltpu.VMEM((3, ATT_TB, W), F32)],
        ("arbitrary",), (zb, do, zb, zb, zb, zb, zb, zb, biasv), comm)


def _att_selectors():
    rsel = np.zeros((ATT_ROWS, 3 * ATT_ROWS, 2 * WIN_H - 1), np.float32)
    for a in range(ATT_ROWS):
        for b in range(3 * ATT_ROWS):
            rsel[a, b, b - a - ATT_ROWS + WIN_H - 1] = 1.0
    csel = np.zeros((GRID_W, GRID_W, 2 * WIN_W - 1), np.float32)
    for c in range(GRID_W):
        for x in range(GRID_W):
            csel[c, x, min(max(x - c, -(WIN_W - 1)), WIN_W - 1) + WIN_W - 1] = 1.0
    return rsel, csel


def _att_bias_table(rpb):
    rsel, csel = _att_selectors()
    hi = lax.Precision.HIGHEST
    t = jnp.einsum('hrd,abr->habd', rpb, rsel, precision=hi)
    t = jnp.einsum('habd,cxd->hacbx', t, csel, precision=hi)
    return t.reshape(ATT_HEADS, ATT_TB, ATT_KB)


def _att_bias_table_t(dtable):
    rsel, csel = _att_selectors()
    hi = lax.Precision.HIGHEST
    t = dtable.reshape(ATT_HEADS, ATT_ROWS, GRID_W, 3 * ATT_ROWS, GRID_W)
    t = jnp.einsum('hacbx,cxd->habd', t, csel, precision=hi)
    return jnp.einsum('habd,abr->hrd', t, rsel, precision=hi)


GELU_K = math.sqrt(2.0 / math.pi)
GELU_C = 0.044715
MERGE_TM = 256


def _gelu(x):
    return 0.5 * x * (1.0 + jnp.tanh(GELU_K * (x + GELU_C * x * x * x)))


def _gelu_grad(x):
    t = jnp.tanh(GELU_K * (x + GELU_C * x * x * x))
    return 0.5 * (1.0 + t) + 0.5 * x * (1.0 - t * t) * GELU_K * (1.0 + 3.0 * GELU_C * x * x)


def _merge_forward(ypre, zs, gs, ga, ya, ssm_d, w_glu, b_glu, w_bs, w_ba):
    ys = ypre + ssm_d * zs
    yg = _gelu(ys)
    sg = jax.nn.sigmoid(_dot(yg.astype(BF16), w_glu) + b_glu)
    y2 = yg * sg
    bs = _dot(y2.astype(BF16), w_bs)
    ba = _dot(ya, w_ba)
    s1 = jax.nn.sigmoid(gs)
    s2 = jax.nn.sigmoid(ga)
    merged = s1 * bs + s2 * ba
    return ys, yg, sg, y2, bs, ba, s1, s2, merged


def _merge_in_specs(D, W, tm):
    tok = lambda w, c: pl.BlockSpec((tm, w), lambda i: (i, c))
    full = lambda r, c: pl.BlockSpec((r, c), lambda i: (0, 0))
    z_specs = [tok(W, 0), tok(D, 4 * W // D), tok(D, 4 * W // D + 1)]
    w_specs = [full(1, W), full(W, W), full(1, W), full(W, D), full(W, D), full(D, D)]
    return tok, z_specs, w_specs


def _merge_fwd(ypre, z, ya, h1, ssm_d, w_glu, b_glu, w_bs, w_ba, w_out):
    T, D = h1.shape
    W = ypre.shape[1]
    tm = min(T, MERGE_TM)
    tok, z_specs, w_specs = _merge_in_specs(D, W, tm)

    def body(ypre_ref, zs_ref, gs_ref, ga_ref, ya_ref, h1_ref, d_ref, wglu_ref, bglu_ref, wbs_ref, wba_ref, wout_ref,
             h2_ref):
        merged = _merge_forward(ypre_ref[...], zs_ref[...], gs_ref[...], ga_ref[...], ya_ref[...], d_ref[...],
                                wglu_ref[...], bglu_ref[...], wbs_ref[...], wba_ref[...])[-1]
        h2_ref[...] = h1_ref[...] + _dot(merged.astype(BF16), wout_ref[...])

    return pl.pallas_call(
        body, name="merge_fwd", grid=(T // tm,),
        in_specs=[tok(W, 0)] + z_specs + [tok(W, 0), tok(D, 0)] + w_specs,
        out_specs=tok(D, 0),
        out_shape=jax.ShapeDtypeStruct((T, D), F32),
        compiler_params=_cparams(("parallel",), VMEM_LIMIT),
    )(ypre, z, z, z, ya, h1, ssm_d, w_glu, b_glu, w_bs, w_ba, w_out)


def _merge_bwd(dh2, ypre, z, ya, ssm_d, w_glu, b_glu, w_bs, w_ba, w_out):
    T, D = dh2.shape
    W = ypre.shape[1]
    tm = min(T, MERGE_TM)
    tok, z_specs, w_specs = _merge_in_specs(D, W, tm)

    def body(dh2_ref, ypre_ref, zs_ref, gs_ref, ga_ref, ya_ref, d_ref, wglu_ref, bglu_ref, wbs_ref, wba_ref, wout_ref,
             dypre_ref, dzs_ref, dgs_ref, dga_ref, dya_ref, dd_ref, dwglu_ref, dbglu_ref, dwbs_ref, dwba_ref, dwout_ref):
        @pl.when(pl.program_id(0) == 0)
        def _():
            for r in (dd_ref, dwglu_ref, dbglu_ref, dwbs_ref, dwba_ref, dwout_ref):
                r[...] = jnp.zeros_like(r)

        zs = zs_ref[...]
        ya = ya_ref[...]
        ys, yg, sg, y2, bs, ba, s1, s2, merged = _merge_forward(
            ypre_ref[...], zs, gs_ref[...], ga_ref[...], ya, d_ref[...],
            wglu_ref[...], bglu_ref[...], wbs_ref[...], wba_ref[...])
        dh2b = dh2_ref[...].astype(BF16)
        dmerged = _dot_nt(dh2b, wout_ref[...])
        dwout_ref[...] += _dot_tn(merged.astype(BF16), dh2b)
        dbs = (dmerged * s1).astype(BF16)
        dba = (dmerged * s2).astype(BF16)
        dgs_ref[...] = (dmerged * bs * s1 * (1.0 - s1)).astype(BF16)
        dga_ref[...] = (dmerged * ba * s2 * (1.0 - s2)).astype(BF16)
        dwbs_ref[...] += _dot_tn(y2.astype(BF16), dbs)
        dwba_ref[...] += _dot_tn(ya, dba)
        dya_ref[...] = _dot_nt(dba, wba_ref[...]).astype(BF16)
        dy2 = _dot_nt(dbs, wbs_ref[...])
        dvv = dy2 * yg * sg * (1.0 - sg)
        dvvb = dvv.astype(BF16)
        dyg = dy2 * sg + _dot_nt(dvvb, wglu_ref[...])
        dwglu_ref[...] += _dot_tn(yg.astype(BF16), dvvb)
        dbglu_ref[...] += _col_sum(dvv)
        dys = dyg * _gelu_grad(ys)
        dd_ref[...] += _col_sum(dys * zs)
        dzs_ref[...] = dys * d_ref[...]
        dypre_ref[...] = dys

    f32 = lambda *s: jax.ShapeDtypeStruct(s, F32)
    b16 = lambda *s: jax.ShapeDtypeStruct(s, BF16)
    return pl.pallas_call(
        body, name="merge_bwd", grid=(T // tm,),
        in_specs=[tok(D, 0), tok(W, 0)] + z_specs + [tok(W, 0)] + w_specs,
        out_specs=[tok(W, 0), tok(W, 0), tok(D, 0), tok(D, 0), tok(W, 0)] + w_specs,
        out_shape=[f32(T, W), f32(T, W), b16(T, D), b16(T, D), b16(T, W),
                   f32(1, W), f32(W, W), f32(1, W), f32(W, D), f32(W, D), f32(D, D)],
        compiler_params=_cparams(("arbitrary",), VMEM_LIMIT),
    )(dh2, ypre, z, z, z, ya, ssm_d, w_glu, b_glu, w_bs, w_ba, w_out)


def _gather_two_level(shards, name):
    n = len(shards)

    def body(*refs):
        x_refs, out_refs = refs[:n], refs[n:2 * n]
        send_sems, recv_sems, local_sems = refs[2 * n:]
        x, y, c = _my_place()
        me, sibling = (x, y, c), (x, y, 1 - c)
        chips = [(1 - x, y), (x, 1 - y), (1 - x, 1 - y)]

        def copy(a, k, block, to, own=False):
            slot = out_refs[a].at[_flat(*block)]
            return pltpu.make_async_remote_copy(
                src_ref=x_refs[a] if own else slot, dst_ref=slot,
                send_sem=send_sems.at[7 * a + k], recv_sem=recv_sems.at[7 * a + k],
                device_id=to, device_id_type=MESH_ID)

        sent, local = [], []
        for a in range(n):
            local.append(pltpu.make_async_copy(x_refs[a], out_refs[a].at[_flat(*me)], local_sems.at[a]))
            local[-1].start()
            sent.append(copy(a, 0, me, sibling, own=True))
            sent += [copy(a, 1 + j, me, (*chip, c), own=True) for j, chip in enumerate(chips)]
        for cp in sent:
            cp.start()
        for a in range(n):
            for j, chip in enumerate(chips):
                copy(a, 1 + j, (*chip, c), me).wait_recv()
                sent.append(copy(a, 4 + j, (*chip, c), sibling))
                sent[-1].start()
        for a in range(n):
            copy(a, 0, sibling, me).wait_recv()
            for j, chip in enumerate(chips):
                copy(a, 4 + j, (*chip, 1 - c), me).wait_recv()
        for cp in sent:
            cp.wait_send()
        for cp in local:
            cp.wait()

    return pl.pallas_call(
        body, name=name, in_specs=[_HBM] * n, out_specs=[_HBM] * n,
        out_shape=[jax.ShapeDtypeStruct((N_DEV,) + s.shape, s.dtype) for s in shards],
        scratch_shapes=[pltpu.SemaphoreType.DMA((7 * n,)), pltpu.SemaphoreType.DMA((7 * n,)),
                        pltpu.SemaphoreType.DMA((n,))],
    )(*shards)


PACK_COLS = 1024
BIG = (("ffn1_w_gate", 1), ("ffn1_w_up", 1), ("ffn1_w_down", 0), ("w_in", 1), ("ssm_w_glu", 0),
       ("w_branch_ssm", 1), ("w_branch_att", 1), ("w_out", 0),
       ("ffn2_w_gate", 1), ("ffn2_w_up", 1), ("ffn2_w_down", 0))
BIG_AXIS = dict(BIG)
SSM_DIR = ("ssm_a_re", "ssm_a_im", "ssm_log_dt", "ssm_b_re", "ssm_b_im", "ssm_c_re", "ssm_c_im")
SMALL_EARLY = (("mix_norm",) + tuple(n + "_fwd" for n in SSM_DIR) + tuple(n + "_bwd" for n in SSM_DIR)
               + ("ssm_d", "ssm_b_glu", "att_rpb", "ffn2_norm", "final_norm"))
SMALL_LATE = ("ffn1_norm",)
WEIGHTS = ("ffn1_norm", "ffn1_w_gate", "ffn1_w_up", "ffn1_w_down", "mix_norm", "w_in") \
    + tuple(n + "_fwd" for n in SSM_DIR) + tuple(n + "_bwd" for n in SSM_DIR) \
    + ("ssm_d", "ssm_w_glu", "ssm_b_glu", "att_rpb", "w_branch_ssm", "w_branch_att", "w_out",
       "ffn2_norm", "ffn2_w_gate", "ffn2_w_up", "ffn2_w_down", "final_norm")


def _pad_rows(a, mult):
    pad = (-a.shape[-2]) % mult
    if pad:
        a = jnp.concatenate([a, jnp.zeros(a.shape[:-2] + (pad, a.shape[-1]), a.dtype)], axis=-2)
    return a


def _pack(arrays, row_mult):
    flat = jnp.concatenate([a.reshape(-1) for a in arrays])
    pad = (-flat.shape[0]) % PACK_COLS
    if pad:
        flat = jnp.concatenate([flat, jnp.zeros((pad,), flat.dtype)])
    return _pad_rows(flat.reshape(-1, PACK_COLS), row_mult)


def _unpack(slab, shapes):
    flat = slab.reshape(-1)
    out, at = [], 0
    for s in shapes:
        n = int(np.prod(s))
        out.append(flat[at:at + n].reshape(s))
        at += n
    return out


def _split_for_devices(g, axis):
    r, c = g.shape
    if axis == 1:
        return g.reshape(r, N_DEV, c // N_DEV).transpose(1, 0, 2).astype(BF16)
    return g.reshape(N_DEV, r // N_DEV, c).astype(BF16)


def _join_shards(gathered, axis):
    _, r, c = gathered.shape
    if axis == 1:
        return gathered.transpose(1, 0, 2).reshape(r, N_DEV * c)
    return gathered.reshape(N_DEV * r, c)


def _s5_direction_inputs(p, sfx, chain_len):
    bt_re = p["ssm_b_re" + sfx][0].transpose(0, 2, 1)
    bt_im = p["ssm_b_im" + sfx][0].transpose(0, 2, 1)
    raw = (p["ssm_a_re" + sfx][0], p["ssm_a_im" + sfx][0], p["ssm_log_dt" + sfx][0][:, None], bt_re, bt_im)
    lr, li, sr, si, bbr, bbi = _disc_fwd(*raw, chain_len,"s5_disc" + sfx)
    lam = jnp.stack([_s5_pack_lam(t) for t in (lr, li, sr, si)], axis=2)
    mats = (_s5_pack_b(bbr), _s5_pack_b(bbi), lam,
            _s5_pack_c(p["ssm_c_re" + sfx][0]), _s5_pack_c(-p["ssm_c_im" + sfx][0]))
    return raw, mats


def kernel(x, ffn1_norm, ffn1_w_gate, ffn1_w_up, ffn1_w_down, mix_norm, w_in, ssm_a_re_fwd, ssm_a_im_fwd, ssm_log_dt_fwd, ssm_b_re_fwd, ssm_b_im_fwd, ssm_c_re_fwd, ssm_c_im_fwd, ssm_a_re_bwd, ssm_a_im_bwd, ssm_log_dt_bwd, ssm_b_re_bwd, ssm_b_im_bwd, ssm_c_re_bwd, ssm_c_im_bwd, ssm_d, ssm_w_glu, ssm_b_glu, att_rpb, w_branch_ssm, w_branch_att, w_out, ffn2_norm, ffn2_w_gate, ffn2_w_up, ffn2_w_down, final_norm, loss_target, m_ffn1_norm, m_ffn1_w_gate, m_ffn1_w_up, m_ffn1_w_down, m_mix_norm, m_w_in, m_ssm_a_re_fwd, m_ssm_a_im_fwd, m_ssm_log_dt_fwd, m_ssm_b_re_fwd, m_ssm_b_im_fwd, m_ssm_c_re_fwd, m_ssm_c_im_fwd, m_ssm_a_re_bwd, m_ssm_a_im_bwd, m_ssm_log_dt_bwd, m_ssm_b_re_bwd, m_ssm_b_im_bwd, m_ssm_c_re_bwd, m_ssm_c_im_bwd, m_ssm_d, m_ssm_w_glu, m_ssm_b_glu, m_att_rpb, m_w_branch_ssm, m_w_branch_att, m_w_out, m_ffn2_norm, m_ffn2_w_gate, m_ffn2_w_up, m_ffn2_w_down, m_final_norm, v_ffn1_norm, v_ffn1_w_gate, v_ffn1_w_up, v_ffn1_w_down, v_mix_norm, v_w_in, v_ssm_a_re_fwd, v_ssm_a_im_fwd, v_ssm_log_dt_fwd, v_ssm_b_re_fwd, v_ssm_b_im_fwd, v_ssm_c_re_fwd, v_ssm_c_im_fwd, v_ssm_a_re_bwd, v_ssm_a_im_bwd, v_ssm_log_dt_bwd, v_ssm_b_re_bwd, v_ssm_b_im_bwd, v_ssm_c_re_bwd, v_ssm_c_im_bwd, v_ssm_d, v_ssm_w_glu, v_ssm_b_glu, v_att_rpb, v_w_branch_ssm, v_w_branch_att, v_w_out, v_ffn2_norm, v_ffn2_w_gate, v_ffn2_w_up, v_ffn2_w_down, v_final_norm):
    p = dict(locals())
    x = p["x"][0]
    target = p["loss_target"][0]
    T, D = x.shape

    shard = {n: p[n][0].astype(BF16) for n, _ in BIG}
    ffn1_w = ("ffn1_w_gate", "ffn1_w_up", "ffn1_w_down")
    mix_w = ("w_in", "ssm_w_glu", "w_branch_ssm", "w_branch_att", "w_out")
    ffn2_w = ("ffn2_w_gate", "ffn2_w_up", "ffn2_w_down")
    gathered = dict(zip(ffn1_w, _gather_two_level([shard[n] for n in ffn1_w], "gather_ffn1")))
    full = lambda n: _join_shards(gathered[n], BIG_AXIS[n])

    h0 = x
    wg1, wu1, wd1 = [full(n) for n in ffn1_w]
    (h1, xn1, g1, u1), got = _ffn_fwd(h0, p["ffn1_norm"], wg1, wu1, wd1, "ffn1_fwd",
                                      _Comm("gather", [shard[n] for n in mix_w]))
    gathered.update(zip(mix_w, got))
    z, zb, un = _mixin_fwd(h1, p["mix_norm"], gathered["w_in"])
    W = SSM_WIDTH
    zp = _permute_rows(zb[:, :W])
    chain_len = T // SCAN_LANES // S5_NQ
    raw_f, mats_f = _s5_direction_inputs(p, "_fwd", chain_len)
    raw_b, mats_b = _s5_direction_inputs(p, "_bwd", chain_len)
    bre, bim, lam, cre, cimn = [jnp.stack([f, b]) for f, b in zip(mats_f, mats_b)]
    bre, bim, cre, cimn = [t.astype(BF16) for t in (bre, bim, cre, cimn)]
    (yp,), got = _s5_fwd(zp, bre, bim, lam, cre, cimn, _Comm("gather", [shard[n] for n in ffn2_w]))
    gathered.update(zip(ffn2_w, got))
    ypre = _unpermute_rows(yp)
    table = _att_bias_table(p["att_rpb"][0])
    ya = _att_fwd(zb, table)
    tail_w = (p["ssm_d"], full("ssm_w_glu"), p["ssm_b_glu"], full("w_branch_ssm"), full("w_branch_att"), full("w_out"))
    h2 = _merge_fwd(ypre, z, ya, h1, *tail_w)
    wg2, wu2, wd2 = [full(n) for n in ffn2_w]
    (h3, xn2, g2, u2), _ = _ffn_fwd(h2, p["ffn2_norm"], wg2, wu2, wd2, "ffn2_fwd")
    loss_part, dh3, d_final = _loss_head(h3, p["final_norm"][None], target)

    grads = {"final_norm": d_final[0]}
    to_send = lambda names: _Comm("exchange", [_split_for_devices(grads[n], BIG_AXIS[n]) for n in names])
    parts = {}
    (dh2, grads["ffn2_norm"], do2, a2, dg2, du2), _ = _ffn_bwd(
        dh3, h2, p["ffn2_norm"], g2, u2, wg2, wu2, wd2, "ffn2_bwd")
    grads["ffn2_w_gate"] = _xty(xn2, dg2, "ffn2_dw_gate")
    grads["ffn2_w_up"] = _xty(xn2, du2, "ffn2_dw_up")
    grads["ffn2_w_down"] = _xty(a2, do2, "ffn2_dw_down")
    (dypre, dzs_skip, dgs, dga, dya, grads["ssm_d"], grads["ssm_w_glu"], grads["ssm_b_glu"],
     grads["w_branch_ssm"], grads["w_branch_att"], grads["w_out"]) = _merge_bwd(dh2, ypre, z, ya, *tail_w)
    (dq, dk, dv, dtable), got = _att_bwd(zb, dya, table, to_send(ffn2_w))
    parts.update(zip(ffn2_w, got))
    grads["att_rpb"] = _att_bias_table_t(dtable)
    dyp = _permute_rows(dypre).astype(BF16)
    tail_names = ("ssm_w_glu", "w_branch_ssm", "w_branch_att", "w_out")
    (dzp, dbre, dbim, dlam, dcre, dcimn), got = _s5_bwd(zp, dyp, bre, bim, lam, cre, cimn, to_send(tail_names))
    parts.update(zip(tail_names, got))
    G, P = SSM_GROUPS, SSM_STATE
    for d, (sfx, raw) in enumerate((("_fwd", raw_f), ("_bwd", raw_b))):
        da_re, da_im, dldt, dbt_re, dbt_im = _disc_bwd(
            *raw, dlam[d, :, :, 0, :].reshape(G, P), dlam[d, :, :, 1, :].reshape(G, P),
            _s5_unpack_b(dbre[d]), _s5_unpack_b(dbim[d]), "s5_disc_grad" + sfx)
        grads["ssm_a_re" + sfx] = da_re
        grads["ssm_a_im" + sfx] = da_im
        grads["ssm_log_dt" + sfx] = dldt[:, 0]
        grads["ssm_b_re" + sfx] = dbt_re.transpose(0, 2, 1)
        grads["ssm_b_im" + sfx] = dbt_im.transpose(0, 2, 1)
        grads["ssm_c_re" + sfx] = _s5_unpack_c(dcre[d])
        grads["ssm_c_im" + sfx] = -_s5_unpack_c(dcimn[d])
    dzs = _unpermute_rows(dzp) + dzs_skip
    dz = jnp.concatenate([dzs.astype(BF16), dq, dk, dv, dgs, dga], axis=1)
    dh1, grads["mix_norm"] = _mixin_bwd(dz, dh2, h1, p["mix_norm"], gathered["w_in"])
    grads["w_in"] = _xty(un, dz, "dw_in")
    pack_small = lambda names, src, pre: _pack([src[pre + n].astype(F32) for n in names], 8)
    early = _Comm(["exchange", "gather"],
                  [_split_for_devices(grads["w_in"], 1), pack_small(SMALL_EARLY, grads, "")])
    (dh0, grads["ffn1_norm"], do1, a1, dg1, du1), (parts["w_in"], got_early) = _ffn_bwd(
        dh1, h0, p["ffn1_norm"], g1, u1, wg1, wu1, wd1, "ffn1_bwd", early)
    grads["ffn1_w_down"] = _xty(a1, do1, "ffn1_dw_down")
    grads["ffn1_w_gate"], (parts["ffn1_w_down"],) = _xty(xn1, dg1, "ffn1_dw_gate", to_send(("ffn1_w_down",)))
    grads["ffn1_w_up"], (parts["ffn1_w_gate"],) = _xty(xn1, du1, "ffn1_dw_up", to_send(("ffn1_w_gate",)))
    last = _Comm(["exchange", "gather"],
                 [_split_for_devices(grads["ffn1_w_up"], 1), pack_small(SMALL_LATE, grads, "")])
    parts["ffn1_w_up"], got_late = _comm_call(last, "exchange_last")
    got_small = jnp.concatenate([got_early, got_late], axis=1)

    results = {}
    for n, _ in BIG:
        outs = _adamw(parts[n], p[n][0], p["m_" + n][0], p["v_" + n][0], "adamw_" + n)
        results[n] = [o[None] for o in outs]
    early_rows = got_early.shape[1]
    slab = lambda pre: jnp.concatenate([pack_small(SMALL_EARLY, p, pre), pack_small(SMALL_LATE, p, pre)], axis=0)
    small_out = _adamw(got_small, slab(""), slab("m_"), slab("v_"), "adamw_small")
    for names, rows in ((SMALL_EARLY, slice(0, early_rows)), (SMALL_LATE, slice(early_rows, None))):
        shapes = [p[n].shape for n in names]
        for n, vals in zip(names, zip(*[_unpack(out[rows], shapes) for out in small_out])):
            results[n] = list(vals)

    loss = lax.psum(loss_part[0, 0], ("x", "y", "c"))
    out = [loss, dh0[None]]
    for kind in range(4):
        out += [results[n][kind] for n in WEIGHTS]
    return tuple(out)
```

```python
import functools
import math

import numpy as np
import jax
import jax.numpy as jnp
from jax import lax
from jax.experimental import pallas as pl
from jax.experimental.pallas import tpu as pltpu

F32 = jnp.float32
BF16 = jnp.bfloat16
MESH_ID = pl.DeviceIdType.MESH

SSM_GROUP = 16
SSM_GROUPS = 32
SSM_STATE = 64
SSM_WIDTH = 512
ATT_HEADS = 8
ATT_HEAD_DIM = 64
ATT_WIDTH = 512
GRID_W = 64
WIN_H = 8
WIN_W = 16
EPS = 1e-6
NEG_INF = -1e30
ADAM_LR = 0.001
ADAM_B1 = 0.9
ADAM_B2 = 0.999
ADAM_EPS = 1e-08
ADAM_WD = 0.01
ADAM_STEP = 10

N_DEV = 8
V7X_VMEM_BYTES = 64 * 1024 * 1024
VMEM_LIMIT = V7X_VMEM_BYTES - 8 * 1024 * 1024
SCAN_LANES = 8
ATT_ROWS = 4


def _cparams(sem, vmem=None):
    return pltpu.CompilerParams(dimension_semantics=sem, vmem_limit_bytes=vmem)


def _dot(a, b):
    return jnp.dot(a, b, preferred_element_type=F32)


def _dot_nt(a, b):
    return lax.dot_general(a, b, (((1,), (1,)), ((), ())), preferred_element_type=F32)


def _dot_tn(a, b):
    return lax.dot_general(a, b, (((0,), (0,)), ((), ())), preferred_element_type=F32)


def _rms(h):
    return lax.rsqrt(jnp.mean(h * h, axis=-1, keepdims=True) + EPS)


def _rms_bwd(h, r, v):
    return r * v - h * (r * r * r) * jnp.mean(h * v, axis=-1, keepdims=True)


def _col_sum(x):
    return jnp.sum(x, axis=0, keepdims=True)


def _my_place():
    return lax.axis_index("x"), lax.axis_index("y"), lax.axis_index("c")


def _flat(px, py, pc):
    return 4 * px + 2 * py + pc


class _Comm:
    def __init__(self, kind, arrays):
        self.arrays = list(arrays)
        self.n = len(self.arrays)
        self.kinds = [kind] * self.n if isinstance(kind, str) else list(kind)

    def out_shapes(self):
        return [jax.ShapeDtypeStruct((N_DEV,) + a.shape if k == "gather" else a.shape, a.dtype)
                for k, a in zip(self.kinds, self.arrays)]

    def scratch(self):
        return [pltpu.SemaphoreType.DMA((7 * self.n,)), pltpu.SemaphoreType.DMA((7 * self.n,)),
                pltpu.SemaphoreType.DMA((self.n,))]

    def run(self, srcs, dsts, sems, start):
        send_sems, recv_sems, local_sems = sems
        x, y, c = _my_place()
        mine = _flat(x, y, c)
        for a, (src, dst) in enumerate(zip(srcs, dsts)):
            whole = self.kinds[a] == "gather"
            local = pltpu.make_async_copy(src if whole else src.at[mine], dst.at[mine], local_sems.at[a])
            local.start() if start else local.wait()
            for k in range(1, N_DEV):
                px = 1 - x if k & 4 else x
                py = 1 - y if k & 2 else y
                pc = 1 - c if k & 1 else c
                cp = pltpu.make_async_remote_copy(
                    src_ref=src if whole else src.at[_flat(px, py, pc)], dst_ref=dst.at[mine],
                    send_sem=send_sems.at[7 * a + k - 1], recv_sem=recv_sems.at[7 * a + k - 1],
                    device_id=(px, py, pc), device_id_type=MESH_ID)
                cp.start() if start else cp.wait()


_HBM = pl.BlockSpec(memory_space=pltpu.HBM)


def _comm_call(comm, name):
    def body(*refs):
        srcs, dsts, sems = refs[:comm.n], refs[comm.n:2 * comm.n], refs[2 * comm.n:]
        comm.run(srcs, dsts, sems, True)
        comm.run(srcs, dsts, sems, False)

    return pl.pallas_call(body, name=name, in_specs=[_HBM] * comm.n, out_specs=[_HBM] * comm.n,
                          out_shape=comm.out_shapes(), scratch_shapes=comm.scratch())(*comm.arrays)


def _pallas(core, name, grid, in_specs, out_specs, out_shape, scratch, sem, args, comm=None):
    if comm is None:
        out = pl.pallas_call(core, name=name, grid=grid, in_specs=in_specs, out_specs=out_specs,
                             out_shape=out_shape, scratch_shapes=scratch,
                             compiler_params=_cparams(sem, VMEM_LIMIT))(*args)
        return out, []
    n_in, n_out, n_scr, n = len(in_specs), len(out_specs), len(scratch), comm.n

    def body(*refs):
        ins, srcs = refs[:n_in], refs[n_in:n_in + n]
        outs, dsts = refs[n_in + n:n_in + n + n_out], refs[n_in + n + n_out:n_in + 2 * n + n_out]
        scr, sems = refs[n_in + 2 * n + n_out:n_in + 2 * n + n_out + n_scr], refs[n_in + 2 * n + n_out + n_scr:]
        ids = [pl.program_id(k) for k in range(len(grid))]
        first = functools.reduce(lambda a, b: a & b, [i == 0 for i in ids])
        last = functools.reduce(lambda a, b: a & b, [i == g - 1 for i, g in zip(ids, grid)])

        @pl.when(first)
        def _():
            comm.run(srcs, dsts, sems, True)

        core(*ins, *outs, *scr)

        @pl.when(last)
        def _():
            comm.run(srcs, dsts, sems, False)

    out = pl.pallas_call(
        body, name=name, grid=grid, in_specs=list(in_specs) + [_HBM] * n, out_specs=list(out_specs) + [_HBM] * n,
        out_shape=list(out_shape) + comm.out_shapes(), scratch_shapes=list(scratch) + comm.scratch(),
        compiler_params=_cparams(("arbitrary",) * len(grid), VMEM_LIMIT))(*args, *comm.arrays)
    return out[:n_out], out[n_out:]


def _ffn_tiles(T, F, wide):
    if wide and F % 1408 == 0:
        return min(T, 512), 1408
    return min(T, 1024), 256 if F % 256 == 0 else F


def _ffn_fwd(h, gain, wg, wu, wd, name, comm=None):
    T, D = h.shape
    F = wg.shape[1]
    tm, tf = _ffn_tiles(T, F, wide=True)
    nj = F // tf

    def body(h_ref, gain_ref, wg_ref, wu_ref, wd_ref, ho_ref, xn_ref, g_ref, u_ref, acc_ref):
        j = pl.program_id(1)

        @pl.when(j == 0)
        def _():
            hh = h_ref[...]
            xn_ref[...] = (hh * _rms(hh) * gain_ref[...]).astype(BF16)
            acc_ref[...] = jnp.zeros_like(acc_ref)

        xn = xn_ref[...]
        g = _dot(xn, wg_ref[...])
        u = _dot(xn, wu_ref[...])
        g_ref[...] = g.astype(BF16)
        u_ref[...] = u.astype(BF16)
        a = (g * jax.nn.sigmoid(g) * u).astype(BF16)
        acc_ref[...] += _dot(a, wd_ref[...])

        @pl.when(j == nj - 1)
        def _():
            ho_ref[...] = h_ref[...] + 0.5 * acc_ref[...]

    return _pallas(
        body, name, (T // tm, nj),
        [pl.BlockSpec((tm, D), lambda i, j: (i, 0)),
         pl.BlockSpec((1, D), lambda i, j: (0, 0)),
         pl.BlockSpec((D, tf), lambda i, j: (0, j)),
         pl.BlockSpec((D, tf), lambda i, j: (0, j)),
         pl.BlockSpec((tf, D), lambda i, j: (j, 0))],
        [pl.BlockSpec((tm, D), lambda i, j: (i, 0)),
         pl.BlockSpec((tm, D), lambda i, j: (i, 0)),
         pl.BlockSpec((tm, tf), lambda i, j: (i, j)),
         pl.BlockSpec((tm, tf), lambda i, j: (i, j))],
        [jax.ShapeDtypeStruct((T, D), F32), jax.ShapeDtypeStruct((T, D), BF16),
         jax.ShapeDtypeStruct((T, F), BF16), jax.ShapeDtypeStruct((T, F), BF16)],
        [pltpu.VMEM((tm, D), F32)], ("parallel", "arbitrary"), (h, gain, wg, wu, wd), comm)


def _ffn_bwd(dho, h, gain, g, u, wg, wu, wd, name, comm=None):
    T, D = h.shape
    F = wg.shape[1]
    tm, tf = _ffn_tiles(T, F, wide=False)
    nj = F // tf

    def body(dho_ref, h_ref, gain_ref, g_ref, u_ref, wg_ref, wu_ref, wd_ref,
             dh_ref, dgain_ref, do_ref, a_ref, dg_ref, du_ref, acc_ref):
        i = pl.program_id(0)
        j = pl.program_id(1)

        @pl.when(j == 0)
        def _():
            do_ref[...] = (0.5 * dho_ref[...]).astype(BF16)
            acc_ref[...] = jnp.zeros_like(acc_ref)

        @pl.when((i == 0) & (j == 0))
        def _():
            dgain_ref[...] = jnp.zeros_like(dgain_ref)

        da = _dot_nt(do_ref[...], wd_ref[...])
        gg = g_ref[...].astype(F32)
        uu = u_ref[...].astype(F32)
        s = jax.nn.sigmoid(gg)
        sl = gg * s
        a_ref[...] = (sl * uu).astype(BF16)
        dg = (da * uu * (s * (1.0 + gg * (1.0 - s)))).astype(BF16)
        du = (da * sl).astype(BF16)
        dg_ref[...] = dg
        du_ref[...] = du
        acc_ref[...] += _dot_nt(dg, wg_ref[...]) + _dot_nt(du, wu_ref[...])

        @pl.when(j == nj - 1)
        def _():
            hh = h_ref[...]
            r = _rms(hh)
            dxn = acc_ref[...]
            dgain_ref[...] += _col_sum(dxn * hh * r)
            dh_ref[...] = dho_ref[...] + _rms_bwd(hh, r, dxn * gain_ref[...])

    return _pallas(
        body, name, (T // tm, nj),
        [pl.BlockSpec((tm, D), lambda i, j: (i, 0)),
         pl.BlockSpec((tm, D), lambda i, j: (i, 0)),
         pl.BlockSpec((1, D), lambda i, j: (0, 0)),
         pl.BlockSpec((tm, tf), lambda i, j: (i, j)),
         pl.BlockSpec((tm, tf), lambda i, j: (i, j)),
         pl.BlockSpec((D, tf), lambda i, j: (0, j)),
         pl.BlockSpec((D, tf), lambda i, j: (0, j)),
         pl.BlockSpec((tf, D), lambda i, j: (j, 0))],
        [pl.BlockSpec((tm, D), lambda i, j: (i, 0)),
         pl.BlockSpec((1, D), lambda i, j: (0, 0)),
         pl.BlockSpec((tm, D), lambda i, j: (i, 0)),
         pl.BlockSpec((tm, tf), lambda i, j: (i, j)),
         pl.BlockSpec((tm, tf), lambda i, j: (i, j)),
         pl.BlockSpec((tm, tf), lambda i, j: (i, j))],
        [jax.ShapeDtypeStruct((T, D), F32), jax.ShapeDtypeStruct((1, D), F32),
         jax.ShapeDtypeStruct((T, D), BF16), jax.ShapeDtypeStruct((T, F), BF16),
         jax.ShapeDtypeStruct((T, F), BF16), jax.ShapeDtypeStruct((T, F), BF16)],
        [pltpu.VMEM((tm, D), F32)], ("arbitrary", "arbitrary"), (dho, h, gain, g, u, wg, wu, wd), comm)


def _xty(x, y, name, comm=None):
    T, K = x.shape
    N = y.shape[1]
    tt = min(T, 1024)
    tk = K if K <= 1024 else (1408 if K % 1408 == 0 else K)
    tn = N if N <= 1024 else (1408 if N % 1408 == 0 else (1024 if N % 1024 == 0 else N))
    nt = T // tt

    def body(x_ref, y_ref, o_ref):
        t = pl.program_id(2)

        @pl.when(t == 0)
        def _():
            o_ref[...] = jnp.zeros_like(o_ref)

        o_ref[...] += _dot_tn(x_ref[...], y_ref[...])

    (out,), got = _pallas(
        body, name, (K // tk, N // tn, nt),
        [pl.BlockSpec((tt, tk), lambda k, n, t: (t, k)), pl.BlockSpec((tt, tn), lambda k, n, t: (t, n))],
        [pl.BlockSpec((tk, tn), lambda k, n, t: (k, n))], [jax.ShapeDtypeStruct((K, N), F32)], [],
        ("parallel", "parallel", "arbitrary"), (x, y), comm)
    return out if comm is None else (out, got)


def _mixin_fwd(h, gain, w_in):
    T, D = h.shape
    nn, _, tn = w_in.shape
    N = nn * tn
    tm = min(T, 1024)

    def body(h_ref, gain_ref, w_ref, z_ref, zb_ref, un_ref):
        @pl.when(pl.program_id(1) == 0)
        def _():
            hh = h_ref[...]
            un_ref[...] = (hh * _rms(hh) * gain_ref[...]).astype(BF16)

        z = _dot(un_ref[...], w_ref[...])
        z_ref[...] = z
        zb_ref[...] = z.astype(BF16)

    return pl.pallas_call(
        body, name="mixin_fwd", grid=(T // tm, nn),
        in_specs=[pl.BlockSpec((tm, D), lambda i, n: (i, 0)),
                  pl.BlockSpec((1, D), lambda i, n: (0, 0)),
                  pl.BlockSpec((None, D, tn), lambda i, n: (n, 0, 0))],
        out_specs=[pl.BlockSpec((tm, tn), lambda i, n: (i, n)),
                   pl.BlockSpec((tm, tn), lambda i, n: (i, n)),
                   pl.BlockSpec((tm, D), lambda i, n: (i, 0))],
        out_shape=[jax.ShapeDtypeStruct((T, N), F32), jax.ShapeDtypeStruct((T, N), BF16),
                   jax.ShapeDtypeStruct((T, D), BF16)],
        compiler_params=_cparams(("parallel", "arbitrary"), VMEM_LIMIT),
    )(h, gain, w_in)


def _mixin_bwd(dz, dh_res, h, gain, w_in):
    T, D = h.shape
    nn, _, tn = w_in.shape
    tm = min(T, 1024)

    def body(dz_ref, dres_ref, h_ref, gain_ref, w_ref, dh_ref, dgain_ref, acc_ref):
        i = pl.program_id(0)
        n = pl.program_id(1)

        @pl.when(n == 0)
        def _():
            acc_ref[...] = jnp.zeros_like(acc_ref)

        @pl.when((i == 0) & (n == 0))
        def _():
            dgain_ref[...] = jnp.zeros_like(dgain_ref)

        acc_ref[...] += _dot_nt(dz_ref[...], w_ref[...])

        @pl.when(n == nn - 1)
        def _():
            hh = h_ref[...]
            r = _rms(hh)
            dun = acc_ref[...]
            dgain_ref[...] += _col_sum(dun * hh * r)
            dh_ref[...] = dres_ref[...] + _rms_bwd(hh, r, dun * gain_ref[...])

    return pl.pallas_call(
        body, name="mixin_bwd", grid=(T // tm, nn),
        in_specs=[pl.BlockSpec((tm, tn), lambda i, n: (i, n)),
                  pl.BlockSpec((tm, D), lambda i, n: (i, 0)),
                  pl.BlockSpec((tm, D), lambda i, n: (i, 0)),
                  pl.BlockSpec((1, D), lambda i, n: (0, 0)),
                  pl.BlockSpec((None, D, tn), lambda i, n: (n, 0, 0))],
        out_specs=[pl.BlockSpec((tm, D), lambda i, n: (i, 0)),
                   pl.BlockSpec((1, D), lambda i, n: (0, 0))],
        out_shape=[jax.ShapeDtypeStruct((T, D), F32), jax.ShapeDtypeStruct((1, D), F32)],
        scratch_shapes=[pltpu.VMEM((tm, D), F32)],
        compiler_params=_cparams(("arbitrary", "arbitrary"), VMEM_LIMIT),
    )(dz, dh_res, h, gain, w_in)


def _loss_head(h, gain, target):
    T, D = h.shape
    tm = min(T, 1024)

    def body(h_ref, gain_ref, t_ref, loss_ref, dh_ref, dgain_ref):
        @pl.when(pl.program_id(0) == 0)
        def _():
            loss_ref[...] = jnp.zeros_like(loss_ref)
            dgain_ref[...] = jnp.zeros_like(dgain_ref)

        hh = h_ref[...]
        r = _rms(hh)
        e = hh * r * gain_ref[...] - t_ref[...]
        loss_ref[...] += (0.5 / D) * jnp.sum(e * e)
        dy = e * (1.0 / D)
        dgain_ref[...] += _col_sum(dy * hh * r)
        dh_ref[...] = _rms_bwd(hh, r, dy * gain_ref[...])

    return pl.pallas_call(
        body, name="loss_head", grid=(T // tm,),
        in_specs=[pl.BlockSpec((tm, D), lambda i: (i, 0)),
                  pl.BlockSpec((1, D), lambda i: (0, 0)),
                  pl.BlockSpec((tm, D), lambda i: (i, 0))],
        out_specs=[pl.BlockSpec((1, 128), lambda i: (0, 0)),
                   pl.BlockSpec((tm, D), lambda i: (i, 0)),
                   pl.BlockSpec((1, D), lambda i: (0, 0))],
        out_shape=[jax.ShapeDtypeStruct((1, 128), F32), jax.ShapeDtypeStruct((T, D), F32),
                   jax.ShapeDtypeStruct((1, D), F32)],
        compiler_params=_cparams(("arbitrary",), VMEM_LIMIT),
    )(h, gain, target)


def _adamw(parts, w, m, v, name):
    R, C = w.shape
    mult = 16 if parts.dtype == BF16 else 8
    tr = max(t for t in range(mult, min(R, 512) + 1, mult) if R % t == 0)
    c1 = 1.0 - ADAM_B1 ** ADAM_STEP
    c2 = 1.0 - ADAM_B2 ** ADAM_STEP

    def body(p_ref, w_ref, m_ref, v_ref, g_ref, d_ref, nm_ref, nv_ref):
        g = p_ref[0].astype(F32)
        for k in range(1, N_DEV):
            g = g + p_ref[k].astype(F32)
        mm = ADAM_B1 * m_ref[...] + (1.0 - ADAM_B1) * g
        vv = ADAM_B2 * v_ref[...] + (1.0 - ADAM_B2) * (g * g)
        g_ref[...] = g
        nm_ref[...] = mm
        nv_ref[...] = vv
        d_ref[...] = -ADAM_LR * ((mm / c1) / (jnp.sqrt(vv / c2) + ADAM_EPS) + ADAM_WD * w_ref[...])

    spec = pl.BlockSpec((tr, C), lambda i: (i, 0))
    return pl.pallas_call(
        body, name=name, grid=(R // tr,),
        in_specs=[pl.BlockSpec((N_DEV, tr, C), lambda i: (0, i, 0)), spec, spec, spec],
        out_specs=[spec, spec, spec, spec],
        out_shape=[jax.ShapeDtypeStruct((R, C), F32)] * 4,
        compiler_params=_cparams(("parallel",), VMEM_LIMIT),
    )(parts, w, m, v)


S5_NS = 256
S5_NH = 2
S5_NCB = 4
S5_RC = 512
S5_NQ = 4


def _disc_math(a_re, a_im, log_dt, bt_re, bt_im):
    dt = jnp.exp(log_dt)
    zr, zi = a_re * dt, a_im * dt
    mag = jnp.exp(zr)
    lb_re, lb_im = mag * jnp.cos(zi), mag * jnp.sin(zi)
    den = a_re * a_re + a_im * a_im
    nr, ni = lb_re - 1.0, lb_im
    f_re = (nr * a_re + ni * a_im) / den
    f_im = (ni * a_re - nr * a_im) / den
    bb_re = f_re[:, None, :] * bt_re - f_im[:, None, :] * bt_im
    bb_im = f_re[:, None, :] * bt_im + f_im[:, None, :] * bt_re
    return lb_re, lb_im, bb_re, bb_im


def _disc_fwd(a_re, a_im, log_dt, bt_re, bt_im, chain_len, name):
    G, P = a_re.shape
    C = bt_re.shape[1]
    n_sq = int(round(math.log2(chain_len)))
    assert 2 ** n_sq == chain_len

    def body(a_re_ref, a_im_ref, ldt_ref, br_ref, bi_ref, lr_ref, li_ref, sr_ref, si_ref, bbr_ref, bbi_ref):
        lr, li, bbr, bbi = _disc_math(a_re_ref[...], a_im_ref[...], ldt_ref[...], br_ref[...], bi_ref[...])
        lr_ref[...] = lr
        li_ref[...] = li
        bbr_ref[...] = bbr
        bbi_ref[...] = bbi
        pr, pi = lr, li
        for _ in range(n_sq):
            pr, pi = pr * pr - pi * pi, 2.0 * pr * pi
        sr_ref[...] = pr
        si_ref[...] = pi

    s2 = jax.ShapeDtypeStruct((G, P), F32)
    s3 = jax.ShapeDtypeStruct((G, C, P), F32)
    return pl.pallas_call(body, name=name, out_shape=[s2, s2, s2, s2, s3, s3])(a_re, a_im, log_dt, bt_re, bt_im)


def _disc_bwd(a_re, a_im, log_dt, bt_re, bt_im, d_lr, d_li, d_bbr, d_bbi, name):
    G, P = a_re.shape
    C = bt_re.shape[1]

    def body(a_re_ref, a_im_ref, ldt_ref, br_ref, bi_ref, c1, c2, c3, c4, o1, o2, o3, o4, o5):
        _, vjp = jax.vjp(_disc_math, a_re_ref[...], a_im_ref[...], ldt_ref[...], br_ref[...], bi_ref[...])
        o1[...], o2[...], o3[...], o4[...], o5[...] = vjp((c1[...], c2[...], c3[...], c4[...]))

    s2 = jax.ShapeDtypeStruct((G, P), F32)
    s3 = jax.ShapeDtypeStruct((G, C, P), F32)
    return pl.pallas_call(body, name=name, out_shape=[s2, s2, jax.ShapeDtypeStruct((G, 1), F32), s3, s3])(
        a_re, a_im, log_dt, bt_re, bt_im, d_lr, d_li, d_bbr, d_bbi)


def _row_block(ib):
    return pl.ds(pl.multiple_of(ib * SCAN_LANES, SCAN_LANES), SCAN_LANES)


def _chain_block(j, i, ascending, n_blocks):
    at = j * (n_blocks // S5_NQ) + i
    return _row_block(jnp.where(ascending, at, n_blocks - 1 - at))


def _unrolled_loop(n, unroll, body, carry):
    trips = n // unroll
    carry = lax.fori_loop(
        0, trips, lambda t, c: functools.reduce(lambda cc, u: body(t * unroll + u, cc), range(unroll), c), carry)
    for i in range(trips * unroll, n):
        carry = body(i, carry)
    return carry


def _cmul_add(lr, li, sr, si, xr, xi):
    return lr * sr - li * si + xr, lr * si + li * sr + xi


def _scan(xr_ref, xi_ref, lr, li, init, ascending, n_blocks, store):
    steps = n_blocks // S5_NQ

    def load(i):
        blocks = [_chain_block(j, i, ascending, n_blocks) for j in range(S5_NQ)]
        return tuple((xr_ref[rows, :], xi_ref[rows, :]) for rows in blocks)

    if not store:
        def step(i, carry):
            return tuple(_cmul_add(lr, li, sr, si, xr, xi) for (sr, si), (xr, xi) in zip(carry, load(i)))

        return _unrolled_loop(steps, 4, step, init)

    def step_in_place(i, carry):
        states, x_now = carry
        x_next = load(jnp.minimum(i + 1, steps - 1))
        out = []
        for j, ((sr, si), (xr, xi)) in enumerate(zip(states, x_now)):
            nr, ni = _cmul_add(lr, li, sr, si, xr, xi)
            rows = _chain_block(j, i, ascending, n_blocks)
            xr_ref[rows, :] = nr
            xi_ref[rows, :] = ni
            out.append((nr, ni))
        return tuple(out), x_next

    return _unrolled_loop(steps, 2, step_in_place, (init, load(0)))[0]


def _segment_starts(w, lsr, lsi, ascending):
    shape = w[0][0].shape
    row = lax.broadcasted_iota(jnp.int32, shape, 0)
    keep = row != jnp.where(ascending, 0, SCAN_LANES - 1)

    def shift(t):
        t = jnp.where(ascending, pltpu.roll(t, 1, 0), pltpu.roll(t, SCAN_LANES - 1, 0))
        return jnp.where(keep, t, 0.0)

    zero = jnp.zeros(shape, F32)
    c = [(zero, zero)] * S5_NQ
    for _ in range(SCAN_LANES):
        tr, ti = _cmul_add(lsr, lsi, *c[-1], *w[-1])
        c[0] = (shift(tr), shift(ti))
        for j in range(1, S5_NQ):
            c[j] = _cmul_add(lsr, lsi, *c[j - 1], *w[j - 1])
    return tuple(c)


def _first_pass(xr_ref, xi_ref, lam_ref, ascending, n_blocks, conj):
    shape = (SCAN_LANES, xr_ref.shape[1])
    sign = -1.0 if conj else 1.0
    lr = jnp.broadcast_to(lam_ref[0:1, :], shape)
    li = sign * jnp.broadcast_to(lam_ref[1:2, :], shape)
    lsr = jnp.broadcast_to(lam_ref[2:3, :], shape)
    lsi = sign * jnp.broadcast_to(lam_ref[3:4, :], shape)
    zero = jnp.zeros(shape, F32)
    w = _scan(xr_ref, xi_ref, lr, li, ((zero, zero),) * S5_NQ, ascending, n_blocks, store=False)
    return _segment_starts(w, lsr, lsi, ascending), lr, li


def _s5_specs(T):
    NS = S5_NS
    tok = pl.BlockSpec((T, 128), lambda c, d, h: (0, c))
    b_spec = pl.BlockSpec((None, None, None, 128, NS), lambda c, d, h: (d, c, h, 0, 0))
    c_spec = pl.BlockSpec((None, None, None, NS, 128), lambda c, d, h: (d, c, h, 0, 0))
    lam_spec = pl.BlockSpec((None, None, None, 4, NS), lambda c, d, h: (d, c, h, 0, 0))
    return tok, b_spec, c_spec, lam_spec


def _s5_fwd(zp, bre, bim, lam, cre, cimn, comm=None):
    T = zp.shape[0]
    NS = S5_NS
    nb = T // SCAN_LANES
    rc = min(S5_RC, T)
    tok, b_spec, c_spec, lam_spec = _s5_specs(T)

    def body(zp_ref, bre_ref, bim_ref, lam_ref, cre_ref, cim_ref, y_ref, xr_ref, xi_ref):
        d = pl.program_id(1)
        ascending = d == 0

        @pl.when((d == 0) & (pl.program_id(2) == 0))
        def _():
            y_ref[...] = jnp.zeros_like(y_ref)

        def proj(c, _):
            rows = pl.ds(pl.multiple_of(c * rc, rc), rc)
            zz = zp_ref[rows, :]
            xr_ref[rows, :] = _dot(zz, bre_ref[...])
            xi_ref[rows, :] = _dot(zz, bim_ref[...])
            return 0

        lax.fori_loop(0, T // rc, proj, 0)
        starts, lr, li = _first_pass(xr_ref, xi_ref, lam_ref, ascending, nb, conj=False)
        _scan(xr_ref, xi_ref, lr, li, starts, ascending, nb, store=True)

        def outp(c, _):
            rows = pl.ds(pl.multiple_of(c * rc, rc), rc)
            y_ref[rows, :] += (_dot(xr_ref[rows, :].astype(BF16), cre_ref[...])
                               + _dot(xi_ref[rows, :].astype(BF16), cim_ref[...]))
            return 0

        lax.fori_loop(0, T // rc, outp, 0)

    return _pallas(
        body, "s5_fwd", (S5_NCB, 2, S5_NH),
        [tok, b_spec, b_spec, lam_spec, c_spec, c_spec], [tok],
        [jax.ShapeDtypeStruct((T, SSM_WIDTH), F32)],
        [pltpu.VMEM((T, NS), F32), pltpu.VMEM((T, NS), F32)],
        ("parallel", "arbitrary", "arbitrary"), (zp, bre, bim, lam, cre, cimn), comm)


def _s5_bwd(zp, dyp, bre, bim, lam, cre, cimn, comm=None):
    T = zp.shape[0]
    NS, NH = S5_NS, S5_NH
    nb = T // SCAN_LANES
    rc = min(S5_RC, T)
    tok, b_spec, c_spec, lam_spec = _s5_specs(T)
    dlam_spec = pl.BlockSpec((None, None, None, 2, NS), lambda c, d, h: (d, c, h, 0, 0))

    def body(zp_ref, dyp_ref, bre_ref, bim_ref, lam_ref, cre_ref, cim_ref,
             dzp_ref, dbre_ref, dbim_ref, dlam_ref, dcre_ref, dcim_ref,
             sr_ref, si_ref, gr_ref, gi_ref):
        d = pl.program_id(1)
        ascending = d == 0
        g_ascending = d != 0

        @pl.when((d == 0) & (pl.program_id(2) == 0))
        def _():
            dzp_ref[...] = jnp.zeros_like(dzp_ref)

        dcre_ref[...] = jnp.zeros_like(dcre_ref)
        dcim_ref[...] = jnp.zeros_like(dcim_ref)
        dbre_ref[...] = jnp.zeros_like(dbre_ref)
        dbim_ref[...] = jnp.zeros_like(dbim_ref)

        def proj(c, _):
            rows = pl.ds(pl.multiple_of(c * rc, rc), rc)
            zz = zp_ref[rows, :]
            sr_ref[rows, :] = _dot(zz, bre_ref[...])
            si_ref[rows, :] = _dot(zz, bim_ref[...])
            dy = dyp_ref[rows, :]
            gr_ref[rows, :] = _dot_nt(dy, cre_ref[...])
            gi_ref[rows, :] = _dot_nt(dy, cim_ref[...])
            return 0

        lax.fori_loop(0, T // rc, proj, 0)
        s_starts, lr, li = _first_pass(sr_ref, si_ref, lam_ref, ascending, nb, conj=False)
        _scan(sr_ref, si_ref, lr, li, s_starts, ascending, nb, store=True)
        g_starts, lr, lic = _first_pass(gr_ref, gi_ref, lam_ref, g_ascending, nb, conj=True)

        steps = nb // S5_NQ

        def direct(i):
            blocks = [_chain_block(j, i, g_ascending, nb) for j in range(S5_NQ)]
            return tuple((gr_ref[rows, :], gi_ref[rows, :]) for rows in blocks)

        def gstep(i, carry, last):
            g, (ar, ai), d_now = carry
            d_next = d_now if last else direct(jnp.minimum(i + 1, steps - 1))
            out = []
            for j, ((g_r, g_i), (d_r, d_i)) in enumerate(zip(g, d_now)):
                n_r, n_i = _cmul_add(lr, lic, g_r, g_i, d_r, d_i)
                rows = _chain_block(j, i, g_ascending, nb)
                gr_ref[rows, :] = n_r
                gi_ref[rows, :] = n_i
                if last:
                    s_r, s_i = s_starts[S5_NQ - 1 - j]
                else:
                    prev = _chain_block(j, i + 1, g_ascending, nb)
                    s_r, s_i = sr_ref[prev, :], si_ref[prev, :]
                ar = ar + n_r * s_r + n_i * s_i
                ai = ai + n_i * s_r - n_r * s_i
                out.append((n_r, n_i))
            return tuple(out), (ar, ai), d_next

        zero = jnp.zeros((SCAN_LANES, NS), F32)
        carry = _unrolled_loop(steps - 1, 2, lambda i, c: gstep(i, c, False), (g_starts, (zero, zero), direct(0)))
        _, (ar, ai), _ = gstep(steps - 1, carry, True)
        dlam_ref[0:1, :] = _col_sum(ar)
        dlam_ref[1:2, :] = _col_sum(ai)

        def grads(c, _):
            rows = pl.ds(pl.multiple_of(c * rc, rc), rc)
            zz = zp_ref[rows, :]
            dy = dyp_ref[rows, :]
            g_rb = gr_ref[rows, :].astype(BF16)
            g_ib = gi_ref[rows, :].astype(BF16)
            dcre_ref[...] += _dot_tn(sr_ref[rows, :].astype(BF16), dy)
            dcim_ref[...] += _dot_tn(si_ref[rows, :].astype(BF16), dy)
            dbre_ref[...] += _dot_tn(zz, g_rb)
            dbim_ref[...] += _dot_tn(zz, g_ib)
            dzp_ref[rows, :] += _dot_nt(g_rb, bre_ref[...]) + _dot_nt(g_ib, bim_ref[...])
            return 0

        lax.fori_loop(0, T // rc, grads, 0)

    f32 = lambda *s: jax.ShapeDtypeStruct(s, F32)
    return _pallas(
        body, "s5_bwd", (S5_NCB, 2, S5_NH),
        [tok, tok, b_spec, b_spec, lam_spec, c_spec, c_spec],
        [tok, b_spec, b_spec, dlam_spec, c_spec, c_spec],
        [f32(T, SSM_WIDTH), f32(2, S5_NCB, NH, 128, NS), f32(2, S5_NCB, NH, 128, NS),
         f32(2, S5_NCB, NH, 2, NS), f32(2, S5_NCB, NH, NS, 128), f32(2, S5_NCB, NH, NS, 128)],
        [pltpu.VMEM((T, NS), F32)] * 4,
        ("parallel", "arbitrary", "arbitrary"), (zp, dyp, bre, bim, lam, cre, cimn), comm)


def _s5_delta():
    d = np.zeros((S5_NH, 8, 8 // S5_NH), np.float32)
    for h in range(S5_NH):
        for go in range(8 // S5_NH):
            d[h, h * (8 // S5_NH) + go, go] = 1.0
    return d


def _s5_pack_b(bbt):
    gh = 8 // S5_NH
    b5 = bbt.reshape(S5_NCB, S5_NH, gh, SSM_GROUP, SSM_STATE).transpose(0, 1, 3, 2, 4)
    m = b5[:, :, None] * _s5_delta()[None, :, :, None, :, None]
    return m.reshape(S5_NCB, S5_NH, 128, S5_NS)


def _s5_unpack_b(dm):
    gh = 8 // S5_NH
    d6 = dm.reshape(S5_NCB, S5_NH, 8, SSM_GROUP, gh, SSM_STATE)
    b5 = jnp.sum(d6 * _s5_delta()[None, :, :, None, :, None], axis=2)
    return b5.transpose(0, 1, 3, 2, 4).reshape(SSM_GROUPS, SSM_GROUP, SSM_STATE)


def _s5_pack_c(c):
    gh = 8 // S5_NH
    c5 = c.reshape(S5_NCB, S5_NH, gh, SSM_GROUP, SSM_STATE).transpose(0, 1, 2, 4, 3)
    m = c5[:, :, :, :, None, :] * _s5_delta().transpose(0, 2, 1)[None, :, :, None, :, None]
    return m.reshape(S5_NCB, S5_NH, S5_NS, 128)


def _s5_unpack_c(dm):
    gh = 8 // S5_NH
    d6 = dm.reshape(S5_NCB, S5_NH, gh, SSM_STATE, 8, SSM_GROUP)
    c5 = jnp.sum(d6 * _s5_delta().transpose(0, 2, 1)[None, :, :, None, :, None], axis=4)
    return c5.transpose(0, 1, 2, 4, 3).reshape(SSM_GROUPS, SSM_GROUP, SSM_STATE)


def _s5_pack_lam(x):
    return x.reshape(S5_NCB, S5_NH, S5_NS)


def _permute_rows(x):
    T = x.shape[0]
    return x.reshape(SCAN_LANES, T // SCAN_LANES, -1).transpose(1, 0, 2).reshape(T, -1)


def _unpermute_rows(x):
    T = x.shape[0]
    return x.reshape(T // SCAN_LANES, SCAN_LANES, -1).transpose(1, 0, 2).reshape(T, -1)


ATT_TB = ATT_ROWS * GRID_W
ATT_KB = 3 * ATT_TB


def _att_valid(i, n_rows):
    qi, kj = np.meshgrid(np.arange(ATT_TB), np.arange(ATT_KB), indexing="ij")
    r = i * ATT_ROWS + qi // GRID_W
    c = qi % GRID_W
    rk = (i - 1) * ATT_ROWS + kj // GRID_W
    x = kj % GRID_W
    rs = np.clip(r - WIN_H // 2, 0, n_rows - WIN_H)
    cs = np.clip(c - WIN_W // 2, 0, GRID_W - WIN_W)
    return (rk >= rs) & (rk < rs + WIN_H) & (x >= cs) & (x < cs + WIN_W)


def _att_masked_tables(table, n_rows):
    n = n_rows // ATT_ROWS
    assert n >= 3
    masks = np.stack([_att_valid(i, n_rows) for i in (0, 1, n - 1)])
    return jnp.where(masks[:, None], table[None], NEG_INF)


def _att_variant(i, n):
    return jnp.where(i == 0, 0, jnp.where(i >= n - 1, 2, 1))


def _att_probs(qh, kh, bias):
    s = _dot_nt(qh, kh) + bias
    p = jnp.exp(s - jnp.max(s, axis=1, keepdims=True))
    return p * (1.0 / jnp.sum(p, axis=1, keepdims=True))


def _att_specs(n, col):
    last = n - 1
    cur = lambda i: (jnp.minimum(i, last), col)
    prv = lambda i: (jnp.maximum(jnp.minimum(i, last) - 1, 0), col)
    nxt = lambda i: (jnp.minimum(i + 1, last), col)
    blk = lambda f: pl.BlockSpec((ATT_TB, ATT_WIDTH), f)
    return blk(cur), blk(prv), blk(nxt)


def _att_fwd(zb, biasv):
    T = zb.shape[0]
    W = ATT_WIDTH
    n = T // ATT_TB
    n_rows = T // GRID_W
    cur = _att_specs(n, 0)[0]
    q_cur = _att_specs(n, 1)[0]
    k_cur, k_prv, k_nxt = _att_specs(n, 2)
    v_cur, v_prv, v_nxt = _att_specs(n, 3)

    def body(q_ref, kp_ref, kc_ref, kn_ref, vp_ref, vc_ref, vn_ref, b_ref, y_ref):
        qs = q_ref[...] * 0.125
        kb = jnp.concatenate([kp_ref[...], kc_ref[...], kn_ref[...]], axis=0)
        vb = jnp.concatenate([vp_ref[...], vc_ref[...], vn_ref[...]], axis=0)
        outs = []
        for h in range(ATT_HEADS):
            hs = slice(h * ATT_HEAD_DIM, (h + 1) * ATT_HEAD_DIM)
            p = _att_probs(qs[:, hs], kb[:, hs], b_ref[h])
            outs.append(_dot(p.astype(BF16), vb[:, hs]))
        y_ref[...] = jnp.concatenate(outs, axis=1).astype(BF16)

    return pl.pallas_call(
        body, name="att_fwd", grid=(n,),
        in_specs=[q_cur, k_prv, k_cur, k_nxt, v_prv, v_cur, v_nxt,
                  pl.BlockSpec((None, ATT_HEADS, ATT_TB, ATT_KB), lambda i: (_att_variant(i, n), 0, 0, 0))],
        out_specs=cur,
        out_shape=jax.ShapeDtypeStruct((T, W), BF16),
        compiler_params=_cparams(("parallel",), VMEM_LIMIT),
    )(zb, zb, zb, zb, zb, zb, zb, biasv)


def _att_bwd(zb, do, biasv, comm=None):
    T = zb.shape[0]
    W = ATT_WIDTH
    n = T // ATT_TB
    n_rows = T // GRID_W
    cur = _att_specs(n, 0)[0]
    q_cur = _att_specs(n, 1)[0]
    k_cur, k_prv, k_nxt = _att_specs(n, 2)
    v_cur, v_prv, v_nxt = _att_specs(n, 3)
    done = pl.BlockSpec((ATT_TB, W), lambda i: (jnp.maximum(i - 1, 0), 0))
    bias_spec = pl.BlockSpec((None, ATT_HEADS, ATT_TB, ATT_KB), lambda i: (_att_variant(i, n), 0, 0, 0))

    def body(q_ref, do_ref, kp_ref, kc_ref, kn_ref, vp_ref, vc_ref, vn_ref, b_ref,
             dq_ref, dk_ref, dv_ref, db_ref, acck_ref, accv_ref):
        i = pl.program_id(0)

        @pl.when((i == 0) | (i == 1) | (i == n - 1))
        def _():
            db_ref[...] = jnp.zeros_like(db_ref)

        @pl.when(i == 0)
        def _():
            acck_ref[...] = jnp.zeros_like(acck_ref)
            accv_ref[...] = jnp.zeros_like(accv_ref)

        @pl.when((i > 0) & (i < n))
        def _():
            slot = lax.rem(i + 1, 3)
            acck_ref[slot] = jnp.zeros((ATT_TB, W), F32)
            accv_ref[slot] = jnp.zeros((ATT_TB, W), F32)

        @pl.when(i < n)
        def _():
            qs = q_ref[...] * 0.125
            dob = do_ref[...]
            kb = jnp.concatenate([kp_ref[...], kc_ref[...], kn_ref[...]], axis=0)
            vb = jnp.concatenate([vp_ref[...], vc_ref[...], vn_ref[...]], axis=0)
            dqs, dks, dvs = [], [], []
            for h in range(ATT_HEADS):
                hs = slice(h * ATT_HEAD_DIM, (h + 1) * ATT_HEAD_DIM)
                qh, kh, vh, doh = qs[:, hs], kb[:, hs], vb[:, hs], dob[:, hs]
                p = _att_probs(qh, kh, b_ref[h])
                dp = _dot_nt(doh, vh)
                ds = p * (dp - jnp.sum(p * dp, axis=1, keepdims=True))
                db_ref[h] += ds
                dsb = ds.astype(BF16)
                dqs.append(_dot(dsb, kh) * 0.125)
                dks.append(_dot_tn(dsb, qh))
                dvs.append(_dot_tn(p.astype(BF16), doh))
            dq_ref[...] = jnp.concatenate(dqs, axis=1).astype(BF16)
            dk_all = jnp.concatenate(dks, axis=1)
            dv_all = jnp.concatenate(dvs, axis=1)
            for b in range(3):
                slot = lax.rem(i + 2 + b, 3)
                rows = slice(b * ATT_TB, (b + 1) * ATT_TB)
                acck_ref[slot] += dk_all[rows]
                accv_ref[slot] += dv_all[rows]

        slot = lax.rem(i + 2, 3)
        dk_ref[...] = acck_ref[slot].astype(BF16)
        dv_ref[...] = accv_ref[slot].astype(BF16)

    return _pallas(
        body, "att_bwd", (n + 1,),
        [q_cur, cur, k_prv, k_cur, k_nxt, v_prv, v_cur, v_nxt, bias_spec],
        [cur, done, done, bias_spec],
        [jax.ShapeDtypeStruct((T, W), BF16)] * 3 + [jax.ShapeDtypeStruct((3, ATT_HEADS, ATT_TB, ATT_KB), F32)],
        [pltpu.VMEM((3, ATT_TB, W), F32), pltpu.VMEM((3, ATT_TB, W), F32)],
        ("arbitrary",), (zb, do, zb, zb, zb, zb, zb, zb, biasv), comm)


def _att_selectors():
    rsel = np.zeros((ATT_ROWS, 3 * ATT_ROWS, 2 * WIN_H - 1), np.float32)
    for a in range(ATT_ROWS):
        for b in range(3 * ATT_ROWS):
            rsel[a, b, b - a - ATT_ROWS + WIN_H - 1] = 1.0
    csel = np.zeros((GRID_W, GRID_W, 2 * WIN_W - 1), np.float32)
    for c in range(GRID_W):
        for x in range(GRID_W):
            csel[c, x, min(max(x - c, -(WIN_W - 1)), WIN_W - 1) + WIN_W - 1] = 1.0
    return rsel, csel


def _att_bias_table(rpb):
    rsel, csel = _att_selectors()
    hi = lax.Precision.HIGHEST
    t = jnp.einsum('hrd,abr->habd', rpb, rsel, precision=hi)
    t = jnp.einsum('habd,cxd->hacbx', t, csel, precision=hi)
    return t.reshape(ATT_HEADS, ATT_TB, ATT_KB)


def _att_bias_table_t(dtable):
    rsel, csel = _att_selectors()
    hi = lax.Precision.HIGHEST
    t = dtable.reshape(ATT_HEADS, ATT_ROWS, GRID_W, 3 * ATT_ROWS, GRID_W)
    t = jnp.einsum('hacbx,cxd->habd', t, csel, precision=hi)
    return jnp.einsum('habd,abr->hrd', t, rsel, precision=hi)


GELU_K = math.sqrt(2.0 / math.pi)
GELU_C = 0.044715
MERGE_TM = 256


def _gelu(x):
    return 0.5 * x * (1.0 + jnp.tanh(GELU_K * (x + GELU_C * x * x * x)))


def _gelu_grad(x):
    t = jnp.tanh(GELU_K * (x + GELU_C * x * x * x))
    return 0.5 * (1.0 + t) + 0.5 * x * (1.0 - t * t) * GELU_K * (1.0 + 3.0 * GELU_C * x * x)


def _merge_forward(ypre, zs, gs, ga, ya, ssm_d, w_glu, b_glu, w_bs, w_ba):
    ys = ypre + ssm_d * zs
    yg = _gelu(ys)
    sg = jax.nn.sigmoid(_dot(yg.astype(BF16), w_glu) + b_glu)
    y2 = yg * sg
    bs = _dot(y2.astype(BF16), w_bs)
    ba = _dot(ya, w_ba)
    s1 = jax.nn.sigmoid(gs)
    s2 = jax.nn.sigmoid(ga)
    merged = s1 * bs + s2 * ba
    return ys, yg, sg, y2, bs, ba, s1, s2, merged


def _merge_in_specs(D, W, tm):
    tok = lambda w, c: pl.BlockSpec((tm, w), lambda i: (i, c))
    full = lambda r, c: pl.BlockSpec((r, c), lambda i: (0, 0))
    z_specs = [tok(W, 0), tok(D, 4 * W // D), tok(D, 4 * W // D + 1)]
    w_specs = [full(1, W), full(W, W), full(1, W), full(W, D), full(W, D), full(D, D)]
    return tok, z_specs, w_specs


def _merge_fwd(ypre, z, ya, h1, ssm_d, w_glu, b_glu, w_bs, w_ba, w_out):
    T, D = h1.shape
    W = ypre.shape[1]
    tm = min(T, MERGE_TM)
    tok, z_specs, w_specs = _merge_in_specs(D, W, tm)

    def body(ypre_ref, zs_ref, gs_ref, ga_ref, ya_ref, h1_ref, d_ref, wglu_ref, bglu_ref, wbs_ref, wba_ref, wout_ref,
             h2_ref):
        merged = _merge_forward(ypre_ref[...], zs_ref[...], gs_ref[...], ga_ref[...], ya_ref[...], d_ref[...],
                                wglu_ref[...], bglu_ref[...], wbs_ref[...], wba_ref[...])[-1]
        h2_ref[...] = h1_ref[...] + _dot(merged.astype(BF16), wout_ref[...])

    return pl.pallas_call(
        body, name="merge_fwd", grid=(T // tm,),
        in_specs=[tok(W, 0)] + z_specs + [tok(W, 0), tok(D, 0)] + w_specs,
        out_specs=tok(D, 0),
        out_shape=jax.ShapeDtypeStruct((T, D), F32),
        compiler_params=_cparams(("parallel",), VMEM_LIMIT),
    )(ypre, z, z, z, ya, h1, ssm_d, w_glu, b_glu, w_bs, w_ba, w_out)


def _merge_bwd(dh2, ypre, z, ya, ssm_d, w_glu, b_glu, w_bs, w_ba, w_out):
    T, D = dh2.shape
    W = ypre.shape[1]
    tm = min(T, MERGE_TM)
    tok, z_specs, w_specs = _merge_in_specs(D, W, tm)

    def body(dh2_ref, ypre_ref, zs_ref, gs_ref, ga_ref, ya_ref, d_ref, wglu_ref, bglu_ref, wbs_ref, wba_ref, wout_ref,
             dypre_ref, dzs_ref, dgs_ref, dga_ref, dya_ref, dd_ref, dwglu_ref, dbglu_ref, dwbs_ref, dwba_ref, dwout_ref):
        @pl.when(pl.program_id(0) == 0)
        def _():
            for r in (dd_ref, dwglu_ref, dbglu_ref, dwbs_ref, dwba_ref, dwout_ref):
                r[...] = jnp.zeros_like(r)

        zs = zs_ref[...]
        ya = ya_ref[...]
        ys, yg, sg, y2, bs, ba, s1, s2, merged = _merge_forward(
            ypre_ref[...], zs, gs_ref[...], ga_ref[...], ya, d_ref[...],
            wglu_ref[...], bglu_ref[...], wbs_ref[...], wba_ref[...])
        dh2b = dh2_ref[...].astype(BF16)
        dmerged = _dot_nt(dh2b, wout_ref[...])
        dwout_ref[...] += _dot_tn(merged.astype(BF16), dh2b)
        dbs = (dmerged * s1).astype(BF16)
        dba = (dmerged * s2).astype(BF16)
        dgs_ref[...] = (dmerged * bs * s1 * (1.0 - s1)).astype(BF16)
        dga_ref[...] = (dmerged * ba * s2 * (1.0 - s2)).astype(BF16)
        dwbs_ref[...] += _dot_tn(y2.astype(BF16), dbs)
        dwba_ref[...] += _dot_tn(ya, dba)
        dya_ref[...] = _dot_nt(dba, wba_ref[...]).astype(BF16)
        dy2 = _dot_nt(dbs, wbs_ref[...])
        dvv = dy2 * yg * sg * (1.0 - sg)
        dvvb = dvv.astype(BF16)
        dyg = dy2 * sg + _dot_nt(dvvb, wglu_ref[...])
        dwglu_ref[...] += _dot_tn(yg.astype(BF16), dvvb)
        dbglu_ref[...] += _col_sum(dvv)
        dys = dyg * _gelu_grad(ys)
        dd_ref[...] += _col_sum(dys * zs)
        dzs_ref[...] = dys * d_ref[...]
        dypre_ref[...] = dys

    f32 = lambda *s: jax.ShapeDtypeStruct(s, F32)
    b16 = lambda *s: jax.ShapeDtypeStruct(s, BF16)
    return pl.pallas_call(
        body, name="merge_bwd", grid=(T // tm,),
        in_specs=[tok(D, 0), tok(W, 0)] + z_specs + [tok(W, 0)] + w_specs,
        out_specs=[tok(W, 0), tok(W, 0), tok(D, 0), tok(D, 0), tok(W, 0)] + w_specs,
        out_shape=[f32(T, W), f32(T, W), b16(T, D), b16(T, D), b16(T, W),
                   f32(1, W), f32(W, W), f32(1, W), f32(W, D), f32(W, D), f32(D, D)],
        compiler_params=_cparams(("arbitrary",), VMEM_LIMIT),
    )(dh2, ypre, z, z, z, ya, ssm_d, w_glu, b_glu, w_bs, w_ba, w_out)


def _cast_shards(weights):
    def body(*refs):
        n = len(refs) // 2
        for src, dst in zip(refs[:n], refs[n:]):
            dst[...] = src[0].astype(BF16)

    return pl.pallas_call(
        body, name="cast_shards",
        out_shape=[jax.ShapeDtypeStruct(w.shape[1:], BF16) for w in weights],
        compiler_params=_cparams(None, VMEM_LIMIT))(*weights)


def _gather_two_level(shards, name):
    n = len(shards)

    def body(*refs):
        x_refs, out_refs = refs[:n], refs[n:2 * n]
        send_sems, recv_sems, local_sems = refs[2 * n:]
        x, y, c = _my_place()
        me, sibling = (x, y, c), (x, y, 1 - c)
        chips = [(1 - x, y), (x, 1 - y), (1 - x, 1 - y)]

        def copy(a, k, block, to, own=False):
            slot = out_refs[a].at[_flat(*block)]
            return pltpu.make_async_remote_copy(
                src_ref=x_refs[a] if own else slot, dst_ref=slot,
                send_sem=send_sems.at[7 * a + k], recv_sem=recv_sems.at[7 * a + k],
                device_id=to, device_id_type=MESH_ID)

        sent, local = [], []
        for a in range(n):
            local.append(pltpu.make_async_copy(x_refs[a], out_refs[a].at[_flat(*me)], local_sems.at[a]))
            local[-1].start()
            sent.append(copy(a, 0, me, sibling, own=True))
            sent += [copy(a, 1 + j, me, (*chip, c), own=True) for j, chip in enumerate(chips)]
        for cp in sent:
            cp.start()
        for a in range(n):
            for j, chip in enumerate(chips):
                copy(a, 1 + j, (*chip, c), me).wait_recv()
                sent.append(copy(a, 4 + j, (*chip, c), sibling))
                sent[-1].start()
        for a in range(n):
            copy(a, 0, sibling, me).wait_recv()
            for j, chip in enumerate(chips):
                copy(a, 4 + j, (*chip, 1 - c), me).wait_recv()
        for cp in sent:
            cp.wait_send()
        for cp in local:
            cp.wait()

    return pl.pallas_call(
        body, name=name, in_specs=[_HBM] * n, out_specs=[_HBM] * n,
        out_shape=[jax.ShapeDtypeStruct((N_DEV,) + s.shape, s.dtype) for s in shards],
        scratch_shapes=[pltpu.SemaphoreType.DMA((7 * n,)), pltpu.SemaphoreType.DMA((7 * n,)),
                        pltpu.SemaphoreType.DMA((n,))],
    )(*shards)


PACK_COLS = 1024
BIG = (("ffn1_w_gate", 1), ("ffn1_w_up", 1), ("ffn1_w_down", 0), ("w_in", 1), ("ssm_w_glu", 0),
       ("w_branch_ssm", 1), ("w_branch_att", 1), ("w_out", 0),
       ("ffn2_w_gate", 1), ("ffn2_w_up", 1), ("ffn2_w_down", 0))
BIG_AXIS = dict(BIG)
SSM_DIR = ("ssm_a_re", "ssm_a_im", "ssm_log_dt", "ssm_b_re", "ssm_b_im", "ssm_c_re", "ssm_c_im")
SMALL_EARLY = (("mix_norm",) + tuple(n + "_fwd" for n in SSM_DIR) + tuple(n + "_bwd" for n in SSM_DIR)
               + ("ssm_d", "ssm_b_glu", "att_rpb", "ffn2_norm", "final_norm"))
SMALL_LATE = ("ffn1_norm",)
WEIGHTS = ("ffn1_norm", "ffn1_w_gate", "ffn1_w_up", "ffn1_w_down", "mix_norm", "w_in") \
    + tuple(n + "_fwd" for n in SSM_DIR) + tuple(n + "_bwd" for n in SSM_DIR) \
    + ("ssm_d", "ssm_w_glu", "ssm_b_glu", "att_rpb", "w_branch_ssm", "w_branch_att", "w_out",
       "ffn2_norm", "ffn2_w_gate", "ffn2_w_up", "ffn2_w_down", "final_norm")


def _pad_rows(a, mult):
    pad = (-a.shape[-2]) % mult
    if pad:
        a = jnp.concatenate([a, jnp.zeros(a.shape[:-2] + (pad, a.shape[-1]), a.dtype)], axis=-2)
    return a


def _pack(arrays, row_mult):
    flat = jnp.concatenate([a.reshape(-1) for a in arrays])
    pad = (-flat.shape[0]) % PACK_COLS
    if pad:
        flat = jnp.concatenate([flat, jnp.zeros((pad,), flat.dtype)])
    return _pad_rows(flat.reshape(-1, PACK_COLS), row_mult)


def _unpack(slab, shapes):
    flat = slab.reshape(-1)
    out, at = [], 0
    for s in shapes:
        n = int(np.prod(s))
        out.append(flat[at:at + n].reshape(s))
        at += n
    return out


def _split_for_devices(g, axis):
    r, c = g.shape
    if axis == 1:
        return g.reshape(r, N_DEV, c // N_DEV).transpose(1, 0, 2).astype(BF16)
    return g.reshape(N_DEV, r // N_DEV, c).astype(BF16)


def _join_shards(gathered, axis):
    _, r, c = gathered.shape
    if axis == 1:
        return gathered.transpose(1, 0, 2).reshape(r, N_DEV * c)
    return gathered.reshape(N_DEV * r, c)


def _s5_direction_inputs(p, sfx, chain_len):
    bt_re = p["ssm_b_re" + sfx][0].transpose(0, 2, 1)
    bt_im = p["ssm_b_im" + sfx][0].transpose(0, 2, 1)
    raw = (p["ssm_a_re" + sfx][0], p["ssm_a_im" + sfx][0], p["ssm_log_dt" + sfx][0][:, None], bt_re, bt_im)
    lr, li, sr, si, bbr, bbi = _disc_fwd(*raw, chain_len,"s5_disc" + sfx)
    lam = jnp.stack([_s5_pack_lam(t) for t in (lr, li, sr, si)], axis=2)
    mats = (_s5_pack_b(bbr), _s5_pack_b(bbi), lam,
            _s5_pack_c(p["ssm_c_re" + sfx][0]), _s5_pack_c(-p["ssm_c_im" + sfx][0]))
    return raw, mats


def kernel(x, ffn1_norm, ffn1_w_gate, ffn1_w_up, ffn1_w_down, mix_norm, w_in, ssm_a_re_fwd, ssm_a_im_fwd, ssm_log_dt_fwd, ssm_b_re_fwd, ssm_b_im_fwd, ssm_c_re_fwd, ssm_c_im_fwd, ssm_a_re_bwd, ssm_a_im_bwd, ssm_log_dt_bwd, ssm_b_re_bwd, ssm_b_im_bwd, ssm_c_re_bwd, ssm_c_im_bwd, ssm_d, ssm_w_glu, ssm_b_glu, att_rpb, w_branch_ssm, w_branch_att, w_out, ffn2_norm, ffn2_w_gate, ffn2_w_up, ffn2_w_down, final_norm, loss_target, m_ffn1_norm, m_ffn1_w_gate, m_ffn1_w_up, m_ffn1_w_down, m_mix_norm, m_w_in, m_ssm_a_re_fwd, m_ssm_a_im_fwd, m_ssm_log_dt_fwd, m_ssm_b_re_fwd, m_ssm_b_im_fwd, m_ssm_c_re_fwd, m_ssm_c_im_fwd, m_ssm_a_re_bwd, m_ssm_a_im_bwd, m_ssm_log_dt_bwd, m_ssm_b_re_bwd, m_ssm_b_im_bwd, m_ssm_c_re_bwd, m_ssm_c_im_bwd, m_ssm_d, m_ssm_w_glu, m_ssm_b_glu, m_att_rpb, m_w_branch_ssm, m_w_branch_att, m_w_out, m_ffn2_norm, m_ffn2_w_gate, m_ffn2_w_up, m_ffn2_w_down, m_final_norm, v_ffn1_norm, v_ffn1_w_gate, v_ffn1_w_up, v_ffn1_w_down, v_mix_norm, v_w_in, v_ssm_a_re_fwd, v_ssm_a_im_fwd, v_ssm_log_dt_fwd, v_ssm_b_re_fwd, v_ssm_b_im_fwd, v_ssm_c_re_fwd, v_ssm_c_im_fwd, v_ssm_a_re_bwd, v_ssm_a_im_bwd, v_ssm_log_dt_bwd, v_ssm_b_re_bwd, v_ssm_b_im_bwd, v_ssm_c_re_bwd, v_ssm_c_im_bwd, v_ssm_d, v_ssm_w_glu, v_ssm_b_glu, v_att_rpb, v_w_branch_ssm, v_w_branch_att, v_w_out, v_ffn2_norm, v_ffn2_w_gate, v_ffn2_w_up, v_ffn2_w_down, v_final_norm):
    p = dict(locals())
    x = p["x"][0]
    target = p["loss_target"][0]
    T, D = x.shape

    shard = dict(zip([n for n, _ in BIG], _cast_shards([p[n] for n, _ in BIG])))
    ffn1_w = ("ffn1_w_gate", "ffn1_w_up", "ffn1_w_down")
    mix_w = ("w_in", "ssm_w_glu", "w_branch_ssm", "w_branch_att", "w_out")
    ffn2_w = ("ffn2_w_gate", "ffn2_w_up", "ffn2_w_down")
    gathered = dict(zip(ffn1_w, _gather_two_level([shard[n] for n in ffn1_w], "gather_ffn1")))
    full = lambda n: _join_shards(gathered[n], BIG_AXIS[n])

    h0 = x
    wg1, wu1, wd1 = [full(n) for n in ffn1_w]
    (h1, xn1, g1, u1), got = _ffn_fwd(h0, p["ffn1_norm"], wg1, wu1, wd1, "ffn1_fwd",
                                      _Comm("gather", [shard[n] for n in mix_w]))
    gathered.update(zip(mix_w, got))
    z, zb, un = _mixin_fwd(h1, p["mix_norm"], gathered["w_in"])
    W = SSM_WIDTH
    zp = _permute_rows(zb[:, :W])
    chain_len = T // SCAN_LANES // S5_NQ
    raw_f, mats_f = _s5_direction_inputs(p, "_fwd", chain_len)
    raw_b, mats_b = _s5_direction_inputs(p, "_bwd", chain_len)
    bre, bim, lam, cre, cimn = [jnp.stack([f, b]) for f, b in zip(mats_f, mats_b)]
    bre, bim, cre, cimn = [t.astype(BF16) for t in (bre, bim, cre, cimn)]
    (yp,), got = _s5_fwd(zp, bre, bim, lam, cre, cimn, _Comm("gather", [shard[n] for n in ffn2_w]))
    gathered.update(zip(ffn2_w, got))
    ypre = _unpermute_rows(yp)
    table = _att_masked_tables(_att_bias_table(p["att_rpb"][0]), T // GRID_W)
    ya = _att_fwd(zb, table)
    tail_w = (p["ssm_d"], full("ssm_w_glu"), p["ssm_b_glu"], full("w_branch_ssm"), full("w_branch_att"), full("w_out"))
    h2 = _merge_fwd(ypre, z, ya, h1, *tail_w)
    wg2, wu2, wd2 = [full(n) for n in ffn2_w]
    (h3, xn2, g2, u2), _ = _ffn_fwd(h2, p["ffn2_norm"], wg2, wu2, wd2, "ffn2_fwd")
    loss_part, dh3, d_final = _loss_head(h3, p["final_norm"][None], target)

    grads = {"final_norm": d_final[0]}
    to_send = lambda names: _Comm("exchange", [_split_for_devices(grads[n], BIG_AXIS[n]) for n in names])
    parts = {}
    (dh2, grads["ffn2_norm"], do2, a2, dg2, du2), _ = _ffn_bwd(
        dh3, h2, p["ffn2_norm"], g2, u2, wg2, wu2, wd2, "ffn2_bwd")
    grads["ffn2_w_gate"] = _xty(xn2, dg2, "ffn2_dw_gate")
    grads["ffn2_w_up"] = _xty(xn2, du2, "ffn2_dw_up")
    grads["ffn2_w_down"] = _xty(a2, do2, "ffn2_dw_down")
    (dypre, dzs_skip, dgs, dga, dya, grads["ssm_d"], grads["ssm_w_glu"], grads["ssm_b_glu"],
     grads["w_branch_ssm"], grads["w_branch_att"], grads["w_out"]) = _merge_bwd(dh2, ypre, z, ya, *tail_w)
    (dq, dk, dv, dtable), got = _att_bwd(zb, dya, table, to_send(ffn2_w))
    parts.update(zip(ffn2_w, got))
    grads["att_rpb"] = _att_bias_table_t(jnp.sum(dtable, axis=0))
    dyp = _permute_rows(dypre).astype(BF16)
    tail_names = ("ssm_w_glu", "w_branch_ssm", "w_branch_att", "w_out")
    (dzp, dbre, dbim, dlam, dcre, dcimn), got = _s5_bwd(zp, dyp, bre, bim, lam, cre, cimn, to_send(tail_names))
    parts.update(zip(tail_names, got))
    G, P = SSM_GROUPS, SSM_STATE
    for d, (sfx, raw) in enumerate((("_fwd", raw_f), ("_bwd", raw_b))):
        da_re, da_im, dldt, dbt_re, dbt_im = _disc_bwd(
            *raw, dlam[d, :, :, 0, :].reshape(G, P), dlam[d, :, :, 1, :].reshape(G, P),
            _s5_unpack_b(dbre[d]), _s5_unpack_b(dbim[d]), "s5_disc_grad" + sfx)
        grads["ssm_a_re" + sfx] = da_re
        grads["ssm_a_im" + sfx] = da_im
        grads["ssm_log_dt" + sfx] = dldt[:, 0]
        grads["ssm_b_re" + sfx] = dbt_re.transpose(0, 2, 1)
        grads["ssm_b_im" + sfx] = dbt_im.transpose(0, 2, 1)
        grads["ssm_c_re" + sfx] = _s5_unpack_c(dcre[d])
        grads["ssm_c_im" + sfx] = -_s5_unpack_c(dcimn[d])
    dzs = _unpermute_rows(dzp) + dzs_skip
    dz = jnp.concatenate([dzs.astype(BF16), dq, dk, dv, dgs, dga], axis=1)
    dh1, grads["mix_norm"] = _mixin_bwd(dz, dh2, h1, p["mix_norm"], gathered["w_in"])
    grads["w_in"] = _xty(un, dz, "dw_in")
    pack_small = lambda names, src, pre: _pack([src[pre + n].astype(F32) for n in names], 8)
    early = _Comm(["exchange", "gather"],
                  [_split_for_devices(grads["w_in"], 1), pack_small(SMALL_EARLY, grads, "")])
    (dh0, grads["ffn1_norm"], do1, a1, dg1, du1), (parts["w_in"], got_early) = _ffn_bwd(
        dh1, h0, p["ffn1_norm"], g1, u1, wg1, wu1, wd1, "ffn1_bwd", early)
    grads["ffn1_w_down"] = _xty(a1, do1, "ffn1_dw_down")
    grads["ffn1_w_gate"], (parts["ffn1_w_down"],) = _xty(xn1, dg1, "ffn1_dw_gate", to_send(("ffn1_w_down",)))
    grads["ffn1_w_up"], (parts["ffn1_w_gate"],) = _xty(xn1, du1, "ffn1_dw_up", to_send(("ffn1_w_gate",)))
    last = _Comm(["exchange", "gather"],
                 [_split_for_devices(grads["ffn1_w_up"], 1), pack_small(SMALL_LATE, grads, "")])
    parts["ffn1_w_up"], got_late = _comm_call(last, "exchange_last")
    got_small = jnp.concatenate([got_early, got_late], axis=1)

    results = {}
    for n, _ in BIG:
        outs = _adamw(parts[n], p[n][0], p["m_" + n][0], p["v_" + n][0], "adamw_" + n)
        results[n] = [o[None] for o in outs]
    early_rows = got_early.shape[1]
    slab = lambda pre: jnp.concatenate([pack_small(SMALL_EARLY, p, pre), pack_small(SMALL_LATE, p, pre)], axis=0)
    small_out = _adamw(got_small, slab(""), slab("m_"), slab("v_"), "adamw_small")
    for names, rows in ((SMALL_EARLY, slice(0, early_rows)), (SMALL_LATE, slice(early_rows, None))):
        shapes = [p[n].shape for n in names]
        for n, vals in zip(names, zip(*[_unpack(out[rows], shapes) for out in small_out])):
            results[n] = list(vals)

    loss = lax.psum(loss_part[0, 0], ("x", "y", "c"))
    out = [loss, dh0[None]]
    for kind in range(4):
        out += [results[n][kind] for n in WEIGHTS]
    return tuple(out)
```

```python
import functools
import math

import numpy as np
import jax
import jax.numpy as jnp
from jax import lax
from jax.experimental import pallas as pl
from jax.experimental.pallas import tpu as pltpu

F32 = jnp.float32
BF16 = jnp.bfloat16
MESH_ID = pl.DeviceIdType.MESH

SSM_GROUP = 16
SSM_GROUPS = 32
SSM_STATE = 64
SSM_WIDTH = 512
ATT_HEADS = 8
ATT_HEAD_DIM = 64
ATT_WIDTH = 512
GRID_W = 64
WIN_H = 8
WIN_W = 16
EPS = 1e-6
NEG_INF = -1e30
ADAM_LR = 0.001
ADAM_B1 = 0.9
ADAM_B2 = 0.999
ADAM_EPS = 1e-08
ADAM_WD = 0.01
ADAM_STEP = 10

N_DEV = 8
V7X_VMEM_BYTES = 64 * 1024 * 1024
VMEM_LIMIT = V7X_VMEM_BYTES - 8 * 1024 * 1024
SCAN_LANES = 8
ATT_ROWS = 4


def _cparams(sem, vmem=None):
    return pltpu.CompilerParams(dimension_semantics=sem, vmem_limit_bytes=vmem)


def _dot(a, b):
    return jnp.dot(a, b, preferred_element_type=F32)


def _dot_nt(a, b):
    return lax.dot_general(a, b, (((1,), (1,)), ((), ())), preferred_element_type=F32)


def _dot_tn(a, b):
    return lax.dot_general(a, b, (((0,), (0,)), ((), ())), preferred_element_type=F32)


def _rms(h):
    return lax.rsqrt(jnp.mean(h * h, axis=-1, keepdims=True) + EPS)


def _rms_bwd(h, r, v):
    return r * v - h * (r * r * r) * jnp.mean(h * v, axis=-1, keepdims=True)


def _col_sum(x):
    return jnp.sum(x, axis=0, keepdims=True)


def _my_place():
    return lax.axis_index("x"), lax.axis_index("y"), lax.axis_index("c")


def _flat(px, py, pc):
    return 4 * px + 2 * py + pc


class _Comm:
    def __init__(self, kind, arrays):
        self.arrays = list(arrays)
        self.n = len(self.arrays)
        self.kinds = [kind] * self.n if isinstance(kind, str) else list(kind)

    def out_shapes(self):
        return [jax.ShapeDtypeStruct((N_DEV,) + a.shape if k == "gather" else a.shape, a.dtype)
                for k, a in zip(self.kinds, self.arrays)]

    def scratch(self):
        return [pltpu.SemaphoreType.DMA((7 * self.n,)), pltpu.SemaphoreType.DMA((7 * self.n,)),
                pltpu.SemaphoreType.DMA((self.n,))]

    def run(self, srcs, dsts, sems, start):
        send_sems, recv_sems, local_sems = sems
        x, y, c = _my_place()
        mine = _flat(x, y, c)
        for a, (src, dst) in enumerate(zip(srcs, dsts)):
            whole = self.kinds[a] == "gather"
            local = pltpu.make_async_copy(src if whole else src.at[mine], dst.at[mine], local_sems.at[a])
            local.start() if start else local.wait()
            for k in range(1, N_DEV):
                px = 1 - x if k & 4 else x
                py = 1 - y if k & 2 else y
                pc = 1 - c if k & 1 else c
                cp = pltpu.make_async_remote_copy(
                    src_ref=src if whole else src.at[_flat(px, py, pc)], dst_ref=dst.at[mine],
                    send_sem=send_sems.at[7 * a + k - 1], recv_sem=recv_sems.at[7 * a + k - 1],
                    device_id=(px, py, pc), device_id_type=MESH_ID)
                cp.start() if start else cp.wait()


_HBM = pl.BlockSpec(memory_space=pltpu.HBM)


def _comm_call(comm, name):
    def body(*refs):
        srcs, dsts, sems = refs[:comm.n], refs[comm.n:2 * comm.n], refs[2 * comm.n:]
        comm.run(srcs, dsts, sems, True)
        comm.run(srcs, dsts, sems, False)

    return pl.pallas_call(body, name=name, in_specs=[_HBM] * comm.n, out_specs=[_HBM] * comm.n,
                          out_shape=comm.out_shapes(), scratch_shapes=comm.scratch())(*comm.arrays)


def _pallas(core, name, grid, in_specs, out_specs, out_shape, scratch, sem, args, comm=None):
    if comm is None:
        out = pl.pallas_call(core, name=name, grid=grid, in_specs=in_specs, out_specs=out_specs,
                             out_shape=out_shape, scratch_shapes=scratch,
                             compiler_params=_cparams(sem, VMEM_LIMIT))(*args)
        return out, []
    n_in, n_out, n_scr, n = len(in_specs), len(out_specs), len(scratch), comm.n

    def body(*refs):
        ins, srcs = refs[:n_in], refs[n_in:n_in + n]
        outs, dsts = refs[n_in + n:n_in + n + n_out], refs[n_in + n + n_out:n_in + 2 * n + n_out]
        scr, sems = refs[n_in + 2 * n + n_out:n_in + 2 * n + n_out + n_scr], refs[n_in + 2 * n + n_out + n_scr:]
        ids = [pl.program_id(k) for k in range(len(grid))]
        first = functools.reduce(lambda a, b: a & b, [i == 0 for i in ids])
        last = functools.reduce(lambda a, b: a & b, [i == g - 1 for i, g in zip(ids, grid)])

        @pl.when(first)
        def _():
            comm.run(srcs, dsts, sems, True)

        core(*ins, *outs, *scr)

        @pl.when(last)
        def _():
            comm.run(srcs, dsts, sems, False)

    out = pl.pallas_call(
        body, name=name, grid=grid, in_specs=list(in_specs) + [_HBM] * n, out_specs=list(out_specs) + [_HBM] * n,
        out_shape=list(out_shape) + comm.out_shapes(), scratch_shapes=list(scratch) + comm.scratch(),
        compiler_params=_cparams(("arbitrary",) * len(grid), VMEM_LIMIT))(*args, *comm.arrays)
    return out[:n_out], out[n_out:]


def _ffn_tiles(T, F, wide):
    if F % 1408 == 0:
        return min(T, 512 if wide else 256), 1408
    return min(T, 1024), 256 if F % 256 == 0 else F


def _ffn_fwd(h, gain, wg, wu, wd, name, comm=None):
    T, D = h.shape
    F = wg.shape[1]
    tm, tf = _ffn_tiles(T, F, wide=True)
    nj = F // tf

    def body(h_ref, gain_ref, wg_ref, wu_ref, wd_ref, ho_ref, xn_ref, g_ref, u_ref, acc_ref):
        j = pl.program_id(1)

        @pl.when(j == 0)
        def _():
            hh = h_ref[...]
            xn_ref[...] = (hh * _rms(hh) * gain_ref[...]).astype(BF16)
            acc_ref[...] = jnp.zeros_like(acc_ref)

        xn = xn_ref[...]
        g = _dot(xn, wg_ref[...])
        u = _dot(xn, wu_ref[...])
        g_ref[...] = g.astype(BF16)
        u_ref[...] = u.astype(BF16)
        a = (g * jax.nn.sigmoid(g) * u).astype(BF16)
        acc_ref[...] += _dot(a, wd_ref[...])

        @pl.when(j == nj - 1)
        def _():
            ho_ref[...] = h_ref[...] + 0.5 * acc_ref[...]

    return _pallas(
        body, name, (T // tm, nj),
        [pl.BlockSpec((tm, D), lambda i, j: (i, 0)),
         pl.BlockSpec((1, D), lambda i, j: (0, 0)),
         pl.BlockSpec((D, tf), lambda i, j: (0, j)),
         pl.BlockSpec((D, tf), lambda i, j: (0, j)),
         pl.BlockSpec((tf, D), lambda i, j: (j, 0))],
        [pl.BlockSpec((tm, D), lambda i, j: (i, 0)),
         pl.BlockSpec((tm, D), lambda i, j: (i, 0)),
         pl.BlockSpec((tm, tf), lambda i, j: (i, j)),
         pl.BlockSpec((tm, tf), lambda i, j: (i, j))],
        [jax.ShapeDtypeStruct((T, D), F32), jax.ShapeDtypeStruct((T, D), BF16),
         jax.ShapeDtypeStruct((T, F), BF16), jax.ShapeDtypeStruct((T, F), BF16)],
        [pltpu.VMEM((tm, D), F32)], ("parallel", "arbitrary"), (h, gain, wg, wu, wd), comm)


def _ffn_bwd(dho, h, gain, g, u, wg, wu, wd, name, comm=None):
    T, D = h.shape
    F = wg.shape[1]
    tm, tf = _ffn_tiles(T, F, wide=False)
    nj = F // tf

    def body(dho_ref, h_ref, gain_ref, g_ref, u_ref, wg_ref, wu_ref, wd_ref,
             dh_ref, dgain_ref, do_ref, a_ref, dg_ref, du_ref, acc_ref):
        i = pl.program_id(0)
        j = pl.program_id(1)

        @pl.when(j == 0)
        def _():
            do_ref[...] = (0.5 * dho_ref[...]).astype(BF16)
            acc_ref[...] = jnp.zeros_like(acc_ref)

        @pl.when((i == 0) & (j == 0))
        def _():
            dgain_ref[...] = jnp.zeros_like(dgain_ref)

        da = _dot_nt(do_ref[...], wd_ref[...])
        gg = g_ref[...].astype(F32)
        uu = u_ref[...].astype(F32)
        s = jax.nn.sigmoid(gg)
        sl = gg * s
        a_ref[...] = (sl * uu).astype(BF16)
        dg = (da * uu * (s * (1.0 + gg * (1.0 - s)))).astype(BF16)
        du = (da * sl).astype(BF16)
        dg_ref[...] = dg
        du_ref[...] = du
        acc_ref[...] += _dot_nt(dg, wg_ref[...]) + _dot_nt(du, wu_ref[...])

        @pl.when(j == nj - 1)
        def _():
            hh = h_ref[...]
            r = _rms(hh)
            dxn = acc_ref[...]
            dgain_ref[...] += _col_sum(dxn * hh * r)
            dh_ref[...] = dho_ref[...] + _rms_bwd(hh, r, dxn * gain_ref[...])

    return _pallas(
        body, name, (T // tm, nj),
        [pl.BlockSpec((tm, D), lambda i, j: (i, 0)),
         pl.BlockSpec((tm, D), lambda i, j: (i, 0)),
         pl.BlockSpec((1, D), lambda i, j: (0, 0)),
         pl.BlockSpec((tm, tf), lambda i, j: (i, j)),
         pl.BlockSpec((tm, tf), lambda i, j: (i, j)),
         pl.BlockSpec((D, tf), lambda i, j: (0, j)),
         pl.BlockSpec((D, tf), lambda i, j: (0, j)),
         pl.BlockSpec((tf, D), lambda i, j: (j, 0))],
        [pl.BlockSpec((tm, D), lambda i, j: (i, 0)),
         pl.BlockSpec((1, D), lambda i, j: (0, 0)),
         pl.BlockSpec((tm, D), lambda i, j: (i, 0)),
         pl.BlockSpec((tm, tf), lambda i, j: (i, j)),
         pl.BlockSpec((tm, tf), lambda i, j: (i, j)),
         pl.BlockSpec((tm, tf), lambda i, j: (i, j))],
        [jax.ShapeDtypeStruct((T, D), F32), jax.ShapeDtypeStruct((1, D), F32),
         jax.ShapeDtypeStruct((T, D), BF16), jax.ShapeDtypeStruct((T, F), BF16),
         jax.ShapeDtypeStruct((T, F), BF16), jax.ShapeDtypeStruct((T, F), BF16)],
        [pltpu.VMEM((tm, D), F32)], ("arbitrary", "arbitrary"), (dho, h, gain, g, u, wg, wu, wd), comm)


def _xty(x, y, name, comm=None):
    T, K = x.shape
    N = y.shape[1]
    tt = min(T, 1024)
    tk = K if K <= 1024 else (1408 if K % 1408 == 0 else K)
    tn = N if N <= 1024 else (1408 if N % 1408 == 0 else (1024 if N % 1024 == 0 else N))
    nt = T // tt

    def body(x_ref, y_ref, o_ref):
        t = pl.program_id(2)

        @pl.when(t == 0)
        def _():
            o_ref[...] = jnp.zeros_like(o_ref)

        o_ref[...] += _dot_tn(x_ref[...], y_ref[...])

    (out,), got = _pallas(
        body, name, (K // tk, N // tn, nt),
        [pl.BlockSpec((tt, tk), lambda k, n, t: (t, k)), pl.BlockSpec((tt, tn), lambda k, n, t: (t, n))],
        [pl.BlockSpec((tk, tn), lambda k, n, t: (k, n))], [jax.ShapeDtypeStruct((K, N), F32)], [],
        ("parallel", "parallel", "arbitrary"), (x, y), comm)
    return out if comm is None else (out, got)


def _mixin_fwd(h, gain, w_in):
    T, D = h.shape
    nn, _, tn = w_in.shape
    N = nn * tn
    tm = min(T, 1024)

    def body(h_ref, gain_ref, w_ref, z_ref, zb_ref, un_ref):
        @pl.when(pl.program_id(1) == 0)
        def _():
            hh = h_ref[...]
            un_ref[...] = (hh * _rms(hh) * gain_ref[...]).astype(BF16)

        z = _dot(un_ref[...], w_ref[...])
        z_ref[...] = z
        zb_ref[...] = z.astype(BF16)

    return pl.pallas_call(
        body, name="mixin_fwd", grid=(T // tm, nn),
        in_specs=[pl.BlockSpec((tm, D), lambda i, n: (i, 0)),
                  pl.BlockSpec((1, D), lambda i, n: (0, 0)),
                  pl.BlockSpec((None, D, tn), lambda i, n: (n, 0, 0))],
        out_specs=[pl.BlockSpec((tm, tn), lambda i, n: (i, n)),
                   pl.BlockSpec((tm, tn), lambda i, n: (i, n)),
                   pl.BlockSpec((tm, D), lambda i, n: (i, 0))],
        out_shape=[jax.ShapeDtypeStruct((T, N), F32), jax.ShapeDtypeStruct((T, N), BF16),
                   jax.ShapeDtypeStruct((T, D), BF16)],
        compiler_params=_cparams(("parallel", "arbitrary"), VMEM_LIMIT),
    )(h, gain, w_in)


def _mixin_bwd(dz, dh_res, h, gain, w_in):
    T, D = h.shape
    nn, _, tn = w_in.shape
    tm = min(T, 1024)

    def body(dz_ref, dres_ref, h_ref, gain_ref, w_ref, dh_ref, dgain_ref, acc_ref):
        i = pl.program_id(0)
        n = pl.program_id(1)

        @pl.when(n == 0)
        def _():
            acc_ref[...] = jnp.zeros_like(acc_ref)

        @pl.when((i == 0) & (n == 0))
        def _():
            dgain_ref[...] = jnp.zeros_like(dgain_ref)

        acc_ref[...] += _dot_nt(dz_ref[...], w_ref[...])

        @pl.when(n == nn - 1)
        def _():
            hh = h_ref[...]
            r = _rms(hh)
            dun = acc_ref[...]
            dgain_ref[...] += _col_sum(dun * hh * r)
            dh_ref[...] = dres_ref[...] + _rms_bwd(hh, r, dun * gain_ref[...])

    return pl.pallas_call(
        body, name="mixin_bwd", grid=(T // tm, nn),
        in_specs=[pl.BlockSpec((tm, tn), lambda i, n: (i, n)),
                  pl.BlockSpec((tm, D), lambda i, n: (i, 0)),
                  pl.BlockSpec((tm, D), lambda i, n: (i, 0)),
                  pl.BlockSpec((1, D), lambda i, n: (0, 0)),
                  pl.BlockSpec((None, D, tn), lambda i, n: (n, 0, 0))],
        out_specs=[pl.BlockSpec((tm, D), lambda i, n: (i, 0)),
                   pl.BlockSpec((1, D), lambda i, n: (0, 0))],
        out_shape=[jax.ShapeDtypeStruct((T, D), F32), jax.ShapeDtypeStruct((1, D), F32)],
        scratch_shapes=[pltpu.VMEM((tm, D), F32)],
        compiler_params=_cparams(("arbitrary", "arbitrary"), VMEM_LIMIT),
    )(dz, dh_res, h, gain, w_in)


def _loss_head(h, gain, target):
    T, D = h.shape
    tm = min(T, 1024)

    def body(h_ref, gain_ref, t_ref, loss_ref, dh_ref, dgain_ref):
        @pl.when(pl.program_id(0) == 0)
        def _():
            loss_ref[...] = jnp.zeros_like(loss_ref)
            dgain_ref[...] = jnp.zeros_like(dgain_ref)

        hh = h_ref[...]
        r = _rms(hh)
        e = hh * r * gain_ref[...] - t_ref[...]
        loss_ref[...] += (0.5 / D) * jnp.sum(e * e)
        dy = e * (1.0 / D)
        dgain_ref[...] += _col_sum(dy * hh * r)
        dh_ref[...] = _rms_bwd(hh, r, dy * gain_ref[...])

    return pl.pallas_call(
        body, name="loss_head", grid=(T // tm,),
        in_specs=[pl.BlockSpec((tm, D), lambda i: (i, 0)),
                  pl.BlockSpec((1, D), lambda i: (0, 0)),
                  pl.BlockSpec((tm, D), lambda i: (i, 0))],
        out_specs=[pl.BlockSpec((1, 128), lambda i: (0, 0)),
                   pl.BlockSpec((tm, D), lambda i: (i, 0)),
                   pl.BlockSpec((1, D), lambda i: (0, 0))],
        out_shape=[jax.ShapeDtypeStruct((1, 128), F32), jax.ShapeDtypeStruct((T, D), F32),
                   jax.ShapeDtypeStruct((1, D), F32)],
        compiler_params=_cparams(("arbitrary",), VMEM_LIMIT),
    )(h, gain, target)


def _adamw(parts, w, m, v, name):
    R, C = w.shape
    mult = 16 if parts.dtype == BF16 else 8
    tr = max(t for t in range(mult, min(R, 512) + 1, mult) if R % t == 0)
    c1 = 1.0 - ADAM_B1 ** ADAM_STEP
    c2 = 1.0 - ADAM_B2 ** ADAM_STEP

    def body(p_ref, w_ref, m_ref, v_ref, g_ref, d_ref, nm_ref, nv_ref):
        g = p_ref[0].astype(F32)
        for k in range(1, N_DEV):
            g = g + p_ref[k].astype(F32)
        mm = ADAM_B1 * m_ref[...] + (1.0 - ADAM_B1) * g
        vv = ADAM_B2 * v_ref[...] + (1.0 - ADAM_B2) * (g * g)
        g_ref[...] = g
        nm_ref[...] = mm
        nv_ref[...] = vv
        d_ref[...] = -ADAM_LR * ((mm / c1) / (jnp.sqrt(vv / c2) + ADAM_EPS) + ADAM_WD * w_ref[...])

    spec = pl.BlockSpec((tr, C), lambda i: (i, 0))
    return pl.pallas_call(
        body, name=name, grid=(R // tr,),
        in_specs=[pl.BlockSpec((N_DEV, tr, C), lambda i: (0, i, 0)), spec, spec, spec],
        out_specs=[spec, spec, spec, spec],
        out_shape=[jax.ShapeDtypeStruct((R, C), F32)] * 4,
        compiler_params=_cparams(("parallel",), VMEM_LIMIT),
    )(parts, w, m, v)


S5_NS = 256
S5_NH = 2
S5_NCB = 4
S5_RC = 512
S5_NQ = 4
S5_GROUP = 2


def _disc_math(a_re, a_im, log_dt, bt_re, bt_im):
    dt = jnp.exp(log_dt)
    zr, zi = a_re * dt, a_im * dt
    mag = jnp.exp(zr)
    lb_re, lb_im = mag * jnp.cos(zi), mag * jnp.sin(zi)
    den = a_re * a_re + a_im * a_im
    nr, ni = lb_re - 1.0, lb_im
    f_re = (nr * a_re + ni * a_im) / den
    f_im = (ni * a_re - nr * a_im) / den
    bb_re = f_re[:, None, :] * bt_re - f_im[:, None, :] * bt_im
    bb_im = f_re[:, None, :] * bt_im + f_im[:, None, :] * bt_re
    return lb_re, lb_im, bb_re, bb_im


def _disc_fwd(a_re, a_im, log_dt, bt_re, bt_im, chain_len, name):
    G, P = a_re.shape
    C = bt_re.shape[1]
    n_sq = int(round(math.log2(chain_len)))
    assert 2 ** n_sq == chain_len

    def body(a_re_ref, a_im_ref, ldt_ref, br_ref, bi_ref, lr_ref, li_ref, sr_ref, si_ref, bbr_ref, bbi_ref):
        lr, li, bbr, bbi = _disc_math(a_re_ref[...], a_im_ref[...], ldt_ref[...], br_ref[...], bi_ref[...])
        lr_ref[...] = lr
        li_ref[...] = li
        bbr_ref[...] = bbr
        bbi_ref[...] = bbi
        pr, pi = lr, li
        for _ in range(n_sq):
            pr, pi = pr * pr - pi * pi, 2.0 * pr * pi
        sr_ref[...] = pr
        si_ref[...] = pi

    s2 = jax.ShapeDtypeStruct((G, P), F32)
    s3 = jax.ShapeDtypeStruct((G, C, P), F32)
    return pl.pallas_call(body, name=name, out_shape=[s2, s2, s2, s2, s3, s3])(a_re, a_im, log_dt, bt_re, bt_im)


def _disc_bwd(a_re, a_im, log_dt, bt_re, bt_im, d_lr, d_li, d_bbr, d_bbi, name):
    G, P = a_re.shape
    C = bt_re.shape[1]

    def body(a_re_ref, a_im_ref, ldt_ref, br_ref, bi_ref, c1, c2, c3, c4, o1, o2, o3, o4, o5):
        _, vjp = jax.vjp(_disc_math, a_re_ref[...], a_im_ref[...], ldt_ref[...], br_ref[...], bi_ref[...])
        o1[...], o2[...], o3[...], o4[...], o5[...] = vjp((c1[...], c2[...], c3[...], c4[...]))

    s2 = jax.ShapeDtypeStruct((G, P), F32)
    s3 = jax.ShapeDtypeStruct((G, C, P), F32)
    return pl.pallas_call(body, name=name, out_shape=[s2, s2, jax.ShapeDtypeStruct((G, 1), F32), s3, s3])(
        a_re, a_im, log_dt, bt_re, bt_im, d_lr, d_li, d_bbr, d_bbi)


def _row_block(ib):
    return pl.ds(pl.multiple_of(ib * SCAN_LANES, SCAN_LANES), SCAN_LANES)


def _chain_block(j, i, ascending, n_blocks):
    at = j * (n_blocks // S5_NQ) + i
    return _row_block(jnp.where(ascending, at, n_blocks - 1 - at))


def _unrolled_loop(n, unroll, body, carry):
    trips = n // unroll
    carry = lax.fori_loop(
        0, trips, lambda t, c: functools.reduce(lambda cc, u: body(t * unroll + u, cc), range(unroll), c), carry)
    for i in range(trips * unroll, n):
        carry = body(i, carry)
    return carry


def _cmul_add(lr, li, sr, si, xr, xi):
    return lr * sr - li * si + xr, lr * si + li * sr + xi


def _scan(xr_ref, xi_ref, lr, li, init, ascending, n_blocks, store):
    steps = n_blocks // S5_NQ
    if not store:
        def step(i, carry):
            blocks = [_chain_block(j, i, ascending, n_blocks) for j in range(S5_NQ)]
            return tuple(_cmul_add(lr, li, sr, si, xr_ref[rows, :], xi_ref[rows, :])
                         for (sr, si), rows in zip(carry, blocks))

        return _unrolled_loop(steps, 4, step, init)

    group = S5_GROUP
    assert steps % group == 0

    def trip(t, carry):
        blocks = [[_chain_block(j, t * group + u, ascending, n_blocks) for j in range(S5_NQ)] for u in range(group)]
        xs = [[(xr_ref[rows, :], xi_ref[rows, :]) for rows in row] for row in blocks]
        states = list(carry)
        done = []
        for u in range(group):
            states = [_cmul_add(lr, li, sr, si, xr, xi) for (sr, si), (xr, xi) in zip(states, xs[u])]
            done.append(states)
        for u in range(group):
            for rows, (nr, ni) in zip(blocks[u], done[u]):
                xr_ref[rows, :] = nr
                xi_ref[rows, :] = ni
        return tuple(states)

    return lax.fori_loop(0, steps // group, trip, init)


def _segment_starts(w, lsr, lsi, ascending):
    shape = w[0][0].shape
    row = lax.broadcasted_iota(jnp.int32, shape, 0)
    keep = row != jnp.where(ascending, 0, SCAN_LANES - 1)

    def shift(t):
        t = jnp.where(ascending, pltpu.roll(t, 1, 0), pltpu.roll(t, SCAN_LANES - 1, 0))
        return jnp.where(keep, t, 0.0)

    zero = jnp.zeros(shape, F32)
    c = [(zero, zero)] * S5_NQ
    for _ in range(SCAN_LANES):
        tr, ti = _cmul_add(lsr, lsi, *c[-1], *w[-1])
        c[0] = (shift(tr), shift(ti))
        for j in range(1, S5_NQ):
            c[j] = _cmul_add(lsr, lsi, *c[j - 1], *w[j - 1])
    return tuple(c)


def _first_pass(xr_ref, xi_ref, lam_ref, ascending, n_blocks, conj):
    shape = (SCAN_LANES, xr_ref.shape[1])
    sign = -1.0 if conj else 1.0
    lr = jnp.broadcast_to(lam_ref[0:1, :], shape)
    li = sign * jnp.broadcast_to(lam_ref[1:2, :], shape)
    lsr = jnp.broadcast_to(lam_ref[2:3, :], shape)
    lsi = sign * jnp.broadcast_to(lam_ref[3:4, :], shape)
    zero = jnp.zeros(shape, F32)
    w = _scan(xr_ref, xi_ref, lr, li, ((zero, zero),) * S5_NQ, ascending, n_blocks, store=False)
    return _segment_starts(w, lsr, lsi, ascending), lr, li


def _s5_specs(T):
    NS = S5_NS
    tok = pl.BlockSpec((T, 128), lambda c, d, h: (0, c))
    b_spec = pl.BlockSpec((None, None, None, 128, NS), lambda c, d, h: (d, c, h, 0, 0))
    c_spec = pl.BlockSpec((None, None, None, NS, 128), lambda c, d, h: (d, c, h, 0, 0))
    lam_spec = pl.BlockSpec((None, None, None, 4, NS), lambda c, d, h: (d, c, h, 0, 0))
    return tok, b_spec, c_spec, lam_spec


def _s5_fwd(zp, bre, bim, lam, cre, cimn, comm=None):
    T = zp.shape[0]
    NS = S5_NS
    nb = T // SCAN_LANES
    rc = min(S5_RC, T)
    tok, b_spec, c_spec, lam_spec = _s5_specs(T)

    def body(zp_ref, bre_ref, bim_ref, lam_ref, cre_ref, cim_ref, y_ref, xr_ref, xi_ref):
        d = pl.program_id(1)
        ascending = d == 0

        @pl.when((d == 0) & (pl.program_id(2) == 0))
        def _():
            y_ref[...] = jnp.zeros_like(y_ref)

        def proj(c, _):
            rows = pl.ds(pl.multiple_of(c * rc, rc), rc)
            zz = zp_ref[rows, :]
            xr_ref[rows, :] = _dot(zz, bre_ref[...])
            xi_ref[rows, :] = _dot(zz, bim_ref[...])
            return 0

        lax.fori_loop(0, T // rc, proj, 0)
        starts, lr, li = _first_pass(xr_ref, xi_ref, lam_ref, ascending, nb, conj=False)
        _scan(xr_ref, xi_ref, lr, li, starts, ascending, nb, store=True)

        def outp(c, _):
            rows = pl.ds(pl.multiple_of(c * rc, rc), rc)
            y_ref[rows, :] += (_dot(xr_ref[rows, :].astype(BF16), cre_ref[...])
                               + _dot(xi_ref[rows, :].astype(BF16), cim_ref[...]))
            return 0

        lax.fori_loop(0, T // rc, outp, 0)

    return _pallas(
        body, "s5_fwd", (S5_NCB, 2, S5_NH),
        [tok, b_spec, b_spec, lam_spec, c_spec, c_spec], [tok],
        [jax.ShapeDtypeStruct((T, SSM_WIDTH), F32)],
        [pltpu.VMEM((T, NS), F32), pltpu.VMEM((T, NS), F32)],
        ("parallel", "arbitrary", "arbitrary"), (zp, bre, bim, lam, cre, cimn), comm)


def _s5_bwd(zp, dyp, bre, bim, lam, cre, cimn, comm=None):
    T = zp.shape[0]
    NS, NH = S5_NS, S5_NH
    nb = T // SCAN_LANES
    rc = min(S5_RC, T)
    tok, b_spec, c_spec, lam_spec = _s5_specs(T)
    dlam_spec = pl.BlockSpec((None, None, None, 2, NS), lambda c, d, h: (d, c, h, 0, 0))

    def body(zp_ref, dyp_ref, bre_ref, bim_ref, lam_ref, cre_ref, cim_ref,
             dzp_ref, dbre_ref, dbim_ref, dlam_ref, dcre_ref, dcim_ref,
             sr_ref, si_ref, gr_ref, gi_ref):
        d = pl.program_id(1)
        ascending = d == 0
        g_ascending = d != 0

        @pl.when((d == 0) & (pl.program_id(2) == 0))
        def _():
            dzp_ref[...] = jnp.zeros_like(dzp_ref)

        dcre_ref[...] = jnp.zeros_like(dcre_ref)
        dcim_ref[...] = jnp.zeros_like(dcim_ref)
        dbre_ref[...] = jnp.zeros_like(dbre_ref)
        dbim_ref[...] = jnp.zeros_like(dbim_ref)

        def proj(c, _):
            rows = pl.ds(pl.multiple_of(c * rc, rc), rc)
            zz = zp_ref[rows, :]
            sr_ref[rows, :] = _dot(zz, bre_ref[...])
            si_ref[rows, :] = _dot(zz, bim_ref[...])
            dy = dyp_ref[rows, :]
            gr_ref[rows, :] = _dot_nt(dy, cre_ref[...])
            gi_ref[rows, :] = _dot_nt(dy, cim_ref[...])
            return 0

        lax.fori_loop(0, T // rc, proj, 0)
        s_starts, lr, li = _first_pass(sr_ref, si_ref, lam_ref, ascending, nb, conj=False)
        _scan(sr_ref, si_ref, lr, li, s_starts, ascending, nb, store=True)
        g_starts, lr, lic = _first_pass(gr_ref, gi_ref, lam_ref, g_ascending, nb, conj=True)

        steps = nb // S5_NQ
        group = S5_GROUP
        assert steps % group == 0

        def gtrip(t, carry, last):
            g, (ar, ai) = carry
            first = t * group
            blocks = [[_chain_block(j, first + u, g_ascending, nb) for j in range(S5_NQ)] for u in range(group)]
            direct = [[(gr_ref[rows, :], gi_ref[rows, :]) for rows in row] for row in blocks]
            done = []
            for u in range(group):
                new = []
                for j, ((g_r, g_i), (d_r, d_i)) in enumerate(zip(g, direct[u])):
                    n_r, n_i = _cmul_add(lr, lic, g_r, g_i, d_r, d_i)
                    if last and u == group - 1:
                        s_r, s_i = s_starts[S5_NQ - 1 - j]
                    else:
                        prev = _chain_block(j, first + u + 1, g_ascending, nb)
                        s_r, s_i = sr_ref[prev, :], si_ref[prev, :]
                    ar = ar + n_r * s_r + n_i * s_i
                    ai = ai + n_i * s_r - n_r * s_i
                    new.append((n_r, n_i))
                g = new
                done.append(new)
            for u in range(group):
                for rows, (n_r, n_i) in zip(blocks[u], done[u]):
                    gr_ref[rows, :] = n_r
                    gi_ref[rows, :] = n_i
            return tuple(g), (ar, ai)

        zero = jnp.zeros((SCAN_LANES, NS), F32)
        carry = lax.fori_loop(0, steps // group - 1, lambda t, c: gtrip(t, c, False), (g_starts, (zero, zero)))
        _, (ar, ai) = gtrip(steps // group - 1, carry, True)
        dlam_ref[0:1, :] = _col_sum(ar)
        dlam_ref[1:2, :] = _col_sum(ai)

        def grads(c, _):
            rows = pl.ds(pl.multiple_of(c * rc, rc), rc)
            zz = zp_ref[rows, :]
            dy = dyp_ref[rows, :]
            g_rb = gr_ref[rows, :].astype(BF16)
            g_ib = gi_ref[rows, :].astype(BF16)
            dcre_ref[...] += _dot_tn(sr_ref[rows, :].astype(BF16), dy)
            dcim_ref[...] += _dot_tn(si_ref[rows, :].astype(BF16), dy)
            dbre_ref[...] += _dot_tn(zz, g_rb)
            dbim_ref[...] += _dot_tn(zz, g_ib)
            dzp_ref[rows, :] += _dot_nt(g_rb, bre_ref[...]) + _dot_nt(g_ib, bim_ref[...])
            return 0

        lax.fori_loop(0, T // rc, grads, 0)

    f32 = lambda *s: jax.ShapeDtypeStruct(s, F32)
    return _pallas(
        body, "s5_bwd", (S5_NCB, 2, S5_NH),
        [tok, tok, b_spec, b_spec, lam_spec, c_spec, c_spec],
        [tok, b_spec, b_spec, dlam_spec, c_spec, c_spec],
        [f32(T, SSM_WIDTH), f32(2, S5_NCB, NH, 128, NS), f32(2, S5_NCB, NH, 128, NS),
         f32(2, S5_NCB, NH, 2, NS), f32(2, S5_NCB, NH, NS, 128), f32(2, S5_NCB, NH, NS, 128)],
        [pltpu.VMEM((T, NS), F32)] * 4,
        ("parallel", "arbitrary", "arbitrary"), (zp, dyp, bre, bim, lam, cre, cimn), comm)


def _s5_delta():
    d = np.zeros((S5_NH, 8, 8 // S5_NH), np.float32)
    for h in range(S5_NH):
        for go in range(8 // S5_NH):
            d[h, h * (8 // S5_NH) + go, go] = 1.0
    return d


def _s5_pack_b(bbt):
    gh = 8 // S5_NH
    b5 = bbt.reshape(S5_NCB, S5_NH, gh, SSM_GROUP, SSM_STATE).transpose(0, 1, 3, 2, 4)
    m = b5[:, :, None] * _s5_delta()[None, :, :, None, :, None]
    return m.reshape(S5_NCB, S5_NH, 128, S5_NS)


def _s5_unpack_b(dm):
    gh = 8 // S5_NH
    d6 = dm.reshape(S5_NCB, S5_NH, 8, SSM_GROUP, gh, SSM_STATE)
    b5 = jnp.sum(d6 * _s5_delta()[None, :, :, None, :, None], axis=2)
    return b5.transpose(0, 1, 3, 2, 4).reshape(SSM_GROUPS, SSM_GROUP, SSM_STATE)


def _s5_pack_c(c):
    gh = 8 // S5_NH
    c5 = c.reshape(S5_NCB, S5_NH, gh, SSM_GROUP, SSM_STATE).transpose(0, 1, 2, 4, 3)
    m = c5[:, :, :, :, None, :] * _s5_delta().transpose(0, 2, 1)[None, :, :, None, :, None]
    return m.reshape(S5_NCB, S5_NH, S5_NS, 128)


def _s5_unpack_c(dm):
    gh = 8 // S5_NH
    d6 = dm.reshape(S5_NCB, S5_NH, gh, SSM_STATE, 8, SSM_GROUP)
    c5 = jnp.sum(d6 * _s5_delta().transpose(0, 2, 1)[None, :, :, None, :, None], axis=4)
    return c5.transpose(0, 1, 2, 4, 3).reshape(SSM_GROUPS, SSM_GROUP, SSM_STATE)


def _s5_pack_lam(x):
    return x.reshape(S5_NCB, S5_NH, S5_NS)


def _permute_rows(x):
    T = x.shape[0]
    return x.reshape(SCAN_LANES, T // SCAN_LANES, -1).transpose(1, 0, 2).reshape(T, -1)


def _unpermute_rows(x):
    T = x.shape[0]
    return x.reshape(T // SCAN_LANES, SCAN_LANES, -1).transpose(1, 0, 2).reshape(T, -1)


ATT_TB = ATT_ROWS * GRID_W
ATT_KB = 3 * ATT_TB


def _att_valid(i, n_rows):
    qi, kj = np.meshgrid(np.arange(ATT_TB), np.arange(ATT_KB), indexing="ij")
    r = i * ATT_ROWS + qi // GRID_W
    c = qi % GRID_W
    rk = (i - 1) * ATT_ROWS + kj // GRID_W
    x = kj % GRID_W
    rs = np.clip(r - WIN_H // 2, 0, n_rows - WIN_H)
    cs = np.clip(c - WIN_W // 2, 0, GRID_W - WIN_W)
    return (rk >= rs) & (rk < rs + WIN_H) & (x >= cs) & (x < cs + WIN_W)


def _att_masked_tables(table, n_rows):
    n = n_rows // ATT_ROWS
    assert n >= 3
    masks = np.stack([_att_valid(i, n_rows) for i in (0, 1, n - 1)])
    return jnp.where(masks[:, None], table[None], NEG_INF)


def _att_variant(i, n):
    return jnp.where(i == 0, 0, jnp.where(i >= n - 1, 2, 1))


def _att_probs(qh, kh, bias):
    s = _dot_nt(qh, kh) + bias
    p = jnp.exp(s - jnp.max(s, axis=1, keepdims=True))
    return p * (1.0 / jnp.sum(p, axis=1, keepdims=True))


def _att_specs(n, col):
    last = n - 1
    cur = lambda i: (jnp.minimum(i, last), col)
    prv = lambda i: (jnp.maximum(jnp.minimum(i, last) - 1, 0), col)
    nxt = lambda i: (jnp.minimum(i + 1, last), col)
    blk = lambda f: pl.BlockSpec((ATT_TB, ATT_WIDTH), f)
    return blk(cur), blk(prv), blk(nxt)


def _att_fwd(zb, biasv):
    T = zb.shape[0]
    W = ATT_WIDTH
    n = T // ATT_TB
    n_rows = T // GRID_W
    cur = _att_specs(n, 0)[0]
    q_cur = _att_specs(n, 1)[0]
    k_cur, k_prv, k_nxt = _att_specs(n, 2)
    v_cur, v_prv, v_nxt = _att_specs(n, 3)

    def body(q_ref, kp_ref, kc_ref, kn_ref, vp_ref, vc_ref, vn_ref, b_ref, y_ref):
        qs = q_ref[...] * 0.125
        kb = jnp.concatenate([kp_ref[...], kc_ref[...], kn_ref[...]], axis=0)
        vb = jnp.concatenate([vp_ref[...], vc_ref[...], vn_ref[...]], axis=0)
        outs = []
        for h in range(ATT_HEADS):
            hs = slice(h * ATT_HEAD_DIM, (h + 1) * ATT_HEAD_DIM)
            p = _att_probs(qs[:, hs], kb[:, hs], b_ref[h])
            outs.append(_dot(p.astype(BF16), vb[:, hs]))
        y_ref[...] = jnp.concatenate(outs, axis=1).astype(BF16)

    return pl.pallas_call(
        body, name="att_fwd", grid=(n,),
        in_specs=[q_cur, k_prv, k_cur, k_nxt, v_prv, v_cur, v_nxt,
                  pl.BlockSpec((None, ATT_HEADS, ATT_TB, ATT_KB), lambda i: (_att_variant(i, n), 0, 0, 0))],
        out_specs=cur,
        out_shape=jax.ShapeDtypeStruct((T, W), BF16),
        compiler_params=_cparams(("parallel",), VMEM_LIMIT),
    )(zb, zb, zb, zb, zb, zb, zb, biasv)


def _att_bwd(zb, do, biasv, comm=None):
    T = zb.shape[0]
    W = ATT_WIDTH
    n = T // ATT_TB
    n_rows = T // GRID_W
    cur = _att_specs(n, 0)[0]
    q_cur = _att_specs(n, 1)[0]
    k_cur, k_prv, k_nxt = _att_specs(n, 2)
    v_cur, v_prv, v_nxt = _att_specs(n, 3)
    done = pl.BlockSpec((ATT_TB, W), lambda i: (jnp.maximum(i - 1, 0), 0))
    bias_spec = pl.BlockSpec((None, ATT_HEADS, ATT_TB, ATT_KB), lambda i: (_att_variant(i, n), 0, 0, 0))

    def body(q_ref, do_ref, kp_ref, kc_ref, kn_ref, vp_ref, vc_ref, vn_ref, b_ref,
             dq_ref, dk_ref, dv_ref, db_ref, acck_ref, accv_ref):
        i = pl.program_id(0)

        @pl.when((i == 0) | (i == 1) | (i == n - 1))
        def _():
            db_ref[...] = jnp.zeros_like(db_ref)

        @pl.when(i == 0)
        def _():
            acck_ref[...] = jnp.zeros_like(acck_ref)
            accv_ref[...] = jnp.zeros_like(accv_ref)

        @pl.when((i > 0) & (i < n))
        def _():
            slot = lax.rem(i + 1, 3)
            acck_ref[slot] = jnp.zeros((ATT_TB, W), F32)
            accv_ref[slot] = jnp.zeros((ATT_TB, W), F32)

        @pl.when(i < n)
        def _():
            qs = q_ref[...] * 0.125
            dob = do_ref[...]
            kb = jnp.concatenate([kp_ref[...], kc_ref[...], kn_ref[...]], axis=0)
            vb = jnp.concatenate([vp_ref[...], vc_ref[...], vn_ref[...]], axis=0)
            dqs, dks, dvs = [], [], []
            for h in range(ATT_HEADS):
                hs = slice(h * ATT_HEAD_DIM, (h + 1) * ATT_HEAD_DIM)
                qh, kh, vh, doh = qs[:, hs], kb[:, hs], vb[:, hs], dob[:, hs]
                p = _att_probs(qh, kh, b_ref[h])
                dp = _dot_nt(doh, vh)
                ds = p * (dp - jnp.sum(p * dp, axis=1, keepdims=True))
                db_ref[h] += ds
                dsb = ds.astype(BF16)
                dqs.append(_dot(dsb, kh) * 0.125)
                dks.append(_dot_tn(dsb, qh))
                dvs.append(_dot_tn(p.astype(BF16), doh))
            dq_ref[...] = jnp.concatenate(dqs, axis=1).astype(BF16)
            dk_all = jnp.concatenate(dks, axis=1)
            dv_all = jnp.concatenate(dvs, axis=1)
            for b in range(3):
                slot = lax.rem(i + 2 + b, 3)
                rows = slice(b * ATT_TB, (b + 1) * ATT_TB)
                acck_ref[slot] += dk_all[rows]
                accv_ref[slot] += dv_all[rows]

        slot = lax.rem(i + 2, 3)
        dk_ref[...] = acck_ref[slot].astype(BF16)
        dv_ref[...] = accv_ref[slot].astype(BF16)

    return _pallas(
        body, "att_bwd", (n + 1,),
        [q_cur, cur, k_prv, k_cur, k_nxt, v_prv, v_cur, v_nxt, bias_spec],
        [cur, done, done, bias_spec],
        [jax.ShapeDtypeStruct((T, W), BF16)] * 3 + [jax.ShapeDtypeStruct((3, ATT_HEADS, ATT_TB, ATT_KB), F32)],
        [pltpu.VMEM((3, ATT_TB, W), F32), pltpu.VMEM((3, ATT_TB, W), F32)],
        ("arbitrary",), (zb, do, zb, zb, zb, zb, zb, zb, biasv), comm)


def _att_selectors():
    rsel = np.zeros((ATT_ROWS, 3 * ATT_ROWS, 2 * WIN_H - 1), np.float32)
    for a in range(ATT_ROWS):
        for b in range(3 * ATT_ROWS):
            rsel[a, b, b - a - ATT_ROWS + WIN_H - 1] = 1.0
    csel = np.zeros((GRID_W, GRID_W, 2 * WIN_W - 1), np.float32)
    for c in range(GRID_W):
        for x in range(GRID_W):
            csel[c, x, min(max(x - c, -(WIN_W - 1)), WIN_W - 1) + WIN_W - 1] = 1.0
    return rsel, csel


def _att_bias_table(rpb):
    rsel, csel = _att_selectors()
    hi = lax.Precision.HIGHEST
    t = jnp.einsum('hrd,abr->habd', rpb, rsel, precision=hi)
    t = jnp.einsum('habd,cxd->hacbx', t, csel, precision=hi)
    return t.reshape(ATT_HEADS, ATT_TB, ATT_KB)


def _att_bias_table_t(dtable):
    rsel, csel = _att_selectors()
    hi = lax.Precision.HIGHEST
    t = dtable.reshape(ATT_HEADS, ATT_ROWS, GRID_W, 3 * ATT_ROWS, GRID_W)
    t = jnp.einsum('hacbx,cxd->habd', t, csel, precision=hi)
    return jnp.einsum('habd,abr->hrd', t, rsel, precision=hi)


GELU_K = math.sqrt(2.0 / math.pi)
GELU_C = 0.044715
MERGE_TM = 256


def _gelu(x):
    return 0.5 * x * (1.0 + jnp.tanh(GELU_K * (x + GELU_C * x * x * x)))


def _gelu_grad(x):
    t = jnp.tanh(GELU_K * (x + GELU_C * x * x * x))
    return 0.5 * (1.0 + t) + 0.5 * x * (1.0 - t * t) * GELU_K * (1.0 + 3.0 * GELU_C * x * x)


def _merge_forward(ypre, zs, gs, ga, ya, ssm_d, w_glu, b_glu, w_bs, w_ba):
    ys = ypre + ssm_d * zs
    yg = _gelu(ys)
    sg = jax.nn.sigmoid(_dot(yg.astype(BF16), w_glu) + b_glu)
    y2 = yg * sg
    bs = _dot(y2.astype(BF16), w_bs)
    ba = _dot(ya, w_ba)
    s1 = jax.nn.sigmoid(gs)
    s2 = jax.nn.sigmoid(ga)
    merged = s1 * bs + s2 * ba
    return ys, yg, sg, y2, bs, ba, s1, s2, merged


def _merge_in_specs(D, W, tm):
    tok = lambda w, c: pl.BlockSpec((tm, w), lambda i: (i, c))
    full = lambda r, c: pl.BlockSpec((r, c), lambda i: (0, 0))
    z_specs = [tok(W, 0), tok(D, 4 * W // D), tok(D, 4 * W // D + 1)]
    w_specs = [full(1, W), full(W, W), full(1, W), full(W, D), full(W, D), full(D, D)]
    return tok, z_specs, w_specs


def _merge_fwd(ypre, z, ya, h1, ssm_d, w_glu, b_glu, w_bs, w_ba, w_out):
    T, D = h1.shape
    W = ypre.shape[1]
    tm = min(T, MERGE_TM)
    tok, z_specs, w_specs = _merge_in_specs(D, W, tm)

    def body(ypre_ref, zs_ref, gs_ref, ga_ref, ya_ref, h1_ref, d_ref, wglu_ref, bglu_ref, wbs_ref, wba_ref, wout_ref,
             h2_ref):
        merged = _merge_forward(ypre_ref[...], zs_ref[...], gs_ref[...], ga_ref[...], ya_ref[...], d_ref[...],
                                wglu_ref[...], bglu_ref[...], wbs_ref[...], wba_ref[...])[-1]
        h2_ref[...] = h1_ref[...] + _dot(merged.astype(BF16), wout_ref[...])

    return pl.pallas_call(
        body, name="merge_fwd", grid=(T // tm,),
        in_specs=[tok(W, 0)] + z_specs + [tok(W, 0), tok(D, 0)] + w_specs,
        out_specs=tok(D, 0),
        out_shape=jax.ShapeDtypeStruct((T, D), F32),
        compiler_params=_cparams(("parallel",), VMEM_LIMIT),
    )(ypre, z, z, z, ya, h1, ssm_d, w_glu, b_glu, w_bs, w_ba, w_out)


def _merge_bwd(dh2, ypre, z, ya, ssm_d, w_glu, b_glu, w_bs, w_ba, w_out):
    T, D = dh2.shape
    W = ypre.shape[1]
    tm = min(T, MERGE_TM)
    tok, z_specs, w_specs = _merge_in_specs(D, W, tm)

    def body(dh2_ref, ypre_ref, zs_ref, gs_ref, ga_ref, ya_ref, d_ref, wglu_ref, bglu_ref, wbs_ref, wba_ref, wout_ref,
             dypre_ref, dzs_ref, dgs_ref, dga_ref, dya_ref, dd_ref, dwglu_ref, dbglu_ref, dwbs_ref, dwba_ref, dwout_ref):
        @pl.when(pl.program_id(0) == 0)
        def _():
            for r in (dd_ref, dwglu_ref, dbglu_ref, dwbs_ref, dwba_ref, dwout_ref):
                r[...] = jnp.zeros_like(r)

        zs = zs_ref[...]
        ya = ya_ref[...]
        ys, yg, sg, y2, bs, ba, s1, s2, merged = _merge_forward(
            ypre_ref[...], zs, gs_ref[...], ga_ref[...], ya, d_ref[...],
            wglu_ref[...], bglu_ref[...], wbs_ref[...], wba_ref[...])
        dh2b = dh2_ref[...].astype(BF16)
        dmerged = _dot_nt(dh2b, wout_ref[...])
        dwout_ref[...] += _dot_tn(merged.astype(BF16), dh2b)
        dbs = (dmerged * s1).astype(BF16)
        dba = (dmerged * s2).astype(BF16)
        dgs_ref[...] = (dmerged * bs * s1 * (1.0 - s1)).astype(BF16)
        dga_ref[...] = (dmerged * ba * s2 * (1.0 - s2)).astype(BF16)
        dwbs_ref[...] += _dot_tn(y2.astype(BF16), dbs)
        dwba_ref[...] += _dot_tn(ya, dba)
        dya_ref[...] = _dot_nt(dba, wba_ref[...]).astype(BF16)
        dy2 = _dot_nt(dbs, wbs_ref[...])
        dvv = dy2 * yg * sg * (1.0 - sg)
        dvvb = dvv.astype(BF16)
        dyg = dy2 * sg + _dot_nt(dvvb, wglu_ref[...])
        dwglu_ref[...] += _dot_tn(yg.astype(BF16), dvvb)
        dbglu_ref[...] += _col_sum(dvv)
        dys = dyg * _gelu_grad(ys)
        dd_ref[...] += _col_sum(dys * zs)
        dzs_ref[...] = dys * d_ref[...]
        dypre_ref[...] = dys

    f32 = lambda *s: jax.ShapeDtypeStruct(s, F32)
    b16 = lambda *s: jax.ShapeDtypeStruct(s, BF16)
    return pl.pallas_call(
        body, name="merge_bwd", grid=(T // tm,),
        in_specs=[tok(D, 0), tok(W, 0)] + z_specs + [tok(W, 0)] + w_specs,
        out_specs=[tok(W, 0), tok(W, 0), tok(D, 0), tok(D, 0), tok(W, 0)] + w_specs,
        out_shape=[f32(T, W), f32(T, W), b16(T, D), b16(T, D), b16(T, W),
                   f32(1, W), f32(W, W), f32(1, W), f32(W, D), f32(W, D), f32(D, D)],
        compiler_params=_cparams(("arbitrary",), VMEM_LIMIT),
    )(dh2, ypre, z, z, z, ya, ssm_d, w_glu, b_glu, w_bs, w_ba, w_out)


def _cast_shards(weights):
    def body(*refs):
        n = len(refs) // 2
        for src, dst in zip(refs[:n], refs[n:]):
            dst[...] = src[0].astype(BF16)

    return pl.pallas_call(
        body, name="cast_shards",
        out_shape=[jax.ShapeDtypeStruct(w.shape[1:], BF16) for w in weights],
        compiler_params=_cparams(None, VMEM_LIMIT))(*weights)


def _gather_two_level(shards, name):
    n = len(shards)

    def body(*refs):
        x_refs, out_refs = refs[:n], refs[n:2 * n]
        send_sems, recv_sems, local_sems = refs[2 * n:]
        x, y, c = _my_place()
        me, sibling = (x, y, c), (x, y, 1 - c)
        chips = [(1 - x, y), (x, 1 - y), (1 - x, 1 - y)]

        def copy(a, k, block, to, own=False):
            slot = out_refs[a].at[_flat(*block)]
            return pltpu.make_async_remote_copy(
                src_ref=x_refs[a] if own else slot, dst_ref=slot,
                send_sem=send_sems.at[7 * a + k], recv_sem=recv_sems.at[7 * a + k],
                device_id=to, device_id_type=MESH_ID)

        sent, local = [], []
        for a in range(n):
            local.append(pltpu.make_async_copy(x_refs[a], out_refs[a].at[_flat(*me)], local_sems.at[a]))
            local[-1].start()
            sent.append(copy(a, 0, me, sibling, own=True))
            sent += [copy(a, 1 + j, me, (*chip, c), own=True) for j, chip in enumerate(chips)]
        for cp in sent:
            cp.start()
        for a in range(n):
            for j, chip in enumerate(chips):
                copy(a, 1 + j, (*chip, c), me).wait_recv()
                sent.append(copy(a, 4 + j, (*chip, c), sibling))
                sent[-1].start()
        for a in range(n):
            copy(a, 0, sibling, me).wait_recv()
            for j, chip in enumerate(chips):
                copy(a, 4 + j, (*chip, 1 - c), me).wait_recv()
        for cp in sent:
            cp.wait_send()
        for cp in local:
            cp.wait()

    return pl.pallas_call(
        body, name=name, in_specs=[_HBM] * n, out_specs=[_HBM] * n,
        out_shape=[jax.ShapeDtypeStruct((N_DEV,) + s.shape, s.dtype) for s in shards],
        scratch_shapes=[pltpu.SemaphoreType.DMA((7 * n,)), pltpu.SemaphoreType.DMA((7 * n,)),
                        pltpu.SemaphoreType.DMA((n,))],
    )(*shards)


PACK_COLS = 1024
BIG = (("ffn1_w_gate", 1), ("ffn1_w_up", 1), ("ffn1_w_down", 0), ("w_in", 1), ("ssm_w_glu", 0),
       ("w_branch_ssm", 1), ("w_branch_att", 1), ("w_out", 0),
       ("ffn2_w_gate", 1), ("ffn2_w_up", 1), ("ffn2_w_down", 0))
BIG_AXIS = dict(BIG)
SSM_DIR = ("ssm_a_re", "ssm_a_im", "ssm_log_dt", "ssm_b_re", "ssm_b_im", "ssm_c_re", "ssm_c_im")
SMALL_EARLY = (("mix_norm",) + tuple(n + "_fwd" for n in SSM_DIR) + tuple(n + "_bwd" for n in SSM_DIR)
               + ("ssm_d", "ssm_b_glu", "att_rpb", "ffn2_norm", "final_norm"))
SMALL_LATE = ("ffn1_norm",)
WEIGHTS = ("ffn1_norm", "ffn1_w_gate", "ffn1_w_up", "ffn1_w_down", "mix_norm", "w_in") \
    + tuple(n + "_fwd" for n in SSM_DIR) + tuple(n + "_bwd" for n in SSM_DIR) \
    + ("ssm_d", "ssm_w_glu", "ssm_b_glu", "att_rpb", "w_branch_ssm", "w_branch_att", "w_out",
       "ffn2_norm", "ffn2_w_gate", "ffn2_w_up", "ffn2_w_down", "final_norm")


def _pad_rows(a, mult):
    pad = (-a.shape[-2]) % mult
    if pad:
        a = jnp.concatenate([a, jnp.zeros(a.shape[:-2] + (pad, a.shape[-1]), a.dtype)], axis=-2)
    return a


def _pack(arrays, row_mult):
    flat = jnp.concatenate([a.reshape(-1) for a in arrays])
    pad = (-flat.shape[0]) % PACK_COLS
    if pad:
        flat = jnp.concatenate([flat, jnp.zeros((pad,), flat.dtype)])
    return _pad_rows(flat.reshape(-1, PACK_COLS), row_mult)


def _unpack(slab, shapes):
    flat = slab.reshape(-1)
    out, at = [], 0
    for s in shapes:
        n = int(np.prod(s))
        out.append(flat[at:at + n].reshape(s))
        at += n
    return out


def _split_for_devices(g, axis):
    r, c = g.shape
    if axis == 1:
        return g.reshape(r, N_DEV, c // N_DEV).transpose(1, 0, 2).astype(BF16)
    return g.reshape(N_DEV, r // N_DEV, c).astype(BF16)


def _join_shards(gathered, axis):
    _, r, c = gathered.shape
    if axis == 1:
        return gathered.transpose(1, 0, 2).reshape(r, N_DEV * c)
    return gathered.reshape(N_DEV * r, c)


def _s5_direction_inputs(p, sfx, chain_len):
    bt_re = p["ssm_b_re" + sfx][0].transpose(0, 2, 1)
    bt_im = p["ssm_b_im" + sfx][0].transpose(0, 2, 1)
    raw = (p["ssm_a_re" + sfx][0], p["ssm_a_im" + sfx][0], p["ssm_log_dt" + sfx][0][:, None], bt_re, bt_im)
    lr, li, sr, si, bbr, bbi = _disc_fwd(*raw, chain_len,"s5_disc" + sfx)
    lam = jnp.stack([_s5_pack_lam(t) for t in (lr, li, sr, si)], axis=2)
    mats = (_s5_pack_b(bbr), _s5_pack_b(bbi), lam,
            _s5_pack_c(p["ssm_c_re" + sfx][0]), _s5_pack_c(-p["ssm_c_im" + sfx][0]))
    return raw, mats


def kernel(x, ffn1_norm, ffn1_w_gate, ffn1_w_up, ffn1_w_down, mix_norm, w_in, ssm_a_re_fwd, ssm_a_im_fwd, ssm_log_dt_fwd, ssm_b_re_fwd, ssm_b_im_fwd, ssm_c_re_fwd, ssm_c_im_fwd, ssm_a_re_bwd, ssm_a_im_bwd, ssm_log_dt_bwd, ssm_b_re_bwd, ssm_b_im_bwd, ssm_c_re_bwd, ssm_c_im_bwd, ssm_d, ssm_w_glu, ssm_b_glu, att_rpb, w_branch_ssm, w_branch_att, w_out, ffn2_norm, ffn2_w_gate, ffn2_w_up, ffn2_w_down, final_norm, loss_target, m_ffn1_norm, m_ffn1_w_gate, m_ffn1_w_up, m_ffn1_w_down, m_mix_norm, m_w_in, m_ssm_a_re_fwd, m_ssm_a_im_fwd, m_ssm_log_dt_fwd, m_ssm_b_re_fwd, m_ssm_b_im_fwd, m_ssm_c_re_fwd, m_ssm_c_im_fwd, m_ssm_a_re_bwd, m_ssm_a_im_bwd, m_ssm_log_dt_bwd, m_ssm_b_re_bwd, m_ssm_b_im_bwd, m_ssm_c_re_bwd, m_ssm_c_im_bwd, m_ssm_d, m_ssm_w_glu, m_ssm_b_glu, m_att_rpb, m_w_branch_ssm, m_w_branch_att, m_w_out, m_ffn2_norm, m_ffn2_w_gate, m_ffn2_w_up, m_ffn2_w_down, m_final_norm, v_ffn1_norm, v_ffn1_w_gate, v_ffn1_w_up, v_ffn1_w_down, v_mix_norm, v_w_in, v_ssm_a_re_fwd, v_ssm_a_im_fwd, v_ssm_log_dt_fwd, v_ssm_b_re_fwd, v_ssm_b_im_fwd, v_ssm_c_re_fwd, v_ssm_c_im_fwd, v_ssm_a_re_bwd, v_ssm_a_im_bwd, v_ssm_log_dt_bwd, v_ssm_b_re_bwd, v_ssm_b_im_bwd, v_ssm_c_re_bwd, v_ssm_c_im_bwd, v_ssm_d, v_ssm_w_glu, v_ssm_b_glu, v_att_rpb, v_w_branch_ssm, v_w_branch_att, v_w_out, v_ffn2_norm, v_ffn2_w_gate, v_ffn2_w_up, v_ffn2_w_down, v_final_norm):
    p = dict(locals())
    x = p["x"][0]
    target = p["loss_target"][0]
    T, D = x.shape

    shard = dict(zip([n for n, _ in BIG], _cast_shards([p[n] for n, _ in BIG])))
    ffn1_w = ("ffn1_w_gate", "ffn1_w_up", "ffn1_w_down")
    mix_w = ("w_in", "ssm_w_glu", "w_branch_ssm", "w_branch_att", "w_out")
    ffn2_w = ("ffn2_w_gate", "ffn2_w_up", "ffn2_w_down")
    gathered = dict(zip(ffn1_w, _gather_two_level([shard[n] for n in ffn1_w], "gather_ffn1")))
    full = lambda n: _join_shards(gathered[n], BIG_AXIS[n])

    h0 = x
    wg1, wu1, wd1 = [full(n) for n in ffn1_w]
    (h1, xn1, g1, u1), got = _ffn_fwd(h0, p["ffn1_norm"], wg1, wu1, wd1, "ffn1_fwd",
                                      _Comm("gather", [shard[n] for n in mix_w]))
    gathered.update(zip(mix_w, got))
    z, zb, un = _mixin_fwd(h1, p["mix_norm"], gathered["w_in"])
    W = SSM_WIDTH
    zp = _permute_rows(zb[:, :W])
    chain_len = T // SCAN_LANES // S5_NQ
    raw_f, mats_f = _s5_direction_inputs(p, "_fwd", chain_len)
    raw_b, mats_b = _s5_direction_inputs(p, "_bwd", chain_len)
    bre, bim, lam, cre, cimn = [jnp.stack([f, b]) for f, b in zip(mats_f, mats_b)]
    bre, bim, cre, cimn = [t.astype(BF16) for t in (bre, bim, cre, cimn)]
    (yp,), got = _s5_fwd(zp, bre, bim, lam, cre, cimn, _Comm("gather", [shard[n] for n in ffn2_w]))
    gathered.update(zip(ffn2_w, got))
    ypre = _unpermute_rows(yp)
    table = _att_masked_tables(_att_bias_table(p["att_rpb"][0]), T // GRID_W)
    ya = _att_fwd(zb, table)
    tail_w = (p["ssm_d"], full("ssm_w_glu"), p["ssm_b_glu"], full("w_branch_ssm"), full("w_branch_att"), full("w_out"))
    h2 = _merge_fwd(ypre, z, ya, h1, *tail_w)
    wg2, wu2, wd2 = [full(n) for n in ffn2_w]
    (h3, xn2, g2, u2), _ = _ffn_fwd(h2, p["ffn2_norm"], wg2, wu2, wd2, "ffn2_fwd")
    loss_part, dh3, d_final = _loss_head(h3, p["final_norm"][None], target)

    grads = {"final_norm": d_final[0]}
    to_send = lambda names: _Comm("exchange", [_split_for_devices(grads[n], BIG_AXIS[n]) for n in names])
    parts = {}
    (dh2, grads["ffn2_norm"], do2, a2, dg2, du2), _ = _ffn_bwd(
        dh3, h2, p["ffn2_norm"], g2, u2, wg2, wu2, wd2, "ffn2_bwd")
    grads["ffn2_w_gate"] = _xty(xn2, dg2, "ffn2_dw_gate")
    grads["ffn2_w_up"] = _xty(xn2, du2, "ffn2_dw_up")
    grads["ffn2_w_down"] = _xty(a2, do2, "ffn2_dw_down")
    (dypre, dzs_skip, dgs, dga, dya, grads["ssm_d"], grads["ssm_w_glu"], grads["ssm_b_glu"],
     grads["w_branch_ssm"], grads["w_branch_att"], grads["w_out"]) = _merge_bwd(dh2, ypre, z, ya, *tail_w)
    (dq, dk, dv, dtable), got = _att_bwd(zb, dya, table, to_send(ffn2_w))
    parts.update(zip(ffn2_w, got))
    grads["att_rpb"] = _att_bias_table_t(jnp.sum(dtable, axis=0))
    dyp = _permute_rows(dypre).astype(BF16)
    tail_names = ("ssm_w_glu", "w_branch_ssm", "w_branch_att", "w_out")
    (dzp, dbre, dbim, dlam, dcre, dcimn), got = _s5_bwd(zp, dyp, bre, bim, lam, cre, cimn, to_send(tail_names))
    parts.update(zip(tail_names, got))
    G, P = SSM_GROUPS, SSM_STATE
    for d, (sfx, raw) in enumerate((("_fwd", raw_f), ("_bwd", raw_b))):
        da_re, da_im, dldt, dbt_re, dbt_im = _disc_bwd(
            *raw, dlam[d, :, :, 0, :].reshape(G, P), dlam[d, :, :, 1, :].reshape(G, P),
            _s5_unpack_b(dbre[d]), _s5_unpack_b(dbim[d]), "s5_disc_grad" + sfx)
        grads["ssm_a_re" + sfx] = da_re
        grads["ssm_a_im" + sfx] = da_im
        grads["ssm_log_dt" + sfx] = dldt[:, 0]
        grads["ssm_b_re" + sfx] = dbt_re.transpose(0, 2, 1)
        grads["ssm_b_im" + sfx] = dbt_im.transpose(0, 2, 1)
        grads["ssm_c_re" + sfx] = _s5_unpack_c(dcre[d])
        grads["ssm_c_im" + sfx] = -_s5_unpack_c(dcimn[d])
    dzs = _unpermute_rows(dzp) + dzs_skip
    dz = jnp.concatenate([dzs.astype(BF16), dq, dk, dv, dgs, dga], axis=1)
    dh1, grads["mix_norm"] = _mixin_bwd(dz, dh2, h1, p["mix_norm"], gathered["w_in"])
    grads["w_in"] = _xty(un, dz, "dw_in")
    pack_small = lambda names, src, pre: _pack([src[pre + n].astype(F32) for n in names], 8)
    early = _Comm(["exchange", "gather"],
                  [_split_for_devices(grads["w_in"], 1), pack_small(SMALL_EARLY, grads, "")])
    (dh0, grads["ffn1_norm"], do1, a1, dg1, du1), (parts["w_in"], got_early) = _ffn_bwd(
        dh1, h0, p["ffn1_norm"], g1, u1, wg1, wu1, wd1, "ffn1_bwd", early)
    grads["ffn1_w_down"] = _xty(a1, do1, "ffn1_dw_down")
    grads["ffn1_w_gate"], (parts["ffn1_w_down"],) = _xty(xn1, dg1, "ffn1_dw_gate", to_send(("ffn1_w_down",)))
    grads["ffn1_w_up"], (parts["ffn1_w_gate"],) = _xty(xn1, du1, "ffn1_dw_up", to_send(("ffn1_w_gate",)))
    last = _Comm(["exchange", "gather"],
                 [_split_for_devices(grads["ffn1_w_up"], 1), pack_small(SMALL_LATE, grads, "")])
    parts["ffn1_w_up"], got_late = _comm_call(last, "exchange_last")
    got_small = jnp.concatenate([got_early, got_late], axis=1)

    results = {}
    for n, _ in BIG:
        outs = _adamw(parts[n], p[n][0], p["m_" + n][0], p["v_" + n][0], "adamw_" + n)
        results[n] = [o[None] for o in outs]
    early_rows = got_early.shape[1]
    slab = lambda pre: jnp.concatenate([pack_small(SMALL_EARLY, p, pre), pack_small(SMALL_LATE, p, pre)], axis=0)
    small_out = _adamw(got_small, slab(""), slab("m_"), slab("v_"), "adamw_small")
    for names, rows in ((SMALL_EARLY, slice(0, early_rows)), (SMALL_LATE, slice(early_rows, None))):
        shapes = [p[n].shape for n in names]
        for n, vals in zip(names, zip(*[_unpack(out[rows], shapes) for out in small_out])):
            results[n] = list(vals)

    loss = lax.psum(loss_part[0, 0], ("x", "y", "c"))
    out = [loss, dh0[None]]
    for kind in range(4):
        out += [results[n][kind] for n in WEIGHTS]
    return tuple(out)
```

```python
import functools
import math

import numpy as np
import jax
import jax.numpy as jnp
from jax import lax
from jax.experimental import pallas as pl
from jax.experimental.pallas import tpu as pltpu

F32 = jnp.float32
BF16 = jnp.bfloat16
MESH_ID = pl.DeviceIdType.MESH

SSM_GROUP = 16
SSM_GROUPS = 32
SSM_STATE = 64
SSM_WIDTH = 512
ATT_HEADS = 8
ATT_HEAD_DIM = 64
ATT_WIDTH = 512
GRID_W = 64
WIN_H = 8
WIN_W = 16
EPS = 1e-6
NEG_INF = -1e30
ADAM_LR = 0.001
ADAM_B1 = 0.9
ADAM_B2 = 0.999
ADAM_EPS = 1e-08
ADAM_WD = 0.01
ADAM_STEP = 10

N_DEV = 8
V7X_VMEM_BYTES = 64 * 1024 * 1024
VMEM_LIMIT = V7X_VMEM_BYTES - 8 * 1024 * 1024
SCAN_LANES = 8
ATT_ROWS = 4


def _cparams(sem, vmem=None):
    return pltpu.CompilerParams(dimension_semantics=sem, vmem_limit_bytes=vmem)


def _dot(a, b):
    return jnp.dot(a, b, preferred_element_type=F32)


def _dot_nt(a, b):
    return lax.dot_general(a, b, (((1,), (1,)), ((), ())), preferred_element_type=F32)


def _dot_tn(a, b):
    return lax.dot_general(a, b, (((0,), (0,)), ((), ())), preferred_element_type=F32)


def _rms(h):
    return lax.rsqrt(jnp.mean(h * h, axis=-1, keepdims=True) + EPS)


def _rms_bwd(h, r, v):
    return r * v - h * (r * r * r) * jnp.mean(h * v, axis=-1, keepdims=True)


def _col_sum(x):
    return jnp.sum(x, axis=0, keepdims=True)


def _my_place():
    return lax.axis_index("x"), lax.axis_index("y"), lax.axis_index("c")


def _flat(px, py, pc):
    return 4 * px + 2 * py + pc


class _Comm:
    def __init__(self, kind, arrays):
        self.arrays = list(arrays)
        self.n = len(self.arrays)
        self.kinds = [kind] * self.n if isinstance(kind, str) else list(kind)

    def out_shapes(self):
        return [jax.ShapeDtypeStruct((N_DEV,) + a.shape if k == "gather" else a.shape, a.dtype)
                for k, a in zip(self.kinds, self.arrays)]

    def scratch(self):
        return [pltpu.SemaphoreType.DMA((7 * self.n,)), pltpu.SemaphoreType.DMA((7 * self.n,)),
                pltpu.SemaphoreType.DMA((self.n,))]

    def run(self, srcs, dsts, sems, start):
        send_sems, recv_sems, local_sems = sems
        x, y, c = _my_place()
        mine = _flat(x, y, c)
        for a, (src, dst) in enumerate(zip(srcs, dsts)):
            whole = self.kinds[a] == "gather"
            local = pltpu.make_async_copy(src if whole else src.at[mine], dst.at[mine], local_sems.at[a])
            local.start() if start else local.wait()
            for k in range(1, N_DEV):
                px = 1 - x if k & 4 else x
                py = 1 - y if k & 2 else y
                pc = 1 - c if k & 1 else c
                cp = pltpu.make_async_remote_copy(
                    src_ref=src if whole else src.at[_flat(px, py, pc)], dst_ref=dst.at[mine],
                    send_sem=send_sems.at[7 * a + k - 1], recv_sem=recv_sems.at[7 * a + k - 1],
                    device_id=(px, py, pc), device_id_type=MESH_ID)
                cp.start() if start else cp.wait()


_HBM = pl.BlockSpec(memory_space=pltpu.HBM)


def _comm_call(comm, name):
    def body(*refs):
        srcs, dsts, sems = refs[:comm.n], refs[comm.n:2 * comm.n], refs[2 * comm.n:]
        comm.run(srcs, dsts, sems, True)
        comm.run(srcs, dsts, sems, False)

    return pl.pallas_call(body, name=name, in_specs=[_HBM] * comm.n, out_specs=[_HBM] * comm.n,
                          out_shape=comm.out_shapes(), scratch_shapes=comm.scratch())(*comm.arrays)


def _pallas(core, name, grid, in_specs, out_specs, out_shape, scratch, sem, args, comm=None):
    if comm is None:
        out = pl.pallas_call(core, name=name, grid=grid, in_specs=in_specs, out_specs=out_specs,
                             out_shape=out_shape, scratch_shapes=scratch,
                             compiler_params=_cparams(sem, VMEM_LIMIT))(*args)
        return out, []
    n_in, n_out, n_scr, n = len(in_specs), len(out_specs), len(scratch), comm.n

    def body(*refs):
        ins, srcs = refs[:n_in], refs[n_in:n_in + n]
        outs, dsts = refs[n_in + n:n_in + n + n_out], refs[n_in + n + n_out:n_in + 2 * n + n_out]
        scr, sems = refs[n_in + 2 * n + n_out:n_in + 2 * n + n_out + n_scr], refs[n_in + 2 * n + n_out + n_scr:]
        ids = [pl.program_id(k) for k in range(len(grid))]
        first = functools.reduce(lambda a, b: a & b, [i == 0 for i in ids])
        last = functools.reduce(lambda a, b: a & b, [i == g - 1 for i, g in zip(ids, grid)])

        @pl.when(first)
        def _():
            comm.run(srcs, dsts, sems, True)

        core(*ins, *outs, *scr)

        @pl.when(last)
        def _():
            comm.run(srcs, dsts, sems, False)

    out = pl.pallas_call(
        body, name=name, grid=grid, in_specs=list(in_specs) + [_HBM] * n, out_specs=list(out_specs) + [_HBM] * n,
        out_shape=list(out_shape) + comm.out_shapes(), scratch_shapes=list(scratch) + comm.scratch(),
        compiler_params=_cparams(("arbitrary",) * len(grid), VMEM_LIMIT))(*args, *comm.arrays)
    return out[:n_out], out[n_out:]


def _ffn_tiles(T, F, wide):
    if F % 1408 == 0:
        return min(T, 512 if wide else 256), 1408
    return min(T, 1024), 256 if F % 256 == 0 else F


def _ffn_fwd(h, gain, wg, wu, wd, name, comm=None):
    T, D = h.shape
    F = wg.shape[0]
    tm, tf = _ffn_tiles(T, F, wide=True)
    nj = F // tf

    def body(h_ref, gain_ref, wg_ref, wu_ref, wd_ref, ho_ref, xn_ref, g_ref, u_ref, acc_ref):
        j = pl.program_id(1)

        @pl.when(j == 0)
        def _():
            hh = h_ref[...]
            xn_ref[...] = (hh * _rms(hh) * gain_ref[...]).astype(BF16)
            acc_ref[...] = jnp.zeros_like(acc_ref)

        xn = xn_ref[...]
        g = _dot_nt(xn, wg_ref[...])
        u = _dot_nt(xn, wu_ref[...])
        g_ref[...] = g.astype(BF16)
        u_ref[...] = u.astype(BF16)
        a = (g * jax.nn.sigmoid(g) * u).astype(BF16)
        acc_ref[...] += _dot(a, wd_ref[...])

        @pl.when(j == nj - 1)
        def _():
            ho_ref[...] = h_ref[...] + 0.5 * acc_ref[...]

    return _pallas(
        body, name, (T // tm, nj),
        [pl.BlockSpec((tm, D), lambda i, j: (i, 0)),
         pl.BlockSpec((1, D), lambda i, j: (0, 0)),
         pl.BlockSpec((tf, D), lambda i, j: (j, 0)),
         pl.BlockSpec((tf, D), lambda i, j: (j, 0)),
         pl.BlockSpec((tf, D), lambda i, j: (j, 0))],
        [pl.BlockSpec((tm, D), lambda i, j: (i, 0)),
         pl.BlockSpec((tm, D), lambda i, j: (i, 0)),
         pl.BlockSpec((tm, tf), lambda i, j: (i, j)),
         pl.BlockSpec((tm, tf), lambda i, j: (i, j))],
        [jax.ShapeDtypeStruct((T, D), F32), jax.ShapeDtypeStruct((T, D), BF16),
         jax.ShapeDtypeStruct((T, F), BF16), jax.ShapeDtypeStruct((T, F), BF16)],
        [pltpu.VMEM((tm, D), F32)], ("parallel", "arbitrary"), (h, gain, wg, wu, wd), comm)


def _ffn_bwd(dho, h, gain, g, u, wg, wu, wd, name, comm=None):
    T, D = h.shape
    F = wg.shape[0]
    tm, tf = _ffn_tiles(T, F, wide=False)
    nj = F // tf

    def body(dho_ref, h_ref, gain_ref, g_ref, u_ref, wg_ref, wu_ref, wd_ref,
             dh_ref, dgain_ref, do_ref, a_ref, dg_ref, du_ref, acc_ref):
        i = pl.program_id(0)
        j = pl.program_id(1)

        @pl.when(j == 0)
        def _():
            do_ref[...] = (0.5 * dho_ref[...]).astype(BF16)
            acc_ref[...] = jnp.zeros_like(acc_ref)

        @pl.when((i == 0) & (j == 0))
        def _():
            dgain_ref[...] = jnp.zeros_like(dgain_ref)

        da = _dot_nt(do_ref[...], wd_ref[...])
        gg = g_ref[...].astype(F32)
        uu = u_ref[...].astype(F32)
        s = jax.nn.sigmoid(gg)
        sl = gg * s
        a_ref[...] = (sl * uu).astype(BF16)
        dg = (da * uu * (s * (1.0 + gg * (1.0 - s)))).astype(BF16)
        du = (da * sl).astype(BF16)
        dg_ref[...] = dg
        du_ref[...] = du
        acc_ref[...] += _dot(dg, wg_ref[...]) + _dot(du, wu_ref[...])

        @pl.when(j == nj - 1)
        def _():
            hh = h_ref[...]
            r = _rms(hh)
            dxn = acc_ref[...]
            dgain_ref[...] += _col_sum(dxn * hh * r)
            dh_ref[...] = dho_ref[...] + _rms_bwd(hh, r, dxn * gain_ref[...])

    return _pallas(
        body, name, (T // tm, nj),
        [pl.BlockSpec((tm, D), lambda i, j: (i, 0)),
         pl.BlockSpec((tm, D), lambda i, j: (i, 0)),
         pl.BlockSpec((1, D), lambda i, j: (0, 0)),
         pl.BlockSpec((tm, tf), lambda i, j: (i, j)),
         pl.BlockSpec((tm, tf), lambda i, j: (i, j)),
         pl.BlockSpec((tf, D), lambda i, j: (j, 0)),
         pl.BlockSpec((tf, D), lambda i, j: (j, 0)),
         pl.BlockSpec((tf, D), lambda i, j: (j, 0))],
        [pl.BlockSpec((tm, D), lambda i, j: (i, 0)),
         pl.BlockSpec((1, D), lambda i, j: (0, 0)),
         pl.BlockSpec((tm, D), lambda i, j: (i, 0)),
         pl.BlockSpec((tm, tf), lambda i, j: (i, j)),
         pl.BlockSpec((tm, tf), lambda i, j: (i, j)),
         pl.BlockSpec((tm, tf), lambda i, j: (i, j))],
        [jax.ShapeDtypeStruct((T, D), F32), jax.ShapeDtypeStruct((1, D), F32),
         jax.ShapeDtypeStruct((T, D), BF16), jax.ShapeDtypeStruct((T, F), BF16),
         jax.ShapeDtypeStruct((T, F), BF16), jax.ShapeDtypeStruct((T, F), BF16)],
        [pltpu.VMEM((tm, D), F32)], ("arbitrary", "arbitrary"), (dho, h, gain, g, u, wg, wu, wd), comm)


def _xty(x, y, name, comm=None, col_shards=1):
    T, K = x.shape
    N = y.shape[1]
    tt = min(T, 1024)
    tk = K if K <= 1024 else (1408 if K % 1408 == 0 else K)
    tn = N if N <= 1024 else (1408 if N % 1408 == 0 else (1024 if N % 1024 == 0 else N))
    nt = T // tt
    ws = N // col_shards
    per = tn // ws if col_shards > 1 else 1
    assert col_shards == 1 or (tn % ws == 0 and ws % 128 == 0)

    def body(x_ref, y_ref, o_ref, acc_ref):
        t = pl.program_id(2)

        @pl.when(t == 0)
        def _():
            acc_ref[...] = jnp.zeros_like(acc_ref)

        acc_ref[...] += _dot_tn(x_ref[...], y_ref[...])

        @pl.when(t == nt - 1)
        def _():
            if col_shards == 1:
                o_ref[...] = acc_ref[...].astype(BF16)
            else:
                for s in range(per):
                    o_ref[s] = acc_ref[:, s * ws:(s + 1) * ws].astype(BF16)

    if col_shards == 1:
        out_spec = pl.BlockSpec((tk, tn), lambda k, n, t: (k, n))
        out_shape = jax.ShapeDtypeStruct((K, N), BF16)
    else:
        out_spec = pl.BlockSpec((per, tk, ws), lambda k, n, t: (n, k, 0))
        out_shape = jax.ShapeDtypeStruct((col_shards, K, ws), BF16)
    (out,), got = _pallas(
        body, name, (K // tk, N // tn, nt),
        [pl.BlockSpec((tt, tk), lambda k, n, t: (t, k)), pl.BlockSpec((tt, tn), lambda k, n, t: (t, n))],
        [out_spec], [out_shape], [pltpu.VMEM((tk, tn), F32)],
        ("parallel", "parallel", "arbitrary"), (x, y), comm)
    return out if comm is None else (out, got)


def _mixin_fwd(h, gain, w_in):
    T, D = h.shape
    nn, _, tn = w_in.shape
    N = nn * tn
    tm = min(T, 1024)

    def body(h_ref, gain_ref, w_ref, z_ref, zb_ref, un_ref):
        @pl.when(pl.program_id(1) == 0)
        def _():
            hh = h_ref[...]
            un_ref[...] = (hh * _rms(hh) * gain_ref[...]).astype(BF16)

        z = _dot(un_ref[...], w_ref[...])
        z_ref[...] = z
        zb_ref[...] = z.astype(BF16)

    return pl.pallas_call(
        body, name="mixin_fwd", grid=(T // tm, nn),
        in_specs=[pl.BlockSpec((tm, D), lambda i, n: (i, 0)),
                  pl.BlockSpec((1, D), lambda i, n: (0, 0)),
                  pl.BlockSpec((None, D, tn), lambda i, n: (n, 0, 0))],
        out_specs=[pl.BlockSpec((tm, tn), lambda i, n: (i, n)),
                   pl.BlockSpec((tm, tn), lambda i, n: (i, n)),
                   pl.BlockSpec((tm, D), lambda i, n: (i, 0))],
        out_shape=[jax.ShapeDtypeStruct((T, N), F32), jax.ShapeDtypeStruct((T, N), BF16),
                   jax.ShapeDtypeStruct((T, D), BF16)],
        compiler_params=_cparams(("parallel", "arbitrary"), VMEM_LIMIT),
    )(h, gain, w_in)


def _mixin_bwd(dz, dh_res, h, gain, w_in):
    T, D = h.shape
    nn, _, tn = w_in.shape
    tm = min(T, 1024)

    def body(dz_ref, dres_ref, h_ref, gain_ref, w_ref, dh_ref, dgain_ref, acc_ref):
        i = pl.program_id(0)
        n = pl.program_id(1)

        @pl.when(n == 0)
        def _():
            acc_ref[...] = jnp.zeros_like(acc_ref)

        @pl.when((i == 0) & (n == 0))
        def _():
            dgain_ref[...] = jnp.zeros_like(dgain_ref)

        acc_ref[...] += _dot_nt(dz_ref[...], w_ref[...])

        @pl.when(n == nn - 1)
        def _():
            hh = h_ref[...]
            r = _rms(hh)
            dun = acc_ref[...]
            dgain_ref[...] += _col_sum(dun * hh * r)
            dh_ref[...] = dres_ref[...] + _rms_bwd(hh, r, dun * gain_ref[...])

    return pl.pallas_call(
        body, name="mixin_bwd", grid=(T // tm, nn),
        in_specs=[pl.BlockSpec((tm, tn), lambda i, n: (i, n)),
                  pl.BlockSpec((tm, D), lambda i, n: (i, 0)),
                  pl.BlockSpec((tm, D), lambda i, n: (i, 0)),
                  pl.BlockSpec((1, D), lambda i, n: (0, 0)),
                  pl.BlockSpec((None, D, tn), lambda i, n: (n, 0, 0))],
        out_specs=[pl.BlockSpec((tm, D), lambda i, n: (i, 0)),
                   pl.BlockSpec((1, D), lambda i, n: (0, 0))],
        out_shape=[jax.ShapeDtypeStruct((T, D), F32), jax.ShapeDtypeStruct((1, D), F32)],
        scratch_shapes=[pltpu.VMEM((tm, D), F32)],
        compiler_params=_cparams(("arbitrary", "arbitrary"), VMEM_LIMIT),
    )(dz, dh_res, h, gain, w_in)


def _loss_head(h, gain, target):
    T, D = h.shape
    tm = min(T, 1024)

    def body(h_ref, gain_ref, t_ref, loss_ref, dh_ref, dgain_ref):
        @pl.when(pl.program_id(0) == 0)
        def _():
            loss_ref[...] = jnp.zeros_like(loss_ref)
            dgain_ref[...] = jnp.zeros_like(dgain_ref)

        hh = h_ref[...]
        r = _rms(hh)
        e = hh * r * gain_ref[...] - t_ref[...]
        loss_ref[...] += (0.5 / D) * jnp.sum(e * e)
        dy = e * (1.0 / D)
        dgain_ref[...] += _col_sum(dy * hh * r)
        dh_ref[...] = _rms_bwd(hh, r, dy * gain_ref[...])

    return pl.pallas_call(
        body, name="loss_head", grid=(T // tm,),
        in_specs=[pl.BlockSpec((tm, D), lambda i: (i, 0)),
                  pl.BlockSpec((1, D), lambda i: (0, 0)),
                  pl.BlockSpec((tm, D), lambda i: (i, 0))],
        out_specs=[pl.BlockSpec((1, 128), lambda i: (0, 0)),
                   pl.BlockSpec((tm, D), lambda i: (i, 0)),
                   pl.BlockSpec((1, D), lambda i: (0, 0))],
        out_shape=[jax.ShapeDtypeStruct((1, 128), F32), jax.ShapeDtypeStruct((T, D), F32),
                   jax.ShapeDtypeStruct((1, D), F32)],
        compiler_params=_cparams(("arbitrary",), VMEM_LIMIT),
    )(h, gain, target)


def _adamw(parts, w, m, v, name):
    R, C = w.shape
    mult = 16 if parts.dtype == BF16 else 8
    tr = max(t for t in range(mult, min(R, 512) + 1, mult) if R % t == 0)
    c1 = 1.0 - ADAM_B1 ** ADAM_STEP
    c2 = 1.0 - ADAM_B2 ** ADAM_STEP

    def body(p_ref, w_ref, m_ref, v_ref, g_ref, d_ref, nm_ref, nv_ref):
        g = p_ref[0].astype(F32)
        for k in range(1, N_DEV):
            g = g + p_ref[k].astype(F32)
        mm = ADAM_B1 * m_ref[...] + (1.0 - ADAM_B1) * g
        vv = ADAM_B2 * v_ref[...] + (1.0 - ADAM_B2) * (g * g)
        g_ref[...] = g
        nm_ref[...] = mm
        nv_ref[...] = vv
        d_ref[...] = -ADAM_LR * ((mm / c1) / (jnp.sqrt(vv / c2) + ADAM_EPS) + ADAM_WD * w_ref[...])

    spec = pl.BlockSpec((tr, C), lambda i: (i, 0))
    return pl.pallas_call(
        body, name=name, grid=(R // tr,),
        in_specs=[pl.BlockSpec((N_DEV, tr, C), lambda i: (0, i, 0)), spec, spec, spec],
        out_specs=[spec, spec, spec, spec],
        out_shape=[jax.ShapeDtypeStruct((R, C), F32)] * 4,
        compiler_params=_cparams(("parallel",), VMEM_LIMIT),
    )(parts, w, m, v)


S5_NS = 256
S5_NH = 2
S5_NCB = 4
S5_RC = 512
S5_NQ = 4
S5_GROUP = 2


def _disc_math(a_re, a_im, log_dt, bt_re, bt_im):
    dt = jnp.exp(log_dt)
    zr, zi = a_re * dt, a_im * dt
    mag = jnp.exp(zr)
    lb_re, lb_im = mag * jnp.cos(zi), mag * jnp.sin(zi)
    den = a_re * a_re + a_im * a_im
    nr, ni = lb_re - 1.0, lb_im
    f_re = (nr * a_re + ni * a_im) / den
    f_im = (ni * a_re - nr * a_im) / den
    bb_re = f_re[:, None, :] * bt_re - f_im[:, None, :] * bt_im
    bb_im = f_re[:, None, :] * bt_im + f_im[:, None, :] * bt_re
    return lb_re, lb_im, bb_re, bb_im


def _disc_fwd(a_re, a_im, log_dt, bt_re, bt_im, chain_len, name):
    G, P = a_re.shape
    C = bt_re.shape[1]
    n_sq = int(round(math.log2(chain_len)))
    assert 2 ** n_sq == chain_len

    def body(a_re_ref, a_im_ref, ldt_ref, br_ref, bi_ref, lr_ref, li_ref, sr_ref, si_ref, bbr_ref, bbi_ref):
        lr, li, bbr, bbi = _disc_math(a_re_ref[...], a_im_ref[...], ldt_ref[...], br_ref[...], bi_ref[...])
        lr_ref[...] = lr
        li_ref[...] = li
        bbr_ref[...] = bbr
        bbi_ref[...] = bbi
        pr, pi = lr, li
        for _ in range(n_sq):
            pr, pi = pr * pr - pi * pi, 2.0 * pr * pi
        sr_ref[...] = pr
        si_ref[...] = pi

    s2 = jax.ShapeDtypeStruct((G, P), F32)
    s3 = jax.ShapeDtypeStruct((G, C, P), F32)
    return pl.pallas_call(body, name=name, out_shape=[s2, s2, s2, s2, s3, s3])(a_re, a_im, log_dt, bt_re, bt_im)


def _disc_bwd(a_re, a_im, log_dt, bt_re, bt_im, d_lr, d_li, d_bbr, d_bbi, name):
    G, P = a_re.shape
    C = bt_re.shape[1]

    def body(a_re_ref, a_im_ref, ldt_ref, br_ref, bi_ref, c1, c2, c3, c4, o1, o2, o3, o4, o5):
        _, vjp = jax.vjp(_disc_math, a_re_ref[...], a_im_ref[...], ldt_ref[...], br_ref[...], bi_ref[...])
        o1[...], o2[...], o3[...], o4[...], o5[...] = vjp((c1[...], c2[...], c3[...], c4[...]))

    s2 = jax.ShapeDtypeStruct((G, P), F32)
    s3 = jax.ShapeDtypeStruct((G, C, P), F32)
    return pl.pallas_call(body, name=name, out_shape=[s2, s2, jax.ShapeDtypeStruct((G, 1), F32), s3, s3])(
        a_re, a_im, log_dt, bt_re, bt_im, d_lr, d_li, d_bbr, d_bbi)


def _row_block(ib):
    return pl.ds(pl.multiple_of(ib * SCAN_LANES, SCAN_LANES), SCAN_LANES)


def _chain_block(j, i, ascending, n_blocks):
    at = j * (n_blocks // S5_NQ) + i
    return _row_block(jnp.where(ascending, at, n_blocks - 1 - at))


def _unrolled_loop(n, unroll, body, carry):
    trips = n // unroll
    carry = lax.fori_loop(
        0, trips, lambda t, c: functools.reduce(lambda cc, u: body(t * unroll + u, cc), range(unroll), c), carry)
    for i in range(trips * unroll, n):
        carry = body(i, carry)
    return carry


def _cmul_add(lr, li, sr, si, xr, xi):
    return lr * sr - li * si + xr, lr * si + li * sr + xi


def _scan(xr_ref, xi_ref, lr, li, init, ascending, n_blocks, store):
    steps = n_blocks // S5_NQ
    if not store:
        def step(i, carry):
            blocks = [_chain_block(j, i, ascending, n_blocks) for j in range(S5_NQ)]
            return tuple(_cmul_add(lr, li, sr, si, xr_ref[rows, :], xi_ref[rows, :])
                         for (sr, si), rows in zip(carry, blocks))

        return _unrolled_loop(steps, 4, step, init)

    group = S5_GROUP
    assert steps % group == 0

    def trip(t, carry):
        blocks = [[_chain_block(j, t * group + u, ascending, n_blocks) for j in range(S5_NQ)] for u in range(group)]
        xs = [[(xr_ref[rows, :], xi_ref[rows, :]) for rows in row] for row in blocks]
        states = list(carry)
        done = []
        for u in range(group):
            states = [_cmul_add(lr, li, sr, si, xr, xi) for (sr, si), (xr, xi) in zip(states, xs[u])]
            done.append(states)
        for u in range(group):
            for rows, (nr, ni) in zip(blocks[u], done[u]):
                xr_ref[rows, :] = nr
                xi_ref[rows, :] = ni
        return tuple(states)

    return lax.fori_loop(0, steps // group, trip, init)


def _segment_starts(w, lsr, lsi, ascending):
    shape = w[0][0].shape
    row = lax.broadcasted_iota(jnp.int32, shape, 0)
    keep = row != jnp.where(ascending, 0, SCAN_LANES - 1)

    def shift(t):
        t = jnp.where(ascending, pltpu.roll(t, 1, 0), pltpu.roll(t, SCAN_LANES - 1, 0))
        return jnp.where(keep, t, 0.0)

    zero = jnp.zeros(shape, F32)
    c = [(zero, zero)] * S5_NQ
    for _ in range(SCAN_LANES):
        tr, ti = _cmul_add(lsr, lsi, *c[-1], *w[-1])
        c[0] = (shift(tr), shift(ti))
        for j in range(1, S5_NQ):
            c[j] = _cmul_add(lsr, lsi, *c[j - 1], *w[j - 1])
    return tuple(c)


def _first_pass(xr_ref, xi_ref, lam_ref, ascending, n_blocks, conj):
    shape = (SCAN_LANES, xr_ref.shape[1])
    sign = -1.0 if conj else 1.0
    lr = jnp.broadcast_to(lam_ref[0:1, :], shape)
    li = sign * jnp.broadcast_to(lam_ref[1:2, :], shape)
    lsr = jnp.broadcast_to(lam_ref[2:3, :], shape)
    lsi = sign * jnp.broadcast_to(lam_ref[3:4, :], shape)
    zero = jnp.zeros(shape, F32)
    w = _scan(xr_ref, xi_ref, lr, li, ((zero, zero),) * S5_NQ, ascending, n_blocks, store=False)
    return _segment_starts(w, lsr, lsi, ascending), lr, li


def _s5_specs(T):
    NS = S5_NS
    tok = pl.BlockSpec((T, 128), lambda c, d, h: (0, c))
    b_spec = pl.BlockSpec((None, None, None, 128, NS), lambda c, d, h: (d, c, h, 0, 0))
    c_spec = pl.BlockSpec((None, None, None, NS, 128), lambda c, d, h: (d, c, h, 0, 0))
    lam_spec = pl.BlockSpec((None, None, None, 4, NS), lambda c, d, h: (d, c, h, 0, 0))
    return tok, b_spec, c_spec, lam_spec


def _s5_fwd(zp, bre, bim, lam, cre, cimn, comm=None):
    T = zp.shape[0]
    NS = S5_NS
    nb = T // SCAN_LANES
    rc = min(S5_RC, T)
    tok, b_spec, c_spec, lam_spec = _s5_specs(T)

    def body(zp_ref, bre_ref, bim_ref, lam_ref, cre_ref, cim_ref, y_ref, xr_ref, xi_ref):
        d = pl.program_id(1)
        ascending = d == 0

        @pl.when((d == 0) & (pl.program_id(2) == 0))
        def _():
            y_ref[...] = jnp.zeros_like(y_ref)

        def proj(c, _):
            rows = pl.ds(pl.multiple_of(c * rc, rc), rc)
            zz = zp_ref[rows, :]
            xr_ref[rows, :] = _dot(zz, bre_ref[...])
            xi_ref[rows, :] = _dot(zz, bim_ref[...])
            return 0

        lax.fori_loop(0, T // rc, proj, 0)
        starts, lr, li = _first_pass(xr_ref, xi_ref, lam_ref, ascending, nb, conj=False)
        _scan(xr_ref, xi_ref, lr, li, starts, ascending, nb, store=True)

        def outp(c, _):
            rows = pl.ds(pl.multiple_of(c * rc, rc), rc)
            y_ref[rows, :] += (_dot(xr_ref[rows, :].astype(BF16), cre_ref[...])
                               + _dot(xi_ref[rows, :].astype(BF16), cim_ref[...]))
            return 0

        lax.fori_loop(0, T // rc, outp, 0)

    return _pallas(
        body, "s5_fwd", (S5_NCB, 2, S5_NH),
        [tok, b_spec, b_spec, lam_spec, c_spec, c_spec], [tok],
        [jax.ShapeDtypeStruct((T, SSM_WIDTH), F32)],
        [pltpu.VMEM((T, NS), F32), pltpu.VMEM((T, NS), F32)],
        ("parallel", "arbitrary", "arbitrary"), (zp, bre, bim, lam, cre, cimn), comm)


def _s5_bwd(zp, dyp, bre, bim, lam, cre, cimn, comm=None):
    T = zp.shape[0]
    NS, NH = S5_NS, S5_NH
    nb = T // SCAN_LANES
    rc = min(S5_RC, T)
    tok, b_spec, c_spec, lam_spec = _s5_specs(T)
    dlam_spec = pl.BlockSpec((None, None, None, 2, NS), lambda c, d, h: (d, c, h, 0, 0))

    def body(zp_ref, dyp_ref, bre_ref, bim_ref, lam_ref, cre_ref, cim_ref,
             dzp_ref, dbre_ref, dbim_ref, dlam_ref, dcre_ref, dcim_ref,
             sr_ref, si_ref, gr_ref, gi_ref):
        d = pl.program_id(1)
        ascending = d == 0
        g_ascending = d != 0

        @pl.when((d == 0) & (pl.program_id(2) == 0))
        def _():
            dzp_ref[...] = jnp.zeros_like(dzp_ref)

        dcre_ref[...] = jnp.zeros_like(dcre_ref)
        dcim_ref[...] = jnp.zeros_like(dcim_ref)
        dbre_ref[...] = jnp.zeros_like(dbre_ref)
        dbim_ref[...] = jnp.zeros_like(dbim_ref)

        def proj(c, _):
            rows = pl.ds(pl.multiple_of(c * rc, rc), rc)
            zz = zp_ref[rows, :]
            sr_ref[rows, :] = _dot(zz, bre_ref[...])
            si_ref[rows, :] = _dot(zz, bim_ref[...])
            dy = dyp_ref[rows, :]
            gr_ref[rows, :] = _dot_nt(dy, cre_ref[...])
            gi_ref[rows, :] = _dot_nt(dy, cim_ref[...])
            return 0

        lax.fori_loop(0, T // rc, proj, 0)
        s_starts, lr, li = _first_pass(sr_ref, si_ref, lam_ref, ascending, nb, conj=False)
        _scan(sr_ref, si_ref, lr, li, s_starts, ascending, nb, store=True)
        g_starts, lr, lic = _first_pass(gr_ref, gi_ref, lam_ref, g_ascending, nb, conj=True)

        steps = nb // S5_NQ
        group = S5_GROUP
        assert steps % group == 0

        def gtrip(t, carry, last):
            g, (ar, ai) = carry
            first = t * group
            blocks = [[_chain_block(j, first + u, g_ascending, nb) for j in range(S5_NQ)] for u in range(group)]
            direct = [[(gr_ref[rows, :], gi_ref[rows, :]) for rows in row] for row in blocks]
            done = []
            for u in range(group):
                new = []
                for j, ((g_r, g_i), (d_r, d_i)) in enumerate(zip(g, direct[u])):
                    n_r, n_i = _cmul_add(lr, lic, g_r, g_i, d_r, d_i)
                    if last and u == group - 1:
                        s_r, s_i = s_starts[S5_NQ - 1 - j]
                    else:
                        prev = _chain_block(j, first + u + 1, g_ascending, nb)
                        s_r, s_i = sr_ref[prev, :], si_ref[prev, :]
                    ar = ar + n_r * s_r + n_i * s_i
                    ai = ai + n_i * s_r - n_r * s_i
                    new.append((n_r, n_i))
                g = new
                done.append(new)
            for u in range(group):
                for rows, (n_r, n_i) in zip(blocks[u], done[u]):
                    gr_ref[rows, :] = n_r
                    gi_ref[rows, :] = n_i
            return tuple(g), (ar, ai)

        zero = jnp.zeros((SCAN_LANES, NS), F32)
        carry = lax.fori_loop(0, steps // group - 1, lambda t, c: gtrip(t, c, False), (g_starts, (zero, zero)))
        _, (ar, ai) = gtrip(steps // group - 1, carry, True)
        dlam_ref[0:1, :] = _col_sum(ar)
        dlam_ref[1:2, :] = _col_sum(ai)

        def grads(c, _):
            rows = pl.ds(pl.multiple_of(c * rc, rc), rc)
            zz = zp_ref[rows, :]
            dy = dyp_ref[rows, :]
            g_rb = gr_ref[rows, :].astype(BF16)
            g_ib = gi_ref[rows, :].astype(BF16)
            dcre_ref[...] += _dot_tn(sr_ref[rows, :].astype(BF16), dy)
            dcim_ref[...] += _dot_tn(si_ref[rows, :].astype(BF16), dy)
            dbre_ref[...] += _dot_tn(zz, g_rb)
            dbim_ref[...] += _dot_tn(zz, g_ib)
            dzp_ref[rows, :] += _dot_nt(g_rb, bre_ref[...]) + _dot_nt(g_ib, bim_ref[...])
            return 0

        lax.fori_loop(0, T // rc, grads, 0)

    f32 = lambda *s: jax.ShapeDtypeStruct(s, F32)
    return _pallas(
        body, "s5_bwd", (S5_NCB, 2, S5_NH),
        [tok, tok, b_spec, b_spec, lam_spec, c_spec, c_spec],
        [tok, b_spec, b_spec, dlam_spec, c_spec, c_spec],
        [f32(T, SSM_WIDTH), f32(2, S5_NCB, NH, 128, NS), f32(2, S5_NCB, NH, 128, NS),
         f32(2, S5_NCB, NH, 2, NS), f32(2, S5_NCB, NH, NS, 128), f32(2, S5_NCB, NH, NS, 128)],
        [pltpu.VMEM((T, NS), F32)] * 4,
        ("parallel", "arbitrary", "arbitrary"), (zp, dyp, bre, bim, lam, cre, cimn), comm)


def _s5_delta():
    d = np.zeros((S5_NH, 8, 8 // S5_NH), np.float32)
    for h in range(S5_NH):
        for go in range(8 // S5_NH):
            d[h, h * (8 // S5_NH) + go, go] = 1.0
    return d


def _s5_pack_b(bbt):
    gh = 8 // S5_NH
    b5 = bbt.reshape(S5_NCB, S5_NH, gh, SSM_GROUP, SSM_STATE).transpose(0, 1, 3, 2, 4)
    m = b5[:, :, None] * _s5_delta()[None, :, :, None, :, None]
    return m.reshape(S5_NCB, S5_NH, 128, S5_NS)


def _s5_unpack_b(dm):
    gh = 8 // S5_NH
    d6 = dm.reshape(S5_NCB, S5_NH, 8, SSM_GROUP, gh, SSM_STATE)
    b5 = jnp.sum(d6 * _s5_delta()[None, :, :, None, :, None], axis=2)
    return b5.transpose(0, 1, 3, 2, 4).reshape(SSM_GROUPS, SSM_GROUP, SSM_STATE)


def _s5_pack_c(c):
    gh = 8 // S5_NH
    c5 = c.reshape(S5_NCB, S5_NH, gh, SSM_GROUP, SSM_STATE).transpose(0, 1, 2, 4, 3)
    m = c5[:, :, :, :, None, :] * _s5_delta().transpose(0, 2, 1)[None, :, :, None, :, None]
    return m.reshape(S5_NCB, S5_NH, S5_NS, 128)


def _s5_unpack_c(dm):
    gh = 8 // S5_NH
    d6 = dm.reshape(S5_NCB, S5_NH, gh, SSM_STATE, 8, SSM_GROUP)
    c5 = jnp.sum(d6 * _s5_delta().transpose(0, 2, 1)[None, :, :, None, :, None], axis=4)
    return c5.transpose(0, 1, 2, 4, 3).reshape(SSM_GROUPS, SSM_GROUP, SSM_STATE)


def _s5_pack_lam(x):
    return x.reshape(S5_NCB, S5_NH, S5_NS)


def _permute_rows(x):
    T = x.shape[0]
    return x.reshape(SCAN_LANES, T // SCAN_LANES, -1).transpose(1, 0, 2).reshape(T, -1)


def _unpermute_rows(x):
    T = x.shape[0]
    return x.reshape(T // SCAN_LANES, SCAN_LANES, -1).transpose(1, 0, 2).reshape(T, -1)


ATT_TB = ATT_ROWS * GRID_W
ATT_KB = 3 * ATT_TB


def _att_valid(i, n_rows):
    qi, kj = np.meshgrid(np.arange(ATT_TB), np.arange(ATT_KB), indexing="ij")
    r = i * ATT_ROWS + qi // GRID_W
    c = qi % GRID_W
    rk = (i - 1) * ATT_ROWS + kj // GRID_W
    x = kj % GRID_W
    rs = np.clip(r - WIN_H // 2, 0, n_rows - WIN_H)
    cs = np.clip(c - WIN_W // 2, 0, GRID_W - WIN_W)
    return (rk >= rs) & (rk < rs + WIN_H) & (x >= cs) & (x < cs + WIN_W)


def _att_masked_tables(table, n_rows):
    n = n_rows // ATT_ROWS
    assert n >= 3
    masks = np.stack([_att_valid(i, n_rows) for i in (0, 1, n - 1)])
    return jnp.where(masks[:, None], table[None], NEG_INF)


def _att_variant(i, n):
    return jnp.where(i == 0, 0, jnp.where(i >= n - 1, 2, 1))


def _att_probs(qh, kh, bias):
    s = _dot_nt(qh, kh) + bias
    p = jnp.exp(s - jnp.max(s, axis=1, keepdims=True))
    return p * (1.0 / jnp.sum(p, axis=1, keepdims=True))


def _att_specs(n, col):
    last = n - 1
    cur = lambda i: (jnp.minimum(i, last), col)
    prv = lambda i: (jnp.maximum(jnp.minimum(i, last) - 1, 0), col)
    nxt = lambda i: (jnp.minimum(i + 1, last), col)
    blk = lambda f: pl.BlockSpec((ATT_TB, ATT_WIDTH), f)
    return blk(cur), blk(prv), blk(nxt)


def _att_fwd(zb, biasv):
    T = zb.shape[0]
    W = ATT_WIDTH
    n = T // ATT_TB
    n_rows = T // GRID_W
    cur = _att_specs(n, 0)[0]
    q_cur = _att_specs(n, 1)[0]
    k_cur, k_prv, k_nxt = _att_specs(n, 2)
    v_cur, v_prv, v_nxt = _att_specs(n, 3)

    def body(q_ref, kp_ref, kc_ref, kn_ref, vp_ref, vc_ref, vn_ref, b_ref, y_ref):
        qs = q_ref[...] * 0.125
        kb = jnp.concatenate([kp_ref[...], kc_ref[...], kn_ref[...]], axis=0)
        vb = jnp.concatenate([vp_ref[...], vc_ref[...], vn_ref[...]], axis=0)
        outs = []
        for h in range(ATT_HEADS):
            hs = slice(h * ATT_HEAD_DIM, (h + 1) * ATT_HEAD_DIM)
            p = _att_probs(qs[:, hs], kb[:, hs], b_ref[h])
            outs.append(_dot(p.astype(BF16), vb[:, hs]))
        y_ref[...] = jnp.concatenate(outs, axis=1).astype(BF16)

    return pl.pallas_call(
        body, name="att_fwd", grid=(n,),
        in_specs=[q_cur, k_prv, k_cur, k_nxt, v_prv, v_cur, v_nxt,
                  pl.BlockSpec((None, ATT_HEADS, ATT_TB, ATT_KB), lambda i: (_att_variant(i, n), 0, 0, 0))],
        out_specs=cur,
        out_shape=jax.ShapeDtypeStruct((T, W), BF16),
        compiler_params=_cparams(("parallel",), VMEM_LIMIT),
    )(zb, zb, zb, zb, zb, zb, zb, biasv)


def _att_bwd(zb, do, biasv, comm=None):
    T = zb.shape[0]
    W = ATT_WIDTH
    n = T // ATT_TB
    n_rows = T // GRID_W
    cur = _att_specs(n, 0)[0]
    q_cur = _att_specs(n, 1)[0]
    k_cur, k_prv, k_nxt = _att_specs(n, 2)
    v_cur, v_prv, v_nxt = _att_specs(n, 3)
    done = pl.BlockSpec((ATT_TB, W), lambda i: (jnp.maximum(i - 1, 0), 0))
    bias_spec = pl.BlockSpec((None, ATT_HEADS, ATT_TB, ATT_KB), lambda i: (_att_variant(i, n), 0, 0, 0))

    def body(q_ref, do_ref, kp_ref, kc_ref, kn_ref, vp_ref, vc_ref, vn_ref, b_ref,
             dq_ref, dk_ref, dv_ref, db_ref, acck_ref, accv_ref):
        i = pl.program_id(0)

        @pl.when((i == 0) | (i == 1) | (i == n - 1))
        def _():
            db_ref[...] = jnp.zeros_like(db_ref)

        @pl.when(i == 0)
        def _():
            acck_ref[...] = jnp.zeros_like(acck_ref)
            accv_ref[...] = jnp.zeros_like(accv_ref)

        @pl.when((i > 0) & (i < n))
        def _():
            slot = lax.rem(i + 1, 3)
            acck_ref[slot] = jnp.zeros((ATT_TB, W), F32)
            accv_ref[slot] = jnp.zeros((ATT_TB, W), F32)

        @pl.when(i < n)
        def _():
            qs = q_ref[...] * 0.125
            dob = do_ref[...]
            kb = jnp.concatenate([kp_ref[...], kc_ref[...], kn_ref[...]], axis=0)
            vb = jnp.concatenate([vp_ref[...], vc_ref[...], vn_ref[...]], axis=0)
            dqs, dks, dvs = [], [], []
            for h in range(ATT_HEADS):
                hs = slice(h * ATT_HEAD_DIM, (h + 1) * ATT_HEAD_DIM)
                qh, kh, vh, doh = qs[:, hs], kb[:, hs], vb[:, hs], dob[:, hs]
                p = _att_probs(qh, kh, b_ref[h])
                dp = _dot_nt(doh, vh)
                ds = p * (dp - jnp.sum(p * dp, axis=1, keepdims=True))
                db_ref[h] += ds
                dsb = ds.astype(BF16)
                dqs.append(_dot(dsb, kh) * 0.125)
                dks.append(_dot_tn(dsb, qh))
                dvs.append(_dot_tn(p.astype(BF16), doh))
            dq_ref[...] = jnp.concatenate(dqs, axis=1).astype(BF16)
            dk_all = jnp.concatenate(dks, axis=1)
            dv_all = jnp.concatenate(dvs, axis=1)
            for b in range(3):
                slot = lax.rem(i + 2 + b, 3)
                rows = slice(b * ATT_TB, (b + 1) * ATT_TB)
                acck_ref[slot] += dk_all[rows]
                accv_ref[slot] += dv_all[rows]

        slot = lax.rem(i + 2, 3)
        dk_ref[...] = acck_ref[slot].astype(BF16)
        dv_ref[...] = accv_ref[slot].astype(BF16)

    return _pallas(
        body, "att_bwd", (n + 1,),
        [q_cur, cur, k_prv, k_cur, k_nxt, v_prv, v_cur, v_nxt, bias_spec],
        [cur, done, done, bias_spec],
        [jax.ShapeDtypeStruct((T, W), BF16)] * 3 + [jax.ShapeDtypeStruct((3, ATT_HEADS, ATT_TB, ATT_KB), F32)],
        [pltpu.VMEM((3, ATT_TB, W), F32), pltpu.VMEM((3, ATT_TB, W), F32)],
        ("arbitrary",), (zb, do, zb, zb, zb, zb, zb, zb, biasv), comm)


def _att_selectors():
    rsel = np.zeros((ATT_ROWS, 3 * ATT_ROWS, 2 * WIN_H - 1), np.float32)
    for a in range(ATT_ROWS):
        for b in range(3 * ATT_ROWS):
            rsel[a, b, b - a - ATT_ROWS + WIN_H - 1] = 1.0
    csel = np.zeros((GRID_W, GRID_W, 2 * WIN_W - 1), np.float32)
    for c in range(GRID_W):
        for x in range(GRID_W):
            csel[c, x, min(max(x - c, -(WIN_W - 1)), WIN_W - 1) + WIN_W - 1] = 1.0
    return rsel, csel


def _att_bias_table(rpb):
    rsel, csel = _att_selectors()
    hi = lax.Precision.HIGHEST
    t = jnp.einsum('hrd,abr->habd', rpb, rsel, precision=hi)
    t = jnp.einsum('habd,cxd->hacbx', t, csel, precision=hi)
    return t.reshape(ATT_HEADS, ATT_TB, ATT_KB)


def _att_bias_table_t(dtable):
    rsel, csel = _att_selectors()
    hi = lax.Precision.HIGHEST
    t = dtable.reshape(ATT_HEADS, ATT_ROWS, GRID_W, 3 * ATT_ROWS, GRID_W)
    t = jnp.einsum('hacbx,cxd->habd', t, csel, precision=hi)
    return jnp.einsum('habd,abr->hrd', t, rsel, precision=hi)


GELU_K = math.sqrt(2.0 / math.pi)
GELU_C = 0.044715
MERGE_TM = 256


def _gelu(x):
    return 0.5 * x * (1.0 + jnp.tanh(GELU_K * (x + GELU_C * x * x * x)))


def _gelu_grad(x):
    t = jnp.tanh(GELU_K * (x + GELU_C * x * x * x))
    return 0.5 * (1.0 + t) + 0.5 * x * (1.0 - t * t) * GELU_K * (1.0 + 3.0 * GELU_C * x * x)


def _merge_forward(ypre, zs, gs, ga, ya, ssm_d, w_glu, b_glu, w_bs, w_ba):
    ys = ypre + ssm_d * zs
    yg = _gelu(ys)
    sg = jax.nn.sigmoid(_dot(yg.astype(BF16), w_glu) + b_glu)
    y2 = yg * sg
    bs = _dot(y2.astype(BF16), w_bs)
    ba = _dot(ya, w_ba)
    s1 = jax.nn.sigmoid(gs)
    s2 = jax.nn.sigmoid(ga)
    merged = s1 * bs + s2 * ba
    return ys, yg, sg, y2, bs, ba, s1, s2, merged


def _merge_in_specs(D, W, tm):
    tok = lambda w, c: pl.BlockSpec((tm, w), lambda i: (i, c))
    full = lambda r, c: pl.BlockSpec((r, c), lambda i: (0, 0))
    z_specs = [tok(W, 0), tok(D, 4 * W // D), tok(D, 4 * W // D + 1)]
    w_specs = [full(1, W), full(W, W), full(1, W), full(W, D), full(W, D), full(D, D)]
    return tok, z_specs, w_specs


def _merge_fwd(ypre, z, ya, h1, ssm_d, w_glu, b_glu, w_bs, w_ba, w_out):
    T, D = h1.shape
    W = ypre.shape[1]
    tm = min(T, MERGE_TM)
    tok, z_specs, w_specs = _merge_in_specs(D, W, tm)

    def body(ypre_ref, zs_ref, gs_ref, ga_ref, ya_ref, h1_ref, d_ref, wglu_ref, bglu_ref, wbs_ref, wba_ref, wout_ref,
             h2_ref):
        merged = _merge_forward(ypre_ref[...], zs_ref[...], gs_ref[...], ga_ref[...], ya_ref[...], d_ref[...],
                                wglu_ref[...], bglu_ref[...], wbs_ref[...], wba_ref[...])[-1]
        h2_ref[...] = h1_ref[...] + _dot(merged.astype(BF16), wout_ref[...])

    return pl.pallas_call(
        body, name="merge_fwd", grid=(T // tm,),
        in_specs=[tok(W, 0)] + z_specs + [tok(W, 0), tok(D, 0)] + w_specs,
        out_specs=tok(D, 0),
        out_shape=jax.ShapeDtypeStruct((T, D), F32),
        compiler_params=_cparams(("parallel",), VMEM_LIMIT),
    )(ypre, z, z, z, ya, h1, ssm_d, w_glu, b_glu, w_bs, w_ba, w_out)


def _merge_bwd(dh2, ypre, z, ya, ssm_d, w_glu, b_glu, w_bs, w_ba, w_out):
    T, D = dh2.shape
    W = ypre.shape[1]
    tm = min(T, MERGE_TM)
    tok, z_specs, w_specs = _merge_in_specs(D, W, tm)

    def body(dh2_ref, ypre_ref, zs_ref, gs_ref, ga_ref, ya_ref, d_ref, wglu_ref, bglu_ref, wbs_ref, wba_ref, wout_ref,
             dypre_ref, dzs_ref, dgs_ref, dga_ref, dya_ref, dd_ref, dwglu_ref, dbglu_ref, dwbs_ref, dwba_ref, dwout_ref):
        @pl.when(pl.program_id(0) == 0)
        def _():
            for r in (dd_ref, dwglu_ref, dbglu_ref, dwbs_ref, dwba_ref, dwout_ref):
                r[...] = jnp.zeros_like(r)

        zs = zs_ref[...]
        ya = ya_ref[...]
        ys, yg, sg, y2, bs, ba, s1, s2, merged = _merge_forward(
            ypre_ref[...], zs, gs_ref[...], ga_ref[...], ya, d_ref[...],
            wglu_ref[...], bglu_ref[...], wbs_ref[...], wba_ref[...])
        dh2b = dh2_ref[...].astype(BF16)
        dmerged = _dot_nt(dh2b, wout_ref[...])
        dwout_ref[...] += _dot_tn(merged.astype(BF16), dh2b)
        dbs = (dmerged * s1).astype(BF16)
        dba = (dmerged * s2).astype(BF16)
        dgs_ref[...] = (dmerged * bs * s1 * (1.0 - s1)).astype(BF16)
        dga_ref[...] = (dmerged * ba * s2 * (1.0 - s2)).astype(BF16)
        dwbs_ref[...] += _dot_tn(y2.astype(BF16), dbs)
        dwba_ref[...] += _dot_tn(ya, dba)
        dya_ref[...] = _dot_nt(dba, wba_ref[...]).astype(BF16)
        dy2 = _dot_nt(dbs, wbs_ref[...])
        dvv = dy2 * yg * sg * (1.0 - sg)
        dvvb = dvv.astype(BF16)
        dyg = dy2 * sg + _dot_nt(dvvb, wglu_ref[...])
        dwglu_ref[...] += _dot_tn(yg.astype(BF16), dvvb)
        dbglu_ref[...] += _col_sum(dvv)
        dys = dyg * _gelu_grad(ys)
        dd_ref[...] += _col_sum(dys * zs)
        dzs_ref[...] = dys * d_ref[...]
        dypre_ref[...] = dys

    f32 = lambda *s: jax.ShapeDtypeStruct(s, F32)
    b16 = lambda *s: jax.ShapeDtypeStruct(s, BF16)
    return pl.pallas_call(
        body, name="merge_bwd", grid=(T // tm,),
        in_specs=[tok(D, 0), tok(W, 0)] + z_specs + [tok(W, 0)] + w_specs,
        out_specs=[tok(W, 0), tok(W, 0), tok(D, 0), tok(D, 0), tok(W, 0)] + w_specs,
        out_shape=[f32(T, W), f32(T, W), b16(T, D), b16(T, D), b16(T, W),
                   f32(1, W), f32(W, W), f32(1, W), f32(W, D), f32(W, D), f32(D, D)],
        compiler_params=_cparams(("arbitrary",), VMEM_LIMIT),
    )(dh2, ypre, z, z, z, ya, ssm_d, w_glu, b_glu, w_bs, w_ba, w_out)


def _cast_shards(weights):
    def body(*refs):
        n = len(refs) // 2
        for src, dst in zip(refs[:n], refs[n:]):
            dst[...] = src[0].astype(BF16)

    return pl.pallas_call(
        body, name="cast_shards",
        out_shape=[jax.ShapeDtypeStruct(w.shape[1:], BF16) for w in weights],
        compiler_params=_cparams(None, VMEM_LIMIT))(*weights)


def _gather_two_level(shards, name):
    n = len(shards)

    def body(*refs):
        x_refs, out_refs = refs[:n], refs[n:2 * n]
        send_sems, recv_sems, local_sems = refs[2 * n:]
        x, y, c = _my_place()
        me, sibling = (x, y, c), (x, y, 1 - c)
        chips = [(1 - x, y), (x, 1 - y), (1 - x, 1 - y)]

        def copy(a, k, block, to, own=False):
            slot = out_refs[a].at[_flat(*block)]
            return pltpu.make_async_remote_copy(
                src_ref=x_refs[a] if own else slot, dst_ref=slot,
                send_sem=send_sems.at[7 * a + k], recv_sem=recv_sems.at[7 * a + k],
                device_id=to, device_id_type=MESH_ID)

        sent, local = [], []
        for a in range(n):
            local.append(pltpu.make_async_copy(x_refs[a], out_refs[a].at[_flat(*me)], local_sems.at[a]))
            local[-1].start()
            sent.append(copy(a, 0, me, sibling, own=True))
            sent += [copy(a, 1 + j, me, (*chip, c), own=True) for j, chip in enumerate(chips)]
        for cp in sent:
            cp.start()
        for a in range(n):
            for j, chip in enumerate(chips):
                copy(a, 1 + j, (*chip, c), me).wait_recv()
                sent.append(copy(a, 4 + j, (*chip, c), sibling))
                sent[-1].start()
        for a in range(n):
            copy(a, 0, sibling, me).wait_recv()
            for j, chip in enumerate(chips):
                copy(a, 4 + j, (*chip, 1 - c), me).wait_recv()
        for cp in sent:
            cp.wait_send()
        for cp in local:
            cp.wait()

    return pl.pallas_call(
        body, name=name, in_specs=[_HBM] * n, out_specs=[_HBM] * n,
        out_shape=[jax.ShapeDtypeStruct((N_DEV,) + s.shape, s.dtype) for s in shards],
        scratch_shapes=[pltpu.SemaphoreType.DMA((7 * n,)), pltpu.SemaphoreType.DMA((7 * n,)),
                        pltpu.SemaphoreType.DMA((n,))],
    )(*shards)


PACK_COLS = 1024
BIG = (("ffn1_w_gate", 1), ("ffn1_w_up", 1), ("ffn1_w_down", 0), ("w_in", 1), ("ssm_w_glu", 0),
       ("w_branch_ssm", 1), ("w_branch_att", 1), ("w_out", 0),
       ("ffn2_w_gate", 1), ("ffn2_w_up", 1), ("ffn2_w_down", 0))
BIG_AXIS = dict(BIG)
TRANSPOSED = ("ffn1_w_gate", "ffn1_w_up", "ffn2_w_gate", "ffn2_w_up")
SSM_DIR = ("ssm_a_re", "ssm_a_im", "ssm_log_dt", "ssm_b_re", "ssm_b_im", "ssm_c_re", "ssm_c_im")
SMALL_EARLY = (("mix_norm",) + tuple(n + "_fwd" for n in SSM_DIR) + tuple(n + "_bwd" for n in SSM_DIR)
               + ("ssm_d", "ssm_b_glu", "att_rpb", "ffn2_norm", "final_norm"))
SMALL_LATE = ("ffn1_norm",)
WEIGHTS = ("ffn1_norm", "ffn1_w_gate", "ffn1_w_up", "ffn1_w_down", "mix_norm", "w_in") \
    + tuple(n + "_fwd" for n in SSM_DIR) + tuple(n + "_bwd" for n in SSM_DIR) \
    + ("ssm_d", "ssm_w_glu", "ssm_b_glu", "att_rpb", "w_branch_ssm", "w_branch_att", "w_out",
       "ffn2_norm", "ffn2_w_gate", "ffn2_w_up", "ffn2_w_down", "final_norm")


def _pad_rows(a, mult):
    pad = (-a.shape[-2]) % mult
    if pad:
        a = jnp.concatenate([a, jnp.zeros(a.shape[:-2] + (pad, a.shape[-1]), a.dtype)], axis=-2)
    return a


def _pack(arrays, row_mult):
    flat = jnp.concatenate([a.reshape(-1) for a in arrays])
    pad = (-flat.shape[0]) % PACK_COLS
    if pad:
        flat = jnp.concatenate([flat, jnp.zeros((pad,), flat.dtype)])
    return _pad_rows(flat.reshape(-1, PACK_COLS), row_mult)


def _unpack(slab, shapes):
    flat = slab.reshape(-1)
    out, at = [], 0
    for s in shapes:
        n = int(np.prod(s))
        out.append(flat[at:at + n].reshape(s))
        at += n
    return out


def _split_for_devices(g, axis):
    r, c = g.shape
    if axis == 1:
        return g.reshape(r, N_DEV, c // N_DEV).transpose(1, 0, 2).astype(BF16)
    return g.reshape(N_DEV, r // N_DEV, c).astype(BF16)


def _join_shards(gathered, axis):
    _, r, c = gathered.shape
    if axis == 1:
        return gathered.transpose(1, 0, 2).reshape(r, N_DEV * c)
    return gathered.reshape(N_DEV * r, c)


def _s5_direction_inputs(p, sfx, chain_len):
    bt_re = p["ssm_b_re" + sfx][0].transpose(0, 2, 1)
    bt_im = p["ssm_b_im" + sfx][0].transpose(0, 2, 1)
    raw = (p["ssm_a_re" + sfx][0], p["ssm_a_im" + sfx][0], p["ssm_log_dt" + sfx][0][:, None], bt_re, bt_im)
    lr, li, sr, si, bbr, bbi = _disc_fwd(*raw, chain_len,"s5_disc" + sfx)
    lam = jnp.stack([_s5_pack_lam(t) for t in (lr, li, sr, si)], axis=2)
    mats = (_s5_pack_b(bbr), _s5_pack_b(bbi), lam,
            _s5_pack_c(p["ssm_c_re" + sfx][0]), _s5_pack_c(-p["ssm_c_im" + sfx][0]))
    return raw, mats


def kernel(x, ffn1_norm, ffn1_w_gate, ffn1_w_up, ffn1_w_down, mix_norm, w_in, ssm_a_re_fwd, ssm_a_im_fwd, ssm_log_dt_fwd, ssm_b_re_fwd, ssm_b_im_fwd, ssm_c_re_fwd, ssm_c_im_fwd, ssm_a_re_bwd, ssm_a_im_bwd, ssm_log_dt_bwd, ssm_b_re_bwd, ssm_b_im_bwd, ssm_c_re_bwd, ssm_c_im_bwd, ssm_d, ssm_w_glu, ssm_b_glu, att_rpb, w_branch_ssm, w_branch_att, w_out, ffn2_norm, ffn2_w_gate, ffn2_w_up, ffn2_w_down, final_norm, loss_target, m_ffn1_norm, m_ffn1_w_gate, m_ffn1_w_up, m_ffn1_w_down, m_mix_norm, m_w_in, m_ssm_a_re_fwd, m_ssm_a_im_fwd, m_ssm_log_dt_fwd, m_ssm_b_re_fwd, m_ssm_b_im_fwd, m_ssm_c_re_fwd, m_ssm_c_im_fwd, m_ssm_a_re_bwd, m_ssm_a_im_bwd, m_ssm_log_dt_bwd, m_ssm_b_re_bwd, m_ssm_b_im_bwd, m_ssm_c_re_bwd, m_ssm_c_im_bwd, m_ssm_d, m_ssm_w_glu, m_ssm_b_glu, m_att_rpb, m_w_branch_ssm, m_w_branch_att, m_w_out, m_ffn2_norm, m_ffn2_w_gate, m_ffn2_w_up, m_ffn2_w_down, m_final_norm, v_ffn1_norm, v_ffn1_w_gate, v_ffn1_w_up, v_ffn1_w_down, v_mix_norm, v_w_in, v_ssm_a_re_fwd, v_ssm_a_im_fwd, v_ssm_log_dt_fwd, v_ssm_b_re_fwd, v_ssm_b_im_fwd, v_ssm_c_re_fwd, v_ssm_c_im_fwd, v_ssm_a_re_bwd, v_ssm_a_im_bwd, v_ssm_log_dt_bwd, v_ssm_b_re_bwd, v_ssm_b_im_bwd, v_ssm_c_re_bwd, v_ssm_c_im_bwd, v_ssm_d, v_ssm_w_glu, v_ssm_b_glu, v_att_rpb, v_w_branch_ssm, v_w_branch_att, v_w_out, v_ffn2_norm, v_ffn2_w_gate, v_ffn2_w_up, v_ffn2_w_down, v_final_norm):
    p = dict(locals())
    x = p["x"][0]
    target = p["loss_target"][0]
    T, D = x.shape

    stored = lambda a, n: jnp.swapaxes(a, -1, -2) if n in TRANSPOSED else a
    cut_axis = lambda n: 0 if n in TRANSPOSED else BIG_AXIS[n]
    shard = dict(zip([n for n, _ in BIG], _cast_shards([stored(p[n], n) for n, _ in BIG])))
    ffn1_w = ("ffn1_w_gate", "ffn1_w_up", "ffn1_w_down")
    mix_w = ("w_in", "ssm_w_glu", "w_branch_ssm", "w_branch_att", "w_out")
    ffn2_w = ("ffn2_w_gate", "ffn2_w_up", "ffn2_w_down")
    gathered = dict(zip(ffn1_w, _gather_two_level([shard[n] for n in ffn1_w], "gather_ffn1")))
    full = lambda n: _join_shards(gathered[n], cut_axis(n))

    h0 = x
    wg1, wu1, wd1 = [full(n) for n in ffn1_w]
    (h1, xn1, g1, u1), got = _ffn_fwd(h0, p["ffn1_norm"], wg1, wu1, wd1, "ffn1_fwd",
                                      _Comm("gather", [shard[n] for n in mix_w]))
    gathered.update(zip(mix_w, got))
    z, zb, un = _mixin_fwd(h1, p["mix_norm"], gathered["w_in"])
    W = SSM_WIDTH
    zp = _permute_rows(zb[:, :W])
    chain_len = T // SCAN_LANES // S5_NQ
    raw_f, mats_f = _s5_direction_inputs(p, "_fwd", chain_len)
    raw_b, mats_b = _s5_direction_inputs(p, "_bwd", chain_len)
    bre, bim, lam, cre, cimn = [jnp.stack([f, b]) for f, b in zip(mats_f, mats_b)]
    bre, bim, cre, cimn = [t.astype(BF16) for t in (bre, bim, cre, cimn)]
    (yp,), got = _s5_fwd(zp, bre, bim, lam, cre, cimn, _Comm("gather", [shard[n] for n in ffn2_w]))
    gathered.update(zip(ffn2_w, got))
    ypre = _unpermute_rows(yp)
    table = _att_masked_tables(_att_bias_table(p["att_rpb"][0]), T // GRID_W)
    ya = _att_fwd(zb, table)
    tail_w = (p["ssm_d"], full("ssm_w_glu"), p["ssm_b_glu"], full("w_branch_ssm"), full("w_branch_att"), full("w_out"))
    h2 = _merge_fwd(ypre, z, ya, h1, *tail_w)
    wg2, wu2, wd2 = [full(n) for n in ffn2_w]
    (h3, xn2, g2, u2), _ = _ffn_fwd(h2, p["ffn2_norm"], wg2, wu2, wd2, "ffn2_fwd")
    loss_part, dh3, d_final = _loss_head(h3, p["final_norm"][None], target)

    grads = {"final_norm": d_final[0]}
    to_send = lambda names: _Comm("exchange", [_split_for_devices(grads[n], cut_axis(n)) for n in names])
    parts = {}
    (dh2, grads["ffn2_norm"], do2, a2, dg2, du2), _ = _ffn_bwd(
        dh3, h2, p["ffn2_norm"], g2, u2, wg2, wu2, wd2, "ffn2_bwd")
    grads["ffn2_w_gate"] = _xty(dg2, xn2, "ffn2_dw_gate")
    grads["ffn2_w_up"] = _xty(du2, xn2, "ffn2_dw_up")
    grads["ffn2_w_down"] = _xty(a2, do2, "ffn2_dw_down")
    (dypre, dzs_skip, dgs, dga, dya, grads["ssm_d"], grads["ssm_w_glu"], grads["ssm_b_glu"],
     grads["w_branch_ssm"], grads["w_branch_att"], grads["w_out"]) = _merge_bwd(dh2, ypre, z, ya, *tail_w)
    (dq, dk, dv, dtable), got = _att_bwd(zb, dya, table, to_send(ffn2_w))
    parts.update(zip(ffn2_w, got))
    grads["att_rpb"] = _att_bias_table_t(jnp.sum(dtable, axis=0))
    dyp = _permute_rows(dypre).astype(BF16)
    tail_names = ("ssm_w_glu", "w_branch_ssm", "w_branch_att", "w_out")
    (dzp, dbre, dbim, dlam, dcre, dcimn), got = _s5_bwd(zp, dyp, bre, bim, lam, cre, cimn, to_send(tail_names))
    parts.update(zip(tail_names, got))
    G, P = SSM_GROUPS, SSM_STATE
    for d, (sfx, raw) in enumerate((("_fwd", raw_f), ("_bwd", raw_b))):
        da_re, da_im, dldt, dbt_re, dbt_im = _disc_bwd(
            *raw, dlam[d, :, :, 0, :].reshape(G, P), dlam[d, :, :, 1, :].reshape(G, P),
            _s5_unpack_b(dbre[d]), _s5_unpack_b(dbim[d]), "s5_disc_grad" + sfx)
        grads["ssm_a_re" + sfx] = da_re
        grads["ssm_a_im" + sfx] = da_im
        grads["ssm_log_dt" + sfx] = dldt[:, 0]
        grads["ssm_b_re" + sfx] = dbt_re.transpose(0, 2, 1)
        grads["ssm_b_im" + sfx] = dbt_im.transpose(0, 2, 1)
        grads["ssm_c_re" + sfx] = _s5_unpack_c(dcre[d])
        grads["ssm_c_im" + sfx] = -_s5_unpack_c(dcimn[d])
    dzs = _unpermute_rows(dzp) + dzs_skip
    dz = jnp.concatenate([dzs.astype(BF16), dq, dk, dv, dgs, dga], axis=1)
    dh1, grads["mix_norm"] = _mixin_bwd(dz, dh2, h1, p["mix_norm"], gathered["w_in"])
    grads["w_in"] = _xty(un, dz, "dw_in", col_shards=N_DEV)
    pack_small = lambda names, src, pre: _pack([src[pre + n].astype(F32) for n in names], 8)
    early = _Comm(["exchange", "gather"],
                  [grads["w_in"], pack_small(SMALL_EARLY, grads, "")])
    (dh0, grads["ffn1_norm"], do1, a1, dg1, du1), (parts["w_in"], got_early) = _ffn_bwd(
        dh1, h0, p["ffn1_norm"], g1, u1, wg1, wu1, wd1, "ffn1_bwd", early)
    grads["ffn1_w_down"] = _xty(a1, do1, "ffn1_dw_down")
    grads["ffn1_w_gate"], (parts["ffn1_w_down"],) = _xty(dg1, xn1, "ffn1_dw_gate", to_send(("ffn1_w_down",)))
    grads["ffn1_w_up"], (parts["ffn1_w_gate"],) = _xty(du1, xn1, "ffn1_dw_up", to_send(("ffn1_w_gate",)))
    last = _Comm(["exchange", "gather"],
                 [_split_for_devices(grads["ffn1_w_up"], 0), pack_small(SMALL_LATE, grads, "")])
    parts["ffn1_w_up"], got_late = _comm_call(last, "exchange_last")
    got_small = jnp.concatenate([got_early, got_late], axis=1)

    results = {}
    for n, _ in BIG:
        outs = _adamw(parts[n], *[stored(p[pre + n][0], n) for pre in ("", "m_", "v_")], "adamw_" + n)
        results[n] = [stored(o, n)[None] for o in outs]
    early_rows = got_early.shape[1]
    slab = lambda pre: jnp.concatenate([pack_small(SMALL_EARLY, p, pre), pack_small(SMALL_LATE, p, pre)], axis=0)
    small_out = _adamw(got_small, slab(""), slab("m_"), slab("v_"), "adamw_small")
    for names, rows in ((SMALL_EARLY, slice(0, early_rows)), (SMALL_LATE, slice(early_rows, None))):
        shapes = [p[n].shape for n in names]
        for n, vals in zip(names, zip(*[_unpack(out[rows], shapes) for out in small_out])):
            results[n] = list(vals)

    loss = lax.psum(loss_part[0, 0], ("x", "y", "c"))
    out = [loss, dh0[None]]
    for kind in range(4):
        out += [results[n][kind] for n in WEIGHTS]
    return tuple(out)
```

```python
import functools
import math

import numpy as np
import jax
import jax.numpy as jnp
from jax import lax
from jax.experimental import pallas as pl
from jax.experimental.pallas import tpu as pltpu

F32 = jnp.float32
BF16 = jnp.bfloat16
MESH_ID = pl.DeviceIdType.MESH

SSM_GROUP = 16
SSM_GROUPS = 32
SSM_STATE = 64
SSM_WIDTH = 512
ATT_HEADS = 8
ATT_HEAD_DIM = 64
ATT_WIDTH = 512
GRID_W = 64
WIN_H = 8
WIN_W = 16
EPS = 1e-6
NEG_INF = -1e30
ADAM_LR = 0.001
ADAM_B1 = 0.9
ADAM_B2 = 0.999
ADAM_EPS = 1e-08
ADAM_WD = 0.01
ADAM_STEP = 10

N_DEV = 8
V7X_VMEM_BYTES = 64 * 1024 * 1024
VMEM_LIMIT = V7X_VMEM_BYTES - 8 * 1024 * 1024
SCAN_LANES = 8
ATT_ROWS = 4


def _cparams(sem, vmem=None):
    return pltpu.CompilerParams(dimension_semantics=sem, vmem_limit_bytes=vmem)


def _dot(a, b):
    return jnp.dot(a, b, preferred_element_type=F32)


def _dot_nt(a, b):
    return lax.dot_general(a, b, (((1,), (1,)), ((), ())), preferred_element_type=F32)


def _dot_tn(a, b):
    return lax.dot_general(a, b, (((0,), (0,)), ((), ())), preferred_element_type=F32)


def _rms(h):
    return lax.rsqrt(jnp.mean(h * h, axis=-1, keepdims=True) + EPS)


def _rms_bwd(h, r, v):
    return r * v - h * (r * r * r) * jnp.mean(h * v, axis=-1, keepdims=True)


def _col_sum(x):
    return jnp.sum(x, axis=0, keepdims=True)


def _my_place():
    return lax.axis_index("x"), lax.axis_index("y"), lax.axis_index("c")


def _flat(px, py, pc):
    return 4 * px + 2 * py + pc


class _Comm:
    def __init__(self, kind, arrays):
        self.arrays = list(arrays)
        self.n = len(self.arrays)
        self.kinds = [kind] * self.n if isinstance(kind, str) else list(kind)

    def out_shapes(self):
        return [jax.ShapeDtypeStruct((N_DEV,) + a.shape if k == "gather" else a.shape, a.dtype)
                for k, a in zip(self.kinds, self.arrays)]

    def scratch(self):
        return [pltpu.SemaphoreType.DMA((7 * self.n,)), pltpu.SemaphoreType.DMA((7 * self.n,)),
                pltpu.SemaphoreType.DMA((self.n,))]

    def run(self, srcs, dsts, sems, start):
        send_sems, recv_sems, local_sems = sems
        x, y, c = _my_place()
        mine = _flat(x, y, c)
        for a, (src, dst) in enumerate(zip(srcs, dsts)):
            whole = self.kinds[a] == "gather"
            local = pltpu.make_async_copy(src if whole else src.at[mine], dst.at[mine], local_sems.at[a])
            local.start() if start else local.wait()
            for k in range(1, N_DEV):
                px = 1 - x if k & 4 else x
                py = 1 - y if k & 2 else y
                pc = 1 - c if k & 1 else c
                cp = pltpu.make_async_remote_copy(
                    src_ref=src if whole else src.at[_flat(px, py, pc)], dst_ref=dst.at[mine],
                    send_sem=send_sems.at[7 * a + k - 1], recv_sem=recv_sems.at[7 * a + k - 1],
                    device_id=(px, py, pc), device_id_type=MESH_ID)
                cp.start() if start else cp.wait()


_HBM = pl.BlockSpec(memory_space=pltpu.HBM)


def _comm_call(comm, name):
    def body(*refs):
        srcs, dsts, sems = refs[:comm.n], refs[comm.n:2 * comm.n], refs[2 * comm.n:]
        comm.run(srcs, dsts, sems, True)
        comm.run(srcs, dsts, sems, False)

    return pl.pallas_call(body, name=name, in_specs=[_HBM] * comm.n, out_specs=[_HBM] * comm.n,
                          out_shape=comm.out_shapes(), scratch_shapes=comm.scratch())(*comm.arrays)


def _pallas(core, name, grid, in_specs, out_specs, out_shape, scratch, sem, args, comm=None):
    if comm is None:
        out = pl.pallas_call(core, name=name, grid=grid, in_specs=in_specs, out_specs=out_specs,
                             out_shape=out_shape, scratch_shapes=scratch,
                             compiler_params=_cparams(sem, VMEM_LIMIT))(*args)
        return out, []
    n_in, n_out, n_scr, n = len(in_specs), len(out_specs), len(scratch), comm.n

    def body(*refs):
        ins, srcs = refs[:n_in], refs[n_in:n_in + n]
        outs, dsts = refs[n_in + n:n_in + n + n_out], refs[n_in + n + n_out:n_in + 2 * n + n_out]
        scr, sems = refs[n_in + 2 * n + n_out:n_in + 2 * n + n_out + n_scr], refs[n_in + 2 * n + n_out + n_scr:]
        ids = [pl.program_id(k) for k in range(len(grid))]
        first = functools.reduce(lambda a, b: a & b, [i == 0 for i in ids])
        last = functools.reduce(lambda a, b: a & b, [i == g - 1 for i, g in zip(ids, grid)])

        @pl.when(first)
        def _():
            comm.run(srcs, dsts, sems, True)

        core(*ins, *outs, *scr)

        @pl.when(last)
        def _():
            comm.run(srcs, dsts, sems, False)

    out = pl.pallas_call(
        body, name=name, grid=grid, in_specs=list(in_specs) + [_HBM] * n, out_specs=list(out_specs) + [_HBM] * n,
        out_shape=list(out_shape) + comm.out_shapes(), scratch_shapes=list(scratch) + comm.scratch(),
        compiler_params=_cparams(("arbitrary",) * len(grid), VMEM_LIMIT))(*args, *comm.arrays)
    return out[:n_out], out[n_out:]


def _ffn_tiles(T, F, wide):
    if F % 1408 == 0:
        return min(T, 512 if wide else 256), 1408
    return min(T, 1024), 256 if F % 256 == 0 else F


def _ffn_fwd(h, gain, wg, wu, wd, name, comm=None):
    T, D = h.shape
    F = wg.shape[0]
    tm, tf = _ffn_tiles(T, F, wide=True)
    nj = F // tf

    def body(h_ref, gain_ref, wg_ref, wu_ref, wd_ref, ho_ref, xn_ref, g_ref, u_ref, acc_ref):
        j = pl.program_id(1)

        @pl.when(j == 0)
        def _():
            hh = h_ref[...]
            xn_ref[...] = (hh * _rms(hh) * gain_ref[...]).astype(BF16)
            acc_ref[...] = jnp.zeros_like(acc_ref)

        xn = xn_ref[...]
        g = _dot_nt(xn, wg_ref[...])
        u = _dot_nt(xn, wu_ref[...])
        g_ref[...] = g.astype(BF16)
        u_ref[...] = u.astype(BF16)
        a = (g * jax.nn.sigmoid(g) * u).astype(BF16)
        acc_ref[...] += _dot(a, wd_ref[...])

        @pl.when(j == nj - 1)
        def _():
            ho_ref[...] = h_ref[...] + 0.5 * acc_ref[...]

    return _pallas(
        body, name, (T // tm, nj),
        [pl.BlockSpec((tm, D), lambda i, j: (i, 0)),
         pl.BlockSpec((1, D), lambda i, j: (0, 0)),
         pl.BlockSpec((tf, D), lambda i, j: (j, 0)),
         pl.BlockSpec((tf, D), lambda i, j: (j, 0)),
         pl.BlockSpec((tf, D), lambda i, j: (j, 0))],
        [pl.BlockSpec((tm, D), lambda i, j: (i, 0)),
         pl.BlockSpec((tm, D), lambda i, j: (i, 0)),
         pl.BlockSpec((tm, tf), lambda i, j: (i, j)),
         pl.BlockSpec((tm, tf), lambda i, j: (i, j))],
        [jax.ShapeDtypeStruct((T, D), F32), jax.ShapeDtypeStruct((T, D), BF16),
         jax.ShapeDtypeStruct((T, F), BF16), jax.ShapeDtypeStruct((T, F), BF16)],
        [pltpu.VMEM((tm, D), F32)], ("parallel", "arbitrary"), (h, gain, wg, wu, wd), comm)


def _ffn_bwd(dho, h, gain, g, u, wg, wu, wd, name, comm=None):
    T, D = h.shape
    F = wg.shape[0]
    tm, tf = _ffn_tiles(T, F, wide=False)
    nj = F // tf

    def body(dho_ref, h_ref, gain_ref, g_ref, u_ref, wg_ref, wu_ref, wd_ref,
             dh_ref, dgain_ref, do_ref, a_ref, dg_ref, du_ref, acc_ref):
        i = pl.program_id(0)
        j = pl.program_id(1)

        @pl.when(j == 0)
        def _():
            do_ref[...] = (0.5 * dho_ref[...]).astype(BF16)
            acc_ref[...] = jnp.zeros_like(acc_ref)

        @pl.when((i == 0) & (j == 0))
        def _():
            dgain_ref[...] = jnp.zeros_like(dgain_ref)

        da = _dot_nt(do_ref[...], wd_ref[...])
        gg = g_ref[...].astype(F32)
        uu = u_ref[...].astype(F32)
        s = jax.nn.sigmoid(gg)
        sl = gg * s
        a_ref[...] = (sl * uu).astype(BF16)
        dg = (da * uu * (s * (1.0 + gg * (1.0 - s)))).astype(BF16)
        du = (da * sl).astype(BF16)
        dg_ref[...] = dg
        du_ref[...] = du
        acc_ref[...] += _dot(dg, wg_ref[...]) + _dot(du, wu_ref[...])

        @pl.when(j == nj - 1)
        def _():
            hh = h_ref[...]
            r = _rms(hh)
            dxn = acc_ref[...]
            dgain_ref[...] += _col_sum(dxn * hh * r)
            dh_ref[...] = dho_ref[...] + _rms_bwd(hh, r, dxn * gain_ref[...])

    return _pallas(
        body, name, (T // tm, nj),
        [pl.BlockSpec((tm, D), lambda i, j: (i, 0)),
         pl.BlockSpec((tm, D), lambda i, j: (i, 0)),
         pl.BlockSpec((1, D), lambda i, j: (0, 0)),
         pl.BlockSpec((tm, tf), lambda i, j: (i, j)),
         pl.BlockSpec((tm, tf), lambda i, j: (i, j)),
         pl.BlockSpec((tf, D), lambda i, j: (j, 0)),
         pl.BlockSpec((tf, D), lambda i, j: (j, 0)),
         pl.BlockSpec((tf, D), lambda i, j: (j, 0))],
        [pl.BlockSpec((tm, D), lambda i, j: (i, 0)),
         pl.BlockSpec((1, D), lambda i, j: (0, 0)),
         pl.BlockSpec((tm, D), lambda i, j: (i, 0)),
         pl.BlockSpec((tm, tf), lambda i, j: (i, j)),
         pl.BlockSpec((tm, tf), lambda i, j: (i, j)),
         pl.BlockSpec((tm, tf), lambda i, j: (i, j))],
        [jax.ShapeDtypeStruct((T, D), F32), jax.ShapeDtypeStruct((1, D), F32),
         jax.ShapeDtypeStruct((T, D), BF16), jax.ShapeDtypeStruct((T, F), BF16),
         jax.ShapeDtypeStruct((T, F), BF16), jax.ShapeDtypeStruct((T, F), BF16)],
        [pltpu.VMEM((tm, D), F32)], ("arbitrary", "arbitrary"), (dho, h, gain, g, u, wg, wu, wd), comm)


def _xty(x, y, name, comm=None, col_shards=1):
    T, K = x.shape
    N = y.shape[1]
    tt = min(T, 1024)
    tk = K if K <= 1024 else (1408 if K % 1408 == 0 else K)
    tn = N if N <= 1024 else (1408 if N % 1408 == 0 else (1024 if N % 1024 == 0 else N))
    nt = T // tt
    ws = N // col_shards
    per = tn // ws if col_shards > 1 else 1
    assert col_shards == 1 or (tn % ws == 0 and ws % 128 == 0)

    def body(x_ref, y_ref, o_ref, acc_ref):
        t = pl.program_id(2)

        @pl.when(t == 0)
        def _():
            acc_ref[...] = jnp.zeros_like(acc_ref)

        acc_ref[...] += _dot_tn(x_ref[...], y_ref[...])

        @pl.when(t == nt - 1)
        def _():
            if col_shards == 1:
                o_ref[...] = acc_ref[...].astype(BF16)
            else:
                for s in range(per):
                    o_ref[s] = acc_ref[:, s * ws:(s + 1) * ws].astype(BF16)

    if col_shards == 1:
        out_spec = pl.BlockSpec((tk, tn), lambda k, n, t: (k, n))
        out_shape = jax.ShapeDtypeStruct((K, N), BF16)
    else:
        out_spec = pl.BlockSpec((per, tk, ws), lambda k, n, t: (n, k, 0))
        out_shape = jax.ShapeDtypeStruct((col_shards, K, ws), BF16)
    (out,), got = _pallas(
        body, name, (K // tk, N // tn, nt),
        [pl.BlockSpec((tt, tk), lambda k, n, t: (t, k)), pl.BlockSpec((tt, tn), lambda k, n, t: (t, n))],
        [out_spec], [out_shape], [pltpu.VMEM((tk, tn), F32)],
        ("parallel", "parallel", "arbitrary"), (x, y), comm)
    return out if comm is None else (out, got)


def _mixin_fwd(h, gain, w_in):
    T, D = h.shape
    nn, _, tn = w_in.shape
    N = nn * tn
    tm = min(T, 1024)

    def body(h_ref, gain_ref, w_ref, z_ref, zb_ref, un_ref):
        @pl.when(pl.program_id(1) == 0)
        def _():
            hh = h_ref[...]
            un_ref[...] = (hh * _rms(hh) * gain_ref[...]).astype(BF16)

        z = _dot(un_ref[...], w_ref[...])
        z_ref[...] = z
        zb_ref[...] = z.astype(BF16)

    return pl.pallas_call(
        body, name="mixin_fwd", grid=(T // tm, nn),
        in_specs=[pl.BlockSpec((tm, D), lambda i, n: (i, 0)),
                  pl.BlockSpec((1, D), lambda i, n: (0, 0)),
                  pl.BlockSpec((None, D, tn), lambda i, n: (n, 0, 0))],
        out_specs=[pl.BlockSpec((tm, tn), lambda i, n: (i, n)),
                   pl.BlockSpec((tm, tn), lambda i, n: (i, n)),
                   pl.BlockSpec((tm, D), lambda i, n: (i, 0))],
        out_shape=[jax.ShapeDtypeStruct((T, N), F32), jax.ShapeDtypeStruct((T, N), BF16),
                   jax.ShapeDtypeStruct((T, D), BF16)],
        compiler_params=_cparams(("parallel", "arbitrary"), VMEM_LIMIT),
    )(h, gain, w_in)


def _mixin_bwd(dz, dh_res, h, gain, w_in):
    T, D = h.shape
    nn, _, tn = w_in.shape
    tm = min(T, 256)

    def body(dz_ref, dres_ref, h_ref, gain_ref, w_ref, dh_ref, dgain_ref):
        @pl.when(pl.program_id(0) == 0)
        def _():
            dgain_ref[...] = jnp.zeros_like(dgain_ref)

        dun = _dot_nt(dz_ref[:, 0:tn], w_ref[0])
        for s in range(1, nn):
            dun = dun + _dot_nt(dz_ref[:, s * tn:(s + 1) * tn], w_ref[s])
        hh = h_ref[...]
        r = _rms(hh)
        dgain_ref[...] += _col_sum(dun * hh * r)
        dh_ref[...] = dres_ref[...] + _rms_bwd(hh, r, dun * gain_ref[...])

    return pl.pallas_call(
        body, name="mixin_bwd", grid=(T // tm,),
        in_specs=[pl.BlockSpec((tm, nn * tn), lambda i: (i, 0)),
                  pl.BlockSpec((tm, D), lambda i: (i, 0)),
                  pl.BlockSpec((tm, D), lambda i: (i, 0)),
                  pl.BlockSpec((1, D), lambda i: (0, 0)),
                  pl.BlockSpec((nn, D, tn), lambda i: (0, 0, 0))],
        out_specs=[pl.BlockSpec((tm, D), lambda i: (i, 0)),
                   pl.BlockSpec((1, D), lambda i: (0, 0))],
        out_shape=[jax.ShapeDtypeStruct((T, D), F32), jax.ShapeDtypeStruct((1, D), F32)],
        compiler_params=_cparams(("arbitrary",), VMEM_LIMIT),
    )(dz, dh_res, h, gain, w_in)


def _loss_head(h, gain, target):
    T, D = h.shape
    tm = min(T, 1024)

    def body(h_ref, gain_ref, t_ref, loss_ref, dh_ref, dgain_ref):
        @pl.when(pl.program_id(0) == 0)
        def _():
            loss_ref[...] = jnp.zeros_like(loss_ref)
            dgain_ref[...] = jnp.zeros_like(dgain_ref)

        hh = h_ref[...]
        r = _rms(hh)
        e = hh * r * gain_ref[...] - t_ref[...]
        loss_ref[...] += (0.5 / D) * jnp.sum(e * e)
        dy = e * (1.0 / D)
        dgain_ref[...] += _col_sum(dy * hh * r)
        dh_ref[...] = _rms_bwd(hh, r, dy * gain_ref[...])

    return pl.pallas_call(
        body, name="loss_head", grid=(T // tm,),
        in_specs=[pl.BlockSpec((tm, D), lambda i: (i, 0)),
                  pl.BlockSpec((1, D), lambda i: (0, 0)),
                  pl.BlockSpec((tm, D), lambda i: (i, 0))],
        out_specs=[pl.BlockSpec((1, 128), lambda i: (0, 0)),
                   pl.BlockSpec((tm, D), lambda i: (i, 0)),
                   pl.BlockSpec((1, D), lambda i: (0, 0))],
        out_shape=[jax.ShapeDtypeStruct((1, 128), F32), jax.ShapeDtypeStruct((T, D), F32),
                   jax.ShapeDtypeStruct((1, D), F32)],
        compiler_params=_cparams(("arbitrary",), VMEM_LIMIT),
    )(h, gain, target)


def _adamw(parts, w, m, v, name):
    R, C = w.shape
    mult = 16 if parts.dtype == BF16 else 8
    tr = max(t for t in range(mult, min(R, 512) + 1, mult) if R % t == 0)
    c1 = 1.0 - ADAM_B1 ** ADAM_STEP
    c2 = 1.0 - ADAM_B2 ** ADAM_STEP

    def body(p_ref, w_ref, m_ref, v_ref, g_ref, d_ref, nm_ref, nv_ref):
        g = p_ref[0].astype(F32)
        for k in range(1, N_DEV):
            g = g + p_ref[k].astype(F32)
        mm = ADAM_B1 * m_ref[...] + (1.0 - ADAM_B1) * g
        vv = ADAM_B2 * v_ref[...] + (1.0 - ADAM_B2) * (g * g)
        g_ref[...] = g
        nm_ref[...] = mm
        nv_ref[...] = vv
        d_ref[...] = -ADAM_LR * ((mm / c1) / (jnp.sqrt(vv / c2) + ADAM_EPS) + ADAM_WD * w_ref[...])

    spec = pl.BlockSpec((tr, C), lambda i: (i, 0))
    return pl.pallas_call(
        body, name=name, grid=(R // tr,),
        in_specs=[pl.BlockSpec((N_DEV, tr, C), lambda i: (0, i, 0)), spec, spec, spec],
        out_specs=[spec, spec, spec, spec],
        out_shape=[jax.ShapeDtypeStruct((R, C), F32)] * 4,
        compiler_params=_cparams(("parallel",), VMEM_LIMIT),
    )(parts, w, m, v)


S5_NS = 256
S5_NH = 2
S5_NCB = 4
S5_RC = 512
S5_NQ = 4
S5_GROUP = 2


def _disc_math(a_re, a_im, log_dt, bt_re, bt_im):
    dt = jnp.exp(log_dt)
    zr, zi = a_re * dt, a_im * dt
    mag = jnp.exp(zr)
    lb_re, lb_im = mag * jnp.cos(zi), mag * jnp.sin(zi)
    den = a_re * a_re + a_im * a_im
    nr, ni = lb_re - 1.0, lb_im
    f_re = (nr * a_re + ni * a_im) / den
    f_im = (ni * a_re - nr * a_im) / den
    bb_re = f_re[:, None, :] * bt_re - f_im[:, None, :] * bt_im
    bb_im = f_re[:, None, :] * bt_im + f_im[:, None, :] * bt_re
    return lb_re, lb_im, bb_re, bb_im


def _disc_fwd(a_re, a_im, log_dt, bt_re, bt_im, chain_len, name):
    G, P = a_re.shape
    C = bt_re.shape[1]
    n_sq = int(round(math.log2(chain_len)))
    assert 2 ** n_sq == chain_len

    def body(a_re_ref, a_im_ref, ldt_ref, br_ref, bi_ref, lr_ref, li_ref, sr_ref, si_ref, bbr_ref, bbi_ref):
        lr, li, bbr, bbi = _disc_math(a_re_ref[...], a_im_ref[...], ldt_ref[...], br_ref[...], bi_ref[...])
        lr_ref[...] = lr
        li_ref[...] = li
        bbr_ref[...] = bbr
        bbi_ref[...] = bbi
        pr, pi = lr, li
        for _ in range(n_sq):
            pr, pi = pr * pr - pi * pi, 2.0 * pr * pi
        sr_ref[...] = pr
        si_ref[...] = pi

    s2 = jax.ShapeDtypeStruct((G, P), F32)
    s3 = jax.ShapeDtypeStruct((G, C, P), F32)
    return pl.pallas_call(body, name=name, out_shape=[s2, s2, s2, s2, s3, s3])(a_re, a_im, log_dt, bt_re, bt_im)


def _disc_bwd(a_re, a_im, log_dt, bt_re, bt_im, d_lr, d_li, d_bbr, d_bbi, name):
    G, P = a_re.shape
    C = bt_re.shape[1]

    def body(a_re_ref, a_im_ref, ldt_ref, br_ref, bi_ref, c1, c2, c3, c4, o1, o2, o3, o4, o5):
        _, vjp = jax.vjp(_disc_math, a_re_ref[...], a_im_ref[...], ldt_ref[...], br_ref[...], bi_ref[...])
        o1[...], o2[...], o3[...], o4[...], o5[...] = vjp((c1[...], c2[...], c3[...], c4[...]))

    s2 = jax.ShapeDtypeStruct((G, P), F32)
    s3 = jax.ShapeDtypeStruct((G, C, P), F32)
    return pl.pallas_call(body, name=name, out_shape=[s2, s2, jax.ShapeDtypeStruct((G, 1), F32), s3, s3])(
        a_re, a_im, log_dt, bt_re, bt_im, d_lr, d_li, d_bbr, d_bbi)


def _row_block(ib):
    return pl.ds(pl.multiple_of(ib * SCAN_LANES, SCAN_LANES), SCAN_LANES)


def _chain_block(j, i, ascending, n_blocks):
    at = j * (n_blocks // S5_NQ) + i
    return _row_block(jnp.where(ascending, at, n_blocks - 1 - at))


def _unrolled_loop(n, unroll, body, carry):
    trips = n // unroll
    carry = lax.fori_loop(
        0, trips, lambda t, c: functools.reduce(lambda cc, u: body(t * unroll + u, cc), range(unroll), c), carry)
    for i in range(trips * unroll, n):
        carry = body(i, carry)
    return carry


def _cmul_add(lr, li, sr, si, xr, xi):
    return lr * sr - li * si + xr, lr * si + li * sr + xi


def _scan(xr_ref, xi_ref, lr, li, init, ascending, n_blocks, store):
    steps = n_blocks // S5_NQ
    if not store:
        def step(i, carry):
            blocks = [_chain_block(j, i, ascending, n_blocks) for j in range(S5_NQ)]
            return tuple(_cmul_add(lr, li, sr, si, xr_ref[rows, :], xi_ref[rows, :])
                         for (sr, si), rows in zip(carry, blocks))

        return _unrolled_loop(steps, 4, step, init)

    group = S5_GROUP
    assert steps % group == 0

    def trip(t, carry):
        blocks = [[_chain_block(j, t * group + u, ascending, n_blocks) for j in range(S5_NQ)] for u in range(group)]
        xs = [[(xr_ref[rows, :], xi_ref[rows, :]) for rows in row] for row in blocks]
        states = list(carry)
        done = []
        for u in range(group):
            states = [_cmul_add(lr, li, sr, si, xr, xi) for (sr, si), (xr, xi) in zip(states, xs[u])]
            done.append(states)
        for u in range(group):
            for rows, (nr, ni) in zip(blocks[u], done[u]):
                xr_ref[rows, :] = nr
                xi_ref[rows, :] = ni
        return tuple(states)

    return lax.fori_loop(0, steps // group, trip, init)


def _segment_starts(w, lsr, lsi, ascending):
    shape = w[0][0].shape
    row = lax.broadcasted_iota(jnp.int32, shape, 0)
    keep = row != jnp.where(ascending, 0, SCAN_LANES - 1)

    def shift(t):
        t = jnp.where(ascending, pltpu.roll(t, 1, 0), pltpu.roll(t, SCAN_LANES - 1, 0))
        return jnp.where(keep, t, 0.0)

    zero = jnp.zeros(shape, F32)
    c = [(zero, zero)] * S5_NQ
    for _ in range(SCAN_LANES):
        tr, ti = _cmul_add(lsr, lsi, *c[-1], *w[-1])
        c[0] = (shift(tr), shift(ti))
        for j in range(1, S5_NQ):
            c[j] = _cmul_add(lsr, lsi, *c[j - 1], *w[j - 1])
    return tuple(c)


def _first_pass(xr_ref, xi_ref, lam_ref, ascending, n_blocks, conj):
    shape = (SCAN_LANES, xr_ref.shape[1])
    sign = -1.0 if conj else 1.0
    lr = jnp.broadcast_to(lam_ref[0:1, :], shape)
    li = sign * jnp.broadcast_to(lam_ref[1:2, :], shape)
    lsr = jnp.broadcast_to(lam_ref[2:3, :], shape)
    lsi = sign * jnp.broadcast_to(lam_ref[3:4, :], shape)
    zero = jnp.zeros(shape, F32)
    w = _scan(xr_ref, xi_ref, lr, li, ((zero, zero),) * S5_NQ, ascending, n_blocks, store=False)
    return _segment_starts(w, lsr, lsi, ascending), lr, li


def _s5_specs(T):
    NS = S5_NS
    tok = pl.BlockSpec((T, 128), lambda c, d, h: (0, c))
    b_spec = pl.BlockSpec((None, None, None, 128, NS), lambda c, d, h: (d, c, h, 0, 0))
    c_spec = pl.BlockSpec((None, None, None, NS, 128), lambda c, d, h: (d, c, h, 0, 0))
    lam_spec = pl.BlockSpec((None, None, None, 4, NS), lambda c, d, h: (d, c, h, 0, 0))
    return tok, b_spec, c_spec, lam_spec


def _s5_fwd(zp, bre, bim, lam, cre, cimn, comm=None):
    T = zp.shape[0]
    NS = S5_NS
    nb = T // SCAN_LANES
    rc = min(S5_RC, T)
    tok, b_spec, c_spec, lam_spec = _s5_specs(T)

    def body(zp_ref, bre_ref, bim_ref, lam_ref, cre_ref, cim_ref, y_ref, xr_ref, xi_ref):
        d = pl.program_id(1)
        ascending = d == 0

        @pl.when((d == 0) & (pl.program_id(2) == 0))
        def _():
            y_ref[...] = jnp.zeros_like(y_ref)

        def proj(c, _):
            rows = pl.ds(pl.multiple_of(c * rc, rc), rc)
            zz = zp_ref[rows, :]
            xr_ref[rows, :] = _dot(zz, bre_ref[...])
            xi_ref[rows, :] = _dot(zz, bim_ref[...])
            return 0

        lax.fori_loop(0, T // rc, proj, 0)
        starts, lr, li = _first_pass(xr_ref, xi_ref, lam_ref, ascending, nb, conj=False)
        _scan(xr_ref, xi_ref, lr, li, starts, ascending, nb, store=True)

        def outp(c, _):
            rows = pl.ds(pl.multiple_of(c * rc, rc), rc)
            y_ref[rows, :] += (_dot(xr_ref[rows, :].astype(BF16), cre_ref[...])
                               + _dot(xi_ref[rows, :].astype(BF16), cim_ref[...]))
            return 0

        lax.fori_loop(0, T // rc, outp, 0)

    return _pallas(
        body, "s5_fwd", (S5_NCB, 2, S5_NH),
        [tok, b_spec, b_spec, lam_spec, c_spec, c_spec], [tok],
        [jax.ShapeDtypeStruct((T, SSM_WIDTH), F32)],
        [pltpu.VMEM((T, NS), F32), pltpu.VMEM((T, NS), F32)],
        ("parallel", "arbitrary", "arbitrary"), (zp, bre, bim, lam, cre, cimn), comm)


def _s5_bwd(zp, dyp, bre, bim, lam, cre, cimn, comm=None):
    T = zp.shape[0]
    NS, NH = S5_NS, S5_NH
    nb = T // SCAN_LANES
    rc = min(S5_RC, T)
    tok, b_spec, c_spec, lam_spec = _s5_specs(T)
    dlam_spec = pl.BlockSpec((None, None, None, 2, NS), lambda c, d, h: (d, c, h, 0, 0))

    def body(zp_ref, dyp_ref, bre_ref, bim_ref, lam_ref, cre_ref, cim_ref,
             dzp_ref, dbre_ref, dbim_ref, dlam_ref, dcre_ref, dcim_ref,
             sr_ref, si_ref, gr_ref, gi_ref):
        d = pl.program_id(1)
        ascending = d == 0
        g_ascending = d != 0

        @pl.when((d == 0) & (pl.program_id(2) == 0))
        def _():
            dzp_ref[...] = jnp.zeros_like(dzp_ref)

        dcre_ref[...] = jnp.zeros_like(dcre_ref)
        dcim_ref[...] = jnp.zeros_like(dcim_ref)
        dbre_ref[...] = jnp.zeros_like(dbre_ref)
        dbim_ref[...] = jnp.zeros_like(dbim_ref)

        def proj(c, _):
            rows = pl.ds(pl.multiple_of(c * rc, rc), rc)
            zz = zp_ref[rows, :]
            sr_ref[rows, :] = _dot(zz, bre_ref[...])
            si_ref[rows, :] = _dot(zz, bim_ref[...])
            dy = dyp_ref[rows, :]
            gr_ref[rows, :] = _dot_nt(dy, cre_ref[...])
            gi_ref[rows, :] = _dot_nt(dy, cim_ref[...])
            return 0

        lax.fori_loop(0, T // rc, proj, 0)
        s_starts, lr, li = _first_pass(sr_ref, si_ref, lam_ref, ascending, nb, conj=False)
        _scan(sr_ref, si_ref, lr, li, s_starts, ascending, nb, store=True)
        g_starts, lr, lic = _first_pass(gr_ref, gi_ref, lam_ref, g_ascending, nb, conj=True)

        steps = nb // S5_NQ
        group = S5_GROUP
        assert steps % group == 0

        def gtrip(t, carry, last):
            g, (ar, ai) = carry
            first = t * group
            blocks = [[_chain_block(j, first + u, g_ascending, nb) for j in range(S5_NQ)] for u in range(group)]
            direct = [[(gr_ref[rows, :], gi_ref[rows, :]) for rows in row] for row in blocks]
            done = []
            for u in range(group):
                new = []
                for j, ((g_r, g_i), (d_r, d_i)) in enumerate(zip(g, direct[u])):
                    n_r, n_i = _cmul_add(lr, lic, g_r, g_i, d_r, d_i)
                    if last and u == group - 1:
                        s_r, s_i = s_starts[S5_NQ - 1 - j]
                    else:
                        prev = _chain_block(j, first + u + 1, g_ascending, nb)
                        s_r, s_i = sr_ref[prev, :], si_ref[prev, :]
                    ar = ar + n_r * s_r + n_i * s_i
                    ai = ai + n_i * s_r - n_r * s_i
                    new.append((n_r, n_i))
                g = new
                done.append(new)
            for u in range(group):
                for rows, (n_r, n_i) in zip(blocks[u], done[u]):
                    gr_ref[rows, :] = n_r
                    gi_ref[rows, :] = n_i
            return tuple(g), (ar, ai)

        zero = jnp.zeros((SCAN_LANES, NS), F32)
        carry = lax.fori_loop(0, steps // group - 1, lambda t, c: gtrip(t, c, False), (g_starts, (zero, zero)))
        _, (ar, ai) = gtrip(steps // group - 1, carry, True)
        dlam_ref[0:1, :] = _col_sum(ar)
        dlam_ref[1:2, :] = _col_sum(ai)

        def grads(c, _):
            rows = pl.ds(pl.multiple_of(c * rc, rc), rc)
            zz = zp_ref[rows, :]
            dy = dyp_ref[rows, :]
            g_rb = gr_ref[rows, :].astype(BF16)
            g_ib = gi_ref[rows, :].astype(BF16)
            dcre_ref[...] += _dot_tn(sr_ref[rows, :].astype(BF16), dy)
            dcim_ref[...] += _dot_tn(si_ref[rows, :].astype(BF16), dy)
            dbre_ref[...] += _dot_tn(zz, g_rb)
            dbim_ref[...] += _dot_tn(zz, g_ib)
            dzp_ref[rows, :] += _dot_nt(g_rb, bre_ref[...]) + _dot_nt(g_ib, bim_ref[...])
            return 0

        lax.fori_loop(0, T // rc, grads, 0)

    f32 = lambda *s: jax.ShapeDtypeStruct(s, F32)
    return _pallas(
        body, "s5_bwd", (S5_NCB, 2, S5_NH),
        [tok, tok, b_spec, b_spec, lam_spec, c_spec, c_spec],
        [tok, b_spec, b_spec, dlam_spec, c_spec, c_spec],
        [f32(T, SSM_WIDTH), f32(2, S5_NCB, NH, 128, NS), f32(2, S5_NCB, NH, 128, NS),
         f32(2, S5_NCB, NH, 2, NS), f32(2, S5_NCB, NH, NS, 128), f32(2, S5_NCB, NH, NS, 128)],
        [pltpu.VMEM((T, NS), F32)] * 4,
        ("parallel", "arbitrary", "arbitrary"), (zp, dyp, bre, bim, lam, cre, cimn), comm)


def _s5_delta():
    d = np.zeros((S5_NH, 8, 8 // S5_NH), np.float32)
    for h in range(S5_NH):
        for go in range(8 // S5_NH):
            d[h, h * (8 // S5_NH) + go, go] = 1.0
    return d


def _s5_pack_b(bbt):
    gh = 8 // S5_NH
    b5 = bbt.reshape(S5_NCB, S5_NH, gh, SSM_GROUP, SSM_STATE).transpose(0, 1, 3, 2, 4)
    m = b5[:, :, None] * _s5_delta()[None, :, :, None, :, None]
    return m.reshape(S5_NCB, S5_NH, 128, S5_NS)


def _s5_unpack_b(dm):
    gh = 8 // S5_NH
    d6 = dm.reshape(S5_NCB, S5_NH, 8, SSM_GROUP, gh, SSM_STATE)
    b5 = jnp.sum(d6 * _s5_delta()[None, :, :, None, :, None], axis=2)
    return b5.transpose(0, 1, 3, 2, 4).reshape(SSM_GROUPS, SSM_GROUP, SSM_STATE)


def _s5_pack_c(c):
    gh = 8 // S5_NH
    c5 = c.reshape(S5_NCB, S5_NH, gh, SSM_GROUP, SSM_STATE).transpose(0, 1, 2, 4, 3)
    m = c5[:, :, :, :, None, :] * _s5_delta().transpose(0, 2, 1)[None, :, :, None, :, None]
    return m.reshape(S5_NCB, S5_NH, S5_NS, 128)


def _s5_unpack_c(dm):
    gh = 8 // S5_NH
    d6 = dm.reshape(S5_NCB, S5_NH, gh, SSM_STATE, 8, SSM_GROUP)
    c5 = jnp.sum(d6 * _s5_delta().transpose(0, 2, 1)[None, :, :, None, :, None], axis=4)
    return c5.transpose(0, 1, 2, 4, 3).reshape(SSM_GROUPS, SSM_GROUP, SSM_STATE)


def _s5_pack_lam(x):
    return x.reshape(S5_NCB, S5_NH, S5_NS)


def _permute_rows(x):
    T = x.shape[0]
    return x.reshape(SCAN_LANES, T // SCAN_LANES, -1).transpose(1, 0, 2).reshape(T, -1)


def _unpermute_rows(x):
    T = x.shape[0]
    return x.reshape(T // SCAN_LANES, SCAN_LANES, -1).transpose(1, 0, 2).reshape(T, -1)


ATT_TB = ATT_ROWS * GRID_W
ATT_KB = 3 * ATT_TB


def _att_valid(i, n_rows):
    qi, kj = np.meshgrid(np.arange(ATT_TB), np.arange(ATT_KB), indexing="ij")
    r = i * ATT_ROWS + qi // GRID_W
    c = qi % GRID_W
    rk = (i - 1) * ATT_ROWS + kj // GRID_W
    x = kj % GRID_W
    rs = np.clip(r - WIN_H // 2, 0, n_rows - WIN_H)
    cs = np.clip(c - WIN_W // 2, 0, GRID_W - WIN_W)
    return (rk >= rs) & (rk < rs + WIN_H) & (x >= cs) & (x < cs + WIN_W)


def _att_masked_tables(table, n_rows):
    n = n_rows // ATT_ROWS
    assert n >= 3
    masks = np.stack([_att_valid(i, n_rows) for i in (0, 1, n - 1)])
    return jnp.where(masks[:, None], table[None], NEG_INF)


def _att_variant(i, n):
    return jnp.where(i == 0, 0, jnp.where(i >= n - 1, 2, 1))


def _att_probs(qh, kh, bias):
    s = _dot_nt(qh, kh) + bias
    p = jnp.exp(s - jnp.max(s, axis=1, keepdims=True))
    return p * (1.0 / jnp.sum(p, axis=1, keepdims=True))


def _att_specs(n, col):
    last = n - 1
    cur = lambda i: (jnp.minimum(i, last), col)
    prv = lambda i: (jnp.maximum(jnp.minimum(i, last) - 1, 0), col)
    nxt = lambda i: (jnp.minimum(i + 1, last), col)
    blk = lambda f: pl.BlockSpec((ATT_TB, ATT_WIDTH), f)
    return blk(cur), blk(prv), blk(nxt)


def _att_fwd(zb, biasv):
    T = zb.shape[0]
    W = ATT_WIDTH
    n = T // ATT_TB
    n_rows = T // GRID_W
    cur = _att_specs(n, 0)[0]
    q_cur = _att_specs(n, 1)[0]
    k_cur, k_prv, k_nxt = _att_specs(n, 2)
    v_cur, v_prv, v_nxt = _att_specs(n, 3)

    def body(q_ref, kp_ref, kc_ref, kn_ref, vp_ref, vc_ref, vn_ref, b_ref, y_ref):
        qs = q_ref[...] * 0.125
        kb = jnp.concatenate([kp_ref[...], kc_ref[...], kn_ref[...]], axis=0)
        vb = jnp.concatenate([vp_ref[...], vc_ref[...], vn_ref[...]], axis=0)
        outs = []
        for h in range(ATT_HEADS):
            hs = slice(h * ATT_HEAD_DIM, (h + 1) * ATT_HEAD_DIM)
            p = _att_probs(qs[:, hs], kb[:, hs], b_ref[h])
            outs.append(_dot(p.astype(BF16), vb[:, hs]))
        y_ref[...] = jnp.concatenate(outs, axis=1).astype(BF16)

    return pl.pallas_call(
        body, name="att_fwd", grid=(n,),
        in_specs=[q_cur, k_prv, k_cur, k_nxt, v_prv, v_cur, v_nxt,
                  pl.BlockSpec((None, ATT_HEADS, ATT_TB, ATT_KB), lambda i: (_att_variant(i, n), 0, 0, 0))],
        out_specs=cur,
        out_shape=jax.ShapeDtypeStruct((T, W), BF16),
        compiler_params=_cparams(("parallel",), VMEM_LIMIT),
    )(zb, zb, zb, zb, zb, zb, zb, biasv)


def _att_bwd(zb, do, biasv, comm=None):
    T = zb.shape[0]
    W = ATT_WIDTH
    n = T // ATT_TB
    n_rows = T // GRID_W
    cur = _att_specs(n, 0)[0]
    q_cur = _att_specs(n, 1)[0]
    k_cur, k_prv, k_nxt = _att_specs(n, 2)
    v_cur, v_prv, v_nxt = _att_specs(n, 3)
    done = pl.BlockSpec((ATT_TB, W), lambda i: (jnp.maximum(i - 1, 0), 0))
    bias_spec = pl.BlockSpec((None, ATT_HEADS, ATT_TB, ATT_KB), lambda i: (_att_variant(i, n), 0, 0, 0))

    def body(q_ref, do_ref, kp_ref, kc_ref, kn_ref, vp_ref, vc_ref, vn_ref, b_ref,
             dq_ref, dk_ref, dv_ref, db_ref, acck_ref, accv_ref):
        i = pl.program_id(0)

        @pl.when((i == 0) | (i == 1) | (i == n - 1))
        def _():
            db_ref[...] = jnp.zeros_like(db_ref)

        @pl.when(i == 0)
        def _():
            acck_ref[...] = jnp.zeros_like(acck_ref)
            accv_ref[...] = jnp.zeros_like(accv_ref)

        @pl.when((i > 0) & (i < n))
        def _():
            slot = lax.rem(i + 1, 3)
            acck_ref[slot] = jnp.zeros((ATT_TB, W), F32)
            accv_ref[slot] = jnp.zeros((ATT_TB, W), F32)

        @pl.when(i < n)
        def _():
            qs = q_ref[...] * 0.125
            dob = do_ref[...]
            kb = jnp.concatenate([kp_ref[...], kc_ref[...], kn_ref[...]], axis=0)
            vb = jnp.concatenate([vp_ref[...], vc_ref[...], vn_ref[...]], axis=0)
            dqs, dks, dvs = [], [], []
            for h in range(ATT_HEADS):
                hs = slice(h * ATT_HEAD_DIM, (h + 1) * ATT_HEAD_DIM)
                qh, kh, vh, doh = qs[:, hs], kb[:, hs], vb[:, hs], dob[:, hs]
                p = _att_probs(qh, kh, b_ref[h])
                dp = _dot_nt(doh, vh)
                ds = p * (dp - jnp.sum(p * dp, axis=1, keepdims=True))
                db_ref[h] += ds
                dsb = ds.astype(BF16)
                dqs.append(_dot(dsb, kh) * 0.125)
                dks.append(_dot_tn(dsb, qh))
                dvs.append(_dot_tn(p.astype(BF16), doh))
            dq_ref[...] = jnp.concatenate(dqs, axis=1).astype(BF16)
            dk_all = jnp.concatenate(dks, axis=1)
            dv_all = jnp.concatenate(dvs, axis=1)
            for b in range(3):
                slot = lax.rem(i + 2 + b, 3)
                rows = slice(b * ATT_TB, (b + 1) * ATT_TB)
                acck_ref[slot] += dk_all[rows]
                accv_ref[slot] += dv_all[rows]

        slot = lax.rem(i + 2, 3)
        dk_ref[...] = acck_ref[slot].astype(BF16)
        dv_ref[...] = accv_ref[slot].astype(BF16)

    return _pallas(
        body, "att_bwd", (n + 1,),
        [q_cur, cur, k_prv, k_cur, k_nxt, v_prv, v_cur, v_nxt, bias_spec],
        [cur, done, done, bias_spec],
        [jax.ShapeDtypeStruct((T, W), BF16)] * 3 + [jax.ShapeDtypeStruct((3, ATT_HEADS, ATT_TB, ATT_KB), F32)],
        [pltpu.VMEM((3, ATT_TB, W), F32), pltpu.VMEM((3, ATT_TB, W), F32)],
        ("arbitrary",), (zb, do, zb, zb, zb, zb, zb, zb, biasv), comm)


def _att_selectors():
    rsel = np.zeros((ATT_ROWS, 3 * ATT_ROWS, 2 * WIN_H - 1), np.float32)
    for a in range(ATT_ROWS):
        for b in range(3 * ATT_ROWS):
            rsel[a, b, b - a - ATT_ROWS + WIN_H - 1] = 1.0
    csel = np.zeros((GRID_W, GRID_W, 2 * WIN_W - 1), np.float32)
    for c in range(GRID_W):
        for x in range(GRID_W):
            csel[c, x, min(max(x - c, -(WIN_W - 1)), WIN_W - 1) + WIN_W - 1] = 1.0
    return rsel, csel


def _att_bias_table(rpb):
    rsel, csel = _att_selectors()
    hi = lax.Precision.HIGHEST
    t = jnp.einsum('hrd,abr->habd', rpb, rsel, precision=hi)
    t = jnp.einsum('habd,cxd->hacbx', t, csel, precision=hi)
    return t.reshape(ATT_HEADS, ATT_TB, ATT_KB)


def _att_bias_table_t(dtable):
    rsel, csel = _att_selectors()
    hi = lax.Precision.HIGHEST
    t = dtable.reshape(ATT_HEADS, ATT_ROWS, GRID_W, 3 * ATT_ROWS, GRID_W)
    t = jnp.einsum('hacbx,cxd->habd', t, csel, precision=hi)
    return jnp.einsum('habd,abr->hrd', t, rsel, precision=hi)


GELU_K = math.sqrt(2.0 / math.pi)
GELU_C = 0.044715
MERGE_TM = 256


def _gelu(x):
    return 0.5 * x * (1.0 + jnp.tanh(GELU_K * (x + GELU_C * x * x * x)))


def _gelu_grad(x):
    t = jnp.tanh(GELU_K * (x + GELU_C * x * x * x))
    return 0.5 * (1.0 + t) + 0.5 * x * (1.0 - t * t) * GELU_K * (1.0 + 3.0 * GELU_C * x * x)


def _merge_forward(ypre, zs, gs, ga, ya, ssm_d, w_glu, b_glu, w_bs, w_ba):
    ys = ypre + ssm_d * zs
    yg = _gelu(ys)
    sg = jax.nn.sigmoid(_dot(yg.astype(BF16), w_glu) + b_glu)
    y2 = yg * sg
    bs = _dot(y2.astype(BF16), w_bs)
    ba = _dot(ya, w_ba)
    s1 = jax.nn.sigmoid(gs)
    s2 = jax.nn.sigmoid(ga)
    merged = s1 * bs + s2 * ba
    return ys, yg, sg, y2, bs, ba, s1, s2, merged


def _merge_in_specs(D, W, tm):
    tok = lambda w, c: pl.BlockSpec((tm, w), lambda i: (i, c))
    full = lambda r, c: pl.BlockSpec((r, c), lambda i: (0, 0))
    z_specs = [tok(W, 0), tok(D, 4 * W // D), tok(D, 4 * W // D + 1)]
    w_specs = [full(1, W), full(W, W), full(1, W), full(W, D), full(W, D), full(D, D)]
    return tok, z_specs, w_specs


def _merge_fwd(ypre, z, ya, h1, ssm_d, w_glu, b_glu, w_bs, w_ba, w_out):
    T, D = h1.shape
    W = ypre.shape[1]
    tm = min(T, MERGE_TM)
    tok, z_specs, w_specs = _merge_in_specs(D, W, tm)

    def body(ypre_ref, zs_ref, gs_ref, ga_ref, ya_ref, h1_ref, d_ref, wglu_ref, bglu_ref, wbs_ref, wba_ref, wout_ref,
             h2_ref):
        merged = _merge_forward(ypre_ref[...], zs_ref[...], gs_ref[...], ga_ref[...], ya_ref[...], d_ref[...],
                                wglu_ref[...], bglu_ref[...], wbs_ref[...], wba_ref[...])[-1]
        h2_ref[...] = h1_ref[...] + _dot(merged.astype(BF16), wout_ref[...])

    return pl.pallas_call(
        body, name="merge_fwd", grid=(T // tm,),
        in_specs=[tok(W, 0)] + z_specs + [tok(W, 0), tok(D, 0)] + w_specs,
        out_specs=tok(D, 0),
        out_shape=jax.ShapeDtypeStruct((T, D), F32),
        compiler_params=_cparams(("parallel",), VMEM_LIMIT),
    )(ypre, z, z, z, ya, h1, ssm_d, w_glu, b_glu, w_bs, w_ba, w_out)


def _merge_bwd(dh2, ypre, z, ya, ssm_d, w_glu, b_glu, w_bs, w_ba, w_out):
    T, D = dh2.shape
    W = ypre.shape[1]
    tm = min(T, MERGE_TM)
    tok, z_specs, w_specs = _merge_in_specs(D, W, tm)

    def body(dh2_ref, ypre_ref, zs_ref, gs_ref, ga_ref, ya_ref, d_ref, wglu_ref, bglu_ref, wbs_ref, wba_ref, wout_ref,
             dypre_ref, dzs_ref, dgs_ref, dga_ref, dya_ref, dd_ref, dwglu_ref, dbglu_ref, dwbs_ref, dwba_ref, dwout_ref):
        @pl.when(pl.program_id(0) == 0)
        def _():
            for r in (dd_ref, dwglu_ref, dbglu_ref, dwbs_ref, dwba_ref, dwout_ref):
                r[...] = jnp.zeros_like(r)

        zs = zs_ref[...]
        ya = ya_ref[...]
        ys, yg, sg, y2, bs, ba, s1, s2, merged = _merge_forward(
            ypre_ref[...], zs, gs_ref[...], ga_ref[...], ya, d_ref[...],
            wglu_ref[...], bglu_ref[...], wbs_ref[...], wba_ref[...])
        dh2b = dh2_ref[...].astype(BF16)
        dmerged = _dot_nt(dh2b, wout_ref[...])
        dwout_ref[...] += _dot_tn(merged.astype(BF16), dh2b)
        dbs = (dmerged * s1).astype(BF16)
        dba = (dmerged * s2).astype(BF16)
        dgs_ref[...] = (dmerged * bs * s1 * (1.0 - s1)).astype(BF16)
        dga_ref[...] = (dmerged * ba * s2 * (1.0 - s2)).astype(BF16)
        dwbs_ref[...] += _dot_tn(y2.astype(BF16), dbs)
        dwba_ref[...] += _dot_tn(ya, dba)
        dya_ref[...] = _dot_nt(dba, wba_ref[...]).astype(BF16)
        dy2 = _dot_nt(dbs, wbs_ref[...])
        dvv = dy2 * yg * sg * (1.0 - sg)
        dvvb = dvv.astype(BF16)
        dyg = dy2 * sg + _dot_nt(dvvb, wglu_ref[...])
        dwglu_ref[...] += _dot_tn(yg.astype(BF16), dvvb)
        dbglu_ref[...] += _col_sum(dvv)
        dys = dyg * _gelu_grad(ys)
        dd_ref[...] += _col_sum(dys * zs)
        dzs_ref[...] = dys * d_ref[...]
        dypre_ref[...] = dys.astype(BF16)

    f32 = lambda *s: jax.ShapeDtypeStruct(s, F32)
    b16 = lambda *s: jax.ShapeDtypeStruct(s, BF16)
    return pl.pallas_call(
        body, name="merge_bwd", grid=(T // tm,),
        in_specs=[tok(D, 0), tok(W, 0)] + z_specs + [tok(W, 0)] + w_specs,
        out_specs=[tok(W, 0), tok(W, 0), tok(D, 0), tok(D, 0), tok(W, 0)] + w_specs,
        out_shape=[b16(T, W), f32(T, W), b16(T, D), b16(T, D), b16(T, W),
                   f32(1, W), f32(W, W), f32(1, W), f32(W, D), f32(W, D), f32(D, D)],
        compiler_params=_cparams(("arbitrary",), VMEM_LIMIT),
    )(dh2, ypre, z, z, z, ya, ssm_d, w_glu, b_glu, w_bs, w_ba, w_out)


def _cast_shards(weights):
    def body(*refs):
        n = len(refs) // 2
        for src, dst in zip(refs[:n], refs[n:]):
            dst[...] = src[0].astype(BF16)

    return pl.pallas_call(
        body, name="cast_shards",
        out_shape=[jax.ShapeDtypeStruct(w.shape[1:], BF16) for w in weights],
        compiler_params=_cparams(None, VMEM_LIMIT))(*weights)


def _gather_two_level(shards, name):
    n = len(shards)

    def body(*refs):
        x_refs, out_refs = refs[:n], refs[n:2 * n]
        send_sems, recv_sems, local_sems = refs[2 * n:]
        x, y, c = _my_place()
        me, sibling = (x, y, c), (x, y, 1 - c)
        chips = [(1 - x, y), (x, 1 - y), (1 - x, 1 - y)]

        def copy(a, k, block, to, own=False):
            slot = out_refs[a].at[_flat(*block)]
            return pltpu.make_async_remote_copy(
                src_ref=x_refs[a] if own else slot, dst_ref=slot,
                send_sem=send_sems.at[7 * a + k], recv_sem=recv_sems.at[7 * a + k],
                device_id=to, device_id_type=MESH_ID)

        sent, local = [], []
        for a in range(n):
            local.append(pltpu.make_async_copy(x_refs[a], out_refs[a].at[_flat(*me)], local_sems.at[a]))
            local[-1].start()
            sent.append(copy(a, 0, me, sibling, own=True))
            sent += [copy(a, 1 + j, me, (*chip, c), own=True) for j, chip in enumerate(chips)]
        for cp in sent:
            cp.start()
        for a in range(n):
            for j, chip in enumerate(chips):
                copy(a, 1 + j, (*chip, c), me).wait_recv()
                sent.append(copy(a, 4 + j, (*chip, c), sibling))
                sent[-1].start()
        for a in range(n):
            copy(a, 0, sibling, me).wait_recv()
            for j, chip in enumerate(chips):
                copy(a, 4 + j, (*chip, 1 - c), me).wait_recv()
        for cp in sent:
            cp.wait_send()
        for cp in local:
            cp.wait()

    return pl.pallas_call(
        body, name=name, in_specs=[_HBM] * n, out_specs=[_HBM] * n,
        out_shape=[jax.ShapeDtypeStruct((N_DEV,) + s.shape, s.dtype) for s in shards],
        scratch_shapes=[pltpu.SemaphoreType.DMA((7 * n,)), pltpu.SemaphoreType.DMA((7 * n,)),
                        pltpu.SemaphoreType.DMA((n,))],
    )(*shards)


PACK_COLS = 1024
BIG = (("ffn1_w_gate", 1), ("ffn1_w_up", 1), ("ffn1_w_down", 0), ("w_in", 1), ("ssm_w_glu", 0),
       ("w_branch_ssm", 1), ("w_branch_att", 1), ("w_out", 0),
       ("ffn2_w_gate", 1), ("ffn2_w_up", 1), ("ffn2_w_down", 0))
BIG_AXIS = dict(BIG)
TRANSPOSED = ("ffn1_w_gate", "ffn1_w_up", "ffn2_w_gate", "ffn2_w_up")
SSM_DIR = ("ssm_a_re", "ssm_a_im", "ssm_log_dt", "ssm_b_re", "ssm_b_im", "ssm_c_re", "ssm_c_im")
SMALL_EARLY = (("mix_norm",) + tuple(n + "_fwd" for n in SSM_DIR) + tuple(n + "_bwd" for n in SSM_DIR)
               + ("ssm_d", "ssm_b_glu", "att_rpb", "ffn2_norm", "final_norm"))
SMALL_LATE = ("ffn1_norm",)
WEIGHTS = ("ffn1_norm", "ffn1_w_gate", "ffn1_w_up", "ffn1_w_down", "mix_norm", "w_in") \
    + tuple(n + "_fwd" for n in SSM_DIR) + tuple(n + "_bwd" for n in SSM_DIR) \
    + ("ssm_d", "ssm_w_glu", "ssm_b_glu", "att_rpb", "w_branch_ssm", "w_branch_att", "w_out",
       "ffn2_norm", "ffn2_w_gate", "ffn2_w_up", "ffn2_w_down", "final_norm")


def _pad_rows(a, mult):
    pad = (-a.shape[-2]) % mult
    if pad:
        a = jnp.concatenate([a, jnp.zeros(a.shape[:-2] + (pad, a.shape[-1]), a.dtype)], axis=-2)
    return a


def _pack(arrays, row_mult):
    flat = jnp.concatenate([a.reshape(-1) for a in arrays])
    pad = (-flat.shape[0]) % PACK_COLS
    if pad:
        flat = jnp.concatenate([flat, jnp.zeros((pad,), flat.dtype)])
    return _pad_rows(flat.reshape(-1, PACK_COLS), row_mult)


def _unpack(slab, shapes):
    flat = slab.reshape(-1)
    out, at = [], 0
    for s in shapes:
        n = int(np.prod(s))
        out.append(flat[at:at + n].reshape(s))
        at += n
    return out


def _split_for_devices(g, axis):
    r, c = g.shape
    if axis == 1:
        return g.reshape(r, N_DEV, c // N_DEV).transpose(1, 0, 2).astype(BF16)
    return g.reshape(N_DEV, r // N_DEV, c).astype(BF16)


def _join_shards(gathered, axis):
    _, r, c = gathered.shape
    if axis == 1:
        return gathered.transpose(1, 0, 2).reshape(r, N_DEV * c)
    return gathered.reshape(N_DEV * r, c)


def _s5_direction_inputs(p, sfx, chain_len):
    bt_re = p["ssm_b_re" + sfx][0].transpose(0, 2, 1)
    bt_im = p["ssm_b_im" + sfx][0].transpose(0, 2, 1)
    raw = (p["ssm_a_re" + sfx][0], p["ssm_a_im" + sfx][0], p["ssm_log_dt" + sfx][0][:, None], bt_re, bt_im)
    lr, li, sr, si, bbr, bbi = _disc_fwd(*raw, chain_len,"s5_disc" + sfx)
    lam = jnp.stack([_s5_pack_lam(t) for t in (lr, li, sr, si)], axis=2)
    mats = (_s5_pack_b(bbr), _s5_pack_b(bbi), lam,
            _s5_pack_c(p["ssm_c_re" + sfx][0]), _s5_pack_c(-p["ssm_c_im" + sfx][0]))
    return raw, mats


def kernel(x, ffn1_norm, ffn1_w_gate, ffn1_w_up, ffn1_w_down, mix_norm, w_in, ssm_a_re_fwd, ssm_a_im_fwd, ssm_log_dt_fwd, ssm_b_re_fwd, ssm_b_im_fwd, ssm_c_re_fwd, ssm_c_im_fwd, ssm_a_re_bwd, ssm_a_im_bwd, ssm_log_dt_bwd, ssm_b_re_bwd, ssm_b_im_bwd, ssm_c_re_bwd, ssm_c_im_bwd, ssm_d, ssm_w_glu, ssm_b_glu, att_rpb, w_branch_ssm, w_branch_att, w_out, ffn2_norm, ffn2_w_gate, ffn2_w_up, ffn2_w_down, final_norm, loss_target, m_ffn1_norm, m_ffn1_w_gate, m_ffn1_w_up, m_ffn1_w_down, m_mix_norm, m_w_in, m_ssm_a_re_fwd, m_ssm_a_im_fwd, m_ssm_log_dt_fwd, m_ssm_b_re_fwd, m_ssm_b_im_fwd, m_ssm_c_re_fwd, m_ssm_c_im_fwd, m_ssm_a_re_bwd, m_ssm_a_im_bwd, m_ssm_log_dt_bwd, m_ssm_b_re_bwd, m_ssm_b_im_bwd, m_ssm_c_re_bwd, m_ssm_c_im_bwd, m_ssm_d, m_ssm_w_glu, m_ssm_b_glu, m_att_rpb, m_w_branch_ssm, m_w_branch_att, m_w_out, m_ffn2_norm, m_ffn2_w_gate, m_ffn2_w_up, m_ffn2_w_down, m_final_norm, v_ffn1_norm, v_ffn1_w_gate, v_ffn1_w_up, v_ffn1_w_down, v_mix_norm, v_w_in, v_ssm_a_re_fwd, v_ssm_a_im_fwd, v_ssm_log_dt_fwd, v_ssm_b_re_fwd, v_ssm_b_im_fwd, v_ssm_c_re_fwd, v_ssm_c_im_fwd, v_ssm_a_re_bwd, v_ssm_a_im_bwd, v_ssm_log_dt_bwd, v_ssm_b_re_bwd, v_ssm_b_im_bwd, v_ssm_c_re_bwd, v_ssm_c_im_bwd, v_ssm_d, v_ssm_w_glu, v_ssm_b_glu, v_att_rpb, v_w_branch_ssm, v_w_branch_att, v_w_out, v_ffn2_norm, v_ffn2_w_gate, v_ffn2_w_up, v_ffn2_w_down, v_final_norm):
    p = dict(locals())
    x = p["x"][0]
    target = p["loss_target"][0]
    T, D = x.shape

    stored = lambda a, n: jnp.swapaxes(a, -1, -2) if n in TRANSPOSED else a
    cut_axis = lambda n: 0 if n in TRANSPOSED else BIG_AXIS[n]
    shard = dict(zip([n for n, _ in BIG], _cast_shards([stored(p[n], n) for n, _ in BIG])))
    ffn1_w = ("ffn1_w_gate", "ffn1_w_up", "ffn1_w_down")
    mix_w = ("w_in", "ssm_w_glu", "w_branch_ssm", "w_branch_att", "w_out")
    ffn2_w = ("ffn2_w_gate", "ffn2_w_up", "ffn2_w_down")
    gathered = dict(zip(ffn1_w, _gather_two_level([shard[n] for n in ffn1_w], "gather_ffn1")))
    full = lambda n: _join_shards(gathered[n], cut_axis(n))

    h0 = x
    wg1, wu1, wd1 = [full(n) for n in ffn1_w]
    (h1, xn1, g1, u1), got = _ffn_fwd(h0, p["ffn1_norm"], wg1, wu1, wd1, "ffn1_fwd",
                                      _Comm("gather", [shard[n] for n in mix_w]))
    gathered.update(zip(mix_w, got))
    z, zb, un = _mixin_fwd(h1, p["mix_norm"], gathered["w_in"])
    W = SSM_WIDTH
    zp = _permute_rows(zb[:, :W])
    chain_len = T // SCAN_LANES // S5_NQ
    raw_f, mats_f = _s5_direction_inputs(p, "_fwd", chain_len)
    raw_b, mats_b = _s5_direction_inputs(p, "_bwd", chain_len)
    bre, bim, lam, cre, cimn = [jnp.stack([f, b]) for f, b in zip(mats_f, mats_b)]
    bre, bim, cre, cimn = [t.astype(BF16) for t in (bre, bim, cre, cimn)]
    (yp,), got = _s5_fwd(zp, bre, bim, lam, cre, cimn, _Comm("gather", [shard[n] for n in ffn2_w]))
    gathered.update(zip(ffn2_w, got))
    ypre = _unpermute_rows(yp)
    table = _att_masked_tables(_att_bias_table(p["att_rpb"][0]), T // GRID_W)
    ya = _att_fwd(zb, table)
    tail_w = (p["ssm_d"], full("ssm_w_glu"), p["ssm_b_glu"], full("w_branch_ssm"), full("w_branch_att"), full("w_out"))
    h2 = _merge_fwd(ypre, z, ya, h1, *tail_w)
    wg2, wu2, wd2 = [full(n) for n in ffn2_w]
    (h3, xn2, g2, u2), _ = _ffn_fwd(h2, p["ffn2_norm"], wg2, wu2, wd2, "ffn2_fwd")
    loss_part, dh3, d_final = _loss_head(h3, p["final_norm"][None], target)

    grads = {"final_norm": d_final[0]}
    to_send = lambda names: _Comm("exchange", [_split_for_devices(grads[n], cut_axis(n)) for n in names])
    parts = {}
    (dh2, grads["ffn2_norm"], do2, a2, dg2, du2), _ = _ffn_bwd(
        dh3, h2, p["ffn2_norm"], g2, u2, wg2, wu2, wd2, "ffn2_bwd")
    grads["ffn2_w_gate"] = _xty(dg2, xn2, "ffn2_dw_gate")
    grads["ffn2_w_up"] = _xty(du2, xn2, "ffn2_dw_up")
    grads["ffn2_w_down"] = _xty(a2, do2, "ffn2_dw_down")
    (dypre, dzs_skip, dgs, dga, dya, grads["ssm_d"], grads["ssm_w_glu"], grads["ssm_b_glu"],
     grads["w_branch_ssm"], grads["w_branch_att"], grads["w_out"]) = _merge_bwd(dh2, ypre, z, ya, *tail_w)
    (dq, dk, dv, dtable), got = _att_bwd(zb, dya, table, to_send(ffn2_w))
    parts.update(zip(ffn2_w, got))
    grads["att_rpb"] = _att_bias_table_t(jnp.sum(dtable, axis=0))
    dyp = _permute_rows(dypre)
    tail_names = ("ssm_w_glu", "w_branch_ssm", "w_branch_att", "w_out")
    (dzp, dbre, dbim, dlam, dcre, dcimn), got = _s5_bwd(zp, dyp, bre, bim, lam, cre, cimn, to_send(tail_names))
    parts.update(zip(tail_names, got))
    G, P = SSM_GROUPS, SSM_STATE
    for d, (sfx, raw) in enumerate((("_fwd", raw_f), ("_bwd", raw_b))):
        da_re, da_im, dldt, dbt_re, dbt_im = _disc_bwd(
            *raw, dlam[d, :, :, 0, :].reshape(G, P), dlam[d, :, :, 1, :].reshape(G, P),
            _s5_unpack_b(dbre[d]), _s5_unpack_b(dbim[d]), "s5_disc_grad" + sfx)
        grads["ssm_a_re" + sfx] = da_re
        grads["ssm_a_im" + sfx] = da_im
        grads["ssm_log_dt" + sfx] = dldt[:, 0]
        grads["ssm_b_re" + sfx] = dbt_re.transpose(0, 2, 1)
        grads["ssm_b_im" + sfx] = dbt_im.transpose(0, 2, 1)
        grads["ssm_c_re" + sfx] = _s5_unpack_c(dcre[d])
        grads["ssm_c_im" + sfx] = -_s5_unpack_c(dcimn[d])
    dzs = _unpermute_rows(dzp) + dzs_skip
    dz = jnp.concatenate([dzs.astype(BF16), dq, dk, dv, dgs, dga], axis=1)
    dh1, grads["mix_norm"] = _mixin_bwd(dz, dh2, h1, p["mix_norm"], gathered["w_in"])
    grads["w_in"] = _xty(un, dz, "dw_in", col_shards=N_DEV)
    pack_small = lambda names, src, pre: _pack([src[pre + n].astype(F32) for n in names], 8)
    early = _Comm(["exchange", "gather"],
                  [grads["w_in"], pack_small(SMALL_EARLY, grads, "")])
    (dh0, grads["ffn1_norm"], do1, a1, dg1, du1), (parts["w_in"], got_early) = _ffn_bwd(
        dh1, h0, p["ffn1_norm"], g1, u1, wg1, wu1, wd1, "ffn1_bwd", early)
    grads["ffn1_w_down"] = _xty(a1, do1, "ffn1_dw_down")
    grads["ffn1_w_gate"], (parts["ffn1_w_down"],) = _xty(dg1, xn1, "ffn1_dw_gate", to_send(("ffn1_w_down",)))
    grads["ffn1_w_up"], (parts["ffn1_w_gate"],) = _xty(du1, xn1, "ffn1_dw_up", to_send(("ffn1_w_gate",)))
    last = _Comm(["exchange", "gather"],
                 [_split_for_devices(grads["ffn1_w_up"], 0), pack_small(SMALL_LATE, grads, "")])
    parts["ffn1_w_up"], got_late = _comm_call(last, "exchange_last")
    got_small = jnp.concatenate([got_early, got_late], axis=1)

    results = {}
    for n, _ in BIG:
        outs = _adamw(parts[n], *[stored(p[pre + n][0], n) for pre in ("", "m_", "v_")], "adamw_" + n)
        results[n] = [stored(o, n)[None] for o in outs]
    early_rows = got_early.shape[1]
    slab = lambda pre: jnp.concatenate([pack_small(SMALL_EARLY, p, pre), pack_small(SMALL_LATE, p, pre)], axis=0)
    small_out = _adamw(got_small, slab(""), slab("m_"), slab("v_"), "adamw_small")
    for names, rows in ((SMALL_EARLY, slice(0, early_rows)), (SMALL_LATE, slice(early_rows, None))):
        shapes = [p[n].shape for n in names]
        for n, vals in zip(names, zip(*[_unpack(out[rows], shapes) for out in small_out])):
            results[n] = list(vals)

    loss = lax.psum(loss_part[0, 0], ("x", "y", "c"))
    out = [loss, dh0[None]]
    for kind in range(4):
        out += [results[n][kind] for n in WEIGHTS]
    return tuple(out)
```

```python
import functools
import math

import numpy as np
import jax
import jax.numpy as jnp
from jax import lax
from jax.experimental import pallas as pl
from jax.experimental.pallas import tpu as pltpu

F32 = jnp.float32
BF16 = jnp.bfloat16
MESH_ID = pl.DeviceIdType.MESH

SSM_GROUP = 16
SSM_GROUPS = 32
SSM_STATE = 64
SSM_WIDTH = 512
ATT_HEADS = 8
ATT_HEAD_DIM = 64
ATT_WIDTH = 512
GRID_W = 64
WIN_H = 8
WIN_W = 16
EPS = 1e-6
NEG_INF = -1e30
ADAM_LR = 0.001
ADAM_B1 = 0.9
ADAM_B2 = 0.999
ADAM_EPS = 1e-08
ADAM_WD = 0.01
ADAM_STEP = 10

N_DEV = 8
V7X_VMEM_BYTES = 64 * 1024 * 1024
VMEM_LIMIT = V7X_VMEM_BYTES - 8 * 1024 * 1024
SCAN_LANES = 8
ATT_ROWS = 4


def _cparams(sem, vmem=None):
    return pltpu.CompilerParams(dimension_semantics=sem, vmem_limit_bytes=vmem)


def _dot(a, b):
    return jnp.dot(a, b, preferred_element_type=F32)


def _dot_nt(a, b):
    return lax.dot_general(a, b, (((1,), (1,)), ((), ())), preferred_element_type=F32)


def _dot_tn(a, b):
    return lax.dot_general(a, b, (((0,), (0,)), ((), ())), preferred_element_type=F32)


def _rms(h):
    return lax.rsqrt(jnp.mean(h * h, axis=-1, keepdims=True) + EPS)


def _rms_bwd(h, r, v):
    return r * v - h * (r * r * r) * jnp.mean(h * v, axis=-1, keepdims=True)


def _col_sum(x):
    return jnp.sum(x, axis=0, keepdims=True)


def _my_place():
    return lax.axis_index("x"), lax.axis_index("y"), lax.axis_index("c")


def _flat(px, py, pc):
    return 4 * px + 2 * py + pc


class _Comm:
    def __init__(self, kind, arrays):
        self.arrays = list(arrays)
        self.n = len(self.arrays)
        self.kinds = [kind] * self.n if isinstance(kind, str) else list(kind)

    def out_shapes(self):
        return [jax.ShapeDtypeStruct((N_DEV,) + a.shape if k == "gather" else a.shape, a.dtype)
                for k, a in zip(self.kinds, self.arrays)]

    def scratch(self):
        return [pltpu.SemaphoreType.DMA((7 * self.n,)), pltpu.SemaphoreType.DMA((7 * self.n,)),
                pltpu.SemaphoreType.DMA((self.n,))]

    def run(self, srcs, dsts, sems, start):
        send_sems, recv_sems, local_sems = sems
        x, y, c = _my_place()
        mine = _flat(x, y, c)
        for a, (src, dst) in enumerate(zip(srcs, dsts)):
            whole = self.kinds[a] == "gather"
            local = pltpu.make_async_copy(src if whole else src.at[mine], dst.at[mine], local_sems.at[a])
            local.start() if start else local.wait()
            for k in range(1, N_DEV):
                px = 1 - x if k & 4 else x
                py = 1 - y if k & 2 else y
                pc = 1 - c if k & 1 else c
                cp = pltpu.make_async_remote_copy(
                    src_ref=src if whole else src.at[_flat(px, py, pc)], dst_ref=dst.at[mine],
                    send_sem=send_sems.at[7 * a + k - 1], recv_sem=recv_sems.at[7 * a + k - 1],
                    device_id=(px, py, pc), device_id_type=MESH_ID)
                cp.start() if start else cp.wait()


_HBM = pl.BlockSpec(memory_space=pltpu.HBM)


def _comm_call(comm, name):
    def body(*refs):
        srcs, dsts, sems = refs[:comm.n], refs[comm.n:2 * comm.n], refs[2 * comm.n:]
        comm.run(srcs, dsts, sems, True)
        comm.run(srcs, dsts, sems, False)

    return pl.pallas_call(body, name=name, in_specs=[_HBM] * comm.n, out_specs=[_HBM] * comm.n,
                          out_shape=comm.out_shapes(), scratch_shapes=comm.scratch())(*comm.arrays)


def _pallas(core, name, grid, in_specs, out_specs, out_shape, scratch, sem, args, comm=None):
    if comm is None:
        out = pl.pallas_call(core, name=name, grid=grid, in_specs=in_specs, out_specs=out_specs,
                             out_shape=out_shape, scratch_shapes=scratch,
                             compiler_params=_cparams(sem, VMEM_LIMIT))(*args)
        return out, []
    n_in, n_out, n_scr, n = len(in_specs), len(out_specs), len(scratch), comm.n

    def body(*refs):
        ins, srcs = refs[:n_in], refs[n_in:n_in + n]
        outs, dsts = refs[n_in + n:n_in + n + n_out], refs[n_in + n + n_out:n_in + 2 * n + n_out]
        scr, sems = refs[n_in + 2 * n + n_out:n_in + 2 * n + n_out + n_scr], refs[n_in + 2 * n + n_out + n_scr:]
        ids = [pl.program_id(k) for k in range(len(grid))]
        first = functools.reduce(lambda a, b: a & b, [i == 0 for i in ids])
        last = functools.reduce(lambda a, b: a & b, [i == g - 1 for i, g in zip(ids, grid)])

        @pl.when(first)
        def _():
            comm.run(srcs, dsts, sems, True)

        core(*ins, *outs, *scr)

        @pl.when(last)
        def _():
            comm.run(srcs, dsts, sems, False)

    out = pl.pallas_call(
        body, name=name, grid=grid, in_specs=list(in_specs) + [_HBM] * n, out_specs=list(out_specs) + [_HBM] * n,
        out_shape=list(out_shape) + comm.out_shapes(), scratch_shapes=list(scratch) + comm.scratch(),
        compiler_params=_cparams(("arbitrary",) * len(grid), VMEM_LIMIT))(*args, *comm.arrays)
    return out[:n_out], out[n_out:]


def _ffn_tiles(T, F, wide):
    if F % 1408 == 0:
        return min(T, 512 if wide else 256), 1408
    return min(T, 1024), 256 if F % 256 == 0 else F


def _ffn_fwd(h, gain, wg, wu, wd, name, comm=None):
    T, D = h.shape
    F = wg.shape[0]
    tm, tf = _ffn_tiles(T, F, wide=True)
    nj = F // tf

    def body(h_ref, gain_ref, wg_ref, wu_ref, wd_ref, ho_ref, xn_ref, g_ref, u_ref, acc_ref):
        j = pl.program_id(1)

        @pl.when(j == 0)
        def _():
            hh = h_ref[...]
            xn_ref[...] = (hh * _rms(hh) * gain_ref[...]).astype(BF16)
            acc_ref[...] = jnp.zeros_like(acc_ref)

        xn = xn_ref[...]
        g = _dot_nt(xn, wg_ref[...])
        u = _dot_nt(xn, wu_ref[...])
        g_ref[...] = g.astype(BF16)
        u_ref[...] = u.astype(BF16)
        a = (g * jax.nn.sigmoid(g) * u).astype(BF16)
        acc_ref[...] += _dot(a, wd_ref[...])

        @pl.when(j == nj - 1)
        def _():
            ho_ref[...] = h_ref[...] + 0.5 * acc_ref[...]

    return _pallas(
        body, name, (T // tm, nj),
        [pl.BlockSpec((tm, D), lambda i, j: (i, 0)),
         pl.BlockSpec((1, D), lambda i, j: (0, 0)),
         pl.BlockSpec((tf, D), lambda i, j: (j, 0)),
         pl.BlockSpec((tf, D), lambda i, j: (j, 0)),
         pl.BlockSpec((tf, D), lambda i, j: (j, 0))],
        [pl.BlockSpec((tm, D), lambda i, j: (i, 0)),
         pl.BlockSpec((tm, D), lambda i, j: (i, 0)),
         pl.BlockSpec((tm, tf), lambda i, j: (i, j)),
         pl.BlockSpec((tm, tf), lambda i, j: (i, j))],
        [jax.ShapeDtypeStruct((T, D), F32), jax.ShapeDtypeStruct((T, D), BF16),
         jax.ShapeDtypeStruct((T, F), BF16), jax.ShapeDtypeStruct((T, F), BF16)],
        [pltpu.VMEM((tm, D), F32)], ("parallel", "arbitrary"), (h, gain, wg, wu, wd), comm)


def _ffn_bwd(dho, h, gain, g, u, wg, wu, wd, name, comm=None):
    T, D = h.shape
    F = wg.shape[0]
    tm, tf = _ffn_tiles(T, F, wide=False)
    nj = F // tf

    def body(dho_ref, h_ref, gain_ref, g_ref, u_ref, wg_ref, wu_ref, wd_ref,
             dh_ref, dgain_ref, do_ref, a_ref, dg_ref, du_ref, acc_ref):
        i = pl.program_id(0)
        j = pl.program_id(1)

        @pl.when(j == 0)
        def _():
            do_ref[...] = (0.5 * dho_ref[...]).astype(BF16)
            acc_ref[...] = jnp.zeros_like(acc_ref)

        @pl.when((i == 0) & (j == 0))
        def _():
            dgain_ref[...] = jnp.zeros_like(dgain_ref)

        da = _dot_nt(do_ref[...], wd_ref[...])
        gg = g_ref[...].astype(F32)
        uu = u_ref[...].astype(F32)
        s = jax.nn.sigmoid(gg)
        sl = gg * s
        a_ref[...] = (sl * uu).astype(BF16)
        dg = (da * uu * (s * (1.0 + gg * (1.0 - s)))).astype(BF16)
        du = (da * sl).astype(BF16)
        dg_ref[...] = dg
        du_ref[...] = du
        acc_ref[...] += _dot(dg, wg_ref[...]) + _dot(du, wu_ref[...])

        @pl.when(j == nj - 1)
        def _():
            hh = h_ref[...]
            r = _rms(hh)
            dxn = acc_ref[...]
            dgain_ref[...] += _col_sum(dxn * hh * r)
            dh_ref[...] = dho_ref[...] + _rms_bwd(hh, r, dxn * gain_ref[...])

    return _pallas(
        body, name, (T // tm, nj),
        [pl.BlockSpec((tm, D), lambda i, j: (i, 0)),
         pl.BlockSpec((tm, D), lambda i, j: (i, 0)),
         pl.BlockSpec((1, D), lambda i, j: (0, 0)),
         pl.BlockSpec((tm, tf), lambda i, j: (i, j)),
         pl.BlockSpec((tm, tf), lambda i, j: (i, j)),
         pl.BlockSpec((tf, D), lambda i, j: (j, 0)),
         pl.BlockSpec((tf, D), lambda i, j: (j, 0)),
         pl.BlockSpec((tf, D), lambda i, j: (j, 0))],
        [pl.BlockSpec((tm, D), lambda i, j: (i, 0)),
         pl.BlockSpec((1, D), lambda i, j: (0, 0)),
         pl.BlockSpec((tm, D), lambda i, j: (i, 0)),
         pl.BlockSpec((tm, tf), lambda i, j: (i, j)),
         pl.BlockSpec((tm, tf), lambda i, j: (i, j)),
         pl.BlockSpec((tm, tf), lambda i, j: (i, j))],
        [jax.ShapeDtypeStruct((T, D), F32), jax.ShapeDtypeStruct((1, D), F32),
         jax.ShapeDtypeStruct((T, D), BF16), jax.ShapeDtypeStruct((T, F), BF16),
         jax.ShapeDtypeStruct((T, F), BF16), jax.ShapeDtypeStruct((T, F), BF16)],
        [pltpu.VMEM((tm, D), F32)], ("arbitrary", "arbitrary"), (dho, h, gain, g, u, wg, wu, wd), comm)


def _xty(x, y, name, comm=None, col_shards=1):
    T, K = x.shape
    N = y.shape[1]
    tt = min(T, 1024)
    tk = K if K <= 1024 else (1408 if K % 1408 == 0 else K)
    tn = N if N <= 1024 else (1408 if N % 1408 == 0 else (1024 if N % 1024 == 0 else N))
    nt = T // tt
    ws = N // col_shards
    per = tn // ws if col_shards > 1 else 1
    assert col_shards == 1 or (tn % ws == 0 and ws % 128 == 0)

    def body(x_ref, y_ref, o_ref, acc_ref):
        t = pl.program_id(2)

        @pl.when(t == 0)
        def _():
            acc_ref[...] = jnp.zeros_like(acc_ref)

        acc_ref[...] += _dot_tn(x_ref[...], y_ref[...])

        @pl.when(t == nt - 1)
        def _():
            if col_shards == 1:
                o_ref[...] = acc_ref[...].astype(BF16)
            else:
                for s in range(per):
                    o_ref[s] = acc_ref[:, s * ws:(s + 1) * ws].astype(BF16)

    if col_shards == 1:
        out_spec = pl.BlockSpec((tk, tn), lambda k, n, t: (k, n))
        out_shape = jax.ShapeDtypeStruct((K, N), BF16)
    else:
        out_spec = pl.BlockSpec((per, tk, ws), lambda k, n, t: (n, k, 0))
        out_shape = jax.ShapeDtypeStruct((col_shards, K, ws), BF16)
    (out,), got = _pallas(
        body, name, (K // tk, N // tn, nt),
        [pl.BlockSpec((tt, tk), lambda k, n, t: (t, k)), pl.BlockSpec((tt, tn), lambda k, n, t: (t, n))],
        [out_spec], [out_shape], [pltpu.VMEM((tk, tn), F32)],
        ("parallel", "parallel", "arbitrary"), (x, y), comm)
    return out if comm is None else (out, got)


def _mixin_fwd(h, gain, w_in):
    T, D = h.shape
    nn, _, tn = w_in.shape
    N = nn * tn
    tm = min(T, 256)

    def body(h_ref, gain_ref, w_ref, z_ref, zb_ref, un_ref):
        hh = h_ref[...]
        un = (hh * _rms(hh) * gain_ref[...]).astype(BF16)
        un_ref[...] = un
        for s in range(nn):
            z = _dot(un, w_ref[s])
            z_ref[:, s * tn:(s + 1) * tn] = z
            zb_ref[:, s * tn:(s + 1) * tn] = z.astype(BF16)

    return pl.pallas_call(
        body, name="mixin_fwd", grid=(T // tm,),
        in_specs=[pl.BlockSpec((tm, D), lambda i: (i, 0)),
                  pl.BlockSpec((1, D), lambda i: (0, 0)),
                  pl.BlockSpec((nn, D, tn), lambda i: (0, 0, 0))],
        out_specs=[pl.BlockSpec((tm, N), lambda i: (i, 0)),
                   pl.BlockSpec((tm, N), lambda i: (i, 0)),
                   pl.BlockSpec((tm, D), lambda i: (i, 0))],
        out_shape=[jax.ShapeDtypeStruct((T, N), F32), jax.ShapeDtypeStruct((T, N), BF16),
                   jax.ShapeDtypeStruct((T, D), BF16)],
        compiler_params=_cparams(("parallel",), VMEM_LIMIT),
    )(h, gain, w_in)


def _mixin_bwd(dz, dh_res, h, gain, w_in):
    T, D = h.shape
    nn, _, tn = w_in.shape
    tm = min(T, 256)

    def body(dz_ref, dres_ref, h_ref, gain_ref, w_ref, dh_ref, dgain_ref):
        @pl.when(pl.program_id(0) == 0)
        def _():
            dgain_ref[...] = jnp.zeros_like(dgain_ref)

        dun = _dot_nt(dz_ref[:, 0:tn], w_ref[0])
        for s in range(1, nn):
            dun = dun + _dot_nt(dz_ref[:, s * tn:(s + 1) * tn], w_ref[s])
        hh = h_ref[...]
        r = _rms(hh)
        dgain_ref[...] += _col_sum(dun * hh * r)
        dh_ref[...] = dres_ref[...] + _rms_bwd(hh, r, dun * gain_ref[...])

    return pl.pallas_call(
        body, name="mixin_bwd", grid=(T // tm,),
        in_specs=[pl.BlockSpec((tm, nn * tn), lambda i: (i, 0)),
                  pl.BlockSpec((tm, D), lambda i: (i, 0)),
                  pl.BlockSpec((tm, D), lambda i: (i, 0)),
                  pl.BlockSpec((1, D), lambda i: (0, 0)),
                  pl.BlockSpec((nn, D, tn), lambda i: (0, 0, 0))],
        out_specs=[pl.BlockSpec((tm, D), lambda i: (i, 0)),
                   pl.BlockSpec((1, D), lambda i: (0, 0))],
        out_shape=[jax.ShapeDtypeStruct((T, D), F32), jax.ShapeDtypeStruct((1, D), F32)],
        compiler_params=_cparams(("arbitrary",), VMEM_LIMIT),
    )(dz, dh_res, h, gain, w_in)


def _loss_head(h, gain, target):
    T, D = h.shape
    tm = min(T, 1024)

    def body(h_ref, gain_ref, t_ref, loss_ref, dh_ref, dgain_ref):
        @pl.when(pl.program_id(0) == 0)
        def _():
            loss_ref[...] = jnp.zeros_like(loss_ref)
            dgain_ref[...] = jnp.zeros_like(dgain_ref)

        hh = h_ref[...]
        r = _rms(hh)
        e = hh * r * gain_ref[...] - t_ref[...]
        loss_ref[...] += (0.5 / D) * jnp.sum(e * e)
        dy = e * (1.0 / D)
        dgain_ref[...] += _col_sum(dy * hh * r)
        dh_ref[...] = _rms_bwd(hh, r, dy * gain_ref[...])

    return pl.pallas_call(
        body, name="loss_head", grid=(T // tm,),
        in_specs=[pl.BlockSpec((tm, D), lambda i: (i, 0)),
                  pl.BlockSpec((1, D), lambda i: (0, 0)),
                  pl.BlockSpec((tm, D), lambda i: (i, 0))],
        out_specs=[pl.BlockSpec((1, 128), lambda i: (0, 0)),
                   pl.BlockSpec((tm, D), lambda i: (i, 0)),
                   pl.BlockSpec((1, D), lambda i: (0, 0))],
        out_shape=[jax.ShapeDtypeStruct((1, 128), F32), jax.ShapeDtypeStruct((T, D), F32),
                   jax.ShapeDtypeStruct((1, D), F32)],
        compiler_params=_cparams(("arbitrary",), VMEM_LIMIT),
    )(h, gain, target)


def _adamw(parts, w, m, v, name):
    R, C = w.shape
    mult = 16 if parts.dtype == BF16 else 8
    tr = max(t for t in range(mult, min(R, 512) + 1, mult) if R % t == 0)
    c1 = 1.0 - ADAM_B1 ** ADAM_STEP
    c2 = 1.0 - ADAM_B2 ** ADAM_STEP

    def body(p_ref, w_ref, m_ref, v_ref, g_ref, d_ref, nm_ref, nv_ref):
        g = p_ref[0].astype(F32)
        for k in range(1, N_DEV):
            g = g + p_ref[k].astype(F32)
        mm = ADAM_B1 * m_ref[...] + (1.0 - ADAM_B1) * g
        vv = ADAM_B2 * v_ref[...] + (1.0 - ADAM_B2) * (g * g)
        g_ref[...] = g
        nm_ref[...] = mm
        nv_ref[...] = vv
        d_ref[...] = -ADAM_LR * ((mm / c1) / (jnp.sqrt(vv / c2) + ADAM_EPS) + ADAM_WD * w_ref[...])

    spec = pl.BlockSpec((tr, C), lambda i: (i, 0))
    return pl.pallas_call(
        body, name=name, grid=(R // tr,),
        in_specs=[pl.BlockSpec((N_DEV, tr, C), lambda i: (0, i, 0)), spec, spec, spec],
        out_specs=[spec, spec, spec, spec],
        out_shape=[jax.ShapeDtypeStruct((R, C), F32)] * 4,
        compiler_params=_cparams(("parallel",), VMEM_LIMIT),
    )(parts, w, m, v)


S5_NS = 256
S5_NH = 2
S5_NCB = 4
S5_RC = 512
S5_NQ = 4
S5_GROUP = 2


def _disc_math(a_re, a_im, log_dt, bt_re, bt_im):
    dt = jnp.exp(log_dt)
    zr, zi = a_re * dt, a_im * dt
    mag = jnp.exp(zr)
    lb_re, lb_im = mag * jnp.cos(zi), mag * jnp.sin(zi)
    den = a_re * a_re + a_im * a_im
    nr, ni = lb_re - 1.0, lb_im
    f_re = (nr * a_re + ni * a_im) / den
    f_im = (ni * a_re - nr * a_im) / den
    bb_re = f_re[:, None, :] * bt_re - f_im[:, None, :] * bt_im
    bb_im = f_re[:, None, :] * bt_im + f_im[:, None, :] * bt_re
    return lb_re, lb_im, bb_re, bb_im


def _disc_fwd(a_re, a_im, log_dt, bt_re, bt_im, chain_len, name):
    G, P = a_re.shape
    C = bt_re.shape[1]
    n_sq = int(round(math.log2(chain_len)))
    assert 2 ** n_sq == chain_len

    def body(a_re_ref, a_im_ref, ldt_ref, br_ref, bi_ref, lr_ref, li_ref, sr_ref, si_ref, bbr_ref, bbi_ref):
        lr, li, bbr, bbi = _disc_math(a_re_ref[...], a_im_ref[...], ldt_ref[...], br_ref[...], bi_ref[...])
        lr_ref[...] = lr
        li_ref[...] = li
        bbr_ref[...] = bbr
        bbi_ref[...] = bbi
        pr, pi = lr, li
        for _ in range(n_sq):
            pr, pi = pr * pr - pi * pi, 2.0 * pr * pi
        sr_ref[...] = pr
        si_ref[...] = pi

    s2 = jax.ShapeDtypeStruct((G, P), F32)
    s3 = jax.ShapeDtypeStruct((G, C, P), F32)
    return pl.pallas_call(body, name=name, out_shape=[s2, s2, s2, s2, s3, s3])(a_re, a_im, log_dt, bt_re, bt_im)


def _disc_bwd(a_re, a_im, log_dt, bt_re, bt_im, d_lr, d_li, d_bbr, d_bbi, name):
    G, P = a_re.shape
    C = bt_re.shape[1]

    def body(a_re_ref, a_im_ref, ldt_ref, br_ref, bi_ref, c1, c2, c3, c4, o1, o2, o3, o4, o5):
        _, vjp = jax.vjp(_disc_math, a_re_ref[...], a_im_ref[...], ldt_ref[...], br_ref[...], bi_ref[...])
        o1[...], o2[...], o3[...], o4[...], o5[...] = vjp((c1[...], c2[...], c3[...], c4[...]))

    s2 = jax.ShapeDtypeStruct((G, P), F32)
    s3 = jax.ShapeDtypeStruct((G, C, P), F32)
    return pl.pallas_call(body, name=name, out_shape=[s2, s2, jax.ShapeDtypeStruct((G, 1), F32), s3, s3])(
        a_re, a_im, log_dt, bt_re, bt_im, d_lr, d_li, d_bbr, d_bbi)


def _row_block(ib):
    return pl.ds(pl.multiple_of(ib * SCAN_LANES, SCAN_LANES), SCAN_LANES)


def _chain_block(j, i, ascending, n_blocks):
    at = j * (n_blocks // S5_NQ) + i
    return _row_block(jnp.where(ascending, at, n_blocks - 1 - at))


def _unrolled_loop(n, unroll, body, carry):
    trips = n // unroll
    carry = lax.fori_loop(
        0, trips, lambda t, c: functools.reduce(lambda cc, u: body(t * unroll + u, cc), range(unroll), c), carry)
    for i in range(trips * unroll, n):
        carry = body(i, carry)
    return carry


def _cmul_add(lr, li, sr, si, xr, xi):
    return lr * sr - li * si + xr, lr * si + li * sr + xi


def _scan(xr_ref, xi_ref, lr, li, init, ascending, n_blocks, store):
    steps = n_blocks // S5_NQ
    if not store:
        def step(i, carry):
            blocks = [_chain_block(j, i, ascending, n_blocks) for j in range(S5_NQ)]
            return tuple(_cmul_add(lr, li, sr, si, xr_ref[rows, :], xi_ref[rows, :])
                         for (sr, si), rows in zip(carry, blocks))

        return _unrolled_loop(steps, 4, step, init)

    group = S5_GROUP
    assert steps % group == 0

    def trip(t, carry):
        blocks = [[_chain_block(j, t * group + u, ascending, n_blocks) for j in range(S5_NQ)] for u in range(group)]
        xs = [[(xr_ref[rows, :], xi_ref[rows, :]) for rows in row] for row in blocks]
        states = list(carry)
        done = []
        for u in range(group):
            states = [_cmul_add(lr, li, sr, si, xr, xi) for (sr, si), (xr, xi) in zip(states, xs[u])]
            done.append(states)
        for u in range(group):
            for rows, (nr, ni) in zip(blocks[u], done[u]):
                xr_ref[rows, :] = nr
                xi_ref[rows, :] = ni
        return tuple(states)

    return lax.fori_loop(0, steps // group, trip, init)


def _segment_starts(w, lsr, lsi, ascending):
    shape = w[0][0].shape
    row = lax.broadcasted_iota(jnp.int32, shape, 0)
    keep = row != jnp.where(ascending, 0, SCAN_LANES - 1)

    def shift(t):
        t = jnp.where(ascending, pltpu.roll(t, 1, 0), pltpu.roll(t, SCAN_LANES - 1, 0))
        return jnp.where(keep, t, 0.0)

    zero = jnp.zeros(shape, F32)
    c = [(zero, zero)] * S5_NQ
    for _ in range(SCAN_LANES):
        tr, ti = _cmul_add(lsr, lsi, *c[-1], *w[-1])
        c[0] = (shift(tr), shift(ti))
        for j in range(1, S5_NQ):
            c[j] = _cmul_add(lsr, lsi, *c[j - 1], *w[j - 1])
    return tuple(c)


def _first_pass(xr_ref, xi_ref, lam_ref, ascending, n_blocks, conj):
    shape = (SCAN_LANES, xr_ref.shape[1])
    sign = -1.0 if conj else 1.0
    lr = jnp.broadcast_to(lam_ref[0:1, :], shape)
    li = sign * jnp.broadcast_to(lam_ref[1:2, :], shape)
    lsr = jnp.broadcast_to(lam_ref[2:3, :], shape)
    lsi = sign * jnp.broadcast_to(lam_ref[3:4, :], shape)
    zero = jnp.zeros(shape, F32)
    w = _scan(xr_ref, xi_ref, lr, li, ((zero, zero),) * S5_NQ, ascending, n_blocks, store=False)
    return _segment_starts(w, lsr, lsi, ascending), lr, li


def _s5_specs(T):
    NS = S5_NS
    tok = pl.BlockSpec((T, 128), lambda c, d, h: (0, c))
    b_spec = pl.BlockSpec((None, None, None, 128, NS), lambda c, d, h: (d, c, h, 0, 0))
    c_spec = pl.BlockSpec((None, None, None, NS, 128), lambda c, d, h: (d, c, h, 0, 0))
    lam_spec = pl.BlockSpec((None, None, None, 4, NS), lambda c, d, h: (d, c, h, 0, 0))
    return tok, b_spec, c_spec, lam_spec


def _s5_fwd(zp, bre, bim, lam, cre, cimn, comm=None):
    T = zp.shape[0]
    NS = S5_NS
    nb = T // SCAN_LANES
    rc = min(S5_RC, T)
    tok, b_spec, c_spec, lam_spec = _s5_specs(T)

    def body(zp_ref, bre_ref, bim_ref, lam_ref, cre_ref, cim_ref, y_ref, xr_ref, xi_ref):
        d = pl.program_id(1)
        ascending = d == 0

        @pl.when((d == 0) & (pl.program_id(2) == 0))
        def _():
            y_ref[...] = jnp.zeros_like(y_ref)

        def proj(c, _):
            rows = pl.ds(pl.multiple_of(c * rc, rc), rc)
            zz = zp_ref[rows, :]
            xr_ref[rows, :] = _dot(zz, bre_ref[...])
            xi_ref[rows, :] = _dot(zz, bim_ref[...])
            return 0

        lax.fori_loop(0, T // rc, proj, 0)
        starts, lr, li = _first_pass(xr_ref, xi_ref, lam_ref, ascending, nb, conj=False)
        _scan(xr_ref, xi_ref, lr, li, starts, ascending, nb, store=True)

        def outp(c, _):
            rows = pl.ds(pl.multiple_of(c * rc, rc), rc)
            y_ref[rows, :] += (_dot(xr_ref[rows, :].astype(BF16), cre_ref[...])
                               + _dot(xi_ref[rows, :].astype(BF16), cim_ref[...]))
            return 0

        lax.fori_loop(0, T // rc, outp, 0)

    return _pallas(
        body, "s5_fwd", (S5_NCB, 2, S5_NH),
        [tok, b_spec, b_spec, lam_spec, c_spec, c_spec], [tok],
        [jax.ShapeDtypeStruct((T, SSM_WIDTH), F32)],
        [pltpu.VMEM((T, NS), F32), pltpu.VMEM((T, NS), F32)],
        ("parallel", "arbitrary", "arbitrary"), (zp, bre, bim, lam, cre, cimn), comm)


def _s5_bwd(zp, dyp, bre, bim, lam, cre, cimn, comm=None):
    T = zp.shape[0]
    NS, NH = S5_NS, S5_NH
    nb = T // SCAN_LANES
    rc = min(S5_RC, T)
    tok, b_spec, c_spec, lam_spec = _s5_specs(T)
    dlam_spec = pl.BlockSpec((None, None, None, 2, NS), lambda c, d, h: (d, c, h, 0, 0))

    def body(zp_ref, dyp_ref, bre_ref, bim_ref, lam_ref, cre_ref, cim_ref,
             dzp_ref, dbre_ref, dbim_ref, dlam_ref, dcre_ref, dcim_ref,
             sr_ref, si_ref, gr_ref, gi_ref):
        d = pl.program_id(1)
        ascending = d == 0
        g_ascending = d != 0

        @pl.when((d == 0) & (pl.program_id(2) == 0))
        def _():
            dzp_ref[...] = jnp.zeros_like(dzp_ref)

        dcre_ref[...] = jnp.zeros_like(dcre_ref)
        dcim_ref[...] = jnp.zeros_like(dcim_ref)
        dbre_ref[...] = jnp.zeros_like(dbre_ref)
        dbim_ref[...] = jnp.zeros_like(dbim_ref)

        def proj(c, _):
            rows = pl.ds(pl.multiple_of(c * rc, rc), rc)
            zz = zp_ref[rows, :]
            sr_ref[rows, :] = _dot(zz, bre_ref[...])
            si_ref[rows, :] = _dot(zz, bim_ref[...])
            dy = dyp_ref[rows, :]
            gr_ref[rows, :] = _dot_nt(dy, cre_ref[...])
            gi_ref[rows, :] = _dot_nt(dy, cim_ref[...])
            return 0

        lax.fori_loop(0, T // rc, proj, 0)
        s_starts, lr, li = _first_pass(sr_ref, si_ref, lam_ref, ascending, nb, conj=False)
        _scan(sr_ref, si_ref, lr, li, s_starts, ascending, nb, store=True)
        g_starts, lr, lic = _first_pass(gr_ref, gi_ref, lam_ref, g_ascending, nb, conj=True)

        steps = nb // S5_NQ
        group = S5_GROUP
        assert steps % group == 0

        def gtrip(t, carry, last):
            g, (ar, ai) = carry
            first = t * group
            blocks = [[_chain_block(j, first + u, g_ascending, nb) for j in range(S5_NQ)] for u in range(group)]
            direct = [[(gr_ref[rows, :], gi_ref[rows, :]) for rows in row] for row in blocks]
            done = []
            for u in range(group):
                new = []
                for j, ((g_r, g_i), (d_r, d_i)) in enumerate(zip(g, direct[u])):
                    n_r, n_i = _cmul_add(lr, lic, g_r, g_i, d_r, d_i)
                    if last and u == group - 1:
                        s_r, s_i = s_starts[S5_NQ - 1 - j]
                    else:
                        prev = _chain_block(j, first + u + 1, g_ascending, nb)
                        s_r, s_i = sr_ref[prev, :], si_ref[prev, :]
                    ar = ar + n_r * s_r + n_i * s_i
                    ai = ai + n_i * s_r - n_r * s_i
                    new.append((n_r, n_i))
                g = new
                done.append(new)
            for u in range(group):
                for rows, (n_r, n_i) in zip(blocks[u], done[u]):
                    gr_ref[rows, :] = n_r
                    gi_ref[rows, :] = n_i
            return tuple(g), (ar, ai)

        zero = jnp.zeros((SCAN_LANES, NS), F32)
        carry = lax.fori_loop(0, steps // group - 1, lambda t, c: gtrip(t, c, False), (g_starts, (zero, zero)))
        _, (ar, ai) = gtrip(steps // group - 1, carry, True)
        dlam_ref[0:1, :] = _col_sum(ar)
        dlam_ref[1:2, :] = _col_sum(ai)

        def grads(c, _):
            rows = pl.ds(pl.multiple_of(c * rc, rc), rc)
            zz = zp_ref[rows, :]
            dy = dyp_ref[rows, :]
            g_rb = gr_ref[rows, :].astype(BF16)
            g_ib = gi_ref[rows, :].astype(BF16)
            dcre_ref[...] += _dot_tn(sr_ref[rows, :].astype(BF16), dy)
            dcim_ref[...] += _dot_tn(si_ref[rows, :].astype(BF16), dy)
            dbre_ref[...] += _dot_tn(zz, g_rb)
            dbim_ref[...] += _dot_tn(zz, g_ib)
            dzp_ref[rows, :] += _dot_nt(g_rb, bre_ref[...]) + _dot_nt(g_ib, bim_ref[...])
            return 0

        lax.fori_loop(0, T // rc, grads, 0)

    f32 = lambda *s: jax.ShapeDtypeStruct(s, F32)
    return _pallas(
        body, "s5_bwd", (S5_NCB, 2, S5_NH),
        [tok, tok, b_spec, b_spec, lam_spec, c_spec, c_spec],
        [tok, b_spec, b_spec, dlam_spec, c_spec, c_spec],
        [f32(T, SSM_WIDTH), f32(2, S5_NCB, NH, 128, NS), f32(2, S5_NCB, NH, 128, NS),
         f32(2, S5_NCB, NH, 2, NS), f32(2, S5_NCB, NH, NS, 128), f32(2, S5_NCB, NH, NS, 128)],
        [pltpu.VMEM((T, NS), F32)] * 4,
        ("parallel", "arbitrary", "arbitrary"), (zp, dyp, bre, bim, lam, cre, cimn), comm)


def _s5_delta():
    d = np.zeros((S5_NH, 8, 8 // S5_NH), np.float32)
    for h in range(S5_NH):
        for go in range(8 // S5_NH):
            d[h, h * (8 // S5_NH) + go, go] = 1.0
    return d


def _s5_pack_b(bbt):
    gh = 8 // S5_NH
    b5 = bbt.reshape(S5_NCB, S5_NH, gh, SSM_GROUP, SSM_STATE).transpose(0, 1, 3, 2, 4)
    m = b5[:, :, None] * _s5_delta()[None, :, :, None, :, None]
    return m.reshape(S5_NCB, S5_NH, 128, S5_NS)


def _s5_unpack_b(dm):
    gh = 8 // S5_NH
    d6 = dm.reshape(S5_NCB, S5_NH, 8, SSM_GROUP, gh, SSM_STATE)
    b5 = jnp.sum(d6 * _s5_delta()[None, :, :, None, :, None], axis=2)
    return b5.transpose(0, 1, 3, 2, 4).reshape(SSM_GROUPS, SSM_GROUP, SSM_STATE)


def _s5_pack_c(c):
    gh = 8 // S5_NH
    c5 = c.reshape(S5_NCB, S5_NH, gh, SSM_GROUP, SSM_STATE).transpose(0, 1, 2, 4, 3)
    m = c5[:, :, :, :, None, :] * _s5_delta().transpose(0, 2, 1)[None, :, :, None, :, None]
    return m.reshape(S5_NCB, S5_NH, S5_NS, 128)


def _s5_unpack_c(dm):
    gh = 8 // S5_NH
    d6 = dm.reshape(S5_NCB, S5_NH, gh, SSM_STATE, 8, SSM_GROUP)
    c5 = jnp.sum(d6 * _s5_delta().transpose(0, 2, 1)[None, :, :, None, :, None], axis=4)
    return c5.transpose(0, 1, 2, 4, 3).reshape(SSM_GROUPS, SSM_GROUP, SSM_STATE)


def _s5_pack_lam(x):
    return x.reshape(S5_NCB, S5_NH, S5_NS)


def _permute_rows(x):
    T = x.shape[0]
    return x.reshape(SCAN_LANES, T // SCAN_LANES, -1).transpose(1, 0, 2).reshape(T, -1)


def _unpermute_rows(x):
    T = x.shape[0]
    return x.reshape(T // SCAN_LANES, SCAN_LANES, -1).transpose(1, 0, 2).reshape(T, -1)


ATT_TB = ATT_ROWS * GRID_W
ATT_KB = 3 * ATT_TB


def _att_valid(i, n_rows):
    qi, kj = np.meshgrid(np.arange(ATT_TB), np.arange(ATT_KB), indexing="ij")
    r = i * ATT_ROWS + qi // GRID_W
    c = qi % GRID_W
    rk = (i - 1) * ATT_ROWS + kj // GRID_W
    x = kj % GRID_W
    rs = np.clip(r - WIN_H // 2, 0, n_rows - WIN_H)
    cs = np.clip(c - WIN_W // 2, 0, GRID_W - WIN_W)
    return (rk >= rs) & (rk < rs + WIN_H) & (x >= cs) & (x < cs + WIN_W)


def _att_masked_tables(table, n_rows):
    n = n_rows // ATT_ROWS
    assert n >= 3
    masks = np.stack([_att_valid(i, n_rows) for i in (0, 1, n - 1)])
    return jnp.where(masks[:, None], table[None], NEG_INF)


def _att_variant(i, n):
    return jnp.where(i == 0, 0, jnp.where(i >= n - 1, 2, 1))


def _att_exp(qh, kh, bias):
    s = _dot_nt(qh, kh) + bias
    return jnp.exp(s - jnp.max(s, axis=1, keepdims=True)).astype(BF16)


def _att_values_and_ones(vh):
    return jnp.concatenate([vh, jnp.ones_like(vh)], axis=1)


def _att_specs(n, col):
    last = n - 1
    cur = lambda i: (jnp.minimum(i, last), col)
    prv = lambda i: (jnp.maximum(jnp.minimum(i, last) - 1, 0), col)
    nxt = lambda i: (jnp.minimum(i + 1, last), col)
    blk = lambda f: pl.BlockSpec((ATT_TB, ATT_WIDTH), f)
    return blk(cur), blk(prv), blk(nxt)


def _att_fwd(zb, biasv):
    T = zb.shape[0]
    W = ATT_WIDTH
    n = T // ATT_TB
    n_rows = T // GRID_W
    cur = _att_specs(n, 0)[0]
    q_cur = _att_specs(n, 1)[0]
    k_cur, k_prv, k_nxt = _att_specs(n, 2)
    v_cur, v_prv, v_nxt = _att_specs(n, 3)

    def body(q_ref, kp_ref, kc_ref, kn_ref, vp_ref, vc_ref, vn_ref, b_ref, y_ref):
        qs = q_ref[...] * 0.125
        kb = jnp.concatenate([kp_ref[...], kc_ref[...], kn_ref[...]], axis=0)
        vb = jnp.concatenate([vp_ref[...], vc_ref[...], vn_ref[...]], axis=0)
        outs = []
        for h in range(ATT_HEADS):
            hs = slice(h * ATT_HEAD_DIM, (h + 1) * ATT_HEAD_DIM)
            e = _att_exp(qs[:, hs], kb[:, hs], b_ref[h])
            ov = _dot(e, _att_values_and_ones(vb[:, hs]))
            outs.append(ov[:, :ATT_HEAD_DIM] * (1.0 / ov[:, ATT_HEAD_DIM:ATT_HEAD_DIM + 1]))
        y_ref[...] = jnp.concatenate(outs, axis=1).astype(BF16)

    return pl.pallas_call(
        body, name="att_fwd", grid=(n,),
        in_specs=[q_cur, k_prv, k_cur, k_nxt, v_prv, v_cur, v_nxt,
                  pl.BlockSpec((None, ATT_HEADS, ATT_TB, ATT_KB), lambda i: (_att_variant(i, n), 0, 0, 0))],
        out_specs=cur,
        out_shape=jax.ShapeDtypeStruct((T, W), BF16),
        compiler_params=_cparams(("parallel",), VMEM_LIMIT),
    )(zb, zb, zb, zb, zb, zb, zb, biasv)


def _att_bwd(zb, y, do, biasv, comm=None):
    T = zb.shape[0]
    W = ATT_WIDTH
    n = T // ATT_TB
    n_rows = T // GRID_W
    cur = _att_specs(n, 0)[0]
    q_cur = _att_specs(n, 1)[0]
    k_cur, k_prv, k_nxt = _att_specs(n, 2)
    v_cur, v_prv, v_nxt = _att_specs(n, 3)
    done = pl.BlockSpec((ATT_TB, W), lambda i: (jnp.maximum(i - 1, 0), 0))
    bias_spec = pl.BlockSpec((None, ATT_HEADS, ATT_TB, ATT_KB), lambda i: (_att_variant(i, n), 0, 0, 0))

    def body(q_ref, y_ref, do_ref, kp_ref, kc_ref, kn_ref, vp_ref, vc_ref, vn_ref, b_ref,
             dq_ref, dk_ref, dv_ref, db_ref, acck_ref, accv_ref):
        i = pl.program_id(0)

        @pl.when(i == 0)
        def _():
            db_ref[...] = jnp.zeros_like(db_ref)
            acck_ref[...] = jnp.zeros_like(acck_ref)
            accv_ref[...] = jnp.zeros_like(accv_ref)

        @pl.when((i > 0) & (i < n))
        def _():
            slot = lax.rem(i + 1, 3)
            acck_ref[slot] = jnp.zeros((ATT_TB, W), F32)
            accv_ref[slot] = jnp.zeros((ATT_TB, W), F32)

        @pl.when(i < n)
        def _():
            qs = q_ref[...] * 0.125
            dob = do_ref[...]
            dy = dob.astype(F32) * y_ref[...].astype(F32)
            kb = jnp.concatenate([kp_ref[...], kc_ref[...], kn_ref[...]], axis=0)
            vb = jnp.concatenate([vp_ref[...], vc_ref[...], vn_ref[...]], axis=0)
            dqs, dks, dvs = [], [], []
            for h in range(ATT_HEADS):
                hs = slice(h * ATT_HEAD_DIM, (h + 1) * ATT_HEAD_DIM)
                qh, kh, vh, doh = qs[:, hs], kb[:, hs], vb[:, hs], dob[:, hs]
                e = _att_exp(qh, kh, b_ref[h])
                denom = _dot(e, jnp.ones((ATT_KB, 128), BF16))[:, :1]
                p = e.astype(F32) * (1.0 / denom)
                dp = _dot_nt(doh, vh)
                ds = p * (dp - jnp.sum(dy[:, hs], axis=1, keepdims=True))
                db_ref[h] += ds
                dsb = ds.astype(BF16)
                dqs.append(_dot(dsb, kh) * 0.125)
                dks.append(_dot_tn(dsb, qh))
                dvs.append(_dot_tn(p.astype(BF16), doh))
            dq_ref[...] = jnp.concatenate(dqs, axis=1).astype(BF16)
            dk_all = jnp.concatenate(dks, axis=1)
            dv_all = jnp.concatenate(dvs, axis=1)
            for b in range(3):
                slot = lax.rem(i + 2 + b, 3)
                rows = slice(b * ATT_TB, (b + 1) * ATT_TB)
                acck_ref[slot] += dk_all[rows]
                accv_ref[slot] += dv_all[rows]

        slot = lax.rem(i + 2, 3)
        dk_ref[...] = acck_ref[slot].astype(BF16)
        dv_ref[...] = accv_ref[slot].astype(BF16)

    return _pallas(
        body, "att_bwd", (n + 1,),
        [q_cur, cur, cur, k_prv, k_cur, k_nxt, v_prv, v_cur, v_nxt, bias_spec],
        [cur, done, done, pl.BlockSpec((ATT_HEADS, ATT_TB, ATT_KB), lambda i: (0, 0, 0))],
        [jax.ShapeDtypeStruct((T, W), BF16)] * 3 + [jax.ShapeDtypeStruct((ATT_HEADS, ATT_TB, ATT_KB), F32)],
        [pltpu.VMEM((3, ATT_TB, W), F32), pltpu.VMEM((3, ATT_TB, W), F32)],
        ("arbitrary",), (zb, y, do, zb, zb, zb, zb, zb, zb, biasv), comm)


def _att_selectors():
    rsel = np.zeros((ATT_ROWS, 3 * ATT_ROWS, 2 * WIN_H - 1), np.float32)
    for a in range(ATT_ROWS):
        for b in range(3 * ATT_ROWS):
            rsel[a, b, b - a - ATT_ROWS + WIN_H - 1] = 1.0
    csel = np.zeros((GRID_W, GRID_W, 2 * WIN_W - 1), np.float32)
    for c in range(GRID_W):
        for x in range(GRID_W):
            csel[c, x, min(max(x - c, -(WIN_W - 1)), WIN_W - 1) + WIN_W - 1] = 1.0
    return rsel, csel


def _att_bias_table(rpb):
    rsel, csel = _att_selectors()
    hi = lax.Precision.HIGHEST
    t = jnp.einsum('hrd,abr->habd', rpb, rsel, precision=hi)
    t = jnp.einsum('habd,cxd->hacbx', t, csel, precision=hi)
    return t.reshape(ATT_HEADS, ATT_TB, ATT_KB)


def _att_bias_table_t(dtable):
    rsel, csel = _att_selectors()
    hi = lax.Precision.HIGHEST
    t = dtable.reshape(ATT_HEADS, ATT_ROWS, GRID_W, 3 * ATT_ROWS, GRID_W)
    t = jnp.einsum('hacbx,cxd->habd', t, csel, precision=hi)
    return jnp.einsum('habd,abr->hrd', t, rsel, precision=hi)


GELU_K = math.sqrt(2.0 / math.pi)
GELU_C = 0.044715
MERGE_TM = 256


def _gelu(x):
    return 0.5 * x * (1.0 + jnp.tanh(GELU_K * (x + GELU_C * x * x * x)))


def _gelu_grad(x):
    t = jnp.tanh(GELU_K * (x + GELU_C * x * x * x))
    return 0.5 * (1.0 + t) + 0.5 * x * (1.0 - t * t) * GELU_K * (1.0 + 3.0 * GELU_C * x * x)


def _merge_forward(ypre, zs, gs, ga, ya, ssm_d, w_glu, b_glu, w_bs, w_ba):
    ys = ypre + ssm_d * zs
    yg = _gelu(ys)
    sg = jax.nn.sigmoid(_dot(yg.astype(BF16), w_glu) + b_glu)
    y2 = yg * sg
    bs = _dot(y2.astype(BF16), w_bs)
    ba = _dot(ya, w_ba)
    s1 = jax.nn.sigmoid(gs)
    s2 = jax.nn.sigmoid(ga)
    merged = s1 * bs + s2 * ba
    return ys, yg, sg, y2, bs, ba, s1, s2, merged


def _merge_in_specs(D, W, tm):
    tok = lambda w, c: pl.BlockSpec((tm, w), lambda i: (i, c))
    full = lambda r, c: pl.BlockSpec((r, c), lambda i: (0, 0))
    z_specs = [tok(W, 0), tok(D, 4 * W // D), tok(D, 4 * W // D + 1)]
    w_specs = [full(1, W), full(W, W), full(1, W), full(W, D), full(W, D), full(D, D)]
    return tok, z_specs, w_specs


def _merge_fwd(ypre, z, ya, h1, ssm_d, w_glu, b_glu, w_bs, w_ba, w_out):
    T, D = h1.shape
    W = ypre.shape[1]
    tm = min(T, MERGE_TM)
    tok, z_specs, w_specs = _merge_in_specs(D, W, tm)

    def body(ypre_ref, zs_ref, gs_ref, ga_ref, ya_ref, h1_ref, d_ref, wglu_ref, bglu_ref, wbs_ref, wba_ref, wout_ref,
             h2_ref):
        merged = _merge_forward(ypre_ref[...], zs_ref[...], gs_ref[...], ga_ref[...], ya_ref[...], d_ref[...],
                                wglu_ref[...], bglu_ref[...], wbs_ref[...], wba_ref[...])[-1]
        h2_ref[...] = h1_ref[...] + _dot(merged.astype(BF16), wout_ref[...])

    return pl.pallas_call(
        body, name="merge_fwd", grid=(T // tm,),
        in_specs=[tok(W, 0)] + z_specs + [tok(W, 0), tok(D, 0)] + w_specs,
        out_specs=tok(D, 0),
        out_shape=jax.ShapeDtypeStruct((T, D), F32),
        compiler_params=_cparams(("parallel",), VMEM_LIMIT),
    )(ypre, z, z, z, ya, h1, ssm_d, w_glu, b_glu, w_bs, w_ba, w_out)


def _merge_bwd(dh2, ypre, z, ya, ssm_d, w_glu, b_glu, w_bs, w_ba, w_out):
    T, D = dh2.shape
    W = ypre.shape[1]
    tm = min(T, MERGE_TM)
    tok, z_specs, w_specs = _merge_in_specs(D, W, tm)

    def body(dh2_ref, ypre_ref, zs_ref, gs_ref, ga_ref, ya_ref, d_ref, wglu_ref, bglu_ref, wbs_ref, wba_ref, wout_ref,
             dypre_ref, dzs_ref, dgs_ref, dga_ref, dya_ref, dd_ref, dwglu_ref, dbglu_ref, dwbs_ref, dwba_ref, dwout_ref):
        @pl.when(pl.program_id(0) == 0)
        def _():
            for r in (dd_ref, dwglu_ref, dbglu_ref, dwbs_ref, dwba_ref, dwout_ref):
                r[...] = jnp.zeros_like(r)

        zs = zs_ref[...]
        ya = ya_ref[...]
        ys, yg, sg, y2, bs, ba, s1, s2, merged = _merge_forward(
            ypre_ref[...], zs, gs_ref[...], ga_ref[...], ya, d_ref[...],
            wglu_ref[...], bglu_ref[...], wbs_ref[...], wba_ref[...])
        dh2b = dh2_ref[...].astype(BF16)
        dmerged = _dot_nt(dh2b, wout_ref[...])
        dwout_ref[...] += _dot_tn(merged.astype(BF16), dh2b)
        dbs = (dmerged * s1).astype(BF16)
        dba = (dmerged * s2).astype(BF16)
        dgs_ref[...] = (dmerged * bs * s1 * (1.0 - s1)).astype(BF16)
        dga_ref[...] = (dmerged * ba * s2 * (1.0 - s2)).astype(BF16)
        dwbs_ref[...] += _dot_tn(y2.astype(BF16), dbs)
        dwba_ref[...] += _dot_tn(ya, dba)
        dya_ref[...] = _dot_nt(dba, wba_ref[...]).astype(BF16)
        dy2 = _dot_nt(dbs, wbs_ref[...])
        dvv = dy2 * yg * sg * (1.0 - sg)
        dvvb = dvv.astype(BF16)
        dyg = dy2 * sg + _dot_nt(dvvb, wglu_ref[...])
        dwglu_ref[...] += _dot_tn(yg.astype(BF16), dvvb)
        dbglu_ref[...] += _col_sum(dvv)
        dys = dyg * _gelu_grad(ys)
        dd_ref[...] += _col_sum(dys * zs)
        dzs_ref[...] = dys * d_ref[...]
        dypre_ref[...] = dys.astype(BF16)

    f32 = lambda *s: jax.ShapeDtypeStruct(s, F32)
    b16 = lambda *s: jax.ShapeDtypeStruct(s, BF16)
    return pl.pallas_call(
        body, name="merge_bwd", grid=(T // tm,),
        in_specs=[tok(D, 0), tok(W, 0)] + z_specs + [tok(W, 0)] + w_specs,
        out_specs=[tok(W, 0), tok(W, 0), tok(D, 0), tok(D, 0), tok(W, 0)] + w_specs,
        out_shape=[b16(T, W), f32(T, W), b16(T, D), b16(T, D), b16(T, W),
                   f32(1, W), f32(W, W), f32(1, W), f32(W, D), f32(W, D), f32(D, D)],
        compiler_params=_cparams(("arbitrary",), VMEM_LIMIT),
    )(dh2, ypre, z, z, z, ya, ssm_d, w_glu, b_glu, w_bs, w_ba, w_out)


def _cast_shards(weights):
    def body(*refs):
        n = len(refs) // 2
        for src, dst in zip(refs[:n], refs[n:]):
            dst[...] = src[0].astype(BF16)

    return pl.pallas_call(
        body, name="cast_shards",
        out_shape=[jax.ShapeDtypeStruct(w.shape[1:], BF16) for w in weights],
        compiler_params=_cparams(None, VMEM_LIMIT))(*weights)


def _gather_two_level(shards, name):
    n = len(shards)

    def body(*refs):
        x_refs, out_refs = refs[:n], refs[n:2 * n]
        send_sems, recv_sems, local_sems = refs[2 * n:]
        x, y, c = _my_place()
        me, sibling = (x, y, c), (x, y, 1 - c)
        chips = [(1 - x, y), (x, 1 - y), (1 - x, 1 - y)]

        def copy(a, k, block, to, own=False):
            slot = out_refs[a].at[_flat(*block)]
            return pltpu.make_async_remote_copy(
                src_ref=x_refs[a] if own else slot, dst_ref=slot,
                send_sem=send_sems.at[7 * a + k], recv_sem=recv_sems.at[7 * a + k],
                device_id=to, device_id_type=MESH_ID)

        sent, local = [], []
        for a in range(n):
            local.append(pltpu.make_async_copy(x_refs[a], out_refs[a].at[_flat(*me)], local_sems.at[a]))
            local[-1].start()
            sent.append(copy(a, 0, me, sibling, own=True))
            sent += [copy(a, 1 + j, me, (*chip, c), own=True) for j, chip in enumerate(chips)]
        for cp in sent:
            cp.start()
        for a in range(n):
            for j, chip in enumerate(chips):
                copy(a, 1 + j, (*chip, c), me).wait_recv()
                sent.append(copy(a, 4 + j, (*chip, c), sibling))
                sent[-1].start()
        for a in range(n):
            copy(a, 0, sibling, me).wait_recv()
            for j, chip in enumerate(chips):
                copy(a, 4 + j, (*chip, 1 - c), me).wait_recv()
        for cp in sent:
            cp.wait_send()
        for cp in local:
            cp.wait()

    return pl.pallas_call(
        body, name=name, in_specs=[_HBM] * n, out_specs=[_HBM] * n,
        out_shape=[jax.ShapeDtypeStruct((N_DEV,) + s.shape, s.dtype) for s in shards],
        scratch_shapes=[pltpu.SemaphoreType.DMA((7 * n,)), pltpu.SemaphoreType.DMA((7 * n,)),
                        pltpu.SemaphoreType.DMA((n,))],
    )(*shards)


PACK_COLS = 1024
BIG = (("ffn1_w_gate", 1), ("ffn1_w_up", 1), ("ffn1_w_down", 0), ("w_in", 1), ("ssm_w_glu", 0),
       ("w_branch_ssm", 1), ("w_branch_att", 1), ("w_out", 0),
       ("ffn2_w_gate", 1), ("ffn2_w_up", 1), ("ffn2_w_down", 0))
BIG_AXIS = dict(BIG)
TRANSPOSED = ("ffn1_w_gate", "ffn1_w_up", "ffn2_w_gate", "ffn2_w_up")
SSM_DIR = ("ssm_a_re", "ssm_a_im", "ssm_log_dt", "ssm_b_re", "ssm_b_im", "ssm_c_re", "ssm_c_im")
SMALL_EARLY = (("mix_norm",) + tuple(n + "_fwd" for n in SSM_DIR) + tuple(n + "_bwd" for n in SSM_DIR)
               + ("ssm_d", "ssm_b_glu", "att_rpb", "ffn2_norm", "final_norm"))
SMALL_LATE = ("ffn1_norm",)
WEIGHTS = ("ffn1_norm", "ffn1_w_gate", "ffn1_w_up", "ffn1_w_down", "mix_norm", "w_in") \
    + tuple(n + "_fwd" for n in SSM_DIR) + tuple(n + "_bwd" for n in SSM_DIR) \
    + ("ssm_d", "ssm_w_glu", "ssm_b_glu", "att_rpb", "w_branch_ssm", "w_branch_att", "w_out",
       "ffn2_norm", "ffn2_w_gate", "ffn2_w_up", "ffn2_w_down", "final_norm")


def _pad_rows(a, mult):
    pad = (-a.shape[-2]) % mult
    if pad:
        a = jnp.concatenate([a, jnp.zeros(a.shape[:-2] + (pad, a.shape[-1]), a.dtype)], axis=-2)
    return a


def _pack(arrays, row_mult):
    flat = jnp.concatenate([a.reshape(-1) for a in arrays])
    pad = (-flat.shape[0]) % PACK_COLS
    if pad:
        flat = jnp.concatenate([flat, jnp.zeros((pad,), flat.dtype)])
    return _pad_rows(flat.reshape(-1, PACK_COLS), row_mult)


def _unpack(slab, shapes):
    flat = slab.reshape(-1)
    out, at = [], 0
    for s in shapes:
        n = int(np.prod(s))
        out.append(flat[at:at + n].reshape(s))
        at += n
    return out


def _split_for_devices(g, axis):
    r, c = g.shape
    if axis == 1:
        return g.reshape(r, N_DEV, c // N_DEV).transpose(1, 0, 2).astype(BF16)
    return g.reshape(N_DEV, r // N_DEV, c).astype(BF16)


def _join_shards(gathered, axis):
    _, r, c = gathered.shape
    if axis == 1:
        return gathered.transpose(1, 0, 2).reshape(r, N_DEV * c)
    return gathered.reshape(N_DEV * r, c)


def _s5_direction_inputs(p, sfx, chain_len):
    bt_re = p["ssm_b_re" + sfx][0].transpose(0, 2, 1)
    bt_im = p["ssm_b_im" + sfx][0].transpose(0, 2, 1)
    raw = (p["ssm_a_re" + sfx][0], p["ssm_a_im" + sfx][0], p["ssm_log_dt" + sfx][0][:, None], bt_re, bt_im)
    lr, li, sr, si, bbr, bbi = _disc_fwd(*raw, chain_len,"s5_disc" + sfx)
    lam = jnp.stack([_s5_pack_lam(t) for t in (lr, li, sr, si)], axis=2)
    mats = (_s5_pack_b(bbr), _s5_pack_b(bbi), lam,
            _s5_pack_c(p["ssm_c_re" + sfx][0]), _s5_pack_c(-p["ssm_c_im" + sfx][0]))
    return raw, mats


def kernel(x, ffn1_norm, ffn1_w_gate, ffn1_w_up, ffn1_w_down, mix_norm, w_in, ssm_a_re_fwd, ssm_a_im_fwd, ssm_log_dt_fwd, ssm_b_re_fwd, ssm_b_im_fwd, ssm_c_re_fwd, ssm_c_im_fwd, ssm_a_re_bwd, ssm_a_im_bwd, ssm_log_dt_bwd, ssm_b_re_bwd, ssm_b_im_bwd, ssm_c_re_bwd, ssm_c_im_bwd, ssm_d, ssm_w_glu, ssm_b_glu, att_rpb, w_branch_ssm, w_branch_att, w_out, ffn2_norm, ffn2_w_gate, ffn2_w_up, ffn2_w_down, final_norm, loss_target, m_ffn1_norm, m_ffn1_w_gate, m_ffn1_w_up, m_ffn1_w_down, m_mix_norm, m_w_in, m_ssm_a_re_fwd, m_ssm_a_im_fwd, m_ssm_log_dt_fwd, m_ssm_b_re_fwd, m_ssm_b_im_fwd, m_ssm_c_re_fwd, m_ssm_c_im_fwd, m_ssm_a_re_bwd, m_ssm_a_im_bwd, m_ssm_log_dt_bwd, m_ssm_b_re_bwd, m_ssm_b_im_bwd, m_ssm_c_re_bwd, m_ssm_c_im_bwd, m_ssm_d, m_ssm_w_glu, m_ssm_b_glu, m_att_rpb, m_w_branch_ssm, m_w_branch_att, m_w_out, m_ffn2_norm, m_ffn2_w_gate, m_ffn2_w_up, m_ffn2_w_down, m_final_norm, v_ffn1_norm, v_ffn1_w_gate, v_ffn1_w_up, v_ffn1_w_down, v_mix_norm, v_w_in, v_ssm_a_re_fwd, v_ssm_a_im_fwd, v_ssm_log_dt_fwd, v_ssm_b_re_fwd, v_ssm_b_im_fwd, v_ssm_c_re_fwd, v_ssm_c_im_fwd, v_ssm_a_re_bwd, v_ssm_a_im_bwd, v_ssm_log_dt_bwd, v_ssm_b_re_bwd, v_ssm_b_im_bwd, v_ssm_c_re_bwd, v_ssm_c_im_bwd, v_ssm_d, v_ssm_w_glu, v_ssm_b_glu, v_att_rpb, v_w_branch_ssm, v_w_branch_att, v_w_out, v_ffn2_norm, v_ffn2_w_gate, v_ffn2_w_up, v_ffn2_w_down, v_final_norm):
    p = dict(locals())
    x = p["x"][0]
    target = p["loss_target"][0]
    T, D = x.shape

    stored = lambda a, n: jnp.swapaxes(a, -1, -2) if n in TRANSPOSED else a
    cut_axis = lambda n: 0 if n in TRANSPOSED else BIG_AXIS[n]
    shard = dict(zip([n for n, _ in BIG], _cast_shards([stored(p[n], n) for n, _ in BIG])))
    ffn1_w = ("ffn1_w_gate", "ffn1_w_up", "ffn1_w_down")
    mix_w = ("w_in", "ssm_w_glu", "w_branch_ssm", "w_branch_att", "w_out")
    ffn2_w = ("ffn2_w_gate", "ffn2_w_up", "ffn2_w_down")
    gathered = dict(zip(ffn1_w, _gather_two_level([shard[n] for n in ffn1_w], "gather_ffn1")))
    full = lambda n: _join_shards(gathered[n], cut_axis(n))

    h0 = x
    wg1, wu1, wd1 = [full(n) for n in ffn1_w]
    (h1, xn1, g1, u1), got = _ffn_fwd(h0, p["ffn1_norm"], wg1, wu1, wd1, "ffn1_fwd",
                                      _Comm("gather", [shard[n] for n in mix_w]))
    gathered.update(zip(mix_w, got))
    z, zb, un = _mixin_fwd(h1, p["mix_norm"], gathered["w_in"])
    W = SSM_WIDTH
    zp = _permute_rows(zb[:, :W])
    chain_len = T // SCAN_LANES // S5_NQ
    raw_f, mats_f = _s5_direction_inputs(p, "_fwd", chain_len)
    raw_b, mats_b = _s5_direction_inputs(p, "_bwd", chain_len)
    bre, bim, lam, cre, cimn = [jnp.stack([f, b]) for f, b in zip(mats_f, mats_b)]
    bre, bim, cre, cimn = [t.astype(BF16) for t in (bre, bim, cre, cimn)]
    (yp,), got = _s5_fwd(zp, bre, bim, lam, cre, cimn, _Comm("gather", [shard[n] for n in ffn2_w]))
    gathered.update(zip(ffn2_w, got))
    ypre = _unpermute_rows(yp)
    table = _att_masked_tables(_att_bias_table(p["att_rpb"][0]), T // GRID_W)
    ya = _att_fwd(zb, table)
    tail_w = (p["ssm_d"], full("ssm_w_glu"), p["ssm_b_glu"], full("w_branch_ssm"), full("w_branch_att"), full("w_out"))
    h2 = _merge_fwd(ypre, z, ya, h1, *tail_w)
    wg2, wu2, wd2 = [full(n) for n in ffn2_w]
    (h3, xn2, g2, u2), _ = _ffn_fwd(h2, p["ffn2_norm"], wg2, wu2, wd2, "ffn2_fwd")
    loss_part, dh3, d_final = _loss_head(h3, p["final_norm"][None], target)

    grads = {"final_norm": d_final[0]}
    to_send = lambda names: _Comm("exchange", [_split_for_devices(grads[n], cut_axis(n)) for n in names])
    parts = {}
    (dh2, grads["ffn2_norm"], do2, a2, dg2, du2), _ = _ffn_bwd(
        dh3, h2, p["ffn2_norm"], g2, u2, wg2, wu2, wd2, "ffn2_bwd")
    grads["ffn2_w_gate"] = _xty(dg2, xn2, "ffn2_dw_gate")
    grads["ffn2_w_up"] = _xty(du2, xn2, "ffn2_dw_up")
    grads["ffn2_w_down"] = _xty(a2, do2, "ffn2_dw_down")
    (dypre, dzs_skip, dgs, dga, dya, grads["ssm_d"], grads["ssm_w_glu"], grads["ssm_b_glu"],
     grads["w_branch_ssm"], grads["w_branch_att"], grads["w_out"]) = _merge_bwd(dh2, ypre, z, ya, *tail_w)
    (dq, dk, dv, dtable), got = _att_bwd(zb, ya, dya, table, to_send(ffn2_w))
    parts.update(zip(ffn2_w, got))
    grads["att_rpb"] = _att_bias_table_t(dtable)
    dyp = _permute_rows(dypre)
    tail_names = ("ssm_w_glu", "w_branch_ssm", "w_branch_att", "w_out")
    (dzp, dbre, dbim, dlam, dcre, dcimn), got = _s5_bwd(zp, dyp, bre, bim, lam, cre, cimn, to_send(tail_names))
    parts.update(zip(tail_names, got))
    G, P = SSM_GROUPS, SSM_STATE
    for d, (sfx, raw) in enumerate((("_fwd", raw_f), ("_bwd", raw_b))):
        da_re, da_im, dldt, dbt_re, dbt_im = _disc_bwd(
            *raw, dlam[d, :, :, 0, :].reshape(G, P), dlam[d, :, :, 1, :].reshape(G, P),
            _s5_unpack_b(dbre[d]), _s5_unpack_b(dbim[d]), "s5_disc_grad" + sfx)
        grads["ssm_a_re" + sfx] = da_re
        grads["ssm_a_im" + sfx] = da_im
        grads["ssm_log_dt" + sfx] = dldt[:, 0]
        grads["ssm_b_re" + sfx] = dbt_re.transpose(0, 2, 1)
        grads["ssm_b_im" + sfx] = dbt_im.transpose(0, 2, 1)
        grads["ssm_c_re" + sfx] = _s5_unpack_c(dcre[d])
        grads["ssm_c_im" + sfx] = -_s5_unpack_c(dcimn[d])
    dzs = _unpermute_rows(dzp) + dzs_skip
    dz = jnp.concatenate([dzs.astype(BF16), dq, dk, dv, dgs, dga], axis=1)
    dh1, grads["mix_norm"] = _mixin_bwd(dz, dh2, h1, p["mix_norm"], gathered["w_in"])
    grads["w_in"] = _xty(un, dz, "dw_in", col_shards=N_DEV)
    pack_small = lambda names, src, pre: _pack([src[pre + n].astype(F32) for n in names], 8)
    early = _Comm(["exchange", "gather"],
                  [grads["w_in"], pack_small(SMALL_EARLY, grads, "")])
    (dh0, grads["ffn1_norm"], do1, a1, dg1, du1), (parts["w_in"], got_early) = _ffn_bwd(
        dh1, h0, p["ffn1_norm"], g1, u1, wg1, wu1, wd1, "ffn1_bwd", early)
    grads["ffn1_w_down"] = _xty(a1, do1, "ffn1_dw_down")
    grads["ffn1_w_gate"], (parts["ffn1_w_down"],) = _xty(dg1, xn1, "ffn1_dw_gate", to_send(("ffn1_w_down",)))
    grads["ffn1_w_up"], (parts["ffn1_w_gate"],) = _xty(du1, xn1, "ffn1_dw_up", to_send(("ffn1_w_gate",)))
    last = _Comm(["exchange", "gather"],
                 [_split_for_devices(grads["ffn1_w_up"], 0), pack_small(SMALL_LATE, grads, "")])
    parts["ffn1_w_up"], got_late = _comm_call(last, "exchange_last")
    got_small = jnp.concatenate([got_early, got_late], axis=1)

    results = {}
    for n, _ in BIG:
        outs = _adamw(parts[n], *[stored(p[pre + n][0], n) for pre in ("", "m_", "v_")], "adamw_" + n)
        results[n] = [stored(o, n)[None] for o in outs]
    early_rows = got_early.shape[1]
    slab = lambda pre: jnp.concatenate([pack_small(SMALL_EARLY, p, pre), pack_small(SMALL_LATE, p, pre)], axis=0)
    small_out = _adamw(got_small, slab(""), slab("m_"), slab("v_"), "adamw_small")
    for names, rows in ((SMALL_EARLY, slice(0, early_rows)), (SMALL_LATE, slice(early_rows, None))):
        shapes = [p[n].shape for n in names]
        for n, vals in zip(names, zip(*[_unpack(out[rows], shapes) for out in small_out])):
            results[n] = list(vals)

    loss = lax.psum(loss_part[0, 0], ("x", "y", "c"))
    out = [loss, dh0[None]]
    for kind in range(4):
        out += [results[n][kind] for n in WEIGHTS]
    return tuple(out)
```

```python
import functools
import math

import numpy as np
import jax
import jax.numpy as jnp
from jax import lax
from jax.experimental import pallas as pl
from jax.experimental.pallas import tpu as pltpu

F32 = jnp.float32
BF16 = jnp.bfloat16
MESH_ID = pl.DeviceIdType.MESH

SSM_GROUP = 16
SSM_GROUPS = 32
SSM_STATE = 64
SSM_WIDTH = 512
ATT_HEADS = 8
ATT_HEAD_DIM = 64
ATT_WIDTH = 512
GRID_W = 64
WIN_H = 8
WIN_W = 16
EPS = 1e-6
NEG_INF = -1e30
ADAM_LR = 0.001
ADAM_B1 = 0.9
ADAM_B2 = 0.999
ADAM_EPS = 1e-08
ADAM_WD = 0.01
ADAM_STEP = 10

N_DEV = 8
V7X_VMEM_BYTES = 64 * 1024 * 1024
VMEM_LIMIT = V7X_VMEM_BYTES - 8 * 1024 * 1024
SCAN_LANES = 8
ATT_ROWS = 4


def _cparams(sem, vmem=None):
    return pltpu.CompilerParams(dimension_semantics=sem, vmem_limit_bytes=vmem)


def _dot(a, b):
    return jnp.dot(a, b, preferred_element_type=F32)


def _dot_nt(a, b):
    return lax.dot_general(a, b, (((1,), (1,)), ((), ())), preferred_element_type=F32)


def _dot_tn(a, b):
    return lax.dot_general(a, b, (((0,), (0,)), ((), ())), preferred_element_type=F32)


def _rms(h):
    return lax.rsqrt(jnp.mean(h * h, axis=-1, keepdims=True) + EPS)


def _rms_bwd(h, r, v):
    return r * v - h * (r * r * r) * jnp.mean(h * v, axis=-1, keepdims=True)


def _col_sum(x):
    return jnp.sum(x, axis=0, keepdims=True)


def _my_place():
    return lax.axis_index("x"), lax.axis_index("y"), lax.axis_index("c")


def _flat(px, py, pc):
    return 4 * px + 2 * py + pc


class _Comm:
    def __init__(self, kind, arrays):
        self.arrays = list(arrays)
        self.n = len(self.arrays)
        self.kinds = [kind] * self.n if isinstance(kind, str) else list(kind)

    def out_shapes(self):
        return [jax.ShapeDtypeStruct((N_DEV,) + a.shape if k == "gather" else a.shape, a.dtype)
                for k, a in zip(self.kinds, self.arrays)]

    def scratch(self):
        return [pltpu.SemaphoreType.DMA((7 * self.n,)), pltpu.SemaphoreType.DMA((7 * self.n,)),
                pltpu.SemaphoreType.DMA((self.n,))]

    def run(self, srcs, dsts, sems, start):
        send_sems, recv_sems, local_sems = sems
        x, y, c = _my_place()
        mine = _flat(x, y, c)
        for a, (src, dst) in enumerate(zip(srcs, dsts)):
            whole = self.kinds[a] == "gather"
            local = pltpu.make_async_copy(src if whole else src.at[mine], dst.at[mine], local_sems.at[a])
            local.start() if start else local.wait()
            for k in range(1, N_DEV):
                px = 1 - x if k & 4 else x
                py = 1 - y if k & 2 else y
                pc = 1 - c if k & 1 else c
                cp = pltpu.make_async_remote_copy(
                    src_ref=src if whole else src.at[_flat(px, py, pc)], dst_ref=dst.at[mine],
                    send_sem=send_sems.at[7 * a + k - 1], recv_sem=recv_sems.at[7 * a + k - 1],
                    device_id=(px, py, pc), device_id_type=MESH_ID)
                cp.start() if start else cp.wait()


_HBM = pl.BlockSpec(memory_space=pltpu.HBM)


def _comm_call(comm, name):
    def body(*refs):
        srcs, dsts, sems = refs[:comm.n], refs[comm.n:2 * comm.n], refs[2 * comm.n:]
        comm.run(srcs, dsts, sems, True)
        comm.run(srcs, dsts, sems, False)

    return pl.pallas_call(body, name=name, in_specs=[_HBM] * comm.n, out_specs=[_HBM] * comm.n,
                          out_shape=comm.out_shapes(), scratch_shapes=comm.scratch())(*comm.arrays)


def _pallas(core, name, grid, in_specs, out_specs, out_shape, scratch, sem, args, comm=None):
    if comm is None:
        out = pl.pallas_call(core, name=name, grid=grid, in_specs=in_specs, out_specs=out_specs,
                             out_shape=out_shape, scratch_shapes=scratch,
                             compiler_params=_cparams(sem, VMEM_LIMIT))(*args)
        return out, []
    n_in, n_out, n_scr, n = len(in_specs), len(out_specs), len(scratch), comm.n

    def body(*refs):
        ins, srcs = refs[:n_in], refs[n_in:n_in + n]
        outs, dsts = refs[n_in + n:n_in + n + n_out], refs[n_in + n + n_out:n_in + 2 * n + n_out]
        scr, sems = refs[n_in + 2 * n + n_out:n_in + 2 * n + n_out + n_scr], refs[n_in + 2 * n + n_out + n_scr:]
        ids = [pl.program_id(k) for k in range(len(grid))]
        first = functools.reduce(lambda a, b: a & b, [i == 0 for i in ids])
        last = functools.reduce(lambda a, b: a & b, [i == g - 1 for i, g in zip(ids, grid)])

        @pl.when(first)
        def _():
            comm.run(srcs, dsts, sems, True)

        core(*ins, *outs, *scr)

        @pl.when(last)
        def _():
            comm.run(srcs, dsts, sems, False)

    out = pl.pallas_call(
        body, name=name, grid=grid, in_specs=list(in_specs) + [_HBM] * n, out_specs=list(out_specs) + [_HBM] * n,
        out_shape=list(out_shape) + comm.out_shapes(), scratch_shapes=list(scratch) + comm.scratch(),
        compiler_params=_cparams(("arbitrary",) * len(grid), VMEM_LIMIT))(*args, *comm.arrays)
    return out[:n_out], out[n_out:]


def _ffn_tiles(T, F, wide):
    if F % 1408 == 0:
        return min(T, 512 if wide else 256), 1408
    return min(T, 1024), 256 if F % 256 == 0 else F


def _ffn_fwd(h, gain, wg, wu, wd, name, comm=None):
    T, D = h.shape
    F = wg.shape[0]
    tm, tf = _ffn_tiles(T, F, wide=True)
    nj = F // tf

    def body(h_ref, gain_ref, wg_ref, wu_ref, wd_ref, ho_ref, xn_ref, g_ref, u_ref, acc_ref):
        j = pl.program_id(1)

        @pl.when(j == 0)
        def _():
            hh = h_ref[...]
            xn_ref[...] = (hh * _rms(hh) * gain_ref[...]).astype(BF16)
            acc_ref[...] = jnp.zeros_like(acc_ref)

        xn = xn_ref[...]
        g = _dot_nt(xn, wg_ref[...])
        u = _dot_nt(xn, wu_ref[...])
        g_ref[...] = g.astype(BF16)
        u_ref[...] = u.astype(BF16)
        a = (g * jax.nn.sigmoid(g) * u).astype(BF16)
        acc_ref[...] += _dot(a, wd_ref[...])

        @pl.when(j == nj - 1)
        def _():
            ho_ref[...] = h_ref[...] + 0.5 * acc_ref[...]

    return _pallas(
        body, name, (T // tm, nj),
        [pl.BlockSpec((tm, D), lambda i, j: (i, 0)),
         pl.BlockSpec((1, D), lambda i, j: (0, 0)),
         pl.BlockSpec((tf, D), lambda i, j: (j, 0)),
         pl.BlockSpec((tf, D), lambda i, j: (j, 0)),
         pl.BlockSpec((tf, D), lambda i, j: (j, 0))],
        [pl.BlockSpec((tm, D), lambda i, j: (i, 0)),
         pl.BlockSpec((tm, D), lambda i, j: (i, 0)),
         pl.BlockSpec((tm, tf), lambda i, j: (i, j)),
         pl.BlockSpec((tm, tf), lambda i, j: (i, j))],
        [jax.ShapeDtypeStruct((T, D), F32), jax.ShapeDtypeStruct((T, D), BF16),
         jax.ShapeDtypeStruct((T, F), BF16), jax.ShapeDtypeStruct((T, F), BF16)],
        [pltpu.VMEM((tm, D), F32)], ("parallel", "arbitrary"), (h, gain, wg, wu, wd), comm)


def _ffn_bwd(dho, h, gain, g, u, wg, wu, wd, name, comm=None):
    T, D = h.shape
    F = wg.shape[0]
    tm, tf = _ffn_tiles(T, F, wide=False)
    nj = F // tf

    def body(dho_ref, h_ref, gain_ref, g_ref, u_ref, wg_ref, wu_ref, wd_ref,
             dh_ref, dgain_ref, do_ref, a_ref, dg_ref, du_ref, acc_ref):
        i = pl.program_id(0)
        j = pl.program_id(1)

        @pl.when(j == 0)
        def _():
            do_ref[...] = (0.5 * dho_ref[...]).astype(BF16)
            acc_ref[...] = jnp.zeros_like(acc_ref)

        @pl.when((i == 0) & (j == 0))
        def _():
            dgain_ref[...] = jnp.zeros_like(dgain_ref)

        da = _dot_nt(do_ref[...], wd_ref[...])
        gg = g_ref[...].astype(F32)
        uu = u_ref[...].astype(F32)
        s = jax.nn.sigmoid(gg)
        sl = gg * s
        a_ref[...] = (sl * uu).astype(BF16)
        dg = (da * uu * (s * (1.0 + gg * (1.0 - s)))).astype(BF16)
        du = (da * sl).astype(BF16)
        dg_ref[...] = dg
        du_ref[...] = du
        acc_ref[...] += _dot(dg, wg_ref[...]) + _dot(du, wu_ref[...])

        @pl.when(j == nj - 1)
        def _():
            hh = h_ref[...]
            r = _rms(hh)
            dxn = acc_ref[...]
            dgain_ref[...] += _col_sum(dxn * hh * r)
            dh_ref[...] = dho_ref[...] + _rms_bwd(hh, r, dxn * gain_ref[...])

    return _pallas(
        body, name, (T // tm, nj),
        [pl.BlockSpec((tm, D), lambda i, j: (i, 0)),
         pl.BlockSpec((tm, D), lambda i, j: (i, 0)),
         pl.BlockSpec((1, D), lambda i, j: (0, 0)),
         pl.BlockSpec((tm, tf), lambda i, j: (i, j)),
         pl.BlockSpec((tm, tf), lambda i, j: (i, j)),
         pl.BlockSpec((tf, D), lambda i, j: (j, 0)),
         pl.BlockSpec((tf, D), lambda i, j: (j, 0)),
         pl.BlockSpec((tf, D), lambda i, j: (j, 0))],
        [pl.BlockSpec((tm, D), lambda i, j: (i, 0)),
         pl.BlockSpec((1, D), lambda i, j: (0, 0)),
         pl.BlockSpec((tm, D), lambda i, j: (i, 0)),
         pl.BlockSpec((tm, tf), lambda i, j: (i, j)),
         pl.BlockSpec((tm, tf), lambda i, j: (i, j)),
         pl.BlockSpec((tm, tf), lambda i, j: (i, j))],
        [jax.ShapeDtypeStruct((T, D), F32), jax.ShapeDtypeStruct((1, D), F32),
         jax.ShapeDtypeStruct((T, D), BF16), jax.ShapeDtypeStruct((T, F), BF16),
         jax.ShapeDtypeStruct((T, F), BF16), jax.ShapeDtypeStruct((T, F), BF16)],
        [pltpu.VMEM((tm, D), F32)], ("arbitrary", "arbitrary"), (dho, h, gain, g, u, wg, wu, wd), comm)


def _xty(x, y, name, comm=None, col_shards=1):
    T, K = x.shape
    N = y.shape[1]
    tn = N // col_shards if col_shards > 1 else min(N, 1024)
    tk = max(t for t in (512, 256, 128) if K % t == 0 and t + tn <= 1280)

    def body(x_ref, y_ref, o_ref):
        o_ref[...] = _dot_tn(x_ref[...], y_ref[...]).astype(BF16)

    if col_shards == 1:
        out_spec = pl.BlockSpec((tk, tn), lambda k, n: (k, n))
        out_shape = jax.ShapeDtypeStruct((K, N), BF16)
    else:
        out_spec = pl.BlockSpec((None, tk, tn), lambda k, n: (n, k, 0))
        out_shape = jax.ShapeDtypeStruct((col_shards, K, tn), BF16)
    (out,), got = _pallas(
        body, name, (K // tk, N // tn),
        [pl.BlockSpec((T, tk), lambda k, n: (0, k)), pl.BlockSpec((T, tn), lambda k, n: (0, n))],
        [out_spec], [out_shape], [], ("parallel", "arbitrary"), (x, y), comm)
    return out if comm is None else (out, got)


def _mixin_fwd(h, gain, w_in):
    T, D = h.shape
    nn, _, tn = w_in.shape
    N = nn * tn
    tm = min(T, 256)

    def body(h_ref, gain_ref, w_ref, z_ref, zb_ref, un_ref):
        hh = h_ref[...]
        un = (hh * _rms(hh) * gain_ref[...]).astype(BF16)
        un_ref[...] = un
        for s in range(nn):
            z = _dot(un, w_ref[s])
            z_ref[:, s * tn:(s + 1) * tn] = z
            zb_ref[:, s * tn:(s + 1) * tn] = z.astype(BF16)

    return pl.pallas_call(
        body, name="mixin_fwd", grid=(T // tm,),
        in_specs=[pl.BlockSpec((tm, D), lambda i: (i, 0)),
                  pl.BlockSpec((1, D), lambda i: (0, 0)),
                  pl.BlockSpec((nn, D, tn), lambda i: (0, 0, 0))],
        out_specs=[pl.BlockSpec((tm, N), lambda i: (i, 0)),
                   pl.BlockSpec((tm, N), lambda i: (i, 0)),
                   pl.BlockSpec((tm, D), lambda i: (i, 0))],
        out_shape=[jax.ShapeDtypeStruct((T, N), F32), jax.ShapeDtypeStruct((T, N), BF16),
                   jax.ShapeDtypeStruct((T, D), BF16)],
        compiler_params=_cparams(("parallel",), VMEM_LIMIT),
    )(h, gain, w_in)


def _mixin_bwd(dz, dh_res, h, gain, w_in):
    T, D = h.shape
    nn, _, tn = w_in.shape
    tm = min(T, 256)

    def body(dz_ref, dres_ref, h_ref, gain_ref, w_ref, dh_ref, dgain_ref):
        @pl.when(pl.program_id(0) == 0)
        def _():
            dgain_ref[...] = jnp.zeros_like(dgain_ref)

        dun = _dot_nt(dz_ref[:, 0:tn], w_ref[0])
        for s in range(1, nn):
            dun = dun + _dot_nt(dz_ref[:, s * tn:(s + 1) * tn], w_ref[s])
        hh = h_ref[...]
        r = _rms(hh)
        dgain_ref[...] += _col_sum(dun * hh * r)
        dh_ref[...] = dres_ref[...] + _rms_bwd(hh, r, dun * gain_ref[...])

    return pl.pallas_call(
        body, name="mixin_bwd", grid=(T // tm,),
        in_specs=[pl.BlockSpec((tm, nn * tn), lambda i: (i, 0)),
                  pl.BlockSpec((tm, D), lambda i: (i, 0)),
                  pl.BlockSpec((tm, D), lambda i: (i, 0)),
                  pl.BlockSpec((1, D), lambda i: (0, 0)),
                  pl.BlockSpec((nn, D, tn), lambda i: (0, 0, 0))],
        out_specs=[pl.BlockSpec((tm, D), lambda i: (i, 0)),
                   pl.BlockSpec((1, D), lambda i: (0, 0))],
        out_shape=[jax.ShapeDtypeStruct((T, D), F32), jax.ShapeDtypeStruct((1, D), F32)],
        compiler_params=_cparams(("arbitrary",), VMEM_LIMIT),
    )(dz, dh_res, h, gain, w_in)


def _loss_head(h, gain, target):
    T, D = h.shape
    tm = min(T, 1024)

    def body(h_ref, gain_ref, t_ref, loss_ref, dh_ref, dgain_ref):
        @pl.when(pl.program_id(0) == 0)
        def _():
            loss_ref[...] = jnp.zeros_like(loss_ref)
            dgain_ref[...] = jnp.zeros_like(dgain_ref)

        hh = h_ref[...]
        r = _rms(hh)
        e = hh * r * gain_ref[...] - t_ref[...]
        loss_ref[...] += (0.5 / D) * jnp.sum(e * e)
        dy = e * (1.0 / D)
        dgain_ref[...] += _col_sum(dy * hh * r)
        dh_ref[...] = _rms_bwd(hh, r, dy * gain_ref[...])

    return pl.pallas_call(
        body, name="loss_head", grid=(T // tm,),
        in_specs=[pl.BlockSpec((tm, D), lambda i: (i, 0)),
                  pl.BlockSpec((1, D), lambda i: (0, 0)),
                  pl.BlockSpec((tm, D), lambda i: (i, 0))],
        out_specs=[pl.BlockSpec((1, 128), lambda i: (0, 0)),
                   pl.BlockSpec((tm, D), lambda i: (i, 0)),
                   pl.BlockSpec((1, D), lambda i: (0, 0))],
        out_shape=[jax.ShapeDtypeStruct((1, 128), F32), jax.ShapeDtypeStruct((T, D), F32),
                   jax.ShapeDtypeStruct((1, D), F32)],
        compiler_params=_cparams(("arbitrary",), VMEM_LIMIT),
    )(h, gain, target)


def _adamw(parts, w, m, v, name):
    R, C = w.shape
    mult = 16 if parts.dtype == BF16 else 8
    tr = max(t for t in range(mult, min(R, 512) + 1, mult) if R % t == 0)
    c1 = 1.0 - ADAM_B1 ** ADAM_STEP
    c2 = 1.0 - ADAM_B2 ** ADAM_STEP

    def body(p_ref, w_ref, m_ref, v_ref, g_ref, d_ref, nm_ref, nv_ref):
        g = p_ref[0].astype(F32)
        for k in range(1, N_DEV):
            g = g + p_ref[k].astype(F32)
        mm = ADAM_B1 * m_ref[...] + (1.0 - ADAM_B1) * g
        vv = ADAM_B2 * v_ref[...] + (1.0 - ADAM_B2) * (g * g)
        g_ref[...] = g
        nm_ref[...] = mm
        nv_ref[...] = vv
        d_ref[...] = -ADAM_LR * ((mm / c1) / (jnp.sqrt(vv / c2) + ADAM_EPS) + ADAM_WD * w_ref[...])

    spec = pl.BlockSpec((tr, C), lambda i: (i, 0))
    return pl.pallas_call(
        body, name=name, grid=(R // tr,),
        in_specs=[pl.BlockSpec((N_DEV, tr, C), lambda i: (0, i, 0)), spec, spec, spec],
        out_specs=[spec, spec, spec, spec],
        out_shape=[jax.ShapeDtypeStruct((R, C), F32)] * 4,
        compiler_params=_cparams(("parallel",), VMEM_LIMIT),
    )(parts, w, m, v)


S5_NS = 256
S5_NH = 2
S5_NCB = 4
S5_RC = 512
S5_NQ = 4
S5_GROUP = 2


def _disc_math(a_re, a_im, log_dt, bt_re, bt_im):
    dt = jnp.exp(log_dt)
    zr, zi = a_re * dt, a_im * dt
    mag = jnp.exp(zr)
    lb_re, lb_im = mag * jnp.cos(zi), mag * jnp.sin(zi)
    den = a_re * a_re + a_im * a_im
    nr, ni = lb_re - 1.0, lb_im
    f_re = (nr * a_re + ni * a_im) / den
    f_im = (ni * a_re - nr * a_im) / den
    bb_re = f_re[:, None, :] * bt_re - f_im[:, None, :] * bt_im
    bb_im = f_re[:, None, :] * bt_im + f_im[:, None, :] * bt_re
    return lb_re, lb_im, bb_re, bb_im


def _disc_fwd(a_re, a_im, log_dt, bt_re, bt_im, chain_len, name):
    G, P = a_re.shape
    C = bt_re.shape[1]
    n_sq = int(round(math.log2(chain_len)))
    assert 2 ** n_sq == chain_len

    def body(a_re_ref, a_im_ref, ldt_ref, br_ref, bi_ref, lr_ref, li_ref, sr_ref, si_ref, bbr_ref, bbi_ref):
        lr, li, bbr, bbi = _disc_math(a_re_ref[...], a_im_ref[...], ldt_ref[...], br_ref[...], bi_ref[...])
        lr_ref[...] = lr
        li_ref[...] = li
        bbr_ref[...] = bbr
        bbi_ref[...] = bbi
        pr, pi = lr, li
        for _ in range(n_sq):
            pr, pi = pr * pr - pi * pi, 2.0 * pr * pi
        sr_ref[...] = pr
        si_ref[...] = pi

    s2 = jax.ShapeDtypeStruct((G, P), F32)
    s3 = jax.ShapeDtypeStruct((G, C, P), F32)
    return pl.pallas_call(body, name=name, out_shape=[s2, s2, s2, s2, s3, s3])(a_re, a_im, log_dt, bt_re, bt_im)


def _disc_bwd(a_re, a_im, log_dt, bt_re, bt_im, d_lr, d_li, d_bbr, d_bbi, name):
    G, P = a_re.shape
    C = bt_re.shape[1]

    def body(a_re_ref, a_im_ref, ldt_ref, br_ref, bi_ref, c1, c2, c3, c4, o1, o2, o3, o4, o5):
        _, vjp = jax.vjp(_disc_math, a_re_ref[...], a_im_ref[...], ldt_ref[...], br_ref[...], bi_ref[...])
        o1[...], o2[...], o3[...], o4[...], o5[...] = vjp((c1[...], c2[...], c3[...], c4[...]))

    s2 = jax.ShapeDtypeStruct((G, P), F32)
    s3 = jax.ShapeDtypeStruct((G, C, P), F32)
    return pl.pallas_call(body, name=name, out_shape=[s2, s2, jax.ShapeDtypeStruct((G, 1), F32), s3, s3])(
        a_re, a_im, log_dt, bt_re, bt_im, d_lr, d_li, d_bbr, d_bbi)


def _row_block(ib):
    return pl.ds(pl.multiple_of(ib * SCAN_LANES, SCAN_LANES), SCAN_LANES)


def _chain_block(j, i, ascending, n_blocks):
    at = j * (n_blocks // S5_NQ) + i
    return _row_block(jnp.where(ascending, at, n_blocks - 1 - at))


def _unrolled_loop(n, unroll, body, carry):
    trips = n // unroll
    carry = lax.fori_loop(
        0, trips, lambda t, c: functools.reduce(lambda cc, u: body(t * unroll + u, cc), range(unroll), c), carry)
    for i in range(trips * unroll, n):
        carry = body(i, carry)
    return carry


def _cmul_add(lr, li, sr, si, xr, xi):
    return lr * sr - li * si + xr, lr * si + li * sr + xi


def _scan(xr_ref, xi_ref, lr, li, init, ascending, n_blocks, store):
    steps = n_blocks // S5_NQ
    if not store:
        def step(i, carry):
            blocks = [_chain_block(j, i, ascending, n_blocks) for j in range(S5_NQ)]
            return tuple(_cmul_add(lr, li, sr, si, xr_ref[rows, :], xi_ref[rows, :])
                         for (sr, si), rows in zip(carry, blocks))

        return _unrolled_loop(steps, 4, step, init)

    group = S5_GROUP
    assert steps % group == 0

    def trip(t, carry):
        blocks = [[_chain_block(j, t * group + u, ascending, n_blocks) for j in range(S5_NQ)] for u in range(group)]
        xs = [[(xr_ref[rows, :], xi_ref[rows, :]) for rows in row] for row in blocks]
        states = list(carry)
        done = []
        for u in range(group):
            states = [_cmul_add(lr, li, sr, si, xr, xi) for (sr, si), (xr, xi) in zip(states, xs[u])]
            done.append(states)
        for u in range(group):
            for rows, (nr, ni) in zip(blocks[u], done[u]):
                xr_ref[rows, :] = nr
                xi_ref[rows, :] = ni
        return tuple(states)

    return lax.fori_loop(0, steps // group, trip, init)


def _segment_starts(w, lsr, lsi, ascending):
    shape = w[0][0].shape
    row = lax.broadcasted_iota(jnp.int32, shape, 0)
    keep = row != jnp.where(ascending, 0, SCAN_LANES - 1)

    def shift(t):
        t = jnp.where(ascending, pltpu.roll(t, 1, 0), pltpu.roll(t, SCAN_LANES - 1, 0))
        return jnp.where(keep, t, 0.0)

    zero = jnp.zeros(shape, F32)
    c = [(zero, zero)] * S5_NQ
    for _ in range(SCAN_LANES):
        tr, ti = _cmul_add(lsr, lsi, *c[-1], *w[-1])
        c[0] = (shift(tr), shift(ti))
        for j in range(1, S5_NQ):
            c[j] = _cmul_add(lsr, lsi, *c[j - 1], *w[j - 1])
    return tuple(c)


def _first_pass(xr_ref, xi_ref, lam_ref, ascending, n_blocks, conj):
    shape = (SCAN_LANES, xr_ref.shape[1])
    sign = -1.0 if conj else 1.0
    lr = jnp.broadcast_to(lam_ref[0:1, :], shape)
    li = sign * jnp.broadcast_to(lam_ref[1:2, :], shape)
    lsr = jnp.broadcast_to(lam_ref[2:3, :], shape)
    lsi = sign * jnp.broadcast_to(lam_ref[3:4, :], shape)
    zero = jnp.zeros(shape, F32)
    w = _scan(xr_ref, xi_ref, lr, li, ((zero, zero),) * S5_NQ, ascending, n_blocks, store=False)
    return _segment_starts(w, lsr, lsi, ascending), lr, li


def _s5_specs(T):
    NS = S5_NS
    tok = pl.BlockSpec((T, 128), lambda c, d, h: (0, c))
    b_spec = pl.BlockSpec((None, None, None, 128, NS), lambda c, d, h: (d, c, h, 0, 0))
    c_spec = pl.BlockSpec((None, None, None, NS, 128), lambda c, d, h: (d, c, h, 0, 0))
    lam_spec = pl.BlockSpec((None, None, None, 4, NS), lambda c, d, h: (d, c, h, 0, 0))
    return tok, b_spec, c_spec, lam_spec


def _s5_fwd(zp, bre, bim, lam, cre, cimn, comm=None):
    T = zp.shape[0]
    NS = S5_NS
    nb = T // SCAN_LANES
    rc = min(S5_RC, T)
    tok, b_spec, c_spec, lam_spec = _s5_specs(T)

    def body(zp_ref, bre_ref, bim_ref, lam_ref, cre_ref, cim_ref, y_ref, xr_ref, xi_ref):
        d = pl.program_id(1)
        ascending = d == 0

        @pl.when((d == 0) & (pl.program_id(2) == 0))
        def _():
            y_ref[...] = jnp.zeros_like(y_ref)

        def proj(c, _):
            rows = pl.ds(pl.multiple_of(c * rc, rc), rc)
            zz = zp_ref[rows, :]
            xr_ref[rows, :] = _dot(zz, bre_ref[...])
            xi_ref[rows, :] = _dot(zz, bim_ref[...])
            return 0

        lax.fori_loop(0, T // rc, proj, 0)
        starts, lr, li = _first_pass(xr_ref, xi_ref, lam_ref, ascending, nb, conj=False)
        _scan(xr_ref, xi_ref, lr, li, starts, ascending, nb, store=True)

        def outp(c, _):
            rows = pl.ds(pl.multiple_of(c * rc, rc), rc)
            y_ref[rows, :] += (_dot(xr_ref[rows, :].astype(BF16), cre_ref[...])
                               + _dot(xi_ref[rows, :].astype(BF16), cim_ref[...]))
            return 0

        lax.fori_loop(0, T // rc, outp, 0)

    return _pallas(
        body, "s5_fwd", (S5_NCB, 2, S5_NH),
        [tok, b_spec, b_spec, lam_spec, c_spec, c_spec], [tok],
        [jax.ShapeDtypeStruct((T, SSM_WIDTH), F32)],
        [pltpu.VMEM((T, NS), F32), pltpu.VMEM((T, NS), F32)],
        ("parallel", "arbitrary", "arbitrary"), (zp, bre, bim, lam, cre, cimn), comm)


def _s5_bwd(zp, dyp, bre, bim, lam, cre, cimn, comm=None):
    T = zp.shape[0]
    NS, NH = S5_NS, S5_NH
    nb = T // SCAN_LANES
    rc = min(S5_RC, T)
    tok, b_spec, c_spec, lam_spec = _s5_specs(T)
    dlam_spec = pl.BlockSpec((None, None, None, 2, NS), lambda c, d, h: (d, c, h, 0, 0))

    def body(zp_ref, dyp_ref, bre_ref, bim_ref, lam_ref, cre_ref, cim_ref,
             dzp_ref, dbre_ref, dbim_ref, dlam_ref, dcre_ref, dcim_ref,
             sr_ref, si_ref, gr_ref, gi_ref):
        d = pl.program_id(1)
        ascending = d == 0
        g_ascending = d != 0

        @pl.when((d == 0) & (pl.program_id(2) == 0))
        def _():
            dzp_ref[...] = jnp.zeros_like(dzp_ref)

        dcre_ref[...] = jnp.zeros_like(dcre_ref)
        dcim_ref[...] = jnp.zeros_like(dcim_ref)
        dbre_ref[...] = jnp.zeros_like(dbre_ref)
        dbim_ref[...] = jnp.zeros_like(dbim_ref)

        def proj(c, _):
            rows = pl.ds(pl.multiple_of(c * rc, rc), rc)
            zz = zp_ref[rows, :]
            sr_ref[rows, :] = _dot(zz, bre_ref[...])
            si_ref[rows, :] = _dot(zz, bim_ref[...])
            dy = dyp_ref[rows, :]
            gr_ref[rows, :] = _dot_nt(dy, cre_ref[...])
            gi_ref[rows, :] = _dot_nt(dy, cim_ref[...])
            return 0

        lax.fori_loop(0, T // rc, proj, 0)
        s_starts, lr, li = _first_pass(sr_ref, si_ref, lam_ref, ascending, nb, conj=False)
        _scan(sr_ref, si_ref, lr, li, s_starts, ascending, nb, store=True)
        g_starts, lr, lic = _first_pass(gr_ref, gi_ref, lam_ref, g_ascending, nb, conj=True)

        steps = nb // S5_NQ
        group = S5_GROUP
        assert steps % group == 0

        def gtrip(t, carry, last):
            g, (ar, ai) = carry
            first = t * group
            blocks = [[_chain_block(j, first + u, g_ascending, nb) for j in range(S5_NQ)] for u in range(group)]
            direct = [[(gr_ref[rows, :], gi_ref[rows, :]) for rows in row] for row in blocks]
            done = []
            for u in range(group):
                new = []
                for j, ((g_r, g_i), (d_r, d_i)) in enumerate(zip(g, direct[u])):
                    n_r, n_i = _cmul_add(lr, lic, g_r, g_i, d_r, d_i)
                    if last and u == group - 1:
                        s_r, s_i = s_starts[S5_NQ - 1 - j]
                    else:
                        prev = _chain_block(j, first + u + 1, g_ascending, nb)
                        s_r, s_i = sr_ref[prev, :], si_ref[prev, :]
                    ar = ar + n_r * s_r + n_i * s_i
                    ai = ai + n_i * s_r - n_r * s_i
                    new.append((n_r, n_i))
                g = new
                done.append(new)
            for u in range(group):
                for rows, (n_r, n_i) in zip(blocks[u], done[u]):
                    gr_ref[rows, :] = n_r
                    gi_ref[rows, :] = n_i
            return tuple(g), (ar, ai)

        zero = jnp.zeros((SCAN_LANES, NS), F32)
        carry = lax.fori_loop(0, steps // group - 1, lambda t, c: gtrip(t, c, False), (g_starts, (zero, zero)))
        _, (ar, ai) = gtrip(steps // group - 1, carry, True)
        dlam_ref[0:1, :] = _col_sum(ar)
        dlam_ref[1:2, :] = _col_sum(ai)

        def grads(c, _):
            rows = pl.ds(pl.multiple_of(c * rc, rc), rc)
            zz = zp_ref[rows, :]
            dy = dyp_ref[rows, :]
            g_rb = gr_ref[rows, :].astype(BF16)
            g_ib = gi_ref[rows, :].astype(BF16)
            dcre_ref[...] += _dot_tn(sr_ref[rows, :].astype(BF16), dy)
            dcim_ref[...] += _dot_tn(si_ref[rows, :].astype(BF16), dy)
            dbre_ref[...] += _dot_tn(zz, g_rb)
            dbim_ref[...] += _dot_tn(zz, g_ib)
            dzp_ref[rows, :] += _dot_nt(g_rb, bre_ref[...]) + _dot_nt(g_ib, bim_ref[...])
            return 0

        lax.fori_loop(0, T // rc, grads, 0)

    f32 = lambda *s: jax.ShapeDtypeStruct(s, F32)
    return _pallas(
        body, "s5_bwd", (S5_NCB, 2, S5_NH),
        [tok, tok, b_spec, b_spec, lam_spec, c_spec, c_spec],
        [tok, b_spec, b_spec, dlam_spec, c_spec, c_spec],
        [f32(T, SSM_WIDTH), f32(2, S5_NCB, NH, 128, NS), f32(2, S5_NCB, NH, 128, NS),
         f32(2, S5_NCB, NH, 2, NS), f32(2, S5_NCB, NH, NS, 128), f32(2, S5_NCB, NH, NS, 128)],
        [pltpu.VMEM((T, NS), F32)] * 4,
        ("parallel", "arbitrary", "arbitrary"), (zp, dyp, bre, bim, lam, cre, cimn), comm)


def _s5_delta():
    d = np.zeros((S5_NH, 8, 8 // S5_NH), np.float32)
    for h in range(S5_NH):
        for go in range(8 // S5_NH):
            d[h, h * (8 // S5_NH) + go, go] = 1.0
    return d


def _s5_pack_b(bbt):
    gh = 8 // S5_NH
    b5 = bbt.reshape(S5_NCB, S5_NH, gh, SSM_GROUP, SSM_STATE).transpose(0, 1, 3, 2, 4)
    m = b5[:, :, None] * _s5_delta()[None, :, :, None, :, None]
    return m.reshape(S5_NCB, S5_NH, 128, S5_NS)


def _s5_unpack_b(dm):
    gh = 8 // S5_NH
    d6 = dm.reshape(S5_NCB, S5_NH, 8, SSM_GROUP, gh, SSM_STATE)
    b5 = jnp.sum(d6 * _s5_delta()[None, :, :, None, :, None], axis=2)
    return b5.transpose(0, 1, 3, 2, 4).reshape(SSM_GROUPS, SSM_GROUP, SSM_STATE)


def _s5_pack_c(c):
    gh = 8 // S5_NH
    c5 = c.reshape(S5_NCB, S5_NH, gh, SSM_GROUP, SSM_STATE).transpose(0, 1, 2, 4, 3)
    m = c5[:, :, :, :, None, :] * _s5_delta().transpose(0, 2, 1)[None, :, :, None, :, None]
    return m.reshape(S5_NCB, S5_NH, S5_NS, 128)


def _s5_unpack_c(dm):
    gh = 8 // S5_NH
    d6 = dm.reshape(S5_NCB, S5_NH, gh, SSM_STATE, 8, SSM_GROUP)
    c5 = jnp.sum(d6 * _s5_delta().transpose(0, 2, 1)[None, :, :, None, :, None], axis=4)
    return c5.transpose(0, 1, 2, 4, 3).reshape(SSM_GROUPS, SSM_GROUP, SSM_STATE)


def _s5_pack_lam(x):
    return x.reshape(S5_NCB, S5_NH, S5_NS)


def _permute_rows(x):
    T = x.shape[0]
    return x.reshape(SCAN_LANES, T // SCAN_LANES, -1).transpose(1, 0, 2).reshape(T, -1)


def _unpermute_rows(x):
    T = x.shape[0]
    return x.reshape(T // SCAN_LANES, SCAN_LANES, -1).transpose(1, 0, 2).reshape(T, -1)


ATT_TB = ATT_ROWS * GRID_W
ATT_KB = 3 * ATT_TB


def _att_valid(i, n_rows):
    qi, kj = np.meshgrid(np.arange(ATT_TB), np.arange(ATT_KB), indexing="ij")
    r = i * ATT_ROWS + qi // GRID_W
    c = qi % GRID_W
    rk = (i - 1) * ATT_ROWS + kj // GRID_W
    x = kj % GRID_W
    rs = np.clip(r - WIN_H // 2, 0, n_rows - WIN_H)
    cs = np.clip(c - WIN_W // 2, 0, GRID_W - WIN_W)
    return (rk >= rs) & (rk < rs + WIN_H) & (x >= cs) & (x < cs + WIN_W)


def _att_masked_tables(table, n_rows):
    n = n_rows // ATT_ROWS
    assert n >= 3
    masks = np.stack([_att_valid(i, n_rows) for i in (0, 1, n - 1)])
    return jnp.where(masks[:, None], table[None], NEG_INF)


def _att_variant(i, n):
    return jnp.where(i == 0, 0, jnp.where(i >= n - 1, 2, 1))


def _att_exp(qh, kh, bias):
    s = _dot_nt(qh, kh) + bias
    return jnp.exp(s - jnp.max(s, axis=1, keepdims=True))


def _att_values_and_ones(vh):
    return jnp.concatenate([vh, jnp.ones_like(vh)], axis=1)


def _att_specs(n, col):
    last = n - 1
    cur = lambda i: (jnp.minimum(i, last), col)
    prv = lambda i: (jnp.maximum(jnp.minimum(i, last) - 1, 0), col)
    nxt = lambda i: (jnp.minimum(i + 1, last), col)
    blk = lambda f: pl.BlockSpec((ATT_TB, ATT_WIDTH), f)
    return blk(cur), blk(prv), blk(nxt)


def _att_fwd(zb, biasv):
    T = zb.shape[0]
    W = ATT_WIDTH
    n = T // ATT_TB
    n_rows = T // GRID_W
    cur = _att_specs(n, 0)[0]
    q_cur = _att_specs(n, 1)[0]
    k_cur, k_prv, k_nxt = _att_specs(n, 2)
    v_cur, v_prv, v_nxt = _att_specs(n, 3)

    def body(q_ref, kp_ref, kc_ref, kn_ref, vp_ref, vc_ref, vn_ref, b_ref, y_ref):
        qs = q_ref[...] * 0.125
        kb = jnp.concatenate([kp_ref[...], kc_ref[...], kn_ref[...]], axis=0)
        vb = jnp.concatenate([vp_ref[...], vc_ref[...], vn_ref[...]], axis=0)
        outs = []
        for h in range(ATT_HEADS):
            hs = slice(h * ATT_HEAD_DIM, (h + 1) * ATT_HEAD_DIM)
            e = _att_exp(qs[:, hs], kb[:, hs], b_ref[h]).astype(BF16)
            ov = _dot(e, _att_values_and_ones(vb[:, hs]))
            outs.append(ov[:, :ATT_HEAD_DIM] * (1.0 / ov[:, ATT_HEAD_DIM:ATT_HEAD_DIM + 1]))
        y_ref[...] = jnp.concatenate(outs, axis=1).astype(BF16)

    return pl.pallas_call(
        body, name="att_fwd", grid=(n,),
        in_specs=[q_cur, k_prv, k_cur, k_nxt, v_prv, v_cur, v_nxt,
                  pl.BlockSpec((None, ATT_HEADS, ATT_TB, ATT_KB), lambda i: (_att_variant(i, n), 0, 0, 0))],
        out_specs=cur,
        out_shape=jax.ShapeDtypeStruct((T, W), BF16),
        compiler_params=_cparams(("parallel",), VMEM_LIMIT),
    )(zb, zb, zb, zb, zb, zb, zb, biasv)


def _att_bwd(zb, y, do, biasv, comm=None):
    T = zb.shape[0]
    W = ATT_WIDTH
    n = T // ATT_TB
    n_rows = T // GRID_W
    cur = _att_specs(n, 0)[0]
    q_cur = _att_specs(n, 1)[0]
    k_cur, k_prv, k_nxt = _att_specs(n, 2)
    v_cur, v_prv, v_nxt = _att_specs(n, 3)
    done = pl.BlockSpec((ATT_TB, W), lambda i: (jnp.maximum(i - 1, 0), 0))
    bias_spec = pl.BlockSpec((None, ATT_HEADS, ATT_TB, ATT_KB), lambda i: (_att_variant(i, n), 0, 0, 0))

    def body(q_ref, y_ref, do_ref, kp_ref, kc_ref, kn_ref, vp_ref, vc_ref, vn_ref, b_ref,
             dq_ref, dk_ref, dv_ref, db_ref, acck_ref, accv_ref):
        i = pl.program_id(0)

        @pl.when(i == 0)
        def _():
            db_ref[...] = jnp.zeros_like(db_ref)
            acck_ref[...] = jnp.zeros_like(acck_ref)
            accv_ref[...] = jnp.zeros_like(accv_ref)

        @pl.when((i > 0) & (i < n))
        def _():
            slot = lax.rem(i + 1, 3)
            acck_ref[slot] = jnp.zeros((ATT_TB, W), F32)
            accv_ref[slot] = jnp.zeros((ATT_TB, W), F32)

        @pl.when(i < n)
        def _():
            qs = q_ref[...] * 0.125
            dob = do_ref[...]
            dy = dob.astype(F32) * y_ref[...].astype(F32)
            kb = jnp.concatenate([kp_ref[...], kc_ref[...], kn_ref[...]], axis=0)
            vb = jnp.concatenate([vp_ref[...], vc_ref[...], vn_ref[...]], axis=0)
            dqs, dks, dvs = [], [], []
            for h in range(ATT_HEADS):
                hs = slice(h * ATT_HEAD_DIM, (h + 1) * ATT_HEAD_DIM)
                qh, kh, vh, doh = qs[:, hs], kb[:, hs], vb[:, hs], dob[:, hs]
                e = _att_exp(qh, kh, b_ref[h])
                p = e * (1.0 / jnp.sum(e, axis=1, keepdims=True))
                dp = _dot_nt(doh, vh)
                ds = p * (dp - jnp.sum(dy[:, hs], axis=1, keepdims=True))
                db_ref[h] += ds
                dsb = ds.astype(BF16)
                dqs.append(_dot(dsb, kh) * 0.125)
                dks.append(_dot_tn(dsb, qh))
                dvs.append(_dot_tn(p.astype(BF16), doh))
            dq_ref[...] = jnp.concatenate(dqs, axis=1).astype(BF16)
            dk_all = jnp.concatenate(dks, axis=1)
            dv_all = jnp.concatenate(dvs, axis=1)
            for b in range(3):
                slot = lax.rem(i + 2 + b, 3)
                rows = slice(b * ATT_TB, (b + 1) * ATT_TB)
                acck_ref[slot] += dk_all[rows]
                accv_ref[slot] += dv_all[rows]

        slot = lax.rem(i + 2, 3)
        dk_ref[...] = acck_ref[slot].astype(BF16)
        dv_ref[...] = accv_ref[slot].astype(BF16)

    return _pallas(
        body, "att_bwd", (n + 1,),
        [q_cur, cur, cur, k_prv, k_cur, k_nxt, v_prv, v_cur, v_nxt, bias_spec],
        [cur, done, done, pl.BlockSpec((ATT_HEADS, ATT_TB, ATT_KB), lambda i: (0, 0, 0))],
        [jax.ShapeDtypeStruct((T, W), BF16)] * 3 + [jax.ShapeDtypeStruct((ATT_HEADS, ATT_TB, ATT_KB), F32)],
        [pltpu.VMEM((3, ATT_TB, W), F32), pltpu.VMEM((3, ATT_TB, W), F32)],
        ("arbitrary",), (zb, y, do, zb, zb, zb, zb, zb, zb, biasv), comm)


def _att_selectors():
    rsel = np.zeros((ATT_ROWS, 3 * ATT_ROWS, 2 * WIN_H - 1), np.float32)
    for a in range(ATT_ROWS):
        for b in range(3 * ATT_ROWS):
            rsel[a, b, b - a - ATT_ROWS + WIN_H - 1] = 1.0
    csel = np.zeros((GRID_W, GRID_W, 2 * WIN_W - 1), np.float32)
    for c in range(GRID_W):
        for x in range(GRID_W):
            csel[c, x, min(max(x - c, -(WIN_W - 1)), WIN_W - 1) + WIN_W - 1] = 1.0
    return rsel, csel


def _att_bias_table(rpb):
    rsel, csel = _att_selectors()
    hi = lax.Precision.HIGHEST
    t = jnp.einsum('hrd,abr->habd', rpb, rsel, precision=hi)
    t = jnp.einsum('habd,cxd->hacbx', t, csel, precision=hi)
    return t.reshape(ATT_HEADS, ATT_TB, ATT_KB)


def _att_bias_table_t(dtable):
    rsel, csel = _att_selectors()
    hi = lax.Precision.HIGHEST
    t = dtable.reshape(ATT_HEADS, ATT_ROWS, GRID_W, 3 * ATT_ROWS, GRID_W)
    t = jnp.einsum('hacbx,cxd->habd', t, csel, precision=hi)
    return jnp.einsum('habd,abr->hrd', t, rsel, precision=hi)


GELU_K = math.sqrt(2.0 / math.pi)
GELU_C = 0.044715
MERGE_TM = 256


def _gelu(x):
    return 0.5 * x * (1.0 + jnp.tanh(GELU_K * (x + GELU_C * x * x * x)))


def _gelu_grad(x):
    t = jnp.tanh(GELU_K * (x + GELU_C * x * x * x))
    return 0.5 * (1.0 + t) + 0.5 * x * (1.0 - t * t) * GELU_K * (1.0 + 3.0 * GELU_C * x * x)


def _merge_forward(ypre, zs, gs, ga, ya, ssm_d, w_glu, b_glu, w_bs, w_ba):
    ys = ypre + ssm_d * zs
    yg = _gelu(ys)
    sg = jax.nn.sigmoid(_dot(yg.astype(BF16), w_glu) + b_glu)
    y2 = yg * sg
    bs = _dot(y2.astype(BF16), w_bs)
    ba = _dot(ya, w_ba)
    s1 = jax.nn.sigmoid(gs)
    s2 = jax.nn.sigmoid(ga)
    merged = s1 * bs + s2 * ba
    return ys, yg, sg, y2, bs, ba, s1, s2, merged


def _merge_in_specs(D, W, tm):
    tok = lambda w, c: pl.BlockSpec((tm, w), lambda i: (i, c))
    full = lambda r, c: pl.BlockSpec((r, c), lambda i: (0, 0))
    z_specs = [tok(W, 0), tok(D, 4 * W // D), tok(D, 4 * W // D + 1)]
    w_specs = [full(1, W), full(W, W), full(1, W), full(W, D), full(W, D), full(D, D)]
    return tok, z_specs, w_specs


def _merge_fwd(ypre, z, ya, h1, ssm_d, w_glu, b_glu, w_bs, w_ba, w_out):
    T, D = h1.shape
    W = ypre.shape[1]
    tm = min(T, MERGE_TM)
    tok, z_specs, w_specs = _merge_in_specs(D, W, tm)

    def body(ypre_ref, zs_ref, gs_ref, ga_ref, ya_ref, h1_ref, d_ref, wglu_ref, bglu_ref, wbs_ref, wba_ref, wout_ref,
             h2_ref):
        merged = _merge_forward(ypre_ref[...], zs_ref[...], gs_ref[...], ga_ref[...], ya_ref[...], d_ref[...],
                                wglu_ref[...], bglu_ref[...], wbs_ref[...], wba_ref[...])[-1]
        h2_ref[...] = h1_ref[...] + _dot(merged.astype(BF16), wout_ref[...])

    return pl.pallas_call(
        body, name="merge_fwd", grid=(T // tm,),
        in_specs=[tok(W, 0)] + z_specs + [tok(W, 0), tok(D, 0)] + w_specs,
        out_specs=tok(D, 0),
        out_shape=jax.ShapeDtypeStruct((T, D), F32),
        compiler_params=_cparams(("parallel",), VMEM_LIMIT),
    )(ypre, z, z, z, ya, h1, ssm_d, w_glu, b_glu, w_bs, w_ba, w_out)


def _merge_bwd(dh2, ypre, z, ya, ssm_d, w_glu, b_glu, w_bs, w_ba, w_out):
    T, D = dh2.shape
    W = ypre.shape[1]
    tm = min(T, MERGE_TM)
    tok, z_specs, w_specs = _merge_in_specs(D, W, tm)

    def body(dh2_ref, ypre_ref, zs_ref, gs_ref, ga_ref, ya_ref, d_ref, wglu_ref, bglu_ref, wbs_ref, wba_ref, wout_ref,
             dypre_ref, dzs_ref, dgs_ref, dga_ref, dya_ref, dd_ref, dwglu_ref, dbglu_ref, dwbs_ref, dwba_ref, dwout_ref):
        @pl.when(pl.program_id(0) == 0)
        def _():
            for r in (dd_ref, dwglu_ref, dbglu_ref, dwbs_ref, dwba_ref, dwout_ref):
                r[...] = jnp.zeros_like(r)

        zs = zs_ref[...]
        ya = ya_ref[...]
        ys, yg, sg, y2, bs, ba, s1, s2, merged = _merge_forward(
            ypre_ref[...], zs, gs_ref[...], ga_ref[...], ya, d_ref[...],
            wglu_ref[...], bglu_ref[...], wbs_ref[...], wba_ref[...])
        dh2b = dh2_ref[...].astype(BF16)
        dmerged = _dot_nt(dh2b, wout_ref[...])
        dwout_ref[...] += _dot_tn(merged.astype(BF16), dh2b)
        dbs = (dmerged * s1).astype(BF16)
        dba = (dmerged * s2).astype(BF16)
        dgs_ref[...] = (dmerged * bs * s1 * (1.0 - s1)).astype(BF16)
        dga_ref[...] = (dmerged * ba * s2 * (1.0 - s2)).astype(BF16)
        dwbs_ref[...] += _dot_tn(y2.astype(BF16), dbs)
        dwba_ref[...] += _dot_tn(ya, dba)
        dya_ref[...] = _dot_nt(dba, wba_ref[...]).astype(BF16)
        dy2 = _dot_nt(dbs, wbs_ref[...])
        dvv = dy2 * yg * sg * (1.0 - sg)
        dvvb = dvv.astype(BF16)
        dyg = dy2 * sg + _dot_nt(dvvb, wglu_ref[...])
        dwglu_ref[...] += _dot_tn(yg.astype(BF16), dvvb)
        dbglu_ref[...] += _col_sum(dvv)
        dys = dyg * _gelu_grad(ys)
        dd_ref[...] += _col_sum(dys * zs)
        dzs_ref[...] = dys * d_ref[...]
        dypre_ref[...] = dys.astype(BF16)

    f32 = lambda *s: jax.ShapeDtypeStruct(s, F32)
    b16 = lambda *s: jax.ShapeDtypeStruct(s, BF16)
    return pl.pallas_call(
        body, name="merge_bwd", grid=(T // tm,),
        in_specs=[tok(D, 0), tok(W, 0)] + z_specs + [tok(W, 0)] + w_specs,
        out_specs=[tok(W, 0), tok(W, 0), tok(D, 0), tok(D, 0), tok(W, 0)] + w_specs,
        out_shape=[b16(T, W), f32(T, W), b16(T, D), b16(T, D), b16(T, W),
                   f32(1, W), f32(W, W), f32(1, W), f32(W, D), f32(W, D), f32(D, D)],
        compiler_params=_cparams(("arbitrary",), VMEM_LIMIT),
    )(dh2, ypre, z, z, z, ya, ssm_d, w_glu, b_glu, w_bs, w_ba, w_out)


def _cast_shards(weights):
    def body(*refs):
        n = len(refs) // 2
        for src, dst in zip(refs[:n], refs[n:]):
            dst[...] = src[0].astype(BF16)

    return pl.pallas_call(
        body, name="cast_shards",
        out_shape=[jax.ShapeDtypeStruct(w.shape[1:], BF16) for w in weights],
        compiler_params=_cparams(None, VMEM_LIMIT))(*weights)


def _gather_two_level(shards, name):
    n = len(shards)

    def body(*refs):
        x_refs, out_refs = refs[:n], refs[n:2 * n]
        send_sems, recv_sems, local_sems = refs[2 * n:]
        x, y, c = _my_place()
        me, sibling = (x, y, c), (x, y, 1 - c)
        chips = [(1 - x, y), (x, 1 - y), (1 - x, 1 - y)]

        def copy(a, k, block, to, own=False):
            slot = out_refs[a].at[_flat(*block)]
            return pltpu.make_async_remote_copy(
                src_ref=x_refs[a] if own else slot, dst_ref=slot,
                send_sem=send_sems.at[7 * a + k], recv_sem=recv_sems.at[7 * a + k],
                device_id=to, device_id_type=MESH_ID)

        sent, local = [], []
        for a in range(n):
            local.append(pltpu.make_async_copy(x_refs[a], out_refs[a].at[_flat(*me)], local_sems.at[a]))
            local[-1].start()
            sent.append(copy(a, 0, me, sibling, own=True))
            sent += [copy(a, 1 + j, me, (*chip, c), own=True) for j, chip in enumerate(chips)]
        for cp in sent:
            cp.start()
        for a in range(n):
            for j, chip in enumerate(chips):
                copy(a, 1 + j, (*chip, c), me).wait_recv()
                sent.append(copy(a, 4 + j, (*chip, c), sibling))
                sent[-1].start()
        for a in range(n):
            copy(a, 0, sibling, me).wait_recv()
            for j, chip in enumerate(chips):
                copy(a, 4 + j, (*chip, 1 - c), me).wait_recv()
        for cp in sent:
            cp.wait_send()
        for cp in local:
            cp.wait()

    return pl.pallas_call(
        body, name=name, in_specs=[_HBM] * n, out_specs=[_HBM] * n,
        out_shape=[jax.ShapeDtypeStruct((N_DEV,) + s.shape, s.dtype) for s in shards],
        scratch_shapes=[pltpu.SemaphoreType.DMA((7 * n,)), pltpu.SemaphoreType.DMA((7 * n,)),
                        pltpu.SemaphoreType.DMA((n,))],
    )(*shards)


PACK_COLS = 1024
BIG = (("ffn1_w_gate", 1), ("ffn1_w_up", 1), ("ffn1_w_down", 0), ("w_in", 1), ("ssm_w_glu", 0),
       ("w_branch_ssm", 1), ("w_branch_att", 1), ("w_out", 0),
       ("ffn2_w_gate", 1), ("ffn2_w_up", 1), ("ffn2_w_down", 0))
BIG_AXIS = dict(BIG)
TRANSPOSED = ("ffn1_w_gate", "ffn1_w_up", "ffn2_w_gate", "ffn2_w_up")
SSM_DIR = ("ssm_a_re", "ssm_a_im", "ssm_log_dt", "ssm_b_re", "ssm_b_im", "ssm_c_re", "ssm_c_im")
SMALL_EARLY = (("mix_norm",) + tuple(n + "_fwd" for n in SSM_DIR) + tuple(n + "_bwd" for n in SSM_DIR)
               + ("ssm_d", "ssm_b_glu", "att_rpb", "ffn2_norm", "final_norm"))
SMALL_LATE = ("ffn1_norm",)
WEIGHTS = ("ffn1_norm", "ffn1_w_gate", "ffn1_w_up", "ffn1_w_down", "mix_norm", "w_in") \
    + tuple(n + "_fwd" for n in SSM_DIR) + tuple(n + "_bwd" for n in SSM_DIR) \
    + ("ssm_d", "ssm_w_glu", "ssm_b_glu", "att_rpb", "w_branch_ssm", "w_branch_att", "w_out",
       "ffn2_norm", "ffn2_w_gate", "ffn2_w_up", "ffn2_w_down", "final_norm")


def _pad_rows(a, mult):
    pad = (-a.shape[-2]) % mult
    if pad:
        a = jnp.concatenate([a, jnp.zeros(a.shape[:-2] + (pad, a.shape[-1]), a.dtype)], axis=-2)
    return a


def _pack(arrays, row_mult):
    flat = jnp.concatenate([a.reshape(-1) for a in arrays])
    pad = (-flat.shape[0]) % PACK_COLS
    if pad:
        flat = jnp.concatenate([flat, jnp.zeros((pad,), flat.dtype)])
    return _pad_rows(flat.reshape(-1, PACK_COLS), row_mult)


def _unpack(slab, shapes):
    flat = slab.reshape(-1)
    out, at = [], 0
    for s in shapes:
        n = int(np.prod(s))
        out.append(flat[at:at + n].reshape(s))
        at += n
    return out


def _split_for_devices(g, axis):
    r, c = g.shape
    if axis == 1:
        return g.reshape(r, N_DEV, c // N_DEV).transpose(1, 0, 2).astype(BF16)
    return g.reshape(N_DEV, r // N_DEV, c).astype(BF16)


def _join_shards(gathered, axis):
    _, r, c = gathered.shape
    if axis == 1:
        return gathered.transpose(1, 0, 2).reshape(r, N_DEV * c)
    return gathered.reshape(N_DEV * r, c)


def _s5_direction_inputs(p, sfx, chain_len):
    bt_re = p["ssm_b_re" + sfx][0].transpose(0, 2, 1)
    bt_im = p["ssm_b_im" + sfx][0].transpose(0, 2, 1)
    raw = (p["ssm_a_re" + sfx][0], p["ssm_a_im" + sfx][0], p["ssm_log_dt" + sfx][0][:, None], bt_re, bt_im)
    lr, li, sr, si, bbr, bbi = _disc_fwd(*raw, chain_len,"s5_disc" + sfx)
    lam = jnp.stack([_s5_pack_lam(t) for t in (lr, li, sr, si)], axis=2)
    mats = (_s5_pack_b(bbr), _s5_pack_b(bbi), lam,
            _s5_pack_c(p["ssm_c_re" + sfx][0]), _s5_pack_c(-p["ssm_c_im" + sfx][0]))
    return raw, mats


def kernel(x, ffn1_norm, ffn1_w_gate, ffn1_w_up, ffn1_w_down, mix_norm, w_in, ssm_a_re_fwd, ssm_a_im_fwd, ssm_log_dt_fwd, ssm_b_re_fwd, ssm_b_im_fwd, ssm_c_re_fwd, ssm_c_im_fwd, ssm_a_re_bwd, ssm_a_im_bwd, ssm_log_dt_bwd, ssm_b_re_bwd, ssm_b_im_bwd, ssm_c_re_bwd, ssm_c_im_bwd, ssm_d, ssm_w_glu, ssm_b_glu, att_rpb, w_branch_ssm, w_branch_att, w_out, ffn2_norm, ffn2_w_gate, ffn2_w_up, ffn2_w_down, final_norm, loss_target, m_ffn1_norm, m_ffn1_w_gate, m_ffn1_w_up, m_ffn1_w_down, m_mix_norm, m_w_in, m_ssm_a_re_fwd, m_ssm_a_im_fwd, m_ssm_log_dt_fwd, m_ssm_b_re_fwd, m_ssm_b_im_fwd, m_ssm_c_re_fwd, m_ssm_c_im_fwd, m_ssm_a_re_bwd, m_ssm_a_im_bwd, m_ssm_log_dt_bwd, m_ssm_b_re_bwd, m_ssm_b_im_bwd, m_ssm_c_re_bwd, m_ssm_c_im_bwd, m_ssm_d, m_ssm_w_glu, m_ssm_b_glu, m_att_rpb, m_w_branch_ssm, m_w_branch_att, m_w_out, m_ffn2_norm, m_ffn2_w_gate, m_ffn2_w_up, m_ffn2_w_down, m_final_norm, v_ffn1_norm, v_ffn1_w_gate, v_ffn1_w_up, v_ffn1_w_down, v_mix_norm, v_w_in, v_ssm_a_re_fwd, v_ssm_a_im_fwd, v_ssm_log_dt_fwd, v_ssm_b_re_fwd, v_ssm_b_im_fwd, v_ssm_c_re_fwd, v_ssm_c_im_fwd, v_ssm_a_re_bwd, v_ssm_a_im_bwd, v_ssm_log_dt_bwd, v_ssm_b_re_bwd, v_ssm_b_im_bwd, v_ssm_c_re_bwd, v_ssm_c_im_bwd, v_ssm_d, v_ssm_w_glu, v_ssm_b_glu, v_att_rpb, v_w_branch_ssm, v_w_branch_att, v_w_out, v_ffn2_norm, v_ffn2_w_gate, v_ffn2_w_up, v_ffn2_w_down, v_final_norm):
    p = dict(locals())
    x = p["x"][0]
    target = p["loss_target"][0]
    T, D = x.shape

    stored = lambda a, n: jnp.swapaxes(a, -1, -2) if n in TRANSPOSED else a
    cut_axis = lambda n: 0 if n in TRANSPOSED else BIG_AXIS[n]
    shard = dict(zip([n for n, _ in BIG], _cast_shards([stored(p[n], n) for n, _ in BIG])))
    ffn1_w = ("ffn1_w_gate", "ffn1_w_up", "ffn1_w_down")
    mix_w = ("w_in", "ssm_w_glu", "w_branch_ssm", "w_branch_att", "w_out")
    ffn2_w = ("ffn2_w_gate", "ffn2_w_up", "ffn2_w_down")
    gathered = dict(zip(ffn1_w, _gather_two_level([shard[n] for n in ffn1_w], "gather_ffn1")))
    full = lambda n: _join_shards(gathered[n], cut_axis(n))

    h0 = x
    wg1, wu1, wd1 = [full(n) for n in ffn1_w]
    (h1, xn1, g1, u1), got = _ffn_fwd(h0, p["ffn1_norm"], wg1, wu1, wd1, "ffn1_fwd",
                                      _Comm("gather", [shard[n] for n in mix_w]))
    gathered.update(zip(mix_w, got))
    z, zb, un = _mixin_fwd(h1, p["mix_norm"], gathered["w_in"])
    W = SSM_WIDTH
    zp = _permute_rows(zb[:, :W])
    chain_len = T // SCAN_LANES // S5_NQ
    raw_f, mats_f = _s5_direction_inputs(p, "_fwd", chain_len)
    raw_b, mats_b = _s5_direction_inputs(p, "_bwd", chain_len)
    bre, bim, lam, cre, cimn = [jnp.stack([f, b]) for f, b in zip(mats_f, mats_b)]
    bre, bim, cre, cimn = [t.astype(BF16) for t in (bre, bim, cre, cimn)]
    (yp,), got = _s5_fwd(zp, bre, bim, lam, cre, cimn, _Comm("gather", [shard[n] for n in ffn2_w]))
    gathered.update(zip(ffn2_w, got))
    ypre = _unpermute_rows(yp)
    table = _att_masked_tables(_att_bias_table(p["att_rpb"][0]), T // GRID_W)
    ya = _att_fwd(zb, table)
    tail_w = (p["ssm_d"], full("ssm_w_glu"), p["ssm_b_glu"], full("w_branch_ssm"), full("w_branch_att"), full("w_out"))
    h2 = _merge_fwd(ypre, z, ya, h1, *tail_w)
    wg2, wu2, wd2 = [full(n) for n in ffn2_w]
    (h3, xn2, g2, u2), _ = _ffn_fwd(h2, p["ffn2_norm"], wg2, wu2, wd2, "ffn2_fwd")
    loss_part, dh3, d_final = _loss_head(h3, p["final_norm"][None], target)

    grads = {"final_norm": d_final[0]}
    to_send = lambda names: _Comm("exchange", [_split_for_devices(grads[n], cut_axis(n)) for n in names])
    parts = {}
    (dh2, grads["ffn2_norm"], do2, a2, dg2, du2), _ = _ffn_bwd(
        dh3, h2, p["ffn2_norm"], g2, u2, wg2, wu2, wd2, "ffn2_bwd")
    grads["ffn2_w_gate"] = _xty(dg2, xn2, "ffn2_dw_gate")
    grads["ffn2_w_up"] = _xty(du2, xn2, "ffn2_dw_up")
    grads["ffn2_w_down"] = _xty(a2, do2, "ffn2_dw_down")
    (dypre, dzs_skip, dgs, dga, dya, grads["ssm_d"], grads["ssm_w_glu"], grads["ssm_b_glu"],
     grads["w_branch_ssm"], grads["w_branch_att"], grads["w_out"]) = _merge_bwd(dh2, ypre, z, ya, *tail_w)
    (dq, dk, dv, dtable), got = _att_bwd(zb, ya, dya, table, to_send(ffn2_w))
    parts.update(zip(ffn2_w, got))
    grads["att_rpb"] = _att_bias_table_t(dtable)
    dyp = _permute_rows(dypre)
    tail_names = ("ssm_w_glu", "w_branch_ssm", "w_branch_att", "w_out")
    (dzp, dbre, dbim, dlam, dcre, dcimn), got = _s5_bwd(zp, dyp, bre, bim, lam, cre, cimn, to_send(tail_names))
    parts.update(zip(tail_names, got))
    G, P = SSM_GROUPS, SSM_STATE
    for d, (sfx, raw) in enumerate((("_fwd", raw_f), ("_bwd", raw_b))):
        da_re, da_im, dldt, dbt_re, dbt_im = _disc_bwd(
            *raw, dlam[d, :, :, 0, :].reshape(G, P), dlam[d, :, :, 1, :].reshape(G, P),
            _s5_unpack_b(dbre[d]), _s5_unpack_b(dbim[d]), "s5_disc_grad" + sfx)
        grads["ssm_a_re" + sfx] = da_re
        grads["ssm_a_im" + sfx] = da_im
        grads["ssm_log_dt" + sfx] = dldt[:, 0]
        grads["ssm_b_re" + sfx] = dbt_re.transpose(0, 2, 1)
        grads["ssm_b_im" + sfx] = dbt_im.transpose(0, 2, 1)
        grads["ssm_c_re" + sfx] = _s5_unpack_c(dcre[d])
        grads["ssm_c_im" + sfx] = -_s5_unpack_c(dcimn[d])
    dzs = _unpermute_rows(dzp) + dzs_skip
    dz = jnp.concatenate([dzs.astype(BF16), dq, dk, dv, dgs, dga], axis=1)
    dh1, grads["mix_norm"] = _mixin_bwd(dz, dh2, h1, p["mix_norm"], gathered["w_in"])
    grads["w_in"] = _xty(un, dz, "dw_in", col_shards=N_DEV)
    pack_small = lambda names, src, pre: _pack([src[pre + n].astype(F32) for n in names], 8)
    early = _Comm(["exchange", "gather"],
                  [grads["w_in"], pack_small(SMALL_EARLY, grads, "")])
    (dh0, grads["ffn1_norm"], do1, a1, dg1, du1), (parts["w_in"], got_early) = _ffn_bwd(
        dh1, h0, p["ffn1_norm"], g1, u1, wg1, wu1, wd1, "ffn1_bwd", early)
    grads["ffn1_w_down"] = _xty(a1, do1, "ffn1_dw_down")
    grads["ffn1_w_gate"], (parts["ffn1_w_down"],) = _xty(dg1, xn1, "ffn1_dw_gate", to_send(("ffn1_w_down",)))
    grads["ffn1_w_up"], (parts["ffn1_w_gate"],) = _xty(du1, xn1, "ffn1_dw_up", to_send(("ffn1_w_gate",)))
    last = _Comm(["exchange", "gather"],
                 [_split_for_devices(grads["ffn1_w_up"], 0), pack_small(SMALL_LATE, grads, "")])
    parts["ffn1_w_up"], got_late = _comm_call(last, "exchange_last")
    got_small = jnp.concatenate([got_early, got_late], axis=1)

    results = {}
    for n, _ in BIG:
        outs = _adamw(parts[n], *[stored(p[pre + n][0], n) for pre in ("", "m_", "v_")], "adamw_" + n)
        results[n] = [stored(o, n)[None] for o in outs]
    early_rows = got_early.shape[1]
    slab = lambda pre: jnp.concatenate([pack_small(SMALL_EARLY, p, pre), pack_small(SMALL_LATE, p, pre)], axis=0)
    small_out = _adamw(got_small, slab(""), slab("m_"), slab("v_"), "adamw_small")
    for names, rows in ((SMALL_EARLY, slice(0, early_rows)), (SMALL_LATE, slice(early_rows, None))):
        shapes = [p[n].shape for n in names]
        for n, vals in zip(names, zip(*[_unpack(out[rows], shapes) for out in small_out])):
            results[n] = list(vals)

    loss = lax.psum(loss_part[0, 0], ("x", "y", "c"))
    out = [loss, dh0[None]]
    for kind in range(4):
        out += [results[n][kind] for n in WEIGHTS]
    return tuple(out)
```

```python
import functools
import math

import numpy as np
import jax
import jax.numpy as jnp
from jax import lax
from jax.experimental import pallas as pl
from jax.experimental.pallas import tpu as pltpu

F32 = jnp.float32
BF16 = jnp.bfloat16
MESH_ID = pl.DeviceIdType.MESH

SSM_GROUP = 16
SSM_GROUPS = 32
SSM_STATE = 64
SSM_WIDTH = 512
ATT_HEADS = 8
ATT_HEAD_DIM = 64
ATT_WIDTH = 512
GRID_W = 64
WIN_H = 8
WIN_W = 16
EPS = 1e-6
NEG_INF = -1e30
ADAM_LR = 0.001
ADAM_B1 = 0.9
ADAM_B2 = 0.999
ADAM_EPS = 1e-08
ADAM_WD = 0.01
ADAM_STEP = 10

N_DEV = 8
V7X_VMEM_BYTES = 64 * 1024 * 1024
VMEM_LIMIT = V7X_VMEM_BYTES - 8 * 1024 * 1024
SCAN_LANES = 8
ATT_ROWS = 4


def _cparams(sem, vmem=None):
    return pltpu.CompilerParams(dimension_semantics=sem, vmem_limit_bytes=vmem)


def _dot(a, b):
    return jnp.dot(a, b, preferred_element_type=F32)


def _dot_nt(a, b):
    return lax.dot_general(a, b, (((1,), (1,)), ((), ())), preferred_element_type=F32)


def _dot_tn(a, b):
    return lax.dot_general(a, b, (((0,), (0,)), ((), ())), preferred_element_type=F32)


def _rms(h):
    return lax.rsqrt(jnp.mean(h * h, axis=-1, keepdims=True) + EPS)


def _rms_bwd(h, r, v):
    return r * v - h * (r * r * r) * jnp.mean(h * v, axis=-1, keepdims=True)


def _col_sum(x):
    return jnp.sum(x, axis=0, keepdims=True)


def _my_place():
    return lax.axis_index("x"), lax.axis_index("y"), lax.axis_index("c")


def _flat(px, py, pc):
    return 4 * px + 2 * py + pc


class _Comm:
    def __init__(self, kind, arrays):
        self.arrays = list(arrays)
        self.n = len(self.arrays)
        self.kinds = [kind] * self.n if isinstance(kind, str) else list(kind)

    def out_shapes(self):
        return [jax.ShapeDtypeStruct((N_DEV,) + a.shape if k == "gather" else a.shape, a.dtype)
                for k, a in zip(self.kinds, self.arrays)]

    def scratch(self):
        return [pltpu.SemaphoreType.DMA((7 * self.n,)), pltpu.SemaphoreType.DMA((7 * self.n,)),
                pltpu.SemaphoreType.DMA((self.n,))]

    def run(self, srcs, dsts, sems, start):
        send_sems, recv_sems, local_sems = sems
        x, y, c = _my_place()
        mine = _flat(x, y, c)
        for a, (src, dst) in enumerate(zip(srcs, dsts)):
            whole = self.kinds[a] == "gather"
            local = pltpu.make_async_copy(src if whole else src.at[mine], dst.at[mine], local_sems.at[a])
            local.start() if start else local.wait()
            for k in range(1, N_DEV):
                px = 1 - x if k & 4 else x
                py = 1 - y if k & 2 else y
                pc = 1 - c if k & 1 else c
                cp = pltpu.make_async_remote_copy(
                    src_ref=src if whole else src.at[_flat(px, py, pc)], dst_ref=dst.at[mine],
                    send_sem=send_sems.at[7 * a + k - 1], recv_sem=recv_sems.at[7 * a + k - 1],
                    device_id=(px, py, pc), device_id_type=MESH_ID)
                cp.start() if start else cp.wait()


_HBM = pl.BlockSpec(memory_space=pltpu.HBM)


def _comm_call(comm, name):
    def body(*refs):
        srcs, dsts, sems = refs[:comm.n], refs[comm.n:2 * comm.n], refs[2 * comm.n:]
        comm.run(srcs, dsts, sems, True)
        comm.run(srcs, dsts, sems, False)

    return pl.pallas_call(body, name=name, in_specs=[_HBM] * comm.n, out_specs=[_HBM] * comm.n,
                          out_shape=comm.out_shapes(), scratch_shapes=comm.scratch())(*comm.arrays)


def _pallas(core, name, grid, in_specs, out_specs, out_shape, scratch, sem, args, comm=None):
    if comm is None:
        out = pl.pallas_call(core, name=name, grid=grid, in_specs=in_specs, out_specs=out_specs,
                             out_shape=out_shape, scratch_shapes=scratch,
                             compiler_params=_cparams(sem, VMEM_LIMIT))(*args)
        return out, []
    n_in, n_out, n_scr, n = len(in_specs), len(out_specs), len(scratch), comm.n

    def body(*refs):
        ins, srcs = refs[:n_in], refs[n_in:n_in + n]
        outs, dsts = refs[n_in + n:n_in + n + n_out], refs[n_in + n + n_out:n_in + 2 * n + n_out]
        scr, sems = refs[n_in + 2 * n + n_out:n_in + 2 * n + n_out + n_scr], refs[n_in + 2 * n + n_out + n_scr:]
        ids = [pl.program_id(k) for k in range(len(grid))]
        first = functools.reduce(lambda a, b: a & b, [i == 0 for i in ids])
        last = functools.reduce(lambda a, b: a & b, [i == g - 1 for i, g in zip(ids, grid)])

        @pl.when(first)
        def _():
            comm.run(srcs, dsts, sems, True)

        core(*ins, *outs, *scr)

        @pl.when(last)
        def _():
            comm.run(srcs, dsts, sems, False)

    out = pl.pallas_call(
        body, name=name, grid=grid, in_specs=list(in_specs) + [_HBM] * n, out_specs=list(out_specs) + [_HBM] * n,
        out_shape=list(out_shape) + comm.out_shapes(), scratch_shapes=list(scratch) + comm.scratch(),
        compiler_params=_cparams(("arbitrary",) * len(grid), VMEM_LIMIT))(*args, *comm.arrays)
    return out[:n_out], out[n_out:]


def _ffn_tiles(T, F, wide):
    if F % 1408 == 0:
        return min(T, 512 if wide else 256), 1408
    return min(T, 1024), 256 if F % 256 == 0 else F


def _ffn_fwd(h, gain, wg, wu, wd, name, comm=None):
    T, D = h.shape
    F = wg.shape[0]
    tm = min(T, 256)
    once = pl.Buffered(1)

    def body(h_ref, gain_ref, wg_ref, wu_ref, wd_ref, ho_ref, xn_ref, g_ref, u_ref):
        hh = h_ref[...]
        xn = (hh * _rms(hh) * gain_ref[...]).astype(BF16)
        xn_ref[...] = xn
        g = _dot_nt(xn, wg_ref[...])
        u = _dot_nt(xn, wu_ref[...])
        g_ref[...] = g.astype(BF16)
        u_ref[...] = u.astype(BF16)
        a = (g * jax.nn.sigmoid(g) * u).astype(BF16)
        ho_ref[...] = hh + 0.5 * _dot(a, wd_ref[...])

    return _pallas(
        body, name, (T // tm,),
        [pl.BlockSpec((tm, D), lambda i: (i, 0)),
         pl.BlockSpec((1, D), lambda i: (0, 0)),
         pl.BlockSpec((F, D), lambda i: (0, 0), pipeline_mode=once),
         pl.BlockSpec((F, D), lambda i: (0, 0), pipeline_mode=once),
         pl.BlockSpec((F, D), lambda i: (0, 0), pipeline_mode=once)],
        [pl.BlockSpec((tm, D), lambda i: (i, 0)),
         pl.BlockSpec((tm, D), lambda i: (i, 0)),
         pl.BlockSpec((tm, F), lambda i: (i, 0)),
         pl.BlockSpec((tm, F), lambda i: (i, 0))],
        [jax.ShapeDtypeStruct((T, D), F32), jax.ShapeDtypeStruct((T, D), BF16),
         jax.ShapeDtypeStruct((T, F), BF16), jax.ShapeDtypeStruct((T, F), BF16)],
        [], ("parallel",), (h, gain, wg, wu, wd), comm)


def _ffn_bwd(dho, h, gain, g, u, wg, wu, wd, name, comm=None):
    T, D = h.shape
    F = wg.shape[0]
    tm, tf = _ffn_tiles(T, F, wide=False)
    nj = F // tf

    def body(dho_ref, h_ref, gain_ref, g_ref, u_ref, wg_ref, wu_ref, wd_ref,
             dh_ref, dgain_ref, do_ref, a_ref, dg_ref, du_ref, acc_ref):
        i = pl.program_id(0)
        j = pl.program_id(1)

        @pl.when(j == 0)
        def _():
            do_ref[...] = (0.5 * dho_ref[...]).astype(BF16)
            acc_ref[...] = jnp.zeros_like(acc_ref)

        @pl.when((i == 0) & (j == 0))
        def _():
            dgain_ref[...] = jnp.zeros_like(dgain_ref)

        da = _dot_nt(do_ref[...], wd_ref[...])
        gg = g_ref[...].astype(F32)
        uu = u_ref[...].astype(F32)
        s = jax.nn.sigmoid(gg)
        sl = gg * s
        a_ref[...] = (sl * uu).astype(BF16)
        dg = (da * uu * (s * (1.0 + gg * (1.0 - s)))).astype(BF16)
        du = (da * sl).astype(BF16)
        dg_ref[...] = dg
        du_ref[...] = du
        acc_ref[...] += _dot(dg, wg_ref[...]) + _dot(du, wu_ref[...])

        @pl.when(j == nj - 1)
        def _():
            hh = h_ref[...]
            r = _rms(hh)
            dxn = acc_ref[...]
            dgain_ref[...] += _col_sum(dxn * hh * r)
            dh_ref[...] = dho_ref[...] + _rms_bwd(hh, r, dxn * gain_ref[...])

    return _pallas(
        body, name, (T // tm, nj),
        [pl.BlockSpec((tm, D), lambda i, j: (i, 0)),
         pl.BlockSpec((tm, D), lambda i, j: (i, 0)),
         pl.BlockSpec((1, D), lambda i, j: (0, 0)),
         pl.BlockSpec((tm, tf), lambda i, j: (i, j)),
         pl.BlockSpec((tm, tf), lambda i, j: (i, j)),
         pl.BlockSpec((tf, D), lambda i, j: (j, 0)),
         pl.BlockSpec((tf, D), lambda i, j: (j, 0)),
         pl.BlockSpec((tf, D), lambda i, j: (j, 0))],
        [pl.BlockSpec((tm, D), lambda i, j: (i, 0)),
         pl.BlockSpec((1, D), lambda i, j: (0, 0)),
         pl.BlockSpec((tm, D), lambda i, j: (i, 0)),
         pl.BlockSpec((tm, tf), lambda i, j: (i, j)),
         pl.BlockSpec((tm, tf), lambda i, j: (i, j)),
         pl.BlockSpec((tm, tf), lambda i, j: (i, j))],
        [jax.ShapeDtypeStruct((T, D), F32), jax.ShapeDtypeStruct((1, D), F32),
         jax.ShapeDtypeStruct((T, D), BF16), jax.ShapeDtypeStruct((T, F), BF16),
         jax.ShapeDtypeStruct((T, F), BF16), jax.ShapeDtypeStruct((T, F), BF16)],
        [pltpu.VMEM((tm, D), F32)], ("arbitrary", "arbitrary"), (dho, h, gain, g, u, wg, wu, wd), comm)


def _xty(x, y, name, comm=None, col_shards=1):
    T, K = x.shape
    N = y.shape[1]
    tt = min(T, 1024)
    tk = K if K <= 1024 else (1408 if K % 1408 == 0 else K)
    tn = N if N <= 1024 else (1408 if N % 1408 == 0 else (1024 if N % 1024 == 0 else N))
    nt = T // tt
    ws = N // col_shards
    per = tn // ws if col_shards > 1 else 1
    assert col_shards == 1 or (tn % ws == 0 and ws % 128 == 0)

    def body(x_ref, y_ref, o_ref, acc_ref):
        t = pl.program_id(2)

        @pl.when(t == 0)
        def _():
            acc_ref[...] = jnp.zeros_like(acc_ref)

        acc_ref[...] += _dot_tn(x_ref[...], y_ref[...])

        @pl.when(t == nt - 1)
        def _():
            if col_shards == 1:
                o_ref[...] = acc_ref[...].astype(BF16)
            else:
                for s in range(per):
                    o_ref[s] = acc_ref[:, s * ws:(s + 1) * ws].astype(BF16)

    if col_shards == 1:
        out_spec = pl.BlockSpec((tk, tn), lambda k, n, t: (k, n))
        out_shape = jax.ShapeDtypeStruct((K, N), BF16)
    else:
        out_spec = pl.BlockSpec((per, tk, ws), lambda k, n, t: (n, k, 0))
        out_shape = jax.ShapeDtypeStruct((col_shards, K, ws), BF16)
    (out,), got = _pallas(
        body, name, (K // tk, N // tn, nt),
        [pl.BlockSpec((tt, tk), lambda k, n, t: (t, k)), pl.BlockSpec((tt, tn), lambda k, n, t: (t, n))],
        [out_spec], [out_shape], [pltpu.VMEM((tk, tn), F32)],
        ("parallel", "parallel", "arbitrary"), (x, y), comm)
    return out if comm is None else (out, got)


def _mixin_fwd(h, gain, w_in):
    T, D = h.shape
    nn, _, tn = w_in.shape
    N = nn * tn
    tm = min(T, 256)

    def body(h_ref, gain_ref, w_ref, z_ref, zb_ref, un_ref):
        hh = h_ref[...]
        un = (hh * _rms(hh) * gain_ref[...]).astype(BF16)
        un_ref[...] = un
        for s in range(nn):
            z = _dot(un, w_ref[s])
            z_ref[:, s * tn:(s + 1) * tn] = z
            zb_ref[:, s * tn:(s + 1) * tn] = z.astype(BF16)

    return pl.pallas_call(
        body, name="mixin_fwd", grid=(T // tm,),
        in_specs=[pl.BlockSpec((tm, D), lambda i: (i, 0)),
                  pl.BlockSpec((1, D), lambda i: (0, 0)),
                  pl.BlockSpec((nn, D, tn), lambda i: (0, 0, 0))],
        out_specs=[pl.BlockSpec((tm, N), lambda i: (i, 0)),
                   pl.BlockSpec((tm, N), lambda i: (i, 0)),
                   pl.BlockSpec((tm, D), lambda i: (i, 0))],
        out_shape=[jax.ShapeDtypeStruct((T, N), F32), jax.ShapeDtypeStruct((T, N), BF16),
                   jax.ShapeDtypeStruct((T, D), BF16)],
        compiler_params=_cparams(("parallel",), VMEM_LIMIT),
    )(h, gain, w_in)


def _mixin_bwd(dz, dh_res, h, gain, w_in):
    T, D = h.shape
    nn, _, tn = w_in.shape
    tm = min(T, 256)

    def body(dz_ref, dres_ref, h_ref, gain_ref, w_ref, dh_ref, dgain_ref):
        @pl.when(pl.program_id(0) == 0)
        def _():
            dgain_ref[...] = jnp.zeros_like(dgain_ref)

        dun = _dot_nt(dz_ref[:, 0:tn], w_ref[0])
        for s in range(1, nn):
            dun = dun + _dot_nt(dz_ref[:, s * tn:(s + 1) * tn], w_ref[s])
        hh = h_ref[...]
        r = _rms(hh)
        dgain_ref[...] += _col_sum(dun * hh * r)
        dh_ref[...] = dres_ref[...] + _rms_bwd(hh, r, dun * gain_ref[...])

    return pl.pallas_call(
        body, name="mixin_bwd", grid=(T // tm,),
        in_specs=[pl.BlockSpec((tm, nn * tn), lambda i: (i, 0)),
                  pl.BlockSpec((tm, D), lambda i: (i, 0)),
                  pl.BlockSpec((tm, D), lambda i: (i, 0)),
                  pl.BlockSpec((1, D), lambda i: (0, 0)),
                  pl.BlockSpec((nn, D, tn), lambda i: (0, 0, 0))],
        out_specs=[pl.BlockSpec((tm, D), lambda i: (i, 0)),
                   pl.BlockSpec((1, D), lambda i: (0, 0))],
        out_shape=[jax.ShapeDtypeStruct((T, D), F32), jax.ShapeDtypeStruct((1, D), F32)],
        compiler_params=_cparams(("arbitrary",), VMEM_LIMIT),
    )(dz, dh_res, h, gain, w_in)


def _loss_head(h, gain, target):
    T, D = h.shape
    tm = min(T, 1024)

    def body(h_ref, gain_ref, t_ref, loss_ref, dh_ref, dgain_ref):
        @pl.when(pl.program_id(0) == 0)
        def _():
            loss_ref[...] = jnp.zeros_like(loss_ref)
            dgain_ref[...] = jnp.zeros_like(dgain_ref)

        hh = h_ref[...]
        r = _rms(hh)
        e = hh * r * gain_ref[...] - t_ref[...]
        loss_ref[...] += (0.5 / D) * jnp.sum(e * e)
        dy = e * (1.0 / D)
        dgain_ref[...] += _col_sum(dy * hh * r)
        dh_ref[...] = _rms_bwd(hh, r, dy * gain_ref[...])

    return pl.pallas_call(
        body, name="loss_head", grid=(T // tm,),
        in_specs=[pl.BlockSpec((tm, D), lambda i: (i, 0)),
                  pl.BlockSpec((1, D), lambda i: (0, 0)),
                  pl.BlockSpec((tm, D), lambda i: (i, 0))],
        out_specs=[pl.BlockSpec((1, 128), lambda i: (0, 0)),
                   pl.BlockSpec((tm, D), lambda i: (i, 0)),
                   pl.BlockSpec((1, D), lambda i: (0, 0))],
        out_shape=[jax.ShapeDtypeStruct((1, 128), F32), jax.ShapeDtypeStruct((T, D), F32),
                   jax.ShapeDtypeStruct((1, D), F32)],
        compiler_params=_cparams(("arbitrary",), VMEM_LIMIT),
    )(h, gain, target)


def _adamw(parts, w, m, v, name):
    R, C = w.shape
    mult = 16 if parts.dtype == BF16 else 8
    tr = max(t for t in range(mult, min(R, 512) + 1, mult) if R % t == 0)
    c1 = 1.0 - ADAM_B1 ** ADAM_STEP
    c2 = 1.0 - ADAM_B2 ** ADAM_STEP

    def body(p_ref, w_ref, m_ref, v_ref, g_ref, d_ref, nm_ref, nv_ref):
        g = p_ref[0].astype(F32)
        for k in range(1, N_DEV):
            g = g + p_ref[k].astype(F32)
        mm = ADAM_B1 * m_ref[...] + (1.0 - ADAM_B1) * g
        vv = ADAM_B2 * v_ref[...] + (1.0 - ADAM_B2) * (g * g)
        g_ref[...] = g
        nm_ref[...] = mm
        nv_ref[...] = vv
        d_ref[...] = -ADAM_LR * ((mm / c1) / (jnp.sqrt(vv / c2) + ADAM_EPS) + ADAM_WD * w_ref[...])

    spec = pl.BlockSpec((tr, C), lambda i: (i, 0))
    return pl.pallas_call(
        body, name=name, grid=(R // tr,),
        in_specs=[pl.BlockSpec((N_DEV, tr, C), lambda i: (0, i, 0)), spec, spec, spec],
        out_specs=[spec, spec, spec, spec],
        out_shape=[jax.ShapeDtypeStruct((R, C), F32)] * 4,
        compiler_params=_cparams(("parallel",), VMEM_LIMIT),
    )(parts, w, m, v)


S5_NS = 256
S5_NH = 2
S5_NCB = 4
S5_RC = 512
S5_NQ = 4
S5_GROUP = 2


def _disc_math(a_re, a_im, log_dt, bt_re, bt_im):
    dt = jnp.exp(log_dt)
    zr, zi = a_re * dt, a_im * dt
    mag = jnp.exp(zr)
    lb_re, lb_im = mag * jnp.cos(zi), mag * jnp.sin(zi)
    den = a_re * a_re + a_im * a_im
    nr, ni = lb_re - 1.0, lb_im
    f_re = (nr * a_re + ni * a_im) / den
    f_im = (ni * a_re - nr * a_im) / den
    bb_re = f_re[:, None, :] * bt_re - f_im[:, None, :] * bt_im
    bb_im = f_re[:, None, :] * bt_im + f_im[:, None, :] * bt_re
    return lb_re, lb_im, bb_re, bb_im


def _disc_fwd(a_re, a_im, log_dt, bt_re, bt_im, chain_len, name):
    G, P = a_re.shape
    C = bt_re.shape[1]
    n_sq = int(round(math.log2(chain_len)))
    assert 2 ** n_sq == chain_len

    def body(a_re_ref, a_im_ref, ldt_ref, br_ref, bi_ref, lr_ref, li_ref, sr_ref, si_ref, bbr_ref, bbi_ref):
        lr, li, bbr, bbi = _disc_math(a_re_ref[...], a_im_ref[...], ldt_ref[...], br_ref[...], bi_ref[...])
        lr_ref[...] = lr
        li_ref[...] = li
        bbr_ref[...] = bbr
        bbi_ref[...] = bbi
        pr, pi = lr, li
        for _ in range(n_sq):
            pr, pi = pr * pr - pi * pi, 2.0 * pr * pi
        sr_ref[...] = pr
        si_ref[...] = pi

    s2 = jax.ShapeDtypeStruct((G, P), F32)
    s3 = jax.ShapeDtypeStruct((G, C, P), F32)
    return pl.pallas_call(body, name=name, out_shape=[s2, s2, s2, s2, s3, s3])(a_re, a_im, log_dt, bt_re, bt_im)


def _disc_bwd(a_re, a_im, log_dt, bt_re, bt_im, d_lr, d_li, d_bbr, d_bbi, name):
    G, P = a_re.shape
    C = bt_re.shape[1]

    def body(a_re_ref, a_im_ref, ldt_ref, br_ref, bi_ref, c1, c2, c3, c4, o1, o2, o3, o4, o5):
        _, vjp = jax.vjp(_disc_math, a_re_ref[...], a_im_ref[...], ldt_ref[...], br_ref[...], bi_ref[...])
        o1[...], o2[...], o3[...], o4[...], o5[...] = vjp((c1[...], c2[...], c3[...], c4[...]))

    s2 = jax.ShapeDtypeStruct((G, P), F32)
    s3 = jax.ShapeDtypeStruct((G, C, P), F32)
    return pl.pallas_call(body, name=name, out_shape=[s2, s2, jax.ShapeDtypeStruct((G, 1), F32), s3, s3])(
        a_re, a_im, log_dt, bt_re, bt_im, d_lr, d_li, d_bbr, d_bbi)


def _row_block(ib):
    return pl.ds(pl.multiple_of(ib * SCAN_LANES, SCAN_LANES), SCAN_LANES)


def _chain_block(j, i, ascending, n_blocks):
    at = j * (n_blocks // S5_NQ) + i
    return _row_block(jnp.where(ascending, at, n_blocks - 1 - at))


def _unrolled_loop(n, unroll, body, carry):
    trips = n // unroll
    carry = lax.fori_loop(
        0, trips, lambda t, c: functools.reduce(lambda cc, u: body(t * unroll + u, cc), range(unroll), c), carry)
    for i in range(trips * unroll, n):
        carry = body(i, carry)
    return carry


def _cmul_add(lr, li, sr, si, xr, xi):
    return lr * sr - li * si + xr, lr * si + li * sr + xi


def _scan(xr_ref, xi_ref, lr, li, init, ascending, n_blocks, store):
    steps = n_blocks // S5_NQ
    if not store:
        def step(i, carry):
            blocks = [_chain_block(j, i, ascending, n_blocks) for j in range(S5_NQ)]
            return tuple(_cmul_add(lr, li, sr, si, xr_ref[rows, :], xi_ref[rows, :])
                         for (sr, si), rows in zip(carry, blocks))

        return _unrolled_loop(steps, 4, step, init)

    group = S5_GROUP
    assert steps % group == 0

    def trip(t, carry):
        blocks = [[_chain_block(j, t * group + u, ascending, n_blocks) for j in range(S5_NQ)] for u in range(group)]
        xs = [[(xr_ref[rows, :], xi_ref[rows, :]) for rows in row] for row in blocks]
        states = list(carry)
        done = []
        for u in range(group):
            states = [_cmul_add(lr, li, sr, si, xr, xi) for (sr, si), (xr, xi) in zip(states, xs[u])]
            done.append(states)
        for u in range(group):
            for rows, (nr, ni) in zip(blocks[u], done[u]):
                xr_ref[rows, :] = nr
                xi_ref[rows, :] = ni
        return tuple(states)

    return lax.fori_loop(0, steps // group, trip, init)


def _segment_starts(w, lsr, lsi, ascending):
    shape = w[0][0].shape
    row = lax.broadcasted_iota(jnp.int32, shape, 0)
    keep = row != jnp.where(ascending, 0, SCAN_LANES - 1)

    def shift(t):
        t = jnp.where(ascending, pltpu.roll(t, 1, 0), pltpu.roll(t, SCAN_LANES - 1, 0))
        return jnp.where(keep, t, 0.0)

    zero = jnp.zeros(shape, F32)
    c = [(zero, zero)] * S5_NQ
    for _ in range(SCAN_LANES):
        tr, ti = _cmul_add(lsr, lsi, *c[-1], *w[-1])
        c[0] = (shift(tr), shift(ti))
        for j in range(1, S5_NQ):
            c[j] = _cmul_add(lsr, lsi, *c[j - 1], *w[j - 1])
    return tuple(c)


def _first_pass(xr_ref, xi_ref, lam_ref, ascending, n_blocks, conj):
    shape = (SCAN_LANES, xr_ref.shape[1])
    sign = -1.0 if conj else 1.0
    lr = jnp.broadcast_to(lam_ref[0:1, :], shape)
    li = sign * jnp.broadcast_to(lam_ref[1:2, :], shape)
    lsr = jnp.broadcast_to(lam_ref[2:3, :], shape)
    lsi = sign * jnp.broadcast_to(lam_ref[3:4, :], shape)
    zero = jnp.zeros(shape, F32)
    w = _scan(xr_ref, xi_ref, lr, li, ((zero, zero),) * S5_NQ, ascending, n_blocks, store=False)
    return _segment_starts(w, lsr, lsi, ascending), lr, li


def _s5_specs(T):
    NS = S5_NS
    tok = pl.BlockSpec((T, 128), lambda c, d, h: (0, c))
    b_spec = pl.BlockSpec((None, None, None, 128, NS), lambda c, d, h: (d, c, h, 0, 0))
    c_spec = pl.BlockSpec((None, None, None, NS, 128), lambda c, d, h: (d, c, h, 0, 0))
    lam_spec = pl.BlockSpec((None, None, None, 4, NS), lambda c, d, h: (d, c, h, 0, 0))
    return tok, b_spec, c_spec, lam_spec


def _s5_fwd(zp, bre, bim, lam, cre, cimn, comm=None):
    T = zp.shape[0]
    NS = S5_NS
    nb = T // SCAN_LANES
    rc = min(S5_RC, T)
    tok, b_spec, c_spec, lam_spec = _s5_specs(T)

    def body(zp_ref, bre_ref, bim_ref, lam_ref, cre_ref, cim_ref, y_ref, xr_ref, xi_ref):
        d = pl.program_id(1)
        ascending = d == 0

        @pl.when((d == 0) & (pl.program_id(2) == 0))
        def _():
            y_ref[...] = jnp.zeros_like(y_ref)

        def proj(c, _):
            rows = pl.ds(pl.multiple_of(c * rc, rc), rc)
            zz = zp_ref[rows, :]
            xr_ref[rows, :] = _dot(zz, bre_ref[...])
            xi_ref[rows, :] = _dot(zz, bim_ref[...])
            return 0

        lax.fori_loop(0, T // rc, proj, 0)
        starts, lr, li = _first_pass(xr_ref, xi_ref, lam_ref, ascending, nb, conj=False)
        _scan(xr_ref, xi_ref, lr, li, starts, ascending, nb, store=True)

        def outp(c, _):
            rows = pl.ds(pl.multiple_of(c * rc, rc), rc)
            y_ref[rows, :] += (_dot(xr_ref[rows, :].astype(BF16), cre_ref[...])
                               + _dot(xi_ref[rows, :].astype(BF16), cim_ref[...]))
            return 0

        lax.fori_loop(0, T // rc, outp, 0)

    return _pallas(
        body, "s5_fwd", (S5_NCB, 2, S5_NH),
        [tok, b_spec, b_spec, lam_spec, c_spec, c_spec], [tok],
        [jax.ShapeDtypeStruct((T, SSM_WIDTH), F32)],
        [pltpu.VMEM((T, NS), F32), pltpu.VMEM((T, NS), F32)],
        ("parallel", "arbitrary", "arbitrary"), (zp, bre, bim, lam, cre, cimn), comm)


def _s5_bwd(zp, dyp, bre, bim, lam, cre, cimn, comm=None):
    T = zp.shape[0]
    NS, NH = S5_NS, S5_NH
    nb = T // SCAN_LANES
    rc = min(S5_RC, T)
    tok, b_spec, c_spec, lam_spec = _s5_specs(T)
    dlam_spec = pl.BlockSpec((None, None, None, 2, NS), lambda c, d, h: (d, c, h, 0, 0))

    def body(zp_ref, dyp_ref, bre_ref, bim_ref, lam_ref, cre_ref, cim_ref,
             dzp_ref, dbre_ref, dbim_ref, dlam_ref, dcre_ref, dcim_ref,
             sr_ref, si_ref, gr_ref, gi_ref):
        d = pl.program_id(1)
        ascending = d == 0
        g_ascending = d != 0

        @pl.when((d == 0) & (pl.program_id(2) == 0))
        def _():
            dzp_ref[...] = jnp.zeros_like(dzp_ref)

        dcre_ref[...] = jnp.zeros_like(dcre_ref)
        dcim_ref[...] = jnp.zeros_like(dcim_ref)
        dbre_ref[...] = jnp.zeros_like(dbre_ref)
        dbim_ref[...] = jnp.zeros_like(dbim_ref)

        def proj(c, _):
            rows = pl.ds(pl.multiple_of(c * rc, rc), rc)
            zz = zp_ref[rows, :]
            sr_ref[rows, :] = _dot(zz, bre_ref[...])
            si_ref[rows, :] = _dot(zz, bim_ref[...])
            dy = dyp_ref[rows, :]
            gr_ref[rows, :] = _dot_nt(dy, cre_ref[...])
            gi_ref[rows, :] = _dot_nt(dy, cim_ref[...])
            return 0

        lax.fori_loop(0, T // rc, proj, 0)
        s_starts, lr, li = _first_pass(sr_ref, si_ref, lam_ref, ascending, nb, conj=False)
        _scan(sr_ref, si_ref, lr, li, s_starts, ascending, nb, store=True)
        g_starts, lr, lic = _first_pass(gr_ref, gi_ref, lam_ref, g_ascending, nb, conj=True)

        steps = nb // S5_NQ
        group = S5_GROUP
        assert steps % group == 0

        def gtrip(t, carry, last):
            g, (ar, ai) = carry
            first = t * group
            blocks = [[_chain_block(j, first + u, g_ascending, nb) for j in range(S5_NQ)] for u in range(group)]
            direct = [[(gr_ref[rows, :], gi_ref[rows, :]) for rows in row] for row in blocks]
            done = []
            for u in range(group):
                new = []
                for j, ((g_r, g_i), (d_r, d_i)) in enumerate(zip(g, direct[u])):
                    n_r, n_i = _cmul_add(lr, lic, g_r, g_i, d_r, d_i)
                    if last and u == group - 1:
                        s_r, s_i = s_starts[S5_NQ - 1 - j]
                    else:
                        prev = _chain_block(j, first + u + 1, g_ascending, nb)
                        s_r, s_i = sr_ref[prev, :], si_ref[prev, :]
                    ar = ar + n_r * s_r + n_i * s_i
                    ai = ai + n_i * s_r - n_r * s_i
                    new.append((n_r, n_i))
                g = new
                done.append(new)
            for u in range(group):
                for rows, (n_r, n_i) in zip(blocks[u], done[u]):
                    gr_ref[rows, :] = n_r
                    gi_ref[rows, :] = n_i
            return tuple(g), (ar, ai)

        zero = jnp.zeros((SCAN_LANES, NS), F32)
        carry = lax.fori_loop(0, steps // group - 1, lambda t, c: gtrip(t, c, False), (g_starts, (zero, zero)))
        _, (ar, ai) = gtrip(steps // group - 1, carry, True)
        dlam_ref[0:1, :] = _col_sum(ar)
        dlam_ref[1:2, :] = _col_sum(ai)

        def grads(c, _):
            rows = pl.ds(pl.multiple_of(c * rc, rc), rc)
            zz = zp_ref[rows, :]
            dy = dyp_ref[rows, :]
            g_rb = gr_ref[rows, :].astype(BF16)
            g_ib = gi_ref[rows, :].astype(BF16)
            dcre_ref[...] += _dot_tn(sr_ref[rows, :].astype(BF16), dy)
            dcim_ref[...] += _dot_tn(si_ref[rows, :].astype(BF16), dy)
            dbre_ref[...] += _dot_tn(zz, g_rb)
            dbim_ref[...] += _dot_tn(zz, g_ib)
            dzp_ref[rows, :] += _dot_nt(g_rb, bre_ref[...]) + _dot_nt(g_ib, bim_ref[...])
            return 0

        lax.fori_loop(0, T // rc, grads, 0)

    f32 = lambda *s: jax.ShapeDtypeStruct(s, F32)
    return _pallas(
        body, "s5_bwd", (S5_NCB, 2, S5_NH),
        [tok, tok, b_spec, b_spec, lam_spec, c_spec, c_spec],
        [tok, b_spec, b_spec, dlam_spec, c_spec, c_spec],
        [f32(T, SSM_WIDTH), f32(2, S5_NCB, NH, 128, NS), f32(2, S5_NCB, NH, 128, NS),
         f32(2, S5_NCB, NH, 2, NS), f32(2, S5_NCB, NH, NS, 128), f32(2, S5_NCB, NH, NS, 128)],
        [pltpu.VMEM((T, NS), F32)] * 4,
        ("parallel", "arbitrary", "arbitrary"), (zp, dyp, bre, bim, lam, cre, cimn), comm)


def _s5_delta():
    d = np.zeros((S5_NH, 8, 8 // S5_NH), np.float32)
    for h in range(S5_NH):
        for go in range(8 // S5_NH):
            d[h, h * (8 // S5_NH) + go, go] = 1.0
    return d


def _s5_pack_b(bbt):
    gh = 8 // S5_NH
    b5 = bbt.reshape(S5_NCB, S5_NH, gh, SSM_GROUP, SSM_STATE).transpose(0, 1, 3, 2, 4)
    m = b5[:, :, None] * _s5_delta()[None, :, :, None, :, None]
    return m.reshape(S5_NCB, S5_NH, 128, S5_NS)


def _s5_unpack_b(dm):
    gh = 8 // S5_NH
    d6 = dm.reshape(S5_NCB, S5_NH, 8, SSM_GROUP, gh, SSM_STATE)
    b5 = jnp.sum(d6 * _s5_delta()[None, :, :, None, :, None], axis=2)
    return b5.transpose(0, 1, 3, 2, 4).reshape(SSM_GROUPS, SSM_GROUP, SSM_STATE)


def _s5_pack_c(c):
    gh = 8 // S5_NH
    c5 = c.reshape(S5_NCB, S5_NH, gh, SSM_GROUP, SSM_STATE).transpose(0, 1, 2, 4, 3)
    m = c5[:, :, :, :, None, :] * _s5_delta().transpose(0, 2, 1)[None, :, :, None, :, None]
    return m.reshape(S5_NCB, S5_NH, S5_NS, 128)


def _s5_unpack_c(dm):
    gh = 8 // S5_NH
    d6 = dm.reshape(S5_NCB, S5_NH, gh, SSM_STATE, 8, SSM_GROUP)
    c5 = jnp.sum(d6 * _s5_delta().transpose(0, 2, 1)[None, :, :, None, :, None], axis=4)
    return c5.transpose(0, 1, 2, 4, 3).reshape(SSM_GROUPS, SSM_GROUP, SSM_STATE)


def _s5_pack_lam(x):
    return x.reshape(S5_NCB, S5_NH, S5_NS)


def _permute_rows(x):
    T = x.shape[0]
    return x.reshape(SCAN_LANES, T // SCAN_LANES, -1).transpose(1, 0, 2).reshape(T, -1)


def _unpermute_rows(x):
    T = x.shape[0]
    return x.reshape(T // SCAN_LANES, SCAN_LANES, -1).transpose(1, 0, 2).reshape(T, -1)


ATT_TB = ATT_ROWS * GRID_W
ATT_KB = 3 * ATT_TB


def _att_valid(i, n_rows):
    qi, kj = np.meshgrid(np.arange(ATT_TB), np.arange(ATT_KB), indexing="ij")
    r = i * ATT_ROWS + qi // GRID_W
    c = qi % GRID_W
    rk = (i - 1) * ATT_ROWS + kj // GRID_W
    x = kj % GRID_W
    rs = np.clip(r - WIN_H // 2, 0, n_rows - WIN_H)
    cs = np.clip(c - WIN_W // 2, 0, GRID_W - WIN_W)
    return (rk >= rs) & (rk < rs + WIN_H) & (x >= cs) & (x < cs + WIN_W)


def _att_masked_tables(table, n_rows):
    n = n_rows // ATT_ROWS
    assert n >= 3
    masks = np.stack([_att_valid(i, n_rows) for i in (0, 1, n - 1)])
    return jnp.where(masks[:, None], table[None], NEG_INF)


def _att_variant(i, n):
    return jnp.where(i == 0, 0, jnp.where(i >= n - 1, 2, 1))


def _att_exp(qh, kh, bias):
    s = _dot_nt(qh, kh) + bias
    return jnp.exp(s - jnp.max(s, axis=1, keepdims=True))


def _att_values_and_ones(vh):
    return jnp.concatenate([vh, jnp.ones_like(vh)], axis=1)


def _att_specs(n, col):
    last = n - 1
    cur = lambda i: (jnp.minimum(i, last), col)
    prv = lambda i: (jnp.maximum(jnp.minimum(i, last) - 1, 0), col)
    nxt = lambda i: (jnp.minimum(i + 1, last), col)
    blk = lambda f: pl.BlockSpec((ATT_TB, ATT_WIDTH), f)
    return blk(cur), blk(prv), blk(nxt)


def _att_fwd(zb, biasv):
    T = zb.shape[0]
    W = ATT_WIDTH
    n = T // ATT_TB
    n_rows = T // GRID_W
    cur = _att_specs(n, 0)[0]
    q_cur = _att_specs(n, 1)[0]
    k_cur, k_prv, k_nxt = _att_specs(n, 2)
    v_cur, v_prv, v_nxt = _att_specs(n, 3)

    def body(q_ref, kp_ref, kc_ref, kn_ref, vp_ref, vc_ref, vn_ref, b_ref, y_ref):
        qs = q_ref[...] * 0.125
        kb = jnp.concatenate([kp_ref[...], kc_ref[...], kn_ref[...]], axis=0)
        vb = jnp.concatenate([vp_ref[...], vc_ref[...], vn_ref[...]], axis=0)
        outs = []
        for h in range(ATT_HEADS):
            hs = slice(h * ATT_HEAD_DIM, (h + 1) * ATT_HEAD_DIM)
            e = _att_exp(qs[:, hs], kb[:, hs], b_ref[h]).astype(BF16)
            ov = _dot(e, _att_values_and_ones(vb[:, hs]))
            outs.append(ov[:, :ATT_HEAD_DIM] * (1.0 / ov[:, ATT_HEAD_DIM:ATT_HEAD_DIM + 1]))
        y_ref[...] = jnp.concatenate(outs, axis=1).astype(BF16)

    return pl.pallas_call(
        body, name="att_fwd", grid=(n,),
        in_specs=[q_cur, k_prv, k_cur, k_nxt, v_prv, v_cur, v_nxt,
                  pl.BlockSpec((None, ATT_HEADS, ATT_TB, ATT_KB), lambda i: (_att_variant(i, n), 0, 0, 0))],
        out_specs=cur,
        out_shape=jax.ShapeDtypeStruct((T, W), BF16),
        compiler_params=_cparams(("parallel",), VMEM_LIMIT),
    )(zb, zb, zb, zb, zb, zb, zb, biasv)


def _att_bwd(zb, y, do, biasv, comm=None):
    T = zb.shape[0]
    W = ATT_WIDTH
    n = T // ATT_TB
    n_rows = T // GRID_W
    cur = _att_specs(n, 0)[0]
    q_cur = _att_specs(n, 1)[0]
    k_cur, k_prv, k_nxt = _att_specs(n, 2)
    v_cur, v_prv, v_nxt = _att_specs(n, 3)
    done = pl.BlockSpec((ATT_TB, W), lambda i: (jnp.maximum(i - 1, 0), 0))
    bias_spec = pl.BlockSpec((None, ATT_HEADS, ATT_TB, ATT_KB), lambda i: (_att_variant(i, n), 0, 0, 0))

    def body(q_ref, y_ref, do_ref, kp_ref, kc_ref, kn_ref, vp_ref, vc_ref, vn_ref, b_ref,
             dq_ref, dk_ref, dv_ref, db_ref, acck_ref, accv_ref):
        i = pl.program_id(0)

        @pl.when(i == 0)
        def _():
            db_ref[...] = jnp.zeros_like(db_ref)
            acck_ref[...] = jnp.zeros_like(acck_ref)
            accv_ref[...] = jnp.zeros_like(accv_ref)

        @pl.when((i > 0) & (i < n))
        def _():
            slot = lax.rem(i + 1, 3)
            acck_ref[slot] = jnp.zeros((ATT_TB, W), F32)
            accv_ref[slot] = jnp.zeros((ATT_TB, W), F32)

        @pl.when(i < n)
        def _():
            qs = q_ref[...] * 0.125
            dob = do_ref[...]
            dy = dob.astype(F32) * y_ref[...].astype(F32)
            kb = jnp.concatenate([kp_ref[...], kc_ref[...], kn_ref[...]], axis=0)
            vb = jnp.concatenate([vp_ref[...], vc_ref[...], vn_ref[...]], axis=0)
            dqs, dks, dvs = [], [], []
            for h in range(ATT_HEADS):
                hs = slice(h * ATT_HEAD_DIM, (h + 1) * ATT_HEAD_DIM)
                qh, kh, vh, doh = qs[:, hs], kb[:, hs], vb[:, hs], dob[:, hs]
                e = _att_exp(qh, kh, b_ref[h])
                p = e * (1.0 / jnp.sum(e, axis=1, keepdims=True))
                dp = _dot_nt(doh, vh)
                ds = p * (dp - jnp.sum(dy[:, hs], axis=1, keepdims=True))
                db_ref[h] += ds
                dsb = ds.astype(BF16)
                dqs.append(_dot(dsb, kh) * 0.125)
                dks.append(_dot_tn(dsb, qh))
                dvs.append(_dot_tn(p.astype(BF16), doh))
            dq_ref[...] = jnp.concatenate(dqs, axis=1).astype(BF16)
            dk_all = jnp.concatenate(dks, axis=1)
            dv_all = jnp.concatenate(dvs, axis=1)
            for b in range(3):
                slot = lax.rem(i + 2 + b, 3)
                rows = slice(b * ATT_TB, (b + 1) * ATT_TB)
                acck_ref[slot] += dk_all[rows]
                accv_ref[slot] += dv_all[rows]

        slot = lax.rem(i + 2, 3)
        dk_ref[...] = acck_ref[slot].astype(BF16)
        dv_ref[...] = accv_ref[slot].astype(BF16)

    return _pallas(
        body, "att_bwd", (n + 1,),
        [q_cur, cur, cur, k_prv, k_cur, k_nxt, v_prv, v_cur, v_nxt, bias_spec],
        [cur, done, done, pl.BlockSpec((ATT_HEADS, ATT_TB, ATT_KB), lambda i: (0, 0, 0))],
        [jax.ShapeDtypeStruct((T, W), BF16)] * 3 + [jax.ShapeDtypeStruct((ATT_HEADS, ATT_TB, ATT_KB), F32)],
        [pltpu.VMEM((3, ATT_TB, W), F32), pltpu.VMEM((3, ATT_TB, W), F32)],
        ("arbitrary",), (zb, y, do, zb, zb, zb, zb, zb, zb, biasv), comm)


def _att_selectors():
    rsel = np.zeros((ATT_ROWS, 3 * ATT_ROWS, 2 * WIN_H - 1), np.float32)
    for a in range(ATT_ROWS):
        for b in range(3 * ATT_ROWS):
            rsel[a, b, b - a - ATT_ROWS + WIN_H - 1] = 1.0
    csel = np.zeros((GRID_W, GRID_W, 2 * WIN_W - 1), np.float32)
    for c in range(GRID_W):
        for x in range(GRID_W):
            csel[c, x, min(max(x - c, -(WIN_W - 1)), WIN_W - 1) + WIN_W - 1] = 1.0
    return rsel, csel


def _att_bias_table(rpb):
    rsel, csel = _att_selectors()
    hi = lax.Precision.HIGHEST
    t = jnp.einsum('hrd,abr->habd', rpb, rsel, precision=hi)
    t = jnp.einsum('habd,cxd->hacbx', t, csel, precision=hi)
    return t.reshape(ATT_HEADS, ATT_TB, ATT_KB)


def _att_bias_table_t(dtable):
    rsel, csel = _att_selectors()
    hi = lax.Precision.HIGHEST
    t = dtable.reshape(ATT_HEADS, ATT_ROWS, GRID_W, 3 * ATT_ROWS, GRID_W)
    t = jnp.einsum('hacbx,cxd->habd', t, csel, precision=hi)
    return jnp.einsum('habd,abr->hrd', t, rsel, precision=hi)


GELU_K = math.sqrt(2.0 / math.pi)
GELU_C = 0.044715
MERGE_TM = 256


def _gelu(x):
    return 0.5 * x * (1.0 + jnp.tanh(GELU_K * (x + GELU_C * x * x * x)))


def _gelu_grad(x):
    t = jnp.tanh(GELU_K * (x + GELU_C * x * x * x))
    return 0.5 * (1.0 + t) + 0.5 * x * (1.0 - t * t) * GELU_K * (1.0 + 3.0 * GELU_C * x * x)


def _merge_forward(ypre, zs, gs, ga, ya, ssm_d, w_glu, b_glu, w_bs, w_ba):
    ys = ypre + ssm_d * zs
    yg = _gelu(ys)
    sg = jax.nn.sigmoid(_dot(yg.astype(BF16), w_glu) + b_glu)
    y2 = yg * sg
    bs = _dot(y2.astype(BF16), w_bs)
    ba = _dot(ya, w_ba)
    s1 = jax.nn.sigmoid(gs)
    s2 = jax.nn.sigmoid(ga)
    merged = s1 * bs + s2 * ba
    return ys, yg, sg, y2, bs, ba, s1, s2, merged


def _merge_in_specs(D, W, tm):
    tok = lambda w, c: pl.BlockSpec((tm, w), lambda i: (i, c))
    full = lambda r, c: pl.BlockSpec((r, c), lambda i: (0, 0))
    z_specs = [tok(W, 0), tok(D, 4 * W // D), tok(D, 4 * W // D + 1)]
    w_specs = [full(1, W), full(W, W), full(1, W), full(W, D), full(W, D), full(D, D)]
    return tok, z_specs, w_specs


def _merge_fwd(ypre, z, ya, h1, ssm_d, w_glu, b_glu, w_bs, w_ba, w_out):
    T, D = h1.shape
    W = ypre.shape[1]
    tm = min(T, MERGE_TM)
    tok, z_specs, w_specs = _merge_in_specs(D, W, tm)

    def body(ypre_ref, zs_ref, gs_ref, ga_ref, ya_ref, h1_ref, d_ref, wglu_ref, bglu_ref, wbs_ref, wba_ref, wout_ref,
             h2_ref):
        merged = _merge_forward(ypre_ref[...], zs_ref[...], gs_ref[...], ga_ref[...], ya_ref[...], d_ref[...],
                                wglu_ref[...], bglu_ref[...], wbs_ref[...], wba_ref[...])[-1]
        h2_ref[...] = h1_ref[...] + _dot(merged.astype(BF16), wout_ref[...])

    return pl.pallas_call(
        body, name="merge_fwd", grid=(T // tm,),
        in_specs=[tok(W, 0)] + z_specs + [tok(W, 0), tok(D, 0)] + w_specs,
        out_specs=tok(D, 0),
        out_shape=jax.ShapeDtypeStruct((T, D), F32),
        compiler_params=_cparams(("parallel",), VMEM_LIMIT),
    )(ypre, z, z, z, ya, h1, ssm_d, w_glu, b_glu, w_bs, w_ba, w_out)


def _merge_bwd(dh2, ypre, z, ya, ssm_d, w_glu, b_glu, w_bs, w_ba, w_out):
    T, D = dh2.shape
    W = ypre.shape[1]
    tm = min(T, MERGE_TM)
    tok, z_specs, w_specs = _merge_in_specs(D, W, tm)

    def body(dh2_ref, ypre_ref, zs_ref, gs_ref, ga_ref, ya_ref, d_ref, wglu_ref, bglu_ref, wbs_ref, wba_ref, wout_ref,
             dypre_ref, dzs_ref, dgs_ref, dga_ref, dya_ref, dd_ref, dwglu_ref, dbglu_ref, dwbs_ref, dwba_ref, dwout_ref):
        @pl.when(pl.program_id(0) == 0)
        def _():
            for r in (dd_ref, dwglu_ref, dbglu_ref, dwbs_ref, dwba_ref, dwout_ref):
                r[...] = jnp.zeros_like(r)

        zs = zs_ref[...]
        ya = ya_ref[...]
        ys, yg, sg, y2, bs, ba, s1, s2, merged = _merge_forward(
            ypre_ref[...], zs, gs_ref[...], ga_ref[...], ya, d_ref[...],
            wglu_ref[...], bglu_ref[...], wbs_ref[...], wba_ref[...])
        dh2b = dh2_ref[...].astype(BF16)
        dmerged = _dot_nt(dh2b, wout_ref[...])
        dwout_ref[...] += _dot_tn(merged.astype(BF16), dh2b)
        dbs = (dmerged * s1).astype(BF16)
        dba = (dmerged * s2).astype(BF16)
        dgs_ref[...] = (dmerged * bs * s1 * (1.0 - s1)).astype(BF16)
        dga_ref[...] = (dmerged * ba * s2 * (1.0 - s2)).astype(BF16)
        dwbs_ref[...] += _dot_tn(y2.astype(BF16), dbs)
        dwba_ref[...] += _dot_tn(ya, dba)
        dya_ref[...] = _dot_nt(dba, wba_ref[...]).astype(BF16)
        dy2 = _dot_nt(dbs, wbs_ref[...])
        dvv = dy2 * yg * sg * (1.0 - sg)
        dvvb = dvv.astype(BF16)
        dyg = dy2 * sg + _dot_nt(dvvb, wglu_ref[...])
        dwglu_ref[...] += _dot_tn(yg.astype(BF16), dvvb)
        dbglu_ref[...] += _col_sum(dvv)
        dys = dyg * _gelu_grad(ys)
        dd_ref[...] += _col_sum(dys * zs)
        dzs_ref[...] = dys * d_ref[...]
        dypre_ref[...] = dys.astype(BF16)

    f32 = lambda *s: jax.ShapeDtypeStruct(s, F32)
    b16 = lambda *s: jax.ShapeDtypeStruct(s, BF16)
    return pl.pallas_call(
        body, name="merge_bwd", grid=(T // tm,),
        in_specs=[tok(D, 0), tok(W, 0)] + z_specs + [tok(W, 0)] + w_specs,
        out_specs=[tok(W, 0), tok(W, 0), tok(D, 0), tok(D, 0), tok(W, 0)] + w_specs,
        out_shape=[b16(T, W), f32(T, W), b16(T, D), b16(T, D), b16(T, W),
                   f32(1, W), f32(W, W), f32(1, W), f32(W, D), f32(W, D), f32(D, D)],
        compiler_params=_cparams(("arbitrary",), VMEM_LIMIT),
    )(dh2, ypre, z, z, z, ya, ssm_d, w_glu, b_glu, w_bs, w_ba, w_out)


def _cast_shards(weights):
    def body(*refs):
        n = len(refs) // 2
        for src, dst in zip(refs[:n], refs[n:]):
            dst[...] = src[0].astype(BF16)

    return pl.pallas_call(
        body, name="cast_shards",
        out_shape=[jax.ShapeDtypeStruct(w.shape[1:], BF16) for w in weights],
        compiler_params=_cparams(None, VMEM_LIMIT))(*weights)


def _gather_two_level(shards, name):
    n = len(shards)

    def body(*refs):
        x_refs, out_refs = refs[:n], refs[n:2 * n]
        send_sems, recv_sems, local_sems = refs[2 * n:]
        x, y, c = _my_place()
        me, sibling = (x, y, c), (x, y, 1 - c)
        chips = [(1 - x, y), (x, 1 - y), (1 - x, 1 - y)]

        def copy(a, k, block, to, own=False):
            slot = out_refs[a].at[_flat(*block)]
            return pltpu.make_async_remote_copy(
                src_ref=x_refs[a] if own else slot, dst_ref=slot,
                send_sem=send_sems.at[7 * a + k], recv_sem=recv_sems.at[7 * a + k],
                device_id=to, device_id_type=MESH_ID)

        sent, local = [], []
        for a in range(n):
            local.append(pltpu.make_async_copy(x_refs[a], out_refs[a].at[_flat(*me)], local_sems.at[a]))
            local[-1].start()
            sent.append(copy(a, 0, me, sibling, own=True))
            sent += [copy(a, 1 + j, me, (*chip, c), own=True) for j, chip in enumerate(chips)]
        for cp in sent:
            cp.start()
        for a in range(n):
            for j, chip in enumerate(chips):
                copy(a, 1 + j, (*chip, c), me).wait_recv()
                sent.append(copy(a, 4 + j, (*chip, c), sibling))
                sent[-1].start()
        for a in range(n):
            copy(a, 0, sibling, me).wait_recv()
            for j, chip in enumerate(chips):
                copy(a, 4 + j, (*chip, 1 - c), me).wait_recv()
        for cp in sent:
            cp.wait_send()
        for cp in local:
            cp.wait()

    return pl.pallas_call(
        body, name=name, in_specs=[_HBM] * n, out_specs=[_HBM] * n,
        out_shape=[jax.ShapeDtypeStruct((N_DEV,) + s.shape, s.dtype) for s in shards],
        scratch_shapes=[pltpu.SemaphoreType.DMA((7 * n,)), pltpu.SemaphoreType.DMA((7 * n,)),
                        pltpu.SemaphoreType.DMA((n,))],
    )(*shards)


PACK_COLS = 1024
BIG = (("ffn1_w_gate", 1), ("ffn1_w_up", 1), ("ffn1_w_down", 0), ("w_in", 1), ("ssm_w_glu", 0),
       ("w_branch_ssm", 1), ("w_branch_att", 1), ("w_out", 0),
       ("ffn2_w_gate", 1), ("ffn2_w_up", 1), ("ffn2_w_down", 0))
BIG_AXIS = dict(BIG)
TRANSPOSED = ("ffn1_w_gate", "ffn1_w_up", "ffn2_w_gate", "ffn2_w_up")
SSM_DIR = ("ssm_a_re", "ssm_a_im", "ssm_log_dt", "ssm_b_re", "ssm_b_im", "ssm_c_re", "ssm_c_im")
SMALL_EARLY = (("mix_norm",) + tuple(n + "_fwd" for n in SSM_DIR) + tuple(n + "_bwd" for n in SSM_DIR)
               + ("ssm_d", "ssm_b_glu", "att_rpb", "ffn2_norm", "final_norm"))
SMALL_LATE = ("ffn1_norm",)
WEIGHTS = ("ffn1_norm", "ffn1_w_gate", "ffn1_w_up", "ffn1_w_down", "mix_norm", "w_in") \
    + tuple(n + "_fwd" for n in SSM_DIR) + tuple(n + "_bwd" for n in SSM_DIR) \
    + ("ssm_d", "ssm_w_glu", "ssm_b_glu", "att_rpb", "w_branch_ssm", "w_branch_att", "w_out",
       "ffn2_norm", "ffn2_w_gate", "ffn2_w_up", "ffn2_w_down", "final_norm")


def _pad_rows(a, mult):
    pad = (-a.shape[-2]) % mult
    if pad:
        a = jnp.concatenate([a, jnp.zeros(a.shape[:-2] + (pad, a.shape[-1]), a.dtype)], axis=-2)
    return a


def _pack(arrays, row_mult):
    flat = jnp.concatenate([a.reshape(-1) for a in arrays])
    pad = (-flat.shape[0]) % PACK_COLS
    if pad:
        flat = jnp.concatenate([flat, jnp.zeros((pad,), flat.dtype)])
    return _pad_rows(flat.reshape(-1, PACK_COLS), row_mult)


def _unpack(slab, shapes):
    flat = slab.reshape(-1)
    out, at = [], 0
    for s in shapes:
        n = int(np.prod(s))
        out.append(flat[at:at + n].reshape(s))
        at += n
    return out


def _split_for_devices(g, axis):
    r, c = g.shape
    if axis == 1:
        return g.reshape(r, N_DEV, c // N_DEV).transpose(1, 0, 2).astype(BF16)
    return g.reshape(N_DEV, r // N_DEV, c).astype(BF16)


def _join_shards(gathered, axis):
    _, r, c = gathered.shape
    if axis == 1:
        return gathered.transpose(1, 0, 2).reshape(r, N_DEV * c)
    return gathered.reshape(N_DEV * r, c)


def _s5_direction_inputs(p, sfx, chain_len):
    bt_re = p["ssm_b_re" + sfx][0].transpose(0, 2, 1)
    bt_im = p["ssm_b_im" + sfx][0].transpose(0, 2, 1)
    raw = (p["ssm_a_re" + sfx][0], p["ssm_a_im" + sfx][0], p["ssm_log_dt" + sfx][0][:, None], bt_re, bt_im)
    lr, li, sr, si, bbr, bbi = _disc_fwd(*raw, chain_len,"s5_disc" + sfx)
    lam = jnp.stack([_s5_pack_lam(t) for t in (lr, li, sr, si)], axis=2)
    mats = (_s5_pack_b(bbr), _s5_pack_b(bbi), lam,
            _s5_pack_c(p["ssm_c_re" + sfx][0]), _s5_pack_c(-p["ssm_c_im" + sfx][0]))
    return raw, mats


def kernel(x, ffn1_norm, ffn1_w_gate, ffn1_w_up, ffn1_w_down, mix_norm, w_in, ssm_a_re_fwd, ssm_a_im_fwd, ssm_log_dt_fwd, ssm_b_re_fwd, ssm_b_im_fwd, ssm_c_re_fwd, ssm_c_im_fwd, ssm_a_re_bwd, ssm_a_im_bwd, ssm_log_dt_bwd, ssm_b_re_bwd, ssm_b_im_bwd, ssm_c_re_bwd, ssm_c_im_bwd, ssm_d, ssm_w_glu, ssm_b_glu, att_rpb, w_branch_ssm, w_branch_att, w_out, ffn2_norm, ffn2_w_gate, ffn2_w_up, ffn2_w_down, final_norm, loss_target, m_ffn1_norm, m_ffn1_w_gate, m_ffn1_w_up, m_ffn1_w_down, m_mix_norm, m_w_in, m_ssm_a_re_fwd, m_ssm_a_im_fwd, m_ssm_log_dt_fwd, m_ssm_b_re_fwd, m_ssm_b_im_fwd, m_ssm_c_re_fwd, m_ssm_c_im_fwd, m_ssm_a_re_bwd, m_ssm_a_im_bwd, m_ssm_log_dt_bwd, m_ssm_b_re_bwd, m_ssm_b_im_bwd, m_ssm_c_re_bwd, m_ssm_c_im_bwd, m_ssm_d, m_ssm_w_glu, m_ssm_b_glu, m_att_rpb, m_w_branch_ssm, m_w_branch_att, m_w_out, m_ffn2_norm, m_ffn2_w_gate, m_ffn2_w_up, m_ffn2_w_down, m_final_norm, v_ffn1_norm, v_ffn1_w_gate, v_ffn1_w_up, v_ffn1_w_down, v_mix_norm, v_w_in, v_ssm_a_re_fwd, v_ssm_a_im_fwd, v_ssm_log_dt_fwd, v_ssm_b_re_fwd, v_ssm_b_im_fwd, v_ssm_c_re_fwd, v_ssm_c_im_fwd, v_ssm_a_re_bwd, v_ssm_a_im_bwd, v_ssm_log_dt_bwd, v_ssm_b_re_bwd, v_ssm_b_im_bwd, v_ssm_c_re_bwd, v_ssm_c_im_bwd, v_ssm_d, v_ssm_w_glu, v_ssm_b_glu, v_att_rpb, v_w_branch_ssm, v_w_branch_att, v_w_out, v_ffn2_norm, v_ffn2_w_gate, v_ffn2_w_up, v_ffn2_w_down, v_final_norm):
    p = dict(locals())
    x = p["x"][0]
    target = p["loss_target"][0]
    T, D = x.shape

    stored = lambda a, n: jnp.swapaxes(a, -1, -2) if n in TRANSPOSED else a
    cut_axis = lambda n: 0 if n in TRANSPOSED else BIG_AXIS[n]
    shard = dict(zip([n for n, _ in BIG], _cast_shards([stored(p[n], n) for n, _ in BIG])))
    ffn1_w = ("ffn1_w_gate", "ffn1_w_up", "ffn1_w_down")
    mix_w = ("w_in", "ssm_w_glu", "w_branch_ssm", "w_branch_att", "w_out")
    ffn2_w = ("ffn2_w_gate", "ffn2_w_up", "ffn2_w_down")
    gathered = dict(zip(ffn1_w, _gather_two_level([shard[n] for n in ffn1_w], "gather_ffn1")))
    full = lambda n: _join_shards(gathered[n], cut_axis(n))

    h0 = x
    wg1, wu1, wd1 = [full(n) for n in ffn1_w]
    (h1, xn1, g1, u1), got = _ffn_fwd(h0, p["ffn1_norm"], wg1, wu1, wd1, "ffn1_fwd",
                                      _Comm("gather", [shard[n] for n in mix_w]))
    gathered.update(zip(mix_w, got))
    z, zb, un = _mixin_fwd(h1, p["mix_norm"], gathered["w_in"])
    W = SSM_WIDTH
    zp = _permute_rows(zb[:, :W])
    chain_len = T // SCAN_LANES // S5_NQ
    raw_f, mats_f = _s5_direction_inputs(p, "_fwd", chain_len)
    raw_b, mats_b = _s5_direction_inputs(p, "_bwd", chain_len)
    bre, bim, lam, cre, cimn = [jnp.stack([f, b]) for f, b in zip(mats_f, mats_b)]
    bre, bim, cre, cimn = [t.astype(BF16) for t in (bre, bim, cre, cimn)]
    (yp,), got = _s5_fwd(zp, bre, bim, lam, cre, cimn, _Comm("gather", [shard[n] for n in ffn2_w]))
    gathered.update(zip(ffn2_w, got))
    ypre = _unpermute_rows(yp)
    table = _att_masked_tables(_att_bias_table(p["att_rpb"][0]), T // GRID_W)
    ya = _att_fwd(zb, table)
    tail_w = (p["ssm_d"], full("ssm_w_glu"), p["ssm_b_glu"], full("w_branch_ssm"), full("w_branch_att"), full("w_out"))
    h2 = _merge_fwd(ypre, z, ya, h1, *tail_w)
    wg2, wu2, wd2 = [full(n) for n in ffn2_w]
    (h3, xn2, g2, u2), _ = _ffn_fwd(h2, p["ffn2_norm"], wg2, wu2, wd2, "ffn2_fwd")
    loss_part, dh3, d_final = _loss_head(h3, p["final_norm"][None], target)

    grads = {"final_norm": d_final[0]}
    to_send = lambda names: _Comm("exchange", [_split_for_devices(grads[n], cut_axis(n)) for n in names])
    parts = {}
    (dh2, grads["ffn2_norm"], do2, a2, dg2, du2), _ = _ffn_bwd(
        dh3, h2, p["ffn2_norm"], g2, u2, wg2, wu2, wd2, "ffn2_bwd")
    grads["ffn2_w_gate"] = _xty(dg2, xn2, "ffn2_dw_gate")
    grads["ffn2_w_up"] = _xty(du2, xn2, "ffn2_dw_up")
    grads["ffn2_w_down"] = _xty(a2, do2, "ffn2_dw_down")
    (dypre, dzs_skip, dgs, dga, dya, grads["ssm_d"], grads["ssm_w_glu"], grads["ssm_b_glu"],
     grads["w_branch_ssm"], grads["w_branch_att"], grads["w_out"]) = _merge_bwd(dh2, ypre, z, ya, *tail_w)
    (dq, dk, dv, dtable), got = _att_bwd(zb, ya, dya, table, to_send(ffn2_w))
    parts.update(zip(ffn2_w, got))
    grads["att_rpb"] = _att_bias_table_t(dtable)
    dyp = _permute_rows(dypre)
    tail_names = ("ssm_w_glu", "w_branch_ssm", "w_branch_att", "w_out")
    (dzp, dbre, dbim, dlam, dcre, dcimn), got = _s5_bwd(zp, dyp, bre, bim, lam, cre, cimn, to_send(tail_names))
    parts.update(zip(tail_names, got))
    G, P = SSM_GROUPS, SSM_STATE
    for d, (sfx, raw) in enumerate((("_fwd", raw_f), ("_bwd", raw_b))):
        da_re, da_im, dldt, dbt_re, dbt_im = _disc_bwd(
            *raw, dlam[d, :, :, 0, :].reshape(G, P), dlam[d, :, :, 1, :].reshape(G, P),
            _s5_unpack_b(dbre[d]), _s5_unpack_b(dbim[d]), "s5_disc_grad" + sfx)
        grads["ssm_a_re" + sfx] = da_re
        grads["ssm_a_im" + sfx] = da_im
        grads["ssm_log_dt" + sfx] = dldt[:, 0]
        grads["ssm_b_re" + sfx] = dbt_re.transpose(0, 2, 1)
        grads["ssm_b_im" + sfx] = dbt_im.transpose(0, 2, 1)
        grads["ssm_c_re" + sfx] = _s5_unpack_c(dcre[d])
        grads["ssm_c_im" + sfx] = -_s5_unpack_c(dcimn[d])
    dzs = _unpermute_rows(dzp) + dzs_skip
    dz = jnp.concatenate([dzs.astype(BF16), dq, dk, dv, dgs, dga], axis=1)
    dh1, grads["mix_norm"] = _mixin_bwd(dz, dh2, h1, p["mix_norm"], gathered["w_in"])
    grads["w_in"] = _xty(un, dz, "dw_in", col_shards=N_DEV)
    small_t = lambda a, n: jnp.swapaxes(a, -1, -2) if n.startswith("ssm_b_") else a
    pack_small = lambda names, src, pre: _pack([small_t(src[pre + n], n).astype(F32) for n in names], 8)
    early = _Comm(["exchange", "gather"],
                  [grads["w_in"], pack_small(SMALL_EARLY, grads, "")])
    (dh0, grads["ffn1_norm"], do1, a1, dg1, du1), (parts["w_in"], got_early) = _ffn_bwd(
        dh1, h0, p["ffn1_norm"], g1, u1, wg1, wu1, wd1, "ffn1_bwd", early)
    grads["ffn1_w_down"] = _xty(a1, do1, "ffn1_dw_down")
    grads["ffn1_w_gate"], (parts["ffn1_w_down"],) = _xty(dg1, xn1, "ffn1_dw_gate", to_send(("ffn1_w_down",)))
    grads["ffn1_w_up"], (parts["ffn1_w_gate"],) = _xty(du1, xn1, "ffn1_dw_up", to_send(("ffn1_w_gate",)))
    last = _Comm(["exchange", "gather"],
                 [_split_for_devices(grads["ffn1_w_up"], 0), pack_small(SMALL_LATE, grads, "")])
    parts["ffn1_w_up"], got_late = _comm_call(last, "exchange_last")
    got_small = jnp.concatenate([got_early, got_late], axis=1)

    results = {}
    for n, _ in BIG:
        outs = _adamw(parts[n], *[stored(p[pre + n][0], n) for pre in ("", "m_", "v_")], "adamw_" + n)
        results[n] = [stored(o, n)[None] for o in outs]
    early_rows = got_early.shape[1]
    slab = lambda pre: jnp.concatenate([pack_small(SMALL_EARLY, p, pre), pack_small(SMALL_LATE, p, pre)], axis=0)
    small_out = _adamw(got_small, slab(""), slab("m_"), slab("v_"), "adamw_small")
    for names, rows in ((SMALL_EARLY, slice(0, early_rows)), (SMALL_LATE, slice(early_rows, None))):
        shapes = [small_t(p[n], n).shape for n in names]
        for n, vals in zip(names, zip(*[_unpack(out[rows], shapes) for out in small_out])):
            results[n] = [small_t(val, n) for val in vals]

    loss = lax.psum(loss_part[0, 0], ("x", "y", "c"))
    out = [loss, dh0[None]]
    for kind in range(4):
        out += [results[n][kind] for n in WEIGHTS]
    return tuple(out)
```

```python
import functools
import math

import numpy as np
import jax
import jax.numpy as jnp
from jax import lax
from jax.experimental import pallas as pl
from jax.experimental.pallas import tpu as pltpu

F32 = jnp.float32
BF16 = jnp.bfloat16
MESH_ID = pl.DeviceIdType.MESH

SSM_GROUP = 16
SSM_GROUPS = 32
SSM_STATE = 64
SSM_WIDTH = 512
ATT_HEADS = 8
ATT_HEAD_DIM = 64
ATT_WIDTH = 512
GRID_W = 64
WIN_H = 8
WIN_W = 16
EPS = 1e-6
NEG_INF = -1e30
ADAM_LR = 0.001
ADAM_B1 = 0.9
ADAM_B2 = 0.999
ADAM_EPS = 1e-08
ADAM_WD = 0.01
ADAM_STEP = 10

N_DEV = 8
V7X_VMEM_BYTES = 64 * 1024 * 1024
VMEM_LIMIT = V7X_VMEM_BYTES - 8 * 1024 * 1024
SCAN_LANES = 8
ATT_ROWS = 4


def _cparams(sem, vmem=None):
    return pltpu.CompilerParams(dimension_semantics=sem, vmem_limit_bytes=vmem)


def _dot(a, b):
    return jnp.dot(a, b, preferred_element_type=F32)


def _dot_nt(a, b):
    return lax.dot_general(a, b, (((1,), (1,)), ((), ())), preferred_element_type=F32)


def _dot_tn(a, b):
    return lax.dot_general(a, b, (((0,), (0,)), ((), ())), preferred_element_type=F32)


def _rms(h):
    return lax.rsqrt(jnp.mean(h * h, axis=-1, keepdims=True) + EPS)


def _rms_bwd(h, r, v):
    return r * v - h * (r * r * r) * jnp.mean(h * v, axis=-1, keepdims=True)


def _col_sum(x):
    return jnp.sum(x, axis=0, keepdims=True)


def _my_place():
    return lax.axis_index("x"), lax.axis_index("y"), lax.axis_index("c")


def _flat(px, py, pc):
    return 4 * px + 2 * py + pc


class _Comm:
    def __init__(self, kind, arrays):
        self.arrays = list(arrays)
        self.n = len(self.arrays)
        self.kinds = [kind] * self.n if isinstance(kind, str) else list(kind)

    def out_shapes(self):
        return [jax.ShapeDtypeStruct((N_DEV,) + a.shape if k == "gather" else a.shape, a.dtype)
                for k, a in zip(self.kinds, self.arrays)]

    def scratch(self):
        return [pltpu.SemaphoreType.DMA((7 * self.n,)), pltpu.SemaphoreType.DMA((7 * self.n,)),
                pltpu.SemaphoreType.DMA((self.n,))]

    def run(self, srcs, dsts, sems, start):
        send_sems, recv_sems, local_sems = sems
        x, y, c = _my_place()
        mine = _flat(x, y, c)
        for a, (src, dst) in enumerate(zip(srcs, dsts)):
            whole = self.kinds[a] == "gather"
            local = pltpu.make_async_copy(src if whole else src.at[mine], dst.at[mine], local_sems.at[a])
            local.start() if start else local.wait()
            for k in range(1, N_DEV):
                px = 1 - x if k & 4 else x
                py = 1 - y if k & 2 else y
                pc = 1 - c if k & 1 else c
                cp = pltpu.make_async_remote_copy(
                    src_ref=src if whole else src.at[_flat(px, py, pc)], dst_ref=dst.at[mine],
                    send_sem=send_sems.at[7 * a + k - 1], recv_sem=recv_sems.at[7 * a + k - 1],
                    device_id=(px, py, pc), device_id_type=MESH_ID)
                cp.start() if start else cp.wait()


_HBM = pl.BlockSpec(memory_space=pltpu.HBM)


def _comm_call(comm, name):
    def body(*refs):
        srcs, dsts, sems = refs[:comm.n], refs[comm.n:2 * comm.n], refs[2 * comm.n:]
        comm.run(srcs, dsts, sems, True)
        comm.run(srcs, dsts, sems, False)

    return pl.pallas_call(body, name=name, in_specs=[_HBM] * comm.n, out_specs=[_HBM] * comm.n,
                          out_shape=comm.out_shapes(), scratch_shapes=comm.scratch())(*comm.arrays)


def _pallas(core, name, grid, in_specs, out_specs, out_shape, scratch, sem, args, comm=None):
    if comm is None:
        out = pl.pallas_call(core, name=name, grid=grid, in_specs=in_specs, out_specs=out_specs,
                             out_shape=out_shape, scratch_shapes=scratch,
                             compiler_params=_cparams(sem, VMEM_LIMIT))(*args)
        return out, []
    n_in, n_out, n_scr, n = len(in_specs), len(out_specs), len(scratch), comm.n

    def body(*refs):
        ins, srcs = refs[:n_in], refs[n_in:n_in + n]
        outs, dsts = refs[n_in + n:n_in + n + n_out], refs[n_in + n + n_out:n_in + 2 * n + n_out]
        scr, sems = refs[n_in + 2 * n + n_out:n_in + 2 * n + n_out + n_scr], refs[n_in + 2 * n + n_out + n_scr:]
        ids = [pl.program_id(k) for k in range(len(grid))]
        first = functools.reduce(lambda a, b: a & b, [i == 0 for i in ids])
        last = functools.reduce(lambda a, b: a & b, [i == g - 1 for i, g in zip(ids, grid)])

        @pl.when(first)
        def _():
            comm.run(srcs, dsts, sems, True)

        core(*ins, *outs, *scr)

        @pl.when(last)
        def _():
            comm.run(srcs, dsts, sems, False)

    out = pl.pallas_call(
        body, name=name, grid=grid, in_specs=list(in_specs) + [_HBM] * n, out_specs=list(out_specs) + [_HBM] * n,
        out_shape=list(out_shape) + comm.out_shapes(), scratch_shapes=list(scratch) + comm.scratch(),
        compiler_params=_cparams(("arbitrary",) * len(grid), VMEM_LIMIT))(*args, *comm.arrays)
    return out[:n_out], out[n_out:]


FFN_TM = 256


def _ffn_fwd(h, gain, wg, wu, wd, name, comm=None):
    T, D = h.shape
    F = wg.shape[0]
    tm = min(T, FFN_TM)
    once = pl.Buffered(1)

    def body(h_ref, gain_ref, wg_ref, wu_ref, wd_ref, ho_ref, xn_ref, g_ref, u_ref):
        hh = h_ref[...]
        xn = (hh * _rms(hh) * gain_ref[...]).astype(BF16)
        xn_ref[...] = xn
        g = _dot_nt(xn, wg_ref[...])
        u = _dot_nt(xn, wu_ref[...])
        g_ref[...] = g.astype(BF16)
        u_ref[...] = u.astype(BF16)
        a = (g * jax.nn.sigmoid(g) * u).astype(BF16)
        ho_ref[...] = hh + 0.5 * _dot(a, wd_ref[...])

    return _pallas(
        body, name, (T // tm,),
        [pl.BlockSpec((tm, D), lambda i: (i, 0)),
         pl.BlockSpec((1, D), lambda i: (0, 0)),
         pl.BlockSpec((F, D), lambda i: (0, 0), pipeline_mode=once),
         pl.BlockSpec((F, D), lambda i: (0, 0), pipeline_mode=once),
         pl.BlockSpec((F, D), lambda i: (0, 0), pipeline_mode=once)],
        [pl.BlockSpec((tm, D), lambda i: (i, 0)),
         pl.BlockSpec((tm, D), lambda i: (i, 0)),
         pl.BlockSpec((tm, F), lambda i: (i, 0)),
         pl.BlockSpec((tm, F), lambda i: (i, 0))],
        [jax.ShapeDtypeStruct((T, D), F32), jax.ShapeDtypeStruct((T, D), BF16),
         jax.ShapeDtypeStruct((T, F), BF16), jax.ShapeDtypeStruct((T, F), BF16)],
        [], ("parallel",), (h, gain, wg, wu, wd), comm)


def _ffn_bwd(dho, h, gain, g, u, wg, wu, wd, name, comm=None):
    T, D = h.shape
    F = wg.shape[0]
    tm = min(T, FFN_TM)
    tf = 1408 if F % 1408 == 0 else F
    once = pl.Buffered(1)

    def body(dho_ref, h_ref, gain_ref, g_ref, u_ref, wg_ref, wu_ref, wd_ref,
             dh_ref, dgain_ref, do_ref, a_ref, dg_ref, du_ref):
        @pl.when(pl.program_id(0) == 0)
        def _():
            dgain_ref[...] = jnp.zeros_like(dgain_ref)

        dho_v = dho_ref[...]
        do = (0.5 * dho_v).astype(BF16)
        do_ref[...] = do
        dxn = None
        for c in range(F // tf):
            cs = slice(c * tf, (c + 1) * tf)
            da = _dot_nt(do, wd_ref[cs, :])
            gg = g_ref[:, cs].astype(F32)
            uu = u_ref[:, cs].astype(F32)
            s = jax.nn.sigmoid(gg)
            sl = gg * s
            a_ref[:, cs] = (sl * uu).astype(BF16)
            dg = (da * uu * (s * (1.0 + gg * (1.0 - s)))).astype(BF16)
            du = (da * sl).astype(BF16)
            dg_ref[:, cs] = dg
            du_ref[:, cs] = du
            part = _dot(dg, wg_ref[cs, :]) + _dot(du, wu_ref[cs, :])
            dxn = part if dxn is None else dxn + part
        hh = h_ref[...]
        r = _rms(hh)
        dgain_ref[...] += _col_sum(dxn * hh * r)
        dh_ref[...] = dho_v + _rms_bwd(hh, r, dxn * gain_ref[...])

    tok = lambda w: pl.BlockSpec((tm, w), lambda i: (i, 0))
    row = pl.BlockSpec((1, D), lambda i: (0, 0))
    weight = pl.BlockSpec((F, D), lambda i: (0, 0), pipeline_mode=once)
    return _pallas(
        body, name, (T // tm,),
        [tok(D), tok(D), row, tok(F), tok(F), weight, weight, weight],
        [tok(D), row, tok(D), tok(F), tok(F), tok(F)],
        [jax.ShapeDtypeStruct((T, D), F32), jax.ShapeDtypeStruct((1, D), F32),
         jax.ShapeDtypeStruct((T, D), BF16), jax.ShapeDtypeStruct((T, F), BF16),
         jax.ShapeDtypeStruct((T, F), BF16), jax.ShapeDtypeStruct((T, F), BF16)],
        [], ("arbitrary",), (dho, h, gain, g, u, wg, wu, wd), comm)


def _xty(x, y, name, comm=None, col_shards=1):
    T, K = x.shape
    N = y.shape[1]
    tt = min(T, 1024)
    tk = K if K <= 1024 else (1408 if K % 1408 == 0 else K)
    tn = N if N <= 1024 else (1408 if N % 1408 == 0 else (1024 if N % 1024 == 0 else N))
    nt = T // tt
    ws = N // col_shards
    per = tn // ws if col_shards > 1 else 1
    assert col_shards == 1 or (tn % ws == 0 and ws % 128 == 0)

    def body(x_ref, y_ref, o_ref, acc_ref):
        t = pl.program_id(2)

        @pl.when(t == 0)
        def _():
            acc_ref[...] = jnp.zeros_like(acc_ref)

        acc_ref[...] += _dot_tn(x_ref[...], y_ref[...])

        @pl.when(t == nt - 1)
        def _():
            if col_shards == 1:
                o_ref[...] = acc_ref[...].astype(BF16)
            else:
                for s in range(per):
                    o_ref[s] = acc_ref[:, s * ws:(s + 1) * ws].astype(BF16)

    if col_shards == 1:
        out_spec = pl.BlockSpec((tk, tn), lambda k, n, t: (k, n))
        out_shape = jax.ShapeDtypeStruct((K, N), BF16)
    else:
        out_spec = pl.BlockSpec((per, tk, ws), lambda k, n, t: (n, k, 0))
        out_shape = jax.ShapeDtypeStruct((col_shards, K, ws), BF16)
    (out,), got = _pallas(
        body, name, (K // tk, N // tn, nt),
        [pl.BlockSpec((tt, tk), lambda k, n, t: (t, k)), pl.BlockSpec((tt, tn), lambda k, n, t: (t, n))],
        [out_spec], [out_shape], [pltpu.VMEM((tk, tn), F32)],
        ("parallel", "parallel", "arbitrary"), (x, y), comm)
    return out if comm is None else (out, got)


def _mixin_fwd(h, gain, w_in):
    T, D = h.shape
    nn, _, tn = w_in.shape
    N = nn * tn
    tm = min(T, 256)

    def body(h_ref, gain_ref, w_ref, z_ref, zb_ref, un_ref):
        hh = h_ref[...]
        un = (hh * _rms(hh) * gain_ref[...]).astype(BF16)
        un_ref[...] = un
        for s in range(nn):
            z = _dot(un, w_ref[s])
            z_ref[:, s * tn:(s + 1) * tn] = z
            zb_ref[:, s * tn:(s + 1) * tn] = z.astype(BF16)

    return pl.pallas_call(
        body, name="mixin_fwd", grid=(T // tm,),
        in_specs=[pl.BlockSpec((tm, D), lambda i: (i, 0)),
                  pl.BlockSpec((1, D), lambda i: (0, 0)),
                  pl.BlockSpec((nn, D, tn), lambda i: (0, 0, 0))],
        out_specs=[pl.BlockSpec((tm, N), lambda i: (i, 0)),
                   pl.BlockSpec((tm, N), lambda i: (i, 0)),
                   pl.BlockSpec((tm, D), lambda i: (i, 0))],
        out_shape=[jax.ShapeDtypeStruct((T, N), F32), jax.ShapeDtypeStruct((T, N), BF16),
                   jax.ShapeDtypeStruct((T, D), BF16)],
        compiler_params=_cparams(("parallel",), VMEM_LIMIT),
    )(h, gain, w_in)


def _mixin_bwd(dz, dh_res, h, gain, w_in):
    T, D = h.shape
    nn, _, tn = w_in.shape
    tm = min(T, 256)

    def body(dz_ref, dres_ref, h_ref, gain_ref, w_ref, dh_ref, dgain_ref):
        @pl.when(pl.program_id(0) == 0)
        def _():
            dgain_ref[...] = jnp.zeros_like(dgain_ref)

        dun = _dot_nt(dz_ref[:, 0:tn], w_ref[0])
        for s in range(1, nn):
            dun = dun + _dot_nt(dz_ref[:, s * tn:(s + 1) * tn], w_ref[s])
        hh = h_ref[...]
        r = _rms(hh)
        dgain_ref[...] += _col_sum(dun * hh * r)
        dh_ref[...] = dres_ref[...] + _rms_bwd(hh, r, dun * gain_ref[...])

    return pl.pallas_call(
        body, name="mixin_bwd", grid=(T // tm,),
        in_specs=[pl.BlockSpec((tm, nn * tn), lambda i: (i, 0)),
                  pl.BlockSpec((tm, D), lambda i: (i, 0)),
                  pl.BlockSpec((tm, D), lambda i: (i, 0)),
                  pl.BlockSpec((1, D), lambda i: (0, 0)),
                  pl.BlockSpec((nn, D, tn), lambda i: (0, 0, 0))],
        out_specs=[pl.BlockSpec((tm, D), lambda i: (i, 0)),
                   pl.BlockSpec((1, D), lambda i: (0, 0))],
        out_shape=[jax.ShapeDtypeStruct((T, D), F32), jax.ShapeDtypeStruct((1, D), F32)],
        compiler_params=_cparams(("arbitrary",), VMEM_LIMIT),
    )(dz, dh_res, h, gain, w_in)


def _loss_head(h, gain, target):
    T, D = h.shape
    tm = min(T, 1024)

    def body(h_ref, gain_ref, t_ref, loss_ref, dh_ref, dgain_ref):
        @pl.when(pl.program_id(0) == 0)
        def _():
            loss_ref[...] = jnp.zeros_like(loss_ref)
            dgain_ref[...] = jnp.zeros_like(dgain_ref)

        hh = h_ref[...]
        r = _rms(hh)
        e = hh * r * gain_ref[...] - t_ref[...]
        loss_ref[...] += (0.5 / D) * jnp.sum(e * e)
        dy = e * (1.0 / D)
        dgain_ref[...] += _col_sum(dy * hh * r)
        dh_ref[...] = _rms_bwd(hh, r, dy * gain_ref[...])

    return pl.pallas_call(
        body, name="loss_head", grid=(T // tm,),
        in_specs=[pl.BlockSpec((tm, D), lambda i: (i, 0)),
                  pl.BlockSpec((1, D), lambda i: (0, 0)),
                  pl.BlockSpec((tm, D), lambda i: (i, 0))],
        out_specs=[pl.BlockSpec((1, 128), lambda i: (0, 0)),
                   pl.BlockSpec((tm, D), lambda i: (i, 0)),
                   pl.BlockSpec((1, D), lambda i: (0, 0))],
        out_shape=[jax.ShapeDtypeStruct((1, 128), F32), jax.ShapeDtypeStruct((T, D), F32),
                   jax.ShapeDtypeStruct((1, D), F32)],
        compiler_params=_cparams(("arbitrary",), VMEM_LIMIT),
    )(h, gain, target)


def _adamw(parts, w, m, v, name):
    R, C = w.shape
    mult = 16 if parts.dtype == BF16 else 8
    tr = max(t for t in range(mult, min(R, 512) + 1, mult) if R % t == 0)
    c1 = 1.0 - ADAM_B1 ** ADAM_STEP
    c2 = 1.0 - ADAM_B2 ** ADAM_STEP

    def body(p_ref, w_ref, m_ref, v_ref, g_ref, d_ref, nm_ref, nv_ref):
        g = p_ref[0].astype(F32)
        for k in range(1, N_DEV):
            g = g + p_ref[k].astype(F32)
        mm = ADAM_B1 * m_ref[...] + (1.0 - ADAM_B1) * g
        vv = ADAM_B2 * v_ref[...] + (1.0 - ADAM_B2) * (g * g)
        g_ref[...] = g
        nm_ref[...] = mm
        nv_ref[...] = vv
        d_ref[...] = -ADAM_LR * ((mm / c1) / (jnp.sqrt(vv / c2) + ADAM_EPS) + ADAM_WD * w_ref[...])

    spec = pl.BlockSpec((tr, C), lambda i: (i, 0))
    return pl.pallas_call(
        body, name=name, grid=(R // tr,),
        in_specs=[pl.BlockSpec((N_DEV, tr, C), lambda i: (0, i, 0)), spec, spec, spec],
        out_specs=[spec, spec, spec, spec],
        out_shape=[jax.ShapeDtypeStruct((R, C), F32)] * 4,
        compiler_params=_cparams(("parallel",), VMEM_LIMIT),
    )(parts, w, m, v)


S5_NS = 256
S5_NH = 2
S5_NCB = 4
S5_RC = 512
S5_NQ = 4
S5_GROUP = 2


def _disc_math(a_re, a_im, log_dt, bt_re, bt_im):
    dt = jnp.exp(log_dt)
    zr, zi = a_re * dt, a_im * dt
    mag = jnp.exp(zr)
    lb_re, lb_im = mag * jnp.cos(zi), mag * jnp.sin(zi)
    den = a_re * a_re + a_im * a_im
    nr, ni = lb_re - 1.0, lb_im
    f_re = (nr * a_re + ni * a_im) / den
    f_im = (ni * a_re - nr * a_im) / den
    bb_re = f_re[:, None, :] * bt_re - f_im[:, None, :] * bt_im
    bb_im = f_re[:, None, :] * bt_im + f_im[:, None, :] * bt_re
    return lb_re, lb_im, bb_re, bb_im


def _disc_fwd(a_re, a_im, log_dt, bt_re, bt_im, chain_len, name):
    G, P = a_re.shape
    C = bt_re.shape[1]
    n_sq = int(round(math.log2(chain_len)))
    assert 2 ** n_sq == chain_len

    def body(a_re_ref, a_im_ref, ldt_ref, br_ref, bi_ref, lr_ref, li_ref, sr_ref, si_ref, bbr_ref, bbi_ref):
        lr, li, bbr, bbi = _disc_math(a_re_ref[...], a_im_ref[...], ldt_ref[...], br_ref[...], bi_ref[...])
        lr_ref[...] = lr
        li_ref[...] = li
        bbr_ref[...] = bbr
        bbi_ref[...] = bbi
        pr, pi = lr, li
        for _ in range(n_sq):
            pr, pi = pr * pr - pi * pi, 2.0 * pr * pi
        sr_ref[...] = pr
        si_ref[...] = pi

    s2 = jax.ShapeDtypeStruct((G, P), F32)
    s3 = jax.ShapeDtypeStruct((G, C, P), F32)
    return pl.pallas_call(body, name=name, out_shape=[s2, s2, s2, s2, s3, s3])(a_re, a_im, log_dt, bt_re, bt_im)


def _disc_bwd(a_re, a_im, log_dt, bt_re, bt_im, d_lr, d_li, d_bbr, d_bbi, name):
    G, P = a_re.shape
    C = bt_re.shape[1]

    def body(a_re_ref, a_im_ref, ldt_ref, br_ref, bi_ref, c1, c2, c3, c4, o1, o2, o3, o4, o5):
        _, vjp = jax.vjp(_disc_math, a_re_ref[...], a_im_ref[...], ldt_ref[...], br_ref[...], bi_ref[...])
        o1[...], o2[...], o3[...], o4[...], o5[...] = vjp((c1[...], c2[...], c3[...], c4[...]))

    s2 = jax.ShapeDtypeStruct((G, P), F32)
    s3 = jax.ShapeDtypeStruct((G, C, P), F32)
    return pl.pallas_call(body, name=name, out_shape=[s2, s2, jax.ShapeDtypeStruct((G, 1), F32), s3, s3])(
        a_re, a_im, log_dt, bt_re, bt_im, d_lr, d_li, d_bbr, d_bbi)


def _row_block(ib):
    return pl.ds(pl.multiple_of(ib * SCAN_LANES, SCAN_LANES), SCAN_LANES)


def _chain_block(j, i, ascending, n_blocks):
    at = j * (n_blocks // S5_NQ) + i
    return _row_block(jnp.where(ascending, at, n_blocks - 1 - at))


def _unrolled_loop(n, unroll, body, carry):
    trips = n // unroll
    carry = lax.fori_loop(
        0, trips, lambda t, c: functools.reduce(lambda cc, u: body(t * unroll + u, cc), range(unroll), c), carry)
    for i in range(trips * unroll, n):
        carry = body(i, carry)
    return carry


def _cmul_add(lr, li, sr, si, xr, xi):
    return lr * sr - li * si + xr, lr * si + li * sr + xi


def _scan(xr_ref, xi_ref, lr, li, init, ascending, n_blocks, store):
    steps = n_blocks // S5_NQ
    if not store:
        def step(i, carry):
            blocks = [_chain_block(j, i, ascending, n_blocks) for j in range(S5_NQ)]
            return tuple(_cmul_add(lr, li, sr, si, xr_ref[rows, :], xi_ref[rows, :])
                         for (sr, si), rows in zip(carry, blocks))

        return _unrolled_loop(steps, 4, step, init)

    group = S5_GROUP
    assert steps % group == 0

    def trip(t, carry):
        blocks = [[_chain_block(j, t * group + u, ascending, n_blocks) for j in range(S5_NQ)] for u in range(group)]
        xs = [[(xr_ref[rows, :], xi_ref[rows, :]) for rows in row] for row in blocks]
        states = list(carry)
        done = []
        for u in range(group):
            states = [_cmul_add(lr, li, sr, si, xr, xi) for (sr, si), (xr, xi) in zip(states, xs[u])]
            done.append(states)
        for u in range(group):
            for rows, (nr, ni) in zip(blocks[u], done[u]):
                xr_ref[rows, :] = nr
                xi_ref[rows, :] = ni
        return tuple(states)

    return lax.fori_loop(0, steps // group, trip, init)


def _segment_starts(w, lsr, lsi, ascending):
    shape = w[0][0].shape
    row = lax.broadcasted_iota(jnp.int32, shape, 0)
    keep = row != jnp.where(ascending, 0, SCAN_LANES - 1)

    def shift(t):
        t = jnp.where(ascending, pltpu.roll(t, 1, 0), pltpu.roll(t, SCAN_LANES - 1, 0))
        return jnp.where(keep, t, 0.0)

    zero = jnp.zeros(shape, F32)
    c = [(zero, zero)] * S5_NQ
    for _ in range(SCAN_LANES):
        tr, ti = _cmul_add(lsr, lsi, *c[-1], *w[-1])
        c[0] = (shift(tr), shift(ti))
        for j in range(1, S5_NQ):
            c[j] = _cmul_add(lsr, lsi, *c[j - 1], *w[j - 1])
    return tuple(c)


def _first_pass(xr_ref, xi_ref, lam_ref, ascending, n_blocks, conj):
    shape = (SCAN_LANES, xr_ref.shape[1])
    sign = -1.0 if conj else 1.0
    lr = jnp.broadcast_to(lam_ref[0:1, :], shape)
    li = sign * jnp.broadcast_to(lam_ref[1:2, :], shape)
    lsr = jnp.broadcast_to(lam_ref[2:3, :], shape)
    lsi = sign * jnp.broadcast_to(lam_ref[3:4, :], shape)
    zero = jnp.zeros(shape, F32)
    w = _scan(xr_ref, xi_ref, lr, li, ((zero, zero),) * S5_NQ, ascending, n_blocks, store=False)
    return _segment_starts(w, lsr, lsi, ascending), lr, li


def _s5_specs(T):
    NS = S5_NS
    tok = pl.BlockSpec((T, 128), lambda c, d, h: (0, c))
    b_spec = pl.BlockSpec((None, None, None, 128, NS), lambda c, d, h: (d, c, h, 0, 0))
    c_spec = pl.BlockSpec((None, None, None, NS, 128), lambda c, d, h: (d, c, h, 0, 0))
    lam_spec = pl.BlockSpec((None, None, None, 4, NS), lambda c, d, h: (d, c, h, 0, 0))
    return tok, b_spec, c_spec, lam_spec


def _s5_fwd(zp, bre, bim, lam, cre, cimn, comm=None):
    T = zp.shape[0]
    NS = S5_NS
    nb = T // SCAN_LANES
    rc = min(S5_RC, T)
    tok, b_spec, c_spec, lam_spec = _s5_specs(T)

    def body(zp_ref, bre_ref, bim_ref, lam_ref, cre_ref, cim_ref, y_ref, xr_ref, xi_ref):
        d = pl.program_id(1)
        ascending = d == 0

        @pl.when((d == 0) & (pl.program_id(2) == 0))
        def _():
            y_ref[...] = jnp.zeros_like(y_ref)

        def proj(c, _):
            rows = pl.ds(pl.multiple_of(c * rc, rc), rc)
            zz = zp_ref[rows, :]
            xr_ref[rows, :] = _dot(zz, bre_ref[...])
            xi_ref[rows, :] = _dot(zz, bim_ref[...])
            return 0

        lax.fori_loop(0, T // rc, proj, 0)
        starts, lr, li = _first_pass(xr_ref, xi_ref, lam_ref, ascending, nb, conj=False)
        _scan(xr_ref, xi_ref, lr, li, starts, ascending, nb, store=True)

        def outp(c, _):
            rows = pl.ds(pl.multiple_of(c * rc, rc), rc)
            y_ref[rows, :] += (_dot(xr_ref[rows, :].astype(BF16), cre_ref[...])
                               + _dot(xi_ref[rows, :].astype(BF16), cim_ref[...]))
            return 0

        lax.fori_loop(0, T // rc, outp, 0)

    return _pallas(
        body, "s5_fwd", (S5_NCB, 2, S5_NH),
        [tok, b_spec, b_spec, lam_spec, c_spec, c_spec], [tok],
        [jax.ShapeDtypeStruct((T, SSM_WIDTH), F32)],
        [pltpu.VMEM((T, NS), F32), pltpu.VMEM((T, NS), F32)],
        ("parallel", "arbitrary", "arbitrary"), (zp, bre, bim, lam, cre, cimn), comm)


def _s5_bwd(zp, dyp, bre, bim, lam, cre, cimn, comm=None):
    T = zp.shape[0]
    NS, NH = S5_NS, S5_NH
    nb = T // SCAN_LANES
    rc = min(S5_RC, T)
    tok, b_spec, c_spec, lam_spec = _s5_specs(T)
    dlam_spec = pl.BlockSpec((None, None, None, 2, NS), lambda c, d, h: (d, c, h, 0, 0))

    def body(zp_ref, dyp_ref, bre_ref, bim_ref, lam_ref, cre_ref, cim_ref,
             dzp_ref, dbre_ref, dbim_ref, dlam_ref, dcre_ref, dcim_ref,
             sr_ref, si_ref, gr_ref, gi_ref):
        d = pl.program_id(1)
        ascending = d == 0
        g_ascending = d != 0

        @pl.when((d == 0) & (pl.program_id(2) == 0))
        def _():
            dzp_ref[...] = jnp.zeros_like(dzp_ref)

        dcre_ref[...] = jnp.zeros_like(dcre_ref)
        dcim_ref[...] = jnp.zeros_like(dcim_ref)
        dbre_ref[...] = jnp.zeros_like(dbre_ref)
        dbim_ref[...] = jnp.zeros_like(dbim_ref)

        def proj(c, _):
            rows = pl.ds(pl.multiple_of(c * rc, rc), rc)
            zz = zp_ref[rows, :]
            sr_ref[rows, :] = _dot(zz, bre_ref[...])
            si_ref[rows, :] = _dot(zz, bim_ref[...])
            dy = dyp_ref[rows, :]
            gr_ref[rows, :] = _dot_nt(dy, cre_ref[...])
            gi_ref[rows, :] = _dot_nt(dy, cim_ref[...])
            return 0

        lax.fori_loop(0, T // rc, proj, 0)
        s_starts, lr, li = _first_pass(sr_ref, si_ref, lam_ref, ascending, nb, conj=False)
        _scan(sr_ref, si_ref, lr, li, s_starts, ascending, nb, store=True)
        g_starts, lr, lic = _first_pass(gr_ref, gi_ref, lam_ref, g_ascending, nb, conj=True)

        steps = nb // S5_NQ
        group = S5_GROUP
        assert steps % group == 0

        def gtrip(t, carry, last):
            g, (ar, ai) = carry
            first = t * group
            blocks = [[_chain_block(j, first + u, g_ascending, nb) for j in range(S5_NQ)] for u in range(group)]
            direct = [[(gr_ref[rows, :], gi_ref[rows, :]) for rows in row] for row in blocks]
            done = []
            for u in range(group):
                new = []
                for j, ((g_r, g_i), (d_r, d_i)) in enumerate(zip(g, direct[u])):
                    n_r, n_i = _cmul_add(lr, lic, g_r, g_i, d_r, d_i)
                    if last and u == group - 1:
                        s_r, s_i = s_starts[S5_NQ - 1 - j]
                    else:
                        prev = _chain_block(j, first + u + 1, g_ascending, nb)
                        s_r, s_i = sr_ref[prev, :], si_ref[prev, :]
                    ar = ar + n_r * s_r + n_i * s_i
                    ai = ai + n_i * s_r - n_r * s_i
                    new.append((n_r, n_i))
                g = new
                done.append(new)
            for u in range(group):
                for rows, (n_r, n_i) in zip(blocks[u], done[u]):
                    gr_ref[rows, :] = n_r
                    gi_ref[rows, :] = n_i
            return tuple(g), (ar, ai)

        zero = jnp.zeros((SCAN_LANES, NS), F32)
        carry = lax.fori_loop(0, steps // group - 1, lambda t, c: gtrip(t, c, False), (g_starts, (zero, zero)))
        _, (ar, ai) = gtrip(steps // group - 1, carry, True)
        dlam_ref[0:1, :] = _col_sum(ar)
        dlam_ref[1:2, :] = _col_sum(ai)

        def grads(c, _):
            rows = pl.ds(pl.multiple_of(c * rc, rc), rc)
            zz = zp_ref[rows, :]
            dy = dyp_ref[rows, :]
            g_rb = gr_ref[rows, :].astype(BF16)
            g_ib = gi_ref[rows, :].astype(BF16)
            dcre_ref[...] += _dot_tn(sr_ref[rows, :].astype(BF16), dy)
            dcim_ref[...] += _dot_tn(si_ref[rows, :].astype(BF16), dy)
            dbre_ref[...] += _dot_tn(zz, g_rb)
            dbim_ref[...] += _dot_tn(zz, g_ib)
            dzp_ref[rows, :] += _dot_nt(g_rb, bre_ref[...]) + _dot_nt(g_ib, bim_ref[...])
            return 0

        lax.fori_loop(0, T // rc, grads, 0)

    f32 = lambda *s: jax.ShapeDtypeStruct(s, F32)
    return _pallas(
        body, "s5_bwd", (S5_NCB, 2, S5_NH),
        [tok, tok, b_spec, b_spec, lam_spec, c_spec, c_spec],
        [tok, b_spec, b_spec, dlam_spec, c_spec, c_spec],
        [f32(T, SSM_WIDTH), f32(2, S5_NCB, NH, 128, NS), f32(2, S5_NCB, NH, 128, NS),
         f32(2, S5_NCB, NH, 2, NS), f32(2, S5_NCB, NH, NS, 128), f32(2, S5_NCB, NH, NS, 128)],
        [pltpu.VMEM((T, NS), F32)] * 4,
        ("parallel", "arbitrary", "arbitrary"), (zp, dyp, bre, bim, lam, cre, cimn), comm)


def _s5_delta():
    d = np.zeros((S5_NH, 8, 8 // S5_NH), np.float32)
    for h in range(S5_NH):
        for go in range(8 // S5_NH):
            d[h, h * (8 // S5_NH) + go, go] = 1.0
    return d


def _s5_pack_b(bbt):
    gh = 8 // S5_NH
    b5 = bbt.reshape(S5_NCB, S5_NH, gh, SSM_GROUP, SSM_STATE).transpose(0, 1, 3, 2, 4)
    m = b5[:, :, None] * _s5_delta()[None, :, :, None, :, None]
    return m.reshape(S5_NCB, S5_NH, 128, S5_NS)


def _s5_unpack_b(dm):
    gh = 8 // S5_NH
    d6 = dm.reshape(S5_NCB, S5_NH, 8, SSM_GROUP, gh, SSM_STATE)
    b5 = jnp.sum(d6 * _s5_delta()[None, :, :, None, :, None], axis=2)
    return b5.transpose(0, 1, 3, 2, 4).reshape(SSM_GROUPS, SSM_GROUP, SSM_STATE)


def _s5_pack_c(c):
    gh = 8 // S5_NH
    c5 = c.reshape(S5_NCB, S5_NH, gh, SSM_GROUP, SSM_STATE).transpose(0, 1, 2, 4, 3)
    m = c5[:, :, :, :, None, :] * _s5_delta().transpose(0, 2, 1)[None, :, :, None, :, None]
    return m.reshape(S5_NCB, S5_NH, S5_NS, 128)


def _s5_unpack_c(dm):
    gh = 8 // S5_NH
    d6 = dm.reshape(S5_NCB, S5_NH, gh, SSM_STATE, 8, SSM_GROUP)
    c5 = jnp.sum(d6 * _s5_delta().transpose(0, 2, 1)[None, :, :, None, :, None], axis=4)
    return c5.transpose(0, 1, 2, 4, 3).reshape(SSM_GROUPS, SSM_GROUP, SSM_STATE)


def _s5_pack_lam(x):
    return x.reshape(S5_NCB, S5_NH, S5_NS)


def _permute_rows(x):
    T = x.shape[0]
    return x.reshape(SCAN_LANES, T // SCAN_LANES, -1).transpose(1, 0, 2).reshape(T, -1)


def _unpermute_rows(x):
    T = x.shape[0]
    return x.reshape(T // SCAN_LANES, SCAN_LANES, -1).transpose(1, 0, 2).reshape(T, -1)


ATT_TB = ATT_ROWS * GRID_W
ATT_KB = 3 * ATT_TB


def _att_valid(i, n_rows):
    qi, kj = np.meshgrid(np.arange(ATT_TB), np.arange(ATT_KB), indexing="ij")
    r = i * ATT_ROWS + qi // GRID_W
    c = qi % GRID_W
    rk = (i - 1) * ATT_ROWS + kj // GRID_W
    x = kj % GRID_W
    rs = np.clip(r - WIN_H // 2, 0, n_rows - WIN_H)
    cs = np.clip(c - WIN_W // 2, 0, GRID_W - WIN_W)
    return (rk >= rs) & (rk < rs + WIN_H) & (x >= cs) & (x < cs + WIN_W)


def _att_masked_tables(table, n_rows):
    n = n_rows // ATT_ROWS
    assert n >= 3
    masks = np.stack([_att_valid(i, n_rows) for i in (0, 1, n - 1)])
    return jnp.where(masks[:, None], table[None], NEG_INF)


def _att_variant(i, n):
    return jnp.where(i == 0, 0, jnp.where(i >= n - 1, 2, 1))


def _att_exp(qh, kh, bias):
    s = _dot_nt(qh, kh) + bias
    return jnp.exp(s - jnp.max(s, axis=1, keepdims=True))


def _att_values_and_ones(vh):
    return jnp.concatenate([vh, jnp.ones_like(vh)], axis=1)


def _att_specs(n, col):
    last = n - 1
    cur = lambda i: (jnp.minimum(i, last), col)
    prv = lambda i: (jnp.maximum(jnp.minimum(i, last) - 1, 0), col)
    nxt = lambda i: (jnp.minimum(i + 1, last), col)
    blk = lambda f: pl.BlockSpec((ATT_TB, ATT_WIDTH), f)
    return blk(cur), blk(prv), blk(nxt)


def _att_fwd(zb, biasv):
    T = zb.shape[0]
    W = ATT_WIDTH
    n = T // ATT_TB
    n_rows = T // GRID_W
    cur = _att_specs(n, 0)[0]
    q_cur = _att_specs(n, 1)[0]
    k_cur, k_prv, k_nxt = _att_specs(n, 2)
    v_cur, v_prv, v_nxt = _att_specs(n, 3)

    def body(q_ref, kp_ref, kc_ref, kn_ref, vp_ref, vc_ref, vn_ref, b_ref, y_ref):
        qs = q_ref[...] * 0.125
        kb = jnp.concatenate([kp_ref[...], kc_ref[...], kn_ref[...]], axis=0)
        vb = jnp.concatenate([vp_ref[...], vc_ref[...], vn_ref[...]], axis=0)
        outs = []
        for h in range(ATT_HEADS):
            hs = slice(h * ATT_HEAD_DIM, (h + 1) * ATT_HEAD_DIM)
            e = _att_exp(qs[:, hs], kb[:, hs], b_ref[h]).astype(BF16)
            ov = _dot(e, _att_values_and_ones(vb[:, hs]))
            outs.append(ov[:, :ATT_HEAD_DIM] * (1.0 / ov[:, ATT_HEAD_DIM:ATT_HEAD_DIM + 1]))
        y_ref[...] = jnp.concatenate(outs, axis=1).astype(BF16)

    return pl.pallas_call(
        body, name="att_fwd", grid=(n,),
        in_specs=[q_cur, k_prv, k_cur, k_nxt, v_prv, v_cur, v_nxt,
                  pl.BlockSpec((None, ATT_HEADS, ATT_TB, ATT_KB), lambda i: (_att_variant(i, n), 0, 0, 0))],
        out_specs=cur,
        out_shape=jax.ShapeDtypeStruct((T, W), BF16),
        compiler_params=_cparams(("parallel",), VMEM_LIMIT),
    )(zb, zb, zb, zb, zb, zb, zb, biasv)


def _att_bwd(zb, y, do, biasv, comm=None):
    T = zb.shape[0]
    W = ATT_WIDTH
    n = T // ATT_TB
    n_rows = T // GRID_W
    cur = _att_specs(n, 0)[0]
    q_cur = _att_specs(n, 1)[0]
    k_cur, k_prv, k_nxt = _att_specs(n, 2)
    v_cur, v_prv, v_nxt = _att_specs(n, 3)
    done = pl.BlockSpec((ATT_TB, W), lambda i: (jnp.maximum(i - 1, 0), 0))
    bias_spec = pl.BlockSpec((None, ATT_HEADS, ATT_TB, ATT_KB), lambda i: (_att_variant(i, n), 0, 0, 0))

    def body(q_ref, y_ref, do_ref, kp_ref, kc_ref, kn_ref, vp_ref, vc_ref, vn_ref, b_ref,
             dq_ref, dk_ref, dv_ref, db_ref, acck_ref, accv_ref):
        i = pl.program_id(0)

        @pl.when(i == 0)
        def _():
            db_ref[...] = jnp.zeros_like(db_ref)
            acck_ref[...] = jnp.zeros_like(acck_ref)
            accv_ref[...] = jnp.zeros_like(accv_ref)

        @pl.when((i > 0) & (i < n))
        def _():
            slot = lax.rem(i + 1, 3)
            acck_ref[slot] = jnp.zeros((ATT_TB, W), F32)
            accv_ref[slot] = jnp.zeros((ATT_TB, W), F32)

        @pl.when(i < n)
        def _():
            qs = q_ref[...] * 0.125
            dob = do_ref[...]
            dy = dob.astype(F32) * y_ref[...].astype(F32)
            kb = jnp.concatenate([kp_ref[...], kc_ref[...], kn_ref[...]], axis=0)
            vb = jnp.concatenate([vp_ref[...], vc_ref[...], vn_ref[...]], axis=0)
            dqs, dks, dvs = [], [], []
            for h in range(ATT_HEADS):
                hs = slice(h * ATT_HEAD_DIM, (h + 1) * ATT_HEAD_DIM)
                qh, kh, vh, doh = qs[:, hs], kb[:, hs], vb[:, hs], dob[:, hs]
                e = _att_exp(qh, kh, b_ref[h])
                p = e * (1.0 / jnp.sum(e, axis=1, keepdims=True))
                dp = _dot_nt(doh, vh)
                ds = p * (dp - jnp.sum(dy[:, hs], axis=1, keepdims=True))
                db_ref[h] += ds
                dsb = ds.astype(BF16)
                dqs.append(_dot(dsb, kh) * 0.125)
                dks.append(_dot_tn(dsb, qh))
                dvs.append(_dot_tn(p.astype(BF16), doh))
            dq_ref[...] = jnp.concatenate(dqs, axis=1).astype(BF16)
            dk_all = jnp.concatenate(dks, axis=1)
            dv_all = jnp.concatenate(dvs, axis=1)
            for b in range(3):
                slot = lax.rem(i + 2 + b, 3)
                rows = slice(b * ATT_TB, (b + 1) * ATT_TB)
                acck_ref[slot] += dk_all[rows]
                accv_ref[slot] += dv_all[rows]

        slot = lax.rem(i + 2, 3)
        dk_ref[...] = acck_ref[slot].astype(BF16)
        dv_ref[...] = accv_ref[slot].astype(BF16)

    return _pallas(
        body, "att_bwd", (n + 1,),
        [q_cur, cur, cur, k_prv, k_cur, k_nxt, v_prv, v_cur, v_nxt, bias_spec],
        [cur, done, done, pl.BlockSpec((ATT_HEADS, ATT_TB, ATT_KB), lambda i: (0, 0, 0))],
        [jax.ShapeDtypeStruct((T, W), BF16)] * 3 + [jax.ShapeDtypeStruct((ATT_HEADS, ATT_TB, ATT_KB), F32)],
        [pltpu.VMEM((3, ATT_TB, W), F32), pltpu.VMEM((3, ATT_TB, W), F32)],
        ("arbitrary",), (zb, y, do, zb, zb, zb, zb, zb, zb, biasv), comm)


def _att_selectors():
    rsel = np.zeros((ATT_ROWS, 3 * ATT_ROWS, 2 * WIN_H - 1), np.float32)
    for a in range(ATT_ROWS):
        for b in range(3 * ATT_ROWS):
            rsel[a, b, b - a - ATT_ROWS + WIN_H - 1] = 1.0
    csel = np.zeros((GRID_W, GRID_W, 2 * WIN_W - 1), np.float32)
    for c in range(GRID_W):
        for x in range(GRID_W):
            csel[c, x, min(max(x - c, -(WIN_W - 1)), WIN_W - 1) + WIN_W - 1] = 1.0
    return rsel, csel


def _att_bias_table(rpb):
    rsel, csel = _att_selectors()
    hi = lax.Precision.HIGHEST
    t = jnp.einsum('hrd,abr->habd', rpb, rsel, precision=hi)
    t = jnp.einsum('habd,cxd->hacbx', t, csel, precision=hi)
    return t.reshape(ATT_HEADS, ATT_TB, ATT_KB)


def _att_bias_table_t(dtable):
    rsel, csel = _att_selectors()
    hi = lax.Precision.HIGHEST
    t = dtable.reshape(ATT_HEADS, ATT_ROWS, GRID_W, 3 * ATT_ROWS, GRID_W)
    t = jnp.einsum('hacbx,cxd->habd', t, csel, precision=hi)
    return jnp.einsum('habd,abr->hrd', t, rsel, precision=hi)


GELU_K = math.sqrt(2.0 / math.pi)
GELU_C = 0.044715
MERGE_TM = 256


def _gelu(x):
    return 0.5 * x * (1.0 + jnp.tanh(GELU_K * (x + GELU_C * x * x * x)))


def _gelu_grad(x):
    t = jnp.tanh(GELU_K * (x + GELU_C * x * x * x))
    return 0.5 * (1.0 + t) + 0.5 * x * (1.0 - t * t) * GELU_K * (1.0 + 3.0 * GELU_C * x * x)


def _merge_forward(ypre, zs, gs, ga, ya, ssm_d, w_glu, b_glu, w_bs, w_ba):
    ys = ypre + ssm_d * zs
    yg = _gelu(ys)
    sg = jax.nn.sigmoid(_dot(yg.astype(BF16), w_glu) + b_glu)
    y2 = yg * sg
    bs = _dot(y2.astype(BF16), w_bs)
    ba = _dot(ya, w_ba)
    s1 = jax.nn.sigmoid(gs)
    s2 = jax.nn.sigmoid(ga)
    merged = s1 * bs + s2 * ba
    return ys, yg, sg, y2, bs, ba, s1, s2, merged


def _merge_in_specs(D, W, tm):
    tok = lambda w, c: pl.BlockSpec((tm, w), lambda i: (i, c))
    full = lambda r, c: pl.BlockSpec((r, c), lambda i: (0, 0))
    z_specs = [tok(W, 0), tok(D, 4 * W // D), tok(D, 4 * W // D + 1)]
    w_specs = [full(1, W), full(W, W), full(1, W), full(W, D), full(W, D), full(D, D)]
    return tok, z_specs, w_specs


def _merge_fwd(ypre, z, ya, h1, ssm_d, w_glu, b_glu, w_bs, w_ba, w_out):
    T, D = h1.shape
    W = ypre.shape[1]
    tm = min(T, MERGE_TM)
    tok, z_specs, w_specs = _merge_in_specs(D, W, tm)

    def body(ypre_ref, zs_ref, gs_ref, ga_ref, ya_ref, h1_ref, d_ref, wglu_ref, bglu_ref, wbs_ref, wba_ref, wout_ref,
             h2_ref):
        merged = _merge_forward(ypre_ref[...], zs_ref[...], gs_ref[...], ga_ref[...], ya_ref[...], d_ref[...],
                                wglu_ref[...], bglu_ref[...], wbs_ref[...], wba_ref[...])[-1]
        h2_ref[...] = h1_ref[...] + _dot(merged.astype(BF16), wout_ref[...])

    return pl.pallas_call(
        body, name="merge_fwd", grid=(T // tm,),
        in_specs=[tok(W, 0)] + z_specs + [tok(W, 0), tok(D, 0)] + w_specs,
        out_specs=tok(D, 0),
        out_shape=jax.ShapeDtypeStruct((T, D), F32),
        compiler_params=_cparams(("parallel",), VMEM_LIMIT),
    )(ypre, z, z, z, ya, h1, ssm_d, w_glu, b_glu, w_bs, w_ba, w_out)


def _merge_bwd(dh2, ypre, z, ya, ssm_d, w_glu, b_glu, w_bs, w_ba, w_out):
    T, D = dh2.shape
    W = ypre.shape[1]
    tm = min(T, MERGE_TM)
    tok, z_specs, w_specs = _merge_in_specs(D, W, tm)

    def body(dh2_ref, ypre_ref, zs_ref, gs_ref, ga_ref, ya_ref, d_ref, wglu_ref, bglu_ref, wbs_ref, wba_ref, wout_ref,
             dypre_ref, dzs_ref, dgs_ref, dga_ref, dya_ref, dd_ref, dwglu_ref, dbglu_ref, dwbs_ref, dwba_ref, dwout_ref):
        @pl.when(pl.program_id(0) == 0)
        def _():
            for r in (dd_ref, dwglu_ref, dbglu_ref, dwbs_ref, dwba_ref, dwout_ref):
                r[...] = jnp.zeros_like(r)

        zs = zs_ref[...]
        ya = ya_ref[...]
        ys, yg, sg, y2, bs, ba, s1, s2, merged = _merge_forward(
            ypre_ref[...], zs, gs_ref[...], ga_ref[...], ya, d_ref[...],
            wglu_ref[...], bglu_ref[...], wbs_ref[...], wba_ref[...])
        dh2b = dh2_ref[...].astype(BF16)
        dmerged = _dot_nt(dh2b, wout_ref[...])
        dwout_ref[...] += _dot_tn(merged.astype(BF16), dh2b)
        dbs = (dmerged * s1).astype(BF16)
        dba = (dmerged * s2).astype(BF16)
        dgs_ref[...] = (dmerged * bs * s1 * (1.0 - s1)).astype(BF16)
        dga_ref[...] = (dmerged * ba * s2 * (1.0 - s2)).astype(BF16)
        dwbs_ref[...] += _dot_tn(y2.astype(BF16), dbs)
        dwba_ref[...] += _dot_tn(ya, dba)
        dya_ref[...] = _dot_nt(dba, wba_ref[...]).astype(BF16)
        dy2 = _dot_nt(dbs, wbs_ref[...])
        dvv = dy2 * yg * sg * (1.0 - sg)
        dvvb = dvv.astype(BF16)
        dyg = dy2 * sg + _dot_nt(dvvb, wglu_ref[...])
        dwglu_ref[...] += _dot_tn(yg.astype(BF16), dvvb)
        dbglu_ref[...] += _col_sum(dvv)
        dys = dyg * _gelu_grad(ys)
        dd_ref[...] += _col_sum(dys * zs)
        dzs_ref[...] = dys * d_ref[...]
        dypre_ref[...] = dys.astype(BF16)

    f32 = lambda *s: jax.ShapeDtypeStruct(s, F32)
    b16 = lambda *s: jax.ShapeDtypeStruct(s, BF16)
    return pl.pallas_call(
        body, name="merge_bwd", grid=(T // tm,),
        in_specs=[tok(D, 0), tok(W, 0)] + z_specs + [tok(W, 0)] + w_specs,
        out_specs=[tok(W, 0), tok(W, 0), tok(D, 0), tok(D, 0), tok(W, 0)] + w_specs,
        out_shape=[b16(T, W), f32(T, W), b16(T, D), b16(T, D), b16(T, W),
                   f32(1, W), f32(W, W), f32(1, W), f32(W, D), f32(W, D), f32(D, D)],
        compiler_params=_cparams(("arbitrary",), VMEM_LIMIT),
    )(dh2, ypre, z, z, z, ya, ssm_d, w_glu, b_glu, w_bs, w_ba, w_out)


def _cast_shards(weights):
    def body(*refs):
        n = len(refs) // 2
        for src, dst in zip(refs[:n], refs[n:]):
            dst[...] = src[0].astype(BF16)

    return pl.pallas_call(
        body, name="cast_shards",
        out_shape=[jax.ShapeDtypeStruct(w.shape[1:], BF16) for w in weights],
        compiler_params=_cparams(None, VMEM_LIMIT))(*weights)


def _gather_two_level(shards, name):
    n = len(shards)

    def body(*refs):
        x_refs, out_refs = refs[:n], refs[n:2 * n]
        send_sems, recv_sems, local_sems = refs[2 * n:]
        x, y, c = _my_place()
        me, sibling = (x, y, c), (x, y, 1 - c)
        chips = [(1 - x, y), (x, 1 - y), (1 - x, 1 - y)]

        def copy(a, k, block, to, own=False):
            slot = out_refs[a].at[_flat(*block)]
            return pltpu.make_async_remote_copy(
                src_ref=x_refs[a] if own else slot, dst_ref=slot,
                send_sem=send_sems.at[7 * a + k], recv_sem=recv_sems.at[7 * a + k],
                device_id=to, device_id_type=MESH_ID)

        sent, local = [], []
        for a in range(n):
            local.append(pltpu.make_async_copy(x_refs[a], out_refs[a].at[_flat(*me)], local_sems.at[a]))
            local[-1].start()
            sent.append(copy(a, 0, me, sibling, own=True))
            sent += [copy(a, 1 + j, me, (*chip, c), own=True) for j, chip in enumerate(chips)]
        for cp in sent:
            cp.start()
        for a in range(n):
            for j, chip in enumerate(chips):
                copy(a, 1 + j, (*chip, c), me).wait_recv()
                sent.append(copy(a, 4 + j, (*chip, c), sibling))
                sent[-1].start()
        for a in range(n):
            copy(a, 0, sibling, me).wait_recv()
            for j, chip in enumerate(chips):
                copy(a, 4 + j, (*chip, 1 - c), me).wait_recv()
        for cp in sent:
            cp.wait_send()
        for cp in local:
            cp.wait()

    return pl.pallas_call(
        body, name=name, in_specs=[_HBM] * n, out_specs=[_HBM] * n,
        out_shape=[jax.ShapeDtypeStruct((N_DEV,) + s.shape, s.dtype) for s in shards],
        scratch_shapes=[pltpu.SemaphoreType.DMA((7 * n,)), pltpu.SemaphoreType.DMA((7 * n,)),
                        pltpu.SemaphoreType.DMA((n,))],
    )(*shards)


PACK_COLS = 1024
BIG = (("ffn1_w_gate", 1), ("ffn1_w_up", 1), ("ffn1_w_down", 0), ("w_in", 1), ("ssm_w_glu", 0),
       ("w_branch_ssm", 1), ("w_branch_att", 1), ("w_out", 0),
       ("ffn2_w_gate", 1), ("ffn2_w_up", 1), ("ffn2_w_down", 0))
BIG_AXIS = dict(BIG)
TRANSPOSED = ("ffn1_w_gate", "ffn1_w_up", "ffn2_w_gate", "ffn2_w_up")
SSM_DIR = ("ssm_a_re", "ssm_a_im", "ssm_log_dt", "ssm_b_re", "ssm_b_im", "ssm_c_re", "ssm_c_im")
SMALL_EARLY = (("mix_norm",) + tuple(n + "_fwd" for n in SSM_DIR) + tuple(n + "_bwd" for n in SSM_DIR)
               + ("ssm_d", "ssm_b_glu", "att_rpb", "ffn2_norm", "final_norm"))
SMALL_LATE = ("ffn1_norm",)
WEIGHTS = ("ffn1_norm", "ffn1_w_gate", "ffn1_w_up", "ffn1_w_down", "mix_norm", "w_in") \
    + tuple(n + "_fwd" for n in SSM_DIR) + tuple(n + "_bwd" for n in SSM_DIR) \
    + ("ssm_d", "ssm_w_glu", "ssm_b_glu", "att_rpb", "w_branch_ssm", "w_branch_att", "w_out",
       "ffn2_norm", "ffn2_w_gate", "ffn2_w_up", "ffn2_w_down", "final_norm")


def _pad_rows(a, mult):
    pad = (-a.shape[-2]) % mult
    if pad:
        a = jnp.concatenate([a, jnp.zeros(a.shape[:-2] + (pad, a.shape[-1]), a.dtype)], axis=-2)
    return a


def _pack(arrays, row_mult):
    flat = jnp.concatenate([a.reshape(-1) for a in arrays])
    pad = (-flat.shape[0]) % PACK_COLS
    if pad:
        flat = jnp.concatenate([flat, jnp.zeros((pad,), flat.dtype)])
    return _pad_rows(flat.reshape(-1, PACK_COLS), row_mult)


def _unpack(slab, shapes):
    flat = slab.reshape(-1)
    out, at = [], 0
    for s in shapes:
        n = int(np.prod(s))
        out.append(flat[at:at + n].reshape(s))
        at += n
    return out


def _split_for_devices(g, axis):
    r, c = g.shape
    if axis == 1:
        return g.reshape(r, N_DEV, c // N_DEV).transpose(1, 0, 2).astype(BF16)
    return g.reshape(N_DEV, r // N_DEV, c).astype(BF16)


def _join_shards(gathered, axis):
    _, r, c = gathered.shape
    if axis == 1:
        return gathered.transpose(1, 0, 2).reshape(r, N_DEV * c)
    return gathered.reshape(N_DEV * r, c)


def _s5_direction_inputs(p, sfx, chain_len):
    bt_re = p["ssm_b_re" + sfx][0].transpose(0, 2, 1)
    bt_im = p["ssm_b_im" + sfx][0].transpose(0, 2, 1)
    raw = (p["ssm_a_re" + sfx][0], p["ssm_a_im" + sfx][0], p["ssm_log_dt" + sfx][0][:, None], bt_re, bt_im)
    lr, li, sr, si, bbr, bbi = _disc_fwd(*raw, chain_len,"s5_disc" + sfx)
    lam = jnp.stack([_s5_pack_lam(t) for t in (lr, li, sr, si)], axis=2)
    mats = (_s5_pack_b(bbr), _s5_pack_b(bbi), lam,
            _s5_pack_c(p["ssm_c_re" + sfx][0]), _s5_pack_c(-p["ssm_c_im" + sfx][0]))
    return raw, mats


def kernel(x, ffn1_norm, ffn1_w_gate, ffn1_w_up, ffn1_w_down, mix_norm, w_in, ssm_a_re_fwd, ssm_a_im_fwd, ssm_log_dt_fwd, ssm_b_re_fwd, ssm_b_im_fwd, ssm_c_re_fwd, ssm_c_im_fwd, ssm_a_re_bwd, ssm_a_im_bwd, ssm_log_dt_bwd, ssm_b_re_bwd, ssm_b_im_bwd, ssm_c_re_bwd, ssm_c_im_bwd, ssm_d, ssm_w_glu, ssm_b_glu, att_rpb, w_branch_ssm, w_branch_att, w_out, ffn2_norm, ffn2_w_gate, ffn2_w_up, ffn2_w_down, final_norm, loss_target, m_ffn1_norm, m_ffn1_w_gate, m_ffn1_w_up, m_ffn1_w_down, m_mix_norm, m_w_in, m_ssm_a_re_fwd, m_ssm_a_im_fwd, m_ssm_log_dt_fwd, m_ssm_b_re_fwd, m_ssm_b_im_fwd, m_ssm_c_re_fwd, m_ssm_c_im_fwd, m_ssm_a_re_bwd, m_ssm_a_im_bwd, m_ssm_log_dt_bwd, m_ssm_b_re_bwd, m_ssm_b_im_bwd, m_ssm_c_re_bwd, m_ssm_c_im_bwd, m_ssm_d, m_ssm_w_glu, m_ssm_b_glu, m_att_rpb, m_w_branch_ssm, m_w_branch_att, m_w_out, m_ffn2_norm, m_ffn2_w_gate, m_ffn2_w_up, m_ffn2_w_down, m_final_norm, v_ffn1_norm, v_ffn1_w_gate, v_ffn1_w_up, v_ffn1_w_down, v_mix_norm, v_w_in, v_ssm_a_re_fwd, v_ssm_a_im_fwd, v_ssm_log_dt_fwd, v_ssm_b_re_fwd, v_ssm_b_im_fwd, v_ssm_c_re_fwd, v_ssm_c_im_fwd, v_ssm_a_re_bwd, v_ssm_a_im_bwd, v_ssm_log_dt_bwd, v_ssm_b_re_bwd, v_ssm_b_im_bwd, v_ssm_c_re_bwd, v_ssm_c_im_bwd, v_ssm_d, v_ssm_w_glu, v_ssm_b_glu, v_att_rpb, v_w_branch_ssm, v_w_branch_att, v_w_out, v_ffn2_norm, v_ffn2_w_gate, v_ffn2_w_up, v_ffn2_w_down, v_final_norm):
    p = dict(locals())
    x = p["x"][0]
    target = p["loss_target"][0]
    T, D = x.shape

    stored = lambda a, n: jnp.swapaxes(a, -1, -2) if n in TRANSPOSED else a
    cut_axis = lambda n: 0 if n in TRANSPOSED else BIG_AXIS[n]
    shard = dict(zip([n for n, _ in BIG], _cast_shards([stored(p[n], n) for n, _ in BIG])))
    ffn1_w = ("ffn1_w_gate", "ffn1_w_up", "ffn1_w_down")
    mix_w = ("w_in", "ssm_w_glu", "w_branch_ssm", "w_branch_att", "w_out")
    ffn2_w = ("ffn2_w_gate", "ffn2_w_up", "ffn2_w_down")
    gathered = dict(zip(ffn1_w, _gather_two_level([shard[n] for n in ffn1_w], "gather_ffn1")))
    full = lambda n: _join_shards(gathered[n], cut_axis(n))

    h0 = x
    wg1, wu1, wd1 = [full(n) for n in ffn1_w]
    (h1, xn1, g1, u1), got = _ffn_fwd(h0, p["ffn1_norm"], wg1, wu1, wd1, "ffn1_fwd",
                                      _Comm("gather", [shard[n] for n in mix_w]))
    gathered.update(zip(mix_w, got))
    z, zb, un = _mixin_fwd(h1, p["mix_norm"], gathered["w_in"])
    W = SSM_WIDTH
    zp = _permute_rows(zb[:, :W])
    chain_len = T // SCAN_LANES // S5_NQ
    raw_f, mats_f = _s5_direction_inputs(p, "_fwd", chain_len)
    raw_b, mats_b = _s5_direction_inputs(p, "_bwd", chain_len)
    bre, bim, lam, cre, cimn = [jnp.stack([f, b]) for f, b in zip(mats_f, mats_b)]
    bre, bim, cre, cimn = [t.astype(BF16) for t in (bre, bim, cre, cimn)]
    (yp,), got = _s5_fwd(zp, bre, bim, lam, cre, cimn, _Comm("gather", [shard[n] for n in ffn2_w]))
    gathered.update(zip(ffn2_w, got))
    ypre = _unpermute_rows(yp)
    table = _att_masked_tables(_att_bias_table(p["att_rpb"][0]), T // GRID_W)
    ya = _att_fwd(zb, table)
    tail_w = (p["ssm_d"], full("ssm_w_glu"), p["ssm_b_glu"], full("w_branch_ssm"), full("w_branch_att"), full("w_out"))
    h2 = _merge_fwd(ypre, z, ya, h1, *tail_w)
    wg2, wu2, wd2 = [full(n) for n in ffn2_w]
    (h3, xn2, g2, u2), _ = _ffn_fwd(h2, p["ffn2_norm"], wg2, wu2, wd2, "ffn2_fwd")
    loss_part, dh3, d_final = _loss_head(h3, p["final_norm"][None], target)

    grads = {"final_norm": d_final[0]}
    to_send = lambda names: _Comm("exchange", [_split_for_devices(grads[n], cut_axis(n)) for n in names])
    parts = {}
    (dh2, grads["ffn2_norm"], do2, a2, dg2, du2), _ = _ffn_bwd(
        dh3, h2, p["ffn2_norm"], g2, u2, wg2, wu2, wd2, "ffn2_bwd")
    grads["ffn2_w_gate"] = _xty(dg2, xn2, "ffn2_dw_gate")
    grads["ffn2_w_up"] = _xty(du2, xn2, "ffn2_dw_up")
    grads["ffn2_w_down"] = _xty(a2, do2, "ffn2_dw_down")
    (dypre, dzs_skip, dgs, dga, dya, grads["ssm_d"], grads["ssm_w_glu"], grads["ssm_b_glu"],
     grads["w_branch_ssm"], grads["w_branch_att"], grads["w_out"]) = _merge_bwd(dh2, ypre, z, ya, *tail_w)
    (dq, dk, dv, dtable), got = _att_bwd(zb, ya, dya, table, to_send(ffn2_w))
    parts.update(zip(ffn2_w, got))
    grads["att_rpb"] = _att_bias_table_t(dtable)
    dyp = _permute_rows(dypre)
    tail_names = ("ssm_w_glu", "w_branch_ssm", "w_branch_att", "w_out")
    (dzp, dbre, dbim, dlam, dcre, dcimn), got = _s5_bwd(zp, dyp, bre, bim, lam, cre, cimn, to_send(tail_names))
    parts.update(zip(tail_names, got))
    G, P = SSM_GROUPS, SSM_STATE
    for d, (sfx, raw) in enumerate((("_fwd", raw_f), ("_bwd", raw_b))):
        da_re, da_im, dldt, dbt_re, dbt_im = _disc_bwd(
            *raw, dlam[d, :, :, 0, :].reshape(G, P), dlam[d, :, :, 1, :].reshape(G, P),
            _s5_unpack_b(dbre[d]), _s5_unpack_b(dbim[d]), "s5_disc_grad" + sfx)
        grads["ssm_a_re" + sfx] = da_re
        grads["ssm_a_im" + sfx] = da_im
        grads["ssm_log_dt" + sfx] = dldt[:, 0]
        grads["ssm_b_re" + sfx] = dbt_re.transpose(0, 2, 1)
        grads["ssm_b_im" + sfx] = dbt_im.transpose(0, 2, 1)
        grads["ssm_c_re" + sfx] = _s5_unpack_c(dcre[d])
        grads["ssm_c_im" + sfx] = -_s5_unpack_c(dcimn[d])
    dzs = _unpermute_rows(dzp) + dzs_skip
    dz = jnp.concatenate([dzs.astype(BF16), dq, dk, dv, dgs, dga], axis=1)
    dh1, grads["mix_norm"] = _mixin_bwd(dz, dh2, h1, p["mix_norm"], gathered["w_in"])
    grads["w_in"] = _xty(un, dz, "dw_in", col_shards=N_DEV)
    small_t = lambda a, n: jnp.swapaxes(a, -1, -2) if n.startswith("ssm_b_") else a
    pack_small = lambda names, src, pre: _pack([small_t(src[pre + n], n).astype(F32) for n in names], 8)
    early = _Comm(["exchange", "gather"],
                  [grads["w_in"], pack_small(SMALL_EARLY, grads, "")])
    (dh0, grads["ffn1_norm"], do1, a1, dg1, du1), (parts["w_in"], got_early) = _ffn_bwd(
        dh1, h0, p["ffn1_norm"], g1, u1, wg1, wu1, wd1, "ffn1_bwd", early)
    grads["ffn1_w_down"] = _xty(a1, do1, "ffn1_dw_down")
    grads["ffn1_w_gate"], (parts["ffn1_w_down"],) = _xty(dg1, xn1, "ffn1_dw_gate", to_send(("ffn1_w_down",)))
    grads["ffn1_w_up"], (parts["ffn1_w_gate"],) = _xty(du1, xn1, "ffn1_dw_up", to_send(("ffn1_w_gate",)))
    last = _Comm(["exchange", "gather"],
                 [_split_for_devices(grads["ffn1_w_up"], 0), pack_small(SMALL_LATE, grads, "")])
    parts["ffn1_w_up"], got_late = _comm_call(last, "exchange_last")
    got_small = jnp.concatenate([got_early, got_late], axis=1)

    results = {}
    for n, _ in BIG:
        outs = _adamw(parts[n], *[stored(p[pre + n][0], n) for pre in ("", "m_", "v_")], "adamw_" + n)
        results[n] = [stored(o, n)[None] for o in outs]
    early_rows = got_early.shape[1]
    slab = lambda pre: jnp.concatenate([pack_small(SMALL_EARLY, p, pre), pack_small(SMALL_LATE, p, pre)], axis=0)
    small_out = _adamw(got_small, slab(""), slab("m_"), slab("v_"), "adamw_small")
    for names, rows in ((SMALL_EARLY, slice(0, early_rows)), (SMALL_LATE, slice(early_rows, None))):
        shapes = [small_t(p[n], n).shape for n in names]
        for n, vals in zip(names, zip(*[_unpack(out[rows], shapes) for out in small_out])):
            results[n] = [small_t(val, n) for val in vals]

    loss = lax.psum(loss_part[0, 0], ("x", "y", "c"))
    out = [loss, dh0[None]]
    for kind in range(4):
        out += [results[n][kind] for n in WEIGHTS]
    return tuple(out)
```

```python
import functools
import math

import numpy as np
import jax
import jax.numpy as jnp
from jax import lax
from jax.experimental import pallas as pl
from jax.experimental.pallas import tpu as pltpu

F32 = jnp.float32
BF16 = jnp.bfloat16
MESH_ID = pl.DeviceIdType.MESH

SSM_GROUP = 16
SSM_GROUPS = 32
SSM_STATE = 64
SSM_WIDTH = 512
ATT_HEADS = 8
ATT_HEAD_DIM = 64
ATT_WIDTH = 512
GRID_W = 64
WIN_H = 8
WIN_W = 16
EPS = 1e-6
NEG_INF = -1e30
ADAM_LR = 0.001
ADAM_B1 = 0.9
ADAM_B2 = 0.999
ADAM_EPS = 1e-08
ADAM_WD = 0.01
ADAM_STEP = 10

N_DEV = 8
V7X_VMEM_BYTES = 64 * 1024 * 1024
VMEM_LIMIT = V7X_VMEM_BYTES - 8 * 1024 * 1024
SCAN_LANES = 8
ATT_ROWS = 4


def _cparams(sem, vmem=None):
    return pltpu.CompilerParams(dimension_semantics=sem, vmem_limit_bytes=vmem)


def _dot(a, b):
    return jnp.dot(a, b, preferred_element_type=F32)


def _dot_nt(a, b):
    return lax.dot_general(a, b, (((1,), (1,)), ((), ())), preferred_element_type=F32)


def _dot_tn(a, b):
    return lax.dot_general(a, b, (((0,), (0,)), ((), ())), preferred_element_type=F32)


def _rms(h):
    return lax.rsqrt(jnp.mean(h * h, axis=-1, keepdims=True) + EPS)


def _rms_bwd(h, r, v):
    return r * v - h * (r * r * r) * jnp.mean(h * v, axis=-1, keepdims=True)


def _col_sum(x):
    return jnp.sum(x, axis=0, keepdims=True)


def _my_place():
    return lax.axis_index("x"), lax.axis_index("y"), lax.axis_index("c")


def _flat(px, py, pc):
    return 4 * px + 2 * py + pc


class _Comm:
    def __init__(self, kind, arrays):
        self.arrays = list(arrays)
        self.n = len(self.arrays)
        self.kinds = [kind] * self.n if isinstance(kind, str) else list(kind)

    def out_shapes(self):
        return [jax.ShapeDtypeStruct((N_DEV,) + a.shape if k == "gather" else a.shape, a.dtype)
                for k, a in zip(self.kinds, self.arrays)]

    def scratch(self):
        return [pltpu.SemaphoreType.DMA((7 * self.n,)), pltpu.SemaphoreType.DMA((7 * self.n,)),
                pltpu.SemaphoreType.DMA((self.n,))]

    def run(self, srcs, dsts, sems, start):
        send_sems, recv_sems, local_sems = sems
        x, y, c = _my_place()
        mine = _flat(x, y, c)
        for a, (src, dst) in enumerate(zip(srcs, dsts)):
            whole = self.kinds[a] == "gather"
            local = pltpu.make_async_copy(src if whole else src.at[mine], dst.at[mine], local_sems.at[a])
            local.start() if start else local.wait()
            for k in range(1, N_DEV):
                px = 1 - x if k & 4 else x
                py = 1 - y if k & 2 else y
                pc = 1 - c if k & 1 else c
                cp = pltpu.make_async_remote_copy(
                    src_ref=src if whole else src.at[_flat(px, py, pc)], dst_ref=dst.at[mine],
                    send_sem=send_sems.at[7 * a + k - 1], recv_sem=recv_sems.at[7 * a + k - 1],
                    device_id=(px, py, pc), device_id_type=MESH_ID)
                cp.start() if start else cp.wait()


_HBM = pl.BlockSpec(memory_space=pltpu.HBM)


def _comm_call(comm, name):
    def body(*refs):
        srcs, dsts, sems = refs[:comm.n], refs[comm.n:2 * comm.n], refs[2 * comm.n:]
        comm.run(srcs, dsts, sems, True)
        comm.run(srcs, dsts, sems, False)

    return pl.pallas_call(body, name=name, in_specs=[_HBM] * comm.n, out_specs=[_HBM] * comm.n,
                          out_shape=comm.out_shapes(), scratch_shapes=comm.scratch())(*comm.arrays)


def _pallas(core, name, grid, in_specs, out_specs, out_shape, scratch, sem, args, comm=None):
    if comm is None:
        out = pl.pallas_call(core, name=name, grid=grid, in_specs=in_specs, out_specs=out_specs,
                             out_shape=out_shape, scratch_shapes=scratch,
                             compiler_params=_cparams(sem, VMEM_LIMIT))(*args)
        return out, []
    n_in, n_out, n_scr, n = len(in_specs), len(out_specs), len(scratch), comm.n

    def body(*refs):
        ins, srcs = refs[:n_in], refs[n_in:n_in + n]
        outs, dsts = refs[n_in + n:n_in + n + n_out], refs[n_in + n + n_out:n_in + 2 * n + n_out]
        scr, sems = refs[n_in + 2 * n + n_out:n_in + 2 * n + n_out + n_scr], refs[n_in + 2 * n + n_out + n_scr:]
        ids = [pl.program_id(k) for k in range(len(grid))]
        first = functools.reduce(lambda a, b: a & b, [i == 0 for i in ids])
        last = functools.reduce(lambda a, b: a & b, [i == g - 1 for i, g in zip(ids, grid)])

        @pl.when(first)
        def _():
            comm.run(srcs, dsts, sems, True)

        core(*ins, *outs, *scr)

        @pl.when(last)
        def _():
            comm.run(srcs, dsts, sems, False)

    out = pl.pallas_call(
        body, name=name, grid=grid, in_specs=list(in_specs) + [_HBM] * n, out_specs=list(out_specs) + [_HBM] * n,
        out_shape=list(out_shape) + comm.out_shapes(), scratch_shapes=list(scratch) + comm.scratch(),
        compiler_params=_cparams(("arbitrary",) * len(grid), VMEM_LIMIT))(*args, *comm.arrays)
    return out[:n_out], out[n_out:]


FFN_TM = 256


def _ffn_fwd(h, gain, wg, wu, wd, name, comm=None):
    T, D = h.shape
    F = wg.shape[0]
    tm = min(T, FFN_TM)
    once = pl.Buffered(1)

    def body(h_ref, gain_ref, wg_ref, wu_ref, wd_ref, ho_ref, xn_ref, g_ref, u_ref):
        hh = h_ref[...]
        xn = (hh * _rms(hh) * gain_ref[...]).astype(BF16)
        xn_ref[...] = xn
        g = _dot_nt(xn, wg_ref[...])
        u = _dot_nt(xn, wu_ref[...])
        g_ref[...] = g.astype(BF16)
        u_ref[...] = u.astype(BF16)
        a = (g * jax.nn.sigmoid(g) * u).astype(BF16)
        ho_ref[...] = hh + 0.5 * _dot(a, wd_ref[...])

    return _pallas(
        body, name, (T // tm,),
        [pl.BlockSpec((tm, D), lambda i: (i, 0)),
         pl.BlockSpec((1, D), lambda i: (0, 0)),
         pl.BlockSpec((F, D), lambda i: (0, 0), pipeline_mode=once),
         pl.BlockSpec((F, D), lambda i: (0, 0), pipeline_mode=once),
         pl.BlockSpec((F, D), lambda i: (0, 0), pipeline_mode=once)],
        [pl.BlockSpec((tm, D), lambda i: (i, 0)),
         pl.BlockSpec((tm, D), lambda i: (i, 0)),
         pl.BlockSpec((tm, F), lambda i: (i, 0)),
         pl.BlockSpec((tm, F), lambda i: (i, 0))],
        [jax.ShapeDtypeStruct((T, D), F32), jax.ShapeDtypeStruct((T, D), BF16),
         jax.ShapeDtypeStruct((T, F), BF16), jax.ShapeDtypeStruct((T, F), BF16)],
        [], ("parallel",), (h, gain, wg, wu, wd), comm)


def _ffn_bwd(dho, h, gain, g, u, wg, wu, wd, name, comm=None):
    T, D = h.shape
    F = wg.shape[0]
    tm = min(T, FFN_TM)
    tf = 1408 if F % 1408 == 0 else F
    once = pl.Buffered(1)

    def body(dho_ref, h_ref, gain_ref, g_ref, u_ref, wg_ref, wu_ref, wd_ref,
             dh_ref, dgain_ref, do_ref, a_ref, dg_ref, du_ref):
        @pl.when(pl.program_id(0) == 0)
        def _():
            dgain_ref[...] = jnp.zeros_like(dgain_ref)

        dho_v = dho_ref[...]
        do = (0.5 * dho_v).astype(BF16)
        do_ref[...] = do
        dxn = None
        for c in range(F // tf):
            cs = slice(c * tf, (c + 1) * tf)
            da = _dot_nt(do, wd_ref[cs, :])
            gg = g_ref[:, cs].astype(F32)
            uu = u_ref[:, cs].astype(F32)
            s = jax.nn.sigmoid(gg)
            sl = gg * s
            a_ref[:, cs] = (sl * uu).astype(BF16)
            dg = (da * uu * (s * (1.0 + gg * (1.0 - s)))).astype(BF16)
            du = (da * sl).astype(BF16)
            dg_ref[:, cs] = dg
            du_ref[:, cs] = du
            part = _dot(dg, wg_ref[cs, :]) + _dot(du, wu_ref[cs, :])
            dxn = part if dxn is None else dxn + part
        hh = h_ref[...]
        r = _rms(hh)
        dgain_ref[...] += _col_sum(dxn * hh * r)
        dh_ref[...] = dho_v + _rms_bwd(hh, r, dxn * gain_ref[...])

    tok = lambda w: pl.BlockSpec((tm, w), lambda i: (i, 0))
    row = pl.BlockSpec((1, D), lambda i: (0, 0))
    weight = pl.BlockSpec((F, D), lambda i: (0, 0), pipeline_mode=once)
    return _pallas(
        body, name, (T // tm,),
        [tok(D), tok(D), row, tok(F), tok(F), weight, weight, weight],
        [tok(D), row, tok(D), tok(F), tok(F), tok(F)],
        [jax.ShapeDtypeStruct((T, D), F32), jax.ShapeDtypeStruct((1, D), F32),
         jax.ShapeDtypeStruct((T, D), BF16), jax.ShapeDtypeStruct((T, F), BF16),
         jax.ShapeDtypeStruct((T, F), BF16), jax.ShapeDtypeStruct((T, F), BF16)],
        [], ("arbitrary",), (dho, h, gain, g, u, wg, wu, wd), comm)


def _xty(x, y, name, comm=None, col_shards=1):
    T, K = x.shape
    N = y.shape[1]
    tt = min(T, 2048)
    tk = K if K <= 1024 else (1408 if K % 1408 == 0 else K)
    tn = N if N <= 1024 else (1408 if N % 1408 == 0 else (1024 if N % 1024 == 0 else N))
    nt = T // tt
    ws = N // col_shards
    per = tn // ws if col_shards > 1 else 1
    assert col_shards == 1 or (tn % ws == 0 and ws % 128 == 0)

    def body(x_ref, y_ref, o_ref, acc_ref):
        t = pl.program_id(2)

        @pl.when(t == 0)
        def _():
            acc_ref[...] = jnp.zeros_like(acc_ref)

        acc_ref[...] += _dot_tn(x_ref[...], y_ref[...])

        @pl.when(t == nt - 1)
        def _():
            if col_shards == 1:
                o_ref[...] = acc_ref[...].astype(BF16)
            else:
                for s in range(per):
                    o_ref[s] = acc_ref[:, s * ws:(s + 1) * ws].astype(BF16)

    if col_shards == 1:
        out_spec = pl.BlockSpec((tk, tn), lambda k, n, t: (k, n))
        out_shape = jax.ShapeDtypeStruct((K, N), BF16)
    else:
        out_spec = pl.BlockSpec((per, tk, ws), lambda k, n, t: (n, k, 0))
        out_shape = jax.ShapeDtypeStruct((col_shards, K, ws), BF16)
    (out,), got = _pallas(
        body, name, (K // tk, N // tn, nt),
        [pl.BlockSpec((tt, tk), lambda k, n, t: (t, k)), pl.BlockSpec((tt, tn), lambda k, n, t: (t, n))],
        [out_spec], [out_shape], [pltpu.VMEM((tk, tn), F32)],
        ("parallel", "parallel", "arbitrary"), (x, y), comm)
    return out if comm is None else (out, got)


def _mixin_fwd(h, gain, w_in):
    T, D = h.shape
    nn, _, tn = w_in.shape
    N = nn * tn
    tm = min(T, 256)

    def body(h_ref, gain_ref, w_ref, z_ref, zb_ref, un_ref):
        hh = h_ref[...]
        un = (hh * _rms(hh) * gain_ref[...]).astype(BF16)
        un_ref[...] = un
        for s in range(nn):
            z = _dot(un, w_ref[s])
            z_ref[:, s * tn:(s + 1) * tn] = z
            zb_ref[:, s * tn:(s + 1) * tn] = z.astype(BF16)

    return pl.pallas_call(
        body, name="mixin_fwd", grid=(T // tm,),
        in_specs=[pl.BlockSpec((tm, D), lambda i: (i, 0)),
                  pl.BlockSpec((1, D), lambda i: (0, 0)),
                  pl.BlockSpec((nn, D, tn), lambda i: (0, 0, 0))],
        out_specs=[pl.BlockSpec((tm, N), lambda i: (i, 0)),
                   pl.BlockSpec((tm, N), lambda i: (i, 0)),
                   pl.BlockSpec((tm, D), lambda i: (i, 0))],
        out_shape=[jax.ShapeDtypeStruct((T, N), F32), jax.ShapeDtypeStruct((T, N), BF16),
                   jax.ShapeDtypeStruct((T, D), BF16)],
        compiler_params=_cparams(("parallel",), VMEM_LIMIT),
    )(h, gain, w_in)


def _mixin_bwd(dz, dh_res, h, gain, w_in):
    T, D = h.shape
    nn, _, tn = w_in.shape
    tm = min(T, 256)

    def body(dz_ref, dres_ref, h_ref, gain_ref, w_ref, dh_ref, dgain_ref):
        @pl.when(pl.program_id(0) == 0)
        def _():
            dgain_ref[...] = jnp.zeros_like(dgain_ref)

        dun = _dot_nt(dz_ref[:, 0:tn], w_ref[0])
        for s in range(1, nn):
            dun = dun + _dot_nt(dz_ref[:, s * tn:(s + 1) * tn], w_ref[s])
        hh = h_ref[...]
        r = _rms(hh)
        dgain_ref[...] += _col_sum(dun * hh * r)
        dh_ref[...] = dres_ref[...] + _rms_bwd(hh, r, dun * gain_ref[...])

    return pl.pallas_call(
        body, name="mixin_bwd", grid=(T // tm,),
        in_specs=[pl.BlockSpec((tm, nn * tn), lambda i: (i, 0)),
                  pl.BlockSpec((tm, D), lambda i: (i, 0)),
                  pl.BlockSpec((tm, D), lambda i: (i, 0)),
                  pl.BlockSpec((1, D), lambda i: (0, 0)),
                  pl.BlockSpec((nn, D, tn), lambda i: (0, 0, 0))],
        out_specs=[pl.BlockSpec((tm, D), lambda i: (i, 0)),
                   pl.BlockSpec((1, D), lambda i: (0, 0))],
        out_shape=[jax.ShapeDtypeStruct((T, D), F32), jax.ShapeDtypeStruct((1, D), F32)],
        compiler_params=_cparams(("arbitrary",), VMEM_LIMIT),
    )(dz, dh_res, h, gain, w_in)


def _loss_head(h, gain, target):
    T, D = h.shape
    tm = min(T, 1024)

    def body(h_ref, gain_ref, t_ref, loss_ref, dh_ref, dgain_ref):
        @pl.when(pl.program_id(0) == 0)
        def _():
            loss_ref[...] = jnp.zeros_like(loss_ref)
            dgain_ref[...] = jnp.zeros_like(dgain_ref)

        hh = h_ref[...]
        r = _rms(hh)
        e = hh * r * gain_ref[...] - t_ref[...]
        loss_ref[...] += (0.5 / D) * jnp.sum(e * e)
        dy = e * (1.0 / D)
        dgain_ref[...] += _col_sum(dy * hh * r)
        dh_ref[...] = _rms_bwd(hh, r, dy * gain_ref[...])

    return pl.pallas_call(
        body, name="loss_head", grid=(T // tm,),
        in_specs=[pl.BlockSpec((tm, D), lambda i: (i, 0)),
                  pl.BlockSpec((1, D), lambda i: (0, 0)),
                  pl.BlockSpec((tm, D), lambda i: (i, 0))],
        out_specs=[pl.BlockSpec((1, 128), lambda i: (0, 0)),
                   pl.BlockSpec((tm, D), lambda i: (i, 0)),
                   pl.BlockSpec((1, D), lambda i: (0, 0))],
        out_shape=[jax.ShapeDtypeStruct((1, 128), F32), jax.ShapeDtypeStruct((T, D), F32),
                   jax.ShapeDtypeStruct((1, D), F32)],
        compiler_params=_cparams(("arbitrary",), VMEM_LIMIT),
    )(h, gain, target)


def _adamw(parts, w, m, v, name):
    R, C = w.shape
    mult = 16 if parts.dtype == BF16 else 8
    tr = max(t for t in range(mult, min(R, 512) + 1, mult) if R % t == 0)
    c1 = 1.0 - ADAM_B1 ** ADAM_STEP
    c2 = 1.0 - ADAM_B2 ** ADAM_STEP

    def body(p_ref, w_ref, m_ref, v_ref, g_ref, d_ref, nm_ref, nv_ref):
        g = p_ref[0].astype(F32)
        for k in range(1, N_DEV):
            g = g + p_ref[k].astype(F32)
        mm = ADAM_B1 * m_ref[...] + (1.0 - ADAM_B1) * g
        vv = ADAM_B2 * v_ref[...] + (1.0 - ADAM_B2) * (g * g)
        g_ref[...] = g
        nm_ref[...] = mm
        nv_ref[...] = vv
        d_ref[...] = -ADAM_LR * ((mm / c1) / (jnp.sqrt(vv / c2) + ADAM_EPS) + ADAM_WD * w_ref[...])

    spec = pl.BlockSpec((tr, C), lambda i: (i, 0))
    return pl.pallas_call(
        body, name=name, grid=(R // tr,),
        in_specs=[pl.BlockSpec((N_DEV, tr, C), lambda i: (0, i, 0)), spec, spec, spec],
        out_specs=[spec, spec, spec, spec],
        out_shape=[jax.ShapeDtypeStruct((R, C), F32)] * 4,
        compiler_params=_cparams(("parallel",), VMEM_LIMIT),
    )(parts, w, m, v)


S5_NS = 256
S5_NH = 2
S5_NCB = 4
S5_RC = 1024
S5_NQ = 4
S5_GROUP = 2


def _disc_math(a_re, a_im, log_dt, bt_re, bt_im):
    dt = jnp.exp(log_dt)
    zr, zi = a_re * dt, a_im * dt
    mag = jnp.exp(zr)
    lb_re, lb_im = mag * jnp.cos(zi), mag * jnp.sin(zi)
    den = a_re * a_re + a_im * a_im
    nr, ni = lb_re - 1.0, lb_im
    f_re = (nr * a_re + ni * a_im) / den
    f_im = (ni * a_re - nr * a_im) / den
    bb_re = f_re[:, None, :] * bt_re - f_im[:, None, :] * bt_im
    bb_im = f_re[:, None, :] * bt_im + f_im[:, None, :] * bt_re
    return lb_re, lb_im, bb_re, bb_im


def _disc_fwd(a_re, a_im, log_dt, bt_re, bt_im, chain_len, name):
    G, P = a_re.shape
    C = bt_re.shape[1]
    n_sq = int(round(math.log2(chain_len)))
    assert 2 ** n_sq == chain_len

    def body(a_re_ref, a_im_ref, ldt_ref, br_ref, bi_ref, lr_ref, li_ref, sr_ref, si_ref, bbr_ref, bbi_ref):
        lr, li, bbr, bbi = _disc_math(a_re_ref[...], a_im_ref[...], ldt_ref[...], br_ref[...], bi_ref[...])
        lr_ref[...] = lr
        li_ref[...] = li
        bbr_ref[...] = bbr
        bbi_ref[...] = bbi
        pr, pi = lr, li
        for _ in range(n_sq):
            pr, pi = pr * pr - pi * pi, 2.0 * pr * pi
        sr_ref[...] = pr
        si_ref[...] = pi

    s2 = jax.ShapeDtypeStruct((G, P), F32)
    s3 = jax.ShapeDtypeStruct((G, C, P), F32)
    return pl.pallas_call(body, name=name, out_shape=[s2, s2, s2, s2, s3, s3])(a_re, a_im, log_dt, bt_re, bt_im)


def _disc_bwd(a_re, a_im, log_dt, bt_re, bt_im, d_lr, d_li, d_bbr, d_bbi, name):
    G, P = a_re.shape
    C = bt_re.shape[1]

    def body(a_re_ref, a_im_ref, ldt_ref, br_ref, bi_ref, c1, c2, c3, c4, o1, o2, o3, o4, o5):
        _, vjp = jax.vjp(_disc_math, a_re_ref[...], a_im_ref[...], ldt_ref[...], br_ref[...], bi_ref[...])
        o1[...], o2[...], o3[...], o4[...], o5[...] = vjp((c1[...], c2[...], c3[...], c4[...]))

    s2 = jax.ShapeDtypeStruct((G, P), F32)
    s3 = jax.ShapeDtypeStruct((G, C, P), F32)
    return pl.pallas_call(body, name=name, out_shape=[s2, s2, jax.ShapeDtypeStruct((G, 1), F32), s3, s3])(
        a_re, a_im, log_dt, bt_re, bt_im, d_lr, d_li, d_bbr, d_bbi)


def _row_block(ib):
    return pl.ds(pl.multiple_of(ib * SCAN_LANES, SCAN_LANES), SCAN_LANES)


def _chain_block(j, i, ascending, n_blocks):
    at = j * (n_blocks // S5_NQ) + i
    return _row_block(jnp.where(ascending, at, n_blocks - 1 - at))


def _unrolled_loop(n, unroll, body, carry):
    trips = n // unroll
    carry = lax.fori_loop(
        0, trips, lambda t, c: functools.reduce(lambda cc, u: body(t * unroll + u, cc), range(unroll), c), carry)
    for i in range(trips * unroll, n):
        carry = body(i, carry)
    return carry


def _cmul_add(lr, li, sr, si, xr, xi):
    return lr * sr - li * si + xr, lr * si + li * sr + xi


def _scan(xr_ref, xi_ref, lr, li, init, ascending, n_blocks, store):
    steps = n_blocks // S5_NQ
    if not store:
        def step(i, carry):
            blocks = [_chain_block(j, i, ascending, n_blocks) for j in range(S5_NQ)]
            return tuple(_cmul_add(lr, li, sr, si, xr_ref[rows, :], xi_ref[rows, :])
                         for (sr, si), rows in zip(carry, blocks))

        return _unrolled_loop(steps, 4, step, init)

    group = S5_GROUP
    assert steps % group == 0

    def trip(t, carry):
        blocks = [[_chain_block(j, t * group + u, ascending, n_blocks) for j in range(S5_NQ)] for u in range(group)]
        xs = [[(xr_ref[rows, :], xi_ref[rows, :]) for rows in row] for row in blocks]
        states = list(carry)
        done = []
        for u in range(group):
            states = [_cmul_add(lr, li, sr, si, xr, xi) for (sr, si), (xr, xi) in zip(states, xs[u])]
            done.append(states)
        for u in range(group):
            for rows, (nr, ni) in zip(blocks[u], done[u]):
                xr_ref[rows, :] = nr
                xi_ref[rows, :] = ni
        return tuple(states)

    return lax.fori_loop(0, steps // group, trip, init)


def _segment_starts(w, lsr, lsi, ascending):
    shape = w[0][0].shape
    row = lax.broadcasted_iota(jnp.int32, shape, 0)
    keep = row != jnp.where(ascending, 0, SCAN_LANES - 1)

    def shift(t):
        t = jnp.where(ascending, pltpu.roll(t, 1, 0), pltpu.roll(t, SCAN_LANES - 1, 0))
        return jnp.where(keep, t, 0.0)

    zero = jnp.zeros(shape, F32)
    c = [(zero, zero)] * S5_NQ
    for _ in range(SCAN_LANES):
        tr, ti = _cmul_add(lsr, lsi, *c[-1], *w[-1])
        c[0] = (shift(tr), shift(ti))
        for j in range(1, S5_NQ):
            c[j] = _cmul_add(lsr, lsi, *c[j - 1], *w[j - 1])
    return tuple(c)


def _first_pass(xr_ref, xi_ref, lam_ref, ascending, n_blocks, conj):
    shape = (SCAN_LANES, xr_ref.shape[1])
    sign = -1.0 if conj else 1.0
    lr = jnp.broadcast_to(lam_ref[0:1, :], shape)
    li = sign * jnp.broadcast_to(lam_ref[1:2, :], shape)
    lsr = jnp.broadcast_to(lam_ref[2:3, :], shape)
    lsi = sign * jnp.broadcast_to(lam_ref[3:4, :], shape)
    zero = jnp.zeros(shape, F32)
    w = _scan(xr_ref, xi_ref, lr, li, ((zero, zero),) * S5_NQ, ascending, n_blocks, store=False)
    return _segment_starts(w, lsr, lsi, ascending), lr, li


def _s5_specs(T):
    NS = S5_NS
    tok = pl.BlockSpec((T, 128), lambda c, d, h: (0, c))
    b_spec = pl.BlockSpec((None, None, None, 128, NS), lambda c, d, h: (d, c, h, 0, 0))
    c_spec = pl.BlockSpec((None, None, None, NS, 128), lambda c, d, h: (d, c, h, 0, 0))
    lam_spec = pl.BlockSpec((None, None, None, 4, NS), lambda c, d, h: (d, c, h, 0, 0))
    return tok, b_spec, c_spec, lam_spec


def _s5_fwd(zp, bre, bim, lam, cre, cimn, comm=None):
    T = zp.shape[0]
    NS = S5_NS
    nb = T // SCAN_LANES
    rc = min(S5_RC, T)
    tok, b_spec, c_spec, lam_spec = _s5_specs(T)

    def body(zp_ref, bre_ref, bim_ref, lam_ref, cre_ref, cim_ref, y_ref, xr_ref, xi_ref):
        d = pl.program_id(1)
        ascending = d == 0

        @pl.when((d == 0) & (pl.program_id(2) == 0))
        def _():
            y_ref[...] = jnp.zeros_like(y_ref)

        def proj(c, _):
            rows = pl.ds(pl.multiple_of(c * rc, rc), rc)
            zz = zp_ref[rows, :]
            xr_ref[rows, :] = _dot(zz, bre_ref[...])
            xi_ref[rows, :] = _dot(zz, bim_ref[...])
            return 0

        lax.fori_loop(0, T // rc, proj, 0)
        starts, lr, li = _first_pass(xr_ref, xi_ref, lam_ref, ascending, nb, conj=False)
        _scan(xr_ref, xi_ref, lr, li, starts, ascending, nb, store=True)

        def outp(c, _):
            rows = pl.ds(pl.multiple_of(c * rc, rc), rc)
            y_ref[rows, :] += (_dot(xr_ref[rows, :].astype(BF16), cre_ref[...])
                               + _dot(xi_ref[rows, :].astype(BF16), cim_ref[...]))
            return 0

        lax.fori_loop(0, T // rc, outp, 0)

    return _pallas(
        body, "s5_fwd", (S5_NCB, 2, S5_NH),
        [tok, b_spec, b_spec, lam_spec, c_spec, c_spec], [tok],
        [jax.ShapeDtypeStruct((T, SSM_WIDTH), F32)],
        [pltpu.VMEM((T, NS), F32), pltpu.VMEM((T, NS), F32)],
        ("parallel", "arbitrary", "arbitrary"), (zp, bre, bim, lam, cre, cimn), comm)


def _s5_bwd(zp, dyp, bre, bim, lam, cre, cimn, comm=None):
    T = zp.shape[0]
    NS, NH = S5_NS, S5_NH
    nb = T // SCAN_LANES
    rc = min(S5_RC, T)
    tok, b_spec, c_spec, lam_spec = _s5_specs(T)
    dlam_spec = pl.BlockSpec((None, None, None, 2, NS), lambda c, d, h: (d, c, h, 0, 0))

    def body(zp_ref, dyp_ref, bre_ref, bim_ref, lam_ref, cre_ref, cim_ref,
             dzp_ref, dbre_ref, dbim_ref, dlam_ref, dcre_ref, dcim_ref,
             sr_ref, si_ref, gr_ref, gi_ref):
        d = pl.program_id(1)
        ascending = d == 0
        g_ascending = d != 0

        @pl.when((d == 0) & (pl.program_id(2) == 0))
        def _():
            dzp_ref[...] = jnp.zeros_like(dzp_ref)

        dcre_ref[...] = jnp.zeros_like(dcre_ref)
        dcim_ref[...] = jnp.zeros_like(dcim_ref)
        dbre_ref[...] = jnp.zeros_like(dbre_ref)
        dbim_ref[...] = jnp.zeros_like(dbim_ref)

        def proj(c, _):
            rows = pl.ds(pl.multiple_of(c * rc, rc), rc)
            zz = zp_ref[rows, :]
            sr_ref[rows, :] = _dot(zz, bre_ref[...])
            si_ref[rows, :] = _dot(zz, bim_ref[...])
            dy = dyp_ref[rows, :]
            gr_ref[rows, :] = _dot_nt(dy, cre_ref[...])
            gi_ref[rows, :] = _dot_nt(dy, cim_ref[...])
            return 0

        lax.fori_loop(0, T // rc, proj, 0)
        s_starts, lr, li = _first_pass(sr_ref, si_ref, lam_ref, ascending, nb, conj=False)
        _scan(sr_ref, si_ref, lr, li, s_starts, ascending, nb, store=True)
        g_starts, lr, lic = _first_pass(gr_ref, gi_ref, lam_ref, g_ascending, nb, conj=True)

        steps = nb // S5_NQ
        group = S5_GROUP
        assert steps % group == 0

        def gtrip(t, carry, last):
            g, (ar, ai) = carry
            first = t * group
            blocks = [[_chain_block(j, first + u, g_ascending, nb) for j in range(S5_NQ)] for u in range(group)]
            direct = [[(gr_ref[rows, :], gi_ref[rows, :]) for rows in row] for row in blocks]
            done = []
            for u in range(group):
                new = []
                for j, ((g_r, g_i), (d_r, d_i)) in enumerate(zip(g, direct[u])):
                    n_r, n_i = _cmul_add(lr, lic, g_r, g_i, d_r, d_i)
                    if last and u == group - 1:
                        s_r, s_i = s_starts[S5_NQ - 1 - j]
                    else:
                        prev = _chain_block(j, first + u + 1, g_ascending, nb)
                        s_r, s_i = sr_ref[prev, :], si_ref[prev, :]
                    ar = ar + n_r * s_r + n_i * s_i
                    ai = ai + n_i * s_r - n_r * s_i
                    new.append((n_r, n_i))
                g = new
                done.append(new)
            for u in range(group):
                for rows, (n_r, n_i) in zip(blocks[u], done[u]):
                    gr_ref[rows, :] = n_r
                    gi_ref[rows, :] = n_i
            return tuple(g), (ar, ai)

        zero = jnp.zeros((SCAN_LANES, NS), F32)
        carry = lax.fori_loop(0, steps // group - 1, lambda t, c: gtrip(t, c, False), (g_starts, (zero, zero)))
        _, (ar, ai) = gtrip(steps // group - 1, carry, True)
        dlam_ref[0:1, :] = _col_sum(ar)
        dlam_ref[1:2, :] = _col_sum(ai)

        def grads(c, _):
            rows = pl.ds(pl.multiple_of(c * rc, rc), rc)
            zz = zp_ref[rows, :]
            dy = dyp_ref[rows, :]
            g_rb = gr_ref[rows, :].astype(BF16)
            g_ib = gi_ref[rows, :].astype(BF16)
            dcre_ref[...] += _dot_tn(sr_ref[rows, :].astype(BF16), dy)
            dcim_ref[...] += _dot_tn(si_ref[rows, :].astype(BF16), dy)
            dbre_ref[...] += _dot_tn(zz, g_rb)
            dbim_ref[...] += _dot_tn(zz, g_ib)
            dzp_ref[rows, :] += _dot_nt(g_rb, bre_ref[...]) + _dot_nt(g_ib, bim_ref[...])
            return 0

        lax.fori_loop(0, T // rc, grads, 0)

    f32 = lambda *s: jax.ShapeDtypeStruct(s, F32)
    return _pallas(
        body, "s5_bwd", (S5_NCB, 2, S5_NH),
        [tok, tok, b_spec, b_spec, lam_spec, c_spec, c_spec],
        [tok, b_spec, b_spec, dlam_spec, c_spec, c_spec],
        [f32(T, SSM_WIDTH), f32(2, S5_NCB, NH, 128, NS), f32(2, S5_NCB, NH, 128, NS),
         f32(2, S5_NCB, NH, 2, NS), f32(2, S5_NCB, NH, NS, 128), f32(2, S5_NCB, NH, NS, 128)],
        [pltpu.VMEM((T, NS), F32)] * 4,
        ("parallel", "arbitrary", "arbitrary"), (zp, dyp, bre, bim, lam, cre, cimn), comm)


def _s5_delta():
    d = np.zeros((S5_NH, 8, 8 // S5_NH), np.float32)
    for h in range(S5_NH):
        for go in range(8 // S5_NH):
            d[h, h * (8 // S5_NH) + go, go] = 1.0
    return d


def _s5_pack_b(bbt):
    gh = 8 // S5_NH
    b5 = bbt.reshape(S5_NCB, S5_NH, gh, SSM_GROUP, SSM_STATE).transpose(0, 1, 3, 2, 4)
    m = b5[:, :, None] * _s5_delta()[None, :, :, None, :, None]
    return m.reshape(S5_NCB, S5_NH, 128, S5_NS)


def _s5_unpack_b(dm):
    gh = 8 // S5_NH
    d6 = dm.reshape(S5_NCB, S5_NH, 8, SSM_GROUP, gh, SSM_STATE)
    b5 = jnp.sum(d6 * _s5_delta()[None, :, :, None, :, None], axis=2)
    return b5.transpose(0, 1, 3, 2, 4).reshape(SSM_GROUPS, SSM_GROUP, SSM_STATE)


def _s5_pack_c(c):
    gh = 8 // S5_NH
    c5 = c.reshape(S5_NCB, S5_NH, gh, SSM_GROUP, SSM_STATE).transpose(0, 1, 2, 4, 3)
    m = c5[:, :, :, :, None, :] * _s5_delta().transpose(0, 2, 1)[None, :, :, None, :, None]
    return m.reshape(S5_NCB, S5_NH, S5_NS, 128)


def _s5_unpack_c(dm):
    gh = 8 // S5_NH
    d6 = dm.reshape(S5_NCB, S5_NH, gh, SSM_STATE, 8, SSM_GROUP)
    c5 = jnp.sum(d6 * _s5_delta().transpose(0, 2, 1)[None, :, :, None, :, None], axis=4)
    return c5.transpose(0, 1, 2, 4, 3).reshape(SSM_GROUPS, SSM_GROUP, SSM_STATE)


def _s5_pack_lam(x):
    return x.reshape(S5_NCB, S5_NH, S5_NS)


def _permute_rows(x):
    T = x.shape[0]
    return x.reshape(SCAN_LANES, T // SCAN_LANES, -1).transpose(1, 0, 2).reshape(T, -1)


def _unpermute_rows(x):
    T = x.shape[0]
    return x.reshape(T // SCAN_LANES, SCAN_LANES, -1).transpose(1, 0, 2).reshape(T, -1)


ATT_TB = ATT_ROWS * GRID_W
ATT_KB = 3 * ATT_TB


def _att_valid(i, n_rows):
    qi, kj = np.meshgrid(np.arange(ATT_TB), np.arange(ATT_KB), indexing="ij")
    r = i * ATT_ROWS + qi // GRID_W
    c = qi % GRID_W
    rk = (i - 1) * ATT_ROWS + kj // GRID_W
    x = kj % GRID_W
    rs = np.clip(r - WIN_H // 2, 0, n_rows - WIN_H)
    cs = np.clip(c - WIN_W // 2, 0, GRID_W - WIN_W)
    return (rk >= rs) & (rk < rs + WIN_H) & (x >= cs) & (x < cs + WIN_W)


def _att_masked_tables(table, n_rows):
    n = n_rows // ATT_ROWS
    assert n >= 3
    masks = np.stack([_att_valid(i, n_rows) for i in (0, 1, n - 1)])
    return jnp.where(masks[:, None], table[None], NEG_INF)


def _att_variant(i, n):
    return jnp.where(i == 0, 0, jnp.where(i >= n - 1, 2, 1))


def _att_exp(qh, kh, bias):
    s = _dot_nt(qh, kh) + bias
    return jnp.exp(s - jnp.max(s, axis=1, keepdims=True))


def _att_values_and_ones(vh):
    return jnp.concatenate([vh, jnp.ones_like(vh)], axis=1)


def _att_specs(n, col):
    last = n - 1
    cur = lambda i: (jnp.minimum(i, last), col)
    prv = lambda i: (jnp.maximum(jnp.minimum(i, last) - 1, 0), col)
    nxt = lambda i: (jnp.minimum(i + 1, last), col)
    blk = lambda f: pl.BlockSpec((ATT_TB, ATT_WIDTH), f)
    return blk(cur), blk(prv), blk(nxt)


def _att_fwd(zb, biasv):
    T = zb.shape[0]
    W = ATT_WIDTH
    n = T // ATT_TB
    n_rows = T // GRID_W
    cur = _att_specs(n, 0)[0]
    q_cur = _att_specs(n, 1)[0]
    k_cur, k_prv, k_nxt = _att_specs(n, 2)
    v_cur, v_prv, v_nxt = _att_specs(n, 3)

    def body(q_ref, kp_ref, kc_ref, kn_ref, vp_ref, vc_ref, vn_ref, b_ref, y_ref):
        qs = q_ref[...] * 0.125
        kb = jnp.concatenate([kp_ref[...], kc_ref[...], kn_ref[...]], axis=0)
        vb = jnp.concatenate([vp_ref[...], vc_ref[...], vn_ref[...]], axis=0)
        outs = []
        for h in range(ATT_HEADS):
            hs = slice(h * ATT_HEAD_DIM, (h + 1) * ATT_HEAD_DIM)
            e = _att_exp(qs[:, hs], kb[:, hs], b_ref[h]).astype(BF16)
            ov = _dot(e, _att_values_and_ones(vb[:, hs]))
            outs.append(ov[:, :ATT_HEAD_DIM] * (1.0 / ov[:, ATT_HEAD_DIM:ATT_HEAD_DIM + 1]))
        y_ref[...] = jnp.concatenate(outs, axis=1).astype(BF16)

    return pl.pallas_call(
        body, name="att_fwd", grid=(n,),
        in_specs=[q_cur, k_prv, k_cur, k_nxt, v_prv, v_cur, v_nxt,
                  pl.BlockSpec((None, ATT_HEADS, ATT_TB, ATT_KB), lambda i: (_att_variant(i, n), 0, 0, 0))],
        out_specs=cur,
        out_shape=jax.ShapeDtypeStruct((T, W), BF16),
        compiler_params=_cparams(("parallel",), VMEM_LIMIT),
    )(zb, zb, zb, zb, zb, zb, zb, biasv)


def _att_bwd(zb, y, do, biasv, comm=None):
    T = zb.shape[0]
    W = ATT_WIDTH
    n = T // ATT_TB
    n_rows = T // GRID_W
    cur = _att_specs(n, 0)[0]
    q_cur = _att_specs(n, 1)[0]
    k_cur, k_prv, k_nxt = _att_specs(n, 2)
    v_cur, v_prv, v_nxt = _att_specs(n, 3)
    done = pl.BlockSpec((ATT_TB, W), lambda i: (jnp.maximum(i - 1, 0), 0))
    bias_spec = pl.BlockSpec((None, ATT_HEADS, ATT_TB, ATT_KB), lambda i: (_att_variant(i, n), 0, 0, 0))

    def body(q_ref, y_ref, do_ref, kp_ref, kc_ref, kn_ref, vp_ref, vc_ref, vn_ref, b_ref,
             dq_ref, dk_ref, dv_ref, db_ref, acck_ref, accv_ref):
        i = pl.program_id(0)

        @pl.when(i == 0)
        def _():
            db_ref[...] = jnp.zeros_like(db_ref)
            acck_ref[...] = jnp.zeros_like(acck_ref)
            accv_ref[...] = jnp.zeros_like(accv_ref)

        @pl.when((i > 0) & (i < n))
        def _():
            slot = lax.rem(i + 1, 3)
            acck_ref[slot] = jnp.zeros((ATT_TB, W), F32)
            accv_ref[slot] = jnp.zeros((ATT_TB, W), F32)

        @pl.when(i < n)
        def _():
            qs = q_ref[...] * 0.125
            dob = do_ref[...]
            dy = dob.astype(F32) * y_ref[...].astype(F32)
            kb = jnp.concatenate([kp_ref[...], kc_ref[...], kn_ref[...]], axis=0)
            vb = jnp.concatenate([vp_ref[...], vc_ref[...], vn_ref[...]], axis=0)
            dqs, dks, dvs = [], [], []
            for h in range(ATT_HEADS):
                hs = slice(h * ATT_HEAD_DIM, (h + 1) * ATT_HEAD_DIM)
                qh, kh, vh, doh = qs[:, hs], kb[:, hs], vb[:, hs], dob[:, hs]
                e = _att_exp(qh, kh, b_ref[h])
                p = e * (1.0 / jnp.sum(e, axis=1, keepdims=True))
                dp = _dot_nt(doh, vh)
                ds = p * (dp - jnp.sum(dy[:, hs], axis=1, keepdims=True))
                db_ref[h] += ds
                dsb = ds.astype(BF16)
                dqs.append(_dot(dsb, kh) * 0.125)
                dks.append(_dot_tn(dsb, qh))
                dvs.append(_dot_tn(p.astype(BF16), doh))
            dq_ref[...] = jnp.concatenate(dqs, axis=1).astype(BF16)
            dk_all = jnp.concatenate(dks, axis=1)
            dv_all = jnp.concatenate(dvs, axis=1)
            for b in range(3):
                slot = lax.rem(i + 2 + b, 3)
                rows = slice(b * ATT_TB, (b + 1) * ATT_TB)
                acck_ref[slot] += dk_all[rows]
                accv_ref[slot] += dv_all[rows]

        slot = lax.rem(i + 2, 3)
        dk_ref[...] = acck_ref[slot].astype(BF16)
        dv_ref[...] = accv_ref[slot].astype(BF16)

    return _pallas(
        body, "att_bwd", (n + 1,),
        [q_cur, cur, cur, k_prv, k_cur, k_nxt, v_prv, v_cur, v_nxt, bias_spec],
        [cur, done, done, pl.BlockSpec((ATT_HEADS, ATT_TB, ATT_KB), lambda i: (0, 0, 0))],
        [jax.ShapeDtypeStruct((T, W), BF16)] * 3 + [jax.ShapeDtypeStruct((ATT_HEADS, ATT_TB, ATT_KB), F32)],
        [pltpu.VMEM((3, ATT_TB, W), F32), pltpu.VMEM((3, ATT_TB, W), F32)],
        ("arbitrary",), (zb, y, do, zb, zb, zb, zb, zb, zb, biasv), comm)


def _att_selectors():
    rsel = np.zeros((ATT_ROWS, 3 * ATT_ROWS, 2 * WIN_H - 1), np.float32)
    for a in range(ATT_ROWS):
        for b in range(3 * ATT_ROWS):
            rsel[a, b, b - a - ATT_ROWS + WIN_H - 1] = 1.0
    csel = np.zeros((GRID_W, GRID_W, 2 * WIN_W - 1), np.float32)
    for c in range(GRID_W):
        for x in range(GRID_W):
            csel[c, x, min(max(x - c, -(WIN_W - 1)), WIN_W - 1) + WIN_W - 1] = 1.0
    return rsel, csel


def _att_bias_table(rpb):
    rsel, csel = _att_selectors()
    hi = lax.Precision.HIGHEST
    t = jnp.einsum('hrd,abr->habd', rpb, rsel, precision=hi)
    t = jnp.einsum('habd,cxd->hacbx', t, csel, precision=hi)
    return t.reshape(ATT_HEADS, ATT_TB, ATT_KB)


def _att_bias_table_t(dtable):
    rsel, csel = _att_selectors()
    hi = lax.Precision.HIGHEST
    t = dtable.reshape(ATT_HEADS, ATT_ROWS, GRID_W, 3 * ATT_ROWS, GRID_W)
    t = jnp.einsum('hacbx,cxd->habd', t, csel, precision=hi)
    return jnp.einsum('habd,abr->hrd', t, rsel, precision=hi)


GELU_K = math.sqrt(2.0 / math.pi)
GELU_C = 0.044715
MERGE_TM = 256


def _gelu(x):
    return 0.5 * x * (1.0 + jnp.tanh(GELU_K * (x + GELU_C * x * x * x)))


def _gelu_grad(x):
    t = jnp.tanh(GELU_K * (x + GELU_C * x * x * x))
    return 0.5 * (1.0 + t) + 0.5 * x * (1.0 - t * t) * GELU_K * (1.0 + 3.0 * GELU_C * x * x)


def _merge_forward(ypre, zs, gs, ga, ya, ssm_d, w_glu, b_glu, w_bs, w_ba):
    ys = ypre + ssm_d * zs
    yg = _gelu(ys)
    sg = jax.nn.sigmoid(_dot(yg.astype(BF16), w_glu) + b_glu)
    y2 = yg * sg
    bs = _dot(y2.astype(BF16), w_bs)
    ba = _dot(ya, w_ba)
    s1 = jax.nn.sigmoid(gs)
    s2 = jax.nn.sigmoid(ga)
    merged = s1 * bs + s2 * ba
    return ys, yg, sg, y2, bs, ba, s1, s2, merged


def _merge_in_specs(D, W, tm):
    tok = lambda w, c: pl.BlockSpec((tm, w), lambda i: (i, c))
    full = lambda r, c: pl.BlockSpec((r, c), lambda i: (0, 0))
    z_specs = [tok(W, 0), tok(D, 4 * W // D), tok(D, 4 * W // D + 1)]
    w_specs = [full(1, W), full(W, W), full(1, W), full(W, D), full(W, D), full(D, D)]
    return tok, z_specs, w_specs


def _merge_fwd(ypre, z, ya, h1, ssm_d, w_glu, b_glu, w_bs, w_ba, w_out):
    T, D = h1.shape
    W = ypre.shape[1]
    tm = min(T, MERGE_TM)
    tok, z_specs, w_specs = _merge_in_specs(D, W, tm)

    def body(ypre_ref, zs_ref, gs_ref, ga_ref, ya_ref, h1_ref, d_ref, wglu_ref, bglu_ref, wbs_ref, wba_ref, wout_ref,
             h2_ref):
        merged = _merge_forward(ypre_ref[...], zs_ref[...], gs_ref[...], ga_ref[...], ya_ref[...], d_ref[...],
                                wglu_ref[...], bglu_ref[...], wbs_ref[...], wba_ref[...])[-1]
        h2_ref[...] = h1_ref[...] + _dot(merged.astype(BF16), wout_ref[...])

    return pl.pallas_call(
        body, name="merge_fwd", grid=(T // tm,),
        in_specs=[tok(W, 0)] + z_specs + [tok(W, 0), tok(D, 0)] + w_specs,
        out_specs=tok(D, 0),
        out_shape=jax.ShapeDtypeStruct((T, D), F32),
        compiler_params=_cparams(("parallel",), VMEM_LIMIT),
    )(ypre, z, z, z, ya, h1, ssm_d, w_glu, b_glu, w_bs, w_ba, w_out)


def _merge_bwd(dh2, ypre, z, ya, ssm_d, w_glu, b_glu, w_bs, w_ba, w_out):
    T, D = dh2.shape
    W = ypre.shape[1]
    tm = min(T, MERGE_TM)
    tok, z_specs, w_specs = _merge_in_specs(D, W, tm)

    def body(dh2_ref, ypre_ref, zs_ref, gs_ref, ga_ref, ya_ref, d_ref, wglu_ref, bglu_ref, wbs_ref, wba_ref, wout_ref,
             dypre_ref, dzs_ref, dgs_ref, dga_ref, dya_ref, dd_ref, dwglu_ref, dbglu_ref, dwbs_ref, dwba_ref, dwout_ref):
        @pl.when(pl.program_id(0) == 0)
        def _():
            for r in (dd_ref, dwglu_ref, dbglu_ref, dwbs_ref, dwba_ref, dwout_ref):
                r[...] = jnp.zeros_like(r)

        zs = zs_ref[...]
        ya = ya_ref[...]
        ys, yg, sg, y2, bs, ba, s1, s2, merged = _merge_forward(
            ypre_ref[...], zs, gs_ref[...], ga_ref[...], ya, d_ref[...],
            wglu_ref[...], bglu_ref[...], wbs_ref[...], wba_ref[...])
        dh2b = dh2_ref[...].astype(BF16)
        dmerged = _dot_nt(dh2b, wout_ref[...])
        dwout_ref[...] += _dot_tn(merged.astype(BF16), dh2b)
        dbs = (dmerged * s1).astype(BF16)
        dba = (dmerged * s2).astype(BF16)
        dgs_ref[...] = (dmerged * bs * s1 * (1.0 - s1)).astype(BF16)
        dga_ref[...] = (dmerged * ba * s2 * (1.0 - s2)).astype(BF16)
        dwbs_ref[...] += _dot_tn(y2.astype(BF16), dbs)
        dwba_ref[...] += _dot_tn(ya, dba)
        dya_ref[...] = _dot_nt(dba, wba_ref[...]).astype(BF16)
        dy2 = _dot_nt(dbs, wbs_ref[...])
        dvv = dy2 * yg * sg * (1.0 - sg)
        dvvb = dvv.astype(BF16)
        dyg = dy2 * sg + _dot_nt(dvvb, wglu_ref[...])
        dwglu_ref[...] += _dot_tn(yg.astype(BF16), dvvb)
        dbglu_ref[...] += _col_sum(dvv)
        dys = dyg * _gelu_grad(ys)
        dd_ref[...] += _col_sum(dys * zs)
        dzs_ref[...] = dys * d_ref[...]
        dypre_ref[...] = dys.astype(BF16)

    f32 = lambda *s: jax.ShapeDtypeStruct(s, F32)
    b16 = lambda *s: jax.ShapeDtypeStruct(s, BF16)
    return pl.pallas_call(
        body, name="merge_bwd", grid=(T // tm,),
        in_specs=[tok(D, 0), tok(W, 0)] + z_specs + [tok(W, 0)] + w_specs,
        out_specs=[tok(W, 0), tok(W, 0), tok(D, 0), tok(D, 0), tok(W, 0)] + w_specs,
        out_shape=[b16(T, W), f32(T, W), b16(T, D), b16(T, D), b16(T, W),
                   f32(1, W), f32(W, W), f32(1, W), f32(W, D), f32(W, D), f32(D, D)],
        compiler_params=_cparams(("arbitrary",), VMEM_LIMIT),
    )(dh2, ypre, z, z, z, ya, ssm_d, w_glu, b_glu, w_bs, w_ba, w_out)


def _cast_shards(weights):
    def body(*refs):
        n = len(refs) // 2
        for src, dst in zip(refs[:n], refs[n:]):
            dst[...] = src[0].astype(BF16)

    return pl.pallas_call(
        body, name="cast_shards",
        out_shape=[jax.ShapeDtypeStruct(w.shape[1:], BF16) for w in weights],
        compiler_params=_cparams(None, VMEM_LIMIT))(*weights)


def _gather_two_level(shards, name):
    n = len(shards)

    def body(*refs):
        x_refs, out_refs = refs[:n], refs[n:2 * n]
        send_sems, recv_sems, local_sems = refs[2 * n:]
        x, y, c = _my_place()
        me, sibling = (x, y, c), (x, y, 1 - c)
        chips = [(1 - x, y), (x, 1 - y), (1 - x, 1 - y)]

        def copy(a, k, block, to, own=False):
            slot = out_refs[a].at[_flat(*block)]
            return pltpu.make_async_remote_copy(
                src_ref=x_refs[a] if own else slot, dst_ref=slot,
                send_sem=send_sems.at[7 * a + k], recv_sem=recv_sems.at[7 * a + k],
                device_id=to, device_id_type=MESH_ID)

        sent, local = [], []
        for a in range(n):
            local.append(pltpu.make_async_copy(x_refs[a], out_refs[a].at[_flat(*me)], local_sems.at[a]))
            local[-1].start()
            sent.append(copy(a, 0, me, sibling, own=True))
            sent += [copy(a, 1 + j, me, (*chip, c), own=True) for j, chip in enumerate(chips)]
        for cp in sent:
            cp.start()
        for a in range(n):
            for j, chip in enumerate(chips):
                copy(a, 1 + j, (*chip, c), me).wait_recv()
                sent.append(copy(a, 4 + j, (*chip, c), sibling))
                sent[-1].start()
        for a in range(n):
            copy(a, 0, sibling, me).wait_recv()
            for j, chip in enumerate(chips):
                copy(a, 4 + j, (*chip, 1 - c), me).wait_recv()
        for cp in sent:
            cp.wait_send()
        for cp in local:
            cp.wait()

    return pl.pallas_call(
        body, name=name, in_specs=[_HBM] * n, out_specs=[_HBM] * n,
        out_shape=[jax.ShapeDtypeStruct((N_DEV,) + s.shape, s.dtype) for s in shards],
        scratch_shapes=[pltpu.SemaphoreType.DMA((7 * n,)), pltpu.SemaphoreType.DMA((7 * n,)),
                        pltpu.SemaphoreType.DMA((n,))],
    )(*shards)


PACK_COLS = 1024
BIG = (("ffn1_w_gate", 1), ("ffn1_w_up", 1), ("ffn1_w_down", 0), ("w_in", 1), ("ssm_w_glu", 0),
       ("w_branch_ssm", 1), ("w_branch_att", 1), ("w_out", 0),
       ("ffn2_w_gate", 1), ("ffn2_w_up", 1), ("ffn2_w_down", 0))
BIG_AXIS = dict(BIG)
TRANSPOSED = ("ffn1_w_gate", "ffn1_w_up", "ffn2_w_gate", "ffn2_w_up")
SSM_DIR = ("ssm_a_re", "ssm_a_im", "ssm_log_dt", "ssm_b_re", "ssm_b_im", "ssm_c_re", "ssm_c_im")
SMALL_EARLY = (("mix_norm",) + tuple(n + "_fwd" for n in SSM_DIR) + tuple(n + "_bwd" for n in SSM_DIR)
               + ("ssm_d", "ssm_b_glu", "att_rpb", "ffn2_norm", "final_norm"))
SMALL_LATE = ("ffn1_norm",)
WEIGHTS = ("ffn1_norm", "ffn1_w_gate", "ffn1_w_up", "ffn1_w_down", "mix_norm", "w_in") \
    + tuple(n + "_fwd" for n in SSM_DIR) + tuple(n + "_bwd" for n in SSM_DIR) \
    + ("ssm_d", "ssm_w_glu", "ssm_b_glu", "att_rpb", "w_branch_ssm", "w_branch_att", "w_out",
       "ffn2_norm", "ffn2_w_gate", "ffn2_w_up", "ffn2_w_down", "final_norm")


def _pad_rows(a, mult):
    pad = (-a.shape[-2]) % mult
    if pad:
        a = jnp.concatenate([a, jnp.zeros(a.shape[:-2] + (pad, a.shape[-1]), a.dtype)], axis=-2)
    return a


def _pack(arrays, row_mult):
    flat = jnp.concatenate([a.reshape(-1) for a in arrays])
    pad = (-flat.shape[0]) % PACK_COLS
    if pad:
        flat = jnp.concatenate([flat, jnp.zeros((pad,), flat.dtype)])
    return _pad_rows(flat.reshape(-1, PACK_COLS), row_mult)


def _unpack(slab, shapes):
    flat = slab.reshape(-1)
    out, at = [], 0
    for s in shapes:
        n = int(np.prod(s))
        out.append(flat[at:at + n].reshape(s))
        at += n
    return out


def _split_for_devices(g, axis):
    r, c = g.shape
    if axis == 1:
        return g.reshape(r, N_DEV, c // N_DEV).transpose(1, 0, 2).astype(BF16)
    return g.reshape(N_DEV, r // N_DEV, c).astype(BF16)


def _join_shards(gathered, axis):
    _, r, c = gathered.shape
    if axis == 1:
        return gathered.transpose(1, 0, 2).reshape(r, N_DEV * c)
    return gathered.reshape(N_DEV * r, c)


def _s5_direction_inputs(p, sfx, chain_len):
    bt_re = p["ssm_b_re" + sfx][0].transpose(0, 2, 1)
    bt_im = p["ssm_b_im" + sfx][0].transpose(0, 2, 1)
    raw = (p["ssm_a_re" + sfx][0], p["ssm_a_im" + sfx][0], p["ssm_log_dt" + sfx][0][:, None], bt_re, bt_im)
    lr, li, sr, si, bbr, bbi = _disc_fwd(*raw, chain_len,"s5_disc" + sfx)
    lam = jnp.stack([_s5_pack_lam(t) for t in (lr, li, sr, si)], axis=2)
    mats = (_s5_pack_b(bbr), _s5_pack_b(bbi), lam,
            _s5_pack_c(p["ssm_c_re" + sfx][0]), _s5_pack_c(-p["ssm_c_im" + sfx][0]))
    return raw, mats


def kernel(x, ffn1_norm, ffn1_w_gate, ffn1_w_up, ffn1_w_down, mix_norm, w_in, ssm_a_re_fwd, ssm_a_im_fwd, ssm_log_dt_fwd, ssm_b_re_fwd, ssm_b_im_fwd, ssm_c_re_fwd, ssm_c_im_fwd, ssm_a_re_bwd, ssm_a_im_bwd, ssm_log_dt_bwd, ssm_b_re_bwd, ssm_b_im_bwd, ssm_c_re_bwd, ssm_c_im_bwd, ssm_d, ssm_w_glu, ssm_b_glu, att_rpb, w_branch_ssm, w_branch_att, w_out, ffn2_norm, ffn2_w_gate, ffn2_w_up, ffn2_w_down, final_norm, loss_target, m_ffn1_norm, m_ffn1_w_gate, m_ffn1_w_up, m_ffn1_w_down, m_mix_norm, m_w_in, m_ssm_a_re_fwd, m_ssm_a_im_fwd, m_ssm_log_dt_fwd, m_ssm_b_re_fwd, m_ssm_b_im_fwd, m_ssm_c_re_fwd, m_ssm_c_im_fwd, m_ssm_a_re_bwd, m_ssm_a_im_bwd, m_ssm_log_dt_bwd, m_ssm_b_re_bwd, m_ssm_b_im_bwd, m_ssm_c_re_bwd, m_ssm_c_im_bwd, m_ssm_d, m_ssm_w_glu, m_ssm_b_glu, m_att_rpb, m_w_branch_ssm, m_w_branch_att, m_w_out, m_ffn2_norm, m_ffn2_w_gate, m_ffn2_w_up, m_ffn2_w_down, m_final_norm, v_ffn1_norm, v_ffn1_w_gate, v_ffn1_w_up, v_ffn1_w_down, v_mix_norm, v_w_in, v_ssm_a_re_fwd, v_ssm_a_im_fwd, v_ssm_log_dt_fwd, v_ssm_b_re_fwd, v_ssm_b_im_fwd, v_ssm_c_re_fwd, v_ssm_c_im_fwd, v_ssm_a_re_bwd, v_ssm_a_im_bwd, v_ssm_log_dt_bwd, v_ssm_b_re_bwd, v_ssm_b_im_bwd, v_ssm_c_re_bwd, v_ssm_c_im_bwd, v_ssm_d, v_ssm_w_glu, v_ssm_b_glu, v_att_rpb, v_w_branch_ssm, v_w_branch_att, v_w_out, v_ffn2_norm, v_ffn2_w_gate, v_ffn2_w_up, v_ffn2_w_down, v_final_norm):
    p = dict(locals())
    x = p["x"][0]
    target = p["loss_target"][0]
    T, D = x.shape

    stored = lambda a, n: jnp.swapaxes(a, -1, -2) if n in TRANSPOSED else a
    cut_axis = lambda n: 0 if n in TRANSPOSED else BIG_AXIS[n]
    shard = dict(zip([n for n, _ in BIG], _cast_shards([stored(p[n], n) for n, _ in BIG])))
    ffn1_w = ("ffn1_w_gate", "ffn1_w_up", "ffn1_w_down")
    mix_w = ("w_in", "ssm_w_glu", "w_branch_ssm", "w_branch_att", "w_out")
    ffn2_w = ("ffn2_w_gate", "ffn2_w_up", "ffn2_w_down")
    gathered = dict(zip(ffn1_w, _gather_two_level([shard[n] for n in ffn1_w], "gather_ffn1")))
    full = lambda n: _join_shards(gathered[n], cut_axis(n))

    h0 = x
    wg1, wu1, wd1 = [full(n) for n in ffn1_w]
    (h1, xn1, g1, u1), got = _ffn_fwd(h0, p["ffn1_norm"], wg1, wu1, wd1, "ffn1_fwd",
                                      _Comm("gather", [shard[n] for n in mix_w]))
    gathered.update(zip(mix_w, got))
    z, zb, un = _mixin_fwd(h1, p["mix_norm"], gathered["w_in"])
    W = SSM_WIDTH
    zp = _permute_rows(zb[:, :W])
    chain_len = T // SCAN_LANES // S5_NQ
    raw_f, mats_f = _s5_direction_inputs(p, "_fwd", chain_len)
    raw_b, mats_b = _s5_direction_inputs(p, "_bwd", chain_len)
    bre, bim, lam, cre, cimn = [jnp.stack([f, b]) for f, b in zip(mats_f, mats_b)]
    bre, bim, cre, cimn = [t.astype(BF16) for t in (bre, bim, cre, cimn)]
    (yp,), got = _s5_fwd(zp, bre, bim, lam, cre, cimn, _Comm("gather", [shard[n] for n in ffn2_w]))
    gathered.update(zip(ffn2_w, got))
    ypre = _unpermute_rows(yp)
    table = _att_masked_tables(_att_bias_table(p["att_rpb"][0]), T // GRID_W)
    ya = _att_fwd(zb, table)
    tail_w = (p["ssm_d"], full("ssm_w_glu"), p["ssm_b_glu"], full("w_branch_ssm"), full("w_branch_att"), full("w_out"))
    h2 = _merge_fwd(ypre, z, ya, h1, *tail_w)
    wg2, wu2, wd2 = [full(n) for n in ffn2_w]
    (h3, xn2, g2, u2), _ = _ffn_fwd(h2, p["ffn2_norm"], wg2, wu2, wd2, "ffn2_fwd")
    loss_part, dh3, d_final = _loss_head(h3, p["final_norm"][None], target)

    grads = {"final_norm": d_final[0]}
    to_send = lambda names: _Comm("exchange", [_split_for_devices(grads[n], cut_axis(n)) for n in names])
    parts = {}
    (dh2, grads["ffn2_norm"], do2, a2, dg2, du2), _ = _ffn_bwd(
        dh3, h2, p["ffn2_norm"], g2, u2, wg2, wu2, wd2, "ffn2_bwd")
    grads["ffn2_w_gate"] = _xty(dg2, xn2, "ffn2_dw_gate")
    grads["ffn2_w_up"] = _xty(du2, xn2, "ffn2_dw_up")
    grads["ffn2_w_down"] = _xty(a2, do2, "ffn2_dw_down")
    (dypre, dzs_skip, dgs, dga, dya, grads["ssm_d"], grads["ssm_w_glu"], grads["ssm_b_glu"],
     grads["w_branch_ssm"], grads["w_branch_att"], grads["w_out"]) = _merge_bwd(dh2, ypre, z, ya, *tail_w)
    (dq, dk, dv, dtable), got = _att_bwd(zb, ya, dya, table, to_send(ffn2_w))
    parts.update(zip(ffn2_w, got))
    grads["att_rpb"] = _att_bias_table_t(dtable)
    dyp = _permute_rows(dypre)
    tail_names = ("ssm_w_glu", "w_branch_ssm", "w_branch_att", "w_out")
    (dzp, dbre, dbim, dlam, dcre, dcimn), got = _s5_bwd(zp, dyp, bre, bim, lam, cre, cimn, to_send(tail_names))
    parts.update(zip(tail_names, got))
    G, P = SSM_GROUPS, SSM_STATE
    for d, (sfx, raw) in enumerate((("_fwd", raw_f), ("_bwd", raw_b))):
        da_re, da_im, dldt, dbt_re, dbt_im = _disc_bwd(
            *raw, dlam[d, :, :, 0, :].reshape(G, P), dlam[d, :, :, 1, :].reshape(G, P),
            _s5_unpack_b(dbre[d]), _s5_unpack_b(dbim[d]), "s5_disc_grad" + sfx)
        grads["ssm_a_re" + sfx] = da_re
        grads["ssm_a_im" + sfx] = da_im
        grads["ssm_log_dt" + sfx] = dldt[:, 0]
        grads["ssm_b_re" + sfx] = dbt_re.transpose(0, 2, 1)
        grads["ssm_b_im" + sfx] = dbt_im.transpose(0, 2, 1)
        grads["ssm_c_re" + sfx] = _s5_unpack_c(dcre[d])
        grads["ssm_c_im" + sfx] = -_s5_unpack_c(dcimn[d])
    dzs = _unpermute_rows(dzp) + dzs_skip
    dz = jnp.concatenate([dzs.astype(BF16), dq, dk, dv, dgs, dga], axis=1)
    dh1, grads["mix_norm"] = _mixin_bwd(dz, dh2, h1, p["mix_norm"], gathered["w_in"])
    grads["w_in"] = _xty(un, dz, "dw_in", col_shards=N_DEV)
    small_t = lambda a, n: jnp.swapaxes(a, -1, -2) if n.startswith("ssm_b_") else a
    pack_small = lambda names, src, pre: _pack([small_t(src[pre + n], n).astype(F32) for n in names], 8)
    early = _Comm(["exchange", "gather"],
                  [grads["w_in"], pack_small(SMALL_EARLY, grads, "")])
    (dh0, grads["ffn1_norm"], do1, a1, dg1, du1), (parts["w_in"], got_early) = _ffn_bwd(
        dh1, h0, p["ffn1_norm"], g1, u1, wg1, wu1, wd1, "ffn1_bwd", early)
    grads["ffn1_w_down"] = _xty(a1, do1, "ffn1_dw_down")
    grads["ffn1_w_gate"], (parts["ffn1_w_down"],) = _xty(dg1, xn1, "ffn1_dw_gate", to_send(("ffn1_w_down",)))
    grads["ffn1_w_up"], (parts["ffn1_w_gate"],) = _xty(du1, xn1, "ffn1_dw_up", to_send(("ffn1_w_gate",)))
    last = _Comm(["exchange", "gather"],
                 [_split_for_devices(grads["ffn1_w_up"], 0), pack_small(SMALL_LATE, grads, "")])
    parts["ffn1_w_up"], got_late = _comm_call(last, "exchange_last")
    got_small = jnp.concatenate([got_early, got_late], axis=1)

    results = {}
    for n, _ in BIG:
        outs = _adamw(parts[n], *[stored(p[pre + n][0], n) for pre in ("", "m_", "v_")], "adamw_" + n)
        results[n] = [stored(o, n)[None] for o in outs]
    early_rows = got_early.shape[1]
    slab = lambda pre: jnp.concatenate([pack_small(SMALL_EARLY, p, pre), pack_small(SMALL_LATE, p, pre)], axis=0)
    small_out = _adamw(got_small, slab(""), slab("m_"), slab("v_"), "adamw_small")
    for names, rows in ((SMALL_EARLY, slice(0, early_rows)), (SMALL_LATE, slice(early_rows, None))):
        shapes = [small_t(p[n], n).shape for n in names]
        for n, vals in zip(names, zip(*[_unpack(out[rows], shapes) for out in small_out])):
            results[n] = [small_t(val, n) for val in vals]

    loss = lax.psum(loss_part[0, 0], ("x", "y", "c"))
    out = [loss, dh0[None]]
    for kind in range(4):
        out += [results[n][kind] for n in WEIGHTS]
    return tuple(out)
```

```python
import functools
import math

import numpy as np
import jax
import jax.numpy as jnp
from jax import lax
from jax.experimental import pallas as pl
from jax.experimental.pallas import tpu as pltpu

F32 = jnp.float32
BF16 = jnp.bfloat16
MESH_ID = pl.DeviceIdType.MESH

SSM_GROUP = 16
SSM_GROUPS = 32
SSM_STATE = 64
SSM_WIDTH = 512
ATT_HEADS = 8
ATT_HEAD_DIM = 64
ATT_WIDTH = 512
GRID_W = 64
WIN_H = 8
WIN_W = 16
EPS = 1e-6
NEG_INF = -1e30
ADAM_LR = 0.001
ADAM_B1 = 0.9
ADAM_B2 = 0.999
ADAM_EPS = 1e-08
ADAM_WD = 0.01
ADAM_STEP = 10

N_DEV = 8
V7X_VMEM_BYTES = 64 * 1024 * 1024
VMEM_LIMIT = V7X_VMEM_BYTES - 8 * 1024 * 1024
SCAN_LANES = 8
ATT_ROWS = 4


def _cparams(sem, vmem=None):
    return pltpu.CompilerParams(dimension_semantics=sem, vmem_limit_bytes=vmem)


def _dot(a, b):
    return jnp.dot(a, b, preferred_element_type=F32)


def _dot_nt(a, b):
    return lax.dot_general(a, b, (((1,), (1,)), ((), ())), preferred_element_type=F32)


def _dot_tn(a, b):
    return lax.dot_general(a, b, (((0,), (0,)), ((), ())), preferred_element_type=F32)


def _rms(h):
    return lax.rsqrt(jnp.mean(h * h, axis=-1, keepdims=True) + EPS)


def _rms_bwd(h, r, v):
    return r * v - h * (r * r * r) * jnp.mean(h * v, axis=-1, keepdims=True)


def _col_sum(x):
    return jnp.sum(x, axis=0, keepdims=True)


def _my_place():
    return lax.axis_index("x"), lax.axis_index("y"), lax.axis_index("c")


def _flat(px, py, pc):
    return 4 * px + 2 * py + pc


class _Comm:
    def __init__(self, kind, arrays):
        self.arrays = list(arrays)
        self.n = len(self.arrays)
        self.kinds = [kind] * self.n if isinstance(kind, str) else list(kind)

    def out_shapes(self):
        return [jax.ShapeDtypeStruct((N_DEV,) + a.shape if k == "gather" else a.shape, a.dtype)
                for k, a in zip(self.kinds, self.arrays)]

    def scratch(self):
        return [pltpu.SemaphoreType.DMA((7 * self.n,)), pltpu.SemaphoreType.DMA((7 * self.n,)),
                pltpu.SemaphoreType.DMA((self.n,))]

    def run(self, srcs, dsts, sems, start):
        send_sems, recv_sems, local_sems = sems
        x, y, c = _my_place()
        mine = _flat(x, y, c)
        for a, (src, dst) in enumerate(zip(srcs, dsts)):
            whole = self.kinds[a] == "gather"
            local = pltpu.make_async_copy(src if whole else src.at[mine], dst.at[mine], local_sems.at[a])
            local.start() if start else local.wait()
            for k in range(1, N_DEV):
                px = 1 - x if k & 4 else x
                py = 1 - y if k & 2 else y
                pc = 1 - c if k & 1 else c
                cp = pltpu.make_async_remote_copy(
                    src_ref=src if whole else src.at[_flat(px, py, pc)], dst_ref=dst.at[mine],
                    send_sem=send_sems.at[7 * a + k - 1], recv_sem=recv_sems.at[7 * a + k - 1],
                    device_id=(px, py, pc), device_id_type=MESH_ID)
                cp.start() if start else cp.wait()


_HBM = pl.BlockSpec(memory_space=pltpu.HBM)


def _comm_call(comm, name):
    def body(*refs):
        srcs, dsts, sems = refs[:comm.n], refs[comm.n:2 * comm.n], refs[2 * comm.n:]
        comm.run(srcs, dsts, sems, True)
        comm.run(srcs, dsts, sems, False)

    return pl.pallas_call(body, name=name, in_specs=[_HBM] * comm.n, out_specs=[_HBM] * comm.n,
                          out_shape=comm.out_shapes(), scratch_shapes=comm.scratch())(*comm.arrays)


def _pallas(core, name, grid, in_specs, out_specs, out_shape, scratch, sem, args, comm=None):
    if comm is None:
        out = pl.pallas_call(core, name=name, grid=grid, in_specs=in_specs, out_specs=out_specs,
                             out_shape=out_shape, scratch_shapes=scratch,
                             compiler_params=_cparams(sem, VMEM_LIMIT))(*args)
        return out, []
    n_in, n_out, n_scr, n = len(in_specs), len(out_specs), len(scratch), comm.n

    def body(*refs):
        ins, srcs = refs[:n_in], refs[n_in:n_in + n]
        outs, dsts = refs[n_in + n:n_in + n + n_out], refs[n_in + n + n_out:n_in + 2 * n + n_out]
        scr, sems = refs[n_in + 2 * n + n_out:n_in + 2 * n + n_out + n_scr], refs[n_in + 2 * n + n_out + n_scr:]
        ids = [pl.program_id(k) for k in range(len(grid))]
        first = functools.reduce(lambda a, b: a & b, [i == 0 for i in ids])
        last = functools.reduce(lambda a, b: a & b, [i == g - 1 for i, g in zip(ids, grid)])

        @pl.when(first)
        def _():
            comm.run(srcs, dsts, sems, True)

        core(*ins, *outs, *scr)

        @pl.when(last)
        def _():
            comm.run(srcs, dsts, sems, False)

    out = pl.pallas_call(
        body, name=name, grid=grid, in_specs=list(in_specs) + [_HBM] * n, out_specs=list(out_specs) + [_HBM] * n,
        out_shape=list(out_shape) + comm.out_shapes(), scratch_shapes=list(scratch) + comm.scratch(),
        compiler_params=_cparams(("arbitrary",) * len(grid), VMEM_LIMIT))(*args, *comm.arrays)
    return out[:n_out], out[n_out:]


FFN_TM = 256


def _ffn_fwd(h, gain, wg, wu, wd, name, comm=None):
    T, D = h.shape
    F = wg.shape[0]
    tm = min(T, FFN_TM)
    once = pl.Buffered(1)

    def body(h_ref, gain_ref, wg_ref, wu_ref, wd_ref, ho_ref, xn_ref, g_ref, u_ref):
        hh = h_ref[...]
        xn = (hh * _rms(hh) * gain_ref[...]).astype(BF16)
        xn_ref[...] = xn
        g = _dot_nt(xn, wg_ref[...])
        u = _dot_nt(xn, wu_ref[...])
        g_ref[...] = g.astype(BF16)
        u_ref[...] = u.astype(BF16)
        a = (g * jax.nn.sigmoid(g) * u).astype(BF16)
        ho_ref[...] = hh + 0.5 * _dot(a, wd_ref[...])

    return _pallas(
        body, name, (T // tm,),
        [pl.BlockSpec((tm, D), lambda i: (i, 0)),
         pl.BlockSpec((1, D), lambda i: (0, 0)),
         pl.BlockSpec((F, D), lambda i: (0, 0), pipeline_mode=once),
         pl.BlockSpec((F, D), lambda i: (0, 0), pipeline_mode=once),
         pl.BlockSpec((F, D), lambda i: (0, 0), pipeline_mode=once)],
        [pl.BlockSpec((tm, D), lambda i: (i, 0)),
         pl.BlockSpec((tm, D), lambda i: (i, 0)),
         pl.BlockSpec((tm, F), lambda i: (i, 0)),
         pl.BlockSpec((tm, F), lambda i: (i, 0))],
        [jax.ShapeDtypeStruct((T, D), F32), jax.ShapeDtypeStruct((T, D), BF16),
         jax.ShapeDtypeStruct((T, F), BF16), jax.ShapeDtypeStruct((T, F), BF16)],
        [], ("parallel",), (h, gain, wg, wu, wd), comm)


def _ffn_bwd(dho, h, gain, g, u, wg, wu, wd, name, comm=None):
    T, D = h.shape
    F = wg.shape[0]
    tm = min(T, FFN_TM)
    tf = 1408 if F % 1408 == 0 else F
    once = pl.Buffered(1)

    def body(dho_ref, h_ref, gain_ref, g_ref, u_ref, wg_ref, wu_ref, wd_ref,
             dh_ref, dgain_ref, do_ref, a_ref, dg_ref, du_ref):
        @pl.when(pl.program_id(0) == 0)
        def _():
            dgain_ref[...] = jnp.zeros_like(dgain_ref)

        dho_v = dho_ref[...]
        do = (0.5 * dho_v).astype(BF16)
        do_ref[...] = do
        dxn = None
        for c in range(F // tf):
            cs = slice(c * tf, (c + 1) * tf)
            da = _dot_nt(do, wd_ref[cs, :])
            gg = g_ref[:, cs].astype(F32)
            uu = u_ref[:, cs].astype(F32)
            s = jax.nn.sigmoid(gg)
            sl = gg * s
            a_ref[:, cs] = (sl * uu).astype(BF16)
            dg = (da * uu * (s * (1.0 + gg * (1.0 - s)))).astype(BF16)
            du = (da * sl).astype(BF16)
            dg_ref[:, cs] = dg
            du_ref[:, cs] = du
            part = _dot(dg, wg_ref[cs, :]) + _dot(du, wu_ref[cs, :])
            dxn = part if dxn is None else dxn + part
        hh = h_ref[...]
        r = _rms(hh)
        dgain_ref[...] += _col_sum(dxn * hh * r)
        dh_ref[...] = dho_v + _rms_bwd(hh, r, dxn * gain_ref[...])

    tok = lambda w: pl.BlockSpec((tm, w), lambda i: (i, 0))
    row = pl.BlockSpec((1, D), lambda i: (0, 0))
    weight = pl.BlockSpec((F, D), lambda i: (0, 0), pipeline_mode=once)
    return _pallas(
        body, name, (T // tm,),
        [tok(D), tok(D), row, tok(F), tok(F), weight, weight, weight],
        [tok(D), row, tok(D), tok(F), tok(F), tok(F)],
        [jax.ShapeDtypeStruct((T, D), F32), jax.ShapeDtypeStruct((1, D), F32),
         jax.ShapeDtypeStruct((T, D), BF16), jax.ShapeDtypeStruct((T, F), BF16),
         jax.ShapeDtypeStruct((T, F), BF16), jax.ShapeDtypeStruct((T, F), BF16)],
        [], ("arbitrary",), (dho, h, gain, g, u, wg, wu, wd), comm)


def _xty(x, y, name, comm=None, col_shards=1):
    T, K = x.shape
    N = y.shape[1]
    tt = min(T, 2048)
    tk = K if K <= 1024 else (1408 if K % 1408 == 0 else K)
    tn = N if N <= 1024 else (1408 if N % 1408 == 0 else (1024 if N % 1024 == 0 else N))
    nt = T // tt
    ws = N // col_shards
    per = tn // ws if col_shards > 1 else 1
    assert col_shards == 1 or (tn % ws == 0 and ws % 128 == 0)

    def body(x_ref, y_ref, o_ref, acc_ref):
        t = pl.program_id(2)

        @pl.when(t == 0)
        def _():
            acc_ref[...] = jnp.zeros_like(acc_ref)

        acc_ref[...] += _dot_tn(x_ref[...], y_ref[...])

        @pl.when(t == nt - 1)
        def _():
            if col_shards == 1:
                o_ref[...] = acc_ref[...].astype(BF16)
            else:
                for s in range(per):
                    o_ref[s] = acc_ref[:, s * ws:(s + 1) * ws].astype(BF16)

    if col_shards == 1:
        out_spec = pl.BlockSpec((tk, tn), lambda k, n, t: (k, n))
        out_shape = jax.ShapeDtypeStruct((K, N), BF16)
    else:
        out_spec = pl.BlockSpec((per, tk, ws), lambda k, n, t: (n, k, 0))
        out_shape = jax.ShapeDtypeStruct((col_shards, K, ws), BF16)
    (out,), got = _pallas(
        body, name, (K // tk, N // tn, nt),
        [pl.BlockSpec((tt, tk), lambda k, n, t: (t, k)), pl.BlockSpec((tt, tn), lambda k, n, t: (t, n))],
        [out_spec], [out_shape], [pltpu.VMEM((tk, tn), F32)],
        ("parallel", "parallel", "arbitrary"), (x, y), comm)
    return out if comm is None else (out, got)


def _mixin_fwd(h, gain, w_in):
    T, D = h.shape
    nn, _, tn = w_in.shape
    N = nn * tn
    tm = min(T, 256)

    def body(h_ref, gain_ref, w_ref, z_ref, zb_ref, un_ref):
        hh = h_ref[...]
        un = (hh * _rms(hh) * gain_ref[...]).astype(BF16)
        un_ref[...] = un
        for s in range(nn):
            z = _dot(un, w_ref[s])
            z_ref[:, s * tn:(s + 1) * tn] = z
            zb_ref[:, s * tn:(s + 1) * tn] = z.astype(BF16)

    return pl.pallas_call(
        body, name="mixin_fwd", grid=(T // tm,),
        in_specs=[pl.BlockSpec((tm, D), lambda i: (i, 0)),
                  pl.BlockSpec((1, D), lambda i: (0, 0)),
                  pl.BlockSpec((nn, D, tn), lambda i: (0, 0, 0))],
        out_specs=[pl.BlockSpec((tm, N), lambda i: (i, 0)),
                   pl.BlockSpec((tm, N), lambda i: (i, 0)),
                   pl.BlockSpec((tm, D), lambda i: (i, 0))],
        out_shape=[jax.ShapeDtypeStruct((T, N), F32), jax.ShapeDtypeStruct((T, N), BF16),
                   jax.ShapeDtypeStruct((T, D), BF16)],
        compiler_params=_cparams(("parallel",), VMEM_LIMIT),
    )(h, gain, w_in)


def _mixin_bwd(dz, dh_res, h, gain, w_in):
    T, D = h.shape
    nn, _, tn = w_in.shape
    tm = min(T, 256)

    def body(dz_ref, dres_ref, h_ref, gain_ref, w_ref, dh_ref, dgain_ref):
        @pl.when(pl.program_id(0) == 0)
        def _():
            dgain_ref[...] = jnp.zeros_like(dgain_ref)

        dun = _dot_nt(dz_ref[:, 0:tn], w_ref[0])
        for s in range(1, nn):
            dun = dun + _dot_nt(dz_ref[:, s * tn:(s + 1) * tn], w_ref[s])
        hh = h_ref[...]
        r = _rms(hh)
        dgain_ref[...] += _col_sum(dun * hh * r)
        dh_ref[...] = dres_ref[...] + _rms_bwd(hh, r, dun * gain_ref[...])

    return pl.pallas_call(
        body, name="mixin_bwd", grid=(T // tm,),
        in_specs=[pl.BlockSpec((tm, nn * tn), lambda i: (i, 0)),
                  pl.BlockSpec((tm, D), lambda i: (i, 0)),
                  pl.BlockSpec((tm, D), lambda i: (i, 0)),
                  pl.BlockSpec((1, D), lambda i: (0, 0)),
                  pl.BlockSpec((nn, D, tn), lambda i: (0, 0, 0))],
        out_specs=[pl.BlockSpec((tm, D), lambda i: (i, 0)),
                   pl.BlockSpec((1, D), lambda i: (0, 0))],
        out_shape=[jax.ShapeDtypeStruct((T, D), F32), jax.ShapeDtypeStruct((1, D), F32)],
        compiler_params=_cparams(("arbitrary",), VMEM_LIMIT),
    )(dz, dh_res, h, gain, w_in)


def _loss_head(h, gain, target):
    T, D = h.shape
    tm = min(T, 1024)

    def body(h_ref, gain_ref, t_ref, loss_ref, dh_ref, dgain_ref):
        @pl.when(pl.program_id(0) == 0)
        def _():
            loss_ref[...] = jnp.zeros_like(loss_ref)
            dgain_ref[...] = jnp.zeros_like(dgain_ref)

        hh = h_ref[...]
        r = _rms(hh)
        e = hh * r * gain_ref[...] - t_ref[...]
        loss_ref[...] += (0.5 / D) * jnp.sum(e * e)
        dy = e * (1.0 / D)
        dgain_ref[...] += _col_sum(dy * hh * r)
        dh_ref[...] = _rms_bwd(hh, r, dy * gain_ref[...])

    return pl.pallas_call(
        body, name="loss_head", grid=(T // tm,),
        in_specs=[pl.BlockSpec((tm, D), lambda i: (i, 0)),
                  pl.BlockSpec((1, D), lambda i: (0, 0)),
                  pl.BlockSpec((tm, D), lambda i: (i, 0))],
        out_specs=[pl.BlockSpec((1, 128), lambda i: (0, 0)),
                   pl.BlockSpec((tm, D), lambda i: (i, 0)),
                   pl.BlockSpec((1, D), lambda i: (0, 0))],
        out_shape=[jax.ShapeDtypeStruct((1, 128), F32), jax.ShapeDtypeStruct((T, D), F32),
                   jax.ShapeDtypeStruct((1, D), F32)],
        compiler_params=_cparams(("arbitrary",), VMEM_LIMIT),
    )(h, gain, target)


def _adamw(parts, w, m, v, name):
    R, C = w.shape
    mult = 16 if parts.dtype == BF16 else 8
    tr = max(t for t in range(mult, min(R, 512) + 1, mult) if R % t == 0)
    c1 = 1.0 - ADAM_B1 ** ADAM_STEP
    c2 = 1.0 - ADAM_B2 ** ADAM_STEP

    def body(p_ref, w_ref, m_ref, v_ref, g_ref, d_ref, nm_ref, nv_ref):
        g = p_ref[0].astype(F32)
        for k in range(1, N_DEV):
            g = g + p_ref[k].astype(F32)
        mm = ADAM_B1 * m_ref[...] + (1.0 - ADAM_B1) * g
        vv = ADAM_B2 * v_ref[...] + (1.0 - ADAM_B2) * (g * g)
        g_ref[...] = g
        nm_ref[...] = mm
        nv_ref[...] = vv
        d_ref[...] = -ADAM_LR * ((mm / c1) / (jnp.sqrt(vv / c2) + ADAM_EPS) + ADAM_WD * w_ref[...])

    spec = pl.BlockSpec((tr, C), lambda i: (i, 0))
    return pl.pallas_call(
        body, name=name, grid=(R // tr,),
        in_specs=[pl.BlockSpec((N_DEV, tr, C), lambda i: (0, i, 0)), spec, spec, spec],
        out_specs=[spec, spec, spec, spec],
        out_shape=[jax.ShapeDtypeStruct((R, C), F32)] * 4,
        compiler_params=_cparams(("parallel",), VMEM_LIMIT),
    )(parts, w, m, v)


S5_NS = 256
S5_NH = 2
S5_NCB = 4
S5_RC = 2048
S5_NQ = 4
S5_GROUP = 2


def _disc_math(a_re, a_im, log_dt, bt_re, bt_im):
    dt = jnp.exp(log_dt)
    zr, zi = a_re * dt, a_im * dt
    mag = jnp.exp(zr)
    lb_re, lb_im = mag * jnp.cos(zi), mag * jnp.sin(zi)
    den = a_re * a_re + a_im * a_im
    nr, ni = lb_re - 1.0, lb_im
    f_re = (nr * a_re + ni * a_im) / den
    f_im = (ni * a_re - nr * a_im) / den
    bb_re = f_re[:, None, :] * bt_re - f_im[:, None, :] * bt_im
    bb_im = f_re[:, None, :] * bt_im + f_im[:, None, :] * bt_re
    return lb_re, lb_im, bb_re, bb_im


def _disc_fwd(a_re, a_im, log_dt, bt_re, bt_im, chain_len, name):
    G, P = a_re.shape
    C = bt_re.shape[1]
    n_sq = int(round(math.log2(chain_len)))
    assert 2 ** n_sq == chain_len

    def body(a_re_ref, a_im_ref, ldt_ref, br_ref, bi_ref, lr_ref, li_ref, sr_ref, si_ref, bbr_ref, bbi_ref):
        lr, li, bbr, bbi = _disc_math(a_re_ref[...], a_im_ref[...], ldt_ref[...], br_ref[...], bi_ref[...])
        lr_ref[...] = lr
        li_ref[...] = li
        bbr_ref[...] = bbr
        bbi_ref[...] = bbi
        pr, pi = lr, li
        for _ in range(n_sq):
            pr, pi = pr * pr - pi * pi, 2.0 * pr * pi
        sr_ref[...] = pr
        si_ref[...] = pi

    s2 = jax.ShapeDtypeStruct((G, P), F32)
    s3 = jax.ShapeDtypeStruct((G, C, P), F32)
    return pl.pallas_call(body, name=name, out_shape=[s2, s2, s2, s2, s3, s3])(a_re, a_im, log_dt, bt_re, bt_im)


def _disc_bwd(a_re, a_im, log_dt, bt_re, bt_im, d_lr, d_li, d_bbr, d_bbi, name):
    G, P = a_re.shape
    C = bt_re.shape[1]

    def body(a_re_ref, a_im_ref, ldt_ref, br_ref, bi_ref, c1, c2, c3, c4, o1, o2, o3, o4, o5):
        _, vjp = jax.vjp(_disc_math, a_re_ref[...], a_im_ref[...], ldt_ref[...], br_ref[...], bi_ref[...])
        o1[...], o2[...], o3[...], o4[...], o5[...] = vjp((c1[...], c2[...], c3[...], c4[...]))

    s2 = jax.ShapeDtypeStruct((G, P), F32)
    s3 = jax.ShapeDtypeStruct((G, C, P), F32)
    return pl.pallas_call(body, name=name, out_shape=[s2, s2, jax.ShapeDtypeStruct((G, 1), F32), s3, s3])(
        a_re, a_im, log_dt, bt_re, bt_im, d_lr, d_li, d_bbr, d_bbi)


def _row_block(ib):
    return pl.ds(pl.multiple_of(ib * SCAN_LANES, SCAN_LANES), SCAN_LANES)


def _chain_block(j, i, ascending, n_blocks):
    at = j * (n_blocks // S5_NQ) + i
    return _row_block(jnp.where(ascending, at, n_blocks - 1 - at))


def _unrolled_loop(n, unroll, body, carry):
    trips = n // unroll
    carry = lax.fori_loop(
        0, trips, lambda t, c: functools.reduce(lambda cc, u: body(t * unroll + u, cc), range(unroll), c), carry)
    for i in range(trips * unroll, n):
        carry = body(i, carry)
    return carry


def _cmul_add(lr, li, sr, si, xr, xi):
    return lr * sr - li * si + xr, lr * si + li * sr + xi


def _scan(xr_ref, xi_ref, lr, li, init, ascending, n_blocks, store):
    steps = n_blocks // S5_NQ
    if not store:
        def step(i, carry):
            blocks = [_chain_block(j, i, ascending, n_blocks) for j in range(S5_NQ)]
            return tuple(_cmul_add(lr, li, sr, si, xr_ref[rows, :], xi_ref[rows, :])
                         for (sr, si), rows in zip(carry, blocks))

        return _unrolled_loop(steps, 4, step, init)

    group = S5_GROUP
    assert steps % group == 0

    def trip(t, carry):
        blocks = [[_chain_block(j, t * group + u, ascending, n_blocks) for j in range(S5_NQ)] for u in range(group)]
        xs = [[(xr_ref[rows, :], xi_ref[rows, :]) for rows in row] for row in blocks]
        states = list(carry)
        done = []
        for u in range(group):
            states = [_cmul_add(lr, li, sr, si, xr, xi) for (sr, si), (xr, xi) in zip(states, xs[u])]
            done.append(states)
        for u in range(group):
            for rows, (nr, ni) in zip(blocks[u], done[u]):
                xr_ref[rows, :] = nr
                xi_ref[rows, :] = ni
        return tuple(states)

    return lax.fori_loop(0, steps // group, trip, init)


def _segment_starts(w, lsr, lsi, ascending):
    shape = w[0][0].shape
    row = lax.broadcasted_iota(jnp.int32, shape, 0)
    keep = row != jnp.where(ascending, 0, SCAN_LANES - 1)

    def shift(t):
        t = jnp.where(ascending, pltpu.roll(t, 1, 0), pltpu.roll(t, SCAN_LANES - 1, 0))
        return jnp.where(keep, t, 0.0)

    zero = jnp.zeros(shape, F32)
    c = [(zero, zero)] * S5_NQ
    for _ in range(SCAN_LANES):
        tr, ti = _cmul_add(lsr, lsi, *c[-1], *w[-1])
        c[0] = (shift(tr), shift(ti))
        for j in range(1, S5_NQ):
            c[j] = _cmul_add(lsr, lsi, *c[j - 1], *w[j - 1])
    return tuple(c)


def _first_pass(xr_ref, xi_ref, lam_ref, ascending, n_blocks, conj):
    shape = (SCAN_LANES, xr_ref.shape[1])
    sign = -1.0 if conj else 1.0
    lr = jnp.broadcast_to(lam_ref[0:1, :], shape)
    li = sign * jnp.broadcast_to(lam_ref[1:2, :], shape)
    lsr = jnp.broadcast_to(lam_ref[2:3, :], shape)
    lsi = sign * jnp.broadcast_to(lam_ref[3:4, :], shape)
    zero = jnp.zeros(shape, F32)
    w = _scan(xr_ref, xi_ref, lr, li, ((zero, zero),) * S5_NQ, ascending, n_blocks, store=False)
    return _segment_starts(w, lsr, lsi, ascending), lr, li


def _s5_specs(T):
    NS = S5_NS
    tok = pl.BlockSpec((T, 128), lambda c, d, h: (0, c))
    b_spec = pl.BlockSpec((None, None, None, 128, NS), lambda c, d, h: (d, c, h, 0, 0))
    c_spec = pl.BlockSpec((None, None, None, NS, 128), lambda c, d, h: (d, c, h, 0, 0))
    lam_spec = pl.BlockSpec((None, None, None, 4, NS), lambda c, d, h: (d, c, h, 0, 0))
    return tok, b_spec, c_spec, lam_spec


def _s5_fwd(zp, bre, bim, lam, cre, cimn, comm=None):
    T = zp.shape[0]
    NS = S5_NS
    nb = T // SCAN_LANES
    rc = min(S5_RC, T)
    tok, b_spec, c_spec, lam_spec = _s5_specs(T)

    def body(zp_ref, bre_ref, bim_ref, lam_ref, cre_ref, cim_ref, y_ref, xr_ref, xi_ref):
        d = pl.program_id(1)
        ascending = d == 0

        @pl.when((d == 0) & (pl.program_id(2) == 0))
        def _():
            y_ref[...] = jnp.zeros_like(y_ref)

        def proj(c, _):
            rows = pl.ds(pl.multiple_of(c * rc, rc), rc)
            zz = zp_ref[rows, :]
            xr_ref[rows, :] = _dot(zz, bre_ref[...])
            xi_ref[rows, :] = _dot(zz, bim_ref[...])
            return 0

        lax.fori_loop(0, T // rc, proj, 0)
        starts, lr, li = _first_pass(xr_ref, xi_ref, lam_ref, ascending, nb, conj=False)
        _scan(xr_ref, xi_ref, lr, li, starts, ascending, nb, store=True)

        def outp(c, _):
            rows = pl.ds(pl.multiple_of(c * rc, rc), rc)
            y_ref[rows, :] += (_dot(xr_ref[rows, :].astype(BF16), cre_ref[...])
                               + _dot(xi_ref[rows, :].astype(BF16), cim_ref[...]))
            return 0

        lax.fori_loop(0, T // rc, outp, 0)

    return _pallas(
        body, "s5_fwd", (S5_NCB, 2, S5_NH),
        [tok, b_spec, b_spec, lam_spec, c_spec, c_spec], [tok],
        [jax.ShapeDtypeStruct((T, SSM_WIDTH), F32)],
        [pltpu.VMEM((T, NS), F32), pltpu.VMEM((T, NS), F32)],
        ("parallel", "arbitrary", "arbitrary"), (zp, bre, bim, lam, cre, cimn), comm)


def _s5_bwd(zp, dyp, bre, bim, lam, cre, cimn, comm=None):
    T = zp.shape[0]
    NS, NH = S5_NS, S5_NH
    nb = T // SCAN_LANES
    rc = min(S5_RC, T)
    tok, b_spec, c_spec, lam_spec = _s5_specs(T)
    dlam_spec = pl.BlockSpec((None, None, None, 2, NS), lambda c, d, h: (d, c, h, 0, 0))

    def body(zp_ref, dyp_ref, bre_ref, bim_ref, lam_ref, cre_ref, cim_ref,
             dzp_ref, dbre_ref, dbim_ref, dlam_ref, dcre_ref, dcim_ref,
             sr_ref, si_ref, gr_ref, gi_ref):
        d = pl.program_id(1)
        ascending = d == 0
        g_ascending = d != 0

        @pl.when((d == 0) & (pl.program_id(2) == 0))
        def _():
            dzp_ref[...] = jnp.zeros_like(dzp_ref)

        dcre_ref[...] = jnp.zeros_like(dcre_ref)
        dcim_ref[...] = jnp.zeros_like(dcim_ref)
        dbre_ref[...] = jnp.zeros_like(dbre_ref)
        dbim_ref[...] = jnp.zeros_like(dbim_ref)

        def proj(c, _):
            rows = pl.ds(pl.multiple_of(c * rc, rc), rc)
            zz = zp_ref[rows, :]
            sr_ref[rows, :] = _dot(zz, bre_ref[...])
            si_ref[rows, :] = _dot(zz, bim_ref[...])
            dy = dyp_ref[rows, :]
            gr_ref[rows, :] = _dot_nt(dy, cre_ref[...])
            gi_ref[rows, :] = _dot_nt(dy, cim_ref[...])
            return 0

        lax.fori_loop(0, T // rc, proj, 0)
        s_starts, lr, li = _first_pass(sr_ref, si_ref, lam_ref, ascending, nb, conj=False)
        _scan(sr_ref, si_ref, lr, li, s_starts, ascending, nb, store=True)
        g_starts, lr, lic = _first_pass(gr_ref, gi_ref, lam_ref, g_ascending, nb, conj=True)

        steps = nb // S5_NQ
        group = S5_GROUP
        assert steps % group == 0

        def gtrip(t, carry, last):
            g, (ar, ai) = carry
            first = t * group
            blocks = [[_chain_block(j, first + u, g_ascending, nb) for j in range(S5_NQ)] for u in range(group)]
            direct = [[(gr_ref[rows, :], gi_ref[rows, :]) for rows in row] for row in blocks]
            done = []
            for u in range(group):
                new = []
                for j, ((g_r, g_i), (d_r, d_i)) in enumerate(zip(g, direct[u])):
                    n_r, n_i = _cmul_add(lr, lic, g_r, g_i, d_r, d_i)
                    if last and u == group - 1:
                        s_r, s_i = s_starts[S5_NQ - 1 - j]
                    else:
                        prev = _chain_block(j, first + u + 1, g_ascending, nb)
                        s_r, s_i = sr_ref[prev, :], si_ref[prev, :]
                    ar = ar + n_r * s_r + n_i * s_i
                    ai = ai + n_i * s_r - n_r * s_i
                    new.append((n_r, n_i))
                g = new
                done.append(new)
            for u in range(group):
                for rows, (n_r, n_i) in zip(blocks[u], done[u]):
                    gr_ref[rows, :] = n_r
                    gi_ref[rows, :] = n_i
            return tuple(g), (ar, ai)

        zero = jnp.zeros((SCAN_LANES, NS), F32)
        carry = lax.fori_loop(0, steps // group - 1, lambda t, c: gtrip(t, c, False), (g_starts, (zero, zero)))
        _, (ar, ai) = gtrip(steps // group - 1, carry, True)
        dlam_ref[0:1, :] = _col_sum(ar)
        dlam_ref[1:2, :] = _col_sum(ai)

        def grads(c, _):
            rows = pl.ds(pl.multiple_of(c * rc, rc), rc)
            zz = zp_ref[rows, :]
            dy = dyp_ref[rows, :]
            g_rb = gr_ref[rows, :].astype(BF16)
            g_ib = gi_ref[rows, :].astype(BF16)
            dcre_ref[...] += _dot_tn(sr_ref[rows, :].astype(BF16), dy)
            dcim_ref[...] += _dot_tn(si_ref[rows, :].astype(BF16), dy)
            dbre_ref[...] += _dot_tn(zz, g_rb)
            dbim_ref[...] += _dot_tn(zz, g_ib)
            dzp_ref[rows, :] += _dot_nt(g_rb, bre_ref[...]) + _dot_nt(g_ib, bim_ref[...])
            return 0

        lax.fori_loop(0, T // rc, grads, 0)

    f32 = lambda *s: jax.ShapeDtypeStruct(s, F32)
    return _pallas(
        body, "s5_bwd", (S5_NCB, 2, S5_NH),
        [tok, tok, b_spec, b_spec, lam_spec, c_spec, c_spec],
        [tok, b_spec, b_spec, dlam_spec, c_spec, c_spec],
        [f32(T, SSM_WIDTH), f32(2, S5_NCB, NH, 128, NS), f32(2, S5_NCB, NH, 128, NS),
         f32(2, S5_NCB, NH, 2, NS), f32(2, S5_NCB, NH, NS, 128), f32(2, S5_NCB, NH, NS, 128)],
        [pltpu.VMEM((T, NS), F32)] * 4,
        ("parallel", "arbitrary", "arbitrary"), (zp, dyp, bre, bim, lam, cre, cimn), comm)


def _s5_delta():
    d = np.zeros((S5_NH, 8, 8 // S5_NH), np.float32)
    for h in range(S5_NH):
        for go in range(8 // S5_NH):
            d[h, h * (8 // S5_NH) + go, go] = 1.0
    return d


def _s5_pack_b(bbt):
    gh = 8 // S5_NH
    b5 = bbt.reshape(S5_NCB, S5_NH, gh, SSM_GROUP, SSM_STATE).transpose(0, 1, 3, 2, 4)
    m = b5[:, :, None] * _s5_delta()[None, :, :, None, :, None]
    return m.reshape(S5_NCB, S5_NH, 128, S5_NS)


def _s5_unpack_b(dm):
    gh = 8 // S5_NH
    d6 = dm.reshape(S5_NCB, S5_NH, 8, SSM_GROUP, gh, SSM_STATE)
    b5 = jnp.sum(d6 * _s5_delta()[None, :, :, None, :, None], axis=2)
    return b5.transpose(0, 1, 3, 2, 4).reshape(SSM_GROUPS, SSM_GROUP, SSM_STATE)


def _s5_pack_c(c):
    gh = 8 // S5_NH
    c5 = c.reshape(S5_NCB, S5_NH, gh, SSM_GROUP, SSM_STATE).transpose(0, 1, 2, 4, 3)
    m = c5[:, :, :, :, None, :] * _s5_delta().transpose(0, 2, 1)[None, :, :, None, :, None]
    return m.reshape(S5_NCB, S5_NH, S5_NS, 128)


def _s5_unpack_c(dm):
    gh = 8 // S5_NH
    d6 = dm.reshape(S5_NCB, S5_NH, gh, SSM_STATE, 8, SSM_GROUP)
    c5 = jnp.sum(d6 * _s5_delta().transpose(0, 2, 1)[None, :, :, None, :, None], axis=4)
    return c5.transpose(0, 1, 2, 4, 3).reshape(SSM_GROUPS, SSM_GROUP, SSM_STATE)


def _s5_pack_lam(x):
    return x.reshape(S5_NCB, S5_NH, S5_NS)


def _permute_rows(x):
    T = x.shape[0]
    return x.reshape(SCAN_LANES, T // SCAN_LANES, -1).transpose(1, 0, 2).reshape(T, -1)


def _unpermute_rows(x):
    T = x.shape[0]
    return x.reshape(T // SCAN_LANES, SCAN_LANES, -1).transpose(1, 0, 2).reshape(T, -1)


ATT_TB = ATT_ROWS * GRID_W
ATT_KB = 3 * ATT_TB


def _att_valid(i, n_rows):
    qi, kj = np.meshgrid(np.arange(ATT_TB), np.arange(ATT_KB), indexing="ij")
    r = i * ATT_ROWS + qi // GRID_W
    c = qi % GRID_W
    rk = (i - 1) * ATT_ROWS + kj // GRID_W
    x = kj % GRID_W
    rs = np.clip(r - WIN_H // 2, 0, n_rows - WIN_H)
    cs = np.clip(c - WIN_W // 2, 0, GRID_W - WIN_W)
    return (rk >= rs) & (rk < rs + WIN_H) & (x >= cs) & (x < cs + WIN_W)


def _att_masked_tables(table, n_rows):
    n = n_rows // ATT_ROWS
    assert n >= 3
    masks = np.stack([_att_valid(i, n_rows) for i in (0, 1, n - 1)])
    return jnp.where(masks[:, None], table[None], NEG_INF)


def _att_variant(i, n):
    return jnp.where(i == 0, 0, jnp.where(i >= n - 1, 2, 1))


def _att_exp(qh, kh, bias):
    s = _dot_nt(qh, kh) + bias
    return jnp.exp(s - jnp.max(s, axis=1, keepdims=True))


def _att_values_and_ones(vh):
    return jnp.concatenate([vh, jnp.ones_like(vh)], axis=1)


def _att_specs(n, col):
    last = n - 1
    cur = lambda i: (jnp.minimum(i, last), col)
    prv = lambda i: (jnp.maximum(jnp.minimum(i, last) - 1, 0), col)
    nxt = lambda i: (jnp.minimum(i + 1, last), col)
    blk = lambda f: pl.BlockSpec((ATT_TB, ATT_WIDTH), f)
    return blk(cur), blk(prv), blk(nxt)


def _att_fwd(zb, biasv):
    T = zb.shape[0]
    W = ATT_WIDTH
    n = T // ATT_TB
    n_rows = T // GRID_W
    cur = _att_specs(n, 0)[0]
    q_cur = _att_specs(n, 1)[0]
    k_cur, k_prv, k_nxt = _att_specs(n, 2)
    v_cur, v_prv, v_nxt = _att_specs(n, 3)

    def body(q_ref, kp_ref, kc_ref, kn_ref, vp_ref, vc_ref, vn_ref, b_ref, y_ref):
        qs = q_ref[...] * 0.125
        kb = jnp.concatenate([kp_ref[...], kc_ref[...], kn_ref[...]], axis=0)
        vb = jnp.concatenate([vp_ref[...], vc_ref[...], vn_ref[...]], axis=0)
        outs = []
        for h in range(ATT_HEADS):
            hs = slice(h * ATT_HEAD_DIM, (h + 1) * ATT_HEAD_DIM)
            e = _att_exp(qs[:, hs], kb[:, hs], b_ref[h]).astype(BF16)
            ov = _dot(e, _att_values_and_ones(vb[:, hs]))
            outs.append(ov[:, :ATT_HEAD_DIM] * (1.0 / ov[:, ATT_HEAD_DIM:ATT_HEAD_DIM + 1]))
        y_ref[...] = jnp.concatenate(outs, axis=1).astype(BF16)

    return pl.pallas_call(
        body, name="att_fwd", grid=(n,),
        in_specs=[q_cur, k_prv, k_cur, k_nxt, v_prv, v_cur, v_nxt,
                  pl.BlockSpec((None, ATT_HEADS, ATT_TB, ATT_KB), lambda i: (_att_variant(i, n), 0, 0, 0))],
        out_specs=cur,
        out_shape=jax.ShapeDtypeStruct((T, W), BF16),
        compiler_params=_cparams(("parallel",), VMEM_LIMIT),
    )(zb, zb, zb, zb, zb, zb, zb, biasv)


def _att_bwd(zb, y, do, biasv, comm=None):
    T = zb.shape[0]
    W = ATT_WIDTH
    n = T // ATT_TB
    n_rows = T // GRID_W
    cur = _att_specs(n, 0)[0]
    q_cur = _att_specs(n, 1)[0]
    k_cur, k_prv, k_nxt = _att_specs(n, 2)
    v_cur, v_prv, v_nxt = _att_specs(n, 3)
    done = pl.BlockSpec((ATT_TB, W), lambda i: (jnp.maximum(i - 1, 0), 0))
    bias_spec = pl.BlockSpec((None, ATT_HEADS, ATT_TB, ATT_KB), lambda i: (_att_variant(i, n), 0, 0, 0))

    def body(q_ref, y_ref, do_ref, kp_ref, kc_ref, kn_ref, vp_ref, vc_ref, vn_ref, b_ref,
             dq_ref, dk_ref, dv_ref, db_ref, acck_ref, accv_ref):
        i = pl.program_id(0)

        @pl.when(i == 0)
        def _():
            db_ref[...] = jnp.zeros_like(db_ref)
            acck_ref[...] = jnp.zeros_like(acck_ref)
            accv_ref[...] = jnp.zeros_like(accv_ref)

        @pl.when((i > 0) & (i < n))
        def _():
            slot = lax.rem(i + 1, 3)
            acck_ref[slot] = jnp.zeros((ATT_TB, W), F32)
            accv_ref[slot] = jnp.zeros((ATT_TB, W), F32)

        @pl.when(i < n)
        def _():
            qs = q_ref[...] * 0.125
            dob = do_ref[...]
            dy = dob.astype(F32) * y_ref[...].astype(F32)
            kb = jnp.concatenate([kp_ref[...], kc_ref[...], kn_ref[...]], axis=0)
            vb = jnp.concatenate([vp_ref[...], vc_ref[...], vn_ref[...]], axis=0)
            dqs, dks, dvs = [], [], []
            for h in range(ATT_HEADS):
                hs = slice(h * ATT_HEAD_DIM, (h + 1) * ATT_HEAD_DIM)
                qh, kh, vh, doh = qs[:, hs], kb[:, hs], vb[:, hs], dob[:, hs]
                e = _att_exp(qh, kh, b_ref[h])
                p = e * (1.0 / jnp.sum(e, axis=1, keepdims=True))
                dp = _dot_nt(doh, vh)
                ds = p * (dp - jnp.sum(dy[:, hs], axis=1, keepdims=True))
                db_ref[h] += ds
                dsb = ds.astype(BF16)
                dqs.append(_dot(dsb, kh) * 0.125)
                dks.append(_dot_tn(dsb, qh))
                dvs.append(_dot_tn(p.astype(BF16), doh))
            dq_ref[...] = jnp.concatenate(dqs, axis=1).astype(BF16)
            dk_all = jnp.concatenate(dks, axis=1)
            dv_all = jnp.concatenate(dvs, axis=1)
            for b in range(3):
                slot = lax.rem(i + 2 + b, 3)
                rows = slice(b * ATT_TB, (b + 1) * ATT_TB)
                acck_ref[slot] += dk_all[rows]
                accv_ref[slot] += dv_all[rows]

        slot = lax.rem(i + 2, 3)
        dk_ref[...] = acck_ref[slot].astype(BF16)
        dv_ref[...] = accv_ref[slot].astype(BF16)

    return _pallas(
        body, "att_bwd", (n + 1,),
        [q_cur, cur, cur, k_prv, k_cur, k_nxt, v_prv, v_cur, v_nxt, bias_spec],
        [cur, done, done, pl.BlockSpec((ATT_HEADS, ATT_TB, ATT_KB), lambda i: (0, 0, 0))],
        [jax.ShapeDtypeStruct((T, W), BF16)] * 3 + [jax.ShapeDtypeStruct((ATT_HEADS, ATT_TB, ATT_KB), F32)],
        [pltpu.VMEM((3, ATT_TB, W), F32), pltpu.VMEM((3, ATT_TB, W), F32)],
        ("arbitrary",), (zb, y, do, zb, zb, zb, zb, zb, zb, biasv), comm)


def _att_selectors():
    rsel = np.zeros((ATT_ROWS, 3 * ATT_ROWS, 2 * WIN_H - 1), np.float32)
    for a in range(ATT_ROWS):
        for b in range(3 * ATT_ROWS):
            rsel[a, b, b - a - ATT_ROWS + WIN_H - 1] = 1.0
    csel = np.zeros((GRID_W, GRID_W, 2 * WIN_W - 1), np.float32)
    for c in range(GRID_W):
        for x in range(GRID_W):
            csel[c, x, min(max(x - c, -(WIN_W - 1)), WIN_W - 1) + WIN_W - 1] = 1.0
    return rsel, csel


def _att_bias_table(rpb):
    rsel, csel = _att_selectors()
    hi = lax.Precision.HIGHEST
    t = jnp.einsum('hrd,abr->habd', rpb, rsel, precision=hi)
    t = jnp.einsum('habd,cxd->hacbx', t, csel, precision=hi)
    return t.reshape(ATT_HEADS, ATT_TB, ATT_KB)


def _att_bias_table_t(dtable):
    rsel, csel = _att_selectors()
    hi = lax.Precision.HIGHEST
    t = dtable.reshape(ATT_HEADS, ATT_ROWS, GRID_W, 3 * ATT_ROWS, GRID_W)
    t = jnp.einsum('hacbx,cxd->habd', t, csel, precision=hi)
    return jnp.einsum('habd,abr->hrd', t, rsel, precision=hi)


GELU_K = math.sqrt(2.0 / math.pi)
GELU_C = 0.044715
MERGE_TM = 256


def _gelu(x):
    return 0.5 * x * (1.0 + jnp.tanh(GELU_K * (x + GELU_C * x * x * x)))


def _gelu_grad(x):
    t = jnp.tanh(GELU_K * (x + GELU_C * x * x * x))
    return 0.5 * (1.0 + t) + 0.5 * x * (1.0 - t * t) * GELU_K * (1.0 + 3.0 * GELU_C * x * x)


def _merge_forward(ypre, zs, gs, ga, ya, ssm_d, w_glu, b_glu, w_bs, w_ba):
    ys = ypre + ssm_d * zs
    yg = _gelu(ys)
    sg = jax.nn.sigmoid(_dot(yg.astype(BF16), w_glu) + b_glu)
    y2 = yg * sg
    bs = _dot(y2.astype(BF16), w_bs)
    ba = _dot(ya, w_ba)
    s1 = jax.nn.sigmoid(gs)
    s2 = jax.nn.sigmoid(ga)
    merged = s1 * bs + s2 * ba
    return ys, yg, sg, y2, bs, ba, s1, s2, merged


def _merge_in_specs(D, W, tm):
    tok = lambda w, c: pl.BlockSpec((tm, w), lambda i: (i, c))
    full = lambda r, c: pl.BlockSpec((r, c), lambda i: (0, 0))
    z_specs = [tok(W, 0), tok(D, 4 * W // D), tok(D, 4 * W // D + 1)]
    w_specs = [full(1, W), full(W, W), full(1, W), full(W, D), full(W, D), full(D, D)]
    return tok, z_specs, w_specs


def _merge_fwd(ypre, z, ya, h1, ssm_d, w_glu, b_glu, w_bs, w_ba, w_out):
    T, D = h1.shape
    W = ypre.shape[1]
    tm = min(T, 2 * MERGE_TM)
    tok, z_specs, w_specs = _merge_in_specs(D, W, tm)

    def body(ypre_ref, zs_ref, gs_ref, ga_ref, ya_ref, h1_ref, d_ref, wglu_ref, bglu_ref, wbs_ref, wba_ref, wout_ref,
             h2_ref):
        merged = _merge_forward(ypre_ref[...], zs_ref[...], gs_ref[...], ga_ref[...], ya_ref[...], d_ref[...],
                                wglu_ref[...], bglu_ref[...], wbs_ref[...], wba_ref[...])[-1]
        h2_ref[...] = h1_ref[...] + _dot(merged.astype(BF16), wout_ref[...])

    return pl.pallas_call(
        body, name="merge_fwd", grid=(T // tm,),
        in_specs=[tok(W, 0)] + z_specs + [tok(W, 0), tok(D, 0)] + w_specs,
        out_specs=tok(D, 0),
        out_shape=jax.ShapeDtypeStruct((T, D), F32),
        compiler_params=_cparams(("parallel",), VMEM_LIMIT),
    )(ypre, z, z, z, ya, h1, ssm_d, w_glu, b_glu, w_bs, w_ba, w_out)


def _merge_bwd(dh2, ypre, z, ya, ssm_d, w_glu, b_glu, w_bs, w_ba, w_out):
    T, D = dh2.shape
    W = ypre.shape[1]
    tm = min(T, MERGE_TM)
    tok, z_specs, w_specs = _merge_in_specs(D, W, tm)

    def body(dh2_ref, ypre_ref, zs_ref, gs_ref, ga_ref, ya_ref, d_ref, wglu_ref, bglu_ref, wbs_ref, wba_ref, wout_ref,
             dypre_ref, dzs_ref, dgs_ref, dga_ref, dya_ref, dd_ref, dwglu_ref, dbglu_ref, dwbs_ref, dwba_ref, dwout_ref):
        @pl.when(pl.program_id(0) == 0)
        def _():
            for r in (dd_ref, dwglu_ref, dbglu_ref, dwbs_ref, dwba_ref, dwout_ref):
                r[...] = jnp.zeros_like(r)

        zs = zs_ref[...]
        ya = ya_ref[...]
        ys, yg, sg, y2, bs, ba, s1, s2, merged = _merge_forward(
            ypre_ref[...], zs, gs_ref[...], ga_ref[...], ya, d_ref[...],
            wglu_ref[...], bglu_ref[...], wbs_ref[...], wba_ref[...])
        dh2b = dh2_ref[...].astype(BF16)
        dmerged = _dot_nt(dh2b, wout_ref[...])
        dwout_ref[...] += _dot_tn(merged.astype(BF16), dh2b)
        dbs = (dmerged * s1).astype(BF16)
        dba = (dmerged * s2).astype(BF16)
        dgs_ref[...] = (dmerged * bs * s1 * (1.0 - s1)).astype(BF16)
        dga_ref[...] = (dmerged * ba * s2 * (1.0 - s2)).astype(BF16)
        dwbs_ref[...] += _dot_tn(y2.astype(BF16), dbs)
        dwba_ref[...] += _dot_tn(ya, dba)
        dya_ref[...] = _dot_nt(dba, wba_ref[...]).astype(BF16)
        dy2 = _dot_nt(dbs, wbs_ref[...])
        dvv = dy2 * yg * sg * (1.0 - sg)
        dvvb = dvv.astype(BF16)
        dyg = dy2 * sg + _dot_nt(dvvb, wglu_ref[...])
        dwglu_ref[...] += _dot_tn(yg.astype(BF16), dvvb)
        dbglu_ref[...] += _col_sum(dvv)
        dys = dyg * _gelu_grad(ys)
        dd_ref[...] += _col_sum(dys * zs)
        dzs_ref[...] = dys * d_ref[...]
        dypre_ref[...] = dys.astype(BF16)

    f32 = lambda *s: jax.ShapeDtypeStruct(s, F32)
    b16 = lambda *s: jax.ShapeDtypeStruct(s, BF16)
    return pl.pallas_call(
        body, name="merge_bwd", grid=(T // tm,),
        in_specs=[tok(D, 0), tok(W, 0)] + z_specs + [tok(W, 0)] + w_specs,
        out_specs=[tok(W, 0), tok(W, 0), tok(D, 0), tok(D, 0), tok(W, 0)] + w_specs,
        out_shape=[b16(T, W), f32(T, W), b16(T, D), b16(T, D), b16(T, W),
                   f32(1, W), f32(W, W), f32(1, W), f32(W, D), f32(W, D), f32(D, D)],
        compiler_params=_cparams(("arbitrary",), VMEM_LIMIT),
    )(dh2, ypre, z, z, z, ya, ssm_d, w_glu, b_glu, w_bs, w_ba, w_out)


def _cast_shards(weights):
    def body(*refs):
        n = len(refs) // 2
        for src, dst in zip(refs[:n], refs[n:]):
            dst[...] = src[0].astype(BF16)

    return pl.pallas_call(
        body, name="cast_shards",
        out_shape=[jax.ShapeDtypeStruct(w.shape[1:], BF16) for w in weights],
        compiler_params=_cparams(None, VMEM_LIMIT))(*weights)


def _gather_two_level(shards, name):
    n = len(shards)

    def body(*refs):
        x_refs, out_refs = refs[:n], refs[n:2 * n]
        send_sems, recv_sems, local_sems = refs[2 * n:]
        x, y, c = _my_place()
        me, sibling = (x, y, c), (x, y, 1 - c)
        chips = [(1 - x, y), (x, 1 - y), (1 - x, 1 - y)]

        def copy(a, k, block, to, own=False):
            slot = out_refs[a].at[_flat(*block)]
            return pltpu.make_async_remote_copy(
                src_ref=x_refs[a] if own else slot, dst_ref=slot,
                send_sem=send_sems.at[7 * a + k], recv_sem=recv_sems.at[7 * a + k],
                device_id=to, device_id_type=MESH_ID)

        sent, local = [], []
        for a in range(n):
            local.append(pltpu.make_async_copy(x_refs[a], out_refs[a].at[_flat(*me)], local_sems.at[a]))
            local[-1].start()
            sent.append(copy(a, 0, me, sibling, own=True))
            sent += [copy(a, 1 + j, me, (*chip, c), own=True) for j, chip in enumerate(chips)]
        for cp in sent:
            cp.start()
        for a in range(n):
            for j, chip in enumerate(chips):
                copy(a, 1 + j, (*chip, c), me).wait_recv()
                sent.append(copy(a, 4 + j, (*chip, c), sibling))
                sent[-1].start()
        for a in range(n):
            copy(a, 0, sibling, me).wait_recv()
            for j, chip in enumerate(chips):
                copy(a, 4 + j, (*chip, 1 - c), me).wait_recv()
        for cp in sent:
            cp.wait_send()
        for cp in local:
            cp.wait()

    return pl.pallas_call(
        body, name=name, in_specs=[_HBM] * n, out_specs=[_HBM] * n,
        out_shape=[jax.ShapeDtypeStruct((N_DEV,) + s.shape, s.dtype) for s in shards],
        scratch_shapes=[pltpu.SemaphoreType.DMA((7 * n,)), pltpu.SemaphoreType.DMA((7 * n,)),
                        pltpu.SemaphoreType.DMA((n,))],
    )(*shards)


PACK_COLS = 1024
BIG = (("ffn1_w_gate", 1), ("ffn1_w_up", 1), ("ffn1_w_down", 0), ("w_in", 1), ("ssm_w_glu", 0),
       ("w_branch_ssm", 1), ("w_branch_att", 1), ("w_out", 0),
       ("ffn2_w_gate", 1), ("ffn2_w_up", 1), ("ffn2_w_down", 0))
BIG_AXIS = dict(BIG)
TRANSPOSED = ("ffn1_w_gate", "ffn1_w_up", "ffn2_w_gate", "ffn2_w_up")
SSM_DIR = ("ssm_a_re", "ssm_a_im", "ssm_log_dt", "ssm_b_re", "ssm_b_im", "ssm_c_re", "ssm_c_im")
SMALL_EARLY = (("mix_norm",) + tuple(n + "_fwd" for n in SSM_DIR) + tuple(n + "_bwd" for n in SSM_DIR)
               + ("ssm_d", "ssm_b_glu", "att_rpb", "ffn2_norm", "final_norm"))
SMALL_LATE = ("ffn1_norm",)
WEIGHTS = ("ffn1_norm", "ffn1_w_gate", "ffn1_w_up", "ffn1_w_down", "mix_norm", "w_in") \
    + tuple(n + "_fwd" for n in SSM_DIR) + tuple(n + "_bwd" for n in SSM_DIR) \
    + ("ssm_d", "ssm_w_glu", "ssm_b_glu", "att_rpb", "w_branch_ssm", "w_branch_att", "w_out",
       "ffn2_norm", "ffn2_w_gate", "ffn2_w_up", "ffn2_w_down", "final_norm")


def _pad_rows(a, mult):
    pad = (-a.shape[-2]) % mult
    if pad:
        a = jnp.concatenate([a, jnp.zeros(a.shape[:-2] + (pad, a.shape[-1]), a.dtype)], axis=-2)
    return a


def _pack(arrays, row_mult):
    flat = jnp.concatenate([a.reshape(-1) for a in arrays])
    pad = (-flat.shape[0]) % PACK_COLS
    if pad:
        flat = jnp.concatenate([flat, jnp.zeros((pad,), flat.dtype)])
    return _pad_rows(flat.reshape(-1, PACK_COLS), row_mult)


def _unpack(slab, shapes):
    flat = slab.reshape(-1)
    out, at = [], 0
    for s in shapes:
        n = int(np.prod(s))
        out.append(flat[at:at + n].reshape(s))
        at += n
    return out


def _split_for_devices(g, axis):
    r, c = g.shape
    if axis == 1:
        return g.reshape(r, N_DEV, c // N_DEV).transpose(1, 0, 2).astype(BF16)
    return g.reshape(N_DEV, r // N_DEV, c).astype(BF16)


def _join_shards(gathered, axis):
    _, r, c = gathered.shape
    if axis == 1:
        return gathered.transpose(1, 0, 2).reshape(r, N_DEV * c)
    return gathered.reshape(N_DEV * r, c)


def _s5_direction_inputs(p, sfx, chain_len):
    bt_re = p["ssm_b_re" + sfx][0].transpose(0, 2, 1)
    bt_im = p["ssm_b_im" + sfx][0].transpose(0, 2, 1)
    raw = (p["ssm_a_re" + sfx][0], p["ssm_a_im" + sfx][0], p["ssm_log_dt" + sfx][0][:, None], bt_re, bt_im)
    lr, li, sr, si, bbr, bbi = _disc_fwd(*raw, chain_len,"s5_disc" + sfx)
    lam = jnp.stack([_s5_pack_lam(t) for t in (lr, li, sr, si)], axis=2)
    mats = (_s5_pack_b(bbr), _s5_pack_b(bbi), lam,
            _s5_pack_c(p["ssm_c_re" + sfx][0]), _s5_pack_c(-p["ssm_c_im" + sfx][0]))
    return raw, mats


def kernel(x, ffn1_norm, ffn1_w_gate, ffn1_w_up, ffn1_w_down, mix_norm, w_in, ssm_a_re_fwd, ssm_a_im_fwd, ssm_log_dt_fwd, ssm_b_re_fwd, ssm_b_im_fwd, ssm_c_re_fwd, ssm_c_im_fwd, ssm_a_re_bwd, ssm_a_im_bwd, ssm_log_dt_bwd, ssm_b_re_bwd, ssm_b_im_bwd, ssm_c_re_bwd, ssm_c_im_bwd, ssm_d, ssm_w_glu, ssm_b_glu, att_rpb, w_branch_ssm, w_branch_att, w_out, ffn2_norm, ffn2_w_gate, ffn2_w_up, ffn2_w_down, final_norm, loss_target, m_ffn1_norm, m_ffn1_w_gate, m_ffn1_w_up, m_ffn1_w_down, m_mix_norm, m_w_in, m_ssm_a_re_fwd, m_ssm_a_im_fwd, m_ssm_log_dt_fwd, m_ssm_b_re_fwd, m_ssm_b_im_fwd, m_ssm_c_re_fwd, m_ssm_c_im_fwd, m_ssm_a_re_bwd, m_ssm_a_im_bwd, m_ssm_log_dt_bwd, m_ssm_b_re_bwd, m_ssm_b_im_bwd, m_ssm_c_re_bwd, m_ssm_c_im_bwd, m_ssm_d, m_ssm_w_glu, m_ssm_b_glu, m_att_rpb, m_w_branch_ssm, m_w_branch_att, m_w_out, m_ffn2_norm, m_ffn2_w_gate, m_ffn2_w_up, m_ffn2_w_down, m_final_norm, v_ffn1_norm, v_ffn1_w_gate, v_ffn1_w_up, v_ffn1_w_down, v_mix_norm, v_w_in, v_ssm_a_re_fwd, v_ssm_a_im_fwd, v_ssm_log_dt_fwd, v_ssm_b_re_fwd, v_ssm_b_im_fwd, v_ssm_c_re_fwd, v_ssm_c_im_fwd, v_ssm_a_re_bwd, v_ssm_a_im_bwd, v_ssm_log_dt_bwd, v_ssm_b_re_bwd, v_ssm_b_im_bwd, v_ssm_c_re_bwd, v_ssm_c_im_bwd, v_ssm_d, v_ssm_w_glu, v_ssm_b_glu, v_att_rpb, v_w_branch_ssm, v_w_branch_att, v_w_out, v_ffn2_norm, v_ffn2_w_gate, v_ffn2_w_up, v_ffn2_w_down, v_final_norm):
    p = dict(locals())
    x = p["x"][0]
    target = p["loss_target"][0]
    T, D = x.shape

    stored = lambda a, n: jnp.swapaxes(a, -1, -2) if n in TRANSPOSED else a
    cut_axis = lambda n: 0 if n in TRANSPOSED else BIG_AXIS[n]
    shard = dict(zip([n for n, _ in BIG], _cast_shards([stored(p[n], n) for n, _ in BIG])))
    ffn1_w = ("ffn1_w_gate", "ffn1_w_up", "ffn1_w_down")
    mix_w = ("w_in", "ssm_w_glu", "w_branch_ssm", "w_branch_att", "w_out")
    ffn2_w = ("ffn2_w_gate", "ffn2_w_up", "ffn2_w_down")
    gathered = dict(zip(ffn1_w, _gather_two_level([shard[n] for n in ffn1_w], "gather_ffn1")))
    full = lambda n: _join_shards(gathered[n], cut_axis(n))

    h0 = x
    wg1, wu1, wd1 = [full(n) for n in ffn1_w]
    (h1, xn1, g1, u1), got = _ffn_fwd(h0, p["ffn1_norm"], wg1, wu1, wd1, "ffn1_fwd",
                                      _Comm("gather", [shard[n] for n in mix_w]))
    gathered.update(zip(mix_w, got))
    z, zb, un = _mixin_fwd(h1, p["mix_norm"], gathered["w_in"])
    W = SSM_WIDTH
    zp = _permute_rows(zb[:, :W])
    chain_len = T // SCAN_LANES // S5_NQ
    raw_f, mats_f = _s5_direction_inputs(p, "_fwd", chain_len)
    raw_b, mats_b = _s5_direction_inputs(p, "_bwd", chain_len)
    bre, bim, lam, cre, cimn = [jnp.stack([f, b]) for f, b in zip(mats_f, mats_b)]
    bre, bim, cre, cimn = [t.astype(BF16) for t in (bre, bim, cre, cimn)]
    (yp,), got = _s5_fwd(zp, bre, bim, lam, cre, cimn, _Comm("gather", [shard[n] for n in ffn2_w]))
    gathered.update(zip(ffn2_w, got))
    ypre = _unpermute_rows(yp)
    table = _att_masked_tables(_att_bias_table(p["att_rpb"][0]), T // GRID_W)
    ya = _att_fwd(zb, table)
    tail_w = (p["ssm_d"], full("ssm_w_glu"), p["ssm_b_glu"], full("w_branch_ssm"), full("w_branch_att"), full("w_out"))
    h2 = _merge_fwd(ypre, z, ya, h1, *tail_w)
    wg2, wu2, wd2 = [full(n) for n in ffn2_w]
    (h3, xn2, g2, u2), _ = _ffn_fwd(h2, p["ffn2_norm"], wg2, wu2, wd2, "ffn2_fwd")
    loss_part, dh3, d_final = _loss_head(h3, p["final_norm"][None], target)

    grads = {"final_norm": d_final[0]}
    to_send = lambda names: _Comm("exchange", [_split_for_devices(grads[n], cut_axis(n)) for n in names])
    parts = {}
    (dh2, grads["ffn2_norm"], do2, a2, dg2, du2), _ = _ffn_bwd(
        dh3, h2, p["ffn2_norm"], g2, u2, wg2, wu2, wd2, "ffn2_bwd")
    grads["ffn2_w_gate"] = _xty(dg2, xn2, "ffn2_dw_gate")
    grads["ffn2_w_up"] = _xty(du2, xn2, "ffn2_dw_up")
    grads["ffn2_w_down"] = _xty(a2, do2, "ffn2_dw_down")
    (dypre, dzs_skip, dgs, dga, dya, grads["ssm_d"], grads["ssm_w_glu"], grads["ssm_b_glu"],
     grads["w_branch_ssm"], grads["w_branch_att"], grads["w_out"]) = _merge_bwd(dh2, ypre, z, ya, *tail_w)
    (dq, dk, dv, dtable), got = _att_bwd(zb, ya, dya, table, to_send(ffn2_w))
    parts.update(zip(ffn2_w, got))
    grads["att_rpb"] = _att_bias_table_t(dtable)
    dyp = _permute_rows(dypre)
    tail_names = ("ssm_w_glu", "w_branch_ssm", "w_branch_att", "w_out")
    (dzp, dbre, dbim, dlam, dcre, dcimn), got = _s5_bwd(zp, dyp, bre, bim, lam, cre, cimn, to_send(tail_names))
    parts.update(zip(tail_names, got))
    G, P = SSM_GROUPS, SSM_STATE
    for d, (sfx, raw) in enumerate((("_fwd", raw_f), ("_bwd", raw_b))):
        da_re, da_im, dldt, dbt_re, dbt_im = _disc_bwd(
            *raw, dlam[d, :, :, 0, :].reshape(G, P), dlam[d, :, :, 1, :].reshape(G, P),
            _s5_unpack_b(dbre[d]), _s5_unpack_b(dbim[d]), "s5_disc_grad" + sfx)
        grads["ssm_a_re" + sfx] = da_re
        grads["ssm_a_im" + sfx] = da_im
        grads["ssm_log_dt" + sfx] = dldt[:, 0]
        grads["ssm_b_re" + sfx] = dbt_re.transpose(0, 2, 1)
        grads["ssm_b_im" + sfx] = dbt_im.transpose(0, 2, 1)
        grads["ssm_c_re" + sfx] = _s5_unpack_c(dcre[d])
        grads["ssm_c_im" + sfx] = -_s5_unpack_c(dcimn[d])
    dzs = _unpermute_rows(dzp) + dzs_skip
    dz = jnp.concatenate([dzs.astype(BF16), dq, dk, dv, dgs, dga], axis=1)
    dh1, grads["mix_norm"] = _mixin_bwd(dz, dh2, h1, p["mix_norm"], gathered["w_in"])
    grads["w_in"] = _xty(un, dz, "dw_in", col_shards=N_DEV)
    small_t = lambda a, n: jnp.swapaxes(a, -1, -2) if n.startswith("ssm_b_") else a
    pack_small = lambda names, src, pre: _pack([small_t(src[pre + n], n).astype(F32) for n in names], 8)
    early = _Comm(["exchange", "gather"],
                  [grads["w_in"], pack_small(SMALL_EARLY, grads, "")])
    (dh0, grads["ffn1_norm"], do1, a1, dg1, du1), (parts["w_in"], got_early) = _ffn_bwd(
        dh1, h0, p["ffn1_norm"], g1, u1, wg1, wu1, wd1, "ffn1_bwd", early)
    grads["ffn1_w_down"] = _xty(a1, do1, "ffn1_dw_down")
    grads["ffn1_w_gate"], (parts["ffn1_w_down"],) = _xty(dg1, xn1, "ffn1_dw_gate", to_send(("ffn1_w_down",)))
    grads["ffn1_w_up"], (parts["ffn1_w_gate"],) = _xty(du1, xn1, "ffn1_dw_up", to_send(("ffn1_w_gate",)))
    last = _Comm(["exchange", "gather"],
                 [_split_for_devices(grads["ffn1_w_up"], 0), pack_small(SMALL_LATE, grads, "")])
    parts["ffn1_w_up"], got_late = _comm_call(last, "exchange_last")
    got_small = jnp.concatenate([got_early, got_late], axis=1)

    results = {}
    for n, _ in BIG:
        outs = _adamw(parts[n], *[stored(p[pre + n][0], n) for pre in ("", "m_", "v_")], "adamw_" + n)
        results[n] = [stored(o, n)[None] for o in outs]
    early_rows = got_early.shape[1]
    slab = lambda pre: jnp.concatenate([pack_small(SMALL_EARLY, p, pre), pack_small(SMALL_LATE, p, pre)], axis=0)
    small_out = _adamw(got_small, slab(""), slab("m_"), slab("v_"), "adamw_small")
    for names, rows in ((SMALL_EARLY, slice(0, early_rows)), (SMALL_LATE, slice(early_rows, None))):
        shapes = [small_t(p[n], n).shape for n in names]
        for n, vals in zip(names, zip(*[_unpack(out[rows], shapes) for out in small_out])):
            results[n] = [small_t(val, n) for val in vals]

    loss = lax.psum(loss_part[0, 0], ("x", "y", "c"))
    out = [loss, dh0[None]]
    for kind in range(4):
        out += [results[n][kind] for n in WEIGHTS]
    return tuple(out)
```

```python
import functools
import math

import numpy as np
import jax
import jax.numpy as jnp
from jax import lax
from jax.experimental import pallas as pl
from jax.experimental.pallas import tpu as pltpu

F32 = jnp.float32
BF16 = jnp.bfloat16
MESH_ID = pl.DeviceIdType.MESH

SSM_GROUP = 16
SSM_GROUPS = 32
SSM_STATE = 64
SSM_WIDTH = 512
ATT_HEADS = 8
ATT_HEAD_DIM = 64
ATT_WIDTH = 512
GRID_W = 64
WIN_H = 8
WIN_W = 16
EPS = 1e-6
NEG_INF = -1e30
ADAM_LR = 0.001
ADAM_B1 = 0.9
ADAM_B2 = 0.999
ADAM_EPS = 1e-08
ADAM_WD = 0.01
ADAM_STEP = 10

N_DEV = 8
V7X_VMEM_BYTES = 64 * 1024 * 1024
VMEM_LIMIT = V7X_VMEM_BYTES - 8 * 1024 * 1024
SCAN_LANES = 8
ATT_ROWS = 4


def _cparams(sem, vmem=None):
    return pltpu.CompilerParams(dimension_semantics=sem, vmem_limit_bytes=vmem)


def _dot(a, b):
    return jnp.dot(a, b, preferred_element_type=F32)


def _dot_nt(a, b):
    return lax.dot_general(a, b, (((1,), (1,)), ((), ())), preferred_element_type=F32)


def _dot_tn(a, b):
    return lax.dot_general(a, b, (((0,), (0,)), ((), ())), preferred_element_type=F32)


def _rms(h):
    return lax.rsqrt(jnp.mean(h * h, axis=-1, keepdims=True) + EPS)


def _rms_bwd(h, r, v):
    return r * v - h * (r * r * r) * jnp.mean(h * v, axis=-1, keepdims=True)


def _col_sum(x):
    return jnp.sum(x, axis=0, keepdims=True)


def _my_place():
    return lax.axis_index("x"), lax.axis_index("y"), lax.axis_index("c")


def _flat(px, py, pc):
    return 4 * px + 2 * py + pc


class _Comm:
    def __init__(self, kind, arrays):
        self.arrays = list(arrays)
        self.n = len(self.arrays)
        self.kinds = [kind] * self.n if isinstance(kind, str) else list(kind)

    def out_shapes(self):
        return [jax.ShapeDtypeStruct((N_DEV,) + a.shape if k == "gather" else a.shape, a.dtype)
                for k, a in zip(self.kinds, self.arrays)]

    def scratch(self):
        return [pltpu.SemaphoreType.DMA((7 * self.n,)), pltpu.SemaphoreType.DMA((7 * self.n,)),
                pltpu.SemaphoreType.DMA((self.n,))]

    def run(self, srcs, dsts, sems, start):
        send_sems, recv_sems, local_sems = sems
        x, y, c = _my_place()
        mine = _flat(x, y, c)
        for a, (src, dst) in enumerate(zip(srcs, dsts)):
            whole = self.kinds[a] == "gather"
            local = pltpu.make_async_copy(src if whole else src.at[mine], dst.at[mine], local_sems.at[a])
            local.start() if start else local.wait()
            for k in range(1, N_DEV):
                px = 1 - x if k & 4 else x
                py = 1 - y if k & 2 else y
                pc = 1 - c if k & 1 else c
                cp = pltpu.make_async_remote_copy(
                    src_ref=src if whole else src.at[_flat(px, py, pc)], dst_ref=dst.at[mine],
                    send_sem=send_sems.at[7 * a + k - 1], recv_sem=recv_sems.at[7 * a + k - 1],
                    device_id=(px, py, pc), device_id_type=MESH_ID)
                cp.start() if start else cp.wait()


_HBM = pl.BlockSpec(memory_space=pltpu.HBM)


def _comm_call(comm, name):
    def body(*refs):
        srcs, dsts, sems = refs[:comm.n], refs[comm.n:2 * comm.n], refs[2 * comm.n:]
        comm.run(srcs, dsts, sems, True)
        comm.run(srcs, dsts, sems, False)

    return pl.pallas_call(body, name=name, in_specs=[_HBM] * comm.n, out_specs=[_HBM] * comm.n,
                          out_shape=comm.out_shapes(), scratch_shapes=comm.scratch())(*comm.arrays)


def _pallas(core, name, grid, in_specs, out_specs, out_shape, scratch, sem, args, comm=None):
    if comm is None:
        out = pl.pallas_call(core, name=name, grid=grid, in_specs=in_specs, out_specs=out_specs,
                             out_shape=out_shape, scratch_shapes=scratch,
                             compiler_params=_cparams(sem, VMEM_LIMIT))(*args)
        return out, []
    n_in, n_out, n_scr, n = len(in_specs), len(out_specs), len(scratch), comm.n

    def body(*refs):
        ins, srcs = refs[:n_in], refs[n_in:n_in + n]
        outs, dsts = refs[n_in + n:n_in + n + n_out], refs[n_in + n + n_out:n_in + 2 * n + n_out]
        scr, sems = refs[n_in + 2 * n + n_out:n_in + 2 * n + n_out + n_scr], refs[n_in + 2 * n + n_out + n_scr:]
        ids = [pl.program_id(k) for k in range(len(grid))]
        first = functools.reduce(lambda a, b: a & b, [i == 0 for i in ids])
        last = functools.reduce(lambda a, b: a & b, [i == g - 1 for i, g in zip(ids, grid)])

        @pl.when(first)
        def _():
            comm.run(srcs, dsts, sems, True)

        core(*ins, *outs, *scr)

        @pl.when(last)
        def _():
            comm.run(srcs, dsts, sems, False)

    out = pl.pallas_call(
        body, name=name, grid=grid, in_specs=list(in_specs) + [_HBM] * n, out_specs=list(out_specs) + [_HBM] * n,
        out_shape=list(out_shape) + comm.out_shapes(), scratch_shapes=list(scratch) + comm.scratch(),
        compiler_params=_cparams(("arbitrary",) * len(grid), VMEM_LIMIT))(*args, *comm.arrays)
    return out[:n_out], out[n_out:]


FFN_TM = 256


def _ffn_fwd(h, gain, wg, wu, wd, name, comm=None):
    T, D = h.shape
    F = wg.shape[0]
    tm = min(T, FFN_TM)
    once = pl.Buffered(1)

    def body(h_ref, gain_ref, wg_ref, wu_ref, wd_ref, ho_ref, xn_ref, g_ref, u_ref):
        hh = h_ref[...]
        xn = (hh * _rms(hh) * gain_ref[...]).astype(BF16)
        xn_ref[...] = xn
        g = _dot_nt(xn, wg_ref[...])
        u = _dot_nt(xn, wu_ref[...])
        g_ref[...] = g.astype(BF16)
        u_ref[...] = u.astype(BF16)
        a = (g * jax.nn.sigmoid(g) * u).astype(BF16)
        ho_ref[...] = hh + 0.5 * _dot(a, wd_ref[...])

    return _pallas(
        body, name, (T // tm,),
        [pl.BlockSpec((tm, D), lambda i: (i, 0)),
         pl.BlockSpec((1, D), lambda i: (0, 0)),
         pl.BlockSpec((F, D), lambda i: (0, 0), pipeline_mode=once),
         pl.BlockSpec((F, D), lambda i: (0, 0), pipeline_mode=once),
         pl.BlockSpec((F, D), lambda i: (0, 0), pipeline_mode=once)],
        [pl.BlockSpec((tm, D), lambda i: (i, 0)),
         pl.BlockSpec((tm, D), lambda i: (i, 0)),
         pl.BlockSpec((tm, F), lambda i: (i, 0)),
         pl.BlockSpec((tm, F), lambda i: (i, 0))],
        [jax.ShapeDtypeStruct((T, D), F32), jax.ShapeDtypeStruct((T, D), BF16),
         jax.ShapeDtypeStruct((T, F), BF16), jax.ShapeDtypeStruct((T, F), BF16)],
        [], ("parallel",), (h, gain, wg, wu, wd), comm)


def _ffn_bwd(dho, h, gain, g, u, wg, wu, wd, name, comm=None):
    T, D = h.shape
    F = wg.shape[0]
    tm = min(T, FFN_TM)
    tf = 1408 if F % 1408 == 0 else F
    once = pl.Buffered(1)

    def body(dho_ref, h_ref, gain_ref, g_ref, u_ref, wg_ref, wu_ref, wd_ref,
             dh_ref, dgain_ref, do_ref, a_ref, dg_ref, du_ref):
        @pl.when(pl.program_id(0) == 0)
        def _():
            dgain_ref[...] = jnp.zeros_like(dgain_ref)

        dho_v = dho_ref[...]
        do = (0.5 * dho_v).astype(BF16)
        do_ref[...] = do
        dxn = None
        for c in range(F // tf):
            cs = slice(c * tf, (c + 1) * tf)
            da = _dot_nt(do, wd_ref[cs, :])
            gg = g_ref[:, cs].astype(F32)
            uu = u_ref[:, cs].astype(F32)
            s = jax.nn.sigmoid(gg)
            sl = gg * s
            a_ref[:, cs] = (sl * uu).astype(BF16)
            dg = (da * uu * (s * (1.0 + gg * (1.0 - s)))).astype(BF16)
            du = (da * sl).astype(BF16)
            dg_ref[:, cs] = dg
            du_ref[:, cs] = du
            part = _dot(dg, wg_ref[cs, :]) + _dot(du, wu_ref[cs, :])
            dxn = part if dxn is None else dxn + part
        hh = h_ref[...]
        r = _rms(hh)
        dgain_ref[...] += _col_sum(dxn * hh * r)
        dh_ref[...] = dho_v + _rms_bwd(hh, r, dxn * gain_ref[...])

    tok = lambda w: pl.BlockSpec((tm, w), lambda i: (i, 0))
    row = pl.BlockSpec((1, D), lambda i: (0, 0))
    weight = pl.BlockSpec((F, D), lambda i: (0, 0), pipeline_mode=once)
    return _pallas(
        body, name, (T // tm,),
        [tok(D), tok(D), row, tok(F), tok(F), weight, weight, weight],
        [tok(D), row, tok(D), tok(F), tok(F), tok(F)],
        [jax.ShapeDtypeStruct((T, D), F32), jax.ShapeDtypeStruct((1, D), F32),
         jax.ShapeDtypeStruct((T, D), BF16), jax.ShapeDtypeStruct((T, F), BF16),
         jax.ShapeDtypeStruct((T, F), BF16), jax.ShapeDtypeStruct((T, F), BF16)],
        [], ("arbitrary",), (dho, h, gain, g, u, wg, wu, wd), comm)


def _xty(x, y, name, comm=None, col_shards=1):
    T, K = x.shape
    N = y.shape[1]
    tt = min(T, 2048)
    tk = K if K <= 1024 else (1408 if K % 1408 == 0 else K)
    tn = N if N <= 1024 else (1408 if N % 1408 == 0 else (1024 if N % 1024 == 0 else N))
    nt = T // tt
    ws = N // col_shards
    per = tn // ws if col_shards > 1 else 1
    assert col_shards == 1 or (tn % ws == 0 and ws % 128 == 0)

    def body(x_ref, y_ref, o_ref, acc_ref):
        t = pl.program_id(2)

        @pl.when(t == 0)
        def _():
            acc_ref[...] = jnp.zeros_like(acc_ref)

        acc_ref[...] += _dot_tn(x_ref[...], y_ref[...])

        @pl.when(t == nt - 1)
        def _():
            if col_shards == 1:
                o_ref[...] = acc_ref[...].astype(BF16)
            else:
                for s in range(per):
                    o_ref[s] = acc_ref[:, s * ws:(s + 1) * ws].astype(BF16)

    if col_shards == 1:
        out_spec = pl.BlockSpec((tk, tn), lambda k, n, t: (k, n))
        out_shape = jax.ShapeDtypeStruct((K, N), BF16)
    else:
        out_spec = pl.BlockSpec((per, tk, ws), lambda k, n, t: (n, k, 0))
        out_shape = jax.ShapeDtypeStruct((col_shards, K, ws), BF16)
    (out,), got = _pallas(
        body, name, (K // tk, N // tn, nt),
        [pl.BlockSpec((tt, tk), lambda k, n, t: (t, k)), pl.BlockSpec((tt, tn), lambda k, n, t: (t, n))],
        [out_spec], [out_shape], [pltpu.VMEM((tk, tn), F32)],
        ("parallel", "parallel", "arbitrary"), (x, y), comm)
    return out if comm is None else (out, got)


def _mixin_fwd(h, gain, w_in):
    T, D = h.shape
    nn, _, tn = w_in.shape
    N = nn * tn
    tm = min(T, 512)

    def body(h_ref, gain_ref, w_ref, z_ref, zb_ref, un_ref):
        hh = h_ref[...]
        un = (hh * _rms(hh) * gain_ref[...]).astype(BF16)
        un_ref[...] = un
        for s in range(nn):
            z = _dot(un, w_ref[s])
            z_ref[:, s * tn:(s + 1) * tn] = z
            zb_ref[:, s * tn:(s + 1) * tn] = z.astype(BF16)

    return pl.pallas_call(
        body, name="mixin_fwd", grid=(T // tm,),
        in_specs=[pl.BlockSpec((tm, D), lambda i: (i, 0)),
                  pl.BlockSpec((1, D), lambda i: (0, 0)),
                  pl.BlockSpec((nn, D, tn), lambda i: (0, 0, 0))],
        out_specs=[pl.BlockSpec((tm, N), lambda i: (i, 0)),
                   pl.BlockSpec((tm, N), lambda i: (i, 0)),
                   pl.BlockSpec((tm, D), lambda i: (i, 0))],
        out_shape=[jax.ShapeDtypeStruct((T, N), F32), jax.ShapeDtypeStruct((T, N), BF16),
                   jax.ShapeDtypeStruct((T, D), BF16)],
        compiler_params=_cparams(("parallel",), VMEM_LIMIT),
    )(h, gain, w_in)


def _mixin_bwd(dz, dh_res, h, gain, w_in):
    T, D = h.shape
    nn, _, tn = w_in.shape
    tm = min(T, 512)

    def body(dz_ref, dres_ref, h_ref, gain_ref, w_ref, dh_ref, dgain_ref):
        @pl.when(pl.program_id(0) == 0)
        def _():
            dgain_ref[...] = jnp.zeros_like(dgain_ref)

        dun = _dot_nt(dz_ref[:, 0:tn], w_ref[0])
        for s in range(1, nn):
            dun = dun + _dot_nt(dz_ref[:, s * tn:(s + 1) * tn], w_ref[s])
        hh = h_ref[...]
        r = _rms(hh)
        dgain_ref[...] += _col_sum(dun * hh * r)
        dh_ref[...] = dres_ref[...] + _rms_bwd(hh, r, dun * gain_ref[...])

    return pl.pallas_call(
        body, name="mixin_bwd", grid=(T // tm,),
        in_specs=[pl.BlockSpec((tm, nn * tn), lambda i: (i, 0)),
                  pl.BlockSpec((tm, D), lambda i: (i, 0)),
                  pl.BlockSpec((tm, D), lambda i: (i, 0)),
                  pl.BlockSpec((1, D), lambda i: (0, 0)),
                  pl.BlockSpec((nn, D, tn), lambda i: (0, 0, 0))],
        out_specs=[pl.BlockSpec((tm, D), lambda i: (i, 0)),
                   pl.BlockSpec((1, D), lambda i: (0, 0))],
        out_shape=[jax.ShapeDtypeStruct((T, D), F32), jax.ShapeDtypeStruct((1, D), F32)],
        compiler_params=_cparams(("arbitrary",), VMEM_LIMIT),
    )(dz, dh_res, h, gain, w_in)


def _loss_head(h, gain, target):
    T, D = h.shape
    tm = min(T, 1024)

    def body(h_ref, gain_ref, t_ref, loss_ref, dh_ref, dgain_ref):
        @pl.when(pl.program_id(0) == 0)
        def _():
            loss_ref[...] = jnp.zeros_like(loss_ref)
            dgain_ref[...] = jnp.zeros_like(dgain_ref)

        hh = h_ref[...]
        r = _rms(hh)
        e = hh * r * gain_ref[...] - t_ref[...]
        loss_ref[...] += (0.5 / D) * jnp.sum(e * e)
        dy = e * (1.0 / D)
        dgain_ref[...] += _col_sum(dy * hh * r)
        dh_ref[...] = _rms_bwd(hh, r, dy * gain_ref[...])

    return pl.pallas_call(
        body, name="loss_head", grid=(T // tm,),
        in_specs=[pl.BlockSpec((tm, D), lambda i: (i, 0)),
                  pl.BlockSpec((1, D), lambda i: (0, 0)),
                  pl.BlockSpec((tm, D), lambda i: (i, 0))],
        out_specs=[pl.BlockSpec((1, 128), lambda i: (0, 0)),
                   pl.BlockSpec((tm, D), lambda i: (i, 0)),
                   pl.BlockSpec((1, D), lambda i: (0, 0))],
        out_shape=[jax.ShapeDtypeStruct((1, 128), F32), jax.ShapeDtypeStruct((T, D), F32),
                   jax.ShapeDtypeStruct((1, D), F32)],
        compiler_params=_cparams(("arbitrary",), VMEM_LIMIT),
    )(h, gain, target)


def _adamw(parts, w, m, v, name):
    R, C = w.shape
    mult = 16 if parts.dtype == BF16 else 8
    tr = max(t for t in range(mult, min(R, 512) + 1, mult) if R % t == 0)
    c1 = 1.0 - ADAM_B1 ** ADAM_STEP
    c2 = 1.0 - ADAM_B2 ** ADAM_STEP

    def body(p_ref, w_ref, m_ref, v_ref, g_ref, d_ref, nm_ref, nv_ref):
        g = p_ref[0].astype(F32)
        for k in range(1, N_DEV):
            g = g + p_ref[k].astype(F32)
        mm = ADAM_B1 * m_ref[...] + (1.0 - ADAM_B1) * g
        vv = ADAM_B2 * v_ref[...] + (1.0 - ADAM_B2) * (g * g)
        g_ref[...] = g
        nm_ref[...] = mm
        nv_ref[...] = vv
        d_ref[...] = -ADAM_LR * ((mm / c1) / (jnp.sqrt(vv / c2) + ADAM_EPS) + ADAM_WD * w_ref[...])

    spec = pl.BlockSpec((tr, C), lambda i: (i, 0))
    return pl.pallas_call(
        body, name=name, grid=(R // tr,),
        in_specs=[pl.BlockSpec((N_DEV, tr, C), lambda i: (0, i, 0)), spec, spec, spec],
        out_specs=[spec, spec, spec, spec],
        out_shape=[jax.ShapeDtypeStruct((R, C), F32)] * 4,
        compiler_params=_cparams(("parallel",), VMEM_LIMIT),
    )(parts, w, m, v)


S5_NS = 256
S5_NH = 2
S5_NCB = 4
S5_RC = 4096
S5_NQ = 4
S5_GROUP = 2


def _disc_math(a_re, a_im, log_dt, bt_re, bt_im):
    dt = jnp.exp(log_dt)
    zr, zi = a_re * dt, a_im * dt
    mag = jnp.exp(zr)
    lb_re, lb_im = mag * jnp.cos(zi), mag * jnp.sin(zi)
    den = a_re * a_re + a_im * a_im
    nr, ni = lb_re - 1.0, lb_im
    f_re = (nr * a_re + ni * a_im) / den
    f_im = (ni * a_re - nr * a_im) / den
    bb_re = f_re[:, None, :] * bt_re - f_im[:, None, :] * bt_im
    bb_im = f_re[:, None, :] * bt_im + f_im[:, None, :] * bt_re
    return lb_re, lb_im, bb_re, bb_im


def _disc_fwd(a_re, a_im, log_dt, bt_re, bt_im, chain_len, name):
    G, P = a_re.shape
    C = bt_re.shape[1]
    n_sq = int(round(math.log2(chain_len)))
    assert 2 ** n_sq == chain_len

    def body(a_re_ref, a_im_ref, ldt_ref, br_ref, bi_ref, lr_ref, li_ref, sr_ref, si_ref, bbr_ref, bbi_ref):
        lr, li, bbr, bbi = _disc_math(a_re_ref[...], a_im_ref[...], ldt_ref[...], br_ref[...], bi_ref[...])
        lr_ref[...] = lr
        li_ref[...] = li
        bbr_ref[...] = bbr
        bbi_ref[...] = bbi
        pr, pi = lr, li
        for _ in range(n_sq):
            pr, pi = pr * pr - pi * pi, 2.0 * pr * pi
        sr_ref[...] = pr
        si_ref[...] = pi

    s2 = jax.ShapeDtypeStruct((G, P), F32)
    s3 = jax.ShapeDtypeStruct((G, C, P), F32)
    return pl.pallas_call(body, name=name, out_shape=[s2, s2, s2, s2, s3, s3])(a_re, a_im, log_dt, bt_re, bt_im)


def _disc_bwd(a_re, a_im, log_dt, bt_re, bt_im, d_lr, d_li, d_bbr, d_bbi, name):
    G, P = a_re.shape
    C = bt_re.shape[1]

    def body(a_re_ref, a_im_ref, ldt_ref, br_ref, bi_ref, c1, c2, c3, c4, o1, o2, o3, o4, o5):
        _, vjp = jax.vjp(_disc_math, a_re_ref[...], a_im_ref[...], ldt_ref[...], br_ref[...], bi_ref[...])
        o1[...], o2[...], o3[...], o4[...], o5[...] = vjp((c1[...], c2[...], c3[...], c4[...]))

    s2 = jax.ShapeDtypeStruct((G, P), F32)
    s3 = jax.ShapeDtypeStruct((G, C, P), F32)
    return pl.pallas_call(body, name=name, out_shape=[s2, s2, jax.ShapeDtypeStruct((G, 1), F32), s3, s3])(
        a_re, a_im, log_dt, bt_re, bt_im, d_lr, d_li, d_bbr, d_bbi)


def _row_block(ib):
    return pl.ds(pl.multiple_of(ib * SCAN_LANES, SCAN_LANES), SCAN_LANES)


def _chain_block(j, i, ascending, n_blocks):
    at = j * (n_blocks // S5_NQ) + i
    return _row_block(jnp.where(ascending, at, n_blocks - 1 - at))


def _unrolled_loop(n, unroll, body, carry):
    trips = n // unroll
    carry = lax.fori_loop(
        0, trips, lambda t, c: functools.reduce(lambda cc, u: body(t * unroll + u, cc), range(unroll), c), carry)
    for i in range(trips * unroll, n):
        carry = body(i, carry)
    return carry


def _cmul_add(lr, li, sr, si, xr, xi):
    return lr * sr - li * si + xr, lr * si + li * sr + xi


def _scan(xr_ref, xi_ref, lr, li, init, ascending, n_blocks, store):
    steps = n_blocks // S5_NQ
    if not store:
        def step(i, carry):
            blocks = [_chain_block(j, i, ascending, n_blocks) for j in range(S5_NQ)]
            return tuple(_cmul_add(lr, li, sr, si, xr_ref[rows, :], xi_ref[rows, :])
                         for (sr, si), rows in zip(carry, blocks))

        return _unrolled_loop(steps, 4, step, init)

    group = S5_GROUP
    assert steps % group == 0

    def trip(t, carry):
        blocks = [[_chain_block(j, t * group + u, ascending, n_blocks) for j in range(S5_NQ)] for u in range(group)]
        xs = [[(xr_ref[rows, :], xi_ref[rows, :]) for rows in row] for row in blocks]
        states = list(carry)
        done = []
        for u in range(group):
            states = [_cmul_add(lr, li, sr, si, xr, xi) for (sr, si), (xr, xi) in zip(states, xs[u])]
            done.append(states)
        for u in range(group):
            for rows, (nr, ni) in zip(blocks[u], done[u]):
                xr_ref[rows, :] = nr
                xi_ref[rows, :] = ni
        return tuple(states)

    return lax.fori_loop(0, steps // group, trip, init)


def _segment_starts(w, lsr, lsi, ascending):
    shape = w[0][0].shape
    row = lax.broadcasted_iota(jnp.int32, shape, 0)
    keep = row != jnp.where(ascending, 0, SCAN_LANES - 1)

    def shift(t):
        t = jnp.where(ascending, pltpu.roll(t, 1, 0), pltpu.roll(t, SCAN_LANES - 1, 0))
        return jnp.where(keep, t, 0.0)

    zero = jnp.zeros(shape, F32)
    c = [(zero, zero)] * S5_NQ
    for _ in range(SCAN_LANES):
        tr, ti = _cmul_add(lsr, lsi, *c[-1], *w[-1])
        c[0] = (shift(tr), shift(ti))
        for j in range(1, S5_NQ):
            c[j] = _cmul_add(lsr, lsi, *c[j - 1], *w[j - 1])
    return tuple(c)


def _first_pass(xr_ref, xi_ref, lam_ref, ascending, n_blocks, conj):
    shape = (SCAN_LANES, xr_ref.shape[1])
    sign = -1.0 if conj else 1.0
    lr = jnp.broadcast_to(lam_ref[0:1, :], shape)
    li = sign * jnp.broadcast_to(lam_ref[1:2, :], shape)
    lsr = jnp.broadcast_to(lam_ref[2:3, :], shape)
    lsi = sign * jnp.broadcast_to(lam_ref[3:4, :], shape)
    zero = jnp.zeros(shape, F32)
    w = _scan(xr_ref, xi_ref, lr, li, ((zero, zero),) * S5_NQ, ascending, n_blocks, store=False)
    return _segment_starts(w, lsr, lsi, ascending), lr, li


def _s5_specs(T):
    NS = S5_NS
    tok = pl.BlockSpec((T, 128), lambda c, d, h: (0, c))
    b_spec = pl.BlockSpec((None, None, None, 128, NS), lambda c, d, h: (d, c, h, 0, 0))
    c_spec = pl.BlockSpec((None, None, None, NS, 128), lambda c, d, h: (d, c, h, 0, 0))
    lam_spec = pl.BlockSpec((None, None, None, 4, NS), lambda c, d, h: (d, c, h, 0, 0))
    return tok, b_spec, c_spec, lam_spec


def _s5_fwd(zp, bre, bim, lam, cre, cimn, comm=None):
    T = zp.shape[0]
    NS = S5_NS
    nb = T // SCAN_LANES
    rc = min(S5_RC, T)
    tok, b_spec, c_spec, lam_spec = _s5_specs(T)

    def body(zp_ref, bre_ref, bim_ref, lam_ref, cre_ref, cim_ref, y_ref, xr_ref, xi_ref):
        d = pl.program_id(1)
        ascending = d == 0

        @pl.when((d == 0) & (pl.program_id(2) == 0))
        def _():
            y_ref[...] = jnp.zeros_like(y_ref)

        def proj(c, _):
            rows = pl.ds(pl.multiple_of(c * rc, rc), rc)
            zz = zp_ref[rows, :]
            xr_ref[rows, :] = _dot(zz, bre_ref[...])
            xi_ref[rows, :] = _dot(zz, bim_ref[...])
            return 0

        lax.fori_loop(0, T // rc, proj, 0)
        starts, lr, li = _first_pass(xr_ref, xi_ref, lam_ref, ascending, nb, conj=False)
        _scan(xr_ref, xi_ref, lr, li, starts, ascending, nb, store=True)

        def outp(c, _):
            rows = pl.ds(pl.multiple_of(c * rc, rc), rc)
            y_ref[rows, :] += (_dot(xr_ref[rows, :].astype(BF16), cre_ref[...])
                               + _dot(xi_ref[rows, :].astype(BF16), cim_ref[...]))
            return 0

        lax.fori_loop(0, T // rc, outp, 0)

    return _pallas(
        body, "s5_fwd", (S5_NCB, 2, S5_NH),
        [tok, b_spec, b_spec, lam_spec, c_spec, c_spec], [tok],
        [jax.ShapeDtypeStruct((T, SSM_WIDTH), F32)],
        [pltpu.VMEM((T, NS), F32), pltpu.VMEM((T, NS), F32)],
        ("parallel", "arbitrary", "arbitrary"), (zp, bre, bim, lam, cre, cimn), comm)


def _s5_bwd(zp, dyp, bre, bim, lam, cre, cimn, comm=None):
    T = zp.shape[0]
    NS, NH = S5_NS, S5_NH
    nb = T // SCAN_LANES
    rc = min(S5_RC, T)
    tok, b_spec, c_spec, lam_spec = _s5_specs(T)
    dlam_spec = pl.BlockSpec((None, None, None, 2, NS), lambda c, d, h: (d, c, h, 0, 0))

    def body(zp_ref, dyp_ref, bre_ref, bim_ref, lam_ref, cre_ref, cim_ref,
             dzp_ref, dbre_ref, dbim_ref, dlam_ref, dcre_ref, dcim_ref,
             sr_ref, si_ref, gr_ref, gi_ref):
        d = pl.program_id(1)
        ascending = d == 0
        g_ascending = d != 0

        @pl.when((d == 0) & (pl.program_id(2) == 0))
        def _():
            dzp_ref[...] = jnp.zeros_like(dzp_ref)

        dcre_ref[...] = jnp.zeros_like(dcre_ref)
        dcim_ref[...] = jnp.zeros_like(dcim_ref)
        dbre_ref[...] = jnp.zeros_like(dbre_ref)
        dbim_ref[...] = jnp.zeros_like(dbim_ref)

        def proj(c, _):
            rows = pl.ds(pl.multiple_of(c * rc, rc), rc)
            zz = zp_ref[rows, :]
            sr_ref[rows, :] = _dot(zz, bre_ref[...])
            si_ref[rows, :] = _dot(zz, bim_ref[...])
            dy = dyp_ref[rows, :]
            gr_ref[rows, :] = _dot_nt(dy, cre_ref[...])
            gi_ref[rows, :] = _dot_nt(dy, cim_ref[...])
            return 0

        lax.fori_loop(0, T // rc, proj, 0)
        s_starts, lr, li = _first_pass(sr_ref, si_ref, lam_ref, ascending, nb, conj=False)
        _scan(sr_ref, si_ref, lr, li, s_starts, ascending, nb, store=True)
        g_starts, lr, lic = _first_pass(gr_ref, gi_ref, lam_ref, g_ascending, nb, conj=True)

        steps = nb // S5_NQ
        group = S5_GROUP
        assert steps % group == 0

        def gtrip(t, carry, last):
            g, (ar, ai) = carry
            first = t * group
            blocks = [[_chain_block(j, first + u, g_ascending, nb) for j in range(S5_NQ)] for u in range(group)]
            direct = [[(gr_ref[rows, :], gi_ref[rows, :]) for rows in row] for row in blocks]
            done = []
            for u in range(group):
                new = []
                for j, ((g_r, g_i), (d_r, d_i)) in enumerate(zip(g, direct[u])):
                    n_r, n_i = _cmul_add(lr, lic, g_r, g_i, d_r, d_i)
                    if last and u == group - 1:
                        s_r, s_i = s_starts[S5_NQ - 1 - j]
                    else:
                        prev = _chain_block(j, first + u + 1, g_ascending, nb)
                        s_r, s_i = sr_ref[prev, :], si_ref[prev, :]
                    ar = ar + n_r * s_r + n_i * s_i
                    ai = ai + n_i * s_r - n_r * s_i
                    new.append((n_r, n_i))
                g = new
                done.append(new)
            for u in range(group):
                for rows, (n_r, n_i) in zip(blocks[u], done[u]):
                    gr_ref[rows, :] = n_r
                    gi_ref[rows, :] = n_i
            return tuple(g), (ar, ai)

        zero = jnp.zeros((SCAN_LANES, NS), F32)
        carry = lax.fori_loop(0, steps // group - 1, lambda t, c: gtrip(t, c, False), (g_starts, (zero, zero)))
        _, (ar, ai) = gtrip(steps // group - 1, carry, True)
        dlam_ref[0:1, :] = _col_sum(ar)
        dlam_ref[1:2, :] = _col_sum(ai)

        def grads(c, _):
            rows = pl.ds(pl.multiple_of(c * rc, rc), rc)
            zz = zp_ref[rows, :]
            dy = dyp_ref[rows, :]
            g_rb = gr_ref[rows, :].astype(BF16)
            g_ib = gi_ref[rows, :].astype(BF16)
            dcre_ref[...] += _dot_tn(sr_ref[rows, :].astype(BF16), dy)
            dcim_ref[...] += _dot_tn(si_ref[rows, :].astype(BF16), dy)
            dbre_ref[...] += _dot_tn(zz, g_rb)
            dbim_ref[...] += _dot_tn(zz, g_ib)
            dzp_ref[rows, :] += _dot_nt(g_rb, bre_ref[...]) + _dot_nt(g_ib, bim_ref[...])
            return 0

        lax.fori_loop(0, T // rc, grads, 0)

    f32 = lambda *s: jax.ShapeDtypeStruct(s, F32)
    return _pallas(
        body, "s5_bwd", (S5_NCB, 2, S5_NH),
        [tok, tok, b_spec, b_spec, lam_spec, c_spec, c_spec],
        [tok, b_spec, b_spec, dlam_spec, c_spec, c_spec],
        [f32(T, SSM_WIDTH), f32(2, S5_NCB, NH, 128, NS), f32(2, S5_NCB, NH, 128, NS),
         f32(2, S5_NCB, NH, 2, NS), f32(2, S5_NCB, NH, NS, 128), f32(2, S5_NCB, NH, NS, 128)],
        [pltpu.VMEM((T, NS), F32)] * 4,
        ("parallel", "arbitrary", "arbitrary"), (zp, dyp, bre, bim, lam, cre, cimn), comm)


def _s5_delta():
    d = np.zeros((S5_NH, 8, 8 // S5_NH), np.float32)
    for h in range(S5_NH):
        for go in range(8 // S5_NH):
            d[h, h * (8 // S5_NH) + go, go] = 1.0
    return d


def _s5_pack_b(bbt):
    gh = 8 // S5_NH
    b5 = bbt.reshape(S5_NCB, S5_NH, gh, SSM_GROUP, SSM_STATE).transpose(0, 1, 3, 2, 4)
    m = b5[:, :, None] * _s5_delta()[None, :, :, None, :, None]
    return m.reshape(S5_NCB, S5_NH, 128, S5_NS)


def _s5_unpack_b(dm):
    gh = 8 // S5_NH
    d6 = dm.reshape(S5_NCB, S5_NH, 8, SSM_GROUP, gh, SSM_STATE)
    b5 = jnp.sum(d6 * _s5_delta()[None, :, :, None, :, None], axis=2)
    return b5.transpose(0, 1, 3, 2, 4).reshape(SSM_GROUPS, SSM_GROUP, SSM_STATE)


def _s5_pack_c(c):
    gh = 8 // S5_NH
    c5 = c.reshape(S5_NCB, S5_NH, gh, SSM_GROUP, SSM_STATE).transpose(0, 1, 2, 4, 3)
    m = c5[:, :, :, :, None, :] * _s5_delta().transpose(0, 2, 1)[None, :, :, None, :, None]
    return m.reshape(S5_NCB, S5_NH, S5_NS, 128)


def _s5_unpack_c(dm):
    gh = 8 // S5_NH
    d6 = dm.reshape(S5_NCB, S5_NH, gh, SSM_STATE, 8, SSM_GROUP)
    c5 = jnp.sum(d6 * _s5_delta().transpose(0, 2, 1)[None, :, :, None, :, None], axis=4)
    return c5.transpose(0, 1, 2, 4, 3).reshape(SSM_GROUPS, SSM_GROUP, SSM_STATE)


def _s5_pack_lam(x):
    return x.reshape(S5_NCB, S5_NH, S5_NS)


def _permute_rows(x):
    T = x.shape[0]
    return x.reshape(SCAN_LANES, T // SCAN_LANES, -1).transpose(1, 0, 2).reshape(T, -1)


def _unpermute_rows(x):
    T = x.shape[0]
    return x.reshape(T // SCAN_LANES, SCAN_LANES, -1).transpose(1, 0, 2).reshape(T, -1)


ATT_TB = ATT_ROWS * GRID_W
ATT_KB = 3 * ATT_TB


def _att_valid(i, n_rows):
    qi, kj = np.meshgrid(np.arange(ATT_TB), np.arange(ATT_KB), indexing="ij")
    r = i * ATT_ROWS + qi // GRID_W
    c = qi % GRID_W
    rk = (i - 1) * ATT_ROWS + kj // GRID_W
    x = kj % GRID_W
    rs = np.clip(r - WIN_H // 2, 0, n_rows - WIN_H)
    cs = np.clip(c - WIN_W // 2, 0, GRID_W - WIN_W)
    return (rk >= rs) & (rk < rs + WIN_H) & (x >= cs) & (x < cs + WIN_W)


def _att_masked_tables(table, n_rows):
    n = n_rows // ATT_ROWS
    assert n >= 3
    masks = np.stack([_att_valid(i, n_rows) for i in (0, 1, n - 1)])
    return jnp.where(masks[:, None], table[None], NEG_INF)


def _att_variant(i, n):
    return jnp.where(i == 0, 0, jnp.where(i >= n - 1, 2, 1))


def _att_exp(qh, kh, bias):
    s = _dot_nt(qh, kh) + bias
    return jnp.exp(s - jnp.max(s, axis=1, keepdims=True))


def _att_values_and_ones(vh):
    return jnp.concatenate([vh, jnp.ones_like(vh)], axis=1)


def _att_specs(n, col):
    last = n - 1
    cur = lambda i: (jnp.minimum(i, last), col)
    prv = lambda i: (jnp.maximum(jnp.minimum(i, last) - 1, 0), col)
    nxt = lambda i: (jnp.minimum(i + 1, last), col)
    blk = lambda f: pl.BlockSpec((ATT_TB, ATT_WIDTH), f)
    return blk(cur), blk(prv), blk(nxt)


def _att_fwd(zb, biasv):
    T = zb.shape[0]
    W = ATT_WIDTH
    n = T // ATT_TB
    n_rows = T // GRID_W
    cur = _att_specs(n, 0)[0]
    q_cur = _att_specs(n, 1)[0]
    k_cur, k_prv, k_nxt = _att_specs(n, 2)
    v_cur, v_prv, v_nxt = _att_specs(n, 3)

    def body(q_ref, kp_ref, kc_ref, kn_ref, vp_ref, vc_ref, vn_ref, b_ref, y_ref):
        qs = q_ref[...] * 0.125
        kb = jnp.concatenate([kp_ref[...], kc_ref[...], kn_ref[...]], axis=0)
        vb = jnp.concatenate([vp_ref[...], vc_ref[...], vn_ref[...]], axis=0)
        outs = []
        for h in range(ATT_HEADS):
            hs = slice(h * ATT_HEAD_DIM, (h + 1) * ATT_HEAD_DIM)
            e = _att_exp(qs[:, hs], kb[:, hs], b_ref[h]).astype(BF16)
            ov = _dot(e, _att_values_and_ones(vb[:, hs]))
            outs.append(ov[:, :ATT_HEAD_DIM] * (1.0 / ov[:, ATT_HEAD_DIM:ATT_HEAD_DIM + 1]))
        y_ref[...] = jnp.concatenate(outs, axis=1).astype(BF16)

    return pl.pallas_call(
        body, name="att_fwd", grid=(n,),
        in_specs=[q_cur, k_prv, k_cur, k_nxt, v_prv, v_cur, v_nxt,
                  pl.BlockSpec((None, ATT_HEADS, ATT_TB, ATT_KB), lambda i: (_att_variant(i, n), 0, 0, 0))],
        out_specs=cur,
        out_shape=jax.ShapeDtypeStruct((T, W), BF16),
        compiler_params=_cparams(("parallel",), VMEM_LIMIT),
    )(zb, zb, zb, zb, zb, zb, zb, biasv)


def _att_bwd(zb, y, do, biasv, comm=None):
    T = zb.shape[0]
    W = ATT_WIDTH
    n = T // ATT_TB
    n_rows = T // GRID_W
    cur = _att_specs(n, 0)[0]
    q_cur = _att_specs(n, 1)[0]
    k_cur, k_prv, k_nxt = _att_specs(n, 2)
    v_cur, v_prv, v_nxt = _att_specs(n, 3)
    done = pl.BlockSpec((ATT_TB, W), lambda i: (jnp.maximum(i - 1, 0), 0))
    bias_spec = pl.BlockSpec((None, ATT_HEADS, ATT_TB, ATT_KB), lambda i: (_att_variant(i, n), 0, 0, 0))

    def body(q_ref, y_ref, do_ref, kp_ref, kc_ref, kn_ref, vp_ref, vc_ref, vn_ref, b_ref,
             dq_ref, dk_ref, dv_ref, db_ref, acck_ref, accv_ref):
        i = pl.program_id(0)

        @pl.when(i == 0)
        def _():
            db_ref[...] = jnp.zeros_like(db_ref)
            acck_ref[...] = jnp.zeros_like(acck_ref)
            accv_ref[...] = jnp.zeros_like(accv_ref)

        @pl.when((i > 0) & (i < n))
        def _():
            slot = lax.rem(i + 1, 3)
            acck_ref[slot] = jnp.zeros((ATT_TB, W), F32)
            accv_ref[slot] = jnp.zeros((ATT_TB, W), F32)

        @pl.when(i < n)
        def _():
            qs = q_ref[...] * 0.125
            dob = do_ref[...]
            dy = dob.astype(F32) * y_ref[...].astype(F32)
            kb = jnp.concatenate([kp_ref[...], kc_ref[...], kn_ref[...]], axis=0)
            vb = jnp.concatenate([vp_ref[...], vc_ref[...], vn_ref[...]], axis=0)
            dqs, dks, dvs = [], [], []
            for h in range(ATT_HEADS):
                hs = slice(h * ATT_HEAD_DIM, (h + 1) * ATT_HEAD_DIM)
                qh, kh, vh, doh = qs[:, hs], kb[:, hs], vb[:, hs], dob[:, hs]
                e = _att_exp(qh, kh, b_ref[h])
                p = e * (1.0 / jnp.sum(e, axis=1, keepdims=True))
                dp = _dot_nt(doh, vh)
                ds = p * (dp - jnp.sum(dy[:, hs], axis=1, keepdims=True))
                db_ref[h] += ds
                dsb = ds.astype(BF16)
                dqs.append(_dot(dsb, kh) * 0.125)
                dks.append(_dot_tn(dsb, qh))
                dvs.append(_dot_tn(p.astype(BF16), doh))
            dq_ref[...] = jnp.concatenate(dqs, axis=1).astype(BF16)
            dk_all = jnp.concatenate(dks, axis=1)
            dv_all = jnp.concatenate(dvs, axis=1)
            for b in range(3):
                slot = lax.rem(i + 2 + b, 3)
                rows = slice(b * ATT_TB, (b + 1) * ATT_TB)
                acck_ref[slot] += dk_all[rows]
                accv_ref[slot] += dv_all[rows]

        slot = lax.rem(i + 2, 3)
        dk_ref[...] = acck_ref[slot].astype(BF16)
        dv_ref[...] = accv_ref[slot].astype(BF16)

    return _pallas(
        body, "att_bwd", (n + 1,),
        [q_cur, cur, cur, k_prv, k_cur, k_nxt, v_prv, v_cur, v_nxt, bias_spec],
        [cur, done, done, pl.BlockSpec((ATT_HEADS, ATT_TB, ATT_KB), lambda i: (0, 0, 0))],
        [jax.ShapeDtypeStruct((T, W), BF16)] * 3 + [jax.ShapeDtypeStruct((ATT_HEADS, ATT_TB, ATT_KB), F32)],
        [pltpu.VMEM((3, ATT_TB, W), F32), pltpu.VMEM((3, ATT_TB, W), F32)],
        ("arbitrary",), (zb, y, do, zb, zb, zb, zb, zb, zb, biasv), comm)


def _att_selectors():
    rsel = np.zeros((ATT_ROWS, 3 * ATT_ROWS, 2 * WIN_H - 1), np.float32)
    for a in range(ATT_ROWS):
        for b in range(3 * ATT_ROWS):
            rsel[a, b, b - a - ATT_ROWS + WIN_H - 1] = 1.0
    csel = np.zeros((GRID_W, GRID_W, 2 * WIN_W - 1), np.float32)
    for c in range(GRID_W):
        for x in range(GRID_W):
            csel[c, x, min(max(x - c, -(WIN_W - 1)), WIN_W - 1) + WIN_W - 1] = 1.0
    return rsel, csel


def _att_bias_table(rpb):
    rsel, csel = _att_selectors()
    hi = lax.Precision.HIGHEST
    t = jnp.einsum('hrd,abr->habd', rpb, rsel, precision=hi)
    t = jnp.einsum('habd,cxd->hacbx', t, csel, precision=hi)
    return t.reshape(ATT_HEADS, ATT_TB, ATT_KB)


def _att_bias_table_t(dtable):
    rsel, csel = _att_selectors()
    hi = lax.Precision.HIGHEST
    t = dtable.reshape(ATT_HEADS, ATT_ROWS, GRID_W, 3 * ATT_ROWS, GRID_W)
    t = jnp.einsum('hacbx,cxd->habd', t, csel, precision=hi)
    return jnp.einsum('habd,abr->hrd', t, rsel, precision=hi)


GELU_K = math.sqrt(2.0 / math.pi)
GELU_C = 0.044715
MERGE_TM = 256


def _gelu(x):
    return 0.5 * x * (1.0 + jnp.tanh(GELU_K * (x + GELU_C * x * x * x)))


def _gelu_grad(x):
    t = jnp.tanh(GELU_K * (x + GELU_C * x * x * x))
    return 0.5 * (1.0 + t) + 0.5 * x * (1.0 - t * t) * GELU_K * (1.0 + 3.0 * GELU_C * x * x)


def _merge_forward(ypre, zs, gs, ga, ya, ssm_d, w_glu, b_glu, w_bs, w_ba):
    ys = ypre + ssm_d * zs
    yg = _gelu(ys)
    sg = jax.nn.sigmoid(_dot(yg.astype(BF16), w_glu) + b_glu)
    y2 = yg * sg
    bs = _dot(y2.astype(BF16), w_bs)
    ba = _dot(ya, w_ba)
    s1 = jax.nn.sigmoid(gs)
    s2 = jax.nn.sigmoid(ga)
    merged = s1 * bs + s2 * ba
    return ys, yg, sg, y2, bs, ba, s1, s2, merged


def _merge_in_specs(D, W, tm):
    tok = lambda w, c: pl.BlockSpec((tm, w), lambda i: (i, c))
    full = lambda r, c: pl.BlockSpec((r, c), lambda i: (0, 0))
    z_specs = [tok(W, 0), tok(D, 4 * W // D), tok(D, 4 * W // D + 1)]
    w_specs = [full(1, W), full(W, W), full(1, W), full(W, D), full(W, D), full(D, D)]
    return tok, z_specs, w_specs


def _merge_fwd(ypre, z, ya, h1, ssm_d, w_glu, b_glu, w_bs, w_ba, w_out):
    T, D = h1.shape
    W = ypre.shape[1]
    tm = min(T, 2 * MERGE_TM)
    tok, z_specs, w_specs = _merge_in_specs(D, W, tm)

    def body(ypre_ref, zs_ref, gs_ref, ga_ref, ya_ref, h1_ref, d_ref, wglu_ref, bglu_ref, wbs_ref, wba_ref, wout_ref,
             h2_ref):
        merged = _merge_forward(ypre_ref[...], zs_ref[...], gs_ref[...], ga_ref[...], ya_ref[...], d_ref[...],
                                wglu_ref[...], bglu_ref[...], wbs_ref[...], wba_ref[...])[-1]
        h2_ref[...] = h1_ref[...] + _dot(merged.astype(BF16), wout_ref[...])

    return pl.pallas_call(
        body, name="merge_fwd", grid=(T // tm,),
        in_specs=[tok(W, 0)] + z_specs + [tok(W, 0), tok(D, 0)] + w_specs,
        out_specs=tok(D, 0),
        out_shape=jax.ShapeDtypeStruct((T, D), F32),
        compiler_params=_cparams(("parallel",), VMEM_LIMIT),
    )(ypre, z, z, z, ya, h1, ssm_d, w_glu, b_glu, w_bs, w_ba, w_out)


def _merge_bwd(dh2, ypre, z, ya, ssm_d, w_glu, b_glu, w_bs, w_ba, w_out):
    T, D = dh2.shape
    W = ypre.shape[1]
    tm = min(T, MERGE_TM)
    tok, z_specs, w_specs = _merge_in_specs(D, W, tm)

    def body(dh2_ref, ypre_ref, zs_ref, gs_ref, ga_ref, ya_ref, d_ref, wglu_ref, bglu_ref, wbs_ref, wba_ref, wout_ref,
             dypre_ref, dzs_ref, dgs_ref, dga_ref, dya_ref, dd_ref, dwglu_ref, dbglu_ref, dwbs_ref, dwba_ref, dwout_ref):
        @pl.when(pl.program_id(0) == 0)
        def _():
            for r in (dd_ref, dwglu_ref, dbglu_ref, dwbs_ref, dwba_ref, dwout_ref):
                r[...] = jnp.zeros_like(r)

        zs = zs_ref[...]
        ya = ya_ref[...]
        ys, yg, sg, y2, bs, ba, s1, s2, merged = _merge_forward(
            ypre_ref[...], zs, gs_ref[...], ga_ref[...], ya, d_ref[...],
            wglu_ref[...], bglu_ref[...], wbs_ref[...], wba_ref[...])
        dh2b = dh2_ref[...].astype(BF16)
        dmerged = _dot_nt(dh2b, wout_ref[...])
        dwout_ref[...] += _dot_tn(merged.astype(BF16), dh2b)
        dbs = (dmerged * s1).astype(BF16)
        dba = (dmerged * s2).astype(BF16)
        dgs_ref[...] = (dmerged * bs * s1 * (1.0 - s1)).astype(BF16)
        dga_ref[...] = (dmerged * ba * s2 * (1.0 - s2)).astype(BF16)
        dwbs_ref[...] += _dot_tn(y2.astype(BF16), dbs)
        dwba_ref[...] += _dot_tn(ya, dba)
        dya_ref[...] = _dot_nt(dba, wba_ref[...]).astype(BF16)
        dy2 = _dot_nt(dbs, wbs_ref[...])
        dvv = dy2 * yg * sg * (1.0 - sg)
        dvvb = dvv.astype(BF16)
        dyg = dy2 * sg + _dot_nt(dvvb, wglu_ref[...])
        dwglu_ref[...] += _dot_tn(yg.astype(BF16), dvvb)
        dbglu_ref[...] += _col_sum(dvv)
        dys = dyg * _gelu_grad(ys)
        dd_ref[...] += _col_sum(dys * zs)
        dzs_ref[...] = dys * d_ref[...]
        dypre_ref[...] = dys.astype(BF16)

    f32 = lambda *s: jax.ShapeDtypeStruct(s, F32)
    b16 = lambda *s: jax.ShapeDtypeStruct(s, BF16)
    return pl.pallas_call(
        body, name="merge_bwd", grid=(T // tm,),
        in_specs=[tok(D, 0), tok(W, 0)] + z_specs + [tok(W, 0)] + w_specs,
        out_specs=[tok(W, 0), tok(W, 0), tok(D, 0), tok(D, 0), tok(W, 0)] + w_specs,
        out_shape=[b16(T, W), f32(T, W), b16(T, D), b16(T, D), b16(T, W),
                   f32(1, W), f32(W, W), f32(1, W), f32(W, D), f32(W, D), f32(D, D)],
        compiler_params=_cparams(("arbitrary",), VMEM_LIMIT),
    )(dh2, ypre, z, z, z, ya, ssm_d, w_glu, b_glu, w_bs, w_ba, w_out)


def _cast_shards(weights):
    def body(*refs):
        n = len(refs) // 2
        for src, dst in zip(refs[:n], refs[n:]):
            dst[...] = src[0].astype(BF16)

    return pl.pallas_call(
        body, name="cast_shards",
        out_shape=[jax.ShapeDtypeStruct(w.shape[1:], BF16) for w in weights],
        compiler_params=_cparams(None, VMEM_LIMIT))(*weights)


def _gather_two_level(shards, name):
    n = len(shards)

    def body(*refs):
        x_refs, out_refs = refs[:n], refs[n:2 * n]
        send_sems, recv_sems, local_sems = refs[2 * n:]
        x, y, c = _my_place()
        me, sibling = (x, y, c), (x, y, 1 - c)
        chips = [(1 - x, y), (x, 1 - y), (1 - x, 1 - y)]

        def copy(a, k, block, to, own=False):
            slot = out_refs[a].at[_flat(*block)]
            return pltpu.make_async_remote_copy(
                src_ref=x_refs[a] if own else slot, dst_ref=slot,
                send_sem=send_sems.at[7 * a + k], recv_sem=recv_sems.at[7 * a + k],
                device_id=to, device_id_type=MESH_ID)

        sent, local = [], []
        for a in range(n):
            local.append(pltpu.make_async_copy(x_refs[a], out_refs[a].at[_flat(*me)], local_sems.at[a]))
            local[-1].start()
            sent.append(copy(a, 0, me, sibling, own=True))
            sent += [copy(a, 1 + j, me, (*chip, c), own=True) for j, chip in enumerate(chips)]
        for cp in sent:
            cp.start()
        for a in range(n):
            for j, chip in enumerate(chips):
                copy(a, 1 + j, (*chip, c), me).wait_recv()
                sent.append(copy(a, 4 + j, (*chip, c), sibling))
                sent[-1].start()
        for a in range(n):
            copy(a, 0, sibling, me).wait_recv()
            for j, chip in enumerate(chips):
                copy(a, 4 + j, (*chip, 1 - c), me).wait_recv()
        for cp in sent:
            cp.wait_send()
        for cp in local:
            cp.wait()

    return pl.pallas_call(
        body, name=name, in_specs=[_HBM] * n, out_specs=[_HBM] * n,
        out_shape=[jax.ShapeDtypeStruct((N_DEV,) + s.shape, s.dtype) for s in shards],
        scratch_shapes=[pltpu.SemaphoreType.DMA((7 * n,)), pltpu.SemaphoreType.DMA((7 * n,)),
                        pltpu.SemaphoreType.DMA((n,))],
    )(*shards)


PACK_COLS = 1024
BIG = (("ffn1_w_gate", 1), ("ffn1_w_up", 1), ("ffn1_w_down", 0), ("w_in", 1), ("ssm_w_glu", 0),
       ("w_branch_ssm", 1), ("w_branch_att", 1), ("w_out", 0),
       ("ffn2_w_gate", 1), ("ffn2_w_up", 1), ("ffn2_w_down", 0))
BIG_AXIS = dict(BIG)
TRANSPOSED = ("ffn1_w_gate", "ffn1_w_up", "ffn2_w_gate", "ffn2_w_up")
SSM_DIR = ("ssm_a_re", "ssm_a_im", "ssm_log_dt", "ssm_b_re", "ssm_b_im", "ssm_c_re", "ssm_c_im")
SMALL_EARLY = (("mix_norm",) + tuple(n + "_fwd" for n in SSM_DIR) + tuple(n + "_bwd" for n in SSM_DIR)
               + ("ssm_d", "ssm_b_glu", "att_rpb", "ffn2_norm", "final_norm"))
SMALL_LATE = ("ffn1_norm",)
WEIGHTS = ("ffn1_norm", "ffn1_w_gate", "ffn1_w_up", "ffn1_w_down", "mix_norm", "w_in") \
    + tuple(n + "_fwd" for n in SSM_DIR) + tuple(n + "_bwd" for n in SSM_DIR) \
    + ("ssm_d", "ssm_w_glu", "ssm_b_glu", "att_rpb", "w_branch_ssm", "w_branch_att", "w_out",
       "ffn2_norm", "ffn2_w_gate", "ffn2_w_up", "ffn2_w_down", "final_norm")


def _pad_rows(a, mult):
    pad = (-a.shape[-2]) % mult
    if pad:
        a = jnp.concatenate([a, jnp.zeros(a.shape[:-2] + (pad, a.shape[-1]), a.dtype)], axis=-2)
    return a


def _pack(arrays, row_mult):
    flat = jnp.concatenate([a.reshape(-1) for a in arrays])
    pad = (-flat.shape[0]) % PACK_COLS
    if pad:
        flat = jnp.concatenate([flat, jnp.zeros((pad,), flat.dtype)])
    return _pad_rows(flat.reshape(-1, PACK_COLS), row_mult)


def _unpack(slab, shapes):
    flat = slab.reshape(-1)
    out, at = [], 0
    for s in shapes:
        n = int(np.prod(s))
        out.append(flat[at:at + n].reshape(s))
        at += n
    return out


def _split_for_devices(g, axis):
    r, c = g.shape
    if axis == 1:
        return g.reshape(r, N_DEV, c // N_DEV).transpose(1, 0, 2).astype(BF16)
    return g.reshape(N_DEV, r // N_DEV, c).astype(BF16)


def _join_shards(gathered, axis):
    _, r, c = gathered.shape
    if axis == 1:
        return gathered.transpose(1, 0, 2).reshape(r, N_DEV * c)
    return gathered.reshape(N_DEV * r, c)


def _s5_direction_inputs(p, sfx, chain_len):
    bt_re = p["ssm_b_re" + sfx][0].transpose(0, 2, 1)
    bt_im = p["ssm_b_im" + sfx][0].transpose(0, 2, 1)
    raw = (p["ssm_a_re" + sfx][0], p["ssm_a_im" + sfx][0], p["ssm_log_dt" + sfx][0][:, None], bt_re, bt_im)
    lr, li, sr, si, bbr, bbi = _disc_fwd(*raw, chain_len,"s5_disc" + sfx)
    lam = jnp.stack([_s5_pack_lam(t) for t in (lr, li, sr, si)], axis=2)
    mats = (_s5_pack_b(bbr), _s5_pack_b(bbi), lam,
            _s5_pack_c(p["ssm_c_re" + sfx][0]), _s5_pack_c(-p["ssm_c_im" + sfx][0]))
    return raw, mats


def kernel(x, ffn1_norm, ffn1_w_gate, ffn1_w_up, ffn1_w_down, mix_norm, w_in, ssm_a_re_fwd, ssm_a_im_fwd, ssm_log_dt_fwd, ssm_b_re_fwd, ssm_b_im_fwd, ssm_c_re_fwd, ssm_c_im_fwd, ssm_a_re_bwd, ssm_a_im_bwd, ssm_log_dt_bwd, ssm_b_re_bwd, ssm_b_im_bwd, ssm_c_re_bwd, ssm_c_im_bwd, ssm_d, ssm_w_glu, ssm_b_glu, att_rpb, w_branch_ssm, w_branch_att, w_out, ffn2_norm, ffn2_w_gate, ffn2_w_up, ffn2_w_down, final_norm, loss_target, m_ffn1_norm, m_ffn1_w_gate, m_ffn1_w_up, m_ffn1_w_down, m_mix_norm, m_w_in, m_ssm_a_re_fwd, m_ssm_a_im_fwd, m_ssm_log_dt_fwd, m_ssm_b_re_fwd, m_ssm_b_im_fwd, m_ssm_c_re_fwd, m_ssm_c_im_fwd, m_ssm_a_re_bwd, m_ssm_a_im_bwd, m_ssm_log_dt_bwd, m_ssm_b_re_bwd, m_ssm_b_im_bwd, m_ssm_c_re_bwd, m_ssm_c_im_bwd, m_ssm_d, m_ssm_w_glu, m_ssm_b_glu, m_att_rpb, m_w_branch_ssm, m_w_branch_att, m_w_out, m_ffn2_norm, m_ffn2_w_gate, m_ffn2_w_up, m_ffn2_w_down, m_final_norm, v_ffn1_norm, v_ffn1_w_gate, v_ffn1_w_up, v_ffn1_w_down, v_mix_norm, v_w_in, v_ssm_a_re_fwd, v_ssm_a_im_fwd, v_ssm_log_dt_fwd, v_ssm_b_re_fwd, v_ssm_b_im_fwd, v_ssm_c_re_fwd, v_ssm_c_im_fwd, v_ssm_a_re_bwd, v_ssm_a_im_bwd, v_ssm_log_dt_bwd, v_ssm_b_re_bwd, v_ssm_b_im_bwd, v_ssm_c_re_bwd, v_ssm_c_im_bwd, v_ssm_d, v_ssm_w_glu, v_ssm_b_glu, v_att_rpb, v_w_branch_ssm, v_w_branch_att, v_w_out, v_ffn2_norm, v_ffn2_w_gate, v_ffn2_w_up, v_ffn2_w_down, v_final_norm):
    p = dict(locals())
    x = p["x"][0]
    target = p["loss_target"][0]
    T, D = x.shape

    stored = lambda a, n: jnp.swapaxes(a, -1, -2) if n in TRANSPOSED else a
    cut_axis = lambda n: 0 if n in TRANSPOSED else BIG_AXIS[n]
    shard = dict(zip([n for n, _ in BIG], _cast_shards([stored(p[n], n) for n, _ in BIG])))
    ffn1_w = ("ffn1_w_gate", "ffn1_w_up", "ffn1_w_down")
    mix_w = ("w_in", "ssm_w_glu", "w_branch_ssm", "w_branch_att", "w_out")
    ffn2_w = ("ffn2_w_gate", "ffn2_w_up", "ffn2_w_down")
    gathered = dict(zip(ffn1_w, _gather_two_level([shard[n] for n in ffn1_w], "gather_ffn1")))
    full = lambda n: _join_shards(gathered[n], cut_axis(n))

    h0 = x
    wg1, wu1, wd1 = [full(n) for n in ffn1_w]
    (h1, xn1, g1, u1), got = _ffn_fwd(h0, p["ffn1_norm"], wg1, wu1, wd1, "ffn1_fwd",
                                      _Comm("gather", [shard[n] for n in mix_w]))
    gathered.update(zip(mix_w, got))
    z, zb, un = _mixin_fwd(h1, p["mix_norm"], gathered["w_in"])
    W = SSM_WIDTH
    zp = _permute_rows(zb[:, :W])
    chain_len = T // SCAN_LANES // S5_NQ
    raw_f, mats_f = _s5_direction_inputs(p, "_fwd", chain_len)
    raw_b, mats_b = _s5_direction_inputs(p, "_bwd", chain_len)
    bre, bim, lam, cre, cimn = [jnp.stack([f, b]) for f, b in zip(mats_f, mats_b)]
    bre, bim, cre, cimn = [t.astype(BF16) for t in (bre, bim, cre, cimn)]
    (yp,), got = _s5_fwd(zp, bre, bim, lam, cre, cimn, _Comm("gather", [shard[n] for n in ffn2_w]))
    gathered.update(zip(ffn2_w, got))
    ypre = _unpermute_rows(yp)
    table = _att_masked_tables(_att_bias_table(p["att_rpb"][0]), T // GRID_W)
    ya = _att_fwd(zb, table)
    tail_w = (p["ssm_d"], full("ssm_w_glu"), p["ssm_b_glu"], full("w_branch_ssm"), full("w_branch_att"), full("w_out"))
    h2 = _merge_fwd(ypre, z, ya, h1, *tail_w)
    wg2, wu2, wd2 = [full(n) for n in ffn2_w]
    (h3, xn2, g2, u2), _ = _ffn_fwd(h2, p["ffn2_norm"], wg2, wu2, wd2, "ffn2_fwd")
    loss_part, dh3, d_final = _loss_head(h3, p["final_norm"][None], target)

    grads = {"final_norm": d_final[0]}
    to_send = lambda names: _Comm("exchange", [_split_for_devices(grads[n], cut_axis(n)) for n in names])
    parts = {}
    (dh2, grads["ffn2_norm"], do2, a2, dg2, du2), _ = _ffn_bwd(
        dh3, h2, p["ffn2_norm"], g2, u2, wg2, wu2, wd2, "ffn2_bwd")
    grads["ffn2_w_gate"] = _xty(dg2, xn2, "ffn2_dw_gate")
    grads["ffn2_w_up"] = _xty(du2, xn2, "ffn2_dw_up")
    grads["ffn2_w_down"] = _xty(a2, do2, "ffn2_dw_down")
    (dypre, dzs_skip, dgs, dga, dya, grads["ssm_d"], grads["ssm_w_glu"], grads["ssm_b_glu"],
     grads["w_branch_ssm"], grads["w_branch_att"], grads["w_out"]) = _merge_bwd(dh2, ypre, z, ya, *tail_w)
    (dq, dk, dv, dtable), got = _att_bwd(zb, ya, dya, table, to_send(ffn2_w))
    parts.update(zip(ffn2_w, got))
    grads["att_rpb"] = _att_bias_table_t(dtable)
    dyp = _permute_rows(dypre)
    tail_names = ("ssm_w_glu", "w_branch_ssm", "w_branch_att", "w_out")
    (dzp, dbre, dbim, dlam, dcre, dcimn), got = _s5_bwd(zp, dyp, bre, bim, lam, cre, cimn, to_send(tail_names))
    parts.update(zip(tail_names, got))
    G, P = SSM_GROUPS, SSM_STATE
    for d, (sfx, raw) in enumerate((("_fwd", raw_f), ("_bwd", raw_b))):
        da_re, da_im, dldt, dbt_re, dbt_im = _disc_bwd(
            *raw, dlam[d, :, :, 0, :].reshape(G, P), dlam[d, :, :, 1, :].reshape(G, P),
            _s5_unpack_b(dbre[d]), _s5_unpack_b(dbim[d]), "s5_disc_grad" + sfx)
        grads["ssm_a_re" + sfx] = da_re
        grads["ssm_a_im" + sfx] = da_im
        grads["ssm_log_dt" + sfx] = dldt[:, 0]
        grads["ssm_b_re" + sfx] = dbt_re.transpose(0, 2, 1)
        grads["ssm_b_im" + sfx] = dbt_im.transpose(0, 2, 1)
        grads["ssm_c_re" + sfx] = _s5_unpack_c(dcre[d])
        grads["ssm_c_im" + sfx] = -_s5_unpack_c(dcimn[d])
    dzs = _unpermute_rows(dzp) + dzs_skip
    dz = jnp.concatenate([dzs.astype(BF16), dq, dk, dv, dgs, dga], axis=1)
    dh1, grads["mix_norm"] = _mixin_bwd(dz, dh2, h1, p["mix_norm"], gathered["w_in"])
    grads["w_in"] = _xty(un, dz, "dw_in", col_shards=N_DEV)
    small_t = lambda a, n: jnp.swapaxes(a, -1, -2) if n.startswith("ssm_b_") else a
    pack_small = lambda names, src, pre: _pack([small_t(src[pre + n], n).astype(F32) for n in names], 8)
    early = _Comm(["exchange", "gather"],
                  [grads["w_in"], pack_small(SMALL_EARLY, grads, "")])
    (dh0, grads["ffn1_norm"], do1, a1, dg1, du1), (parts["w_in"], got_early) = _ffn_bwd(
        dh1, h0, p["ffn1_norm"], g1, u1, wg1, wu1, wd1, "ffn1_bwd", early)
    grads["ffn1_w_down"] = _xty(a1, do1, "ffn1_dw_down")
    grads["ffn1_w_gate"], (parts["ffn1_w_down"],) = _xty(dg1, xn1, "ffn1_dw_gate", to_send(("ffn1_w_down",)))
    grads["ffn1_w_up"], (parts["ffn1_w_gate"],) = _xty(du1, xn1, "ffn1_dw_up", to_send(("ffn1_w_gate",)))
    last = _Comm(["exchange", "gather"],
                 [_split_for_devices(grads["ffn1_w_up"], 0), pack_small(SMALL_LATE, grads, "")])
    parts["ffn1_w_up"], got_late = _comm_call(last, "exchange_last")
    got_small = jnp.concatenate([got_early, got_late], axis=1)

    results = {}
    for n, _ in BIG:
        outs = _adamw(parts[n], *[stored(p[pre + n][0], n) for pre in ("", "m_", "v_")], "adamw_" + n)
        results[n] = [stored(o, n)[None] for o in outs]
    early_rows = got_early.shape[1]
    slab = lambda pre: jnp.concatenate([pack_small(SMALL_EARLY, p, pre), pack_small(SMALL_LATE, p, pre)], axis=0)
    small_out = _adamw(got_small, slab(""), slab("m_"), slab("v_"), "adamw_small")
    for names, rows in ((SMALL_EARLY, slice(0, early_rows)), (SMALL_LATE, slice(early_rows, None))):
        shapes = [small_t(p[n], n).shape for n in names]
        for n, vals in zip(names, zip(*[_unpack(out[rows], shapes) for out in small_out])):
            results[n] = [small_t(val, n) for val in vals]

    loss = lax.psum(loss_part[0, 0], ("x", "y", "c"))
    out = [loss, dh0[None]]
    for kind in range(4):
        out += [results[n][kind] for n in WEIGHTS]
    return tuple(out)
```

```python
import functools
import math

import numpy as np
import jax
import jax.numpy as jnp
from jax import lax
from jax.experimental import pallas as pl
from jax.experimental.pallas import tpu as pltpu

F32 = jnp.float32
BF16 = jnp.bfloat16
MESH_ID = pl.DeviceIdType.MESH

SSM_GROUP = 16
SSM_GROUPS = 32
SSM_STATE = 64
SSM_WIDTH = 512
ATT_HEADS = 8
ATT_HEAD_DIM = 64
ATT_WIDTH = 512
GRID_W = 64
WIN_H = 8
WIN_W = 16
EPS = 1e-6
NEG_INF = -1e30
ADAM_LR = 0.001
ADAM_B1 = 0.9
ADAM_B2 = 0.999
ADAM_EPS = 1e-08
ADAM_WD = 0.01
ADAM_STEP = 10

N_DEV = 8
V7X_VMEM_BYTES = 64 * 1024 * 1024
VMEM_LIMIT = V7X_VMEM_BYTES - 8 * 1024 * 1024
SCAN_LANES = 8
ATT_ROWS = 4


def _cparams(sem, vmem=None):
    return pltpu.CompilerParams(dimension_semantics=sem, vmem_limit_bytes=vmem)


def _dot(a, b):
    return jnp.dot(a, b, preferred_element_type=F32)


def _dot_nt(a, b):
    return lax.dot_general(a, b, (((1,), (1,)), ((), ())), preferred_element_type=F32)


def _dot_tn(a, b):
    return lax.dot_general(a, b, (((0,), (0,)), ((), ())), preferred_element_type=F32)


def _rms(h):
    return lax.rsqrt(jnp.mean(h * h, axis=-1, keepdims=True) + EPS)


def _rms_bwd(h, r, v):
    return r * v - h * (r * r * r) * jnp.mean(h * v, axis=-1, keepdims=True)


def _col_sum(x):
    return jnp.sum(x, axis=0, keepdims=True)


def _my_place():
    return lax.axis_index("x"), lax.axis_index("y"), lax.axis_index("c")


def _flat(px, py, pc):
    return 4 * px + 2 * py + pc


class _Comm:
    def __init__(self, kind, arrays):
        self.arrays = list(arrays)
        self.n = len(self.arrays)
        self.kinds = [kind] * self.n if isinstance(kind, str) else list(kind)

    def out_shapes(self):
        return [jax.ShapeDtypeStruct((N_DEV,) + a.shape if k == "gather" else a.shape, a.dtype)
                for k, a in zip(self.kinds, self.arrays)]

    def scratch(self):
        return [pltpu.SemaphoreType.DMA((7 * self.n,)), pltpu.SemaphoreType.DMA((7 * self.n,)),
                pltpu.SemaphoreType.DMA((self.n,))]

    def run(self, srcs, dsts, sems, start):
        send_sems, recv_sems, local_sems = sems
        x, y, c = _my_place()
        mine = _flat(x, y, c)
        for a, (src, dst) in enumerate(zip(srcs, dsts)):
            whole = self.kinds[a] == "gather"
            local = pltpu.make_async_copy(src if whole else src.at[mine], dst.at[mine], local_sems.at[a])
            local.start() if start else local.wait()
            for k in range(1, N_DEV):
                px = 1 - x if k & 4 else x
                py = 1 - y if k & 2 else y
                pc = 1 - c if k & 1 else c
                cp = pltpu.make_async_remote_copy(
                    src_ref=src if whole else src.at[_flat(px, py, pc)], dst_ref=dst.at[mine],
                    send_sem=send_sems.at[7 * a + k - 1], recv_sem=recv_sems.at[7 * a + k - 1],
                    device_id=(px, py, pc), device_id_type=MESH_ID)
                cp.start() if start else cp.wait()


_HBM = pl.BlockSpec(memory_space=pltpu.HBM)


def _comm_call(comm, name):
    def body(*refs):
        srcs, dsts, sems = refs[:comm.n], refs[comm.n:2 * comm.n], refs[2 * comm.n:]
        comm.run(srcs, dsts, sems, True)
        comm.run(srcs, dsts, sems, False)

    return pl.pallas_call(body, name=name, in_specs=[_HBM] * comm.n, out_specs=[_HBM] * comm.n,
                          out_shape=comm.out_shapes(), scratch_shapes=comm.scratch())(*comm.arrays)


def _pallas(core, name, grid, in_specs, out_specs, out_shape, scratch, sem, args, comm=None):
    if comm is None:
        out = pl.pallas_call(core, name=name, grid=grid, in_specs=in_specs, out_specs=out_specs,
                             out_shape=out_shape, scratch_shapes=scratch,
                             compiler_params=_cparams(sem, VMEM_LIMIT))(*args)
        return out, []
    n_in, n_out, n_scr, n = len(in_specs), len(out_specs), len(scratch), comm.n

    def body(*refs):
        ins, srcs = refs[:n_in], refs[n_in:n_in + n]
        outs, dsts = refs[n_in + n:n_in + n + n_out], refs[n_in + n + n_out:n_in + 2 * n + n_out]
        scr, sems = refs[n_in + 2 * n + n_out:n_in + 2 * n + n_out + n_scr], refs[n_in + 2 * n + n_out + n_scr:]
        ids = [pl.program_id(k) for k in range(len(grid))]
        first = functools.reduce(lambda a, b: a & b, [i == 0 for i in ids])
        last = functools.reduce(lambda a, b: a & b, [i == g - 1 for i, g in zip(ids, grid)])

        @pl.when(first)
        def _():
            comm.run(srcs, dsts, sems, True)

        core(*ins, *outs, *scr)

        @pl.when(last)
        def _():
            comm.run(srcs, dsts, sems, False)

    out = pl.pallas_call(
        body, name=name, grid=grid, in_specs=list(in_specs) + [_HBM] * n, out_specs=list(out_specs) + [_HBM] * n,
        out_shape=list(out_shape) + comm.out_shapes(), scratch_shapes=list(scratch) + comm.scratch(),
        compiler_params=_cparams(("arbitrary",) * len(grid), VMEM_LIMIT))(*args, *comm.arrays)
    return out[:n_out], out[n_out:]


FFN_TM = 256


def _ffn_fwd(h, gain, wg, wu, wd, name, comm=None):
    T, D = h.shape
    F = wg.shape[0]
    tm = min(T, 2 * FFN_TM)
    tf = 1408 if F % 1408 == 0 else F
    once = pl.Buffered(1)

    def body(h_ref, gain_ref, wg_ref, wu_ref, wd_ref, ho_ref, xn_ref, g_ref, u_ref):
        hh = h_ref[...]
        xn = (hh * _rms(hh) * gain_ref[...]).astype(BF16)
        xn_ref[...] = xn
        out = None
        for c in range(F // tf):
            cs = slice(c * tf, (c + 1) * tf)
            g = _dot_nt(xn, wg_ref[cs, :])
            u = _dot_nt(xn, wu_ref[cs, :])
            g_ref[:, cs] = g.astype(BF16)
            u_ref[:, cs] = u.astype(BF16)
            a = (g * jax.nn.sigmoid(g) * u).astype(BF16)
            part = _dot(a, wd_ref[cs, :])
            out = part if out is None else out + part
        ho_ref[...] = hh + 0.5 * out

    return _pallas(
        body, name, (T // tm,),
        [pl.BlockSpec((tm, D), lambda i: (i, 0)),
         pl.BlockSpec((1, D), lambda i: (0, 0)),
         pl.BlockSpec((F, D), lambda i: (0, 0), pipeline_mode=once),
         pl.BlockSpec((F, D), lambda i: (0, 0), pipeline_mode=once),
         pl.BlockSpec((F, D), lambda i: (0, 0), pipeline_mode=once)],
        [pl.BlockSpec((tm, D), lambda i: (i, 0)),
         pl.BlockSpec((tm, D), lambda i: (i, 0)),
         pl.BlockSpec((tm, F), lambda i: (i, 0)),
         pl.BlockSpec((tm, F), lambda i: (i, 0))],
        [jax.ShapeDtypeStruct((T, D), F32), jax.ShapeDtypeStruct((T, D), BF16),
         jax.ShapeDtypeStruct((T, F), BF16), jax.ShapeDtypeStruct((T, F), BF16)],
        [], ("parallel",), (h, gain, wg, wu, wd), comm)


def _ffn_bwd(dho, h, gain, g, u, wg, wu, wd, name, comm=None):
    T, D = h.shape
    F = wg.shape[0]
    tm = min(T, FFN_TM)
    tf = 1408 if F % 1408 == 0 else F
    once = pl.Buffered(1)

    def body(dho_ref, h_ref, gain_ref, g_ref, u_ref, wg_ref, wu_ref, wd_ref,
             dh_ref, dgain_ref, do_ref, a_ref, dg_ref, du_ref):
        @pl.when(pl.program_id(0) == 0)
        def _():
            dgain_ref[...] = jnp.zeros_like(dgain_ref)

        dho_v = dho_ref[...]
        do = (0.5 * dho_v).astype(BF16)
        do_ref[...] = do
        dxn = None
        for c in range(F // tf):
            cs = slice(c * tf, (c + 1) * tf)
            da = _dot_nt(do, wd_ref[cs, :])
            gg = g_ref[:, cs].astype(F32)
            uu = u_ref[:, cs].astype(F32)
            s = jax.nn.sigmoid(gg)
            sl = gg * s
            a_ref[:, cs] = (sl * uu).astype(BF16)
            dg = (da * uu * (s * (1.0 + gg * (1.0 - s)))).astype(BF16)
            du = (da * sl).astype(BF16)
            dg_ref[:, cs] = dg
            du_ref[:, cs] = du
            part = _dot(dg, wg_ref[cs, :]) + _dot(du, wu_ref[cs, :])
            dxn = part if dxn is None else dxn + part
        hh = h_ref[...]
        r = _rms(hh)
        dgain_ref[...] += _col_sum(dxn * hh * r)
        dh_ref[...] = dho_v + _rms_bwd(hh, r, dxn * gain_ref[...])

    tok = lambda w: pl.BlockSpec((tm, w), lambda i: (i, 0))
    row = pl.BlockSpec((1, D), lambda i: (0, 0))
    weight = pl.BlockSpec((F, D), lambda i: (0, 0), pipeline_mode=once)
    return _pallas(
        body, name, (T // tm,),
        [tok(D), tok(D), row, tok(F), tok(F), weight, weight, weight],
        [tok(D), row, tok(D), tok(F), tok(F), tok(F)],
        [jax.ShapeDtypeStruct((T, D), F32), jax.ShapeDtypeStruct((1, D), F32),
         jax.ShapeDtypeStruct((T, D), BF16), jax.ShapeDtypeStruct((T, F), BF16),
         jax.ShapeDtypeStruct((T, F), BF16), jax.ShapeDtypeStruct((T, F), BF16)],
        [], ("arbitrary",), (dho, h, gain, g, u, wg, wu, wd), comm)


def _xty(x, y, name, comm=None, col_shards=1):
    T, K = x.shape
    N = y.shape[1]
    tt = min(T, 2048)
    tk = K if K <= 1024 else (1408 if K % 1408 == 0 else K)
    tn = N if N <= 1024 else (1408 if N % 1408 == 0 else (1024 if N % 1024 == 0 else N))
    nt = T // tt
    ws = N // col_shards
    per = tn // ws if col_shards > 1 else 1
    assert col_shards == 1 or (tn % ws == 0 and ws % 128 == 0)

    def body(x_ref, y_ref, o_ref, acc_ref):
        t = pl.program_id(2)

        @pl.when(t == 0)
        def _():
            acc_ref[...] = jnp.zeros_like(acc_ref)

        acc_ref[...] += _dot_tn(x_ref[...], y_ref[...])

        @pl.when(t == nt - 1)
        def _():
            if col_shards == 1:
                o_ref[...] = acc_ref[...].astype(BF16)
            else:
                for s in range(per):
                    o_ref[s] = acc_ref[:, s * ws:(s + 1) * ws].astype(BF16)

    if col_shards == 1:
        out_spec = pl.BlockSpec((tk, tn), lambda k, n, t: (k, n))
        out_shape = jax.ShapeDtypeStruct((K, N), BF16)
    else:
        out_spec = pl.BlockSpec((per, tk, ws), lambda k, n, t: (n, k, 0))
        out_shape = jax.ShapeDtypeStruct((col_shards, K, ws), BF16)
    (out,), got = _pallas(
        body, name, (K // tk, N // tn, nt),
        [pl.BlockSpec((tt, tk), lambda k, n, t: (t, k)), pl.BlockSpec((tt, tn), lambda k, n, t: (t, n))],
        [out_spec], [out_shape], [pltpu.VMEM((tk, tn), F32)],
        ("parallel", "parallel", "arbitrary"), (x, y), comm)
    return out if comm is None else (out, got)


def _mixin_fwd(h, gain, w_in):
    T, D = h.shape
    nn, _, tn = w_in.shape
    N = nn * tn
    tm = min(T, 512)

    def body(h_ref, gain_ref, w_ref, z_ref, zb_ref, un_ref):
        hh = h_ref[...]
        un = (hh * _rms(hh) * gain_ref[...]).astype(BF16)
        un_ref[...] = un
        for s in range(nn):
            z = _dot(un, w_ref[s])
            z_ref[:, s * tn:(s + 1) * tn] = z
            zb_ref[:, s * tn:(s + 1) * tn] = z.astype(BF16)

    return pl.pallas_call(
        body, name="mixin_fwd", grid=(T // tm,),
        in_specs=[pl.BlockSpec((tm, D), lambda i: (i, 0)),
                  pl.BlockSpec((1, D), lambda i: (0, 0)),
                  pl.BlockSpec((nn, D, tn), lambda i: (0, 0, 0))],
        out_specs=[pl.BlockSpec((tm, N), lambda i: (i, 0)),
                   pl.BlockSpec((tm, N), lambda i: (i, 0)),
                   pl.BlockSpec((tm, D), lambda i: (i, 0))],
        out_shape=[jax.ShapeDtypeStruct((T, N), F32), jax.ShapeDtypeStruct((T, N), BF16),
                   jax.ShapeDtypeStruct((T, D), BF16)],
        compiler_params=_cparams(("parallel",), VMEM_LIMIT),
    )(h, gain, w_in)


def _mixin_bwd(dz, dh_res, h, gain, w_in):
    T, D = h.shape
    nn, _, tn = w_in.shape
    tm = min(T, 512)

    def body(dz_ref, dres_ref, h_ref, gain_ref, w_ref, dh_ref, dgain_ref):
        @pl.when(pl.program_id(0) == 0)
        def _():
            dgain_ref[...] = jnp.zeros_like(dgain_ref)

        dun = _dot_nt(dz_ref[:, 0:tn], w_ref[0])
        for s in range(1, nn):
            dun = dun + _dot_nt(dz_ref[:, s * tn:(s + 1) * tn], w_ref[s])
        hh = h_ref[...]
        r = _rms(hh)
        dgain_ref[...] += _col_sum(dun * hh * r)
        dh_ref[...] = dres_ref[...] + _rms_bwd(hh, r, dun * gain_ref[...])

    return pl.pallas_call(
        body, name="mixin_bwd", grid=(T // tm,),
        in_specs=[pl.BlockSpec((tm, nn * tn), lambda i: (i, 0)),
                  pl.BlockSpec((tm, D), lambda i: (i, 0)),
                  pl.BlockSpec((tm, D), lambda i: (i, 0)),
                  pl.BlockSpec((1, D), lambda i: (0, 0)),
                  pl.BlockSpec((nn, D, tn), lambda i: (0, 0, 0))],
        out_specs=[pl.BlockSpec((tm, D), lambda i: (i, 0)),
                   pl.BlockSpec((1, D), lambda i: (0, 0))],
        out_shape=[jax.ShapeDtypeStruct((T, D), F32), jax.ShapeDtypeStruct((1, D), F32)],
        compiler_params=_cparams(("arbitrary",), VMEM_LIMIT),
    )(dz, dh_res, h, gain, w_in)


def _loss_head(h, gain, target):
    T, D = h.shape
    tm = min(T, 1024)

    def body(h_ref, gain_ref, t_ref, loss_ref, dh_ref, dgain_ref):
        @pl.when(pl.program_id(0) == 0)
        def _():
            loss_ref[...] = jnp.zeros_like(loss_ref)
            dgain_ref[...] = jnp.zeros_like(dgain_ref)

        hh = h_ref[...]
        r = _rms(hh)
        e = hh * r * gain_ref[...] - t_ref[...]
        loss_ref[...] += (0.5 / D) * jnp.sum(e * e)
        dy = e * (1.0 / D)
        dgain_ref[...] += _col_sum(dy * hh * r)
        dh_ref[...] = _rms_bwd(hh, r, dy * gain_ref[...])

    return pl.pallas_call(
        body, name="loss_head", grid=(T // tm,),
        in_specs=[pl.BlockSpec((tm, D), lambda i: (i, 0)),
                  pl.BlockSpec((1, D), lambda i: (0, 0)),
                  pl.BlockSpec((tm, D), lambda i: (i, 0))],
        out_specs=[pl.BlockSpec((1, 128), lambda i: (0, 0)),
                   pl.BlockSpec((tm, D), lambda i: (i, 0)),
                   pl.BlockSpec((1, D), lambda i: (0, 0))],
        out_shape=[jax.ShapeDtypeStruct((1, 128), F32), jax.ShapeDtypeStruct((T, D), F32),
                   jax.ShapeDtypeStruct((1, D), F32)],
        compiler_params=_cparams(("arbitrary",), VMEM_LIMIT),
    )(h, gain, target)


def _adamw(parts, w, m, v, name):
    R, C = w.shape
    mult = 16 if parts.dtype == BF16 else 8
    tr = max(t for t in range(mult, min(R, 512) + 1, mult) if R % t == 0)
    c1 = 1.0 - ADAM_B1 ** ADAM_STEP
    c2 = 1.0 - ADAM_B2 ** ADAM_STEP

    def body(p_ref, w_ref, m_ref, v_ref, g_ref, d_ref, nm_ref, nv_ref):
        g = p_ref[0].astype(F32)
        for k in range(1, N_DEV):
            g = g + p_ref[k].astype(F32)
        mm = ADAM_B1 * m_ref[...] + (1.0 - ADAM_B1) * g
        vv = ADAM_B2 * v_ref[...] + (1.0 - ADAM_B2) * (g * g)
        g_ref[...] = g
        nm_ref[...] = mm
        nv_ref[...] = vv
        d_ref[...] = -ADAM_LR * ((mm / c1) / (jnp.sqrt(vv / c2) + ADAM_EPS) + ADAM_WD * w_ref[...])

    spec = pl.BlockSpec((tr, C), lambda i: (i, 0))
    return pl.pallas_call(
        body, name=name, grid=(R // tr,),
        in_specs=[pl.BlockSpec((N_DEV, tr, C), lambda i: (0, i, 0)), spec, spec, spec],
        out_specs=[spec, spec, spec, spec],
        out_shape=[jax.ShapeDtypeStruct((R, C), F32)] * 4,
        compiler_params=_cparams(("parallel",), VMEM_LIMIT),
    )(parts, w, m, v)


S5_NS = 256
S5_NH = 2
S5_NCB = 4
S5_RC = 8192
S5_RC_BWD = 4096
S5_NQ = 4
S5_GROUP = 2


def _disc_math(a_re, a_im, log_dt, bt_re, bt_im):
    dt = jnp.exp(log_dt)
    zr, zi = a_re * dt, a_im * dt
    mag = jnp.exp(zr)
    lb_re, lb_im = mag * jnp.cos(zi), mag * jnp.sin(zi)
    den = a_re * a_re + a_im * a_im
    nr, ni = lb_re - 1.0, lb_im
    f_re = (nr * a_re + ni * a_im) / den
    f_im = (ni * a_re - nr * a_im) / den
    bb_re = f_re[:, None, :] * bt_re - f_im[:, None, :] * bt_im
    bb_im = f_re[:, None, :] * bt_im + f_im[:, None, :] * bt_re
    return lb_re, lb_im, bb_re, bb_im


def _disc_fwd(a_re, a_im, log_dt, bt_re, bt_im, chain_len, name):
    G, P = a_re.shape
    C = bt_re.shape[1]
    n_sq = int(round(math.log2(chain_len)))
    assert 2 ** n_sq == chain_len

    def body(a_re_ref, a_im_ref, ldt_ref, br_ref, bi_ref, lr_ref, li_ref, sr_ref, si_ref, bbr_ref, bbi_ref):
        lr, li, bbr, bbi = _disc_math(a_re_ref[...], a_im_ref[...], ldt_ref[...], br_ref[...], bi_ref[...])
        lr_ref[...] = lr
        li_ref[...] = li
        bbr_ref[...] = bbr
        bbi_ref[...] = bbi
        pr, pi = lr, li
        for _ in range(n_sq):
            pr, pi = pr * pr - pi * pi, 2.0 * pr * pi
        sr_ref[...] = pr
        si_ref[...] = pi

    s2 = jax.ShapeDtypeStruct((G, P), F32)
    s3 = jax.ShapeDtypeStruct((G, C, P), F32)
    return pl.pallas_call(body, name=name, out_shape=[s2, s2, s2, s2, s3, s3])(a_re, a_im, log_dt, bt_re, bt_im)


def _disc_bwd(a_re, a_im, log_dt, bt_re, bt_im, d_lr, d_li, d_bbr, d_bbi, name):
    G, P = a_re.shape
    C = bt_re.shape[1]

    def body(a_re_ref, a_im_ref, ldt_ref, br_ref, bi_ref, c1, c2, c3, c4, o1, o2, o3, o4, o5):
        _, vjp = jax.vjp(_disc_math, a_re_ref[...], a_im_ref[...], ldt_ref[...], br_ref[...], bi_ref[...])
        o1[...], o2[...], o3[...], o4[...], o5[...] = vjp((c1[...], c2[...], c3[...], c4[...]))

    s2 = jax.ShapeDtypeStruct((G, P), F32)
    s3 = jax.ShapeDtypeStruct((G, C, P), F32)
    return pl.pallas_call(body, name=name, out_shape=[s2, s2, jax.ShapeDtypeStruct((G, 1), F32), s3, s3])(
        a_re, a_im, log_dt, bt_re, bt_im, d_lr, d_li, d_bbr, d_bbi)


def _row_block(ib):
    return pl.ds(pl.multiple_of(ib * SCAN_LANES, SCAN_LANES), SCAN_LANES)


def _chain_block(j, i, ascending, n_blocks):
    at = j * (n_blocks // S5_NQ) + i
    return _row_block(jnp.where(ascending, at, n_blocks - 1 - at))


def _unrolled_loop(n, unroll, body, carry):
    trips = n // unroll
    carry = lax.fori_loop(
        0, trips, lambda t, c: functools.reduce(lambda cc, u: body(t * unroll + u, cc), range(unroll), c), carry)
    for i in range(trips * unroll, n):
        carry = body(i, carry)
    return carry


def _cmul_add(lr, li, sr, si, xr, xi):
    return lr * sr - li * si + xr, lr * si + li * sr + xi


def _scan(xr_ref, xi_ref, lr, li, init, ascending, n_blocks, store):
    steps = n_blocks // S5_NQ
    if not store:
        def step(i, carry):
            blocks = [_chain_block(j, i, ascending, n_blocks) for j in range(S5_NQ)]
            return tuple(_cmul_add(lr, li, sr, si, xr_ref[rows, :], xi_ref[rows, :])
                         for (sr, si), rows in zip(carry, blocks))

        return _unrolled_loop(steps, 4, step, init)

    group = S5_GROUP
    assert steps % group == 0

    def trip(t, carry):
        blocks = [[_chain_block(j, t * group + u, ascending, n_blocks) for j in range(S5_NQ)] for u in range(group)]
        xs = [[(xr_ref[rows, :], xi_ref[rows, :]) for rows in row] for row in blocks]
        states = list(carry)
        done = []
        for u in range(group):
            states = [_cmul_add(lr, li, sr, si, xr, xi) for (sr, si), (xr, xi) in zip(states, xs[u])]
            done.append(states)
        for u in range(group):
            for rows, (nr, ni) in zip(blocks[u], done[u]):
                xr_ref[rows, :] = nr
                xi_ref[rows, :] = ni
        return tuple(states)

    return lax.fori_loop(0, steps // group, trip, init)


def _segment_starts(w, lsr, lsi, ascending):
    shape = w[0][0].shape
    row = lax.broadcasted_iota(jnp.int32, shape, 0)
    keep = row != jnp.where(ascending, 0, SCAN_LANES - 1)

    def shift(t):
        t = jnp.where(ascending, pltpu.roll(t, 1, 0), pltpu.roll(t, SCAN_LANES - 1, 0))
        return jnp.where(keep, t, 0.0)

    zero = jnp.zeros(shape, F32)
    c = [(zero, zero)] * S5_NQ
    for _ in range(SCAN_LANES):
        tr, ti = _cmul_add(lsr, lsi, *c[-1], *w[-1])
        c[0] = (shift(tr), shift(ti))
        for j in range(1, S5_NQ):
            c[j] = _cmul_add(lsr, lsi, *c[j - 1], *w[j - 1])
    return tuple(c)


def _first_pass(xr_ref, xi_ref, lam_ref, ascending, n_blocks, conj):
    shape = (SCAN_LANES, xr_ref.shape[1])
    sign = -1.0 if conj else 1.0
    lr = jnp.broadcast_to(lam_ref[0:1, :], shape)
    li = sign * jnp.broadcast_to(lam_ref[1:2, :], shape)
    lsr = jnp.broadcast_to(lam_ref[2:3, :], shape)
    lsi = sign * jnp.broadcast_to(lam_ref[3:4, :], shape)
    zero = jnp.zeros(shape, F32)
    w = _scan(xr_ref, xi_ref, lr, li, ((zero, zero),) * S5_NQ, ascending, n_blocks, store=False)
    return _segment_starts(w, lsr, lsi, ascending), lr, li


def _s5_specs(T):
    NS = S5_NS
    tok = pl.BlockSpec((T, 128), lambda c, d, h: (0, c))
    b_spec = pl.BlockSpec((None, None, None, 128, NS), lambda c, d, h: (d, c, h, 0, 0))
    c_spec = pl.BlockSpec((None, None, None, NS, 128), lambda c, d, h: (d, c, h, 0, 0))
    lam_spec = pl.BlockSpec((None, None, None, 4, NS), lambda c, d, h: (d, c, h, 0, 0))
    return tok, b_spec, c_spec, lam_spec


def _s5_fwd(zp, bre, bim, lam, cre, cimn, comm=None):
    T = zp.shape[0]
    NS = S5_NS
    nb = T // SCAN_LANES
    rc = min(S5_RC, T)
    tok, b_spec, c_spec, lam_spec = _s5_specs(T)

    def body(zp_ref, bre_ref, bim_ref, lam_ref, cre_ref, cim_ref, y_ref, xr_ref, xi_ref):
        d = pl.program_id(1)
        ascending = d == 0

        @pl.when((d == 0) & (pl.program_id(2) == 0))
        def _():
            y_ref[...] = jnp.zeros_like(y_ref)

        def proj(c, _):
            rows = pl.ds(pl.multiple_of(c * rc, rc), rc)
            zz = zp_ref[rows, :]
            xr_ref[rows, :] = _dot(zz, bre_ref[...])
            xi_ref[rows, :] = _dot(zz, bim_ref[...])
            return 0

        lax.fori_loop(0, T // rc, proj, 0)
        starts, lr, li = _first_pass(xr_ref, xi_ref, lam_ref, ascending, nb, conj=False)
        _scan(xr_ref, xi_ref, lr, li, starts, ascending, nb, store=True)

        def outp(c, _):
            rows = pl.ds(pl.multiple_of(c * rc, rc), rc)
            y_ref[rows, :] += (_dot(xr_ref[rows, :].astype(BF16), cre_ref[...])
                               + _dot(xi_ref[rows, :].astype(BF16), cim_ref[...]))
            return 0

        lax.fori_loop(0, T // rc, outp, 0)

    return _pallas(
        body, "s5_fwd", (S5_NCB, 2, S5_NH),
        [tok, b_spec, b_spec, lam_spec, c_spec, c_spec], [tok],
        [jax.ShapeDtypeStruct((T, SSM_WIDTH), F32)],
        [pltpu.VMEM((T, NS), F32), pltpu.VMEM((T, NS), F32)],
        ("parallel", "arbitrary", "arbitrary"), (zp, bre, bim, lam, cre, cimn), comm)


def _s5_bwd(zp, dyp, bre, bim, lam, cre, cimn, comm=None):
    T = zp.shape[0]
    NS, NH = S5_NS, S5_NH
    nb = T // SCAN_LANES
    rc = min(S5_RC_BWD, T)
    tok, b_spec, c_spec, lam_spec = _s5_specs(T)
    dlam_spec = pl.BlockSpec((None, None, None, 2, NS), lambda c, d, h: (d, c, h, 0, 0))

    def body(zp_ref, dyp_ref, bre_ref, bim_ref, lam_ref, cre_ref, cim_ref,
             dzp_ref, dbre_ref, dbim_ref, dlam_ref, dcre_ref, dcim_ref,
             sr_ref, si_ref, gr_ref, gi_ref):
        d = pl.program_id(1)
        ascending = d == 0
        g_ascending = d != 0

        @pl.when((d == 0) & (pl.program_id(2) == 0))
        def _():
            dzp_ref[...] = jnp.zeros_like(dzp_ref)

        dcre_ref[...] = jnp.zeros_like(dcre_ref)
        dcim_ref[...] = jnp.zeros_like(dcim_ref)
        dbre_ref[...] = jnp.zeros_like(dbre_ref)
        dbim_ref[...] = jnp.zeros_like(dbim_ref)

        def proj(c, _):
            rows = pl.ds(pl.multiple_of(c * rc, rc), rc)
            zz = zp_ref[rows, :]
            sr_ref[rows, :] = _dot(zz, bre_ref[...])
            si_ref[rows, :] = _dot(zz, bim_ref[...])
            dy = dyp_ref[rows, :]
            gr_ref[rows, :] = _dot_nt(dy, cre_ref[...])
            gi_ref[rows, :] = _dot_nt(dy, cim_ref[...])
            return 0

        lax.fori_loop(0, T // rc, proj, 0)
        s_starts, lr, li = _first_pass(sr_ref, si_ref, lam_ref, ascending, nb, conj=False)
        _scan(sr_ref, si_ref, lr, li, s_starts, ascending, nb, store=True)
        g_starts, lr, lic = _first_pass(gr_ref, gi_ref, lam_ref, g_ascending, nb, conj=True)

        steps = nb // S5_NQ
        group = S5_GROUP
        assert steps % group == 0

        def gtrip(t, carry, last):
            g, (ar, ai) = carry
            first = t * group
            blocks = [[_chain_block(j, first + u, g_ascending, nb) for j in range(S5_NQ)] for u in range(group)]
            direct = [[(gr_ref[rows, :], gi_ref[rows, :]) for rows in row] for row in blocks]
            done = []
            for u in range(group):
                new = []
                for j, ((g_r, g_i), (d_r, d_i)) in enumerate(zip(g, direct[u])):
                    n_r, n_i = _cmul_add(lr, lic, g_r, g_i, d_r, d_i)
                    if last and u == group - 1:
                        s_r, s_i = s_starts[S5_NQ - 1 - j]
                    else:
                        prev = _chain_block(j, first + u + 1, g_ascending, nb)
                        s_r, s_i = sr_ref[prev, :], si_ref[prev, :]
                    ar = ar + n_r * s_r + n_i * s_i
                    ai = ai + n_i * s_r - n_r * s_i
                    new.append((n_r, n_i))
                g = new
                done.append(new)
            for u in range(group):
                for rows, (n_r, n_i) in zip(blocks[u], done[u]):
                    gr_ref[rows, :] = n_r
                    gi_ref[rows, :] = n_i
            return tuple(g), (ar, ai)

        zero = jnp.zeros((SCAN_LANES, NS), F32)
        carry = lax.fori_loop(0, steps // group - 1, lambda t, c: gtrip(t, c, False), (g_starts, (zero, zero)))
        _, (ar, ai) = gtrip(steps // group - 1, carry, True)
        dlam_ref[0:1, :] = _col_sum(ar)
        dlam_ref[1:2, :] = _col_sum(ai)

        def grads(c, _):
            rows = pl.ds(pl.multiple_of(c * rc, rc), rc)
            zz = zp_ref[rows, :]
            dy = dyp_ref[rows, :]
            g_rb = gr_ref[rows, :].astype(BF16)
            g_ib = gi_ref[rows, :].astype(BF16)
            dcre_ref[...] += _dot_tn(sr_ref[rows, :].astype(BF16), dy)
            dcim_ref[...] += _dot_tn(si_ref[rows, :].astype(BF16), dy)
            dbre_ref[...] += _dot_tn(zz, g_rb)
            dbim_ref[...] += _dot_tn(zz, g_ib)
            dzp_ref[rows, :] += _dot_nt(g_rb, bre_ref[...]) + _dot_nt(g_ib, bim_ref[...])
            return 0

        lax.fori_loop(0, T // rc, grads, 0)

    f32 = lambda *s: jax.ShapeDtypeStruct(s, F32)
    return _pallas(
        body, "s5_bwd", (S5_NCB, 2, S5_NH),
        [tok, tok, b_spec, b_spec, lam_spec, c_spec, c_spec],
        [tok, b_spec, b_spec, dlam_spec, c_spec, c_spec],
        [f32(T, SSM_WIDTH), f32(2, S5_NCB, NH, 128, NS), f32(2, S5_NCB, NH, 128, NS),
         f32(2, S5_NCB, NH, 2, NS), f32(2, S5_NCB, NH, NS, 128), f32(2, S5_NCB, NH, NS, 128)],
        [pltpu.VMEM((T, NS), F32)] * 4,
        ("parallel", "arbitrary", "arbitrary"), (zp, dyp, bre, bim, lam, cre, cimn), comm)


def _s5_delta():
    d = np.zeros((S5_NH, 8, 8 // S5_NH), np.float32)
    for h in range(S5_NH):
        for go in range(8 // S5_NH):
            d[h, h * (8 // S5_NH) + go, go] = 1.0
    return d


def _s5_pack_b(bbt):
    gh = 8 // S5_NH
    b5 = bbt.reshape(S5_NCB, S5_NH, gh, SSM_GROUP, SSM_STATE).transpose(0, 1, 3, 2, 4)
    m = b5[:, :, None] * _s5_delta()[None, :, :, None, :, None]
    return m.reshape(S5_NCB, S5_NH, 128, S5_NS)


def _s5_unpack_b(dm):
    gh = 8 // S5_NH
    d6 = dm.reshape(S5_NCB, S5_NH, 8, SSM_GROUP, gh, SSM_STATE)
    b5 = jnp.sum(d6 * _s5_delta()[None, :, :, None, :, None], axis=2)
    return b5.transpose(0, 1, 3, 2, 4).reshape(SSM_GROUPS, SSM_GROUP, SSM_STATE)


def _s5_pack_c(c):
    gh = 8 // S5_NH
    c5 = c.reshape(S5_NCB, S5_NH, gh, SSM_GROUP, SSM_STATE).transpose(0, 1, 2, 4, 3)
    m = c5[:, :, :, :, None, :] * _s5_delta().transpose(0, 2, 1)[None, :, :, None, :, None]
    return m.reshape(S5_NCB, S5_NH, S5_NS, 128)


def _s5_unpack_c(dm):
    gh = 8 // S5_NH
    d6 = dm.reshape(S5_NCB, S5_NH, gh, SSM_STATE, 8, SSM_GROUP)
    c5 = jnp.sum(d6 * _s5_delta().transpose(0, 2, 1)[None, :, :, None, :, None], axis=4)
    return c5.transpose(0, 1, 2, 4, 3).reshape(SSM_GROUPS, SSM_GROUP, SSM_STATE)


def _s5_pack_lam(x):
    return x.reshape(S5_NCB, S5_NH, S5_NS)


def _permute_rows(x):
    T = x.shape[0]
    return x.reshape(SCAN_LANES, T // SCAN_LANES, -1).transpose(1, 0, 2).reshape(T, -1)


def _unpermute_rows(x):
    T = x.shape[0]
    return x.reshape(T // SCAN_LANES, SCAN_LANES, -1).transpose(1, 0, 2).reshape(T, -1)


ATT_TB = ATT_ROWS * GRID_W
ATT_KB = 3 * ATT_TB


def _att_valid(i, n_rows):
    qi, kj = np.meshgrid(np.arange(ATT_TB), np.arange(ATT_KB), indexing="ij")
    r = i * ATT_ROWS + qi // GRID_W
    c = qi % GRID_W
    rk = (i - 1) * ATT_ROWS + kj // GRID_W
    x = kj % GRID_W
    rs = np.clip(r - WIN_H // 2, 0, n_rows - WIN_H)
    cs = np.clip(c - WIN_W // 2, 0, GRID_W - WIN_W)
    return (rk >= rs) & (rk < rs + WIN_H) & (x >= cs) & (x < cs + WIN_W)


def _att_masked_tables(table, n_rows):
    n = n_rows // ATT_ROWS
    assert n >= 3
    masks = np.stack([_att_valid(i, n_rows) for i in (0, 1, n - 1)])
    return jnp.where(masks[:, None], table[None], NEG_INF)


def _att_variant(i, n):
    return jnp.where(i == 0, 0, jnp.where(i >= n - 1, 2, 1))


def _att_exp(qh, kh, bias):
    s = _dot_nt(qh, kh) + bias
    return jnp.exp(s - jnp.max(s, axis=1, keepdims=True))


def _att_values_and_ones(vh):
    return jnp.concatenate([vh, jnp.ones_like(vh)], axis=1)


def _att_specs(n, col):
    last = n - 1
    cur = lambda i: (jnp.minimum(i, last), col)
    prv = lambda i: (jnp.maximum(jnp.minimum(i, last) - 1, 0), col)
    nxt = lambda i: (jnp.minimum(i + 1, last), col)
    blk = lambda f: pl.BlockSpec((ATT_TB, ATT_WIDTH), f)
    return blk(cur), blk(prv), blk(nxt)


def _att_fwd(zb, biasv):
    T = zb.shape[0]
    W = ATT_WIDTH
    n = T // ATT_TB
    n_rows = T // GRID_W
    cur = _att_specs(n, 0)[0]
    q_cur = _att_specs(n, 1)[0]
    k_cur, k_prv, k_nxt = _att_specs(n, 2)
    v_cur, v_prv, v_nxt = _att_specs(n, 3)

    def body(q_ref, kp_ref, kc_ref, kn_ref, vp_ref, vc_ref, vn_ref, b_ref, y_ref):
        qs = q_ref[...] * 0.125
        kb = jnp.concatenate([kp_ref[...], kc_ref[...], kn_ref[...]], axis=0)
        vb = jnp.concatenate([vp_ref[...], vc_ref[...], vn_ref[...]], axis=0)
        outs = []
        for h in range(ATT_HEADS):
            hs = slice(h * ATT_HEAD_DIM, (h + 1) * ATT_HEAD_DIM)
            e = _att_exp(qs[:, hs], kb[:, hs], b_ref[h]).astype(BF16)
            ov = _dot(e, _att_values_and_ones(vb[:, hs]))
            outs.append(ov[:, :ATT_HEAD_DIM] * (1.0 / ov[:, ATT_HEAD_DIM:ATT_HEAD_DIM + 1]))
        y_ref[...] = jnp.concatenate(outs, axis=1).astype(BF16)

    return pl.pallas_call(
        body, name="att_fwd", grid=(n,),
        in_specs=[q_cur, k_prv, k_cur, k_nxt, v_prv, v_cur, v_nxt,
                  pl.BlockSpec((None, ATT_HEADS, ATT_TB, ATT_KB), lambda i: (_att_variant(i, n), 0, 0, 0))],
        out_specs=cur,
        out_shape=jax.ShapeDtypeStruct((T, W), BF16),
        compiler_params=_cparams(("parallel",), VMEM_LIMIT),
    )(zb, zb, zb, zb, zb, zb, zb, biasv)


def _att_bwd(zb, y, do, biasv, comm=None):
    T = zb.shape[0]
    W = ATT_WIDTH
    n = T // ATT_TB
    n_rows = T // GRID_W
    cur = _att_specs(n, 0)[0]
    q_cur = _att_specs(n, 1)[0]
    k_cur, k_prv, k_nxt = _att_specs(n, 2)
    v_cur, v_prv, v_nxt = _att_specs(n, 3)
    done = pl.BlockSpec((ATT_TB, W), lambda i: (jnp.maximum(i - 1, 0), 0))
    bias_spec = pl.BlockSpec((None, ATT_HEADS, ATT_TB, ATT_KB), lambda i: (_att_variant(i, n), 0, 0, 0))

    def body(q_ref, y_ref, do_ref, kp_ref, kc_ref, kn_ref, vp_ref, vc_ref, vn_ref, b_ref,
             dq_ref, dk_ref, dv_ref, db_ref, acck_ref, accv_ref):
        i = pl.program_id(0)

        @pl.when(i == 0)
        def _():
            db_ref[...] = jnp.zeros_like(db_ref)
            acck_ref[...] = jnp.zeros_like(acck_ref)
            accv_ref[...] = jnp.zeros_like(accv_ref)

        @pl.when((i > 0) & (i < n))
        def _():
            slot = lax.rem(i + 1, 3)
            acck_ref[slot] = jnp.zeros((ATT_TB, W), F32)
            accv_ref[slot] = jnp.zeros((ATT_TB, W), F32)

        @pl.when(i < n)
        def _():
            qs = q_ref[...] * 0.125
            dob = do_ref[...]
            dy = dob.astype(F32) * y_ref[...].astype(F32)
            kb = jnp.concatenate([kp_ref[...], kc_ref[...], kn_ref[...]], axis=0)
            vb = jnp.concatenate([vp_ref[...], vc_ref[...], vn_ref[...]], axis=0)
            dqs, dks, dvs = [], [], []
            for h in range(ATT_HEADS):
                hs = slice(h * ATT_HEAD_DIM, (h + 1) * ATT_HEAD_DIM)
                qh, kh, vh, doh = qs[:, hs], kb[:, hs], vb[:, hs], dob[:, hs]
                e = _att_exp(qh, kh, b_ref[h])
                p = e * (1.0 / jnp.sum(e, axis=1, keepdims=True))
                dp = _dot_nt(doh, vh)
                ds = p * (dp - jnp.sum(dy[:, hs], axis=1, keepdims=True))
                db_ref[h] += ds
                dsb = ds.astype(BF16)
                dqs.append(_dot(dsb, kh) * 0.125)
                dks.append(_dot_tn(dsb, qh))
                dvs.append(_dot_tn(p.astype(BF16), doh))
            dq_ref[...] = jnp.concatenate(dqs, axis=1).astype(BF16)
            dk_all = jnp.concatenate(dks, axis=1)
            dv_all = jnp.concatenate(dvs, axis=1)
            for b in range(3):
                slot = lax.rem(i + 2 + b, 3)
                rows = slice(b * ATT_TB, (b + 1) * ATT_TB)
                acck_ref[slot] += dk_all[rows]
                accv_ref[slot] += dv_all[rows]

        slot = lax.rem(i + 2, 3)
        dk_ref[...] = acck_ref[slot].astype(BF16)
        dv_ref[...] = accv_ref[slot].astype(BF16)

    return _pallas(
        body, "att_bwd", (n + 1,),
        [q_cur, cur, cur, k_prv, k_cur, k_nxt, v_prv, v_cur, v_nxt, bias_spec],
        [cur, done, done, pl.BlockSpec((ATT_HEADS, ATT_TB, ATT_KB), lambda i: (0, 0, 0))],
        [jax.ShapeDtypeStruct((T, W), BF16)] * 3 + [jax.ShapeDtypeStruct((ATT_HEADS, ATT_TB, ATT_KB), F32)],
        [pltpu.VMEM((3, ATT_TB, W), F32), pltpu.VMEM((3, ATT_TB, W), F32)],
        ("arbitrary",), (zb, y, do, zb, zb, zb, zb, zb, zb, biasv), comm)


def _att_selectors():
    rsel = np.zeros((ATT_ROWS, 3 * ATT_ROWS, 2 * WIN_H - 1), np.float32)
    for a in range(ATT_ROWS):
        for b in range(3 * ATT_ROWS):
            rsel[a, b, b - a - ATT_ROWS + WIN_H - 1] = 1.0
    csel = np.zeros((GRID_W, GRID_W, 2 * WIN_W - 1), np.float32)
    for c in range(GRID_W):
        for x in range(GRID_W):
            csel[c, x, min(max(x - c, -(WIN_W - 1)), WIN_W - 1) + WIN_W - 1] = 1.0
    return rsel, csel


def _att_bias_table(rpb):
    rsel, csel = _att_selectors()
    hi = lax.Precision.HIGHEST
    t = jnp.einsum('hrd,abr->habd', rpb, rsel, precision=hi)
    t = jnp.einsum('habd,cxd->hacbx', t, csel, precision=hi)
    return t.reshape(ATT_HEADS, ATT_TB, ATT_KB)


def _att_bias_table_t(dtable):
    rsel, csel = _att_selectors()
    hi = lax.Precision.HIGHEST
    t = dtable.reshape(ATT_HEADS, ATT_ROWS, GRID_W, 3 * ATT_ROWS, GRID_W)
    t = jnp.einsum('hacbx,cxd->habd', t, csel, precision=hi)
    return jnp.einsum('habd,abr->hrd', t, rsel, precision=hi)


GELU_K = math.sqrt(2.0 / math.pi)
GELU_C = 0.044715
MERGE_TM = 256


def _gelu(x):
    return 0.5 * x * (1.0 + jnp.tanh(GELU_K * (x + GELU_C * x * x * x)))


def _gelu_grad(x):
    t = jnp.tanh(GELU_K * (x + GELU_C * x * x * x))
    return 0.5 * (1.0 + t) + 0.5 * x * (1.0 - t * t) * GELU_K * (1.0 + 3.0 * GELU_C * x * x)


def _merge_forward(ypre, zs, gs, ga, ya, ssm_d, w_glu, b_glu, w_bs, w_ba):
    ys = ypre + ssm_d * zs
    yg = _gelu(ys)
    sg = jax.nn.sigmoid(_dot(yg.astype(BF16), w_glu) + b_glu)
    y2 = yg * sg
    bs = _dot(y2.astype(BF16), w_bs)
    ba = _dot(ya, w_ba)
    s1 = jax.nn.sigmoid(gs)
    s2 = jax.nn.sigmoid(ga)
    merged = s1 * bs + s2 * ba
    return ys, yg, sg, y2, bs, ba, s1, s2, merged


def _merge_in_specs(D, W, tm):
    tok = lambda w, c: pl.BlockSpec((tm, w), lambda i: (i, c))
    full = lambda r, c: pl.BlockSpec((r, c), lambda i: (0, 0))
    z_specs = [tok(W, 0), tok(D, 4 * W // D), tok(D, 4 * W // D + 1)]
    w_specs = [full(1, W), full(W, W), full(1, W), full(W, D), full(W, D), full(D, D)]
    return tok, z_specs, w_specs


def _merge_fwd(ypre, z, ya, h1, ssm_d, w_glu, b_glu, w_bs, w_ba, w_out):
    T, D = h1.shape
    W = ypre.shape[1]
    tm = min(T, 2 * MERGE_TM)
    tok, z_specs, w_specs = _merge_in_specs(D, W, tm)

    def body(ypre_ref, zs_ref, gs_ref, ga_ref, ya_ref, h1_ref, d_ref, wglu_ref, bglu_ref, wbs_ref, wba_ref, wout_ref,
             h2_ref):
        merged = _merge_forward(ypre_ref[...], zs_ref[...], gs_ref[...], ga_ref[...], ya_ref[...], d_ref[...],
                                wglu_ref[...], bglu_ref[...], wbs_ref[...], wba_ref[...])[-1]
        h2_ref[...] = h1_ref[...] + _dot(merged.astype(BF16), wout_ref[...])

    return pl.pallas_call(
        body, name="merge_fwd", grid=(T // tm,),
        in_specs=[tok(W, 0)] + z_specs + [tok(W, 0), tok(D, 0)] + w_specs,
        out_specs=tok(D, 0),
        out_shape=jax.ShapeDtypeStruct((T, D), F32),
        compiler_params=_cparams(("parallel",), VMEM_LIMIT),
    )(ypre, z, z, z, ya, h1, ssm_d, w_glu, b_glu, w_bs, w_ba, w_out)


def _merge_bwd(dh2, ypre, z, ya, ssm_d, w_glu, b_glu, w_bs, w_ba, w_out):
    T, D = dh2.shape
    W = ypre.shape[1]
    tm = min(T, MERGE_TM)
    tok, z_specs, w_specs = _merge_in_specs(D, W, tm)

    def body(dh2_ref, ypre_ref, zs_ref, gs_ref, ga_ref, ya_ref, d_ref, wglu_ref, bglu_ref, wbs_ref, wba_ref, wout_ref,
             dypre_ref, dzs_ref, dgs_ref, dga_ref, dya_ref, dd_ref, dwglu_ref, dbglu_ref, dwbs_ref, dwba_ref, dwout_ref):
        @pl.when(pl.program_id(0) == 0)
        def _():
            for r in (dd_ref, dwglu_ref, dbglu_ref, dwbs_ref, dwba_ref, dwout_ref):
                r[...] = jnp.zeros_like(r)

        zs = zs_ref[...]
        ya = ya_ref[...]
        ys, yg, sg, y2, bs, ba, s1, s2, merged = _merge_forward(
            ypre_ref[...], zs, gs_ref[...], ga_ref[...], ya, d_ref[...],
            wglu_ref[...], bglu_ref[...], wbs_ref[...], wba_ref[...])
        dh2b = dh2_ref[...].astype(BF16)
        dmerged = _dot_nt(dh2b, wout_ref[...])
        dwout_ref[...] += _dot_tn(merged.astype(BF16), dh2b)
        dbs = (dmerged * s1).astype(BF16)
        dba = (dmerged * s2).astype(BF16)
        dgs_ref[...] = (dmerged * bs * s1 * (1.0 - s1)).astype(BF16)
        dga_ref[...] = (dmerged * ba * s2 * (1.0 - s2)).astype(BF16)
        dwbs_ref[...] += _dot_tn(y2.astype(BF16), dbs)
        dwba_ref[...] += _dot_tn(ya, dba)
        dya_ref[...] = _dot_nt(dba, wba_ref[...]).astype(BF16)
        dy2 = _dot_nt(dbs, wbs_ref[...])
        dvv = dy2 * yg * sg * (1.0 - sg)
        dvvb = dvv.astype(BF16)
        dyg = dy2 * sg + _dot_nt(dvvb, wglu_ref[...])
        dwglu_ref[...] += _dot_tn(yg.astype(BF16), dvvb)
        dbglu_ref[...] += _col_sum(dvv)
        dys = dyg * _gelu_grad(ys)
        dd_ref[...] += _col_sum(dys * zs)
        dzs_ref[...] = dys * d_ref[...]
        dypre_ref[...] = dys.astype(BF16)

    f32 = lambda *s: jax.ShapeDtypeStruct(s, F32)
    b16 = lambda *s: jax.ShapeDtypeStruct(s, BF16)
    return pl.pallas_call(
        body, name="merge_bwd", grid=(T // tm,),
        in_specs=[tok(D, 0), tok(W, 0)] + z_specs + [tok(W, 0)] + w_specs,
        out_specs=[tok(W, 0), tok(W, 0), tok(D, 0), tok(D, 0), tok(W, 0)] + w_specs,
        out_shape=[b16(T, W), f32(T, W), b16(T, D), b16(T, D), b16(T, W),
                   f32(1, W), f32(W, W), f32(1, W), f32(W, D), f32(W, D), f32(D, D)],
        compiler_params=_cparams(("arbitrary",), VMEM_LIMIT),
    )(dh2, ypre, z, z, z, ya, ssm_d, w_glu, b_glu, w_bs, w_ba, w_out)


def _cast_shards(weights):
    def body(*refs):
        n = len(refs) // 2
        for src, dst in zip(refs[:n], refs[n:]):
            dst[...] = src[0].astype(BF16)

    return pl.pallas_call(
        body, name="cast_shards",
        out_shape=[jax.ShapeDtypeStruct(w.shape[1:], BF16) for w in weights],
        compiler_params=_cparams(None, VMEM_LIMIT))(*weights)


def _gather_two_level(shards, name):
    n = len(shards)

    def body(*refs):
        x_refs, out_refs = refs[:n], refs[n:2 * n]
        send_sems, recv_sems, local_sems = refs[2 * n:]
        x, y, c = _my_place()
        me, sibling = (x, y, c), (x, y, 1 - c)
        chips = [(1 - x, y), (x, 1 - y), (1 - x, 1 - y)]

        def copy(a, k, block, to, own=False):
            slot = out_refs[a].at[_flat(*block)]
            return pltpu.make_async_remote_copy(
                src_ref=x_refs[a] if own else slot, dst_ref=slot,
                send_sem=send_sems.at[7 * a + k], recv_sem=recv_sems.at[7 * a + k],
                device_id=to, device_id_type=MESH_ID)

        sent, local = [], []
        for a in range(n):
            local.append(pltpu.make_async_copy(x_refs[a], out_refs[a].at[_flat(*me)], local_sems.at[a]))
            local[-1].start()
            sent.append(copy(a, 0, me, sibling, own=True))
            sent += [copy(a, 1 + j, me, (*chip, c), own=True) for j, chip in enumerate(chips)]
        for cp in sent:
            cp.start()
        for a in range(n):
            for j, chip in enumerate(chips):
                copy(a, 1 + j, (*chip, c), me).wait_recv()
                sent.append(copy(a, 4 + j, (*chip, c), sibling))
                sent[-1].start()
        for a in range(n):
            copy(a, 0, sibling, me).wait_recv()
            for j, chip in enumerate(chips):
                copy(a, 4 + j, (*chip, 1 - c), me).wait_recv()
        for cp in sent:
            cp.wait_send()
        for cp in local:
            cp.wait()

    return pl.pallas_call(
        body, name=name, in_specs=[_HBM] * n, out_specs=[_HBM] * n,
        out_shape=[jax.ShapeDtypeStruct((N_DEV,) + s.shape, s.dtype) for s in shards],
        scratch_shapes=[pltpu.SemaphoreType.DMA((7 * n,)), pltpu.SemaphoreType.DMA((7 * n,)),
                        pltpu.SemaphoreType.DMA((n,))],
    )(*shards)


PACK_COLS = 1024
BIG = (("ffn1_w_gate", 1), ("ffn1_w_up", 1), ("ffn1_w_down", 0), ("w_in", 1), ("ssm_w_glu", 0),
       ("w_branch_ssm", 1), ("w_branch_att", 1), ("w_out", 0),
       ("ffn2_w_gate", 1), ("ffn2_w_up", 1), ("ffn2_w_down", 0))
BIG_AXIS = dict(BIG)
TRANSPOSED = ("ffn1_w_gate", "ffn1_w_up", "ffn2_w_gate", "ffn2_w_up")
SSM_DIR = ("ssm_a_re", "ssm_a_im", "ssm_log_dt", "ssm_b_re", "ssm_b_im", "ssm_c_re", "ssm_c_im")
SMALL_EARLY = (("mix_norm",) + tuple(n + "_fwd" for n in SSM_DIR) + tuple(n + "_bwd" for n in SSM_DIR)
               + ("ssm_d", "ssm_b_glu", "att_rpb", "ffn2_norm", "final_norm"))
SMALL_LATE = ("ffn1_norm",)
WEIGHTS = ("ffn1_norm", "ffn1_w_gate", "ffn1_w_up", "ffn1_w_down", "mix_norm", "w_in") \
    + tuple(n + "_fwd" for n in SSM_DIR) + tuple(n + "_bwd" for n in SSM_DIR) \
    + ("ssm_d", "ssm_w_glu", "ssm_b_glu", "att_rpb", "w_branch_ssm", "w_branch_att", "w_out",
       "ffn2_norm", "ffn2_w_gate", "ffn2_w_up", "ffn2_w_down", "final_norm")


def _pad_rows(a, mult):
    pad = (-a.shape[-2]) % mult
    if pad:
        a = jnp.concatenate([a, jnp.zeros(a.shape[:-2] + (pad, a.shape[-1]), a.dtype)], axis=-2)
    return a


def _pack(arrays, row_mult):
    flat = jnp.concatenate([a.reshape(-1) for a in arrays])
    pad = (-flat.shape[0]) % PACK_COLS
    if pad:
        flat = jnp.concatenate([flat, jnp.zeros((pad,), flat.dtype)])
    return _pad_rows(flat.reshape(-1, PACK_COLS), row_mult)


def _unpack(slab, shapes):
    flat = slab.reshape(-1)
    out, at = [], 0
    for s in shapes:
        n = int(np.prod(s))
        out.append(flat[at:at + n].reshape(s))
        at += n
    return out


def _split_for_devices(g, axis):
    r, c = g.shape
    if axis == 1:
        return g.reshape(r, N_DEV, c // N_DEV).transpose(1, 0, 2).astype(BF16)
    return g.reshape(N_DEV, r // N_DEV, c).astype(BF16)


def _join_shards(gathered, axis):
    _, r, c = gathered.shape
    if axis == 1:
        return gathered.transpose(1, 0, 2).reshape(r, N_DEV * c)
    return gathered.reshape(N_DEV * r, c)


def _s5_direction_inputs(p, sfx, chain_len):
    bt_re = p["ssm_b_re" + sfx][0].transpose(0, 2, 1)
    bt_im = p["ssm_b_im" + sfx][0].transpose(0, 2, 1)
    raw = (p["ssm_a_re" + sfx][0], p["ssm_a_im" + sfx][0], p["ssm_log_dt" + sfx][0][:, None], bt_re, bt_im)
    lr, li, sr, si, bbr, bbi = _disc_fwd(*raw, chain_len,"s5_disc" + sfx)
    lam = jnp.stack([_s5_pack_lam(t) for t in (lr, li, sr, si)], axis=2)
    mats = (_s5_pack_b(bbr), _s5_pack_b(bbi), lam,
            _s5_pack_c(p["ssm_c_re" + sfx][0]), _s5_pack_c(-p["ssm_c_im" + sfx][0]))
    return raw, mats


def kernel(x, ffn1_norm, ffn1_w_gate, ffn1_w_up, ffn1_w_down, mix_norm, w_in, ssm_a_re_fwd, ssm_a_im_fwd, ssm_log_dt_fwd, ssm_b_re_fwd, ssm_b_im_fwd, ssm_c_re_fwd, ssm_c_im_fwd, ssm_a_re_bwd, ssm_a_im_bwd, ssm_log_dt_bwd, ssm_b_re_bwd, ssm_b_im_bwd, ssm_c_re_bwd, ssm_c_im_bwd, ssm_d, ssm_w_glu, ssm_b_glu, att_rpb, w_branch_ssm, w_branch_att, w_out, ffn2_norm, ffn2_w_gate, ffn2_w_up, ffn2_w_down, final_norm, loss_target, m_ffn1_norm, m_ffn1_w_gate, m_ffn1_w_up, m_ffn1_w_down, m_mix_norm, m_w_in, m_ssm_a_re_fwd, m_ssm_a_im_fwd, m_ssm_log_dt_fwd, m_ssm_b_re_fwd, m_ssm_b_im_fwd, m_ssm_c_re_fwd, m_ssm_c_im_fwd, m_ssm_a_re_bwd, m_ssm_a_im_bwd, m_ssm_log_dt_bwd, m_ssm_b_re_bwd, m_ssm_b_im_bwd, m_ssm_c_re_bwd, m_ssm_c_im_bwd, m_ssm_d, m_ssm_w_glu, m_ssm_b_glu, m_att_rpb, m_w_branch_ssm, m_w_branch_att, m_w_out, m_ffn2_norm, m_ffn2_w_gate, m_ffn2_w_up, m_ffn2_w_down, m_final_norm, v_ffn1_norm, v_ffn1_w_gate, v_ffn1_w_up, v_ffn1_w_down, v_mix_norm, v_w_in, v_ssm_a_re_fwd, v_ssm_a_im_fwd, v_ssm_log_dt_fwd, v_ssm_b_re_fwd, v_ssm_b_im_fwd, v_ssm_c_re_fwd, v_ssm_c_im_fwd, v_ssm_a_re_bwd, v_ssm_a_im_bwd, v_ssm_log_dt_bwd, v_ssm_b_re_bwd, v_ssm_b_im_bwd, v_ssm_c_re_bwd, v_ssm_c_im_bwd, v_ssm_d, v_ssm_w_glu, v_ssm_b_glu, v_att_rpb, v_w_branch_ssm, v_w_branch_att, v_w_out, v_ffn2_norm, v_ffn2_w_gate, v_ffn2_w_up, v_ffn2_w_down, v_final_norm):
    p = dict(locals())
    x = p["x"][0]
    target = p["loss_target"][0]
    T, D = x.shape

    stored = lambda a, n: jnp.swapaxes(a, -1, -2) if n in TRANSPOSED else a
    cut_axis = lambda n: 0 if n in TRANSPOSED else BIG_AXIS[n]
    shard = dict(zip([n for n, _ in BIG], _cast_shards([stored(p[n], n) for n, _ in BIG])))
    ffn1_w = ("ffn1_w_gate", "ffn1_w_up", "ffn1_w_down")
    mix_w = ("w_in", "ssm_w_glu", "w_branch_ssm", "w_branch_att", "w_out")
    ffn2_w = ("ffn2_w_gate", "ffn2_w_up", "ffn2_w_down")
    gathered = dict(zip(ffn1_w, _gather_two_level([shard[n] for n in ffn1_w], "gather_ffn1")))
    full = lambda n: _join_shards(gathered[n], cut_axis(n))

    h0 = x
    wg1, wu1, wd1 = [full(n) for n in ffn1_w]
    (h1, xn1, g1, u1), got = _ffn_fwd(h0, p["ffn1_norm"], wg1, wu1, wd1, "ffn1_fwd",
                                      _Comm("gather", [shard[n] for n in mix_w]))
    gathered.update(zip(mix_w, got))
    z, zb, un = _mixin_fwd(h1, p["mix_norm"], gathered["w_in"])
    W = SSM_WIDTH
    zp = _permute_rows(zb[:, :W])
    chain_len = T // SCAN_LANES // S5_NQ
    raw_f, mats_f = _s5_direction_inputs(p, "_fwd", chain_len)
    raw_b, mats_b = _s5_direction_inputs(p, "_bwd", chain_len)
    bre, bim, lam, cre, cimn = [jnp.stack([f, b]) for f, b in zip(mats_f, mats_b)]
    bre, bim, cre, cimn = [t.astype(BF16) for t in (bre, bim, cre, cimn)]
    (yp,), got = _s5_fwd(zp, bre, bim, lam, cre, cimn, _Comm("gather", [shard[n] for n in ffn2_w]))
    gathered.update(zip(ffn2_w, got))
    ypre = _unpermute_rows(yp)
    table = _att_masked_tables(_att_bias_table(p["att_rpb"][0]), T // GRID_W)
    ya = _att_fwd(zb, table)
    tail_w = (p["ssm_d"], full("ssm_w_glu"), p["ssm_b_glu"], full("w_branch_ssm"), full("w_branch_att"), full("w_out"))
    h2 = _merge_fwd(ypre, z, ya, h1, *tail_w)
    wg2, wu2, wd2 = [full(n) for n in ffn2_w]
    (h3, xn2, g2, u2), _ = _ffn_fwd(h2, p["ffn2_norm"], wg2, wu2, wd2, "ffn2_fwd")
    loss_part, dh3, d_final = _loss_head(h3, p["final_norm"][None], target)

    grads = {"final_norm": d_final[0]}
    to_send = lambda names: _Comm("exchange", [_split_for_devices(grads[n], cut_axis(n)) for n in names])
    parts = {}
    (dh2, grads["ffn2_norm"], do2, a2, dg2, du2), _ = _ffn_bwd(
        dh3, h2, p["ffn2_norm"], g2, u2, wg2, wu2, wd2, "ffn2_bwd")
    grads["ffn2_w_gate"] = _xty(dg2, xn2, "ffn2_dw_gate")
    grads["ffn2_w_up"] = _xty(du2, xn2, "ffn2_dw_up")
    grads["ffn2_w_down"] = _xty(a2, do2, "ffn2_dw_down")
    (dypre, dzs_skip, dgs, dga, dya, grads["ssm_d"], grads["ssm_w_glu"], grads["ssm_b_glu"],
     grads["w_branch_ssm"], grads["w_branch_att"], grads["w_out"]) = _merge_bwd(dh2, ypre, z, ya, *tail_w)
    (dq, dk, dv, dtable), got = _att_bwd(zb, ya, dya, table, to_send(ffn2_w))
    parts.update(zip(ffn2_w, got))
    grads["att_rpb"] = _att_bias_table_t(dtable)
    dyp = _permute_rows(dypre)
    tail_names = ("ssm_w_glu", "w_branch_ssm", "w_branch_att", "w_out")
    (dzp, dbre, dbim, dlam, dcre, dcimn), got = _s5_bwd(zp, dyp, bre, bim, lam, cre, cimn, to_send(tail_names))
    parts.update(zip(tail_names, got))
    G, P = SSM_GROUPS, SSM_STATE
    for d, (sfx, raw) in enumerate((("_fwd", raw_f), ("_bwd", raw_b))):
        da_re, da_im, dldt, dbt_re, dbt_im = _disc_bwd(
            *raw, dlam[d, :, :, 0, :].reshape(G, P), dlam[d, :, :, 1, :].reshape(G, P),
            _s5_unpack_b(dbre[d]), _s5_unpack_b(dbim[d]), "s5_disc_grad" + sfx)
        grads["ssm_a_re" + sfx] = da_re
        grads["ssm_a_im" + sfx] = da_im
        grads["ssm_log_dt" + sfx] = dldt[:, 0]
        grads["ssm_b_re" + sfx] = dbt_re.transpose(0, 2, 1)
        grads["ssm_b_im" + sfx] = dbt_im.transpose(0, 2, 1)
        grads["ssm_c_re" + sfx] = _s5_unpack_c(dcre[d])
        grads["ssm_c_im" + sfx] = -_s5_unpack_c(dcimn[d])
    dzs = _unpermute_rows(dzp) + dzs_skip
    dz = jnp.concatenate([dzs.astype(BF16), dq, dk, dv, dgs, dga], axis=1)
    dh1, grads["mix_norm"] = _mixin_bwd(dz, dh2, h1, p["mix_norm"], gathered["w_in"])
    grads["w_in"] = _xty(un, dz, "dw_in", col_shards=N_DEV)
    small_t = lambda a, n: jnp.swapaxes(a, -1, -2) if n.startswith("ssm_b_") else a
    pack_small = lambda names, src, pre: _pack([small_t(src[pre + n], n).astype(F32) for n in names], 8)
    early = _Comm(["exchange", "gather"],
                  [grads["w_in"], pack_small(SMALL_EARLY, grads, "")])
    (dh0, grads["ffn1_norm"], do1, a1, dg1, du1), (parts["w_in"], got_early) = _ffn_bwd(
        dh1, h0, p["ffn1_norm"], g1, u1, wg1, wu1, wd1, "ffn1_bwd", early)
    grads["ffn1_w_down"] = _xty(a1, do1, "ffn1_dw_down")
    grads["ffn1_w_gate"], (parts["ffn1_w_down"],) = _xty(dg1, xn1, "ffn1_dw_gate", to_send(("ffn1_w_down",)))
    grads["ffn1_w_up"], (parts["ffn1_w_gate"],) = _xty(du1, xn1, "ffn1_dw_up", to_send(("ffn1_w_gate",)))
    last = _Comm(["exchange", "gather"],
                 [_split_for_devices(grads["ffn1_w_up"], 0), pack_small(SMALL_LATE, grads, "")])
    parts["ffn1_w_up"], got_late = _comm_call(last, "exchange_last")
    got_small = jnp.concatenate([got_early, got_late], axis=1)

    results = {}
    for n, _ in BIG:
        outs = _adamw(parts[n], *[stored(p[pre + n][0], n) for pre in ("", "m_", "v_")], "adamw_" + n)
        results[n] = [stored(o, n)[None] for o in outs]
    early_rows = got_early.shape[1]
    slab = lambda pre: jnp.concatenate([pack_small(SMALL_EARLY, p, pre), pack_small(SMALL_LATE, p, pre)], axis=0)
    small_out = _adamw(got_small, slab(""), slab("m_"), slab("v_"), "adamw_small")
    for names, rows in ((SMALL_EARLY, slice(0, early_rows)), (SMALL_LATE, slice(early_rows, None))):
        shapes = [small_t(p[n], n).shape for n in names]
        for n, vals in zip(names, zip(*[_unpack(out[rows], shapes) for out in small_out])):
            results[n] = [small_t(val, n) for val in vals]

    loss = lax.psum(loss_part[0, 0], ("x", "y", "c"))
    out = [loss, dh0[None]]
    for kind in range(4):
        out += [results[n][kind] for n in WEIGHTS]
    return tuple(out)
```

```python
import functools
import math

import numpy as np
import jax
import jax.numpy as jnp
from jax import lax
from jax.experimental import pallas as pl
from jax.experimental.pallas import tpu as pltpu

F32 = jnp.float32
BF16 = jnp.bfloat16
MESH_ID = pl.DeviceIdType.MESH

SSM_GROUP = 16
SSM_GROUPS = 32
SSM_STATE = 64
SSM_WIDTH = 512
ATT_HEADS = 8
ATT_HEAD_DIM = 64
ATT_WIDTH = 512
GRID_W = 64
WIN_H = 8
WIN_W = 16
EPS = 1e-6
NEG_INF = -1e30
ADAM_LR = 0.001
ADAM_B1 = 0.9
ADAM_B2 = 0.999
ADAM_EPS = 1e-08
ADAM_WD = 0.01
ADAM_STEP = 10

N_DEV = 8
V7X_VMEM_BYTES = 64 * 1024 * 1024
VMEM_LIMIT = V7X_VMEM_BYTES - 8 * 1024 * 1024
SCAN_LANES = 8
ATT_ROWS = 4


def _cparams(sem, vmem=None):
    return pltpu.CompilerParams(dimension_semantics=sem, vmem_limit_bytes=vmem)


def _dot(a, b):
    return jnp.dot(a, b, preferred_element_type=F32)


def _dot_nt(a, b):
    return lax.dot_general(a, b, (((1,), (1,)), ((), ())), preferred_element_type=F32)


def _dot_tn(a, b):
    return lax.dot_general(a, b, (((0,), (0,)), ((), ())), preferred_element_type=F32)


def _rms(h):
    return lax.rsqrt(jnp.mean(h * h, axis=-1, keepdims=True) + EPS)


def _rms_bwd(h, r, v):
    return r * v - h * (r * r * r) * jnp.mean(h * v, axis=-1, keepdims=True)


def _col_sum(x):
    return jnp.sum(x, axis=0, keepdims=True)


def _my_place():
    return lax.axis_index("x"), lax.axis_index("y"), lax.axis_index("c")


def _flat(px, py, pc):
    return 4 * px + 2 * py + pc


class _Comm:
    def __init__(self, kind, arrays):
        self.arrays = list(arrays)
        self.n = len(self.arrays)
        self.kinds = [kind] * self.n if isinstance(kind, str) else list(kind)

    def out_shapes(self):
        return [jax.ShapeDtypeStruct((N_DEV,) + a.shape if k == "gather" else a.shape, a.dtype)
                for k, a in zip(self.kinds, self.arrays)]

    def scratch(self):
        return [pltpu.SemaphoreType.DMA((7 * self.n,)), pltpu.SemaphoreType.DMA((7 * self.n,)),
                pltpu.SemaphoreType.DMA((self.n,))]

    def run(self, srcs, dsts, sems, start):
        send_sems, recv_sems, local_sems = sems
        x, y, c = _my_place()
        mine = _flat(x, y, c)
        for a, (src, dst) in enumerate(zip(srcs, dsts)):
            whole = self.kinds[a] == "gather"
            local = pltpu.make_async_copy(src if whole else src.at[mine], dst.at[mine], local_sems.at[a])
            local.start() if start else local.wait()
            for k in range(1, N_DEV):
                px = 1 - x if k & 4 else x
                py = 1 - y if k & 2 else y
                pc = 1 - c if k & 1 else c
                cp = pltpu.make_async_remote_copy(
                    src_ref=src if whole else src.at[_flat(px, py, pc)], dst_ref=dst.at[mine],
                    send_sem=send_sems.at[7 * a + k - 1], recv_sem=recv_sems.at[7 * a + k - 1],
                    device_id=(px, py, pc), device_id_type=MESH_ID)
                cp.start() if start else cp.wait()


_HBM = pl.BlockSpec(memory_space=pltpu.HBM)


def _comm_call(comm, name):
    def body(*refs):
        srcs, dsts, sems = refs[:comm.n], refs[comm.n:2 * comm.n], refs[2 * comm.n:]
        comm.run(srcs, dsts, sems, True)
        comm.run(srcs, dsts, sems, False)

    return pl.pallas_call(body, name=name, in_specs=[_HBM] * comm.n, out_specs=[_HBM] * comm.n,
                          out_shape=comm.out_shapes(), scratch_shapes=comm.scratch())(*comm.arrays)


def _pallas(core, name, grid, in_specs, out_specs, out_shape, scratch, sem, args, comm=None):
    if comm is None:
        out = pl.pallas_call(core, name=name, grid=grid, in_specs=in_specs, out_specs=out_specs,
                             out_shape=out_shape, scratch_shapes=scratch,
                             compiler_params=_cparams(sem, VMEM_LIMIT))(*args)
        return out, []
    n_in, n_out, n_scr, n = len(in_specs), len(out_specs), len(scratch), comm.n

    def body(*refs):
        ins, srcs = refs[:n_in], refs[n_in:n_in + n]
        outs, dsts = refs[n_in + n:n_in + n + n_out], refs[n_in + n + n_out:n_in + 2 * n + n_out]
        scr, sems = refs[n_in + 2 * n + n_out:n_in + 2 * n + n_out + n_scr], refs[n_in + 2 * n + n_out + n_scr:]
        ids = [pl.program_id(k) for k in range(len(grid))]
        first = functools.reduce(lambda a, b: a & b, [i == 0 for i in ids])
        last = functools.reduce(lambda a, b: a & b, [i == g - 1 for i, g in zip(ids, grid)])

        @pl.when(first)
        def _():
            comm.run(srcs, dsts, sems, True)

        core(*ins, *outs, *scr)

        @pl.when(last)
        def _():
            comm.run(srcs, dsts, sems, False)

    out = pl.pallas_call(
        body, name=name, grid=grid, in_specs=list(in_specs) + [_HBM] * n, out_specs=list(out_specs) + [_HBM] * n,
        out_shape=list(out_shape) + comm.out_shapes(), scratch_shapes=list(scratch) + comm.scratch(),
        compiler_params=_cparams(("arbitrary",) * len(grid), VMEM_LIMIT))(*args, *comm.arrays)
    return out[:n_out], out[n_out:]


FFN_TM = 256


def _ffn_fwd(h, gain, wg, wu, wd, name, comm=None):
    T, D = h.shape
    F = wg.shape[0]
    tm = min(T, FFN_TM)
    once = pl.Buffered(1)

    def body(h_ref, gain_ref, wg_ref, wu_ref, wd_ref, ho_ref, xn_ref, g_ref, u_ref):
        hh = h_ref[...]
        xn = (hh * _rms(hh) * gain_ref[...]).astype(BF16)
        xn_ref[...] = xn
        g = _dot_nt(xn, wg_ref[...])
        u = _dot_nt(xn, wu_ref[...])
        g_ref[...] = g.astype(BF16)
        u_ref[...] = u.astype(BF16)
        a = (g * jax.nn.sigmoid(g) * u).astype(BF16)
        ho_ref[...] = hh + 0.5 * _dot(a, wd_ref[...])

    return _pallas(
        body, name, (T // tm,),
        [pl.BlockSpec((tm, D), lambda i: (i, 0)),
         pl.BlockSpec((1, D), lambda i: (0, 0)),
         pl.BlockSpec((F, D), lambda i: (0, 0), pipeline_mode=once),
         pl.BlockSpec((F, D), lambda i: (0, 0), pipeline_mode=once),
         pl.BlockSpec((F, D), lambda i: (0, 0), pipeline_mode=once)],
        [pl.BlockSpec((tm, D), lambda i: (i, 0)),
         pl.BlockSpec((tm, D), lambda i: (i, 0)),
         pl.BlockSpec((tm, F), lambda i: (i, 0)),
         pl.BlockSpec((tm, F), lambda i: (i, 0))],
        [jax.ShapeDtypeStruct((T, D), F32), jax.ShapeDtypeStruct((T, D), BF16),
         jax.ShapeDtypeStruct((T, F), BF16), jax.ShapeDtypeStruct((T, F), BF16)],
        [], ("parallel",), (h, gain, wg, wu, wd), comm)


def _ffn_bwd(dho, h, gain, g, u, wg, wu, wd, name, comm=None):
    T, D = h.shape
    F = wg.shape[0]
    tm = min(T, FFN_TM)
    tf = 1408 if F % 1408 == 0 else F
    once = pl.Buffered(1)

    def body(dho_ref, h_ref, gain_ref, g_ref, u_ref, wg_ref, wu_ref, wd_ref,
             dh_ref, dgain_ref, do_ref, a_ref, dg_ref, du_ref):
        @pl.when(pl.program_id(0) == 0)
        def _():
            dgain_ref[...] = jnp.zeros_like(dgain_ref)

        dho_v = dho_ref[...]
        do = (0.5 * dho_v).astype(BF16)
        do_ref[...] = do
        dxn = None
        for c in range(F // tf):
            cs = slice(c * tf, (c + 1) * tf)
            da = _dot_nt(do, wd_ref[cs, :])
            gg = g_ref[:, cs].astype(F32)
            uu = u_ref[:, cs].astype(F32)
            s = jax.nn.sigmoid(gg)
            sl = gg * s
            a_ref[:, cs] = (sl * uu).astype(BF16)
            dg = (da * uu * (s * (1.0 + gg * (1.0 - s)))).astype(BF16)
            du = (da * sl).astype(BF16)
            dg_ref[:, cs] = dg
            du_ref[:, cs] = du
            part = _dot(dg, wg_ref[cs, :]) + _dot(du, wu_ref[cs, :])
            dxn = part if dxn is None else dxn + part
        hh = h_ref[...]
        r = _rms(hh)
        dgain_ref[...] += _col_sum(dxn * hh * r)
        dh_ref[...] = dho_v + _rms_bwd(hh, r, dxn * gain_ref[...])

    tok = lambda w: pl.BlockSpec((tm, w), lambda i: (i, 0))
    row = pl.BlockSpec((1, D), lambda i: (0, 0))
    weight = pl.BlockSpec((F, D), lambda i: (0, 0), pipeline_mode=once)
    return _pallas(
        body, name, (T // tm,),
        [tok(D), tok(D), row, tok(F), tok(F), weight, weight, weight],
        [tok(D), row, tok(D), tok(F), tok(F), tok(F)],
        [jax.ShapeDtypeStruct((T, D), F32), jax.ShapeDtypeStruct((1, D), F32),
         jax.ShapeDtypeStruct((T, D), BF16), jax.ShapeDtypeStruct((T, F), BF16),
         jax.ShapeDtypeStruct((T, F), BF16), jax.ShapeDtypeStruct((T, F), BF16)],
        [], ("arbitrary",), (dho, h, gain, g, u, wg, wu, wd), comm)


def _xty(x, y, name, comm=None, col_shards=1):
    T, K = x.shape
    N = y.shape[1]
    tt = min(T, 2048)
    tk = K if K <= 1024 else (1408 if K % 1408 == 0 else K)
    tn = N if N <= 1024 else (1408 if N % 1408 == 0 else (1024 if N % 1024 == 0 else N))
    nt = T // tt
    ws = N // col_shards
    per = tn // ws if col_shards > 1 else 1
    assert col_shards == 1 or (tn % ws == 0 and ws % 128 == 0)

    def body(x_ref, y_ref, o_ref, acc_ref):
        t = pl.program_id(2)

        @pl.when(t == 0)
        def _():
            acc_ref[...] = jnp.zeros_like(acc_ref)

        acc_ref[...] += _dot_tn(x_ref[...], y_ref[...])

        @pl.when(t == nt - 1)
        def _():
            if col_shards == 1:
                o_ref[...] = acc_ref[...].astype(BF16)
            else:
                for s in range(per):
                    o_ref[s] = acc_ref[:, s * ws:(s + 1) * ws].astype(BF16)

    if col_shards == 1:
        out_spec = pl.BlockSpec((tk, tn), lambda k, n, t: (k, n))
        out_shape = jax.ShapeDtypeStruct((K, N), BF16)
    else:
        out_spec = pl.BlockSpec((per, tk, ws), lambda k, n, t: (n, k, 0))
        out_shape = jax.ShapeDtypeStruct((col_shards, K, ws), BF16)
    (out,), got = _pallas(
        body, name, (K // tk, N // tn, nt),
        [pl.BlockSpec((tt, tk), lambda k, n, t: (t, k)), pl.BlockSpec((tt, tn), lambda k, n, t: (t, n))],
        [out_spec], [out_shape], [pltpu.VMEM((tk, tn), F32)],
        ("parallel", "parallel", "arbitrary"), (x, y), comm)
    return out if comm is None else (out, got)


def _mixin_fwd(h, gain, w_in):
    T, D = h.shape
    nn, _, tn = w_in.shape
    N = nn * tn
    tm = min(T, 512)

    def body(h_ref, gain_ref, w_ref, zb_ref, un_ref):
        hh = h_ref[...]
        un = (hh * _rms(hh) * gain_ref[...]).astype(BF16)
        un_ref[...] = un
        for s in range(nn):
            zb_ref[:, s * tn:(s + 1) * tn] = _dot(un, w_ref[s]).astype(BF16)

    return pl.pallas_call(
        body, name="mixin_fwd", grid=(T // tm,),
        in_specs=[pl.BlockSpec((tm, D), lambda i: (i, 0)),
                  pl.BlockSpec((1, D), lambda i: (0, 0)),
                  pl.BlockSpec((nn, D, tn), lambda i: (0, 0, 0))],
        out_specs=[pl.BlockSpec((tm, N), lambda i: (i, 0)),
                   pl.BlockSpec((tm, D), lambda i: (i, 0))],
        out_shape=[jax.ShapeDtypeStruct((T, N), BF16), jax.ShapeDtypeStruct((T, D), BF16)],
        compiler_params=_cparams(("parallel",), VMEM_LIMIT),
    )(h, gain, w_in)


def _mixin_bwd(dz, dh_res, h, gain, w_in):
    T, D = h.shape
    nn, _, tn = w_in.shape
    tm = min(T, 512)

    def body(dz_ref, dres_ref, h_ref, gain_ref, w_ref, dh_ref, dgain_ref):
        @pl.when(pl.program_id(0) == 0)
        def _():
            dgain_ref[...] = jnp.zeros_like(dgain_ref)

        dun = _dot_nt(dz_ref[:, 0:tn], w_ref[0])
        for s in range(1, nn):
            dun = dun + _dot_nt(dz_ref[:, s * tn:(s + 1) * tn], w_ref[s])
        hh = h_ref[...]
        r = _rms(hh)
        dgain_ref[...] += _col_sum(dun * hh * r)
        dh_ref[...] = dres_ref[...] + _rms_bwd(hh, r, dun * gain_ref[...])

    return pl.pallas_call(
        body, name="mixin_bwd", grid=(T // tm,),
        in_specs=[pl.BlockSpec((tm, nn * tn), lambda i: (i, 0)),
                  pl.BlockSpec((tm, D), lambda i: (i, 0)),
                  pl.BlockSpec((tm, D), lambda i: (i, 0)),
                  pl.BlockSpec((1, D), lambda i: (0, 0)),
                  pl.BlockSpec((nn, D, tn), lambda i: (0, 0, 0))],
        out_specs=[pl.BlockSpec((tm, D), lambda i: (i, 0)),
                   pl.BlockSpec((1, D), lambda i: (0, 0))],
        out_shape=[jax.ShapeDtypeStruct((T, D), F32), jax.ShapeDtypeStruct((1, D), F32)],
        compiler_params=_cparams(("arbitrary",), VMEM_LIMIT),
    )(dz, dh_res, h, gain, w_in)


def _loss_head(h, gain, target):
    T, D = h.shape
    tm = min(T, 1024)

    def body(h_ref, gain_ref, t_ref, loss_ref, dh_ref, dgain_ref):
        @pl.when(pl.program_id(0) == 0)
        def _():
            loss_ref[...] = jnp.zeros_like(loss_ref)
            dgain_ref[...] = jnp.zeros_like(dgain_ref)

        hh = h_ref[...]
        r = _rms(hh)
        e = hh * r * gain_ref[...] - t_ref[...]
        loss_ref[...] += (0.5 / D) * jnp.sum(e * e)
        dy = e * (1.0 / D)
        dgain_ref[...] += _col_sum(dy * hh * r)
        dh_ref[...] = _rms_bwd(hh, r, dy * gain_ref[...])

    return pl.pallas_call(
        body, name="loss_head", grid=(T // tm,),
        in_specs=[pl.BlockSpec((tm, D), lambda i: (i, 0)),
                  pl.BlockSpec((1, D), lambda i: (0, 0)),
                  pl.BlockSpec((tm, D), lambda i: (i, 0))],
        out_specs=[pl.BlockSpec((1, 128), lambda i: (0, 0)),
                   pl.BlockSpec((tm, D), lambda i: (i, 0)),
                   pl.BlockSpec((1, D), lambda i: (0, 0))],
        out_shape=[jax.ShapeDtypeStruct((1, 128), F32), jax.ShapeDtypeStruct((T, D), F32),
                   jax.ShapeDtypeStruct((1, D), F32)],
        compiler_params=_cparams(("arbitrary",), VMEM_LIMIT),
    )(h, gain, target)


def _adamw(parts, w, m, v, name):
    R, C = w.shape
    mult = 16 if parts.dtype == BF16 else 8
    tr = max(t for t in range(mult, min(R, 512) + 1, mult) if R % t == 0)
    c1 = 1.0 - ADAM_B1 ** ADAM_STEP
    c2 = 1.0 - ADAM_B2 ** ADAM_STEP

    def body(p_ref, w_ref, m_ref, v_ref, g_ref, d_ref, nm_ref, nv_ref):
        g = p_ref[0].astype(F32)
        for k in range(1, N_DEV):
            g = g + p_ref[k].astype(F32)
        mm = ADAM_B1 * m_ref[...] + (1.0 - ADAM_B1) * g
        vv = ADAM_B2 * v_ref[...] + (1.0 - ADAM_B2) * (g * g)
        g_ref[...] = g
        nm_ref[...] = mm
        nv_ref[...] = vv
        d_ref[...] = -ADAM_LR * ((mm / c1) / (jnp.sqrt(vv / c2) + ADAM_EPS) + ADAM_WD * w_ref[...])

    spec = pl.BlockSpec((tr, C), lambda i: (i, 0))
    return pl.pallas_call(
        body, name=name, grid=(R // tr,),
        in_specs=[pl.BlockSpec((N_DEV, tr, C), lambda i: (0, i, 0)), spec, spec, spec],
        out_specs=[spec, spec, spec, spec],
        out_shape=[jax.ShapeDtypeStruct((R, C), F32)] * 4,
        compiler_params=_cparams(("parallel",), VMEM_LIMIT),
    )(parts, w, m, v)


S5_NS = 256
S5_NH = 2
S5_NCB = 4
S5_RC = 4096
S5_NQ = 4
S5_GROUP = 2


def _disc_math(a_re, a_im, log_dt, bt_re, bt_im):
    dt = jnp.exp(log_dt)
    zr, zi = a_re * dt, a_im * dt
    mag = jnp.exp(zr)
    lb_re, lb_im = mag * jnp.cos(zi), mag * jnp.sin(zi)
    den = a_re * a_re + a_im * a_im
    nr, ni = lb_re - 1.0, lb_im
    f_re = (nr * a_re + ni * a_im) / den
    f_im = (ni * a_re - nr * a_im) / den
    bb_re = f_re[:, None, :] * bt_re - f_im[:, None, :] * bt_im
    bb_im = f_re[:, None, :] * bt_im + f_im[:, None, :] * bt_re
    return lb_re, lb_im, bb_re, bb_im


def _disc_fwd(a_re, a_im, log_dt, bt_re, bt_im, chain_len, name):
    G, P = a_re.shape
    C = bt_re.shape[1]
    n_sq = int(round(math.log2(chain_len)))
    assert 2 ** n_sq == chain_len

    def body(a_re_ref, a_im_ref, ldt_ref, br_ref, bi_ref, lr_ref, li_ref, sr_ref, si_ref, bbr_ref, bbi_ref):
        lr, li, bbr, bbi = _disc_math(a_re_ref[...], a_im_ref[...], ldt_ref[...], br_ref[...], bi_ref[...])
        lr_ref[...] = lr
        li_ref[...] = li
        bbr_ref[...] = bbr
        bbi_ref[...] = bbi
        pr, pi = lr, li
        for _ in range(n_sq):
            pr, pi = pr * pr - pi * pi, 2.0 * pr * pi
        sr_ref[...] = pr
        si_ref[...] = pi

    s2 = jax.ShapeDtypeStruct((G, P), F32)
    s3 = jax.ShapeDtypeStruct((G, C, P), F32)
    return pl.pallas_call(body, name=name, out_shape=[s2, s2, s2, s2, s3, s3])(a_re, a_im, log_dt, bt_re, bt_im)


def _disc_bwd(a_re, a_im, log_dt, bt_re, bt_im, d_lr, d_li, d_bbr, d_bbi, name):
    G, P = a_re.shape
    C = bt_re.shape[1]

    def body(a_re_ref, a_im_ref, ldt_ref, br_ref, bi_ref, c1, c2, c3, c4, o1, o2, o3, o4, o5):
        _, vjp = jax.vjp(_disc_math, a_re_ref[...], a_im_ref[...], ldt_ref[...], br_ref[...], bi_ref[...])
        o1[...], o2[...], o3[...], o4[...], o5[...] = vjp((c1[...], c2[...], c3[...], c4[...]))

    s2 = jax.ShapeDtypeStruct((G, P), F32)
    s3 = jax.ShapeDtypeStruct((G, C, P), F32)
    return pl.pallas_call(body, name=name, out_shape=[s2, s2, jax.ShapeDtypeStruct((G, 1), F32), s3, s3])(
        a_re, a_im, log_dt, bt_re, bt_im, d_lr, d_li, d_bbr, d_bbi)


def _row_block(ib):
    return pl.ds(pl.multiple_of(ib * SCAN_LANES, SCAN_LANES), SCAN_LANES)


def _chain_block(j, i, ascending, n_blocks):
    at = j * (n_blocks // S5_NQ) + i
    return _row_block(jnp.where(ascending, at, n_blocks - 1 - at))


def _unrolled_loop(n, unroll, body, carry):
    trips = n // unroll
    carry = lax.fori_loop(
        0, trips, lambda t, c: functools.reduce(lambda cc, u: body(t * unroll + u, cc), range(unroll), c), carry)
    for i in range(trips * unroll, n):
        carry = body(i, carry)
    return carry


def _cmul_add(lr, li, sr, si, xr, xi):
    return lr * sr - li * si + xr, lr * si + li * sr + xi


def _scan(xr_ref, xi_ref, lr, li, init, ascending, n_blocks, store):
    steps = n_blocks // S5_NQ
    if not store:
        def step(i, carry):
            blocks = [_chain_block(j, i, ascending, n_blocks) for j in range(S5_NQ)]
            return tuple(_cmul_add(lr, li, sr, si, xr_ref[rows, :], xi_ref[rows, :])
                         for (sr, si), rows in zip(carry, blocks))

        return _unrolled_loop(steps, 4, step, init)

    group = S5_GROUP
    assert steps % group == 0

    def trip(t, carry):
        blocks = [[_chain_block(j, t * group + u, ascending, n_blocks) for j in range(S5_NQ)] for u in range(group)]
        xs = [[(xr_ref[rows, :], xi_ref[rows, :]) for rows in row] for row in blocks]
        states = list(carry)
        done = []
        for u in range(group):
            states = [_cmul_add(lr, li, sr, si, xr, xi) for (sr, si), (xr, xi) in zip(states, xs[u])]
            done.append(states)
        for u in range(group):
            for rows, (nr, ni) in zip(blocks[u], done[u]):
                xr_ref[rows, :] = nr
                xi_ref[rows, :] = ni
        return tuple(states)

    return lax.fori_loop(0, steps // group, trip, init)


def _segment_starts(w, lsr, lsi, ascending):
    shape = w[0][0].shape
    row = lax.broadcasted_iota(jnp.int32, shape, 0)
    keep = row != jnp.where(ascending, 0, SCAN_LANES - 1)

    def shift(t):
        t = jnp.where(ascending, pltpu.roll(t, 1, 0), pltpu.roll(t, SCAN_LANES - 1, 0))
        return jnp.where(keep, t, 0.0)

    zero = jnp.zeros(shape, F32)
    c = [(zero, zero)] * S5_NQ
    for _ in range(SCAN_LANES):
        tr, ti = _cmul_add(lsr, lsi, *c[-1], *w[-1])
        c[0] = (shift(tr), shift(ti))
        for j in range(1, S5_NQ):
            c[j] = _cmul_add(lsr, lsi, *c[j - 1], *w[j - 1])
    return tuple(c)


def _first_pass(xr_ref, xi_ref, lam_ref, ascending, n_blocks, conj):
    shape = (SCAN_LANES, xr_ref.shape[1])
    sign = -1.0 if conj else 1.0
    lr = jnp.broadcast_to(lam_ref[0:1, :], shape)
    li = sign * jnp.broadcast_to(lam_ref[1:2, :], shape)
    lsr = jnp.broadcast_to(lam_ref[2:3, :], shape)
    lsi = sign * jnp.broadcast_to(lam_ref[3:4, :], shape)
    zero = jnp.zeros(shape, F32)
    w = _scan(xr_ref, xi_ref, lr, li, ((zero, zero),) * S5_NQ, ascending, n_blocks, store=False)
    return _segment_starts(w, lsr, lsi, ascending), lr, li


def _s5_specs(T):
    NS = S5_NS
    tok = pl.BlockSpec((T, 128), lambda c, d, h: (0, c))
    b_spec = pl.BlockSpec((None, None, None, 128, NS), lambda c, d, h: (d, c, h, 0, 0))
    c_spec = pl.BlockSpec((None, None, None, NS, 128), lambda c, d, h: (d, c, h, 0, 0))
    lam_spec = pl.BlockSpec((None, None, None, 4, NS), lambda c, d, h: (d, c, h, 0, 0))
    return tok, b_spec, c_spec, lam_spec


def _s5_fwd(zp, bre, bim, lam, cre, cimn, comm=None):
    T = zp.shape[0]
    NS = S5_NS
    nb = T // SCAN_LANES
    rc = min(S5_RC, T)
    tok, b_spec, c_spec, lam_spec = _s5_specs(T)

    def body(zp_ref, bre_ref, bim_ref, lam_ref, cre_ref, cim_ref, y_ref, xr_ref, xi_ref):
        d = pl.program_id(1)
        ascending = d == 0

        @pl.when((d == 0) & (pl.program_id(2) == 0))
        def _():
            y_ref[...] = jnp.zeros_like(y_ref)

        def proj(c, _):
            rows = pl.ds(pl.multiple_of(c * rc, rc), rc)
            zz = zp_ref[rows, :]
            xr_ref[rows, :] = _dot(zz, bre_ref[...])
            xi_ref[rows, :] = _dot(zz, bim_ref[...])
            return 0

        lax.fori_loop(0, T // rc, proj, 0)
        starts, lr, li = _first_pass(xr_ref, xi_ref, lam_ref, ascending, nb, conj=False)
        _scan(xr_ref, xi_ref, lr, li, starts, ascending, nb, store=True)

        def outp(c, _):
            rows = pl.ds(pl.multiple_of(c * rc, rc), rc)
            y_ref[rows, :] += (_dot(xr_ref[rows, :].astype(BF16), cre_ref[...])
                               + _dot(xi_ref[rows, :].astype(BF16), cim_ref[...]))
            return 0

        lax.fori_loop(0, T // rc, outp, 0)

    return _pallas(
        body, "s5_fwd", (S5_NCB, 2, S5_NH),
        [tok, b_spec, b_spec, lam_spec, c_spec, c_spec], [tok],
        [jax.ShapeDtypeStruct((T, SSM_WIDTH), F32)],
        [pltpu.VMEM((T, NS), F32), pltpu.VMEM((T, NS), F32)],
        ("parallel", "arbitrary", "arbitrary"), (zp, bre, bim, lam, cre, cimn), comm)


def _s5_bwd(zp, dyp, bre, bim, lam, cre, cimn, comm=None):
    T = zp.shape[0]
    NS, NH = S5_NS, S5_NH
    nb = T // SCAN_LANES
    rc = min(S5_RC, T)
    tok, b_spec, c_spec, lam_spec = _s5_specs(T)
    dlam_spec = pl.BlockSpec((None, None, None, 2, NS), lambda c, d, h: (d, c, h, 0, 0))

    def body(zp_ref, dyp_ref, bre_ref, bim_ref, lam_ref, cre_ref, cim_ref,
             dzp_ref, dbre_ref, dbim_ref, dlam_ref, dcre_ref, dcim_ref,
             sr_ref, si_ref, gr_ref, gi_ref):
        d = pl.program_id(1)
        ascending = d == 0
        g_ascending = d != 0

        @pl.when((d == 0) & (pl.program_id(2) == 0))
        def _():
            dzp_ref[...] = jnp.zeros_like(dzp_ref)

        dcre_ref[...] = jnp.zeros_like(dcre_ref)
        dcim_ref[...] = jnp.zeros_like(dcim_ref)
        dbre_ref[...] = jnp.zeros_like(dbre_ref)
        dbim_ref[...] = jnp.zeros_like(dbim_ref)

        def proj(c, _):
            rows = pl.ds(pl.multiple_of(c * rc, rc), rc)
            zz = zp_ref[rows, :]
            sr_ref[rows, :] = _dot(zz, bre_ref[...])
            si_ref[rows, :] = _dot(zz, bim_ref[...])
            dy = dyp_ref[rows, :]
            gr_ref[rows, :] = _dot_nt(dy, cre_ref[...])
            gi_ref[rows, :] = _dot_nt(dy, cim_ref[...])
            return 0

        lax.fori_loop(0, T // rc, proj, 0)
        s_starts, lr, li = _first_pass(sr_ref, si_ref, lam_ref, ascending, nb, conj=False)
        _scan(sr_ref, si_ref, lr, li, s_starts, ascending, nb, store=True)
        g_starts, lr, lic = _first_pass(gr_ref, gi_ref, lam_ref, g_ascending, nb, conj=True)

        steps = nb // S5_NQ
        group = S5_GROUP
        assert steps % group == 0

        def gtrip(t, carry, last):
            g, (ar, ai) = carry
            first = t * group
            blocks = [[_chain_block(j, first + u, g_ascending, nb) for j in range(S5_NQ)] for u in range(group)]
            direct = [[(gr_ref[rows, :], gi_ref[rows, :]) for rows in row] for row in blocks]
            done = []
            for u in range(group):
                new = []
                for j, ((g_r, g_i), (d_r, d_i)) in enumerate(zip(g, direct[u])):
                    n_r, n_i = _cmul_add(lr, lic, g_r, g_i, d_r, d_i)
                    if last and u == group - 1:
                        s_r, s_i = s_starts[S5_NQ - 1 - j]
                    else:
                        prev = _chain_block(j, first + u + 1, g_ascending, nb)
                        s_r, s_i = sr_ref[prev, :], si_ref[prev, :]
                    ar = ar + n_r * s_r + n_i * s_i
                    ai = ai + n_i * s_r - n_r * s_i
                    new.append((n_r, n_i))
                g = new
                done.append(new)
            for u in range(group):
                for rows, (n_r, n_i) in zip(blocks[u], done[u]):
                    gr_ref[rows, :] = n_r
                    gi_ref[rows, :] = n_i
            return tuple(g), (ar, ai)

        zero = jnp.zeros((SCAN_LANES, NS), F32)
        carry = lax.fori_loop(0, steps // group - 1, lambda t, c: gtrip(t, c, False), (g_starts, (zero, zero)))
        _, (ar, ai) = gtrip(steps // group - 1, carry, True)
        dlam_ref[0:1, :] = _col_sum(ar)
        dlam_ref[1:2, :] = _col_sum(ai)

        def grads(c, _):
            rows = pl.ds(pl.multiple_of(c * rc, rc), rc)
            zz = zp_ref[rows, :]
            dy = dyp_ref[rows, :]
            g_rb = gr_ref[rows, :].astype(BF16)
            g_ib = gi_ref[rows, :].astype(BF16)
            dcre_ref[...] += _dot_tn(sr_ref[rows, :].astype(BF16), dy)
            dcim_ref[...] += _dot_tn(si_ref[rows, :].astype(BF16), dy)
            dbre_ref[...] += _dot_tn(zz, g_rb)
            dbim_ref[...] += _dot_tn(zz, g_ib)
            dzp_ref[rows, :] += _dot_nt(g_rb, bre_ref[...]) + _dot_nt(g_ib, bim_ref[...])
            return 0

        lax.fori_loop(0, T // rc, grads, 0)

    f32 = lambda *s: jax.ShapeDtypeStruct(s, F32)
    return _pallas(
        body, "s5_bwd", (S5_NCB, 2, S5_NH),
        [tok, tok, b_spec, b_spec, lam_spec, c_spec, c_spec],
        [tok, b_spec, b_spec, dlam_spec, c_spec, c_spec],
        [f32(T, SSM_WIDTH), f32(2, S5_NCB, NH, 128, NS), f32(2, S5_NCB, NH, 128, NS),
         f32(2, S5_NCB, NH, 2, NS), f32(2, S5_NCB, NH, NS, 128), f32(2, S5_NCB, NH, NS, 128)],
        [pltpu.VMEM((T, NS), F32)] * 4,
        ("parallel", "arbitrary", "arbitrary"), (zp, dyp, bre, bim, lam, cre, cimn), comm)


def _s5_delta():
    d = np.zeros((S5_NH, 8, 8 // S5_NH), np.float32)
    for h in range(S5_NH):
        for go in range(8 // S5_NH):
            d[h, h * (8 // S5_NH) + go, go] = 1.0
    return d


def _s5_pack_b(bbt):
    gh = 8 // S5_NH
    b5 = bbt.reshape(S5_NCB, S5_NH, gh, SSM_GROUP, SSM_STATE).transpose(0, 1, 3, 2, 4)
    m = b5[:, :, None] * _s5_delta()[None, :, :, None, :, None]
    return m.reshape(S5_NCB, S5_NH, 128, S5_NS)


def _s5_unpack_b(dm):
    gh = 8 // S5_NH
    d6 = dm.reshape(S5_NCB, S5_NH, 8, SSM_GROUP, gh, SSM_STATE)
    b5 = jnp.sum(d6 * _s5_delta()[None, :, :, None, :, None], axis=2)
    return b5.transpose(0, 1, 3, 2, 4).reshape(SSM_GROUPS, SSM_GROUP, SSM_STATE)


def _s5_pack_c(c):
    gh = 8 // S5_NH
    c5 = c.reshape(S5_NCB, S5_NH, gh, SSM_GROUP, SSM_STATE).transpose(0, 1, 2, 4, 3)
    m = c5[:, :, :, :, None, :] * _s5_delta().transpose(0, 2, 1)[None, :, :, None, :, None]
    return m.reshape(S5_NCB, S5_NH, S5_NS, 128)


def _s5_unpack_c(dm):
    gh = 8 // S5_NH
    d6 = dm.reshape(S5_NCB, S5_NH, gh, SSM_STATE, 8, SSM_GROUP)
    c5 = jnp.sum(d6 * _s5_delta().transpose(0, 2, 1)[None, :, :, None, :, None], axis=4)
    return c5.transpose(0, 1, 2, 4, 3).reshape(SSM_GROUPS, SSM_GROUP, SSM_STATE)


def _s5_pack_lam(x):
    return x.reshape(S5_NCB, S5_NH, S5_NS)


def _permute_rows(x):
    T = x.shape[0]
    return x.reshape(SCAN_LANES, T // SCAN_LANES, -1).transpose(1, 0, 2).reshape(T, -1)


def _unpermute_rows(x):
    T = x.shape[0]
    return x.reshape(T // SCAN_LANES, SCAN_LANES, -1).transpose(1, 0, 2).reshape(T, -1)


ATT_TB = ATT_ROWS * GRID_W
ATT_KB = 3 * ATT_TB


def _att_valid(i, n_rows):
    qi, kj = np.meshgrid(np.arange(ATT_TB), np.arange(ATT_KB), indexing="ij")
    r = i * ATT_ROWS + qi // GRID_W
    c = qi % GRID_W
    rk = (i - 1) * ATT_ROWS + kj // GRID_W
    x = kj % GRID_W
    rs = np.clip(r - WIN_H // 2, 0, n_rows - WIN_H)
    cs = np.clip(c - WIN_W // 2, 0, GRID_W - WIN_W)
    return (rk >= rs) & (rk < rs + WIN_H) & (x >= cs) & (x < cs + WIN_W)


def _att_masked_tables(table, n_rows):
    n = n_rows // ATT_ROWS
    assert n >= 3
    masks = np.stack([_att_valid(i, n_rows) for i in (0, 1, n - 1)])
    return jnp.where(masks[:, None], table[None], NEG_INF)


def _att_variant(i, n):
    return jnp.where(i == 0, 0, jnp.where(i >= n - 1, 2, 1))


def _att_exp(qh, kh, bias):
    s = _dot_nt(qh, kh) + bias
    return jnp.exp(s - jnp.max(s, axis=1, keepdims=True))


def _att_values_and_ones(vh):
    return jnp.concatenate([vh, jnp.ones_like(vh)], axis=1)


def _att_specs(n, col):
    last = n - 1
    cur = lambda i: (jnp.minimum(i, last), col)
    prv = lambda i: (jnp.maximum(jnp.minimum(i, last) - 1, 0), col)
    nxt = lambda i: (jnp.minimum(i + 1, last), col)
    blk = lambda f: pl.BlockSpec((ATT_TB, ATT_WIDTH), f)
    return blk(cur), blk(prv), blk(nxt)


def _att_fwd(zb, biasv):
    T = zb.shape[0]
    W = ATT_WIDTH
    n = T // ATT_TB
    n_rows = T // GRID_W
    cur = _att_specs(n, 0)[0]
    q_cur = _att_specs(n, 1)[0]
    k_cur, k_prv, k_nxt = _att_specs(n, 2)
    v_cur, v_prv, v_nxt = _att_specs(n, 3)

    def body(q_ref, kp_ref, kc_ref, kn_ref, vp_ref, vc_ref, vn_ref, b_ref, y_ref):
        qs = q_ref[...] * 0.125
        kb = jnp.concatenate([kp_ref[...], kc_ref[...], kn_ref[...]], axis=0)
        vb = jnp.concatenate([vp_ref[...], vc_ref[...], vn_ref[...]], axis=0)
        outs = []
        for h in range(ATT_HEADS):
            hs = slice(h * ATT_HEAD_DIM, (h + 1) * ATT_HEAD_DIM)
            e = _att_exp(qs[:, hs], kb[:, hs], b_ref[h]).astype(BF16)
            ov = _dot(e, _att_values_and_ones(vb[:, hs]))
            outs.append(ov[:, :ATT_HEAD_DIM] * (1.0 / ov[:, ATT_HEAD_DIM:ATT_HEAD_DIM + 1]))
        y_ref[...] = jnp.concatenate(outs, axis=1).astype(BF16)

    return pl.pallas_call(
        body, name="att_fwd", grid=(n,),
        in_specs=[q_cur, k_prv, k_cur, k_nxt, v_prv, v_cur, v_nxt,
                  pl.BlockSpec((None, ATT_HEADS, ATT_TB, ATT_KB), lambda i: (_att_variant(i, n), 0, 0, 0))],
        out_specs=cur,
        out_shape=jax.ShapeDtypeStruct((T, W), BF16),
        compiler_params=_cparams(("parallel",), VMEM_LIMIT),
    )(zb, zb, zb, zb, zb, zb, zb, biasv)


def _att_bwd(zb, y, do, biasv, comm=None):
    T = zb.shape[0]
    W = ATT_WIDTH
    n = T // ATT_TB
    n_rows = T // GRID_W
    cur = _att_specs(n, 0)[0]
    q_cur = _att_specs(n, 1)[0]
    k_cur, k_prv, k_nxt = _att_specs(n, 2)
    v_cur, v_prv, v_nxt = _att_specs(n, 3)
    done = pl.BlockSpec((ATT_TB, W), lambda i: (jnp.maximum(i - 1, 0), 0))
    bias_spec = pl.BlockSpec((None, ATT_HEADS, ATT_TB, ATT_KB), lambda i: (_att_variant(i, n), 0, 0, 0))

    def body(q_ref, y_ref, do_ref, kp_ref, kc_ref, kn_ref, vp_ref, vc_ref, vn_ref, b_ref,
             dq_ref, dk_ref, dv_ref, db_ref, acck_ref, accv_ref):
        i = pl.program_id(0)

        @pl.when(i == 0)
        def _():
            db_ref[...] = jnp.zeros_like(db_ref)
            acck_ref[...] = jnp.zeros_like(acck_ref)
            accv_ref[...] = jnp.zeros_like(accv_ref)

        @pl.when((i > 0) & (i < n))
        def _():
            slot = lax.rem(i + 1, 3)
            acck_ref[slot] = jnp.zeros((ATT_TB, W), F32)
            accv_ref[slot] = jnp.zeros((ATT_TB, W), F32)

        @pl.when(i < n)
        def _():
            qs = q_ref[...] * 0.125
            dob = do_ref[...]
            dy = dob.astype(F32) * y_ref[...].astype(F32)
            kb = jnp.concatenate([kp_ref[...], kc_ref[...], kn_ref[...]], axis=0)
            vb = jnp.concatenate([vp_ref[...], vc_ref[...], vn_ref[...]], axis=0)
            dqs, dks, dvs = [], [], []
            for h in range(ATT_HEADS):
                hs = slice(h * ATT_HEAD_DIM, (h + 1) * ATT_HEAD_DIM)
                qh, kh, vh, doh = qs[:, hs], kb[:, hs], vb[:, hs], dob[:, hs]
                e = _att_exp(qh, kh, b_ref[h])
                p = e * (1.0 / jnp.sum(e, axis=1, keepdims=True))
                dp = _dot_nt(doh, vh)
                ds = p * (dp - jnp.sum(dy[:, hs], axis=1, keepdims=True))
                db_ref[h] += ds
                dsb = ds.astype(BF16)
                dqs.append(_dot(dsb, kh) * 0.125)
                dks.append(_dot_tn(dsb, qh))
                dvs.append(_dot_tn(p.astype(BF16), doh))
            dq_ref[...] = jnp.concatenate(dqs, axis=1).astype(BF16)
            dk_all = jnp.concatenate(dks, axis=1)
            dv_all = jnp.concatenate(dvs, axis=1)
            for b in range(3):
                slot = lax.rem(i + 2 + b, 3)
                rows = slice(b * ATT_TB, (b + 1) * ATT_TB)
                acck_ref[slot] += dk_all[rows]
                accv_ref[slot] += dv_all[rows]

        slot = lax.rem(i + 2, 3)
        dk_ref[...] = acck_ref[slot].astype(BF16)
        dv_ref[...] = accv_ref[slot].astype(BF16)

    return _pallas(
        body, "att_bwd", (n + 1,),
        [q_cur, cur, cur, k_prv, k_cur, k_nxt, v_prv, v_cur, v_nxt, bias_spec],
        [cur, done, done, pl.BlockSpec((ATT_HEADS, ATT_TB, ATT_KB), lambda i: (0, 0, 0))],
        [jax.ShapeDtypeStruct((T, W), BF16)] * 3 + [jax.ShapeDtypeStruct((ATT_HEADS, ATT_TB, ATT_KB), F32)],
        [pltpu.VMEM((3, ATT_TB, W), F32), pltpu.VMEM((3, ATT_TB, W), F32)],
        ("arbitrary",), (zb, y, do, zb, zb, zb, zb, zb, zb, biasv), comm)


def _att_selectors():
    rsel = np.zeros((ATT_ROWS, 3 * ATT_ROWS, 2 * WIN_H - 1), np.float32)
    for a in range(ATT_ROWS):
        for b in range(3 * ATT_ROWS):
            rsel[a, b, b - a - ATT_ROWS + WIN_H - 1] = 1.0
    csel = np.zeros((GRID_W, GRID_W, 2 * WIN_W - 1), np.float32)
    for c in range(GRID_W):
        for x in range(GRID_W):
            csel[c, x, min(max(x - c, -(WIN_W - 1)), WIN_W - 1) + WIN_W - 1] = 1.0
    return rsel, csel


def _att_bias_table(rpb):
    rsel, csel = _att_selectors()
    hi = lax.Precision.HIGHEST
    t = jnp.einsum('hrd,abr->habd', rpb, rsel, precision=hi)
    t = jnp.einsum('habd,cxd->hacbx', t, csel, precision=hi)
    return t.reshape(ATT_HEADS, ATT_TB, ATT_KB)


def _att_bias_table_t(dtable):
    rsel, csel = _att_selectors()
    hi = lax.Precision.HIGHEST
    t = dtable.reshape(ATT_HEADS, ATT_ROWS, GRID_W, 3 * ATT_ROWS, GRID_W)
    t = jnp.einsum('hacbx,cxd->habd', t, csel, precision=hi)
    return jnp.einsum('habd,abr->hrd', t, rsel, precision=hi)


GELU_K = math.sqrt(2.0 / math.pi)
GELU_C = 0.044715
MERGE_TM = 256


def _gelu(x):
    return 0.5 * x * (1.0 + jnp.tanh(GELU_K * (x + GELU_C * x * x * x)))


def _gelu_grad(x):
    t = jnp.tanh(GELU_K * (x + GELU_C * x * x * x))
    return 0.5 * (1.0 + t) + 0.5 * x * (1.0 - t * t) * GELU_K * (1.0 + 3.0 * GELU_C * x * x)


def _merge_forward(ypre, zs, gs, ga, ya, ssm_d, w_glu, b_glu, w_bs, w_ba):
    zs, gs, ga = zs.astype(F32), gs.astype(F32), ga.astype(F32)
    ys = ypre + ssm_d * zs
    yg = _gelu(ys)
    sg = jax.nn.sigmoid(_dot(yg.astype(BF16), w_glu) + b_glu)
    y2 = yg * sg
    bs = _dot(y2.astype(BF16), w_bs)
    ba = _dot(ya, w_ba)
    s1 = jax.nn.sigmoid(gs)
    s2 = jax.nn.sigmoid(ga)
    merged = s1 * bs + s2 * ba
    return ys, yg, sg, y2, bs, ba, s1, s2, merged


def _merge_in_specs(D, W, tm):
    tok = lambda w, c: pl.BlockSpec((tm, w), lambda i: (i, c))
    full = lambda r, c: pl.BlockSpec((r, c), lambda i: (0, 0))
    z_specs = [tok(W, 0), tok(D, 4 * W // D), tok(D, 4 * W // D + 1)]
    w_specs = [full(1, W), full(W, W), full(1, W), full(W, D), full(W, D), full(D, D)]
    return tok, z_specs, w_specs


def _merge_fwd(ypre, z, ya, h1, ssm_d, w_glu, b_glu, w_bs, w_ba, w_out):
    T, D = h1.shape
    W = ypre.shape[1]
    tm = min(T, 2 * MERGE_TM)
    tok, z_specs, w_specs = _merge_in_specs(D, W, tm)

    def body(ypre_ref, zs_ref, gs_ref, ga_ref, ya_ref, h1_ref, d_ref, wglu_ref, bglu_ref, wbs_ref, wba_ref, wout_ref,
             h2_ref):
        merged = _merge_forward(ypre_ref[...], zs_ref[...], gs_ref[...], ga_ref[...], ya_ref[...], d_ref[...],
                                wglu_ref[...], bglu_ref[...], wbs_ref[...], wba_ref[...])[-1]
        h2_ref[...] = h1_ref[...] + _dot(merged.astype(BF16), wout_ref[...])

    return pl.pallas_call(
        body, name="merge_fwd", grid=(T // tm,),
        in_specs=[tok(W, 0)] + z_specs + [tok(W, 0), tok(D, 0)] + w_specs,
        out_specs=tok(D, 0),
        out_shape=jax.ShapeDtypeStruct((T, D), F32),
        compiler_params=_cparams(("parallel",), VMEM_LIMIT),
    )(ypre, z, z, z, ya, h1, ssm_d, w_glu, b_glu, w_bs, w_ba, w_out)


def _merge_bwd(dh2, ypre, z, ya, ssm_d, w_glu, b_glu, w_bs, w_ba, w_out):
    T, D = dh2.shape
    W = ypre.shape[1]
    tm = min(T, MERGE_TM)
    tok, z_specs, w_specs = _merge_in_specs(D, W, tm)

    def body(dh2_ref, ypre_ref, zs_ref, gs_ref, ga_ref, ya_ref, d_ref, wglu_ref, bglu_ref, wbs_ref, wba_ref, wout_ref,
             dypre_ref, dzs_ref, dgs_ref, dga_ref, dya_ref, dd_ref, dwglu_ref, dbglu_ref, dwbs_ref, dwba_ref, dwout_ref):
        @pl.when(pl.program_id(0) == 0)
        def _():
            for r in (dd_ref, dwglu_ref, dbglu_ref, dwbs_ref, dwba_ref, dwout_ref):
                r[...] = jnp.zeros_like(r)

        zs = zs_ref[...].astype(F32)
        ya = ya_ref[...]
        ys, yg, sg, y2, bs, ba, s1, s2, merged = _merge_forward(
            ypre_ref[...], zs, gs_ref[...], ga_ref[...], ya, d_ref[...],
            wglu_ref[...], bglu_ref[...], wbs_ref[...], wba_ref[...])
        dh2b = dh2_ref[...].astype(BF16)
        dmerged = _dot_nt(dh2b, wout_ref[...])
        dwout_ref[...] += _dot_tn(merged.astype(BF16), dh2b)
        dbs = (dmerged * s1).astype(BF16)
        dba = (dmerged * s2).astype(BF16)
        dgs_ref[...] = (dmerged * bs * s1 * (1.0 - s1)).astype(BF16)
        dga_ref[...] = (dmerged * ba * s2 * (1.0 - s2)).astype(BF16)
        dwbs_ref[...] += _dot_tn(y2.astype(BF16), dbs)
        dwba_ref[...] += _dot_tn(ya, dba)
        dya_ref[...] = _dot_nt(dba, wba_ref[...]).astype(BF16)
        dy2 = _dot_nt(dbs, wbs_ref[...])
        dvv = dy2 * yg * sg * (1.0 - sg)
        dvvb = dvv.astype(BF16)
        dyg = dy2 * sg + _dot_nt(dvvb, wglu_ref[...])
        dwglu_ref[...] += _dot_tn(yg.astype(BF16), dvvb)
        dbglu_ref[...] += _col_sum(dvv)
        dys = dyg * _gelu_grad(ys)
        dd_ref[...] += _col_sum(dys * zs)
        dzs_ref[...] = dys * d_ref[...]
        dypre_ref[...] = dys.astype(BF16)

    f32 = lambda *s: jax.ShapeDtypeStruct(s, F32)
    b16 = lambda *s: jax.ShapeDtypeStruct(s, BF16)
    return pl.pallas_call(
        body, name="merge_bwd", grid=(T // tm,),
        in_specs=[tok(D, 0), tok(W, 0)] + z_specs + [tok(W, 0)] + w_specs,
        out_specs=[tok(W, 0), tok(W, 0), tok(D, 0), tok(D, 0), tok(W, 0)] + w_specs,
        out_shape=[b16(T, W), f32(T, W), b16(T, D), b16(T, D), b16(T, W),
                   f32(1, W), f32(W, W), f32(1, W), f32(W, D), f32(W, D), f32(D, D)],
        compiler_params=_cparams(("arbitrary",), VMEM_LIMIT),
    )(dh2, ypre, z, z, z, ya, ssm_d, w_glu, b_glu, w_bs, w_ba, w_out)


def _cast_shards(weights):
    def body(*refs):
        n = len(refs) // 2
        for src, dst in zip(refs[:n], refs[n:]):
            dst[...] = src[0].astype(BF16)

    return pl.pallas_call(
        body, name="cast_shards",
        out_shape=[jax.ShapeDtypeStruct(w.shape[1:], BF16) for w in weights],
        compiler_params=_cparams(None, VMEM_LIMIT))(*weights)


def _gather_two_level(shards, name):
    n = len(shards)

    def body(*refs):
        x_refs, out_refs = refs[:n], refs[n:2 * n]
        send_sems, recv_sems, local_sems = refs[2 * n:]
        x, y, c = _my_place()
        me, sibling = (x, y, c), (x, y, 1 - c)
        chips = [(1 - x, y), (x, 1 - y), (1 - x, 1 - y)]

        def copy(a, k, block, to, own=False):
            slot = out_refs[a].at[_flat(*block)]
            return pltpu.make_async_remote_copy(
                src_ref=x_refs[a] if own else slot, dst_ref=slot,
                send_sem=send_sems.at[7 * a + k], recv_sem=recv_sems.at[7 * a + k],
                device_id=to, device_id_type=MESH_ID)

        sent, local = [], []
        for a in range(n):
            local.append(pltpu.make_async_copy(x_refs[a], out_refs[a].at[_flat(*me)], local_sems.at[a]))
            local[-1].start()
            sent.append(copy(a, 0, me, sibling, own=True))
            sent += [copy(a, 1 + j, me, (*chip, c), own=True) for j, chip in enumerate(chips)]
        for cp in sent:
            cp.start()
        for a in range(n):
            for j, chip in enumerate(chips):
                copy(a, 1 + j, (*chip, c), me).wait_recv()
                sent.append(copy(a, 4 + j, (*chip, c), sibling))
                sent[-1].start()
        for a in range(n):
            copy(a, 0, sibling, me).wait_recv()
            for j, chip in enumerate(chips):
                copy(a, 4 + j, (*chip, 1 - c), me).wait_recv()
        for cp in sent:
            cp.wait_send()
        for cp in local:
            cp.wait()

    return pl.pallas_call(
        body, name=name, in_specs=[_HBM] * n, out_specs=[_HBM] * n,
        out_shape=[jax.ShapeDtypeStruct((N_DEV,) + s.shape, s.dtype) for s in shards],
        scratch_shapes=[pltpu.SemaphoreType.DMA((7 * n,)), pltpu.SemaphoreType.DMA((7 * n,)),
                        pltpu.SemaphoreType.DMA((n,))],
    )(*shards)


PACK_COLS = 1024
BIG = (("ffn1_w_gate", 1), ("ffn1_w_up", 1), ("ffn1_w_down", 0), ("w_in", 1), ("ssm_w_glu", 0),
       ("w_branch_ssm", 1), ("w_branch_att", 1), ("w_out", 0),
       ("ffn2_w_gate", 1), ("ffn2_w_up", 1), ("ffn2_w_down", 0))
BIG_AXIS = dict(BIG)
TRANSPOSED = ("ffn1_w_gate", "ffn1_w_up", "ffn2_w_gate", "ffn2_w_up")
SSM_DIR = ("ssm_a_re", "ssm_a_im", "ssm_log_dt", "ssm_b_re", "ssm_b_im", "ssm_c_re", "ssm_c_im")
SMALL_EARLY = (("mix_norm",) + tuple(n + "_fwd" for n in SSM_DIR) + tuple(n + "_bwd" for n in SSM_DIR)
               + ("ssm_d", "ssm_b_glu", "att_rpb", "ffn2_norm", "final_norm"))
SMALL_LATE = ("ffn1_norm",)
WEIGHTS = ("ffn1_norm", "ffn1_w_gate", "ffn1_w_up", "ffn1_w_down", "mix_norm", "w_in") \
    + tuple(n + "_fwd" for n in SSM_DIR) + tuple(n + "_bwd" for n in SSM_DIR) \
    + ("ssm_d", "ssm_w_glu", "ssm_b_glu", "att_rpb", "w_branch_ssm", "w_branch_att", "w_out",
       "ffn2_norm", "ffn2_w_gate", "ffn2_w_up", "ffn2_w_down", "final_norm")


def _pad_rows(a, mult):
    pad = (-a.shape[-2]) % mult
    if pad:
        a = jnp.concatenate([a, jnp.zeros(a.shape[:-2] + (pad, a.shape[-1]), a.dtype)], axis=-2)
    return a


def _pack(arrays, row_mult):
    flat = jnp.concatenate([a.reshape(-1) for a in arrays])
    pad = (-flat.shape[0]) % PACK_COLS
    if pad:
        flat = jnp.concatenate([flat, jnp.zeros((pad,), flat.dtype)])
    return _pad_rows(flat.reshape(-1, PACK_COLS), row_mult)


def _unpack(slab, shapes):
    flat = slab.reshape(-1)
    out, at = [], 0
    for s in shapes:
        n = int(np.prod(s))
        out.append(flat[at:at + n].reshape(s))
        at += n
    return out


def _split_for_devices(g, axis):
    r, c = g.shape
    if axis == 1:
        return g.reshape(r, N_DEV, c // N_DEV).transpose(1, 0, 2).astype(BF16)
    return g.reshape(N_DEV, r // N_DEV, c).astype(BF16)


def _join_shards(gathered, axis):
    _, r, c = gathered.shape
    if axis == 1:
        return gathered.transpose(1, 0, 2).reshape(r, N_DEV * c)
    return gathered.reshape(N_DEV * r, c)


def _s5_direction_inputs(p, sfx, chain_len):
    bt_re = p["ssm_b_re" + sfx][0].transpose(0, 2, 1)
    bt_im = p["ssm_b_im" + sfx][0].transpose(0, 2, 1)
    raw = (p["ssm_a_re" + sfx][0], p["ssm_a_im" + sfx][0], p["ssm_log_dt" + sfx][0][:, None], bt_re, bt_im)
    lr, li, sr, si, bbr, bbi = _disc_fwd(*raw, chain_len,"s5_disc" + sfx)
    lam = jnp.stack([_s5_pack_lam(t) for t in (lr, li, sr, si)], axis=2)
    mats = (_s5_pack_b(bbr), _s5_pack_b(bbi), lam,
            _s5_pack_c(p["ssm_c_re" + sfx][0]), _s5_pack_c(-p["ssm_c_im" + sfx][0]))
    return raw, mats


def kernel(x, ffn1_norm, ffn1_w_gate, ffn1_w_up, ffn1_w_down, mix_norm, w_in, ssm_a_re_fwd, ssm_a_im_fwd, ssm_log_dt_fwd, ssm_b_re_fwd, ssm_b_im_fwd, ssm_c_re_fwd, ssm_c_im_fwd, ssm_a_re_bwd, ssm_a_im_bwd, ssm_log_dt_bwd, ssm_b_re_bwd, ssm_b_im_bwd, ssm_c_re_bwd, ssm_c_im_bwd, ssm_d, ssm_w_glu, ssm_b_glu, att_rpb, w_branch_ssm, w_branch_att, w_out, ffn2_norm, ffn2_w_gate, ffn2_w_up, ffn2_w_down, final_norm, loss_target, m_ffn1_norm, m_ffn1_w_gate, m_ffn1_w_up, m_ffn1_w_down, m_mix_norm, m_w_in, m_ssm_a_re_fwd, m_ssm_a_im_fwd, m_ssm_log_dt_fwd, m_ssm_b_re_fwd, m_ssm_b_im_fwd, m_ssm_c_re_fwd, m_ssm_c_im_fwd, m_ssm_a_re_bwd, m_ssm_a_im_bwd, m_ssm_log_dt_bwd, m_ssm_b_re_bwd, m_ssm_b_im_bwd, m_ssm_c_re_bwd, m_ssm_c_im_bwd, m_ssm_d, m_ssm_w_glu, m_ssm_b_glu, m_att_rpb, m_w_branch_ssm, m_w_branch_att, m_w_out, m_ffn2_norm, m_ffn2_w_gate, m_ffn2_w_up, m_ffn2_w_down, m_final_norm, v_ffn1_norm, v_ffn1_w_gate, v_ffn1_w_up, v_ffn1_w_down, v_mix_norm, v_w_in, v_ssm_a_re_fwd, v_ssm_a_im_fwd, v_ssm_log_dt_fwd, v_ssm_b_re_fwd, v_ssm_b_im_fwd, v_ssm_c_re_fwd, v_ssm_c_im_fwd, v_ssm_a_re_bwd, v_ssm_a_im_bwd, v_ssm_log_dt_bwd, v_ssm_b_re_bwd, v_ssm_b_im_bwd, v_ssm_c_re_bwd, v_ssm_c_im_bwd, v_ssm_d, v_ssm_w_glu, v_ssm_b_glu, v_att_rpb, v_w_branch_ssm, v_w_branch_att, v_w_out, v_ffn2_norm, v_ffn2_w_gate, v_ffn2_w_up, v_ffn2_w_down, v_final_norm):
    p = dict(locals())
    x = p["x"][0]
    target = p["loss_target"][0]
    T, D = x.shape

    stored = lambda a, n: jnp.swapaxes(a, -1, -2) if n in TRANSPOSED else a
    cut_axis = lambda n: 0 if n in TRANSPOSED else BIG_AXIS[n]
    shard = dict(zip([n for n, _ in BIG], _cast_shards([stored(p[n], n) for n, _ in BIG])))
    ffn1_w = ("ffn1_w_gate", "ffn1_w_up", "ffn1_w_down")
    mix_w = ("w_in", "ssm_w_glu", "w_branch_ssm", "w_branch_att", "w_out")
    ffn2_w = ("ffn2_w_gate", "ffn2_w_up", "ffn2_w_down")
    gathered = dict(zip(ffn1_w, _gather_two_level([shard[n] for n in ffn1_w], "gather_ffn1")))
    full = lambda n: _join_shards(gathered[n], cut_axis(n))

    h0 = x
    wg1, wu1, wd1 = [full(n) for n in ffn1_w]
    (h1, xn1, g1, u1), got = _ffn_fwd(h0, p["ffn1_norm"], wg1, wu1, wd1, "ffn1_fwd",
                                      _Comm("gather", [shard[n] for n in mix_w]))
    gathered.update(zip(mix_w, got))
    zb, un = _mixin_fwd(h1, p["mix_norm"], gathered["w_in"])
    W = SSM_WIDTH
    zp = _permute_rows(zb[:, :W])
    chain_len = T // SCAN_LANES // S5_NQ
    raw_f, mats_f = _s5_direction_inputs(p, "_fwd", chain_len)
    raw_b, mats_b = _s5_direction_inputs(p, "_bwd", chain_len)
    bre, bim, lam, cre, cimn = [jnp.stack([f, b]) for f, b in zip(mats_f, mats_b)]
    bre, bim, cre, cimn = [t.astype(BF16) for t in (bre, bim, cre, cimn)]
    (yp,), got = _s5_fwd(zp, bre, bim, lam, cre, cimn, _Comm("gather", [shard[n] for n in ffn2_w]))
    gathered.update(zip(ffn2_w, got))
    ypre = _unpermute_rows(yp)
    table = _att_masked_tables(_att_bias_table(p["att_rpb"][0]), T // GRID_W)
    ya = _att_fwd(zb, table)
    tail_w = (p["ssm_d"], full("ssm_w_glu"), p["ssm_b_glu"], full("w_branch_ssm"), full("w_branch_att"), full("w_out"))
    h2 = _merge_fwd(ypre, zb, ya, h1, *tail_w)
    wg2, wu2, wd2 = [full(n) for n in ffn2_w]
    (h3, xn2, g2, u2), _ = _ffn_fwd(h2, p["ffn2_norm"], wg2, wu2, wd2, "ffn2_fwd")
    loss_part, dh3, d_final = _loss_head(h3, p["final_norm"][None], target)

    grads = {"final_norm": d_final[0]}
    to_send = lambda names: _Comm("exchange", [_split_for_devices(grads[n], cut_axis(n)) for n in names])
    parts = {}
    (dh2, grads["ffn2_norm"], do2, a2, dg2, du2), _ = _ffn_bwd(
        dh3, h2, p["ffn2_norm"], g2, u2, wg2, wu2, wd2, "ffn2_bwd")
    grads["ffn2_w_gate"] = _xty(dg2, xn2, "ffn2_dw_gate")
    grads["ffn2_w_up"] = _xty(du2, xn2, "ffn2_dw_up")
    grads["ffn2_w_down"] = _xty(a2, do2, "ffn2_dw_down")
    (dypre, dzs_skip, dgs, dga, dya, grads["ssm_d"], grads["ssm_w_glu"], grads["ssm_b_glu"],
     grads["w_branch_ssm"], grads["w_branch_att"], grads["w_out"]) = _merge_bwd(dh2, ypre, zb, ya, *tail_w)
    (dq, dk, dv, dtable), got = _att_bwd(zb, ya, dya, table, to_send(ffn2_w))
    parts.update(zip(ffn2_w, got))
    grads["att_rpb"] = _att_bias_table_t(dtable)
    dyp = _permute_rows(dypre)
    tail_names = ("ssm_w_glu", "w_branch_ssm", "w_branch_att", "w_out")
    (dzp, dbre, dbim, dlam, dcre, dcimn), got = _s5_bwd(zp, dyp, bre, bim, lam, cre, cimn, to_send(tail_names))
    parts.update(zip(tail_names, got))
    G, P = SSM_GROUPS, SSM_STATE
    for d, (sfx, raw) in enumerate((("_fwd", raw_f), ("_bwd", raw_b))):
        da_re, da_im, dldt, dbt_re, dbt_im = _disc_bwd(
            *raw, dlam[d, :, :, 0, :].reshape(G, P), dlam[d, :, :, 1, :].reshape(G, P),
            _s5_unpack_b(dbre[d]), _s5_unpack_b(dbim[d]), "s5_disc_grad" + sfx)
        grads["ssm_a_re" + sfx] = da_re
        grads["ssm_a_im" + sfx] = da_im
        grads["ssm_log_dt" + sfx] = dldt[:, 0]
        grads["ssm_b_re" + sfx] = dbt_re.transpose(0, 2, 1)
        grads["ssm_b_im" + sfx] = dbt_im.transpose(0, 2, 1)
        grads["ssm_c_re" + sfx] = _s5_unpack_c(dcre[d])
        grads["ssm_c_im" + sfx] = -_s5_unpack_c(dcimn[d])
    dzs = _unpermute_rows(dzp) + dzs_skip
    dz = jnp.concatenate([dzs.astype(BF16), dq, dk, dv, dgs, dga], axis=1)
    dh1, grads["mix_norm"] = _mixin_bwd(dz, dh2, h1, p["mix_norm"], gathered["w_in"])
    grads["w_in"] = _xty(un, dz, "dw_in", col_shards=N_DEV)
    small_t = lambda a, n: jnp.swapaxes(a, -1, -2) if n.startswith("ssm_b_") else a
    pack_small = lambda names, src, pre: _pack([small_t(src[pre + n], n).astype(F32) for n in names], 8)
    early = _Comm(["exchange", "gather"],
                  [grads["w_in"], pack_small(SMALL_EARLY, grads, "")])
    (dh0, grads["ffn1_norm"], do1, a1, dg1, du1), (parts["w_in"], got_early) = _ffn_bwd(
        dh1, h0, p["ffn1_norm"], g1, u1, wg1, wu1, wd1, "ffn1_bwd", early)
    grads["ffn1_w_down"] = _xty(a1, do1, "ffn1_dw_down")
    grads["ffn1_w_gate"], (parts["ffn1_w_down"],) = _xty(dg1, xn1, "ffn1_dw_gate", to_send(("ffn1_w_down",)))
    grads["ffn1_w_up"], (parts["ffn1_w_gate"],) = _xty(du1, xn1, "ffn1_dw_up", to_send(("ffn1_w_gate",)))
    last = _Comm(["exchange", "gather"],
                 [_split_for_devices(grads["ffn1_w_up"], 0), pack_small(SMALL_LATE, grads, "")])
    parts["ffn1_w_up"], got_late = _comm_call(last, "exchange_last")
    got_small = jnp.concatenate([got_early, got_late], axis=1)

    results = {}
    for n, _ in BIG:
        outs = _adamw(parts[n], *[stored(p[pre + n][0], n) for pre in ("", "m_", "v_")], "adamw_" + n)
        results[n] = [stored(o, n)[None] for o in outs]
    early_rows = got_early.shape[1]
    slab = lambda pre: jnp.concatenate([pack_small(SMALL_EARLY, p, pre), pack_small(SMALL_LATE, p, pre)], axis=0)
    small_out = _adamw(got_small, slab(""), slab("m_"), slab("v_"), "adamw_small")
    for names, rows in ((SMALL_EARLY, slice(0, early_rows)), (SMALL_LATE, slice(early_rows, None))):
        shapes = [small_t(p[n], n).shape for n in names]
        for n, vals in zip(names, zip(*[_unpack(out[rows], shapes) for out in small_out])):
            results[n] = [small_t(val, n) for val in vals]

    loss = lax.psum(loss_part[0, 0], ("x", "y", "c"))
    out = [loss, dh0[None]]
    for kind in range(4):
        out += [results[n][kind] for n in WEIGHTS]
    return tuple(out)
```

```python
import functools
import math

import numpy as np
import jax
import jax.numpy as jnp
from jax import lax
from jax.experimental import pallas as pl
from jax.experimental.pallas import tpu as pltpu

F32 = jnp.float32
BF16 = jnp.bfloat16
MESH_ID = pl.DeviceIdType.MESH

SSM_GROUP = 16
SSM_GROUPS = 32
SSM_STATE = 64
SSM_WIDTH = 512
ATT_HEADS = 8
ATT_HEAD_DIM = 64
ATT_WIDTH = 512
GRID_W = 64
WIN_H = 8
WIN_W = 16
EPS = 1e-6
NEG_INF = -1e30
ADAM_LR = 0.001
ADAM_B1 = 0.9
ADAM_B2 = 0.999
ADAM_EPS = 1e-08
ADAM_WD = 0.01
ADAM_STEP = 10

N_DEV = 8
V7X_VMEM_BYTES = 64 * 1024 * 1024
VMEM_LIMIT = V7X_VMEM_BYTES - 8 * 1024 * 1024
SCAN_LANES = 8
ATT_ROWS = 4


def _cparams(sem, vmem=None):
    return pltpu.CompilerParams(dimension_semantics=sem, vmem_limit_bytes=vmem)


def _dot(a, b):
    return jnp.dot(a, b, preferred_element_type=F32)


def _dot_nt(a, b):
    return lax.dot_general(a, b, (((1,), (1,)), ((), ())), preferred_element_type=F32)


def _dot_tn(a, b):
    return lax.dot_general(a, b, (((0,), (0,)), ((), ())), preferred_element_type=F32)


def _rms(h):
    return lax.rsqrt(jnp.mean(h * h, axis=-1, keepdims=True) + EPS)


def _rms_bwd(h, r, v):
    return r * v - h * (r * r * r) * jnp.mean(h * v, axis=-1, keepdims=True)


def _col_sum(x):
    return jnp.sum(x, axis=0, keepdims=True)


def _my_place():
    return lax.axis_index("x"), lax.axis_index("y"), lax.axis_index("c")


def _flat(px, py, pc):
    return 4 * px + 2 * py + pc


class _Comm:
    def __init__(self, kind, arrays):
        self.arrays = list(arrays)
        self.n = len(self.arrays)
        self.kinds = [kind] * self.n if isinstance(kind, str) else list(kind)

    def out_shapes(self):
        return [jax.ShapeDtypeStruct((N_DEV,) + a.shape if k == "gather" else a.shape, a.dtype)
                for k, a in zip(self.kinds, self.arrays)]

    def scratch(self):
        return [pltpu.SemaphoreType.DMA((7 * self.n,)), pltpu.SemaphoreType.DMA((7 * self.n,)),
                pltpu.SemaphoreType.DMA((self.n,))]

    def run(self, srcs, dsts, sems, start):
        send_sems, recv_sems, local_sems = sems
        x, y, c = _my_place()
        mine = _flat(x, y, c)
        for a, (src, dst) in enumerate(zip(srcs, dsts)):
            whole = self.kinds[a] == "gather"
            local = pltpu.make_async_copy(src if whole else src.at[mine], dst.at[mine], local_sems.at[a])
            local.start() if start else local.wait()
            for k in range(1, N_DEV):
                px = 1 - x if k & 4 else x
                py = 1 - y if k & 2 else y
                pc = 1 - c if k & 1 else c
                cp = pltpu.make_async_remote_copy(
                    src_ref=src if whole else src.at[_flat(px, py, pc)], dst_ref=dst.at[mine],
                    send_sem=send_sems.at[7 * a + k - 1], recv_sem=recv_sems.at[7 * a + k - 1],
                    device_id=(px, py, pc), device_id_type=MESH_ID)
                cp.start() if start else cp.wait()


_HBM = pl.BlockSpec(memory_space=pltpu.HBM)


def _comm_call(comm, name):
    def body(*refs):
        srcs, dsts, sems = refs[:comm.n], refs[comm.n:2 * comm.n], refs[2 * comm.n:]
        comm.run(srcs, dsts, sems, True)
        comm.run(srcs, dsts, sems, False)

    return pl.pallas_call(body, name=name, in_specs=[_HBM] * comm.n, out_specs=[_HBM] * comm.n,
                          out_shape=comm.out_shapes(), scratch_shapes=comm.scratch())(*comm.arrays)


def _pallas(core, name, grid, in_specs, out_specs, out_shape, scratch, sem, args, comm=None):
    if comm is None:
        out = pl.pallas_call(core, name=name, grid=grid, in_specs=in_specs, out_specs=out_specs,
                             out_shape=out_shape, scratch_shapes=scratch,
                             compiler_params=_cparams(sem, VMEM_LIMIT))(*args)
        return out, []
    n_in, n_out, n_scr, n = len(in_specs), len(out_specs), len(scratch), comm.n

    def body(*refs):
        ins, srcs = refs[:n_in], refs[n_in:n_in + n]
        outs, dsts = refs[n_in + n:n_in + n + n_out], refs[n_in + n + n_out:n_in + 2 * n + n_out]
        scr, sems = refs[n_in + 2 * n + n_out:n_in + 2 * n + n_out + n_scr], refs[n_in + 2 * n + n_out + n_scr:]
        ids = [pl.program_id(k) for k in range(len(grid))]
        first = functools.reduce(lambda a, b: a & b, [i == 0 for i in ids])
        last = functools.reduce(lambda a, b: a & b, [i == g - 1 for i, g in zip(ids, grid)])

        @pl.when(first)
        def _():
            comm.run(srcs, dsts, sems, True)

        core(*ins, *outs, *scr)

        @pl.when(last)
        def _():
            comm.run(srcs, dsts, sems, False)

    out = pl.pallas_call(
        body, name=name, grid=grid, in_specs=list(in_specs) + [_HBM] * n, out_specs=list(out_specs) + [_HBM] * n,
        out_shape=list(out_shape) + comm.out_shapes(), scratch_shapes=list(scratch) + comm.scratch(),
        compiler_params=_cparams(("arbitrary",) * len(grid), VMEM_LIMIT))(*args, *comm.arrays)
    return out[:n_out], out[n_out:]


FFN_TM = 256


def _ffn_fwd(h, gain, wg, wu, wd, name, comm=None):
    T, D = h.shape
    F = wg.shape[0]
    tm = min(T, FFN_TM)
    once = pl.Buffered(1)

    def body(h_ref, gain_ref, wg_ref, wu_ref, wd_ref, ho_ref, xn_ref, g_ref, u_ref):
        hh = h_ref[...]
        xn = (hh * _rms(hh) * gain_ref[...]).astype(BF16)
        xn_ref[...] = xn
        g = _dot_nt(xn, wg_ref[...])
        u = _dot_nt(xn, wu_ref[...])
        g_ref[...] = g.astype(BF16)
        u_ref[...] = u.astype(BF16)
        a = (g * jax.nn.sigmoid(g) * u).astype(BF16)
        ho_ref[...] = hh + 0.5 * _dot(a, wd_ref[...])

    return _pallas(
        body, name, (T // tm,),
        [pl.BlockSpec((tm, D), lambda i: (i, 0)),
         pl.BlockSpec((1, D), lambda i: (0, 0)),
         pl.BlockSpec((F, D), lambda i: (0, 0), pipeline_mode=once),
         pl.BlockSpec((F, D), lambda i: (0, 0), pipeline_mode=once),
         pl.BlockSpec((F, D), lambda i: (0, 0), pipeline_mode=once)],
        [pl.BlockSpec((tm, D), lambda i: (i, 0)),
         pl.BlockSpec((tm, D), lambda i: (i, 0)),
         pl.BlockSpec((tm, F), lambda i: (i, 0)),
         pl.BlockSpec((tm, F), lambda i: (i, 0))],
        [jax.ShapeDtypeStruct((T, D), F32), jax.ShapeDtypeStruct((T, D), BF16),
         jax.ShapeDtypeStruct((T, F), BF16), jax.ShapeDtypeStruct((T, F), BF16)],
        [], ("parallel",), (h, gain, wg, wu, wd), comm)


def _ffn_bwd(dho, h, gain, g, u, wg, wu, wd, name, comm=None):
    T, D = h.shape
    F = wg.shape[0]
    tm = min(T, FFN_TM)
    tf = 1408 if F % 1408 == 0 else F
    once = pl.Buffered(1)

    def body(dho_ref, h_ref, gain_ref, g_ref, u_ref, wg_ref, wu_ref, wd_ref,
             dh_ref, dgain_ref, do_ref, a_ref, dg_ref, du_ref):
        @pl.when(pl.program_id(0) == 0)
        def _():
            dgain_ref[...] = jnp.zeros_like(dgain_ref)

        dho_v = dho_ref[...]
        do = (0.5 * dho_v).astype(BF16)
        do_ref[...] = do
        dxn = None
        for c in range(F // tf):
            cs = slice(c * tf, (c + 1) * tf)
            da = _dot_nt(do, wd_ref[cs, :])
            gg = g_ref[:, cs].astype(F32)
            uu = u_ref[:, cs].astype(F32)
            s = jax.nn.sigmoid(gg)
            sl = gg * s
            a_ref[:, cs] = (sl * uu).astype(BF16)
            dg = (da * uu * (s * (1.0 + gg * (1.0 - s)))).astype(BF16)
            du = (da * sl).astype(BF16)
            dg_ref[:, cs] = dg
            du_ref[:, cs] = du
            part = _dot(dg, wg_ref[cs, :]) + _dot(du, wu_ref[cs, :])
            dxn = part if dxn is None else dxn + part
        hh = h_ref[...]
        r = _rms(hh)
        dgain_ref[...] += _col_sum(dxn * hh * r)
        dh_ref[...] = dho_v + _rms_bwd(hh, r, dxn * gain_ref[...])

    tok = lambda w: pl.BlockSpec((tm, w), lambda i: (i, 0))
    row = pl.BlockSpec((1, D), lambda i: (0, 0))
    weight = pl.BlockSpec((F, D), lambda i: (0, 0), pipeline_mode=once)
    return _pallas(
        body, name, (T // tm,),
        [tok(D), tok(D), row, tok(F), tok(F), weight, weight, weight],
        [tok(D), row, tok(D), tok(F), tok(F), tok(F)],
        [jax.ShapeDtypeStruct((T, D), F32), jax.ShapeDtypeStruct((1, D), F32),
         jax.ShapeDtypeStruct((T, D), BF16), jax.ShapeDtypeStruct((T, F), BF16),
         jax.ShapeDtypeStruct((T, F), BF16), jax.ShapeDtypeStruct((T, F), BF16)],
        [], ("arbitrary",), (dho, h, gain, g, u, wg, wu, wd), comm)


def _xty(x, y, name, comm=None, col_shards=1):
    T, K = x.shape
    N = y.shape[1]
    tt = min(T, 2048)
    tk = K if K <= 1024 else (1408 if K % 1408 == 0 else K)
    tn = N if N <= 1024 else (1408 if N % 1408 == 0 else (1024 if N % 1024 == 0 else N))
    nt = T // tt
    ws = N // col_shards
    per = tn // ws if col_shards > 1 else 1
    assert col_shards == 1 or (tn % ws == 0 and ws % 128 == 0)

    def body(x_ref, y_ref, o_ref, acc_ref):
        t = pl.program_id(2)

        @pl.when(t == 0)
        def _():
            acc_ref[...] = jnp.zeros_like(acc_ref)

        acc_ref[...] += _dot_tn(x_ref[...], y_ref[...])

        @pl.when(t == nt - 1)
        def _():
            if col_shards == 1:
                o_ref[...] = acc_ref[...].astype(BF16)
            else:
                for s in range(per):
                    o_ref[s] = acc_ref[:, s * ws:(s + 1) * ws].astype(BF16)

    if col_shards == 1:
        out_spec = pl.BlockSpec((tk, tn), lambda k, n, t: (k, n))
        out_shape = jax.ShapeDtypeStruct((K, N), BF16)
    else:
        out_spec = pl.BlockSpec((per, tk, ws), lambda k, n, t: (n, k, 0))
        out_shape = jax.ShapeDtypeStruct((col_shards, K, ws), BF16)
    (out,), got = _pallas(
        body, name, (K // tk, N // tn, nt),
        [pl.BlockSpec((tt, tk), lambda k, n, t: (t, k)), pl.BlockSpec((tt, tn), lambda k, n, t: (t, n))],
        [out_spec], [out_shape], [pltpu.VMEM((tk, tn), F32)],
        ("parallel", "parallel", "arbitrary"), (x, y), comm)
    return out if comm is None else (out, got)


def _mixin_fwd(h, gain, w_in):
    T, D = h.shape
    nn, _, tn = w_in.shape
    N = nn * tn
    tm = min(T, 512)

    def body(h_ref, gain_ref, w_ref, zb_ref, un_ref):
        hh = h_ref[...]
        un = (hh * _rms(hh) * gain_ref[...]).astype(BF16)
        un_ref[...] = un
        for s in range(nn):
            zb_ref[:, s * tn:(s + 1) * tn] = _dot(un, w_ref[s]).astype(BF16)

    return pl.pallas_call(
        body, name="mixin_fwd", grid=(T // tm,),
        in_specs=[pl.BlockSpec((tm, D), lambda i: (i, 0)),
                  pl.BlockSpec((1, D), lambda i: (0, 0)),
                  pl.BlockSpec((nn, D, tn), lambda i: (0, 0, 0))],
        out_specs=[pl.BlockSpec((tm, N), lambda i: (i, 0)),
                   pl.BlockSpec((tm, D), lambda i: (i, 0))],
        out_shape=[jax.ShapeDtypeStruct((T, N), BF16), jax.ShapeDtypeStruct((T, D), BF16)],
        compiler_params=_cparams(("parallel",), VMEM_LIMIT),
    )(h, gain, w_in)


def _mixin_bwd(dz, dh_res, h, gain, w_in):
    T, D = h.shape
    nn, _, tn = w_in.shape
    tm = min(T, 512)

    def body(dz_ref, dres_ref, h_ref, gain_ref, w_ref, dh_ref, dgain_ref):
        @pl.when(pl.program_id(0) == 0)
        def _():
            dgain_ref[...] = jnp.zeros_like(dgain_ref)

        dun = _dot_nt(dz_ref[:, 0:tn], w_ref[0])
        for s in range(1, nn):
            dun = dun + _dot_nt(dz_ref[:, s * tn:(s + 1) * tn], w_ref[s])
        hh = h_ref[...]
        r = _rms(hh)
        dgain_ref[...] += _col_sum(dun * hh * r)
        dh_ref[...] = dres_ref[...] + _rms_bwd(hh, r, dun * gain_ref[...])

    return pl.pallas_call(
        body, name="mixin_bwd", grid=(T // tm,),
        in_specs=[pl.BlockSpec((tm, nn * tn), lambda i: (i, 0)),
                  pl.BlockSpec((tm, D), lambda i: (i, 0)),
                  pl.BlockSpec((tm, D), lambda i: (i, 0)),
                  pl.BlockSpec((1, D), lambda i: (0, 0)),
                  pl.BlockSpec((nn, D, tn), lambda i: (0, 0, 0))],
        out_specs=[pl.BlockSpec((tm, D), lambda i: (i, 0)),
                   pl.BlockSpec((1, D), lambda i: (0, 0))],
        out_shape=[jax.ShapeDtypeStruct((T, D), F32), jax.ShapeDtypeStruct((1, D), F32)],
        compiler_params=_cparams(("arbitrary",), VMEM_LIMIT),
    )(dz, dh_res, h, gain, w_in)


def _loss_head(h, gain, target):
    T, D = h.shape
    tm = min(T, 1024)

    def body(h_ref, gain_ref, t_ref, loss_ref, dh_ref, dgain_ref):
        @pl.when(pl.program_id(0) == 0)
        def _():
            loss_ref[...] = jnp.zeros_like(loss_ref)
            dgain_ref[...] = jnp.zeros_like(dgain_ref)

        hh = h_ref[...]
        r = _rms(hh)
        e = hh * r * gain_ref[...] - t_ref[...]
        loss_ref[...] += (0.5 / D) * jnp.sum(e * e)
        dy = e * (1.0 / D)
        dgain_ref[...] += _col_sum(dy * hh * r)
        dh_ref[...] = _rms_bwd(hh, r, dy * gain_ref[...])

    return pl.pallas_call(
        body, name="loss_head", grid=(T // tm,),
        in_specs=[pl.BlockSpec((tm, D), lambda i: (i, 0)),
                  pl.BlockSpec((1, D), lambda i: (0, 0)),
                  pl.BlockSpec((tm, D), lambda i: (i, 0))],
        out_specs=[pl.BlockSpec((1, 128), lambda i: (0, 0)),
                   pl.BlockSpec((tm, D), lambda i: (i, 0)),
                   pl.BlockSpec((1, D), lambda i: (0, 0))],
        out_shape=[jax.ShapeDtypeStruct((1, 128), F32), jax.ShapeDtypeStruct((T, D), F32),
                   jax.ShapeDtypeStruct((1, D), F32)],
        compiler_params=_cparams(("arbitrary",), VMEM_LIMIT),
    )(h, gain, target)


def _adamw(parts, w, m, v, name):
    R, C = w.shape
    mult = 16 if parts.dtype == BF16 else 8
    tr = max(t for t in range(mult, min(R, 512) + 1, mult) if R % t == 0)
    c1 = 1.0 - ADAM_B1 ** ADAM_STEP
    c2 = 1.0 - ADAM_B2 ** ADAM_STEP

    def body(p_ref, w_ref, m_ref, v_ref, g_ref, d_ref, nm_ref, nv_ref):
        g = p_ref[0].astype(F32)
        for k in range(1, N_DEV):
            g = g + p_ref[k].astype(F32)
        mm = ADAM_B1 * m_ref[...] + (1.0 - ADAM_B1) * g
        vv = ADAM_B2 * v_ref[...] + (1.0 - ADAM_B2) * (g * g)
        g_ref[...] = g
        nm_ref[...] = mm
        nv_ref[...] = vv
        d_ref[...] = -ADAM_LR * ((mm / c1) / (jnp.sqrt(vv / c2) + ADAM_EPS) + ADAM_WD * w_ref[...])

    spec = pl.BlockSpec((tr, C), lambda i: (i, 0))
    return pl.pallas_call(
        body, name=name, grid=(R // tr,),
        in_specs=[pl.BlockSpec((N_DEV, tr, C), lambda i: (0, i, 0)), spec, spec, spec],
        out_specs=[spec, spec, spec, spec],
        out_shape=[jax.ShapeDtypeStruct((R, C), F32)] * 4,
        compiler_params=_cparams(("parallel",), VMEM_LIMIT),
    )(parts, w, m, v)


S5_NS = 256
S5_NH = 2
S5_NCB = 4
S5_RC = 4096
S5_NQ = 4
S5_GROUP = 2


def _disc_math(a_re, a_im, log_dt, bt_re, bt_im):
    dt = jnp.exp(log_dt)
    zr, zi = a_re * dt, a_im * dt
    mag = jnp.exp(zr)
    lb_re, lb_im = mag * jnp.cos(zi), mag * jnp.sin(zi)
    den = a_re * a_re + a_im * a_im
    nr, ni = lb_re - 1.0, lb_im
    f_re = (nr * a_re + ni * a_im) / den
    f_im = (ni * a_re - nr * a_im) / den
    bb_re = f_re[:, None, :] * bt_re - f_im[:, None, :] * bt_im
    bb_im = f_re[:, None, :] * bt_im + f_im[:, None, :] * bt_re
    return lb_re, lb_im, bb_re, bb_im


def _disc_fwd(a_re, a_im, log_dt, bt_re, bt_im, chain_len, name):
    G, P = a_re.shape
    C = bt_re.shape[1]
    n_sq = int(round(math.log2(chain_len)))
    assert 2 ** n_sq == chain_len

    def body(a_re_ref, a_im_ref, ldt_ref, br_ref, bi_ref, lr_ref, li_ref, sr_ref, si_ref, bbr_ref, bbi_ref):
        lr, li, bbr, bbi = _disc_math(a_re_ref[...], a_im_ref[...], ldt_ref[...], br_ref[...], bi_ref[...])
        lr_ref[...] = lr
        li_ref[...] = li
        bbr_ref[...] = bbr
        bbi_ref[...] = bbi
        pr, pi = lr, li
        for _ in range(n_sq):
            pr, pi = pr * pr - pi * pi, 2.0 * pr * pi
        sr_ref[...] = pr
        si_ref[...] = pi

    s2 = jax.ShapeDtypeStruct((G, P), F32)
    s3 = jax.ShapeDtypeStruct((G, C, P), F32)
    return pl.pallas_call(body, name=name, out_shape=[s2, s2, s2, s2, s3, s3])(a_re, a_im, log_dt, bt_re, bt_im)


def _disc_bwd(a_re, a_im, log_dt, bt_re, bt_im, d_lr, d_li, d_bbr, d_bbi, name):
    G, P = a_re.shape
    C = bt_re.shape[1]

    def body(a_re_ref, a_im_ref, ldt_ref, br_ref, bi_ref, c1, c2, c3, c4, o1, o2, o3, o4, o5):
        _, vjp = jax.vjp(_disc_math, a_re_ref[...], a_im_ref[...], ldt_ref[...], br_ref[...], bi_ref[...])
        o1[...], o2[...], o3[...], o4[...], o5[...] = vjp((c1[...], c2[...], c3[...], c4[...]))

    s2 = jax.ShapeDtypeStruct((G, P), F32)
    s3 = jax.ShapeDtypeStruct((G, C, P), F32)
    return pl.pallas_call(body, name=name, out_shape=[s2, s2, jax.ShapeDtypeStruct((G, 1), F32), s3, s3])(
        a_re, a_im, log_dt, bt_re, bt_im, d_lr, d_li, d_bbr, d_bbi)


def _row_block(ib):
    return pl.ds(pl.multiple_of(ib * SCAN_LANES, SCAN_LANES), SCAN_LANES)


def _chain_block(j, i, ascending, n_blocks):
    at = j * (n_blocks // S5_NQ) + i
    return _row_block(jnp.where(ascending, at, n_blocks - 1 - at))


def _unrolled_loop(n, unroll, body, carry):
    trips = n // unroll
    carry = lax.fori_loop(
        0, trips, lambda t, c: functools.reduce(lambda cc, u: body(t * unroll + u, cc), range(unroll), c), carry)
    for i in range(trips * unroll, n):
        carry = body(i, carry)
    return carry


def _cmul_add(lr, li, sr, si, xr, xi):
    return lr * sr - li * si + xr, lr * si + li * sr + xi


def _scan(xr_ref, xi_ref, lr, li, init, ascending, n_blocks, store):
    steps = n_blocks // S5_NQ
    if not store:
        def step(i, carry):
            blocks = [_chain_block(j, i, ascending, n_blocks) for j in range(S5_NQ)]
            return tuple(_cmul_add(lr, li, sr, si, xr_ref[rows, :], xi_ref[rows, :])
                         for (sr, si), rows in zip(carry, blocks))

        return _unrolled_loop(steps, 4, step, init)

    group = S5_GROUP
    assert steps % group == 0

    def trip(t, carry):
        blocks = [[_chain_block(j, t * group + u, ascending, n_blocks) for j in range(S5_NQ)] for u in range(group)]
        xs = [[(xr_ref[rows, :], xi_ref[rows, :]) for rows in row] for row in blocks]
        states = list(carry)
        done = []
        for u in range(group):
            states = [_cmul_add(lr, li, sr, si, xr, xi) for (sr, si), (xr, xi) in zip(states, xs[u])]
            done.append(states)
        for u in range(group):
            for rows, (nr, ni) in zip(blocks[u], done[u]):
                xr_ref[rows, :] = nr
                xi_ref[rows, :] = ni
        return tuple(states)

    return lax.fori_loop(0, steps // group, trip, init)


def _segment_starts(w, lsr, lsi, ascending):
    shape = w[0][0].shape
    row = lax.broadcasted_iota(jnp.int32, shape, 0)
    keep = row != jnp.where(ascending, 0, SCAN_LANES - 1)

    def shift(t):
        t = jnp.where(ascending, pltpu.roll(t, 1, 0), pltpu.roll(t, SCAN_LANES - 1, 0))
        return jnp.where(keep, t, 0.0)

    zero = jnp.zeros(shape, F32)
    c = [(zero, zero)] * S5_NQ
    for _ in range(SCAN_LANES):
        tr, ti = _cmul_add(lsr, lsi, *c[-1], *w[-1])
        c[0] = (shift(tr), shift(ti))
        for j in range(1, S5_NQ):
            c[j] = _cmul_add(lsr, lsi, *c[j - 1], *w[j - 1])
    return tuple(c)


def _first_pass(xr_ref, xi_ref, lam_ref, ascending, n_blocks, conj):
    shape = (SCAN_LANES, xr_ref.shape[1])
    sign = -1.0 if conj else 1.0
    lr = jnp.broadcast_to(lam_ref[0:1, :], shape)
    li = sign * jnp.broadcast_to(lam_ref[1:2, :], shape)
    lsr = jnp.broadcast_to(lam_ref[2:3, :], shape)
    lsi = sign * jnp.broadcast_to(lam_ref[3:4, :], shape)
    zero = jnp.zeros(shape, F32)
    w = _scan(xr_ref, xi_ref, lr, li, ((zero, zero),) * S5_NQ, ascending, n_blocks, store=False)
    return _segment_starts(w, lsr, lsi, ascending), lr, li


def _s5_specs(T):
    NS = S5_NS
    tok = pl.BlockSpec((T, 128), lambda c, d, h: (0, c))
    b_spec = pl.BlockSpec((None, None, None, 128, NS), lambda c, d, h: (d, c, h, 0, 0))
    c_spec = pl.BlockSpec((None, None, None, NS, 128), lambda c, d, h: (d, c, h, 0, 0))
    lam_spec = pl.BlockSpec((None, None, None, 4, NS), lambda c, d, h: (d, c, h, 0, 0))
    return tok, b_spec, c_spec, lam_spec


def _s5_fwd(zp, bre, bim, lam, cre, cimn, comm=None):
    T = zp.shape[0]
    NS = S5_NS
    nb = T // SCAN_LANES
    rc = min(S5_RC, T)
    tok, b_spec, c_spec, lam_spec = _s5_specs(T)

    def body(zp_ref, bre_ref, bim_ref, lam_ref, cre_ref, cim_ref, y_ref, xr_ref, xi_ref):
        d = pl.program_id(1)
        ascending = d == 0

        @pl.when((d == 0) & (pl.program_id(2) == 0))
        def _():
            y_ref[...] = jnp.zeros_like(y_ref)

        def proj(c, _):
            rows = pl.ds(pl.multiple_of(c * rc, rc), rc)
            zz = zp_ref[rows, :]
            xr_ref[rows, :] = _dot(zz, bre_ref[...])
            xi_ref[rows, :] = _dot(zz, bim_ref[...])
            return 0

        lax.fori_loop(0, T // rc, proj, 0)
        starts, lr, li = _first_pass(xr_ref, xi_ref, lam_ref, ascending, nb, conj=False)
        _scan(xr_ref, xi_ref, lr, li, starts, ascending, nb, store=True)

        def outp(c, _):
            rows = pl.ds(pl.multiple_of(c * rc, rc), rc)
            y_ref[rows, :] += (_dot(xr_ref[rows, :].astype(BF16), cre_ref[...])
                               + _dot(xi_ref[rows, :].astype(BF16), cim_ref[...]))
            return 0

        lax.fori_loop(0, T // rc, outp, 0)

    return _pallas(
        body, "s5_fwd", (S5_NCB, 2, S5_NH),
        [tok, b_spec, b_spec, lam_spec, c_spec, c_spec], [tok],
        [jax.ShapeDtypeStruct((T, SSM_WIDTH), F32)],
        [pltpu.VMEM((T, NS), F32), pltpu.VMEM((T, NS), F32)],
        ("parallel", "arbitrary", "arbitrary"), (zp, bre, bim, lam, cre, cimn), comm)


def _s5_bwd(zp, dyp, bre, bim, lam, cre, cimn, comm=None):
    T = zp.shape[0]
    NS, NH = S5_NS, S5_NH
    nb = T // SCAN_LANES
    rc = min(S5_RC, T)
    tok, b_spec, c_spec, lam_spec = _s5_specs(T)
    dlam_spec = pl.BlockSpec((None, None, None, 2, NS), lambda c, d, h: (d, c, h, 0, 0))

    def body(zp_ref, dyp_ref, bre_ref, bim_ref, lam_ref, cre_ref, cim_ref,
             dzp_ref, dbre_ref, dbim_ref, dlam_ref, dcre_ref, dcim_ref,
             sr_ref, si_ref, gr_ref, gi_ref):
        d = pl.program_id(1)
        ascending = d == 0
        g_ascending = d != 0

        @pl.when((d == 0) & (pl.program_id(2) == 0))
        def _():
            dzp_ref[...] = jnp.zeros_like(dzp_ref)

        dcre_ref[...] = jnp.zeros_like(dcre_ref)
        dcim_ref[...] = jnp.zeros_like(dcim_ref)
        dbre_ref[...] = jnp.zeros_like(dbre_ref)
        dbim_ref[...] = jnp.zeros_like(dbim_ref)

        def proj(c, _):
            rows = pl.ds(pl.multiple_of(c * rc, rc), rc)
            zz = zp_ref[rows, :]
            sr_ref[rows, :] = _dot(zz, bre_ref[...])
            si_ref[rows, :] = _dot(zz, bim_ref[...])
            dy = dyp_ref[rows, :]
            gr_ref[rows, :] = _dot_nt(dy, cre_ref[...])
            gi_ref[rows, :] = _dot_nt(dy, cim_ref[...])
            return 0

        lax.fori_loop(0, T // rc, proj, 0)
        s_starts, lr, li = _first_pass(sr_ref, si_ref, lam_ref, ascending, nb, conj=False)
        _scan(sr_ref, si_ref, lr, li, s_starts, ascending, nb, store=True)
        g_starts, lr, lic = _first_pass(gr_ref, gi_ref, lam_ref, g_ascending, nb, conj=True)

        steps = nb // S5_NQ
        group = S5_GROUP
        assert steps % group == 0

        def gtrip(t, carry, last):
            g, (ar, ai) = carry
            first = t * group
            blocks = [[_chain_block(j, first + u, g_ascending, nb) for j in range(S5_NQ)] for u in range(group)]
            direct = [[(gr_ref[rows, :], gi_ref[rows, :]) for rows in row] for row in blocks]
            done = []
            for u in range(group):
                new = []
                for j, ((g_r, g_i), (d_r, d_i)) in enumerate(zip(g, direct[u])):
                    n_r, n_i = _cmul_add(lr, lic, g_r, g_i, d_r, d_i)
                    if last and u == group - 1:
                        s_r, s_i = s_starts[S5_NQ - 1 - j]
                    else:
                        prev = _chain_block(j, first + u + 1, g_ascending, nb)
                        s_r, s_i = sr_ref[prev, :], si_ref[prev, :]
                    ar = ar + n_r * s_r + n_i * s_i
                    ai = ai + n_i * s_r - n_r * s_i
                    new.append((n_r, n_i))
                g = new
                done.append(new)
            for u in range(group):
                for rows, (n_r, n_i) in zip(blocks[u], done[u]):
                    gr_ref[rows, :] = n_r
                    gi_ref[rows, :] = n_i
            return tuple(g), (ar, ai)

        zero = jnp.zeros((SCAN_LANES, NS), F32)
        carry = lax.fori_loop(0, steps // group - 1, lambda t, c: gtrip(t, c, False), (g_starts, (zero, zero)))
        _, (ar, ai) = gtrip(steps // group - 1, carry, True)
        dlam_ref[0:1, :] = _col_sum(ar)
        dlam_ref[1:2, :] = _col_sum(ai)

        def grads(c, _):
            rows = pl.ds(pl.multiple_of(c * rc, rc), rc)
            zz = zp_ref[rows, :]
            dy = dyp_ref[rows, :]
            g_rb = gr_ref[rows, :].astype(BF16)
            g_ib = gi_ref[rows, :].astype(BF16)
            dcre_ref[...] += _dot_tn(sr_ref[rows, :].astype(BF16), dy)
            dcim_ref[...] += _dot_tn(si_ref[rows, :].astype(BF16), dy)
            dbre_ref[...] += _dot_tn(zz, g_rb)
            dbim_ref[...] += _dot_tn(zz, g_ib)
            dzp_ref[rows, :] += _dot_nt(g_rb, bre_ref[...]) + _dot_nt(g_ib, bim_ref[...])
            return 0

        lax.fori_loop(0, T // rc, grads, 0)

    f32 = lambda *s: jax.ShapeDtypeStruct(s, F32)
    return _pallas(
        body, "s5_bwd", (S5_NCB, 2, S5_NH),
        [tok, tok, b_spec, b_spec, lam_spec, c_spec, c_spec],
        [tok, b_spec, b_spec, dlam_spec, c_spec, c_spec],
        [f32(T, SSM_WIDTH), f32(2, S5_NCB, NH, 128, NS), f32(2, S5_NCB, NH, 128, NS),
         f32(2, S5_NCB, NH, 2, NS), f32(2, S5_NCB, NH, NS, 128), f32(2, S5_NCB, NH, NS, 128)],
        [pltpu.VMEM((T, NS), F32)] * 4,
        ("parallel", "arbitrary", "arbitrary"), (zp, dyp, bre, bim, lam, cre, cimn), comm)


def _s5_delta():
    d = np.zeros((S5_NH, 8, 8 // S5_NH), np.float32)
    for h in range(S5_NH):
        for go in range(8 // S5_NH):
            d[h, h * (8 // S5_NH) + go, go] = 1.0
    return d


def _s5_pack_b(bbt):
    gh = 8 // S5_NH
    b5 = bbt.reshape(S5_NCB, S5_NH, gh, SSM_GROUP, SSM_STATE).transpose(0, 1, 3, 2, 4)
    m = b5[:, :, None] * _s5_delta()[None, :, :, None, :, None]
    return m.reshape(S5_NCB, S5_NH, 128, S5_NS)


def _s5_unpack_b(dm):
    gh = 8 // S5_NH
    d6 = dm.reshape(S5_NCB, S5_NH, 8, SSM_GROUP, gh, SSM_STATE)
    b5 = jnp.sum(d6 * _s5_delta()[None, :, :, None, :, None], axis=2)
    return b5.transpose(0, 1, 3, 2, 4).reshape(SSM_GROUPS, SSM_GROUP, SSM_STATE)


def _s5_pack_c(c):
    gh = 8 // S5_NH
    c5 = c.reshape(S5_NCB, S5_NH, gh, SSM_GROUP, SSM_STATE).transpose(0, 1, 2, 4, 3)
    m = c5[:, :, :, :, None, :] * _s5_delta().transpose(0, 2, 1)[None, :, :, None, :, None]
    return m.reshape(S5_NCB, S5_NH, S5_NS, 128)


def _s5_unpack_c(dm):
    gh = 8 // S5_NH
    d6 = dm.reshape(S5_NCB, S5_NH, gh, SSM_STATE, 8, SSM_GROUP)
    c5 = jnp.sum(d6 * _s5_delta().transpose(0, 2, 1)[None, :, :, None, :, None], axis=4)
    return c5.transpose(0, 1, 2, 4, 3).reshape(SSM_GROUPS, SSM_GROUP, SSM_STATE)


def _s5_pack_lam(x):
    return x.reshape(S5_NCB, S5_NH, S5_NS)


def _permute_rows(x):
    T = x.shape[0]
    return x.reshape(SCAN_LANES, T // SCAN_LANES, -1).transpose(1, 0, 2).reshape(T, -1)


def _unpermute_rows(x):
    T = x.shape[0]
    return x.reshape(T // SCAN_LANES, SCAN_LANES, -1).transpose(1, 0, 2).reshape(T, -1)


ATT_TB = ATT_ROWS * GRID_W
ATT_KB = 3 * ATT_TB


def _att_valid(i, n_rows):
    qi, kj = np.meshgrid(np.arange(ATT_TB), np.arange(ATT_KB), indexing="ij")
    r = i * ATT_ROWS + qi // GRID_W
    c = qi % GRID_W
    rk = (i - 1) * ATT_ROWS + kj // GRID_W
    x = kj % GRID_W
    rs = np.clip(r - WIN_H // 2, 0, n_rows - WIN_H)
    cs = np.clip(c - WIN_W // 2, 0, GRID_W - WIN_W)
    return (rk >= rs) & (rk < rs + WIN_H) & (x >= cs) & (x < cs + WIN_W)


def _att_masked_tables(table, n_rows):
    n = n_rows // ATT_ROWS
    assert n >= 3
    masks = np.stack([_att_valid(i, n_rows) for i in (0, 1, n - 1)])
    return jnp.where(masks[:, None], table[None], NEG_INF)


def _att_variant(i, n):
    return jnp.where(i == 0, 0, jnp.where(i >= n - 1, 2, 1))


def _att_exp(qh, kth, bias):
    s = _dot(qh, kth) + bias
    return jnp.exp(s - jnp.max(s, axis=1, keepdims=True))


def _att_values_and_ones(vh):
    return jnp.concatenate([vh, jnp.ones_like(vh)], axis=1)


def _att_specs(n, col):
    last = n - 1
    cur = lambda i: (jnp.minimum(i, last), col)
    prv = lambda i: (jnp.maximum(jnp.minimum(i, last) - 1, 0), col)
    nxt = lambda i: (jnp.minimum(i + 1, last), col)
    blk = lambda f: pl.BlockSpec((ATT_TB, ATT_WIDTH), f)
    return blk(cur), blk(prv), blk(nxt)


def _att_t_specs(n, row):
    last = n - 1
    cur = lambda i: (row, jnp.minimum(i, last))
    prv = lambda i: (row, jnp.maximum(jnp.minimum(i, last) - 1, 0))
    nxt = lambda i: (row, jnp.minimum(i + 1, last))
    blk = lambda f: pl.BlockSpec((ATT_WIDTH, ATT_TB), f)
    return blk(cur), blk(prv), blk(nxt)


def _att_fwd(zb, kvt, biasv):
    T = zb.shape[0]
    W = ATT_WIDTH
    n = T // ATT_TB
    n_rows = T // GRID_W
    cur = _att_specs(n, 0)[0]
    q_cur = _att_specs(n, 1)[0]
    k_cur, k_prv, k_nxt = _att_t_specs(n, 0)
    v_cur, v_prv, v_nxt = _att_specs(n, 3)

    def body(q_ref, kp_ref, kc_ref, kn_ref, vp_ref, vc_ref, vn_ref, b_ref, y_ref):
        qs = q_ref[...] * 0.125
        kt = jnp.concatenate([kp_ref[...], kc_ref[...], kn_ref[...]], axis=1)
        vb = jnp.concatenate([vp_ref[...], vc_ref[...], vn_ref[...]], axis=0)
        outs = []
        for h in range(ATT_HEADS):
            hs = slice(h * ATT_HEAD_DIM, (h + 1) * ATT_HEAD_DIM)
            e = _att_exp(qs[:, hs], kt[hs, :], b_ref[h]).astype(BF16)
            ov = _dot(e, _att_values_and_ones(vb[:, hs]))
            outs.append(ov[:, :ATT_HEAD_DIM] * (1.0 / ov[:, ATT_HEAD_DIM:ATT_HEAD_DIM + 1]))
        y_ref[...] = jnp.concatenate(outs, axis=1).astype(BF16)

    return pl.pallas_call(
        body, name="att_fwd", grid=(n,),
        in_specs=[q_cur, k_prv, k_cur, k_nxt, v_prv, v_cur, v_nxt,
                  pl.BlockSpec((None, ATT_HEADS, ATT_TB, ATT_KB), lambda i: (_att_variant(i, n), 0, 0, 0))],
        out_specs=cur,
        out_shape=jax.ShapeDtypeStruct((T, W), BF16),
        compiler_params=_cparams(("parallel",), VMEM_LIMIT),
    )(zb, kvt, kvt, kvt, zb, zb, zb, biasv)


def _att_bwd(zb, kvt, y, do, biasv, comm=None):
    T = zb.shape[0]
    W = ATT_WIDTH
    n = T // ATT_TB
    n_rows = T // GRID_W
    cur = _att_specs(n, 0)[0]
    q_cur = _att_specs(n, 1)[0]
    k_cur, k_prv, k_nxt = _att_specs(n, 2)
    kt_cur, kt_prv, kt_nxt = _att_t_specs(n, 0)
    vt_cur, vt_prv, vt_nxt = _att_t_specs(n, 1)
    done = pl.BlockSpec((ATT_TB, W), lambda i: (jnp.maximum(i - 1, 0), 0))
    bias_spec = pl.BlockSpec((None, ATT_HEADS, ATT_TB, ATT_KB), lambda i: (_att_variant(i, n), 0, 0, 0))

    def body(q_ref, y_ref, do_ref, ktp_ref, ktc_ref, ktn_ref, vtp_ref, vtc_ref, vtn_ref,
             kp_ref, kc_ref, kn_ref, b_ref,
             dq_ref, dk_ref, dv_ref, db_ref, acck_ref, accv_ref):
        i = pl.program_id(0)

        @pl.when(i == 0)
        def _():
            db_ref[...] = jnp.zeros_like(db_ref)
            acck_ref[...] = jnp.zeros_like(acck_ref)
            accv_ref[...] = jnp.zeros_like(accv_ref)

        @pl.when((i > 0) & (i < n))
        def _():
            slot = lax.rem(i + 1, 3)
            acck_ref[slot] = jnp.zeros((ATT_TB, W), F32)
            accv_ref[slot] = jnp.zeros((ATT_TB, W), F32)

        @pl.when(i < n)
        def _():
            qs = q_ref[...] * 0.125
            dob = do_ref[...]
            dy = dob.astype(F32) * y_ref[...].astype(F32)
            kb = jnp.concatenate([kp_ref[...], kc_ref[...], kn_ref[...]], axis=0)
            kt = jnp.concatenate([ktp_ref[...], ktc_ref[...], ktn_ref[...]], axis=1)
            vt = jnp.concatenate([vtp_ref[...], vtc_ref[...], vtn_ref[...]], axis=1)
            dqs, dks, dvs = [], [], []
            for h in range(ATT_HEADS):
                hs = slice(h * ATT_HEAD_DIM, (h + 1) * ATT_HEAD_DIM)
                qh, kh, doh = qs[:, hs], kb[:, hs], dob[:, hs]
                e = _att_exp(qh, kt[hs, :], b_ref[h])
                p = e * (1.0 / jnp.sum(e, axis=1, keepdims=True))
                dp = _dot(doh, vt[hs, :])
                ds = p * (dp - jnp.sum(dy[:, hs], axis=1, keepdims=True))
                db_ref[h] += ds
                dsb = ds.astype(BF16)
                dqs.append(_dot(dsb, kh) * 0.125)
                dks.append(_dot_tn(dsb, qh))
                dvs.append(_dot_tn(p.astype(BF16), doh))
            dq_ref[...] = jnp.concatenate(dqs, axis=1).astype(BF16)
            dk_all = jnp.concatenate(dks, axis=1)
            dv_all = jnp.concatenate(dvs, axis=1)
            for b in range(3):
                slot = lax.rem(i + 2 + b, 3)
                rows = slice(b * ATT_TB, (b + 1) * ATT_TB)
                acck_ref[slot] += dk_all[rows]
                accv_ref[slot] += dv_all[rows]

        slot = lax.rem(i + 2, 3)
        dk_ref[...] = acck_ref[slot].astype(BF16)
        dv_ref[...] = accv_ref[slot].astype(BF16)

    return _pallas(
        body, "att_bwd", (n + 1,),
        [q_cur, cur, cur, kt_prv, kt_cur, kt_nxt, vt_prv, vt_cur, vt_nxt,
         k_prv, k_cur, k_nxt, bias_spec],
        [cur, done, done, pl.BlockSpec((ATT_HEADS, ATT_TB, ATT_KB), lambda i: (0, 0, 0))],
        [jax.ShapeDtypeStruct((T, W), BF16)] * 3 + [jax.ShapeDtypeStruct((ATT_HEADS, ATT_TB, ATT_KB), F32)],
        [pltpu.VMEM((3, ATT_TB, W), F32), pltpu.VMEM((3, ATT_TB, W), F32)],
        ("arbitrary",), (zb, y, do, kvt, kvt, kvt, kvt, kvt, kvt, zb, zb, zb, biasv), comm)


def _att_selectors():
    rsel = np.zeros((ATT_ROWS, 3 * ATT_ROWS, 2 * WIN_H - 1), np.float32)
    for a in range(ATT_ROWS):
        for b in range(3 * ATT_ROWS):
            rsel[a, b, b - a - ATT_ROWS + WIN_H - 1] = 1.0
    csel = np.zeros((GRID_W, GRID_W, 2 * WIN_W - 1), np.float32)
    for c in range(GRID_W):
        for x in range(GRID_W):
            csel[c, x, min(max(x - c, -(WIN_W - 1)), WIN_W - 1) + WIN_W - 1] = 1.0
    return rsel, csel


def _att_bias_table(rpb):
    rsel, csel = _att_selectors()
    hi = lax.Precision.HIGHEST
    t = jnp.einsum('hrd,abr->habd', rpb, rsel, precision=hi)
    t = jnp.einsum('habd,cxd->hacbx', t, csel, precision=hi)
    return t.reshape(ATT_HEADS, ATT_TB, ATT_KB)


def _att_bias_table_t(dtable):
    rsel, csel = _att_selectors()
    hi = lax.Precision.HIGHEST
    t = dtable.reshape(ATT_HEADS, ATT_ROWS, GRID_W, 3 * ATT_ROWS, GRID_W)
    t = jnp.einsum('hacbx,cxd->habd', t, csel, precision=hi)
    return jnp.einsum('habd,abr->hrd', t, rsel, precision=hi)


GELU_K = math.sqrt(2.0 / math.pi)
GELU_C = 0.044715
MERGE_TM = 256


def _gelu(x):
    return 0.5 * x * (1.0 + jnp.tanh(GELU_K * (x + GELU_C * x * x * x)))


def _gelu_grad(x):
    t = jnp.tanh(GELU_K * (x + GELU_C * x * x * x))
    return 0.5 * (1.0 + t) + 0.5 * x * (1.0 - t * t) * GELU_K * (1.0 + 3.0 * GELU_C * x * x)


def _merge_forward(ypre, zs, gs, ga, ya, ssm_d, w_glu, b_glu, w_bs, w_ba):
    zs, gs, ga = zs.astype(F32), gs.astype(F32), ga.astype(F32)
    ys = ypre + ssm_d * zs
    yg = _gelu(ys)
    sg = jax.nn.sigmoid(_dot(yg.astype(BF16), w_glu) + b_glu)
    y2 = yg * sg
    bs = _dot(y2.astype(BF16), w_bs)
    ba = _dot(ya, w_ba)
    s1 = jax.nn.sigmoid(gs)
    s2 = jax.nn.sigmoid(ga)
    merged = s1 * bs + s2 * ba
    return ys, yg, sg, y2, bs, ba, s1, s2, merged


def _merge_in_specs(D, W, tm):
    tok = lambda w, c: pl.BlockSpec((tm, w), lambda i: (i, c))
    full = lambda r, c: pl.BlockSpec((r, c), lambda i: (0, 0))
    z_specs = [tok(W, 0), tok(D, 4 * W // D), tok(D, 4 * W // D + 1)]
    w_specs = [full(1, W), full(W, W), full(1, W), full(W, D), full(W, D), full(D, D)]
    return tok, z_specs, w_specs


def _merge_fwd(ypre, z, ya, h1, ssm_d, w_glu, b_glu, w_bs, w_ba, w_out):
    T, D = h1.shape
    W = ypre.shape[1]
    tm = min(T, 2 * MERGE_TM)
    tok, z_specs, w_specs = _merge_in_specs(D, W, tm)

    def body(ypre_ref, zs_ref, gs_ref, ga_ref, ya_ref, h1_ref, d_ref, wglu_ref, bglu_ref, wbs_ref, wba_ref, wout_ref,
             h2_ref):
        merged = _merge_forward(ypre_ref[...], zs_ref[...], gs_ref[...], ga_ref[...], ya_ref[...], d_ref[...],
                                wglu_ref[...], bglu_ref[...], wbs_ref[...], wba_ref[...])[-1]
        h2_ref[...] = h1_ref[...] + _dot(merged.astype(BF16), wout_ref[...])

    return pl.pallas_call(
        body, name="merge_fwd", grid=(T // tm,),
        in_specs=[tok(W, 0)] + z_specs + [tok(W, 0), tok(D, 0)] + w_specs,
        out_specs=tok(D, 0),
        out_shape=jax.ShapeDtypeStruct((T, D), F32),
        compiler_params=_cparams(("parallel",), VMEM_LIMIT),
    )(ypre, z, z, z, ya, h1, ssm_d, w_glu, b_glu, w_bs, w_ba, w_out)


def _merge_bwd(dh2, ypre, z, ya, ssm_d, w_glu, b_glu, w_bs, w_ba, w_out):
    T, D = dh2.shape
    W = ypre.shape[1]
    tm = min(T, MERGE_TM)
    tok, z_specs, w_specs = _merge_in_specs(D, W, tm)

    def body(dh2_ref, ypre_ref, zs_ref, gs_ref, ga_ref, ya_ref, d_ref, wglu_ref, bglu_ref, wbs_ref, wba_ref, wout_ref,
             dypre_ref, dzs_ref, dgs_ref, dga_ref, dya_ref, dd_ref, dwglu_ref, dbglu_ref, dwbs_ref, dwba_ref, dwout_ref):
        @pl.when(pl.program_id(0) == 0)
        def _():
            for r in (dd_ref, dwglu_ref, dbglu_ref, dwbs_ref, dwba_ref, dwout_ref):
                r[...] = jnp.zeros_like(r)

        zs = zs_ref[...].astype(F32)
        ya = ya_ref[...]
        ys, yg, sg, y2, bs, ba, s1, s2, merged = _merge_forward(
            ypre_ref[...], zs, gs_ref[...], ga_ref[...], ya, d_ref[...],
            wglu_ref[...], bglu_ref[...], wbs_ref[...], wba_ref[...])
        dh2b = dh2_ref[...].astype(BF16)
        dmerged = _dot_nt(dh2b, wout_ref[...])
        dwout_ref[...] += _dot_tn(merged.astype(BF16), dh2b)
        dbs = (dmerged * s1).astype(BF16)
        dba = (dmerged * s2).astype(BF16)
        dgs_ref[...] = (dmerged * bs * s1 * (1.0 - s1)).astype(BF16)
        dga_ref[...] = (dmerged * ba * s2 * (1.0 - s2)).astype(BF16)
        dwbs_ref[...] += _dot_tn(y2.astype(BF16), dbs)
        dwba_ref[...] += _dot_tn(ya, dba)
        dya_ref[...] = _dot_nt(dba, wba_ref[...]).astype(BF16)
        dy2 = _dot_nt(dbs, wbs_ref[...])
        dvv = dy2 * yg * sg * (1.0 - sg)
        dvvb = dvv.astype(BF16)
        dyg = dy2 * sg + _dot_nt(dvvb, wglu_ref[...])
        dwglu_ref[...] += _dot_tn(yg.astype(BF16), dvvb)
        dbglu_ref[...] += _col_sum(dvv)
        dys = dyg * _gelu_grad(ys)
        dd_ref[...] += _col_sum(dys * zs)
        dzs_ref[...] = dys * d_ref[...]
        dypre_ref[...] = dys.astype(BF16)

    f32 = lambda *s: jax.ShapeDtypeStruct(s, F32)
    b16 = lambda *s: jax.ShapeDtypeStruct(s, BF16)
    return pl.pallas_call(
        body, name="merge_bwd", grid=(T // tm,),
        in_specs=[tok(D, 0), tok(W, 0)] + z_specs + [tok(W, 0)] + w_specs,
        out_specs=[tok(W, 0), tok(W, 0), tok(D, 0), tok(D, 0), tok(W, 0)] + w_specs,
        out_shape=[b16(T, W), f32(T, W), b16(T, D), b16(T, D), b16(T, W),
                   f32(1, W), f32(W, W), f32(1, W), f32(W, D), f32(W, D), f32(D, D)],
        compiler_params=_cparams(("arbitrary",), VMEM_LIMIT),
    )(dh2, ypre, z, z, z, ya, ssm_d, w_glu, b_glu, w_bs, w_ba, w_out)


def _cast_shards(weights):
    def body(*refs):
        n = len(refs) // 2
        for src, dst in zip(refs[:n], refs[n:]):
            dst[...] = src[0].astype(BF16)

    return pl.pallas_call(
        body, name="cast_shards",
        out_shape=[jax.ShapeDtypeStruct(w.shape[1:], BF16) for w in weights],
        compiler_params=_cparams(None, VMEM_LIMIT))(*weights)


def _gather_two_level(shards, name):
    n = len(shards)

    def body(*refs):
        x_refs, out_refs = refs[:n], refs[n:2 * n]
        send_sems, recv_sems, local_sems = refs[2 * n:]
        x, y, c = _my_place()
        me, sibling = (x, y, c), (x, y, 1 - c)
        chips = [(1 - x, y), (x, 1 - y), (1 - x, 1 - y)]

        def copy(a, k, block, to, own=False):
            slot = out_refs[a].at[_flat(*block)]
            return pltpu.make_async_remote_copy(
                src_ref=x_refs[a] if own else slot, dst_ref=slot,
                send_sem=send_sems.at[7 * a + k], recv_sem=recv_sems.at[7 * a + k],
                device_id=to, device_id_type=MESH_ID)

        sent, local = [], []
        for a in range(n):
            local.append(pltpu.make_async_copy(x_refs[a], out_refs[a].at[_flat(*me)], local_sems.at[a]))
            local[-1].start()
            sent.append(copy(a, 0, me, sibling, own=True))
            sent += [copy(a, 1 + j, me, (*chip, c), own=True) for j, chip in enumerate(chips)]
        for cp in sent:
            cp.start()
        for a in range(n):
            for j, chip in enumerate(chips):
                copy(a, 1 + j, (*chip, c), me).wait_recv()
                sent.append(copy(a, 4 + j, (*chip, c), sibling))
                sent[-1].start()
        for a in range(n):
            copy(a, 0, sibling, me).wait_recv()
            for j, chip in enumerate(chips):
                copy(a, 4 + j, (*chip, 1 - c), me).wait_recv()
        for cp in sent:
            cp.wait_send()
        for cp in local:
            cp.wait()

    return pl.pallas_call(
        body, name=name, in_specs=[_HBM] * n, out_specs=[_HBM] * n,
        out_shape=[jax.ShapeDtypeStruct((N_DEV,) + s.shape, s.dtype) for s in shards],
        scratch_shapes=[pltpu.SemaphoreType.DMA((7 * n,)), pltpu.SemaphoreType.DMA((7 * n,)),
                        pltpu.SemaphoreType.DMA((n,))],
    )(*shards)


PACK_COLS = 1024
BIG = (("ffn1_w_gate", 1), ("ffn1_w_up", 1), ("ffn1_w_down", 0), ("w_in", 1), ("ssm_w_glu", 0),
       ("w_branch_ssm", 1), ("w_branch_att", 1), ("w_out", 0),
       ("ffn2_w_gate", 1), ("ffn2_w_up", 1), ("ffn2_w_down", 0))
BIG_AXIS = dict(BIG)
TRANSPOSED = ("ffn1_w_gate", "ffn1_w_up", "ffn2_w_gate", "ffn2_w_up")
SSM_DIR = ("ssm_a_re", "ssm_a_im", "ssm_log_dt", "ssm_b_re", "ssm_b_im", "ssm_c_re", "ssm_c_im")
SMALL_EARLY = (("mix_norm",) + tuple(n + "_fwd" for n in SSM_DIR) + tuple(n + "_bwd" for n in SSM_DIR)
               + ("ssm_d", "ssm_b_glu", "att_rpb", "ffn2_norm", "final_norm"))
SMALL_LATE = ("ffn1_norm",)
WEIGHTS = ("ffn1_norm", "ffn1_w_gate", "ffn1_w_up", "ffn1_w_down", "mix_norm", "w_in") \
    + tuple(n + "_fwd" for n in SSM_DIR) + tuple(n + "_bwd" for n in SSM_DIR) \
    + ("ssm_d", "ssm_w_glu", "ssm_b_glu", "att_rpb", "w_branch_ssm", "w_branch_att", "w_out",
       "ffn2_norm", "ffn2_w_gate", "ffn2_w_up", "ffn2_w_down", "final_norm")


def _pad_rows(a, mult):
    pad = (-a.shape[-2]) % mult
    if pad:
        a = jnp.concatenate([a, jnp.zeros(a.shape[:-2] + (pad, a.shape[-1]), a.dtype)], axis=-2)
    return a


def _pack(arrays, row_mult):
    flat = jnp.concatenate([a.reshape(-1) for a in arrays])
    pad = (-flat.shape[0]) % PACK_COLS
    if pad:
        flat = jnp.concatenate([flat, jnp.zeros((pad,), flat.dtype)])
    return _pad_rows(flat.reshape(-1, PACK_COLS), row_mult)


def _unpack(slab, shapes):
    flat = slab.reshape(-1)
    out, at = [], 0
    for s in shapes:
        n = int(np.prod(s))
        out.append(flat[at:at + n].reshape(s))
        at += n
    return out


def _split_for_devices(g, axis):
    r, c = g.shape
    if axis == 1:
        return g.reshape(r, N_DEV, c // N_DEV).transpose(1, 0, 2).astype(BF16)
    return g.reshape(N_DEV, r // N_DEV, c).astype(BF16)


def _join_shards(gathered, axis):
    _, r, c = gathered.shape
    if axis == 1:
        return gathered.transpose(1, 0, 2).reshape(r, N_DEV * c)
    return gathered.reshape(N_DEV * r, c)


def _s5_direction_inputs(p, sfx, chain_len):
    bt_re = p["ssm_b_re" + sfx][0].transpose(0, 2, 1)
    bt_im = p["ssm_b_im" + sfx][0].transpose(0, 2, 1)
    raw = (p["ssm_a_re" + sfx][0], p["ssm_a_im" + sfx][0], p["ssm_log_dt" + sfx][0][:, None], bt_re, bt_im)
    lr, li, sr, si, bbr, bbi = _disc_fwd(*raw, chain_len, "s5_disc" + sfx)
    lam = jnp.stack([_s5_pack_lam(t) for t in (lr, li, sr, si)], axis=2)
    mats = (_s5_pack_b(bbr), _s5_pack_b(bbi), lam,
            _s5_pack_c(p["ssm_c_re" + sfx][0]), _s5_pack_c(-p["ssm_c_im" + sfx][0]))
    return raw, mats


def kernel(x, ffn1_norm, ffn1_w_gate, ffn1_w_up, ffn1_w_down, mix_norm, w_in, ssm_a_re_fwd, ssm_a_im_fwd, ssm_log_dt_fwd, ssm_b_re_fwd, ssm_b_im_fwd, ssm_c_re_fwd, ssm_c_im_fwd, ssm_a_re_bwd, ssm_a_im_bwd, ssm_log_dt_bwd, ssm_b_re_bwd, ssm_b_im_bwd, ssm_c_re_bwd, ssm_c_im_bwd, ssm_d, ssm_w_glu, ssm_b_glu, att_rpb, w_branch_ssm, w_branch_att, w_out, ffn2_norm, ffn2_w_gate, ffn2_w_up, ffn2_w_down, final_norm, loss_target, m_ffn1_norm, m_ffn1_w_gate, m_ffn1_w_up, m_ffn1_w_down, m_mix_norm, m_w_in, m_ssm_a_re_fwd, m_ssm_a_im_fwd, m_ssm_log_dt_fwd, m_ssm_b_re_fwd, m_ssm_b_im_fwd, m_ssm_c_re_fwd, m_ssm_c_im_fwd, m_ssm_a_re_bwd, m_ssm_a_im_bwd, m_ssm_log_dt_bwd, m_ssm_b_re_bwd, m_ssm_b_im_bwd, m_ssm_c_re_bwd, m_ssm_c_im_bwd, m_ssm_d, m_ssm_w_glu, m_ssm_b_glu, m_att_rpb, m_w_branch_ssm, m_w_branch_att, m_w_out, m_ffn2_norm, m_ffn2_w_gate, m_ffn2_w_up, m_ffn2_w_down, m_final_norm, v_ffn1_norm, v_ffn1_w_gate, v_ffn1_w_up, v_ffn1_w_down, v_mix_norm, v_w_in, v_ssm_a_re_fwd, v_ssm_a_im_fwd, v_ssm_log_dt_fwd, v_ssm_b_re_fwd, v_ssm_b_im_fwd, v_ssm_c_re_fwd, v_ssm_c_im_fwd, v_ssm_a_re_bwd, v_ssm_a_im_bwd, v_ssm_log_dt_bwd, v_ssm_b_re_bwd, v_ssm_b_im_bwd, v_ssm_c_re_bwd, v_ssm_c_im_bwd, v_ssm_d, v_ssm_w_glu, v_ssm_b_glu, v_att_rpb, v_w_branch_ssm, v_w_branch_att, v_w_out, v_ffn2_norm, v_ffn2_w_gate, v_ffn2_w_up, v_ffn2_w_down, v_final_norm):
    p = dict(locals())
    x = p["x"][0]
    target = p["loss_target"][0]
    T, D = x.shape

    stored = lambda a, n: jnp.swapaxes(a, -1, -2) if n in TRANSPOSED else a
    cut_axis = lambda n: 0 if n in TRANSPOSED else BIG_AXIS[n]
    shard = dict(zip([n for n, _ in BIG], _cast_shards([stored(p[n], n) for n, _ in BIG])))
    ffn1_w = ("ffn1_w_gate", "ffn1_w_up", "ffn1_w_down")
    mix_w = ("w_in", "ssm_w_glu", "w_branch_ssm", "w_branch_att", "w_out")
    ffn2_w = ("ffn2_w_gate", "ffn2_w_up", "ffn2_w_down")
    gathered = dict(zip(ffn1_w, _gather_two_level([shard[n] for n in ffn1_w], "gather_ffn1")))
    full = lambda n: _join_shards(gathered[n], cut_axis(n))

    h0 = x
    wg1, wu1, wd1 = [full(n) for n in ffn1_w]
    (h1, xn1, g1, u1), got = _ffn_fwd(h0, p["ffn1_norm"], wg1, wu1, wd1, "ffn1_fwd",
                                      _Comm("gather", [shard[n] for n in mix_w]))
    gathered.update(zip(mix_w, got))
    zb, un = _mixin_fwd(h1, p["mix_norm"], gathered["w_in"])
    W = SSM_WIDTH
    zp = _permute_rows(zb[:, :W])
    chain_len = T // SCAN_LANES // S5_NQ
    raw_f, mats_f = _s5_direction_inputs(p, "_fwd", chain_len)
    raw_b, mats_b = _s5_direction_inputs(p, "_bwd", chain_len)
    bre, bim, lam, cre, cimn = [jnp.stack([f, b]) for f, b in zip(mats_f, mats_b)]
    bre, bim, cre, cimn = [t.astype(BF16) for t in (bre, bim, cre, cimn)]
    (yp,), got = _s5_fwd(zp, bre, bim, lam, cre, cimn, _Comm("gather", [shard[n] for n in ffn2_w]))
    gathered.update(zip(ffn2_w, got))
    ypre = _unpermute_rows(yp)
    table = _att_masked_tables(_att_bias_table(p["att_rpb"][0]), T // GRID_W)
    kvt = jnp.swapaxes(zb[:, 2 * W:4 * W], 0, 1)
    ya = _att_fwd(zb, kvt, table)
    tail_w = (p["ssm_d"], full("ssm_w_glu"), p["ssm_b_glu"], full("w_branch_ssm"), full("w_branch_att"), full("w_out"))
    h2 = _merge_fwd(ypre, zb, ya, h1, *tail_w)
    wg2, wu2, wd2 = [full(n) for n in ffn2_w]
    (h3, xn2, g2, u2), _ = _ffn_fwd(h2, p["ffn2_norm"], wg2, wu2, wd2, "ffn2_fwd")
    loss_part, dh3, d_final = _loss_head(h3, p["final_norm"][None], target)

    grads = {"final_norm": d_final[0]}
    to_send = lambda names: _Comm("exchange", [_split_for_devices(grads[n], cut_axis(n)) for n in names])
    parts = {}
    (dh2, grads["ffn2_norm"], do2, a2, dg2, du2), _ = _ffn_bwd(
        dh3, h2, p["ffn2_norm"], g2, u2, wg2, wu2, wd2, "ffn2_bwd")
    grads["ffn2_w_gate"] = _xty(dg2, xn2, "ffn2_dw_gate")
    grads["ffn2_w_up"] = _xty(du2, xn2, "ffn2_dw_up")
    grads["ffn2_w_down"] = _xty(a2, do2, "ffn2_dw_down")
    (dypre, dzs_skip, dgs, dga, dya, grads["ssm_d"], grads["ssm_w_glu"], grads["ssm_b_glu"],
     grads["w_branch_ssm"], grads["w_branch_att"], grads["w_out"]) = _merge_bwd(dh2, ypre, zb, ya, *tail_w)
    (dq, dk, dv, dtable), got = _att_bwd(zb, kvt, ya, dya, table, to_send(ffn2_w))
    parts.update(zip(ffn2_w, got))
    grads["att_rpb"] = _att_bias_table_t(dtable)
    dyp = _permute_rows(dypre)
    tail_names = ("ssm_w_glu", "w_branch_ssm", "w_branch_att", "w_out")
    (dzp, dbre, dbim, dlam, dcre, dcimn), got = _s5_bwd(zp, dyp, bre, bim, lam, cre, cimn, to_send(tail_names))
    parts.update(zip(tail_names, got))
    G, P = SSM_GROUPS, SSM_STATE
    for d, (sfx, raw) in enumerate((("_fwd", raw_f), ("_bwd", raw_b))):
        da_re, da_im, dldt, dbt_re, dbt_im = _disc_bwd(
            *raw, dlam[d, :, :, 0, :].reshape(G, P), dlam[d, :, :, 1, :].reshape(G, P),
            _s5_unpack_b(dbre[d]), _s5_unpack_b(dbim[d]), "s5_disc_grad" + sfx)
        grads["ssm_a_re" + sfx] = da_re
        grads["ssm_a_im" + sfx] = da_im
        grads["ssm_log_dt" + sfx] = dldt[:, 0]
        grads["ssm_b_re" + sfx] = dbt_re.transpose(0, 2, 1)
        grads["ssm_b_im" + sfx] = dbt_im.transpose(0, 2, 1)
        grads["ssm_c_re" + sfx] = _s5_unpack_c(dcre[d])
        grads["ssm_c_im" + sfx] = -_s5_unpack_c(dcimn[d])
    dzs = _unpermute_rows(dzp) + dzs_skip
    dz = jnp.concatenate([dzs.astype(BF16), dq, dk, dv, dgs, dga], axis=1)
    dh1, grads["mix_norm"] = _mixin_bwd(dz, dh2, h1, p["mix_norm"], gathered["w_in"])
    grads["w_in"] = _xty(un, dz, "dw_in", col_shards=N_DEV)
    small_t = lambda a, n: jnp.swapaxes(a, -1, -2) if n.startswith("ssm_b_") else a
    pack_small = lambda names, src, pre: _pack([small_t(src[pre + n], n).astype(F32) for n in names], 8)
    early = _Comm(["exchange", "gather"],
                  [grads["w_in"], pack_small(SMALL_EARLY, grads, "")])
    (dh0, grads["ffn1_norm"], do1, a1, dg1, du1), (parts["w_in"], got_early) = _ffn_bwd(
        dh1, h0, p["ffn1_norm"], g1, u1, wg1, wu1, wd1, "ffn1_bwd", early)
    grads["ffn1_w_down"] = _xty(a1, do1, "ffn1_dw_down")
    grads["ffn1_w_gate"], (parts["ffn1_w_down"],) = _xty(dg1, xn1, "ffn1_dw_gate", to_send(("ffn1_w_down",)))
    grads["ffn1_w_up"], (parts["ffn1_w_gate"],) = _xty(du1, xn1, "ffn1_dw_up", to_send(("ffn1_w_gate",)))
    last = _Comm(["exchange", "gather"],
                 [_split_for_devices(grads["ffn1_w_up"], 0), pack_small(SMALL_LATE, grads, "")])
    parts["ffn1_w_up"], got_late = _comm_call(last, "exchange_last")
    got_small = jnp.concatenate([got_early, got_late], axis=1)

    results = {}
    for n, _ in BIG:
        outs = _adamw(parts[n], *[stored(p[pre + n][0], n) for pre in ("", "m_", "v_")], "adamw_" + n)
        results[n] = [stored(o, n)[None] for o in outs]
    early_rows = got_early.shape[1]
    slab = lambda pre: jnp.concatenate([pack_small(SMALL_EARLY, p, pre), pack_small(SMALL_LATE, p, pre)], axis=0)
    small_out = _adamw(got_small, slab(""), slab("m_"), slab("v_"), "adamw_small")
    for names, rows in ((SMALL_EARLY, slice(0, early_rows)), (SMALL_LATE, slice(early_rows, None))):
        shapes = [small_t(p[n], n).shape for n in names]
        for n, vals in zip(names, zip(*[_unpack(out[rows], shapes) for out in small_out])):
            results[n] = [small_t(val, n) for val in vals]

    loss = lax.psum(loss_part[0, 0], ("x", "y", "c"))
    out = [loss, dh0[None]]
    for kind in range(4):
        out += [results[n][kind] for n in WEIGHTS]
    return tuple(out)
```

```python
import functools
import math

import numpy as np
import jax
import jax.numpy as jnp
from jax import lax
from jax.experimental import pallas as pl
from jax.experimental.pallas import tpu as pltpu

F32 = jnp.float32
BF16 = jnp.bfloat16
MESH_ID = pl.DeviceIdType.MESH

SSM_GROUP = 16
SSM_GROUPS = 32
SSM_STATE = 64
SSM_WIDTH = 512
ATT_HEADS = 8
ATT_HEAD_DIM = 64
ATT_WIDTH = 512
GRID_W = 64
WIN_H = 8
WIN_W = 16
EPS = 1e-6
NEG_INF = -1e30
ADAM_LR = 0.001
ADAM_B1 = 0.9
ADAM_B2 = 0.999
ADAM_EPS = 1e-08
ADAM_WD = 0.01
ADAM_STEP = 10

N_DEV = 8
V7X_VMEM_BYTES = 64 * 1024 * 1024
VMEM_LIMIT = V7X_VMEM_BYTES - 8 * 1024 * 1024
SCAN_LANES = 8
ATT_ROWS = 4


def _cparams(sem, vmem=None):
    return pltpu.CompilerParams(dimension_semantics=sem, vmem_limit_bytes=vmem)


def _dot(a, b):
    return jnp.dot(a, b, preferred_element_type=F32)


def _dot_nt(a, b):
    return lax.dot_general(a, b, (((1,), (1,)), ((), ())), preferred_element_type=F32)


def _dot_tn(a, b):
    return lax.dot_general(a, b, (((0,), (0,)), ((), ())), preferred_element_type=F32)


def _rms(h):
    return lax.rsqrt(jnp.mean(h * h, axis=-1, keepdims=True) + EPS)


def _rms_bwd(h, r, v):
    return r * v - h * (r * r * r) * jnp.mean(h * v, axis=-1, keepdims=True)


def _col_sum(x):
    return jnp.sum(x, axis=0, keepdims=True)


def _my_place():
    return lax.axis_index("x"), lax.axis_index("y"), lax.axis_index("c")


def _flat(px, py, pc):
    return 4 * px + 2 * py + pc


class _Comm:
    def __init__(self, kind, arrays):
        self.arrays = list(arrays)
        self.n = len(self.arrays)
        self.kinds = [kind] * self.n if isinstance(kind, str) else list(kind)

    def out_shapes(self):
        return [jax.ShapeDtypeStruct((N_DEV,) + a.shape if k == "gather" else a.shape, a.dtype)
                for k, a in zip(self.kinds, self.arrays)]

    def scratch(self):
        return [pltpu.SemaphoreType.DMA((7 * self.n,)), pltpu.SemaphoreType.DMA((7 * self.n,)),
                pltpu.SemaphoreType.DMA((self.n,))]

    def run(self, srcs, dsts, sems, start):
        send_sems, recv_sems, local_sems = sems
        x, y, c = _my_place()
        mine = _flat(x, y, c)
        for a, (src, dst) in enumerate(zip(srcs, dsts)):
            whole = self.kinds[a] == "gather"
            local = pltpu.make_async_copy(src if whole else src.at[mine], dst.at[mine], local_sems.at[a])
            local.start() if start else local.wait()
            for k in range(1, N_DEV):
                px = 1 - x if k & 4 else x
                py = 1 - y if k & 2 else y
                pc = 1 - c if k & 1 else c
                cp = pltpu.make_async_remote_copy(
                    src_ref=src if whole else src.at[_flat(px, py, pc)], dst_ref=dst.at[mine],
                    send_sem=send_sems.at[7 * a + k - 1], recv_sem=recv_sems.at[7 * a + k - 1],
                    device_id=(px, py, pc), device_id_type=MESH_ID)
                cp.start() if start else cp.wait()


_HBM = pl.BlockSpec(memory_space=pltpu.HBM)


def _comm_call(comm, name):
    def body(*refs):
        srcs, dsts, sems = refs[:comm.n], refs[comm.n:2 * comm.n], refs[2 * comm.n:]
        comm.run(srcs, dsts, sems, True)
        comm.run(srcs, dsts, sems, False)

    return pl.pallas_call(body, name=name, in_specs=[_HBM] * comm.n, out_specs=[_HBM] * comm.n,
                          out_shape=comm.out_shapes(), scratch_shapes=comm.scratch())(*comm.arrays)


def _pallas(core, name, grid, in_specs, out_specs, out_shape, scratch, sem, args, comm=None):
    if comm is None:
        out = pl.pallas_call(core, name=name, grid=grid, in_specs=in_specs, out_specs=out_specs,
                             out_shape=out_shape, scratch_shapes=scratch,
                             compiler_params=_cparams(sem, VMEM_LIMIT))(*args)
        return out, []
    n_in, n_out, n_scr, n = len(in_specs), len(out_specs), len(scratch), comm.n

    def body(*refs):
        ins, srcs = refs[:n_in], refs[n_in:n_in + n]
        outs, dsts = refs[n_in + n:n_in + n + n_out], refs[n_in + n + n_out:n_in + 2 * n + n_out]
        scr, sems = refs[n_in + 2 * n + n_out:n_in + 2 * n + n_out + n_scr], refs[n_in + 2 * n + n_out + n_scr:]
        ids = [pl.program_id(k) for k in range(len(grid))]
        first = functools.reduce(lambda a, b: a & b, [i == 0 for i in ids])
        last = functools.reduce(lambda a, b: a & b, [i == g - 1 for i, g in zip(ids, grid)])

        @pl.when(first)
        def _():
            comm.run(srcs, dsts, sems, True)

        core(*ins, *outs, *scr)

        @pl.when(last)
        def _():
            comm.run(srcs, dsts, sems, False)

    out = pl.pallas_call(
        body, name=name, grid=grid, in_specs=list(in_specs) + [_HBM] * n, out_specs=list(out_specs) + [_HBM] * n,
        out_shape=list(out_shape) + comm.out_shapes(), scratch_shapes=list(scratch) + comm.scratch(),
        compiler_params=_cparams(("arbitrary",) * len(grid), VMEM_LIMIT))(*args, *comm.arrays)
    return out[:n_out], out[n_out:]


FFN_TM = 256


def _ffn_fwd(h, gain, wg, wu, wd, name, comm=None):
    T, D = h.shape
    F = wg.shape[0]
    tm = min(T, FFN_TM)
    once = pl.Buffered(1)

    def body(h_ref, gain_ref, wg_ref, wu_ref, wd_ref, ho_ref, xn_ref, g_ref, u_ref):
        hh = h_ref[...]
        xn = (hh * _rms(hh) * gain_ref[...]).astype(BF16)
        xn_ref[...] = xn
        g = _dot_nt(xn, wg_ref[...])
        u = _dot_nt(xn, wu_ref[...])
        g_ref[...] = g.astype(BF16)
        u_ref[...] = u.astype(BF16)
        a = (g * jax.nn.sigmoid(g) * u).astype(BF16)
        ho_ref[...] = hh + 0.5 * _dot(a, wd_ref[...])

    return _pallas(
        body, name, (T // tm,),
        [pl.BlockSpec((tm, D), lambda i: (i, 0)),
         pl.BlockSpec((1, D), lambda i: (0, 0)),
         pl.BlockSpec((F, D), lambda i: (0, 0), pipeline_mode=once),
         pl.BlockSpec((F, D), lambda i: (0, 0), pipeline_mode=once),
         pl.BlockSpec((F, D), lambda i: (0, 0), pipeline_mode=once)],
        [pl.BlockSpec((tm, D), lambda i: (i, 0)),
         pl.BlockSpec((tm, D), lambda i: (i, 0)),
         pl.BlockSpec((tm, F), lambda i: (i, 0)),
         pl.BlockSpec((tm, F), lambda i: (i, 0))],
        [jax.ShapeDtypeStruct((T, D), F32), jax.ShapeDtypeStruct((T, D), BF16),
         jax.ShapeDtypeStruct((T, F), BF16), jax.ShapeDtypeStruct((T, F), BF16)],
        [], ("parallel",), (h, gain, wg, wu, wd), comm)


def _ffn_bwd(dho, h, gain, g, u, wg, wu, wd, name, comm=None):
    T, D = h.shape
    F = wg.shape[0]
    tm = min(T, FFN_TM)
    tf = 1408 if F % 1408 == 0 else F
    once = pl.Buffered(1)

    def body(dho_ref, h_ref, gain_ref, g_ref, u_ref, wg_ref, wu_ref, wd_ref,
             dh_ref, dgain_ref, do_ref, a_ref, dg_ref, du_ref):
        @pl.when(pl.program_id(0) == 0)
        def _():
            dgain_ref[...] = jnp.zeros_like(dgain_ref)

        dho_v = dho_ref[...]
        do = (0.5 * dho_v).astype(BF16)
        do_ref[...] = do
        dxn = None
        for c in range(F // tf):
            cs = slice(c * tf, (c + 1) * tf)
            da = _dot_nt(do, wd_ref[cs, :])
            gg = g_ref[:, cs].astype(F32)
            uu = u_ref[:, cs].astype(F32)
            s = jax.nn.sigmoid(gg)
            sl = gg * s
            a_ref[:, cs] = (sl * uu).astype(BF16)
            dg = (da * uu * (s * (1.0 + gg * (1.0 - s)))).astype(BF16)
            du = (da * sl).astype(BF16)
            dg_ref[:, cs] = dg
            du_ref[:, cs] = du
            part = _dot(dg, wg_ref[cs, :]) + _dot(du, wu_ref[cs, :])
            dxn = part if dxn is None else dxn + part
        hh = h_ref[...]
        r = _rms(hh)
        dgain_ref[...] += _col_sum(dxn * hh * r)
        dh_ref[...] = dho_v + _rms_bwd(hh, r, dxn * gain_ref[...])

    tok = lambda w: pl.BlockSpec((tm, w), lambda i: (i, 0))
    row = pl.BlockSpec((1, D), lambda i: (0, 0))
    weight = pl.BlockSpec((F, D), lambda i: (0, 0), pipeline_mode=once)
    return _pallas(
        body, name, (T // tm,),
        [tok(D), tok(D), row, tok(F), tok(F), weight, weight, weight],
        [tok(D), row, tok(D), tok(F), tok(F), tok(F)],
        [jax.ShapeDtypeStruct((T, D), F32), jax.ShapeDtypeStruct((1, D), F32),
         jax.ShapeDtypeStruct((T, D), BF16), jax.ShapeDtypeStruct((T, F), BF16),
         jax.ShapeDtypeStruct((T, F), BF16), jax.ShapeDtypeStruct((T, F), BF16)],
        [], ("arbitrary",), (dho, h, gain, g, u, wg, wu, wd), comm)


def _xty(x, y, name, comm=None, col_shards=1):
    T, K = x.shape
    N = y.shape[1]
    tt = min(T, 2048)
    tk = K if K <= 1024 else (1408 if K % 1408 == 0 else K)
    tn = N if N <= 1024 else (1408 if N % 1408 == 0 else (1024 if N % 1024 == 0 else N))
    nt = T // tt
    ws = N // col_shards
    per = tn // ws if col_shards > 1 else 1
    assert col_shards == 1 or (tn % ws == 0 and ws % 128 == 0)

    def body(x_ref, y_ref, o_ref, acc_ref):
        t = pl.program_id(2)

        @pl.when(t == 0)
        def _():
            acc_ref[...] = jnp.zeros_like(acc_ref)

        acc_ref[...] += _dot_tn(x_ref[...], y_ref[...])

        @pl.when(t == nt - 1)
        def _():
            if col_shards == 1:
                o_ref[...] = acc_ref[...].astype(BF16)
            else:
                for s in range(per):
                    o_ref[s] = acc_ref[:, s * ws:(s + 1) * ws].astype(BF16)

    if col_shards == 1:
        out_spec = pl.BlockSpec((tk, tn), lambda k, n, t: (k, n))
        out_shape = jax.ShapeDtypeStruct((K, N), BF16)
    else:
        out_spec = pl.BlockSpec((per, tk, ws), lambda k, n, t: (n, k, 0))
        out_shape = jax.ShapeDtypeStruct((col_shards, K, ws), BF16)
    (out,), got = _pallas(
        body, name, (K // tk, N // tn, nt),
        [pl.BlockSpec((tt, tk), lambda k, n, t: (t, k)), pl.BlockSpec((tt, tn), lambda k, n, t: (t, n))],
        [out_spec], [out_shape], [pltpu.VMEM((tk, tn), F32)],
        ("parallel", "parallel", "arbitrary"), (x, y), comm)
    return out if comm is None else (out, got)


def _mixin_fwd(h, gain, w_in):
    T, D = h.shape
    nn, _, tn = w_in.shape
    N = nn * tn
    tm = min(T, 512)

    def body(h_ref, gain_ref, w_ref, zb_ref, un_ref):
        hh = h_ref[...]
        un = (hh * _rms(hh) * gain_ref[...]).astype(BF16)
        un_ref[...] = un
        for s in range(nn):
            zb_ref[:, s * tn:(s + 1) * tn] = _dot(un, w_ref[s]).astype(BF16)

    return pl.pallas_call(
        body, name="mixin_fwd", grid=(T // tm,),
        in_specs=[pl.BlockSpec((tm, D), lambda i: (i, 0)),
                  pl.BlockSpec((1, D), lambda i: (0, 0)),
                  pl.BlockSpec((nn, D, tn), lambda i: (0, 0, 0))],
        out_specs=[pl.BlockSpec((tm, N), lambda i: (i, 0)),
                   pl.BlockSpec((tm, D), lambda i: (i, 0))],
        out_shape=[jax.ShapeDtypeStruct((T, N), BF16), jax.ShapeDtypeStruct((T, D), BF16)],
        compiler_params=_cparams(("parallel",), VMEM_LIMIT),
    )(h, gain, w_in)


def _mixin_bwd(dz, dh_res, h, gain, w_in):
    T, D = h.shape
    nn, _, tn = w_in.shape
    tm = min(T, 512)

    def body(dz_ref, dres_ref, h_ref, gain_ref, w_ref, dh_ref, dgain_ref):
        @pl.when(pl.program_id(0) == 0)
        def _():
            dgain_ref[...] = jnp.zeros_like(dgain_ref)

        dun = _dot_nt(dz_ref[:, 0:tn], w_ref[0])
        for s in range(1, nn):
            dun = dun + _dot_nt(dz_ref[:, s * tn:(s + 1) * tn], w_ref[s])
        hh = h_ref[...]
        r = _rms(hh)
        dgain_ref[...] += _col_sum(dun * hh * r)
        dh_ref[...] = dres_ref[...] + _rms_bwd(hh, r, dun * gain_ref[...])

    return pl.pallas_call(
        body, name="mixin_bwd", grid=(T // tm,),
        in_specs=[pl.BlockSpec((tm, nn * tn), lambda i: (i, 0)),
                  pl.BlockSpec((tm, D), lambda i: (i, 0)),
                  pl.BlockSpec((tm, D), lambda i: (i, 0)),
                  pl.BlockSpec((1, D), lambda i: (0, 0)),
                  pl.BlockSpec((nn, D, tn), lambda i: (0, 0, 0))],
        out_specs=[pl.BlockSpec((tm, D), lambda i: (i, 0)),
                   pl.BlockSpec((1, D), lambda i: (0, 0))],
        out_shape=[jax.ShapeDtypeStruct((T, D), F32), jax.ShapeDtypeStruct((1, D), F32)],
        compiler_params=_cparams(("arbitrary",), VMEM_LIMIT),
    )(dz, dh_res, h, gain, w_in)


def _loss_head(h, gain, target):
    T, D = h.shape
    tm = min(T, 1024)

    def body(h_ref, gain_ref, t_ref, loss_ref, dh_ref, dgain_ref):
        @pl.when(pl.program_id(0) == 0)
        def _():
            loss_ref[...] = jnp.zeros_like(loss_ref)
            dgain_ref[...] = jnp.zeros_like(dgain_ref)

        hh = h_ref[...]
        r = _rms(hh)
        e = hh * r * gain_ref[...] - t_ref[...]
        loss_ref[...] += (0.5 / D) * jnp.sum(e * e)
        dy = e * (1.0 / D)
        dgain_ref[...] += _col_sum(dy * hh * r)
        dh_ref[...] = _rms_bwd(hh, r, dy * gain_ref[...])

    return pl.pallas_call(
        body, name="loss_head", grid=(T // tm,),
        in_specs=[pl.BlockSpec((tm, D), lambda i: (i, 0)),
                  pl.BlockSpec((1, D), lambda i: (0, 0)),
                  pl.BlockSpec((tm, D), lambda i: (i, 0))],
        out_specs=[pl.BlockSpec((1, 128), lambda i: (0, 0)),
                   pl.BlockSpec((tm, D), lambda i: (i, 0)),
                   pl.BlockSpec((1, D), lambda i: (0, 0))],
        out_shape=[jax.ShapeDtypeStruct((1, 128), F32), jax.ShapeDtypeStruct((T, D), F32),
                   jax.ShapeDtypeStruct((1, D), F32)],
        compiler_params=_cparams(("arbitrary",), VMEM_LIMIT),
    )(h, gain, target)


def _adamw(parts, w, m, v, name):
    R, C = w.shape
    mult = 16 if parts.dtype == BF16 else 8
    tr = max(t for t in range(mult, min(R, 512) + 1, mult) if R % t == 0)
    c1 = 1.0 - ADAM_B1 ** ADAM_STEP
    c2 = 1.0 - ADAM_B2 ** ADAM_STEP

    def body(p_ref, w_ref, m_ref, v_ref, g_ref, d_ref, nm_ref, nv_ref):
        g = p_ref[0].astype(F32)
        for k in range(1, N_DEV):
            g = g + p_ref[k].astype(F32)
        mm = ADAM_B1 * m_ref[...] + (1.0 - ADAM_B1) * g
        vv = ADAM_B2 * v_ref[...] + (1.0 - ADAM_B2) * (g * g)
        g_ref[...] = g
        nm_ref[...] = mm
        nv_ref[...] = vv
        d_ref[...] = -ADAM_LR * ((mm / c1) / (jnp.sqrt(vv / c2) + ADAM_EPS) + ADAM_WD * w_ref[...])

    spec = pl.BlockSpec((tr, C), lambda i: (i, 0))
    return pl.pallas_call(
        body, name=name, grid=(R // tr,),
        in_specs=[pl.BlockSpec((N_DEV, tr, C), lambda i: (0, i, 0)), spec, spec, spec],
        out_specs=[spec, spec, spec, spec],
        out_shape=[jax.ShapeDtypeStruct((R, C), F32)] * 4,
        compiler_params=_cparams(("parallel",), VMEM_LIMIT),
    )(parts, w, m, v)


S5_NS = 256
S5_NH = 2
S5_NCB = 4
S5_RC = 4096
S5_NQ = 4
S5_GROUP = 2


def _disc_math(a_re, a_im, log_dt, bt_re, bt_im):
    dt = jnp.exp(log_dt)
    zr, zi = a_re * dt, a_im * dt
    mag = jnp.exp(zr)
    lb_re, lb_im = mag * jnp.cos(zi), mag * jnp.sin(zi)
    den = a_re * a_re + a_im * a_im
    nr, ni = lb_re - 1.0, lb_im
    f_re = (nr * a_re + ni * a_im) / den
    f_im = (ni * a_re - nr * a_im) / den
    bb_re = f_re[:, None, :] * bt_re - f_im[:, None, :] * bt_im
    bb_im = f_re[:, None, :] * bt_im + f_im[:, None, :] * bt_re
    return lb_re, lb_im, bb_re, bb_im


def _disc_fwd(a_re, a_im, log_dt, bt_re, bt_im, chain_len, name):
    G, P = a_re.shape
    C = bt_re.shape[1]
    n_sq = int(round(math.log2(chain_len)))
    assert 2 ** n_sq == chain_len

    def body(a_re_ref, a_im_ref, ldt_ref, br_ref, bi_ref, lr_ref, li_ref, sr_ref, si_ref, bbr_ref, bbi_ref):
        lr, li, bbr, bbi = _disc_math(a_re_ref[...], a_im_ref[...], ldt_ref[...], br_ref[...], bi_ref[...])
        lr_ref[...] = lr
        li_ref[...] = li
        bbr_ref[...] = bbr
        bbi_ref[...] = bbi
        pr, pi = lr, li
        for _ in range(n_sq):
            pr, pi = pr * pr - pi * pi, 2.0 * pr * pi
        sr_ref[...] = pr
        si_ref[...] = pi

    s2 = jax.ShapeDtypeStruct((G, P), F32)
    s3 = jax.ShapeDtypeStruct((G, C, P), F32)
    return pl.pallas_call(body, name=name, out_shape=[s2, s2, s2, s2, s3, s3])(a_re, a_im, log_dt, bt_re, bt_im)


def _disc_bwd(a_re, a_im, log_dt, bt_re, bt_im, d_lr, d_li, d_bbr, d_bbi, name):
    G, P = a_re.shape
    C = bt_re.shape[1]

    def body(a_re_ref, a_im_ref, ldt_ref, br_ref, bi_ref, c1, c2, c3, c4, o1, o2, o3, o4, o5):
        _, vjp = jax.vjp(_disc_math, a_re_ref[...], a_im_ref[...], ldt_ref[...], br_ref[...], bi_ref[...])
        o1[...], o2[...], o3[...], o4[...], o5[...] = vjp((c1[...], c2[...], c3[...], c4[...]))

    s2 = jax.ShapeDtypeStruct((G, P), F32)
    s3 = jax.ShapeDtypeStruct((G, C, P), F32)
    return pl.pallas_call(body, name=name, out_shape=[s2, s2, jax.ShapeDtypeStruct((G, 1), F32), s3, s3])(
        a_re, a_im, log_dt, bt_re, bt_im, d_lr, d_li, d_bbr, d_bbi)


def _row_block(ib):
    return pl.ds(pl.multiple_of(ib * SCAN_LANES, SCAN_LANES), SCAN_LANES)


def _chain_block(j, i, ascending, n_blocks):
    at = j * (n_blocks // S5_NQ) + i
    return _row_block(jnp.where(ascending, at, n_blocks - 1 - at))


def _unrolled_loop(n, unroll, body, carry):
    trips = n // unroll
    carry = lax.fori_loop(
        0, trips, lambda t, c: functools.reduce(lambda cc, u: body(t * unroll + u, cc), range(unroll), c), carry)
    for i in range(trips * unroll, n):
        carry = body(i, carry)
    return carry


def _cmul_add(lr, li, sr, si, xr, xi):
    return lr * sr - li * si + xr, lr * si + li * sr + xi


def _scan(xr_ref, xi_ref, lr, li, init, ascending, n_blocks, store):
    steps = n_blocks // S5_NQ
    if not store:
        def step(i, carry):
            blocks = [_chain_block(j, i, ascending, n_blocks) for j in range(S5_NQ)]
            return tuple(_cmul_add(lr, li, sr, si, xr_ref[rows, :], xi_ref[rows, :])
                         for (sr, si), rows in zip(carry, blocks))

        return _unrolled_loop(steps, 4, step, init)

    group = S5_GROUP
    assert steps % group == 0

    def trip(t, carry):
        blocks = [[_chain_block(j, t * group + u, ascending, n_blocks) for j in range(S5_NQ)] for u in range(group)]
        xs = [[(xr_ref[rows, :], xi_ref[rows, :]) for rows in row] for row in blocks]
        states = list(carry)
        done = []
        for u in range(group):
            states = [_cmul_add(lr, li, sr, si, xr, xi) for (sr, si), (xr, xi) in zip(states, xs[u])]
            done.append(states)
        for u in range(group):
            for rows, (nr, ni) in zip(blocks[u], done[u]):
                xr_ref[rows, :] = nr
                xi_ref[rows, :] = ni
        return tuple(states)

    return lax.fori_loop(0, steps // group, trip, init)


def _segment_starts(w, lsr, lsi, ascending):
    shape = w[0][0].shape
    row = lax.broadcasted_iota(jnp.int32, shape, 0)
    keep = row != jnp.where(ascending, 0, SCAN_LANES - 1)

    def shift(t):
        t = jnp.where(ascending, pltpu.roll(t, 1, 0), pltpu.roll(t, SCAN_LANES - 1, 0))
        return jnp.where(keep, t, 0.0)

    zero = jnp.zeros(shape, F32)
    c = [(zero, zero)] * S5_NQ
    for _ in range(SCAN_LANES):
        tr, ti = _cmul_add(lsr, lsi, *c[-1], *w[-1])
        c[0] = (shift(tr), shift(ti))
        for j in range(1, S5_NQ):
            c[j] = _cmul_add(lsr, lsi, *c[j - 1], *w[j - 1])
    return tuple(c)


def _first_pass(xr_ref, xi_ref, lam_ref, ascending, n_blocks, conj):
    shape = (SCAN_LANES, xr_ref.shape[1])
    sign = -1.0 if conj else 1.0
    lr = jnp.broadcast_to(lam_ref[0:1, :], shape)
    li = sign * jnp.broadcast_to(lam_ref[1:2, :], shape)
    lsr = jnp.broadcast_to(lam_ref[2:3, :], shape)
    lsi = sign * jnp.broadcast_to(lam_ref[3:4, :], shape)
    zero = jnp.zeros(shape, F32)
    w = _scan(xr_ref, xi_ref, lr, li, ((zero, zero),) * S5_NQ, ascending, n_blocks, store=False)
    return _segment_starts(w, lsr, lsi, ascending), lr, li


def _s5_specs(T):
    NS = S5_NS
    tok = pl.BlockSpec((T, 128), lambda c, d, h: (0, c))
    b_spec = pl.BlockSpec((None, None, None, 128, NS), lambda c, d, h: (d, c, h, 0, 0))
    c_spec = pl.BlockSpec((None, None, None, NS, 128), lambda c, d, h: (d, c, h, 0, 0))
    lam_spec = pl.BlockSpec((None, None, None, 4, NS), lambda c, d, h: (d, c, h, 0, 0))
    return tok, b_spec, c_spec, lam_spec


def _s5_fwd(zp, bre, bim, lam, cre, cimn, comm=None):
    T = zp.shape[0]
    NS = S5_NS
    nb = T // SCAN_LANES
    rc = min(S5_RC, T)
    tok, b_spec, c_spec, lam_spec = _s5_specs(T)

    def body(zp_ref, bre_ref, bim_ref, lam_ref, cre_ref, cim_ref, y_ref, xr_ref, xi_ref):
        d = pl.program_id(1)
        ascending = d == 0

        @pl.when((d == 0) & (pl.program_id(2) == 0))
        def _():
            y_ref[...] = jnp.zeros_like(y_ref)

        def proj(c, _):
            rows = pl.ds(pl.multiple_of(c * rc, rc), rc)
            zz = zp_ref[rows, :]
            xr_ref[rows, :] = _dot(zz, bre_ref[...])
            xi_ref[rows, :] = _dot(zz, bim_ref[...])
            return 0

        lax.fori_loop(0, T // rc, proj, 0)
        starts, lr, li = _first_pass(xr_ref, xi_ref, lam_ref, ascending, nb, conj=False)
        _scan(xr_ref, xi_ref, lr, li, starts, ascending, nb, store=True)

        def outp(c, _):
            rows = pl.ds(pl.multiple_of(c * rc, rc), rc)
            y_ref[rows, :] += (_dot(xr_ref[rows, :].astype(BF16), cre_ref[...])
                               + _dot(xi_ref[rows, :].astype(BF16), cim_ref[...]))
            return 0

        lax.fori_loop(0, T // rc, outp, 0)

    return _pallas(
        body, "s5_fwd", (S5_NCB, 2, S5_NH),
        [tok, b_spec, b_spec, lam_spec, c_spec, c_spec], [tok],
        [jax.ShapeDtypeStruct((T, SSM_WIDTH), F32)],
        [pltpu.VMEM((T, NS), F32), pltpu.VMEM((T, NS), F32)],
        ("parallel", "arbitrary", "arbitrary"), (zp, bre, bim, lam, cre, cimn), comm)


def _s5_bwd(zp, dyp, bre, bim, lam, cre, cimn, comm=None):
    T = zp.shape[0]
    NS, NH = S5_NS, S5_NH
    nb = T // SCAN_LANES
    rc = min(S5_RC, T)
    tok, b_spec, c_spec, lam_spec = _s5_specs(T)
    dlam_spec = pl.BlockSpec((None, None, None, 2, NS), lambda c, d, h: (d, c, h, 0, 0))

    def body(zp_ref, dyp_ref, bre_ref, bim_ref, lam_ref, cre_ref, cim_ref,
             dzp_ref, dbre_ref, dbim_ref, dlam_ref, dcre_ref, dcim_ref,
             sr_ref, si_ref, gr_ref, gi_ref):
        d = pl.program_id(1)
        ascending = d == 0
        g_ascending = d != 0

        @pl.when((d == 0) & (pl.program_id(2) == 0))
        def _():
            dzp_ref[...] = jnp.zeros_like(dzp_ref)

        dcre_ref[...] = jnp.zeros_like(dcre_ref)
        dcim_ref[...] = jnp.zeros_like(dcim_ref)
        dbre_ref[...] = jnp.zeros_like(dbre_ref)
        dbim_ref[...] = jnp.zeros_like(dbim_ref)

        def proj(c, _):
            rows = pl.ds(pl.multiple_of(c * rc, rc), rc)
            zz = zp_ref[rows, :]
            sr_ref[rows, :] = _dot(zz, bre_ref[...])
            si_ref[rows, :] = _dot(zz, bim_ref[...])
            dy = dyp_ref[rows, :]
            gr_ref[rows, :] = _dot_nt(dy, cre_ref[...])
            gi_ref[rows, :] = _dot_nt(dy, cim_ref[...])
            return 0

        lax.fori_loop(0, T // rc, proj, 0)
        s_starts, lr, li = _first_pass(sr_ref, si_ref, lam_ref, ascending, nb, conj=False)
        _scan(sr_ref, si_ref, lr, li, s_starts, ascending, nb, store=True)
        g_starts, lr, lic = _first_pass(gr_ref, gi_ref, lam_ref, g_ascending, nb, conj=True)

        steps = nb // S5_NQ
        group = S5_GROUP
        assert steps % group == 0

        def gtrip(t, carry, last):
            g, (ar, ai) = carry
            first = t * group
            blocks = [[_chain_block(j, first + u, g_ascending, nb) for j in range(S5_NQ)] for u in range(group)]
            direct = [[(gr_ref[rows, :], gi_ref[rows, :]) for rows in row] for row in blocks]
            done = []
            for u in range(group):
                new = []
                for j, ((g_r, g_i), (d_r, d_i)) in enumerate(zip(g, direct[u])):
                    n_r, n_i = _cmul_add(lr, lic, g_r, g_i, d_r, d_i)
                    if last and u == group - 1:
                        s_r, s_i = s_starts[S5_NQ - 1 - j]
                    else:
                        prev = _chain_block(j, first + u + 1, g_ascending, nb)
                        s_r, s_i = sr_ref[prev, :], si_ref[prev, :]
                    ar = ar + n_r * s_r + n_i * s_i
                    ai = ai + n_i * s_r - n_r * s_i
                    new.append((n_r, n_i))
                g = new
                done.append(new)
            for u in range(group):
                for rows, (n_r, n_i) in zip(blocks[u], done[u]):
                    gr_ref[rows, :] = n_r
                    gi_ref[rows, :] = n_i
            return tuple(g), (ar, ai)

        zero = jnp.zeros((SCAN_LANES, NS), F32)
        carry = lax.fori_loop(0, steps // group - 1, lambda t, c: gtrip(t, c, False), (g_starts, (zero, zero)))
        _, (ar, ai) = gtrip(steps // group - 1, carry, True)
        dlam_ref[0:1, :] = _col_sum(ar)
        dlam_ref[1:2, :] = _col_sum(ai)

        def grads(c, _):
            rows = pl.ds(pl.multiple_of(c * rc, rc), rc)
            zz = zp_ref[rows, :]
            dy = dyp_ref[rows, :]
            g_rb = gr_ref[rows, :].astype(BF16)
            g_ib = gi_ref[rows, :].astype(BF16)
            dcre_ref[...] += _dot_tn(sr_ref[rows, :].astype(BF16), dy)
            dcim_ref[...] += _dot_tn(si_ref[rows, :].astype(BF16), dy)
            dbre_ref[...] += _dot_tn(zz, g_rb)
            dbim_ref[...] += _dot_tn(zz, g_ib)
            dzp_ref[rows, :] += _dot_nt(g_rb, bre_ref[...]) + _dot_nt(g_ib, bim_ref[...])
            return 0

        lax.fori_loop(0, T // rc, grads, 0)

    f32 = lambda *s: jax.ShapeDtypeStruct(s, F32)
    return _pallas(
        body, "s5_bwd", (S5_NCB, 2, S5_NH),
        [tok, tok, b_spec, b_spec, lam_spec, c_spec, c_spec],
        [tok, b_spec, b_spec, dlam_spec, c_spec, c_spec],
        [f32(T, SSM_WIDTH), f32(2, S5_NCB, NH, 128, NS), f32(2, S5_NCB, NH, 128, NS),
         f32(2, S5_NCB, NH, 2, NS), f32(2, S5_NCB, NH, NS, 128), f32(2, S5_NCB, NH, NS, 128)],
        [pltpu.VMEM((T, NS), F32)] * 4,
        ("parallel", "arbitrary", "arbitrary"), (zp, dyp, bre, bim, lam, cre, cimn), comm)


def _s5_delta():
    d = np.zeros((S5_NH, 8, 8 // S5_NH), np.float32)
    for h in range(S5_NH):
        for go in range(8 // S5_NH):
            d[h, h * (8 // S5_NH) + go, go] = 1.0
    return d


def _s5_pack_b(bbt):
    gh = 8 // S5_NH
    b5 = bbt.reshape(S5_NCB, S5_NH, gh, SSM_GROUP, SSM_STATE).transpose(0, 1, 3, 2, 4)
    m = b5[:, :, None] * _s5_delta()[None, :, :, None, :, None]
    return m.reshape(S5_NCB, S5_NH, 128, S5_NS)


def _s5_unpack_b(dm):
    gh = 8 // S5_NH
    d6 = dm.reshape(S5_NCB, S5_NH, 8, SSM_GROUP, gh, SSM_STATE)
    b5 = jnp.sum(d6 * _s5_delta()[None, :, :, None, :, None], axis=2)
    return b5.transpose(0, 1, 3, 2, 4).reshape(SSM_GROUPS, SSM_GROUP, SSM_STATE)


def _s5_pack_c(c):
    gh = 8 // S5_NH
    c5 = c.reshape(S5_NCB, S5_NH, gh, SSM_GROUP, SSM_STATE).transpose(0, 1, 2, 4, 3)
    m = c5[:, :, :, :, None, :] * _s5_delta().transpose(0, 2, 1)[None, :, :, None, :, None]
    return m.reshape(S5_NCB, S5_NH, S5_NS, 128)


def _s5_unpack_c(dm):
    gh = 8 // S5_NH
    d6 = dm.reshape(S5_NCB, S5_NH, gh, SSM_STATE, 8, SSM_GROUP)
    c5 = jnp.sum(d6 * _s5_delta().transpose(0, 2, 1)[None, :, :, None, :, None], axis=4)
    return c5.transpose(0, 1, 2, 4, 3).reshape(SSM_GROUPS, SSM_GROUP, SSM_STATE)


def _s5_pack_lam(x):
    return x.reshape(S5_NCB, S5_NH, S5_NS)


def _permute_rows(x):
    T = x.shape[0]
    return x.reshape(SCAN_LANES, T // SCAN_LANES, -1).transpose(1, 0, 2).reshape(T, -1)


def _unpermute_rows(x):
    T = x.shape[0]
    return x.reshape(T // SCAN_LANES, SCAN_LANES, -1).transpose(1, 0, 2).reshape(T, -1)


ATT_TB = ATT_ROWS * GRID_W
ATT_KB = 3 * ATT_TB


def _att_valid(i, n_rows):
    qi, kj = np.meshgrid(np.arange(ATT_TB), np.arange(ATT_KB), indexing="ij")
    r = i * ATT_ROWS + qi // GRID_W
    c = qi % GRID_W
    rk = (i - 1) * ATT_ROWS + kj // GRID_W
    x = kj % GRID_W
    rs = np.clip(r - WIN_H // 2, 0, n_rows - WIN_H)
    cs = np.clip(c - WIN_W // 2, 0, GRID_W - WIN_W)
    return (rk >= rs) & (rk < rs + WIN_H) & (x >= cs) & (x < cs + WIN_W)


def _att_masked_tables(table, n_rows):
    n = n_rows // ATT_ROWS
    assert n >= 3
    masks = np.stack([_att_valid(i, n_rows) for i in (0, 1, n - 1)])
    return jnp.where(masks[:, None], table[None], NEG_INF)


def _att_variant(i, n):
    return jnp.where(i == 0, 0, jnp.where(i >= n - 1, 2, 1))


def _att_exp(qh, kh, bias):
    s = _dot_nt(qh, kh) + bias
    return jnp.exp(s - jnp.max(s, axis=1, keepdims=True))


def _att_values_and_ones(vh):
    return jnp.concatenate([vh, jnp.ones_like(vh)], axis=1)


def _att_specs(n, col):
    last = n - 1
    cur = lambda i: (jnp.minimum(i, last), col)
    prv = lambda i: (jnp.maximum(jnp.minimum(i, last) - 1, 0), col)
    nxt = lambda i: (jnp.minimum(i + 1, last), col)
    blk = lambda f: pl.BlockSpec((ATT_TB, ATT_WIDTH), f)
    return blk(cur), blk(prv), blk(nxt)


def _att_fwd(zb, biasv):
    T = zb.shape[0]
    W = ATT_WIDTH
    n = T // ATT_TB
    n_rows = T // GRID_W
    cur = _att_specs(n, 0)[0]
    q_cur = _att_specs(n, 1)[0]
    k_cur, k_prv, k_nxt = _att_specs(n, 2)
    v_cur, v_prv, v_nxt = _att_specs(n, 3)

    def body(q_ref, kp_ref, kc_ref, kn_ref, vp_ref, vc_ref, vn_ref, b_ref, y_ref):
        outs = []
        for h in range(ATT_HEADS):
            hs = slice(h * ATT_HEAD_DIM, (h + 1) * ATT_HEAD_DIM)
            kh = jnp.concatenate([kp_ref[:, hs], kc_ref[:, hs], kn_ref[:, hs]], axis=0)
            vh = jnp.concatenate([vp_ref[:, hs], vc_ref[:, hs], vn_ref[:, hs]], axis=0)
            e = _att_exp(q_ref[:, hs] * 0.125, kh, b_ref[h]).astype(BF16)
            ov = _dot(e, _att_values_and_ones(vh))
            outs.append(ov[:, :ATT_HEAD_DIM] * (1.0 / ov[:, ATT_HEAD_DIM:ATT_HEAD_DIM + 1]))
        y_ref[...] = jnp.concatenate(outs, axis=1).astype(BF16)

    return pl.pallas_call(
        body, name="att_fwd", grid=(n,),
        in_specs=[q_cur, k_prv, k_cur, k_nxt, v_prv, v_cur, v_nxt,
                  pl.BlockSpec((None, ATT_HEADS, ATT_TB, ATT_KB), lambda i: (_att_variant(i, n), 0, 0, 0))],
        out_specs=cur,
        out_shape=jax.ShapeDtypeStruct((T, W), BF16),
        compiler_params=_cparams(("parallel",), VMEM_LIMIT),
    )(zb, zb, zb, zb, zb, zb, zb, biasv)


def _att_bwd(zb, y, do, biasv, comm=None):
    T = zb.shape[0]
    W = ATT_WIDTH
    n = T // ATT_TB
    n_rows = T // GRID_W
    cur = _att_specs(n, 0)[0]
    q_cur = _att_specs(n, 1)[0]
    k_cur, k_prv, k_nxt = _att_specs(n, 2)
    v_cur, v_prv, v_nxt = _att_specs(n, 3)
    done = pl.BlockSpec((ATT_TB, W), lambda i: (jnp.maximum(i - 1, 0), 0))
    bias_spec = pl.BlockSpec((None, ATT_HEADS, ATT_TB, ATT_KB), lambda i: (_att_variant(i, n), 0, 0, 0))

    def body(q_ref, y_ref, do_ref, kp_ref, kc_ref, kn_ref, vp_ref, vc_ref, vn_ref, b_ref,
             dq_ref, dk_ref, dv_ref, db_ref, acck_ref, accv_ref):
        i = pl.program_id(0)

        @pl.when(i == 0)
        def _():
            db_ref[...] = jnp.zeros_like(db_ref)
            acck_ref[...] = jnp.zeros_like(acck_ref)
            accv_ref[...] = jnp.zeros_like(accv_ref)

        @pl.when((i > 0) & (i < n))
        def _():
            slot = lax.rem(i + 1, 3)
            acck_ref[slot] = jnp.zeros((ATT_TB, W), F32)
            accv_ref[slot] = jnp.zeros((ATT_TB, W), F32)

        @pl.when(i < n)
        def _():
            dqs, dks, dvs = [], [], []
            for h in range(ATT_HEADS):
                hs = slice(h * ATT_HEAD_DIM, (h + 1) * ATT_HEAD_DIM)
                qh = q_ref[:, hs] * 0.125
                doh = do_ref[:, hs]
                kh = jnp.concatenate([kp_ref[:, hs], kc_ref[:, hs], kn_ref[:, hs]], axis=0)
                vh = jnp.concatenate([vp_ref[:, hs], vc_ref[:, hs], vn_ref[:, hs]], axis=0)
                delta = jnp.sum(doh.astype(F32) * y_ref[:, hs].astype(F32), axis=1, keepdims=True)
                e = _att_exp(qh, kh, b_ref[h])
                p = e * (1.0 / jnp.sum(e, axis=1, keepdims=True))
                dp = _dot_nt(doh, vh)
                ds = p * (dp - delta)
                db_ref[h] += ds
                dsb = ds.astype(BF16)
                dqs.append(_dot(dsb, kh) * 0.125)
                dks.append(_dot_tn(dsb, qh))
                dvs.append(_dot_tn(p.astype(BF16), doh))
            dq_ref[...] = jnp.concatenate(dqs, axis=1).astype(BF16)
            dk_all = jnp.concatenate(dks, axis=1)
            dv_all = jnp.concatenate(dvs, axis=1)
            for b in range(3):
                slot = lax.rem(i + 2 + b, 3)
                rows = slice(b * ATT_TB, (b + 1) * ATT_TB)
                acck_ref[slot] += dk_all[rows]
                accv_ref[slot] += dv_all[rows]

        slot = lax.rem(i + 2, 3)
        dk_ref[...] = acck_ref[slot].astype(BF16)
        dv_ref[...] = accv_ref[slot].astype(BF16)

    return _pallas(
        body, "att_bwd", (n + 1,),
        [q_cur, cur, cur, k_prv, k_cur, k_nxt, v_prv, v_cur, v_nxt, bias_spec],
        [cur, done, done, pl.BlockSpec((ATT_HEADS, ATT_TB, ATT_KB), lambda i: (0, 0, 0))],
        [jax.ShapeDtypeStruct((T, W), BF16)] * 3 + [jax.ShapeDtypeStruct((ATT_HEADS, ATT_TB, ATT_KB), F32)],
        [pltpu.VMEM((3, ATT_TB, W), F32), pltpu.VMEM((3, ATT_TB, W), F32)],
        ("arbitrary",), (zb, y, do, zb, zb, zb, zb, zb, zb, biasv), comm)


def _att_selectors():
    rsel = np.zeros((ATT_ROWS, 3 * ATT_ROWS, 2 * WIN_H - 1), np.float32)
    for a in range(ATT_ROWS):
        for b in range(3 * ATT_ROWS):
            rsel[a, b, b - a - ATT_ROWS + WIN_H - 1] = 1.0
    csel = np.zeros((GRID_W, GRID_W, 2 * WIN_W - 1), np.float32)
    for c in range(GRID_W):
        for x in range(GRID_W):
            csel[c, x, min(max(x - c, -(WIN_W - 1)), WIN_W - 1) + WIN_W - 1] = 1.0
    return rsel, csel


def _att_bias_table(rpb):
    rsel, csel = _att_selectors()
    hi = lax.Precision.HIGHEST
    t = jnp.einsum('hrd,abr->habd', rpb, rsel, precision=hi)
    t = jnp.einsum('habd,cxd->hacbx', t, csel, precision=hi)
    return t.reshape(ATT_HEADS, ATT_TB, ATT_KB)


def _att_bias_table_t(dtable):
    rsel, csel = _att_selectors()
    hi = lax.Precision.HIGHEST
    t = dtable.reshape(ATT_HEADS, ATT_ROWS, GRID_W, 3 * ATT_ROWS, GRID_W)
    t = jnp.einsum('hacbx,cxd->habd', t, csel, precision=hi)
    return jnp.einsum('habd,abr->hrd', t, rsel, precision=hi)


GELU_K = math.sqrt(2.0 / math.pi)
GELU_C = 0.044715
MERGE_TM = 256


def _gelu(x):
    return 0.5 * x * (1.0 + jnp.tanh(GELU_K * (x + GELU_C * x * x * x)))


def _gelu_grad(x):
    t = jnp.tanh(GELU_K * (x + GELU_C * x * x * x))
    return 0.5 * (1.0 + t) + 0.5 * x * (1.0 - t * t) * GELU_K * (1.0 + 3.0 * GELU_C * x * x)


def _merge_forward(ypre, zs, gs, ga, ya, ssm_d, w_glu, b_glu, w_bs, w_ba):
    zs, gs, ga = zs.astype(F32), gs.astype(F32), ga.astype(F32)
    ys = ypre + ssm_d * zs
    yg = _gelu(ys)
    sg = jax.nn.sigmoid(_dot(yg.astype(BF16), w_glu) + b_glu)
    y2 = yg * sg
    bs = _dot(y2.astype(BF16), w_bs)
    ba = _dot(ya, w_ba)
    s1 = jax.nn.sigmoid(gs)
    s2 = jax.nn.sigmoid(ga)
    merged = s1 * bs + s2 * ba
    return ys, yg, sg, y2, bs, ba, s1, s2, merged


def _merge_in_specs(D, W, tm):
    tok = lambda w, c: pl.BlockSpec((tm, w), lambda i: (i, c))
    full = lambda r, c: pl.BlockSpec((r, c), lambda i: (0, 0))
    z_specs = [tok(W, 0), tok(D, 4 * W // D), tok(D, 4 * W // D + 1)]
    w_specs = [full(1, W), full(W, W), full(1, W), full(W, D), full(W, D), full(D, D)]
    return tok, z_specs, w_specs


def _merge_fwd(ypre, z, ya, h1, ssm_d, w_glu, b_glu, w_bs, w_ba, w_out):
    T, D = h1.shape
    W = ypre.shape[1]
    tm = min(T, 2 * MERGE_TM)
    tok, z_specs, w_specs = _merge_in_specs(D, W, tm)

    def body(ypre_ref, zs_ref, gs_ref, ga_ref, ya_ref, h1_ref, d_ref, wglu_ref, bglu_ref, wbs_ref, wba_ref, wout_ref,
             h2_ref):
        merged = _merge_forward(ypre_ref[...], zs_ref[...], gs_ref[...], ga_ref[...], ya_ref[...], d_ref[...],
                                wglu_ref[...], bglu_ref[...], wbs_ref[...], wba_ref[...])[-1]
        h2_ref[...] = h1_ref[...] + _dot(merged.astype(BF16), wout_ref[...])

    return pl.pallas_call(
        body, name="merge_fwd", grid=(T // tm,),
        in_specs=[tok(W, 0)] + z_specs + [tok(W, 0), tok(D, 0)] + w_specs,
        out_specs=tok(D, 0),
        out_shape=jax.ShapeDtypeStruct((T, D), F32),
        compiler_params=_cparams(("parallel",), VMEM_LIMIT),
    )(ypre, z, z, z, ya, h1, ssm_d, w_glu, b_glu, w_bs, w_ba, w_out)


def _merge_bwd(dh2, ypre, z, ya, ssm_d, w_glu, b_glu, w_bs, w_ba, w_out):
    T, D = dh2.shape
    W = ypre.shape[1]
    tm = min(T, MERGE_TM)
    tok, z_specs, w_specs = _merge_in_specs(D, W, tm)

    def body(dh2_ref, ypre_ref, zs_ref, gs_ref, ga_ref, ya_ref, d_ref, wglu_ref, bglu_ref, wbs_ref, wba_ref, wout_ref,
             dypre_ref, dzs_ref, dgs_ref, dga_ref, dya_ref, dd_ref, dwglu_ref, dbglu_ref, dwbs_ref, dwba_ref, dwout_ref):
        @pl.when(pl.program_id(0) == 0)
        def _():
            for r in (dd_ref, dwglu_ref, dbglu_ref, dwbs_ref, dwba_ref, dwout_ref):
                r[...] = jnp.zeros_like(r)

        zs = zs_ref[...].astype(F32)
        ya = ya_ref[...]
        ys, yg, sg, y2, bs, ba, s1, s2, merged = _merge_forward(
            ypre_ref[...], zs, gs_ref[...], ga_ref[...], ya, d_ref[...],
            wglu_ref[...], bglu_ref[...], wbs_ref[...], wba_ref[...])
        dh2b = dh2_ref[...].astype(BF16)
        dmerged = _dot_nt(dh2b, wout_ref[...])
        dwout_ref[...] += _dot_tn(merged.astype(BF16), dh2b)
        dbs = (dmerged * s1).astype(BF16)
        dba = (dmerged * s2).astype(BF16)
        dgs_ref[...] = (dmerged * bs * s1 * (1.0 - s1)).astype(BF16)
        dga_ref[...] = (dmerged * ba * s2 * (1.0 - s2)).astype(BF16)
        dwbs_ref[...] += _dot_tn(y2.astype(BF16), dbs)
        dwba_ref[...] += _dot_tn(ya, dba)
        dya_ref[...] = _dot_nt(dba, wba_ref[...]).astype(BF16)
        dy2 = _dot_nt(dbs, wbs_ref[...])
        dvv = dy2 * yg * sg * (1.0 - sg)
        dvvb = dvv.astype(BF16)
        dyg = dy2 * sg + _dot_nt(dvvb, wglu_ref[...])
        dwglu_ref[...] += _dot_tn(yg.astype(BF16), dvvb)
        dbglu_ref[...] += _col_sum(dvv)
        dys = dyg * _gelu_grad(ys)
        dd_ref[...] += _col_sum(dys * zs)
        dzs_ref[...] = dys * d_ref[...]
        dypre_ref[...] = dys.astype(BF16)

    f32 = lambda *s: jax.ShapeDtypeStruct(s, F32)
    b16 = lambda *s: jax.ShapeDtypeStruct(s, BF16)
    return pl.pallas_call(
        body, name="merge_bwd", grid=(T // tm,),
        in_specs=[tok(D, 0), tok(W, 0)] + z_specs + [tok(W, 0)] + w_specs,
        out_specs=[tok(W, 0), tok(W, 0), tok(D, 0), tok(D, 0), tok(W, 0)] + w_specs,
        out_shape=[b16(T, W), f32(T, W), b16(T, D), b16(T, D), b16(T, W),
                   f32(1, W), f32(W, W), f32(1, W), f32(W, D), f32(W, D), f32(D, D)],
        compiler_params=_cparams(("arbitrary",), VMEM_LIMIT),
    )(dh2, ypre, z, z, z, ya, ssm_d, w_glu, b_glu, w_bs, w_ba, w_out)


def _cast_shards(weights):
    def body(*refs):
        n = len(refs) // 2
        for src, dst in zip(refs[:n], refs[n:]):
            dst[...] = src[0].astype(BF16)

    return pl.pallas_call(
        body, name="cast_shards",
        out_shape=[jax.ShapeDtypeStruct(w.shape[1:], BF16) for w in weights],
        compiler_params=_cparams(None, VMEM_LIMIT))(*weights)


def _gather_two_level(shards, name):
    n = len(shards)

    def body(*refs):
        x_refs, out_refs = refs[:n], refs[n:2 * n]
        send_sems, recv_sems, local_sems = refs[2 * n:]
        x, y, c = _my_place()
        me, sibling = (x, y, c), (x, y, 1 - c)
        chips = [(1 - x, y), (x, 1 - y), (1 - x, 1 - y)]

        def copy(a, k, block, to, own=False):
            slot = out_refs[a].at[_flat(*block)]
            return pltpu.make_async_remote_copy(
                src_ref=x_refs[a] if own else slot, dst_ref=slot,
                send_sem=send_sems.at[7 * a + k], recv_sem=recv_sems.at[7 * a + k],
                device_id=to, device_id_type=MESH_ID)

        sent, local = [], []
        for a in range(n):
            local.append(pltpu.make_async_copy(x_refs[a], out_refs[a].at[_flat(*me)], local_sems.at[a]))
            local[-1].start()
            sent.append(copy(a, 0, me, sibling, own=True))
            sent += [copy(a, 1 + j, me, (*chip, c), own=True) for j, chip in enumerate(chips)]
        for cp in sent:
            cp.start()
        for a in range(n):
            for j, chip in enumerate(chips):
                copy(a, 1 + j, (*chip, c), me).wait_recv()
                sent.append(copy(a, 4 + j, (*chip, c), sibling))
                sent[-1].start()
        for a in range(n):
            copy(a, 0, sibling, me).wait_recv()
            for j, chip in enumerate(chips):
                copy(a, 4 + j, (*chip, 1 - c), me).wait_recv()
        for cp in sent:
            cp.wait_send()
        for cp in local:
            cp.wait()

    return pl.pallas_call(
        body, name=name, in_specs=[_HBM] * n, out_specs=[_HBM] * n,
        out_shape=[jax.ShapeDtypeStruct((N_DEV,) + s.shape, s.dtype) for s in shards],
        scratch_shapes=[pltpu.SemaphoreType.DMA((7 * n,)), pltpu.SemaphoreType.DMA((7 * n,)),
                        pltpu.SemaphoreType.DMA((n,))],
    )(*shards)


PACK_COLS = 1024
BIG = (("ffn1_w_gate", 1), ("ffn1_w_up", 1), ("ffn1_w_down", 0), ("w_in", 1), ("ssm_w_glu", 0),
       ("w_branch_ssm", 1), ("w_branch_att", 1), ("w_out", 0),
       ("ffn2_w_gate", 1), ("ffn2_w_up", 1), ("ffn2_w_down", 0))
BIG_AXIS = dict(BIG)
TRANSPOSED = ("ffn1_w_gate", "ffn1_w_up", "ffn2_w_gate", "ffn2_w_up")
SSM_DIR = ("ssm_a_re", "ssm_a_im", "ssm_log_dt", "ssm_b_re", "ssm_b_im", "ssm_c_re", "ssm_c_im")
SMALL_EARLY = (("mix_norm",) + tuple(n + "_fwd" for n in SSM_DIR) + tuple(n + "_bwd" for n in SSM_DIR)
               + ("ssm_d", "ssm_b_glu", "att_rpb", "ffn2_norm", "final_norm"))
SMALL_LATE = ("ffn1_norm",)
WEIGHTS = ("ffn1_norm", "ffn1_w_gate", "ffn1_w_up", "ffn1_w_down", "mix_norm", "w_in") \
    + tuple(n + "_fwd" for n in SSM_DIR) + tuple(n + "_bwd" for n in SSM_DIR) \
    + ("ssm_d", "ssm_w_glu", "ssm_b_glu", "att_rpb", "w_branch_ssm", "w_branch_att", "w_out",
       "ffn2_norm", "ffn2_w_gate", "ffn2_w_up", "ffn2_w_down", "final_norm")


def _pad_rows(a, mult):
    pad = (-a.shape[-2]) % mult
    if pad:
        a = jnp.concatenate([a, jnp.zeros(a.shape[:-2] + (pad, a.shape[-1]), a.dtype)], axis=-2)
    return a


def _pack(arrays, row_mult):
    flat = jnp.concatenate([a.reshape(-1) for a in arrays])
    pad = (-flat.shape[0]) % PACK_COLS
    if pad:
        flat = jnp.concatenate([flat, jnp.zeros((pad,), flat.dtype)])
    return _pad_rows(flat.reshape(-1, PACK_COLS), row_mult)


def _unpack(slab, shapes):
    flat = slab.reshape(-1)
    out, at = [], 0
    for s in shapes:
        n = int(np.prod(s))
        out.append(flat[at:at + n].reshape(s))
        at += n
    return out


def _split_for_devices(g, axis):
    r, c = g.shape
    if axis == 1:
        return g.reshape(r, N_DEV, c // N_DEV).transpose(1, 0, 2).astype(BF16)
    return g.reshape(N_DEV, r // N_DEV, c).astype(BF16)


def _join_shards(gathered, axis):
    _, r, c = gathered.shape
    if axis == 1:
        return gathered.transpose(1, 0, 2).reshape(r, N_DEV * c)
    return gathered.reshape(N_DEV * r, c)


def _s5_direction_inputs(p, sfx, chain_len):
    bt_re = p["ssm_b_re" + sfx][0].transpose(0, 2, 1)
    bt_im = p["ssm_b_im" + sfx][0].transpose(0, 2, 1)
    raw = (p["ssm_a_re" + sfx][0], p["ssm_a_im" + sfx][0], p["ssm_log_dt" + sfx][0][:, None], bt_re, bt_im)
    lr, li, sr, si, bbr, bbi = _disc_fwd(*raw, chain_len, "s5_disc" + sfx)
    lam = jnp.stack([_s5_pack_lam(t) for t in (lr, li, sr, si)], axis=2)
    mats = (_s5_pack_b(bbr), _s5_pack_b(bbi), lam,
            _s5_pack_c(p["ssm_c_re" + sfx][0]), _s5_pack_c(-p["ssm_c_im" + sfx][0]))
    return raw, mats


def kernel(x, ffn1_norm, ffn1_w_gate, ffn1_w_up, ffn1_w_down, mix_norm, w_in, ssm_a_re_fwd, ssm_a_im_fwd, ssm_log_dt_fwd, ssm_b_re_fwd, ssm_b_im_fwd, ssm_c_re_fwd, ssm_c_im_fwd, ssm_a_re_bwd, ssm_a_im_bwd, ssm_log_dt_bwd, ssm_b_re_bwd, ssm_b_im_bwd, ssm_c_re_bwd, ssm_c_im_bwd, ssm_d, ssm_w_glu, ssm_b_glu, att_rpb, w_branch_ssm, w_branch_att, w_out, ffn2_norm, ffn2_w_gate, ffn2_w_up, ffn2_w_down, final_norm, loss_target, m_ffn1_norm, m_ffn1_w_gate, m_ffn1_w_up, m_ffn1_w_down, m_mix_norm, m_w_in, m_ssm_a_re_fwd, m_ssm_a_im_fwd, m_ssm_log_dt_fwd, m_ssm_b_re_fwd, m_ssm_b_im_fwd, m_ssm_c_re_fwd, m_ssm_c_im_fwd, m_ssm_a_re_bwd, m_ssm_a_im_bwd, m_ssm_log_dt_bwd, m_ssm_b_re_bwd, m_ssm_b_im_bwd, m_ssm_c_re_bwd, m_ssm_c_im_bwd, m_ssm_d, m_ssm_w_glu, m_ssm_b_glu, m_att_rpb, m_w_branch_ssm, m_w_branch_att, m_w_out, m_ffn2_norm, m_ffn2_w_gate, m_ffn2_w_up, m_ffn2_w_down, m_final_norm, v_ffn1_norm, v_ffn1_w_gate, v_ffn1_w_up, v_ffn1_w_down, v_mix_norm, v_w_in, v_ssm_a_re_fwd, v_ssm_a_im_fwd, v_ssm_log_dt_fwd, v_ssm_b_re_fwd, v_ssm_b_im_fwd, v_ssm_c_re_fwd, v_ssm_c_im_fwd, v_ssm_a_re_bwd, v_ssm_a_im_bwd, v_ssm_log_dt_bwd, v_ssm_b_re_bwd, v_ssm_b_im_bwd, v_ssm_c_re_bwd, v_ssm_c_im_bwd, v_ssm_d, v_ssm_w_glu, v_ssm_b_glu, v_att_rpb, v_w_branch_ssm, v_w_branch_att, v_w_out, v_ffn2_norm, v_ffn2_w_gate, v_ffn2_w_up, v_ffn2_w_down, v_final_norm):
    p = dict(locals())
    x = p["x"][0]
    target = p["loss_target"][0]
    T, D = x.shape

    stored = lambda a, n: jnp.swapaxes(a, -1, -2) if n in TRANSPOSED else a
    cut_axis = lambda n: 0 if n in TRANSPOSED else BIG_AXIS[n]
    shard = dict(zip([n for n, _ in BIG], _cast_shards([stored(p[n], n) for n, _ in BIG])))
    ffn1_w = ("ffn1_w_gate", "ffn1_w_up", "ffn1_w_down")
    mix_w = ("w_in", "ssm_w_glu", "w_branch_ssm", "w_branch_att", "w_out")
    ffn2_w = ("ffn2_w_gate", "ffn2_w_up", "ffn2_w_down")
    gathered = dict(zip(ffn1_w, _gather_two_level([shard[n] for n in ffn1_w], "gather_ffn1")))
    full = lambda n: _join_shards(gathered[n], cut_axis(n))

    h0 = x
    wg1, wu1, wd1 = [full(n) for n in ffn1_w]
    (h1, xn1, g1, u1), got = _ffn_fwd(h0, p["ffn1_norm"], wg1, wu1, wd1, "ffn1_fwd",
                                      _Comm("gather", [shard[n] for n in mix_w]))
    gathered.update(zip(mix_w, got))
    zb, un = _mixin_fwd(h1, p["mix_norm"], gathered["w_in"])
    W = SSM_WIDTH
    zp = _permute_rows(zb[:, :W])
    chain_len = T // SCAN_LANES // S5_NQ
    raw_f, mats_f = _s5_direction_inputs(p, "_fwd", chain_len)
    raw_b, mats_b = _s5_direction_inputs(p, "_bwd", chain_len)
    bre, bim, lam, cre, cimn = [jnp.stack([f, b]) for f, b in zip(mats_f, mats_b)]
    bre, bim, cre, cimn = [t.astype(BF16) for t in (bre, bim, cre, cimn)]
    (yp,), got = _s5_fwd(zp, bre, bim, lam, cre, cimn, _Comm("gather", [shard[n] for n in ffn2_w]))
    gathered.update(zip(ffn2_w, got))
    ypre = _unpermute_rows(yp)
    table = _att_masked_tables(_att_bias_table(p["att_rpb"][0]), T // GRID_W)
    ya = _att_fwd(zb, table)
    tail_w = (p["ssm_d"], full("ssm_w_glu"), p["ssm_b_glu"], full("w_branch_ssm"), full("w_branch_att"), full("w_out"))
    h2 = _merge_fwd(ypre, zb, ya, h1, *tail_w)
    wg2, wu2, wd2 = [full(n) for n in ffn2_w]
    (h3, xn2, g2, u2), _ = _ffn_fwd(h2, p["ffn2_norm"], wg2, wu2, wd2, "ffn2_fwd")
    loss_part, dh3, d_final = _loss_head(h3, p["final_norm"][None], target)

    grads = {"final_norm": d_final[0]}
    to_send = lambda names: _Comm("exchange", [_split_for_devices(grads[n], cut_axis(n)) for n in names])
    parts = {}
    (dh2, grads["ffn2_norm"], do2, a2, dg2, du2), _ = _ffn_bwd(
        dh3, h2, p["ffn2_norm"], g2, u2, wg2, wu2, wd2, "ffn2_bwd")
    grads["ffn2_w_gate"] = _xty(dg2, xn2, "ffn2_dw_gate")
    grads["ffn2_w_up"] = _xty(du2, xn2, "ffn2_dw_up")
    grads["ffn2_w_down"] = _xty(a2, do2, "ffn2_dw_down")
    (dypre, dzs_skip, dgs, dga, dya, grads["ssm_d"], grads["ssm_w_glu"], grads["ssm_b_glu"],
     grads["w_branch_ssm"], grads["w_branch_att"], grads["w_out"]) = _merge_bwd(dh2, ypre, zb, ya, *tail_w)
    (dq, dk, dv, dtable), got = _att_bwd(zb, ya, dya, table, to_send(ffn2_w))
    parts.update(zip(ffn2_w, got))
    grads["att_rpb"] = _att_bias_table_t(dtable)
    dyp = _permute_rows(dypre)
    tail_names = ("ssm_w_glu", "w_branch_ssm", "w_branch_att", "w_out")
    (dzp, dbre, dbim, dlam, dcre, dcimn), got = _s5_bwd(zp, dyp, bre, bim, lam, cre, cimn, to_send(tail_names))
    parts.update(zip(tail_names, got))
    G, P = SSM_GROUPS, SSM_STATE
    for d, (sfx, raw) in enumerate((("_fwd", raw_f), ("_bwd", raw_b))):
        da_re, da_im, dldt, dbt_re, dbt_im = _disc_bwd(
            *raw, dlam[d, :, :, 0, :].reshape(G, P), dlam[d, :, :, 1, :].reshape(G, P),
            _s5_unpack_b(dbre[d]), _s5_unpack_b(dbim[d]), "s5_disc_grad" + sfx)
        grads["ssm_a_re" + sfx] = da_re
        grads["ssm_a_im" + sfx] = da_im
        grads["ssm_log_dt" + sfx] = dldt[:, 0]
        grads["ssm_b_re" + sfx] = dbt_re.transpose(0, 2, 1)
        grads["ssm_b_im" + sfx] = dbt_im.transpose(0, 2, 1)
        grads["ssm_c_re" + sfx] = _s5_unpack_c(dcre[d])
        grads["ssm_c_im" + sfx] = -_s5_unpack_c(dcimn[d])
    dzs = _unpermute_rows(dzp) + dzs_skip
    dz = jnp.concatenate([dzs.astype(BF16), dq, dk, dv, dgs, dga], axis=1)
    dh1, grads["mix_norm"] = _mixin_bwd(dz, dh2, h1, p["mix_norm"], gathered["w_in"])
    grads["w_in"] = _xty(un, dz, "dw_in", col_shards=N_DEV)
    small_t = lambda a, n: jnp.swapaxes(a, -1, -2) if n.startswith("ssm_b_") else a
    pack_small = lambda names, src, pre: _pack([small_t(src[pre + n], n).astype(F32) for n in names], 8)
    early = _Comm(["exchange", "gather"],
                  [grads["w_in"], pack_small(SMALL_EARLY, grads, "")])
    (dh0, grads["ffn1_norm"], do1, a1, dg1, du1), (parts["w_in"], got_early) = _ffn_bwd(
        dh1, h0, p["ffn1_norm"], g1, u1, wg1, wu1, wd1, "ffn1_bwd", early)
    grads["ffn1_w_down"] = _xty(a1, do1, "ffn1_dw_down")
    grads["ffn1_w_gate"], (parts["ffn1_w_down"],) = _xty(dg1, xn1, "ffn1_dw_gate", to_send(("ffn1_w_down",)))
    grads["ffn1_w_up"], (parts["ffn1_w_gate"],) = _xty(du1, xn1, "ffn1_dw_up", to_send(("ffn1_w_gate",)))
    last = _Comm(["exchange", "gather"],
                 [_split_for_devices(grads["ffn1_w_up"], 0), pack_small(SMALL_LATE, grads, "")])
    parts["ffn1_w_up"], got_late = _comm_call(last, "exchange_last")
    got_small = jnp.concatenate([got_early, got_late], axis=1)

    results = {}
    for n, _ in BIG:
        outs = _adamw(parts[n], *[stored(p[pre + n][0], n) for pre in ("", "m_", "v_")], "adamw_" + n)
        results[n] = [stored(o, n)[None] for o in outs]
    early_rows = got_early.shape[1]
    slab = lambda pre: jnp.concatenate([pack_small(SMALL_EARLY, p, pre), pack_small(SMALL_LATE, p, pre)], axis=0)
    small_out = _adamw(got_small, slab(""), slab("m_"), slab("v_"), "adamw_small")
    for names, rows in ((SMALL_EARLY, slice(0, early_rows)), (SMALL_LATE, slice(early_rows, None))):
        shapes = [small_t(p[n], n).shape for n in names]
        for n, vals in zip(names, zip(*[_unpack(out[rows], shapes) for out in small_out])):
            results[n] = [small_t(val, n) for val in vals]

    loss = lax.psum(loss_part[0, 0], ("x", "y", "c"))
    out = [loss, dh0[None]]
    for kind in range(4):
        out += [results[n][kind] for n in WEIGHTS]
    return tuple(out)
```
